```python
import math
import jax, jax.numpy as jnp
from jax import lax
import numpy as np

D_MODEL = 1024
BATCH = 16
SEQ = 2048
DEPTH = 2

EPS = 1e-6
POOL_WINDOWS = (2, 4, 8, 16)
POOL_GROUPS = 4
POOL_DH = D_MODEL // 8
POOL_WIDTH = POOL_GROUPS * POOL_DH
CONV_WIDTH = D_MODEL // 2
CONV_K = 3
AB_IN = POOL_WIDTH + 3 * CONV_WIDTH
AB_OUT = POOL_WIDTH + CONV_WIDTH
SGU_CHUNK = 128
SGU_GROUPS = 4
SGU_DH = D_MODEL // 8
SGU_WIDTH = SGU_GROUPS * SGU_DH
SB_HEADS = 8
SB_DH = 64
SB_WIDTH = SB_HEADS * SB_DH
SB_BLOCK = 128
CD_IN = 2 * SGU_WIDTH + 3 * SB_WIDTH
CD_OUT = SGU_WIDTH + SB_WIDTH
D_FF = 4 * D_MODEL
N_AB = (DEPTH + 1) // 2
N_CD = DEPTH // 2

kernel_name = 'hybrid_pool_conv_sgu_stickbreak_trunk'


def rmsnorm(x, g):
    xf = x.astype(jnp.float32)
    y = xf * lax.rsqrt(jnp.mean(xf * xf, axis=-1, keepdims=True) + EPS)
    return (y * g.astype(jnp.float32)).astype(x.dtype)


def layernorm(x, g, b):
    xf = x.astype(jnp.float32)
    mu = jnp.mean(xf, axis=-1, keepdims=True)
    xc = xf - mu
    y = xc * lax.rsqrt(jnp.mean(xc * xc, axis=-1, keepdims=True) + EPS)
    return (y * g.astype(jnp.float32) + b.astype(jnp.float32)).astype(x.dtype)


def pool_mixer(a, w, scale):
    T = a.shape[1]
    af = a.astype(jnp.float32)
    cs = jnp.pad(jnp.cumsum(af, axis=1), ((0, 0), (1, 0), (0, 0), (0, 0)))
    win = jnp.array(POOL_WINDOWS, dtype=jnp.int32)
    pos = jnp.arange(T, dtype=jnp.int32)
    start = jnp.maximum(pos[:, None] + 1 - win[None, :], 0)
    count = jnp.minimum(pos[:, None] + 1, win[None, :]).astype(jnp.float32)
    grp = jnp.arange(POOL_GROUPS, dtype=jnp.int32)
    window_sum = cs[:, 1:] - cs[:, start, grp[None, :], :]
    pooled = (window_sum / count[None, :, :, None] - af).astype(a.dtype)
    mixed = jnp.einsum('btgc,gcd->btgd', pooled, w)
    return mixed * scale


def short_conv(h, w, b):
    T = h.shape[1]
    hp = jnp.pad(h, ((0, 0), (CONV_K - 1, 0), (0, 0)))
    y = hp[:, 0:T] * w[0]
    for k in range(1, CONV_K):
        y = y + hp[:, k:k + T] * w[k]
    return y + b


def spatial_gating(u, v, g, beta, w_s, b_s):
    B, T, _ = v.shape
    v = layernorm(v, g, beta)
    vc = v.reshape(B, T // SGU_CHUNK, SGU_CHUNK, SGU_GROUPS, SGU_DH)
    causal = jnp.tril(jnp.ones((SGU_CHUNK, SGU_CHUNK), dtype=bool))
    w = jnp.where(causal[None], w_s, 0)
    s = jnp.einsum('gts,bnsgc->bntgc', w, vc) + b_s.T[:, :, None]
    return u * s.reshape(B, T, SGU_WIDTH)


def stick_breaking_attention(q, k, v):
    B, T, H, dh = q.shape
    q = q.transpose(0, 2, 1, 3)
    k = k.transpose(0, 2, 1, 3)
    v = v.transpose(0, 2, 1, 3)
    scale = 1.0 / math.sqrt(dh)
    outs = []
    for i in range(T // SB_BLOCK):
        q0 = i * SB_BLOCK
        kend = q0 + SB_BLOCK
        qb = q[:, :, q0:kend]
        kb = k[:, :, :kend]
        vb = v[:, :, :kend]
        z = jnp.einsum('bhqd,bhkd->bhqk', qb, kb,
                       preferred_element_type=jnp.float32) * scale
        qpos = q0 + jnp.arange(SB_BLOCK, dtype=jnp.int32)
        kpos = jnp.arange(kend, dtype=jnp.int32)
        mask = kpos[None, :] < qpos[:, None]
        log_keep = jnp.where(mask, jax.nn.log_sigmoid(-z), 0.0)
        suffix = lax.cumsum(log_keep, axis=3, reverse=True) - log_keep
        weights = jnp.where(mask, jnp.exp(jax.nn.log_sigmoid(z) + suffix), 0.0)
        outs.append(jnp.einsum('bhqk,bhkd->bhqd', weights.astype(vb.dtype), vb))
    o = jnp.concatenate(outs, axis=2)
    return o.transpose(0, 2, 1, 3).reshape(B, T, H * dh)


def _fwd_setup_inputs(seed: int = 0) -> dict:
    key = jax.random.key(seed)
    ks = jax.random.split(key, 20)
    nrm = jax.random.normal
    f32 = jnp.float32
    res_scale = 1.0 / math.sqrt(2 * DEPTH)
    return {
        'x': nrm(ks[0], (BATCH, SEQ, D_MODEL), f32),
        'mix_norm_g': 1.0 + 0.01 * nrm(ks[1], (DEPTH, D_MODEL), f32),
        'mlp_norm_g': 1.0 + 0.01 * nrm(ks[2], (DEPTH, D_MODEL), f32),
        'ab_w_in': nrm(ks[3], (N_AB, D_MODEL, AB_IN), f32) * D_MODEL ** -0.5,
        'pool_w': nrm(ks[4], (N_AB, POOL_GROUPS, POOL_DH, POOL_DH), f32) * POOL_DH ** -0.5,
        'pool_scale': 1.0 + 0.02 * nrm(ks[5], (N_AB, POOL_GROUPS, POOL_DH), f32),
        'conv_w': nrm(ks[6], (N_AB, CONV_K, CONV_WIDTH), f32) * CONV_K ** -0.5,
        'conv_b': 0.01 * nrm(ks[7], (N_AB, CONV_WIDTH), f32),
        'ab_w_out': nrm(ks[8], (N_AB, AB_OUT, D_MODEL), f32) * AB_OUT ** -0.5 * res_scale,
        'cd_w_in': nrm(ks[9], (N_CD, D_MODEL, CD_IN), f32) * D_MODEL ** -0.5,
        'sgu_norm_g': 1.0 + 0.01 * nrm(ks[10], (N_CD, SGU_WIDTH), f32),
        'sgu_norm_b': 0.01 * nrm(ks[11], (N_CD, SGU_WIDTH), f32),
        'sgu_w': nrm(ks[12], (N_CD, SGU_GROUPS, SGU_CHUNK, SGU_CHUNK), f32) * SGU_CHUNK ** -0.5,
        'sgu_b': 1.0 + 0.01 * nrm(ks[13], (N_CD, SGU_GROUPS, SGU_CHUNK), f32),
        'cd_w_out': nrm(ks[14], (N_CD, CD_OUT, D_MODEL), f32) * CD_OUT ** -0.5 * res_scale,
        'mlp_w1': nrm(ks[15], (DEPTH, D_MODEL, D_FF), f32) * D_MODEL ** -0.5,
        'mlp_w2': nrm(ks[16], (DEPTH, D_FF, D_MODEL), f32) * D_FF ** -0.5 * res_scale,
        'final_norm_g': 1.0 + 0.01 * nrm(ks[17], (D_MODEL,), f32),
    }


def _fwd_reference(x, mix_norm_g, mlp_norm_g, ab_w_in, pool_w, pool_scale, conv_w, conv_b,
              ab_w_out, cd_w_in, sgu_norm_g, sgu_norm_b, sgu_w, sgu_b, cd_w_out,
              mlp_w1, mlp_w2, final_norm_g):
    B, T, _ = x.shape
    h = x
    for layer in range(DEPTH):
        xn = rmsnorm(h, mix_norm_g[layer])
        if layer % 2 == 0:
            i = layer // 2
            p = xn @ ab_w_in[i]
            a, xb, gate_b, gate_c = jnp.split(
                p, [POOL_WIDTH, POOL_WIDTH + CONV_WIDTH, POOL_WIDTH + 2 * CONV_WIDTH], axis=-1)
            a_out = pool_mixer(a.reshape(B, T, POOL_GROUPS, POOL_DH),
                               pool_w[i], pool_scale[i]).reshape(B, T, POOL_WIDTH)
            b_out = gate_b * short_conv(gate_c * xb, conv_w[i], conv_b[i])
            mix = jnp.concatenate([a_out, b_out], axis=-1) @ ab_w_out[i]
        else:
            i = layer // 2
            p = xn @ cd_w_in[i]
            uv = jax.nn.gelu(p[..., :2 * SGU_WIDTH], approximate=False)
            u, v = jnp.split(uv, 2, axis=-1)
            c_out = spatial_gating(u, v, sgu_norm_g[i], sgu_norm_b[i], sgu_w[i], sgu_b[i])
            qkv = p[..., 2 * SGU_WIDTH:].reshape(B, T, 3, SB_HEADS, SB_DH)
            d_out = stick_breaking_attention(qkv[:, :, 0], qkv[:, :, 1], qkv[:, :, 2])
            mix = jnp.concatenate([c_out, d_out], axis=-1) @ cd_w_out[i]
        h = h + mix
        hn = rmsnorm(h, mlp_norm_g[layer])
        h = h + jnp.square(jax.nn.relu(hn @ mlp_w1[layer])) @ mlp_w2[layer]
    return rmsnorm(h, final_norm_g)


import jax as _jax
import jax.numpy as _jnp

TWIN_FORMAT = 'train_step'
FWD_PARAMS = ['x', 'mix_norm_g', 'mlp_norm_g', 'ab_w_in', 'pool_w', 'pool_scale', 'conv_w', 'conv_b', 'ab_w_out', 'cd_w_in', 'sgu_norm_g', 'sgu_norm_b', 'sgu_w', 'sgu_b', 'cd_w_out', 'mlp_w1', 'mlp_w2', 'final_norm_g']
TWIN_WEIGHTS = ['mix_norm_g', 'mlp_norm_g', 'ab_w_in', 'pool_w', 'pool_scale', 'conv_w', 'conv_b', 'ab_w_out', 'cd_w_in', 'sgu_norm_g', 'sgu_norm_b', 'sgu_w', 'sgu_b', 'cd_w_out', 'mlp_w1', 'mlp_w2', 'final_norm_g']
TWIN_DIFF_INPUT = 'x'
TWIN_INPUTS = ['x', 'mix_norm_g', 'mlp_norm_g', 'ab_w_in', 'pool_w', 'pool_scale', 'conv_w', 'conv_b', 'ab_w_out', 'cd_w_in', 'sgu_norm_g', 'sgu_norm_b', 'sgu_w', 'sgu_b', 'cd_w_out', 'mlp_w1', 'mlp_w2', 'final_norm_g', 'loss_target', 'm_mix_norm_g', 'm_mlp_norm_g', 'm_ab_w_in', 'm_pool_w', 'm_pool_scale', 'm_conv_w', 'm_conv_b', 'm_ab_w_out', 'm_cd_w_in', 'm_sgu_norm_g', 'm_sgu_norm_b', 'm_sgu_w', 'm_sgu_b', 'm_cd_w_out', 'm_mlp_w1', 'm_mlp_w2', 'm_final_norm_g', 'v_mix_norm_g', 'v_mlp_norm_g', 'v_ab_w_in', 'v_pool_w', 'v_pool_scale', 'v_conv_w', 'v_conv_b', 'v_ab_w_out', 'v_cd_w_in', 'v_sgu_norm_g', 'v_sgu_norm_b', 'v_sgu_w', 'v_sgu_b', 'v_cd_w_out', 'v_mlp_w1', 'v_mlp_w2', 'v_final_norm_g']
TWIN_OUTPUTS = ['loss', 'grad_x', 'grad_mix_norm_g', 'grad_mlp_norm_g', 'grad_ab_w_in', 'grad_pool_w', 'grad_pool_scale', 'grad_conv_w', 'grad_conv_b', 'grad_ab_w_out', 'grad_cd_w_in', 'grad_sgu_norm_g', 'grad_sgu_norm_b', 'grad_sgu_w', 'grad_sgu_b', 'grad_cd_w_out', 'grad_mlp_w1', 'grad_mlp_w2', 'grad_final_norm_g', 'delta_mix_norm_g', 'delta_mlp_norm_g', 'delta_ab_w_in', 'delta_pool_w', 'delta_pool_scale', 'delta_conv_w', 'delta_conv_b', 'delta_ab_w_out', 'delta_cd_w_in', 'delta_sgu_norm_g', 'delta_sgu_norm_b', 'delta_sgu_w', 'delta_sgu_b', 'delta_cd_w_out', 'delta_mlp_w1', 'delta_mlp_w2', 'delta_final_norm_g', 'new_m_mix_norm_g', 'new_m_mlp_norm_g', 'new_m_ab_w_in', 'new_m_pool_w', 'new_m_pool_scale', 'new_m_conv_w', 'new_m_conv_b', 'new_m_ab_w_out', 'new_m_cd_w_in', 'new_m_sgu_norm_g', 'new_m_sgu_norm_b', 'new_m_sgu_w', 'new_m_sgu_b', 'new_m_cd_w_out', 'new_m_mlp_w1', 'new_m_mlp_w2', 'new_m_final_norm_g', 'new_v_mix_norm_g', 'new_v_mlp_norm_g', 'new_v_ab_w_in', 'new_v_pool_w', 'new_v_pool_scale', 'new_v_conv_w', 'new_v_conv_b', 'new_v_ab_w_out', 'new_v_cd_w_in', 'new_v_sgu_norm_g', 'new_v_sgu_norm_b', 'new_v_sgu_w', 'new_v_sgu_b', 'new_v_cd_w_out', 'new_v_mlp_w1', 'new_v_mlp_w2', 'new_v_final_norm_g']
TWIN_LEAF_KINDS = {'loss': 'loss', 'grad_x': 'grad_x', 'grad_mix_norm_g': 'grad_w', 'grad_mlp_norm_g': 'grad_w', 'grad_ab_w_in': 'grad_w', 'grad_pool_w': 'grad_w', 'grad_pool_scale': 'grad_w', 'grad_conv_w': 'grad_w', 'grad_conv_b': 'grad_w', 'grad_ab_w_out': 'grad_w', 'grad_cd_w_in': 'grad_w', 'grad_sgu_norm_g': 'grad_w', 'grad_sgu_norm_b': 'grad_w', 'grad_sgu_w': 'grad_w', 'grad_sgu_b': 'grad_w', 'grad_cd_w_out': 'grad_w', 'grad_mlp_w1': 'grad_w', 'grad_mlp_w2': 'grad_w', 'grad_final_norm_g': 'grad_w', 'delta_mix_norm_g': 'delta_w', 'delta_mlp_norm_g': 'delta_w', 'delta_ab_w_in': 'delta_w', 'delta_pool_w': 'delta_w', 'delta_pool_scale': 'delta_w', 'delta_conv_w': 'delta_w', 'delta_conv_b': 'delta_w', 'delta_ab_w_out': 'delta_w', 'delta_cd_w_in': 'delta_w', 'delta_sgu_norm_g': 'delta_w', 'delta_sgu_norm_b': 'delta_w', 'delta_sgu_w': 'delta_w', 'delta_sgu_b': 'delta_w', 'delta_cd_w_out': 'delta_w', 'delta_mlp_w1': 'delta_w', 'delta_mlp_w2': 'delta_w', 'delta_final_norm_g': 'delta_w', 'new_m_mix_norm_g': 'new_m', 'new_m_mlp_norm_g': 'new_m', 'new_m_ab_w_in': 'new_m', 'new_m_pool_w': 'new_m', 'new_m_pool_scale': 'new_m', 'new_m_conv_w': 'new_m', 'new_m_conv_b': 'new_m', 'new_m_ab_w_out': 'new_m', 'new_m_cd_w_in': 'new_m', 'new_m_sgu_norm_g': 'new_m', 'new_m_sgu_norm_b': 'new_m', 'new_m_sgu_w': 'new_m', 'new_m_sgu_b': 'new_m', 'new_m_cd_w_out': 'new_m', 'new_m_mlp_w1': 'new_m', 'new_m_mlp_w2': 'new_m', 'new_m_final_norm_g': 'new_m', 'new_v_mix_norm_g': 'new_v', 'new_v_mlp_norm_g': 'new_v', 'new_v_ab_w_in': 'new_v', 'new_v_pool_w': 'new_v', 'new_v_pool_scale': 'new_v', 'new_v_conv_w': 'new_v', 'new_v_conv_b': 'new_v', 'new_v_ab_w_out': 'new_v', 'new_v_cd_w_in': 'new_v', 'new_v_sgu_norm_g': 'new_v', 'new_v_sgu_norm_b': 'new_v', 'new_v_sgu_w': 'new_v', 'new_v_sgu_b': 'new_v', 'new_v_cd_w_out': 'new_v', 'new_v_mlp_w1': 'new_v', 'new_v_mlp_w2': 'new_v', 'new_v_final_norm_g': 'new_v'}


def _forward(args):
    return _fwd_reference(*[args[k] for k in FWD_PARAMS])


def _output_shape():
    out = _jax.eval_shape(lambda: _forward(_fwd_setup_inputs(0)))
    return out.shape, out.dtype

N_MICROBATCH = 1
ADAM_LR = 0.001
ADAM_B1 = 0.9
ADAM_B2 = 0.999
ADAM_EPS = 1e-08
ADAM_WD = 0.01
ADAM_STEP = 10
PER_EXAMPLE_BATCH_AXIS = {'x': 0, 'loss_target': 0}
SHARED_INPUTS = []
_WEIGHT_DTYPES = {'mix_norm_g': _jnp.float32, 'mlp_norm_g': _jnp.float32, 'ab_w_in': _jnp.float32, 'pool_w': _jnp.float32, 'pool_scale': _jnp.float32, 'conv_w': _jnp.float32, 'conv_b': _jnp.float32, 'ab_w_out': _jnp.float32, 'cd_w_in': _jnp.float32, 'sgu_norm_g': _jnp.float32, 'sgu_norm_b': _jnp.float32, 'sgu_w': _jnp.float32, 'sgu_b': _jnp.float32, 'cd_w_out': _jnp.float32, 'mlp_w1': _jnp.float32, 'mlp_w2': _jnp.float32, 'final_norm_g': _jnp.float32}
MOMENT_SCALE = {'mix_norm_g': 9.650477e-02, 'mlp_norm_g': 9.897052e-02, 'ab_w_in': 8.509606e-02, 'pool_w': 7.724093e-02, 'pool_scale': 7.831200e-02, 'conv_w': 8.876470e-02, 'conv_b': 8.987864e-02, 'ab_w_out': 1.646309e-01, 'cd_w_in': 3.865198e-02, 'sgu_norm_g': 3.159107e-02, 'sgu_norm_b': 3.161694e-02, 'sgu_w': 3.176317e-02, 'sgu_b': 4.583648e-02, 'cd_w_out': 1.023255e-01, 'mlp_w1': 4.780133e-02, 'mlp_w2': 1.684281e-01, 'final_norm_g': 3.214066e+01}


def _to_microbatches(a, axis):
    t = _jnp.moveaxis(a, axis, 0)
    t = t.reshape((N_MICROBATCH, t.shape[0] // N_MICROBATCH) + t.shape[1:])
    return _jnp.moveaxis(t, 1, axis + 1)


def setup_inputs(seed: int = 0) -> dict:
    inp = _fwd_setup_inputs(seed)
    key = _jax.random.fold_in(_jax.random.key(seed), 7919)
    shape, _ = _output_shape()
    out = dict(inp)
    out["loss_target"] = _jax.random.normal(_jax.random.fold_in(key, 0), shape, _jnp.float32)
    for i, name in enumerate(TWIN_WEIGHTS):
        w = inp[name].astype(_jnp.float32)
        if MOMENT_SCALE is None:
            s = _jnp.sqrt(_jnp.mean(_jnp.square(w)) + 1e-30)
        else:
            s = MOMENT_SCALE[name]
        km, kv = _jax.random.split(_jax.random.fold_in(key, i + 1))
        out[name] = w
        out["m_" + name] = s * _jax.random.normal(km, w.shape, _jnp.float32)
        out["v_" + name] = (s * s) * _jax.random.uniform(kv, w.shape, _jnp.float32, 0.5, 1.5)
    if N_MICROBATCH > 1:
        for name, axis in PER_EXAMPLE_BATCH_AXIS.items():
            out[name] = _to_microbatches(out[name], axis)
    return {'x': out['x'], 'mix_norm_g': out['mix_norm_g'], 'mlp_norm_g': out['mlp_norm_g'], 'ab_w_in': out['ab_w_in'], 'pool_w': out['pool_w'], 'pool_scale': out['pool_scale'], 'conv_w': out['conv_w'], 'conv_b': out['conv_b'], 'ab_w_out': out['ab_w_out'], 'cd_w_in': out['cd_w_in'], 'sgu_norm_g': out['sgu_norm_g'], 'sgu_norm_b': out['sgu_norm_b'], 'sgu_w': out['sgu_w'], 'sgu_b': out['sgu_b'], 'cd_w_out': out['cd_w_out'], 'mlp_w1': out['mlp_w1'], 'mlp_w2': out['mlp_w2'], 'final_norm_g': out['final_norm_g'], 'loss_target': out['loss_target'], 'm_mix_norm_g': out['m_mix_norm_g'], 'm_mlp_norm_g': out['m_mlp_norm_g'], 'm_ab_w_in': out['m_ab_w_in'], 'm_pool_w': out['m_pool_w'], 'm_pool_scale': out['m_pool_scale'], 'm_conv_w': out['m_conv_w'], 'm_conv_b': out['m_conv_b'], 'm_ab_w_out': out['m_ab_w_out'], 'm_cd_w_in': out['m_cd_w_in'], 'm_sgu_norm_g': out['m_sgu_norm_g'], 'm_sgu_norm_b': out['m_sgu_norm_b'], 'm_sgu_w': out['m_sgu_w'], 'm_sgu_b': out['m_sgu_b'], 'm_cd_w_out': out['m_cd_w_out'], 'm_mlp_w1': out['m_mlp_w1'], 'm_mlp_w2': out['m_mlp_w2'], 'm_final_norm_g': out['m_final_norm_g'], 'v_mix_norm_g': out['v_mix_norm_g'], 'v_mlp_norm_g': out['v_mlp_norm_g'], 'v_ab_w_in': out['v_ab_w_in'], 'v_pool_w': out['v_pool_w'], 'v_pool_scale': out['v_pool_scale'], 'v_conv_w': out['v_conv_w'], 'v_conv_b': out['v_conv_b'], 'v_ab_w_out': out['v_ab_w_out'], 'v_cd_w_in': out['v_cd_w_in'], 'v_sgu_norm_g': out['v_sgu_norm_g'], 'v_sgu_norm_b': out['v_sgu_norm_b'], 'v_sgu_w': out['v_sgu_w'], 'v_sgu_b': out['v_sgu_b'], 'v_cd_w_out': out['v_cd_w_out'], 'v_mlp_w1': out['v_mlp_w1'], 'v_mlp_w2': out['v_mlp_w2'], 'v_final_norm_g': out['v_final_norm_g']}


def _loss(weights, diff, rest, loss_target):
    with _jax.named_scope("forward"):
        args = {**rest, TWIN_DIFF_INPUT: diff, **{k: w.astype(_WEIGHT_DTYPES[k]) for k, w in weights.items()}}
        y = _forward(args)
    with _jax.named_scope("loss_head"):
        err = _jnp.square(y.astype(_jnp.float32) - loss_target)
        return 0.5 * _jnp.sum(_jnp.mean(err, axis=-1)) if err.ndim else 0.5 * err


def _adamw(w, g, m, v):
    m = ADAM_B1 * m + (1.0 - ADAM_B1) * g
    v = ADAM_B2 * v + (1.0 - ADAM_B2) * _jnp.square(g)
    m_hat = m / (1.0 - ADAM_B1 ** ADAM_STEP)
    v_hat = v / (1.0 - ADAM_B2 ** ADAM_STEP)
    delta = -ADAM_LR * (m_hat / (_jnp.sqrt(v_hat) + ADAM_EPS) + ADAM_WD * w)
    return delta, m, v


def reference(x, mix_norm_g, mlp_norm_g, ab_w_in, pool_w, pool_scale, conv_w, conv_b, ab_w_out, cd_w_in, sgu_norm_g, sgu_norm_b, sgu_w, sgu_b, cd_w_out, mlp_w1, mlp_w2, final_norm_g, loss_target, m_mix_norm_g, m_mlp_norm_g, m_ab_w_in, m_pool_w, m_pool_scale, m_conv_w, m_conv_b, m_ab_w_out, m_cd_w_in, m_sgu_norm_g, m_sgu_norm_b, m_sgu_w, m_sgu_b, m_cd_w_out, m_mlp_w1, m_mlp_w2, m_final_norm_g, v_mix_norm_g, v_mlp_norm_g, v_ab_w_in, v_pool_w, v_pool_scale, v_conv_w, v_conv_b, v_ab_w_out, v_cd_w_in, v_sgu_norm_g, v_sgu_norm_b, v_sgu_w, v_sgu_b, v_cd_w_out, v_mlp_w1, v_mlp_w2, v_final_norm_g):
    given = dict(x=x, mix_norm_g=mix_norm_g, mlp_norm_g=mlp_norm_g, ab_w_in=ab_w_in, pool_w=pool_w, pool_scale=pool_scale, conv_w=conv_w, conv_b=conv_b, ab_w_out=ab_w_out, cd_w_in=cd_w_in, sgu_norm_g=sgu_norm_g, sgu_norm_b=sgu_norm_b, sgu_w=sgu_w, sgu_b=sgu_b, cd_w_out=cd_w_out, mlp_w1=mlp_w1, mlp_w2=mlp_w2, final_norm_g=final_norm_g, loss_target=loss_target, m_mix_norm_g=m_mix_norm_g, m_mlp_norm_g=m_mlp_norm_g, m_ab_w_in=m_ab_w_in, m_pool_w=m_pool_w, m_pool_scale=m_pool_scale, m_conv_w=m_conv_w, m_conv_b=m_conv_b, m_ab_w_out=m_ab_w_out, m_cd_w_in=m_cd_w_in, m_sgu_norm_g=m_sgu_norm_g, m_sgu_norm_b=m_sgu_norm_b, m_sgu_w=m_sgu_w, m_sgu_b=m_sgu_b, m_cd_w_out=m_cd_w_out, m_mlp_w1=m_mlp_w1, m_mlp_w2=m_mlp_w2, m_final_norm_g=m_final_norm_g, v_mix_norm_g=v_mix_norm_g, v_mlp_norm_g=v_mlp_norm_g, v_ab_w_in=v_ab_w_in, v_pool_w=v_pool_w, v_pool_scale=v_pool_scale, v_conv_w=v_conv_w, v_conv_b=v_conv_b, v_ab_w_out=v_ab_w_out, v_cd_w_in=v_cd_w_in, v_sgu_norm_g=v_sgu_norm_g, v_sgu_norm_b=v_sgu_norm_b, v_sgu_w=v_sgu_w, v_sgu_b=v_sgu_b, v_cd_w_out=v_cd_w_out, v_mlp_w1=v_mlp_w1, v_mlp_w2=v_mlp_w2, v_final_norm_g=v_final_norm_g)
    weights = {n: given[n] for n in TWIN_WEIGHTS}
    shared = {n: given[n] for n in SHARED_INPUTS}
    per_example = {n: given[n] for n in ['x']}
    grad_fn = _jax.value_and_grad(_loss, argnums=(0, 1))

    def one_microbatch(ex, loss_target):
        ex = dict(ex)
        diff = ex.pop(TWIN_DIFF_INPUT)
        return grad_fn(weights, diff, {**shared, **ex}, loss_target)

    if N_MICROBATCH == 1:
        loss, (grad_w, grad_x) = one_microbatch(per_example, given["loss_target"])
    else:
        def body(carry, xs):
            loss_sum, grad_sum = carry
            l_k, (gw_k, gx_k) = one_microbatch(xs[0], xs[1])
            with _jax.named_scope("update"):
                return (loss_sum + l_k, _jax.tree.map(_jnp.add, grad_sum, gw_k)), gx_k

        init = (_jnp.zeros((), _jnp.float32), _jax.tree.map(_jnp.zeros_like, weights))
        (loss, grad_w), grad_x = _jax.lax.scan(body, init, (per_example, given["loss_target"]))
    with _jax.named_scope("update"):
        delta_w, new_m, new_v = {}, {}, {}
        for n in TWIN_WEIGHTS:
            delta_w[n], new_m[n], new_v[n] = _adamw(weights[n], grad_w[n], given["m_" + n], given["v_" + n])
    return (loss, grad_x, *[grad_w[n] for n in TWIN_WEIGHTS], *[delta_w[n] for n in TWIN_WEIGHTS],
            *[new_m[n] for n in TWIN_WEIGHTS], *[new_v[n] for n in TWIN_WEIGHTS])
```

```python
import functools
import math

import jax
import jax.numpy as jnp
from jax import lax
from jax.experimental import pallas as pl
from jax.experimental.pallas import tpu as pltpu

F32 = jnp.float32
BF16 = jnp.bfloat16

NORM_EPS = 1e-6
ADAM_LR = 0.001
ADAM_B1 = 0.9
ADAM_B2 = 0.999
ADAM_EPS = 1e-08
ADAM_WD = 0.01
ADAM_STEP = 10
ADAM_C1 = 1.0 - ADAM_B1 ** ADAM_STEP
ADAM_C2 = 1.0 - ADAM_B2 ** ADAM_STEP

N_DEV = 8
LANES = 128
SUBLANES = 8
SB_DH = 64
ATT_BLOCK = 128
POOL_LOG_WINDOWS = 4
VMEM_LIMIT = 56 * 1024 * 1024


def _params(semantics=None):
    return pltpu.CompilerParams(dimension_semantics=semantics, vmem_limit_bytes=VMEM_LIMIT)


def _tile(dim, pref, unit=LANES):
    if dim <= pref:
        return dim
    t = (pref // unit) * unit
    while t >= unit:
        if dim % t == 0:
            return t
        t -= unit
    return dim


def _matmul(a, b, *, name, ta=False, tb=False, extras=(), epilogue=None, out_dtypes=(F32,),
            tm=512, tn=512, tk=1024):
    M, K = (a.shape[1], a.shape[0]) if ta else a.shape
    N = b.shape[0] if tb else b.shape[1]
    assert (b.shape[1] if tb else b.shape[0]) == K, (a.shape, b.shape)
    tm, tn, tk = _tile(M, tm), _tile(N, tn), _tile(K, tk)
    nk = K // tk
    dims = (((0 if ta else 1,), (1 if tb else 0,)), ((), ()))
    ne, no = len(extras), len(out_dtypes)

    def body(*refs):
        a_ref, b_ref = refs[0], refs[1]
        e_refs = refs[2:2 + ne]
        o_refs = refs[2 + ne:2 + ne + no]
        k = pl.program_id(2)

        def part():
            return lax.dot_general(a_ref[...].astype(BF16), b_ref[...].astype(BF16), dims,
                                   preferred_element_type=F32)

        def finish(acc):
            outs = epilogue(acc, *[e[...] for e in e_refs]) if epilogue is not None else (acc,)
            for o_ref, val in zip(o_refs, outs):
                o_ref[...] = val.astype(o_ref.dtype)

        if nk == 1:
            finish(part())
        else:
            acc_ref = refs[-1]

            @pl.when(k == 0)
            def _():
                acc_ref[...] = jnp.zeros_like(acc_ref)

            acc_ref[...] += part()

            @pl.when(k == nk - 1)
            def _():
                finish(acc_ref[...])

    a_spec = (pl.BlockSpec((tk, tm), lambda i, j, k: (k, i)) if ta
              else pl.BlockSpec((tm, tk), lambda i, j, k: (i, k)))
    b_spec = (pl.BlockSpec((tn, tk), lambda i, j, k: (j, k)) if tb
              else pl.BlockSpec((tk, tn), lambda i, j, k: (k, j)))
    o_spec = pl.BlockSpec((tm, tn), lambda i, j, k: (i, j))
    outs = pl.pallas_call(
        body,
        name=name,
        grid=(M // tm, N // tn, nk),
        in_specs=[a_spec, b_spec] + [o_spec] * ne,
        out_specs=[o_spec] * no,
        out_shape=[jax.ShapeDtypeStruct((M, N), dt) for dt in out_dtypes],
        scratch_shapes=[pltpu.VMEM((tm, tn), F32)] if nk > 1 else [],
        compiler_params=_params(("parallel", "parallel", "arbitrary")),
    )(a, b, *extras)
    return outs[0] if no == 1 else outs


def _ep_add(acc, res):
    return (acc + res,)


def _ep_relu2(acc):
    r = jnp.maximum(acc, 0.0)
    return acc, r * r


def _ep_relu2_bwd(acc, z):
    return (acc * (2.0 * jnp.maximum(z, 0.0)),)


def _rstd(x):
    return lax.rsqrt(jnp.mean(x * x, axis=-1, keepdims=True) + NORM_EPS)


def _rmsnorm(h, g, *, name, tr=512):
    N, D = h.shape
    tr = _tile(N, tr, SUBLANES)

    def body(h_ref, g_ref, o_ref):
        x = h_ref[...]
        o_ref[...] = ((x * _rstd(x)) * g_ref[...]).astype(o_ref.dtype)

    row = pl.BlockSpec((tr, D), lambda i: (i, 0))
    vec = pl.BlockSpec((1, D), lambda i: (0, 0))
    return pl.pallas_call(
        body, name=name, grid=(N // tr,), in_specs=[row, vec], out_specs=row,
        out_shape=jax.ShapeDtypeStruct((N, D), BF16), compiler_params=_params(("parallel",)),
    )(h, g)


def _rmsnorm_bwd(dy, h, g, dres, *, name, tr=512):
    N, D = h.shape
    tr = _tile(N, tr, SUBLANES)

    def body(dy_ref, h_ref, g_ref, r_ref, dh_ref, dg_ref):
        i = pl.program_id(0)
        x = h_ref[...]
        d = dy_ref[...]
        r = _rstd(x)
        xh = x * r

        @pl.when(i == 0)
        def _():
            dg_ref[...] = jnp.zeros_like(dg_ref)

        dg_ref[...] += jnp.sum(d * xh, axis=0, keepdims=True)
        dxh = d * g_ref[...]
        dh_ref[...] = r_ref[...] + r * (dxh - xh * jnp.mean(dxh * xh, axis=-1, keepdims=True))

    row = pl.BlockSpec((tr, D), lambda i: (i, 0))
    vec = pl.BlockSpec((1, D), lambda i: (0, 0))
    return pl.pallas_call(
        body, name=name, grid=(N // tr,), in_specs=[row, row, vec, row], out_specs=[row, vec],
        out_shape=[jax.ShapeDtypeStruct((N, D), F32), jax.ShapeDtypeStruct((1, D), F32)],
        compiler_params=_params(("arbitrary",)),
    )(dy, h, g, dres)


def _final_loss(h, g, target, *, name, tr=512):
    N, D = h.shape
    tr = _tile(N, tr, SUBLANES)

    def body(h_ref, g_ref, t_ref, loss_ref, dh_ref, dg_ref):
        i = pl.program_id(0)
        x = h_ref[...]
        gg = g_ref[...]
        r = _rstd(x)
        xh = x * r
        err = xh * gg - t_ref[...]

        @pl.when(i == 0)
        def _():
            dg_ref[...] = jnp.zeros_like(dg_ref)
            loss_ref[...] = jnp.zeros_like(loss_ref)

        per_row = jnp.mean(err * err, axis=-1, keepdims=True)
        loss_ref[...] += 0.5 * jnp.sum(per_row, axis=0, keepdims=True)
        dy = err * (1.0 / D)
        dg_ref[...] += jnp.sum(dy * xh, axis=0, keepdims=True)
        dxh = dy * gg
        dh_ref[...] = r * (dxh - xh * jnp.mean(dxh * xh, axis=-1, keepdims=True))

    row = pl.BlockSpec((tr, D), lambda i: (i, 0))
    vec = pl.BlockSpec((1, D), lambda i: (0, 0))
    lvec = pl.BlockSpec((1, LANES), lambda i: (0, 0))
    return pl.pallas_call(
        body, name=name, grid=(N // tr,), in_specs=[row, vec, row], out_specs=[lvec, row, vec],
        out_shape=[jax.ShapeDtypeStruct((1, LANES), F32), jax.ShapeDtypeStruct((N, D), F32),
                   jax.ShapeDtypeStruct((1, D), F32)],
        compiler_params=_params(("arbitrary",)),
    )(h, g, target)


def _shift_down(x, s):
    t = lax.broadcasted_iota(jnp.int32, x.shape, 0)
    return jnp.where(t >= s, pltpu.roll(x, s, 0), 0.0)


def _shift_up(x, s):
    n = x.shape[0]
    t = lax.broadcasted_iota(jnp.int32, x.shape, 0)
    return jnp.where(t < n - s, pltpu.roll(x, n - s, 0), 0.0)


def _window_sum(x, g, shift):
    s = x + shift(x, 1)
    for k in range(1, POOL_LOG_WINDOWS):
        s = jnp.where(k <= g, s + shift(s, 2 ** k), s)
    return s


def _pool_count(shape, g):
    t = lax.broadcasted_iota(jnp.int32, shape, 0)
    return jnp.minimum(t + 1, lax.shift_left(jnp.int32(2), g)).astype(F32)


def _pool_fwd(p, pool_w, pool_scale, B, T, *, name):
    G, dh = pool_w.shape[0], pool_w.shape[1]
    assert G == POOL_LOG_WINDOWS

    def body(a_ref, w_ref, s_ref, o_ref):
        g = pl.program_id(0)
        a = a_ref[...]
        pooled = _window_sum(a, g, _shift_down) / _pool_count(a.shape, g) - a
        m = jnp.dot(pooled.astype(BF16), w_ref[0].astype(BF16), preferred_element_type=F32)
        o_ref[...] = (m * s_ref[0]).astype(o_ref.dtype)

    blk = pl.BlockSpec((T, dh), lambda g, b: (b, g))
    return pl.pallas_call(
        body, name=name, grid=(G, B),
        in_specs=[blk, pl.BlockSpec((1, dh, dh), lambda g, b: (g, 0, 0)),
                  pl.BlockSpec((1, 1, dh), lambda g, b: (g, 0, 0))],
        out_specs=blk, out_shape=jax.ShapeDtypeStruct((B * T, G * dh), BF16),
        compiler_params=_params(("parallel", "parallel")),
    )(p, pool_w, pool_scale.reshape(G, 1, dh))


def _pool_bwd(p, dcat, pool_w, pool_scale, B, T, *, name):
    G, dh = pool_w.shape[0], pool_w.shape[1]

    def body(a_ref, d_ref, w_ref, s_ref, da_ref, dw_ref, ds_ref):
        g = pl.program_id(0)
        b = pl.program_id(1)
        a = a_ref[...]
        d = d_ref[...]
        cnt = _pool_count(a.shape, g)
        pooled = (_window_sum(a, g, _shift_down) / cnt - a).astype(BF16)
        w = w_ref[0].astype(BF16)
        m = jnp.dot(pooled, w, preferred_element_type=F32)

        @pl.when(b == 0)
        def _():
            dw_ref[...] = jnp.zeros_like(dw_ref)
            ds_ref[...] = jnp.zeros_like(ds_ref)

        ds_ref[0] += jnp.sum(d * m, axis=0, keepdims=True)
        dm = (d * s_ref[0]).astype(BF16)
        dw_ref[0] += lax.dot_general(pooled, dm, (((0,), (0,)), ((), ())), preferred_element_type=F32)
        dpooled = lax.dot_general(dm, w, (((1,), (1,)), ((), ())), preferred_element_type=F32)
        da = _window_sum(dpooled / cnt, g, _shift_up) - dpooled
        da_ref[...] = da.astype(da_ref.dtype)

    blk = pl.BlockSpec((T, dh), lambda g, b: (b, g))
    wspec = pl.BlockSpec((1, dh, dh), lambda g, b: (g, 0, 0))
    sspec = pl.BlockSpec((1, 1, dh), lambda g, b: (g, 0, 0))
    return pl.pallas_call(
        body, name=name, grid=(G, B), in_specs=[blk, blk, wspec, sspec], out_specs=[blk, wspec, sspec],
        out_shape=[jax.ShapeDtypeStruct((B * T, G * dh), BF16), jax.ShapeDtypeStruct((G, dh, dh), F32),
                   jax.ShapeDtypeStruct((G, 1, dh), F32)],
        compiler_params=_params(("parallel", "arbitrary")),
    )(p, dcat, pool_w, pool_scale.reshape(G, 1, dh))


def _conv_specs(T, tc, off, CW):
    def at(k):
        base = (off + k * CW) // tc
        return pl.BlockSpec((T, tc), lambda j, b: (b, base + j))
    return [at(0), at(1), at(2)]


def _conv_fwd(p, conv_w, conv_b, B, T, off, *, name):
    CW = conv_w.shape[1]
    tc = LANES
    assert off % tc == 0 and CW % tc == 0

    def body(xb_ref, gb_ref, gc_ref, w_ref, b_ref, o_ref):
        c = gc_ref[...] * xb_ref[...]
        w = w_ref[...]
        y = _shift_down(c, 2) * w[0:1] + _shift_down(c, 1) * w[1:2] + c * w[2:3] + b_ref[...]
        o_ref[...] = (gb_ref[...] * y).astype(o_ref.dtype)

    return pl.pallas_call(
        body, name=name, grid=(CW // tc, B),
        in_specs=_conv_specs(T, tc, off, CW) + [pl.BlockSpec((3, tc), lambda j, b: (0, j)),
                                                 pl.BlockSpec((1, tc), lambda j, b: (0, j))],
        out_specs=pl.BlockSpec((T, tc), lambda j, b: (b, j)),
        out_shape=jax.ShapeDtypeStruct((B * T, CW), BF16),
        compiler_params=_params(("parallel", "parallel")),
    )(p, p, p, conv_w, conv_b)


def _conv_bwd(p, dcat, conv_w, conv_b, B, T, off, doff, *, name):
    CW = conv_w.shape[1]
    tc = LANES
    assert doff % tc == 0

    def body(xb_ref, gb_ref, gc_ref, d_ref, w_ref, b_ref, dxb_ref, dgb_ref, dgc_ref, dw_ref, db_ref):
        b = pl.program_id(1)
        xb, gb, gc, d = xb_ref[...], gb_ref[...], gc_ref[...], d_ref[...]
        w = w_ref[...]
        c = gc * xb
        c1 = _shift_down(c, 1)
        c2 = _shift_down(c, 2)
        y = c2 * w[0:1] + c1 * w[1:2] + c * w[2:3] + b_ref[...]
        dy = d * gb
        dgb_ref[...] = (d * y).astype(dgb_ref.dtype)

        @pl.when(b == 0)
        def _():
            dw_ref[...] = jnp.zeros_like(dw_ref)
            db_ref[...] = jnp.zeros_like(db_ref)

        db_ref[...] += jnp.sum(dy, axis=0, keepdims=True)
        dw_ref[0:1, :] += jnp.sum(dy * c2, axis=0, keepdims=True)
        dw_ref[1:2, :] += jnp.sum(dy * c1, axis=0, keepdims=True)
        dw_ref[2:3, :] += jnp.sum(dy * c, axis=0, keepdims=True)
        dc = dy * w[2:3] + _shift_up(dy, 1) * w[1:2] + _shift_up(dy, 2) * w[0:1]
        dgc_ref[...] = (dc * xb).astype(dgc_ref.dtype)
        dxb_ref[...] = (dc * gc).astype(dxb_ref.dtype)

    dbase = doff // tc
    wspec = pl.BlockSpec((3, tc), lambda j, b: (0, j))
    bspec = pl.BlockSpec((1, tc), lambda j, b: (0, j))

    oblk = pl.BlockSpec((T, tc), lambda j, b: (b, j))
    act = jax.ShapeDtypeStruct((B * T, CW), BF16)

    return pl.pallas_call(
        body, name=name, grid=(CW // tc, B),
        in_specs=_conv_specs(T, tc, off, CW) + [pl.BlockSpec((T, tc), lambda j, b: (b, dbase + j)), wspec, bspec],
        out_specs=[oblk, oblk, oblk, wspec, bspec],
        out_shape=[act, act, act, jax.ShapeDtypeStruct((3, CW), F32), jax.ShapeDtypeStruct((1, CW), F32)],
        compiler_params=_params(("parallel", "arbitrary")),
    )(p, p, p, dcat, conv_w, conv_b)


_SQRT_HALF = 0.7071067811865476
_INV_SQRT_2PI = 0.3989422804014327


def _gelu(x):
    return x * (lax.erf(x * _SQRT_HALF) + 1.0) * 0.5


def _gelu_grad(x):
    return 0.5 * (lax.erf(x * _SQRT_HALF) + 1.0) + x * (_INV_SQRT_2PI * jnp.exp(-0.5 * x * x))


def _layernorm_parts(v):
    mu = jnp.mean(v, axis=-1, keepdims=True)
    vc = v - mu
    rstd = lax.rsqrt(jnp.mean(vc * vc, axis=-1, keepdims=True) + NORM_EPS)
    return vc * rstd, rstd


def _tril_mask(L):
    r = lax.broadcasted_iota(jnp.int32, (L, L), 0)
    c = lax.broadcasted_iota(jnp.int32, (L, L), 1)
    return r >= c


def _sgu_fwd(p, ln_g, ln_b, sgu_w, sgu_b, *, name, tr=512):
    N = p.shape[0]
    G, L = sgu_w.shape[0], sgu_w.shape[1]
    SW = ln_g.shape[1]
    dh = SW // G
    tr = _tile(N, tr, L)
    assert tr % L == 0

    def body(u_ref, v_ref, g_ref, beta_ref, w_ref, b_ref, o_ref):
        u = _gelu(u_ref[...])
        vhat, _ = _layernorm_parts(_gelu(v_ref[...]))
        vn = (vhat * g_ref[...] + beta_ref[...]).astype(BF16)
        mask = _tril_mask(L)
        for gi in range(G):
            w = jnp.where(mask, w_ref[gi], 0.0).astype(BF16)
            bias = b_ref[gi]
            cols = slice(gi * dh, (gi + 1) * dh)
            for n in range(tr // L):
                rows = slice(n * L, (n + 1) * L)
                s = jnp.dot(w, vn[rows, cols], preferred_element_type=F32) + bias
                o_ref[rows, cols] = (u[rows, cols] * s).astype(o_ref.dtype)

    def col(k):
        return pl.BlockSpec((tr, SW), lambda i: (i, k))

    vec = pl.BlockSpec((1, SW), lambda i: (0, 0))
    return pl.pallas_call(
        body, name=name, grid=(N // tr,),
        in_specs=[col(0), col(1), vec, vec, pl.BlockSpec((G, L, L), lambda i: (0, 0, 0)),
                  pl.BlockSpec((G, L, 1), lambda i: (0, 0, 0))],
        out_specs=col(0), out_shape=jax.ShapeDtypeStruct((N, SW), BF16),
        compiler_params=_params(("parallel",)),
    )(p, p, ln_g, ln_b, sgu_w, sgu_b.reshape(G, L, 1))


def _sgu_bwd(p, dcat, ln_g, ln_b, sgu_w, sgu_b, *, name, tr=512):
    N = p.shape[0]
    G, L = sgu_w.shape[0], sgu_w.shape[1]
    SW = ln_g.shape[1]
    dh = SW // G
    tr = _tile(N, tr, L)

    def body(u_ref, v_ref, dc_ref, g_ref, beta_ref, w_ref, b_ref,
             du_ref, dv_ref, dw_ref, db_ref, dg_ref, dbeta_ref, du_s, dvn_s):
        i = pl.program_id(0)
        pu = u_ref[...]
        pv = v_ref[...]
        u = _gelu(pu)
        vhat, rstd = _layernorm_parts(_gelu(pv))
        gg = g_ref[...]
        vn = (vhat * gg + beta_ref[...]).astype(BF16)
        dc = dc_ref[...]
        mask = _tril_mask(L)

        @pl.when(i == 0)
        def _():
            dw_ref[...] = jnp.zeros_like(dw_ref)
            db_ref[...] = jnp.zeros_like(db_ref)
            dg_ref[...] = jnp.zeros_like(dg_ref)
            dbeta_ref[...] = jnp.zeros_like(dbeta_ref)

        for gi in range(G):
            w = jnp.where(mask, w_ref[gi], 0.0).astype(BF16)
            bias = b_ref[gi]
            cols = slice(gi * dh, (gi + 1) * dh)
            dw_acc = jnp.zeros((L, L), F32)
            db_acc = jnp.zeros((L, 1), F32)
            for n in range(tr // L):
                rows = slice(n * L, (n + 1) * L)
                vb = vn[rows, cols]
                s = jnp.dot(w, vb, preferred_element_type=F32) + bias
                du_s[rows, cols] = dc[rows, cols] * s
                ds = dc[rows, cols] * u[rows, cols]
                db_acc += jnp.sum(ds, axis=1, keepdims=True)
                dsb = ds.astype(BF16)
                dw_acc += lax.dot_general(dsb, vb, (((1,), (1,)), ((), ())), preferred_element_type=F32)
                dvn_s[rows, cols] = lax.dot_general(w, dsb, (((0,), (0,)), ((), ())),
                                                    preferred_element_type=F32)
            dw_ref[gi] += jnp.where(mask, dw_acc, 0.0)
            db_ref[gi] += db_acc

        dvn = dvn_s[...]
        dg_ref[...] += jnp.sum(dvn * vhat, axis=0, keepdims=True)
        dbeta_ref[...] += jnp.sum(dvn, axis=0, keepdims=True)
        dvh = dvn * gg
        dv = rstd * (dvh - jnp.mean(dvh, axis=-1, keepdims=True)
                     - vhat * jnp.mean(dvh * vhat, axis=-1, keepdims=True))
        dv_ref[...] = (dv * _gelu_grad(pv)).astype(dv_ref.dtype)
        du_ref[...] = (du_s[...] * _gelu_grad(pu)).astype(du_ref.dtype)

    def col(k):
        return pl.BlockSpec((tr, SW), lambda i: (i, k))

    vec = pl.BlockSpec((1, SW), lambda i: (0, 0))
    wspec = pl.BlockSpec((G, L, L), lambda i: (0, 0, 0))
    bspec = pl.BlockSpec((G, L, 1), lambda i: (0, 0, 0))
    return pl.pallas_call(
        body, name=name, grid=(N // tr,),
        in_specs=[col(0), col(1), col(0), vec, vec, wspec, bspec],
        out_specs=[col(0), col(0), wspec, bspec, vec, vec],
        out_shape=[jax.ShapeDtypeStruct((N, SW), BF16), jax.ShapeDtypeStruct((N, SW), BF16),
                   jax.ShapeDtypeStruct((G, L, L), F32), jax.ShapeDtypeStruct((G, L, 1), F32),
                   jax.ShapeDtypeStruct((1, SW), F32), jax.ShapeDtypeStruct((1, SW), F32)],
        scratch_shapes=[pltpu.VMEM((tr, SW), F32), pltpu.VMEM((tr, SW), F32)],
        compiler_params=_params(("arbitrary",)),
    )(p, p, dcat, ln_g, ln_b, sgu_w, sgu_b.reshape(G, L, 1))


def _log_sigmoid(z):
    return jnp.minimum(z, 0.0) - jnp.log1p(jnp.exp(-jnp.abs(z)))


def _split_dot(x, m):
    hi = x.astype(BF16)
    lo = (x - hi.astype(F32)).astype(BF16)
    return (jnp.dot(hi, m, preferred_element_type=F32) + jnp.dot(lo, m, preferred_element_type=F32))


def _att_tiles(TB):
    r = lax.broadcasted_iota(jnp.int32, (TB, TB), 0)
    c = lax.broadcasted_iota(jnp.int32, (TB, TB), 1)
    return r, c


def _att_weights(q, kb, scale, valid, later, carry):
    z = lax.dot_general(q, kb, (((1,), (1,)), ((), ())), preferred_element_type=F32) * scale
    ls = _log_sigmoid(z)
    lk = jnp.where(valid, ls - z, 0.0)
    suffix = _split_dot(lk, later) + carry
    a = jnp.where(valid, jnp.exp(ls + suffix), 0.0)
    return z, lk, a


def _attn_fwd(q, k, v, *, name):
    BH, T, dh = q.shape
    TB = _tile(T, ATT_BLOCK)
    nb = T // TB
    scale = 1.0 / math.sqrt(dh)

    def body(q_ref, k_ref, v_ref, o_ref):
        r, c = _att_tiles(TB)
        strict = c < r
        later = (r > c).astype(BF16)

        def qblock(i, _):
            q0 = pl.multiple_of(i * TB, TB)
            qt = q_ref[pl.ds(q0, TB), :]

            def kblock(jj, st):
                acc, carry = st
                j = i - jj
                k0 = pl.multiple_of(j * TB, TB)
                valid = jnp.logical_or(strict, j < i)
                _, lk, a = _att_weights(qt, k_ref[pl.ds(k0, TB), :], scale, valid, later, carry)
                acc = acc + jnp.dot(a.astype(BF16), v_ref[pl.ds(k0, TB), :], preferred_element_type=F32)
                return acc, carry + jnp.sum(lk, axis=1, keepdims=True)

            acc, _ = lax.fori_loop(0, i + 1, kblock, (jnp.zeros((TB, dh), F32), jnp.zeros((TB, 1), F32)))
            o_ref[pl.ds(q0, TB), :] = acc.astype(o_ref.dtype)
            return 0

        lax.fori_loop(0, nb, qblock, 0)

    blk = pl.BlockSpec((None, T, dh), lambda g: (g, 0, 0))
    return pl.pallas_call(
        body, name=name, grid=(BH,), in_specs=[blk, blk, blk], out_specs=blk,
        out_shape=jax.ShapeDtypeStruct((BH, T, dh), BF16), compiler_params=_params(("parallel",)),
    )(q, k, v)


def _attn_bwd(q, k, v, do, *, name):
    BH, T, dh = q.shape
    TB = _tile(T, ATT_BLOCK)
    nb = T // TB
    scale = 1.0 / math.sqrt(dh)

    def body(q_ref, k_ref, v_ref, do_ref, dq_ref, dk_ref, dv_ref, z_s, a_s):
        r, c = _att_tiles(TB)
        strict = c < r
        later = (r > c).astype(BF16)
        earlier = (r < c).astype(BF16)
        dk_ref[...] = jnp.zeros_like(dk_ref)
        dv_ref[...] = jnp.zeros_like(dv_ref)

        def qblock(i, _):
            q0 = pl.multiple_of(i * TB, TB)
            qt = q_ref[pl.ds(q0, TB), :]
            dot = do_ref[pl.ds(q0, TB), :]

            def weights(jj, carry):
                j = i - jj
                k0 = pl.multiple_of(j * TB, TB)
                valid = jnp.logical_or(strict, j < i)
                z, lk, a = _att_weights(qt, k_ref[pl.ds(k0, TB), :], scale, valid, later, carry)
                z_s[j] = z
                a_s[j] = a
                dv_ref[pl.ds(k0, TB), :] += lax.dot_general(a.astype(BF16), dot, (((0,), (0,)), ((), ())),
                                                            preferred_element_type=F32)
                return carry + jnp.sum(lk, axis=1, keepdims=True)

            lax.fori_loop(0, i + 1, weights, jnp.zeros((TB, 1), F32))

            def grads(j, st):
                dq, before = st
                k0 = pl.multiple_of(j * TB, TB)
                valid = jnp.logical_or(strict, j < i)
                kb = k_ref[pl.ds(k0, TB), :]
                da = lax.dot_general(dot, v_ref[pl.ds(k0, TB), :], (((1,), (1,)), ((), ())),
                                     preferred_element_type=F32)
                gl = a_s[j] * da
                prefix = _split_dot(gl, earlier) + before
                beta = jax.nn.sigmoid(z_s[j])
                dz = jnp.where(valid, gl * (1.0 - beta) - beta * prefix, 0.0) * scale
                dzb = dz.astype(BF16)
                dq = dq + jnp.dot(dzb, kb, preferred_element_type=F32)
                dk_ref[pl.ds(k0, TB), :] += lax.dot_general(dzb, qt, (((0,), (0,)), ((), ())),
                                                            preferred_element_type=F32)
                return dq, before + jnp.sum(gl, axis=1, keepdims=True)

            dq, _ = lax.fori_loop(0, i + 1, grads, (jnp.zeros((TB, dh), F32), jnp.zeros((TB, 1), F32)))
            dq_ref[pl.ds(q0, TB), :] = dq
            return 0

        lax.fori_loop(0, nb, qblock, 0)

    blk = pl.BlockSpec((None, T, dh), lambda g: (g, 0, 0))
    shp = jax.ShapeDtypeStruct((BH, T, dh), F32)
    return pl.pallas_call(
        body, name=name, grid=(BH,), in_specs=[blk] * 4, out_specs=[blk] * 3, out_shape=[shp] * 3,
        scratch_shapes=[pltpu.VMEM((nb, TB, TB), F32), pltpu.VMEM((nb, TB, TB), F32)],
        compiler_params=_params(("parallel",)),
    )(q, k, v, do)


def _adamw(w, gparts, m, v, *, name):
    R, C = w.shape
    P = gparts.shape[0]
    tr = _tile(R, max(SUBLANES, (1 << 19) // (C * P)), SUBLANES)

    def body(w_ref, g_ref, m_ref, v_ref, go_ref, d_ref, mo_ref, vo_ref):
        g = g_ref[0]
        for i in range(1, P):
            g = g + g_ref[i]
        m2 = ADAM_B1 * m_ref[...] + (1.0 - ADAM_B1) * g
        v2 = ADAM_B2 * v_ref[...] + (1.0 - ADAM_B2) * (g * g)
        m_hat = m2 / ADAM_C1
        v_hat = v2 / ADAM_C2
        go_ref[...] = g
        d_ref[...] = -ADAM_LR * (m_hat / (jnp.sqrt(v_hat) + ADAM_EPS) + ADAM_WD * w_ref[...])
        mo_ref[...] = m2
        vo_ref[...] = v2

    row = pl.BlockSpec((tr, C), lambda i: (i, 0))
    shp = jax.ShapeDtypeStruct((R, C), F32)
    return pl.pallas_call(
        body, name=name, grid=(R // tr,),
        in_specs=[row, pl.BlockSpec((P, tr, C), lambda i: (0, i, 0)), row, row],
        out_specs=[row] * 4, out_shape=[shp] * 4, compiler_params=_params(("parallel",)),
    )(w, gparts, m, v)


def _my_index():
    return 4 * lax.axis_index("x") + 2 * lax.axis_index("y") + lax.axis_index("c")


def _exchange(arrs, gather, *, name):
    n = len(arrs)

    def body(*refs):
        ins, outs = refs[:n], refs[n:2 * n]
        send_sems, recv_sems, local_sems = refs[2 * n:]
        x, y, c = lax.axis_index("x"), lax.axis_index("y"), lax.axis_index("c")
        me = 4 * x + 2 * y + c
        remote, local = [], []
        for a in range(n):
            own = ins[a] if gather[a] else ins[a].at[me]
            cp = pltpu.make_async_copy(own, outs[a].at[me], local_sems.at[a])
            cp.start()
            local.append(cp)
            for k in range(1, N_DEV):
                px = 1 - x if k & 4 else x
                py = 1 - y if k & 2 else y
                pc = 1 - c if k & 1 else c
                src = ins[a] if gather[a] else ins[a].at[4 * px + 2 * py + pc]
                cp = pltpu.make_async_remote_copy(
                    src_ref=src, dst_ref=outs[a].at[me],
                    send_sem=send_sems.at[a, k - 1], recv_sem=recv_sems.at[a, k - 1],
                    device_id=(px, py, pc), device_id_type=pl.DeviceIdType.MESH)
                cp.start()
                remote.append(cp)
        for cp in remote:
            cp.wait()
        for cp in local:
            cp.wait()

    hbm = pl.BlockSpec(memory_space=pltpu.HBM)
    out_shape = [jax.ShapeDtypeStruct(((N_DEV,) + a.shape) if g else a.shape, a.dtype)
                 for a, g in zip(arrs, gather)]
    return pl.pallas_call(
        body, name=name, in_specs=[hbm] * n, out_specs=[hbm] * n, out_shape=out_shape,
        scratch_shapes=[pltpu.SemaphoreType.DMA((n, N_DEV - 1)), pltpu.SemaphoreType.DMA((n, N_DEV - 1)),
                        pltpu.SemaphoreType.DMA((n,))],
    )(*arrs)


def _heads_in(cols, B, T, H):
    t = cols.reshape(B, T, 3, H, SB_DH).transpose(2, 0, 3, 1, 4)
    return t.reshape(3, B * H, T, SB_DH).astype(BF16)


def _heads_out(t, B, T, H):
    return t.reshape(B, H, T, SB_DH).transpose(0, 2, 1, 3).reshape(B * T, H * SB_DH)


def _mlp_fwd(h, g, w1, w2, tag):
    hn = _rmsnorm(h, g, name=f"mlp_norm_{tag}")
    z, act = _matmul(hn, w1, name=f"mlp_up_{tag}", epilogue=_ep_relu2, out_dtypes=(F32, BF16))
    out = _matmul(act, w2, name=f"mlp_down_{tag}", extras=(h,), epilogue=_ep_add)
    return out, (hn, z, act)


def _mlp_bwd(dout, h, g, w1, w2, saved, tag):
    hn, z, act = saved
    dw2 = _matmul(act, dout, ta=True, name=f"mlp_dw2_{tag}")
    dz = _matmul(dout, w2, tb=True, name=f"mlp_dact_{tag}", extras=(z,), epilogue=_ep_relu2_bwd,
                 out_dtypes=(BF16,))
    dw1 = _matmul(hn, dz, ta=True, name=f"mlp_dw1_{tag}")
    dhn = _matmul(dz, w1, tb=True, name=f"mlp_dhn_{tag}")
    dh, dg = _rmsnorm_bwd(dhn, h, g, dout, name=f"mlp_norm_bwd_{tag}")
    return dh, dg, dw1, dw2


def _local_step(x, target, W):
    B, T, D = x.shape
    N = B * T
    G = {}
    row = lambda vec: vec.reshape(1, -1)
    PW = W["pool_w"].shape[0] * W["pool_w"].shape[1]
    SW = W["sgu_norm_g"].shape[-1]
    H = (W["cd_w_in"].shape[1] - 2 * SW) // (3 * SB_DH)

    h0 = x.reshape(N, D)
    xn0 = _rmsnorm(h0, row(W["mix_norm_g"][0]), name="mix_norm_0")
    p0 = _matmul(xn0, W["ab_w_in"], name="ab_in")
    a_out = _pool_fwd(p0, W["pool_w"], W["pool_scale"], B, T, name="pool_fwd")
    b_out = _conv_fwd(p0, W["conv_w"], row(W["conv_b"]), B, T, PW, name="conv_fwd")
    cat0 = jnp.concatenate([a_out, b_out], axis=1)
    h1 = _matmul(cat0, W["ab_w_out"], name="ab_out", extras=(h0,), epilogue=_ep_add)
    h2, mlp0 = _mlp_fwd(h1, row(W["mlp_norm_g"][0]), W["mlp_w1"][0], W["mlp_w2"][0], 0)
    xn1 = _rmsnorm(h2, row(W["mix_norm_g"][1]), name="mix_norm_1")
    p1 = _matmul(xn1, W["cd_w_in"], name="cd_in")
    ln_g, ln_b = row(W["sgu_norm_g"]), row(W["sgu_norm_b"])
    c_out = _sgu_fwd(p1, ln_g, ln_b, W["sgu_w"], W["sgu_b"], name="sgu_fwd")
    qkv = _heads_in(p1[:, 2 * SW:], B, T, H)
    d_out = _heads_out(_attn_fwd(qkv[0], qkv[1], qkv[2], name="attn_fwd"), B, T, H)
    cat1 = jnp.concatenate([c_out, d_out], axis=1)
    h3 = _matmul(cat1, W["cd_w_out"], name="cd_out", extras=(h2,), epilogue=_ep_add)
    h4, mlp1 = _mlp_fwd(h3, row(W["mlp_norm_g"][1]), W["mlp_w1"][1], W["mlp_w2"][1], 1)

    loss, dh4, G["final_norm_g"] = _final_loss(h4, row(W["final_norm_g"]), target.reshape(N, D), name="final_loss")

    dh3, dmlp_g1, dw1_1, dw2_1 = _mlp_bwd(dh4, h3, row(W["mlp_norm_g"][1]), W["mlp_w1"][1], W["mlp_w2"][1], mlp1, 1)
    G["cd_w_out"] = _matmul(cat1, dh3, ta=True, name="cd_out_dw")[None]
    dcat1 = _matmul(dh3, W["cd_w_out"], tb=True, name="cd_out_dx")
    dpu, dpv, G["sgu_w"], dsgu_b, G["sgu_norm_g"], G["sgu_norm_b"] = _sgu_bwd(
        p1, dcat1, ln_g, ln_b, W["sgu_w"], W["sgu_b"], name="sgu_bwd")
    G["sgu_b"] = dsgu_b.reshape(W["sgu_b"].shape)
    do = dcat1[:, SW:].reshape(B, T, H, SB_DH).transpose(0, 2, 1, 3).reshape(B * H, T, SB_DH).astype(BF16)
    dq, dk, dv = _attn_bwd(qkv[0], qkv[1], qkv[2], do, name="attn_bwd")
    dqkv = jnp.concatenate([_heads_out(t, B, T, H) for t in (dq, dk, dv)], axis=1).astype(BF16)
    dp1 = jnp.concatenate([dpu, dpv, dqkv], axis=1)
    G["cd_w_in"] = _matmul(xn1, dp1, ta=True, name="cd_in_dw")[None]
    dxn1 = _matmul(dp1, W["cd_w_in"], tb=True, name="cd_in_dx")
    dh2, dmix_g1 = _rmsnorm_bwd(dxn1, h2, row(W["mix_norm_g"][1]), dh3, name="mix_norm_bwd_1")

    dh1, dmlp_g0, dw1_0, dw2_0 = _mlp_bwd(dh2, h1, row(W["mlp_norm_g"][0]), W["mlp_w1"][0], W["mlp_w2"][0], mlp0, 0)
    G["ab_w_out"] = _matmul(cat0, dh1, ta=True, name="ab_out_dw")[None]
    dcat0 = _matmul(dh1, W["ab_w_out"], tb=True, name="ab_out_dx")
    da, G["pool_w"], dps = _pool_bwd(p0, dcat0, W["pool_w"], W["pool_scale"], B, T, name="pool_bwd")
    G["pool_scale"] = dps.reshape(W["pool_scale"].shape)
    dxb, dgb, dgc, G["conv_w"], dcb = _conv_bwd(p0, dcat0, W["conv_w"], row(W["conv_b"]), B, T, PW, PW,
                                                name="conv_bwd")
    G["conv_b"] = dcb.reshape(-1)
    dp0 = jnp.concatenate([da, dxb, dgb, dgc], axis=1)
    G["ab_w_in"] = _matmul(xn0, dp0, ta=True, name="ab_in_dw")[None]
    dxn0 = _matmul(dp0, W["ab_w_in"], tb=True, name="ab_in_dx")
    dx, dmix_g0 = _rmsnorm_bwd(dxn0, h0, row(W["mix_norm_g"][0]), dh1, name="mix_norm_bwd_0")

    G["mix_norm_g"] = jnp.concatenate([dmix_g0, dmix_g1], axis=0)
    G["mlp_norm_g"] = jnp.concatenate([dmlp_g0, dmlp_g1], axis=0)
    G["mlp_w1"] = jnp.stack([dw1_0, dw1_1])
    G["mlp_w2"] = jnp.stack([dw2_0, dw2_1])
    G["final_norm_g"] = G["final_norm_g"].reshape(-1)
    G["sgu_norm_g"] = G["sgu_norm_g"].reshape(-1)
    G["sgu_norm_b"] = G["sgu_norm_b"].reshape(-1)
    return loss[0, 0], dx.reshape(B, T, D), G


_NAMES = ["mix_norm_g", "mlp_norm_g", "ab_w_in", "pool_w", "pool_scale", "conv_w", "conv_b", "ab_w_out",
          "cd_w_in", "sgu_norm_g", "sgu_norm_b", "sgu_w", "sgu_b", "cd_w_out", "mlp_w1", "mlp_w2",
          "final_norm_g"]
_COL_SHARDED = ["ab_w_in", "cd_w_in", "mlp_w1"]
_ROW_SHARDED = ["ab_w_out", "cd_w_out", "mlp_w2"]
_SMALL_SHARDED = ["conv_w", "sgu_norm_g", "sgu_norm_b"]
_REPLICATED = ["mix_norm_g", "mlp_norm_g", "pool_w", "pool_scale", "conv_b", "sgu_w", "sgu_b", "final_norm_g"]


def _pad_rows(a2d, mult=SUBLANES):
    pad = (-a2d.shape[0]) % mult
    return jnp.pad(a2d, ((0, pad), (0, 0))) if pad else a2d


def _pack(arrays):
    return _pad_rows(jnp.concatenate([a.reshape(-1, LANES) for a in arrays], axis=0))


def _unpack(packed, shapes):
    out, r = [], 0
    for s in shapes:
        n = math.prod(s) // LANES
        out.append(packed[r:r + n].reshape(s))
        r += n
    return out


def _small_shard_pack(arrays):
    rows = jnp.concatenate([a.reshape(-1, a.shape[-1]) for a in arrays], axis=0)
    return _pad_rows(jnp.pad(rows, ((0, 0), (0, LANES - rows.shape[1]))))


def _cols_to_chunks(a):
    n = a.shape[-1] // N_DEV
    return jnp.moveaxis(a.reshape(a.shape[:-1] + (N_DEV, n)), -2, 0)


def _chunks_to_cols(a):
    t = jnp.moveaxis(a, 0, -2)
    return t.reshape(t.shape[:-2] + (t.shape[-2] * t.shape[-1],))


def _rows_to_chunks(a):
    r = a.shape[-2] // N_DEV
    return jnp.moveaxis(a.reshape(a.shape[:-2] + (N_DEV, r, a.shape[-1])), -3, 0)


def _chunks_to_rows(a):
    t = jnp.moveaxis(a, 0, -3)
    return t.reshape(t.shape[:-3] + (t.shape[-3] * t.shape[-2], t.shape[-1]))


def kernel(x, mix_norm_g, mlp_norm_g, ab_w_in, pool_w, pool_scale, conv_w, conv_b, ab_w_out, cd_w_in, sgu_norm_g, sgu_norm_b, sgu_w, sgu_b, cd_w_out, mlp_w1, mlp_w2, final_norm_g, loss_target, m_mix_norm_g, m_mlp_norm_g, m_ab_w_in, m_pool_w, m_pool_scale, m_conv_w, m_conv_b, m_ab_w_out, m_cd_w_in, m_sgu_norm_g, m_sgu_norm_b, m_sgu_w, m_sgu_b, m_cd_w_out, m_mlp_w1, m_mlp_w2, m_final_norm_g, v_mix_norm_g, v_mlp_norm_g, v_ab_w_in, v_pool_w, v_pool_scale, v_conv_w, v_conv_b, v_ab_w_out, v_cd_w_in, v_sgu_norm_g, v_sgu_norm_b, v_sgu_w, v_sgu_b, v_cd_w_out, v_mlp_w1, v_mlp_w2, v_final_norm_g):
    w = dict(zip(_NAMES, (mix_norm_g, mlp_norm_g, ab_w_in, pool_w, pool_scale, conv_w, conv_b, ab_w_out, cd_w_in,
                          sgu_norm_g, sgu_norm_b, sgu_w, sgu_b, cd_w_out, mlp_w1, mlp_w2, final_norm_g)))
    m = dict(zip(_NAMES, (m_mix_norm_g, m_mlp_norm_g, m_ab_w_in, m_pool_w, m_pool_scale, m_conv_w, m_conv_b,
                          m_ab_w_out, m_cd_w_in, m_sgu_norm_g, m_sgu_norm_b, m_sgu_w, m_sgu_b, m_cd_w_out,
                          m_mlp_w1, m_mlp_w2, m_final_norm_g)))
    v = dict(zip(_NAMES, (v_mix_norm_g, v_mlp_norm_g, v_ab_w_in, v_pool_w, v_pool_scale, v_conv_w, v_conv_b,
                          v_ab_w_out, v_cd_w_in, v_sgu_norm_g, v_sgu_norm_b, v_sgu_w, v_sgu_b, v_cd_w_out,
                          v_mlp_w1, v_mlp_w2, v_final_norm_g)))
    big = _COL_SHARDED + _ROW_SHARDED
    me = _my_index()

    small_sh = _small_shard_pack([w[n] for n in _SMALL_SHARDED])
    gathered = _exchange([w[n].astype(BF16) for n in big] + [small_sh], [True] * (len(big) + 1),
                         name="gather_weights")
    W = {}
    for n, g in zip(big, gathered):
        W[n] = _chunks_to_cols(g) if n in _COL_SHARDED else _chunks_to_rows(g)
    for n in ("ab_w_in", "ab_w_out", "cd_w_in", "cd_w_out"):
        W[n] = W[n][0]
    small_full = gathered[-1]
    r = 0
    for n in _SMALL_SHARDED:
        rows, width = math.prod(w[n].shape[:-1]), w[n].shape[-1]
        W[n] = _chunks_to_cols(small_full[:, r:r + rows, :width])
        r += rows
    for n in _REPLICATED:
        W[n] = w[n]
    for n in ("pool_w", "pool_scale", "sgu_w", "sgu_b"):
        W[n] = W[n][0]

    loss_part, grad_x, G = _local_step(x, loss_target, W)

    parts = []
    for n in big:
        parts.append(_cols_to_chunks(G[n]) if n in _COL_SHARDED else _rows_to_chunks(G[n]))
    small_names = _REPLICATED + _SMALL_SHARDED
    small_grads = [G[n].reshape(-1) for n in small_names]
    loss_row = jnp.full((LANES,), loss_part, F32)
    small_pack = _pack(small_grads + [loss_row])
    exchanged = _exchange(parts + [small_pack], [False] * len(big) + [True], name="exchange_grads")

    grads, deltas, new_m, new_v = {}, {}, {}, {}
    for n, gp in zip(big, exchanged):
        shp = w[n].shape
        flat = (math.prod(shp[:-1]), shp[-1])
        outs = _adamw(w[n].reshape(flat), gp.reshape((N_DEV,) + flat), m[n].reshape(flat), v[n].reshape(flat),
                      name=f"adamw_{n}")
        grads[n], deltas[n], new_m[n], new_v[n] = [o.reshape(shp) for o in outs]

    rep_shapes = [w[n].shape for n in _REPLICATED]
    rep_rows = sum(math.prod(s) for s in rep_shapes) // LANES
    small_sum_shapes = [(G[n].size,) for n in small_names] + [(LANES,)]
    zero_tail = [jnp.zeros((math.prod(s),), F32) for s in small_sum_shapes[len(_REPLICATED):]]
    w_pack = _pack([w[n] for n in _REPLICATED] + zero_tail)
    m_pack = _pack([m[n] for n in _REPLICATED] + zero_tail)
    v_pack = _pack([v[n] for n in _REPLICATED] + zero_tail)
    outs = _adamw(w_pack, exchanged[-1], m_pack, v_pack, name="adamw_small")
    summed = _unpack(outs[0], small_sum_shapes)
    for i, n in enumerate(_REPLICATED):
        grads[n] = summed[i].reshape(w[n].shape)
    for dst, o in zip((deltas, new_m, new_v), outs[1:]):
        for n, val in zip(_REPLICATED, _unpack(o[:rep_rows], rep_shapes)):
            dst[n] = val
    loss = summed[-1][0]

    shard_g = []
    for i, n in enumerate(_SMALL_SHARDED):
        full = summed[len(_REPLICATED) + i].reshape(w[n].shape[:-1] + (-1,))
        width = w[n].shape[-1]
        shard_g.append(lax.dynamic_slice_in_dim(full, me * width, width, axis=full.ndim - 1))
    g_sh = _small_shard_pack(shard_g)
    m_sh = _small_shard_pack([m[n] for n in _SMALL_SHARDED])
    v_sh = _small_shard_pack([v[n] for n in _SMALL_SHARDED])
    outs = _adamw(small_sh, g_sh[None], m_sh, v_sh, name="adamw_small_sharded")
    r = 0
    for n in _SMALL_SHARDED:
        rows, width = math.prod(w[n].shape[:-1]), w[n].shape[-1]
        for dst, o in zip((grads, deltas, new_m, new_v), outs):
            dst[n] = o[r:r + rows, :width].reshape(w[n].shape)
        r += rows

    return (loss, grad_x, *[grads[n] for n in _NAMES], *[deltas[n] for n in _NAMES],
            *[new_m[n] for n in _NAMES], *[new_v[n] for n in _NAMES])
```

```python
import functools
import math

import jax
import jax.numpy as jnp
from jax import lax
from jax.experimental import pallas as pl
from jax.experimental.pallas import tpu as pltpu

F32 = jnp.float32
BF16 = jnp.bfloat16
GRAD_WIRE = jnp.bfloat16

NORM_EPS = 1e-6
ADAM_LR = 0.001
ADAM_B1 = 0.9
ADAM_B2 = 0.999
ADAM_EPS = 1e-08
ADAM_WD = 0.01
ADAM_STEP = 10
ADAM_C1 = 1.0 - ADAM_B1 ** ADAM_STEP
ADAM_C2 = 1.0 - ADAM_B2 ** ADAM_STEP

N_DEV = 8
LANES = 128
SUBLANES = 8
SB_DH = 64
ATT_BLOCK = 256
POOL_LOG_WINDOWS = 4
VMEM_LIMIT = 56 * 1024 * 1024


def _params(semantics=None):
    return pltpu.CompilerParams(dimension_semantics=semantics, vmem_limit_bytes=VMEM_LIMIT)


def _tile(dim, pref, unit=LANES):
    if dim <= pref:
        return dim
    t = (pref // unit) * unit
    while t >= unit:
        if dim % t == 0:
            return t
        t -= unit
    return dim


def _matmul(a, b, *, name, ta=False, tb=False, extras=(), epilogue=None, out_dtypes=(F32,),
            tm=512, tn=512, tk=1024):
    M, K = (a.shape[1], a.shape[0]) if ta else a.shape
    N = b.shape[0] if tb else b.shape[1]
    assert (b.shape[1] if tb else b.shape[0]) == K, (a.shape, b.shape)
    tm, tn, tk = _tile(M, tm), _tile(N, tn), _tile(K, tk)
    nk = K // tk
    dims = (((0 if ta else 1,), (1 if tb else 0,)), ((), ()))
    ne, no = len(extras), len(out_dtypes)

    def body(*refs):
        a_ref, b_ref = refs[0], refs[1]
        e_refs = refs[2:2 + ne]
        o_refs = refs[2 + ne:2 + ne + no]
        k = pl.program_id(2)

        def part():
            return lax.dot_general(a_ref[...].astype(BF16), b_ref[...].astype(BF16), dims,
                                   preferred_element_type=F32)

        def finish(acc):
            outs = epilogue(acc, *[e[...] for e in e_refs]) if epilogue is not None else (acc,)
            for o_ref, val in zip(o_refs, outs):
                o_ref[...] = val.astype(o_ref.dtype)

        if nk == 1:
            finish(part())
        else:
            acc_ref = refs[-1]

            @pl.when(k == 0)
            def _():
                acc_ref[...] = jnp.zeros_like(acc_ref)

            acc_ref[...] += part()

            @pl.when(k == nk - 1)
            def _():
                finish(acc_ref[...])

    a_spec = (pl.BlockSpec((tk, tm), lambda i, j, k: (k, i)) if ta
              else pl.BlockSpec((tm, tk), lambda i, j, k: (i, k)))
    b_spec = (pl.BlockSpec((tn, tk), lambda i, j, k: (j, k)) if tb
              else pl.BlockSpec((tk, tn), lambda i, j, k: (k, j)))
    o_spec = pl.BlockSpec((tm, tn), lambda i, j, k: (i, j))
    outs = pl.pallas_call(
        body,
        name=name,
        grid=(M // tm, N // tn, nk),
        in_specs=[a_spec, b_spec] + [o_spec] * ne,
        out_specs=[o_spec] * no,
        out_shape=[jax.ShapeDtypeStruct((M, N), dt) for dt in out_dtypes],
        scratch_shapes=[pltpu.VMEM((tm, tn), F32)] if nk > 1 else [],
        compiler_params=_params(("parallel", "parallel", "arbitrary")),
    )(a, b, *extras)
    return outs[0] if no == 1 else outs


def _ep_add(acc, res):
    return (acc + res,)


def _ep_relu2(acc):
    r = jnp.maximum(acc, 0.0)
    return acc, r * r


def _ep_relu2_bwd(acc, z):
    return (acc * (2.0 * jnp.maximum(z, 0.0)),)


def _rstd(x):
    return lax.rsqrt(jnp.mean(x * x, axis=-1, keepdims=True) + NORM_EPS)


def _rmsnorm(h, g, *, name, tr=512):
    N, D = h.shape
    tr = _tile(N, tr, SUBLANES)

    def body(h_ref, g_ref, o_ref):
        x = h_ref[...]
        o_ref[...] = ((x * _rstd(x)) * g_ref[...]).astype(o_ref.dtype)

    row = pl.BlockSpec((tr, D), lambda i: (i, 0))
    vec = pl.BlockSpec((1, D), lambda i: (0, 0))
    return pl.pallas_call(
        body, name=name, grid=(N // tr,), in_specs=[row, vec], out_specs=row,
        out_shape=jax.ShapeDtypeStruct((N, D), BF16), compiler_params=_params(("parallel",)),
    )(h, g)


def _rmsnorm_bwd(dy, h, g, dres, *, name, tr=512):
    N, D = h.shape
    tr = _tile(N, tr, SUBLANES)

    def body(dy_ref, h_ref, g_ref, r_ref, dh_ref, dg_ref):
        i = pl.program_id(0)
        x = h_ref[...]
        d = dy_ref[...]
        r = _rstd(x)
        xh = x * r

        @pl.when(i == 0)
        def _():
            dg_ref[...] = jnp.zeros_like(dg_ref)

        dg_ref[...] += jnp.sum(d * xh, axis=0, keepdims=True)
        dxh = d * g_ref[...]
        dh_ref[...] = r_ref[...] + r * (dxh - xh * jnp.mean(dxh * xh, axis=-1, keepdims=True))

    row = pl.BlockSpec((tr, D), lambda i: (i, 0))
    vec = pl.BlockSpec((1, D), lambda i: (0, 0))
    return pl.pallas_call(
        body, name=name, grid=(N // tr,), in_specs=[row, row, vec, row], out_specs=[row, vec],
        out_shape=[jax.ShapeDtypeStruct((N, D), F32), jax.ShapeDtypeStruct((1, D), F32)],
        compiler_params=_params(("arbitrary",)),
    )(dy, h, g, dres)


def _final_loss(h, g, target, *, name, tr=512):
    N, D = h.shape
    tr = _tile(N, tr, SUBLANES)

    def body(h_ref, g_ref, t_ref, loss_ref, dh_ref, dg_ref):
        i = pl.program_id(0)
        x = h_ref[...]
        gg = g_ref[...]
        r = _rstd(x)
        xh = x * r
        err = xh * gg - t_ref[...]

        @pl.when(i == 0)
        def _():
            dg_ref[...] = jnp.zeros_like(dg_ref)
            loss_ref[...] = jnp.zeros_like(loss_ref)

        per_row = jnp.mean(err * err, axis=-1, keepdims=True)
        loss_ref[...] += 0.5 * jnp.sum(per_row, axis=0, keepdims=True)
        dy = err * (1.0 / D)
        dg_ref[...] += jnp.sum(dy * xh, axis=0, keepdims=True)
        dxh = dy * gg
        dh_ref[...] = r * (dxh - xh * jnp.mean(dxh * xh, axis=-1, keepdims=True))

    row = pl.BlockSpec((tr, D), lambda i: (i, 0))
    vec = pl.BlockSpec((1, D), lambda i: (0, 0))
    lvec = pl.BlockSpec((1, LANES), lambda i: (0, 0))
    return pl.pallas_call(
        body, name=name, grid=(N // tr,), in_specs=[row, vec, row], out_specs=[lvec, row, vec],
        out_shape=[jax.ShapeDtypeStruct((1, LANES), F32), jax.ShapeDtypeStruct((N, D), F32),
                   jax.ShapeDtypeStruct((1, D), F32)],
        compiler_params=_params(("arbitrary",)),
    )(h, g, target)


def _shift_down(x, s):
    t = lax.broadcasted_iota(jnp.int32, x.shape, 0)
    return jnp.where(t >= s, pltpu.roll(x, s, 0), 0.0)


def _shift_up(x, s):
    n = x.shape[0]
    t = lax.broadcasted_iota(jnp.int32, x.shape, 0)
    return jnp.where(t < n - s, pltpu.roll(x, n - s, 0), 0.0)


def _window_sum(x, g, shift):
    s = x + shift(x, 1)
    for k in range(1, POOL_LOG_WINDOWS):
        s = jnp.where(k <= g, s + shift(s, 2 ** k), s)
    return s


def _pool_count(shape, g):
    t = lax.broadcasted_iota(jnp.int32, shape, 0)
    return jnp.minimum(t + 1, lax.shift_left(jnp.int32(2), g)).astype(F32)


def _pool_fwd(p, pool_w, pool_scale, B, T, *, name):
    G, dh = pool_w.shape[0], pool_w.shape[1]
    assert G == POOL_LOG_WINDOWS

    def body(a_ref, w_ref, s_ref, o_ref):
        g = pl.program_id(0)
        a = a_ref[...]
        pooled = _window_sum(a, g, _shift_down) / _pool_count(a.shape, g) - a
        m = jnp.dot(pooled.astype(BF16), w_ref[0].astype(BF16), preferred_element_type=F32)
        o_ref[...] = (m * s_ref[0]).astype(o_ref.dtype)

    blk = pl.BlockSpec((T, dh), lambda g, b: (b, g))
    return pl.pallas_call(
        body, name=name, grid=(G, B),
        in_specs=[blk, pl.BlockSpec((1, dh, dh), lambda g, b: (g, 0, 0)),
                  pl.BlockSpec((1, 1, dh), lambda g, b: (g, 0, 0))],
        out_specs=blk, out_shape=jax.ShapeDtypeStruct((B * T, G * dh), BF16),
        compiler_params=_params(("parallel", "parallel")),
    )(p, pool_w, pool_scale.reshape(G, 1, dh))


def _pool_bwd(p, dcat, pool_w, pool_scale, B, T, *, name):
    G, dh = pool_w.shape[0], pool_w.shape[1]

    def body(a_ref, d_ref, w_ref, s_ref, da_ref, dw_ref, ds_ref):
        g = pl.program_id(0)
        b = pl.program_id(1)
        a = a_ref[...]
        d = d_ref[...]
        cnt = _pool_count(a.shape, g)
        pooled = (_window_sum(a, g, _shift_down) / cnt - a).astype(BF16)
        w = w_ref[0].astype(BF16)
        m = jnp.dot(pooled, w, preferred_element_type=F32)

        @pl.when(b == 0)
        def _():
            dw_ref[...] = jnp.zeros_like(dw_ref)
            ds_ref[...] = jnp.zeros_like(ds_ref)

        ds_ref[0] += jnp.sum(d * m, axis=0, keepdims=True)
        dm = (d * s_ref[0]).astype(BF16)
        dw_ref[0] += lax.dot_general(pooled, dm, (((0,), (0,)), ((), ())), preferred_element_type=F32)
        dpooled = lax.dot_general(dm, w, (((1,), (1,)), ((), ())), preferred_element_type=F32)
        da = _window_sum(dpooled / cnt, g, _shift_up) - dpooled
        da_ref[...] = da.astype(da_ref.dtype)

    blk = pl.BlockSpec((T, dh), lambda g, b: (b, g))
    wspec = pl.BlockSpec((1, dh, dh), lambda g, b: (g, 0, 0))
    sspec = pl.BlockSpec((1, 1, dh), lambda g, b: (g, 0, 0))
    return pl.pallas_call(
        body, name=name, grid=(G, B), in_specs=[blk, blk, wspec, sspec], out_specs=[blk, wspec, sspec],
        out_shape=[jax.ShapeDtypeStruct((B * T, G * dh), BF16), jax.ShapeDtypeStruct((G, dh, dh), F32),
                   jax.ShapeDtypeStruct((G, 1, dh), F32)],
        compiler_params=_params(("parallel", "arbitrary")),
    )(p, dcat, pool_w, pool_scale.reshape(G, 1, dh))


def _conv_specs(T, tc, off, CW):
    def at(k):
        base = (off + k * CW) // tc
        return pl.BlockSpec((T, tc), lambda j, b: (b, base + j))
    return [at(0), at(1), at(2)]


def _conv_fwd(p, conv_w, conv_b, B, T, off, *, name):
    CW = conv_w.shape[1]
    tc = LANES
    assert off % tc == 0 and CW % tc == 0

    def body(xb_ref, gb_ref, gc_ref, w_ref, b_ref, o_ref):
        c = gc_ref[...] * xb_ref[...]
        w = w_ref[...]
        y = _shift_down(c, 2) * w[0:1] + _shift_down(c, 1) * w[1:2] + c * w[2:3] + b_ref[...]
        o_ref[...] = (gb_ref[...] * y).astype(o_ref.dtype)

    return pl.pallas_call(
        body, name=name, grid=(CW // tc, B),
        in_specs=_conv_specs(T, tc, off, CW) + [pl.BlockSpec((3, tc), lambda j, b: (0, j)),
                                                 pl.BlockSpec((1, tc), lambda j, b: (0, j))],
        out_specs=pl.BlockSpec((T, tc), lambda j, b: (b, j)),
        out_shape=jax.ShapeDtypeStruct((B * T, CW), BF16),
        compiler_params=_params(("parallel", "parallel")),
    )(p, p, p, conv_w, conv_b)


def _conv_bwd(p, dcat, conv_w, conv_b, B, T, off, doff, *, name):
    CW = conv_w.shape[1]
    tc = LANES
    assert doff % tc == 0

    def body(xb_ref, gb_ref, gc_ref, d_ref, w_ref, b_ref, dxb_ref, dgb_ref, dgc_ref, dw_ref, db_ref):
        b = pl.program_id(1)
        xb, gb, gc, d = xb_ref[...], gb_ref[...], gc_ref[...], d_ref[...]
        w = w_ref[...]
        c = gc * xb
        c1 = _shift_down(c, 1)
        c2 = _shift_down(c, 2)
        y = c2 * w[0:1] + c1 * w[1:2] + c * w[2:3] + b_ref[...]
        dy = d * gb
        dgb_ref[...] = (d * y).astype(dgb_ref.dtype)

        @pl.when(b == 0)
        def _():
            dw_ref[...] = jnp.zeros_like(dw_ref)
            db_ref[...] = jnp.zeros_like(db_ref)

        db_ref[...] += jnp.sum(dy, axis=0, keepdims=True)
        dw_ref[0:1, :] += jnp.sum(dy * c2, axis=0, keepdims=True)
        dw_ref[1:2, :] += jnp.sum(dy * c1, axis=0, keepdims=True)
        dw_ref[2:3, :] += jnp.sum(dy * c, axis=0, keepdims=True)
        dc = dy * w[2:3] + _shift_up(dy, 1) * w[1:2] + _shift_up(dy, 2) * w[0:1]
        dgc_ref[...] = (dc * xb).astype(dgc_ref.dtype)
        dxb_ref[...] = (dc * gc).astype(dxb_ref.dtype)

    dbase = doff // tc
    wspec = pl.BlockSpec((3, tc), lambda j, b: (0, j))
    bspec = pl.BlockSpec((1, tc), lambda j, b: (0, j))

    oblk = pl.BlockSpec((T, tc), lambda j, b: (b, j))
    act = jax.ShapeDtypeStruct((B * T, CW), BF16)

    return pl.pallas_call(
        body, name=name, grid=(CW // tc, B),
        in_specs=_conv_specs(T, tc, off, CW) + [pl.BlockSpec((T, tc), lambda j, b: (b, dbase + j)), wspec, bspec],
        out_specs=[oblk, oblk, oblk, wspec, bspec],
        out_shape=[act, act, act, jax.ShapeDtypeStruct((3, CW), F32), jax.ShapeDtypeStruct((1, CW), F32)],
        compiler_params=_params(("parallel", "arbitrary")),
    )(p, p, p, dcat, conv_w, conv_b)


_SQRT_HALF = 0.7071067811865476
_INV_SQRT_2PI = 0.3989422804014327


def _gelu(x):
    return x * (lax.erf(x * _SQRT_HALF) + 1.0) * 0.5


def _gelu_grad(x):
    return 0.5 * (lax.erf(x * _SQRT_HALF) + 1.0) + x * (_INV_SQRT_2PI * jnp.exp(-0.5 * x * x))


def _layernorm_parts(v):
    mu = jnp.mean(v, axis=-1, keepdims=True)
    vc = v - mu
    rstd = lax.rsqrt(jnp.mean(vc * vc, axis=-1, keepdims=True) + NORM_EPS)
    return vc * rstd, rstd


def _tril_mask(L):
    r = lax.broadcasted_iota(jnp.int32, (L, L), 0)
    c = lax.broadcasted_iota(jnp.int32, (L, L), 1)
    return r >= c


def _sgu_fwd(p, ln_g, ln_b, sgu_w, sgu_b, *, name, tr=512):
    N = p.shape[0]
    G, L = sgu_w.shape[0], sgu_w.shape[1]
    SW = ln_g.shape[1]
    dh = SW // G
    tr = _tile(N, tr, L)
    assert tr % L == 0

    def body(u_ref, v_ref, g_ref, beta_ref, w_ref, b_ref, o_ref):
        u = _gelu(u_ref[...])
        vhat, _ = _layernorm_parts(_gelu(v_ref[...]))
        vn = (vhat * g_ref[...] + beta_ref[...]).astype(BF16)
        mask = _tril_mask(L)
        for gi in range(G):
            w = jnp.where(mask, w_ref[gi], 0.0).astype(BF16)
            bias = b_ref[gi]
            cols = slice(gi * dh, (gi + 1) * dh)
            for n in range(tr // L):
                rows = slice(n * L, (n + 1) * L)
                s = jnp.dot(w, vn[rows, cols], preferred_element_type=F32) + bias
                o_ref[rows, cols] = (u[rows, cols] * s).astype(o_ref.dtype)

    def col(k):
        return pl.BlockSpec((tr, SW), lambda i: (i, k))

    vec = pl.BlockSpec((1, SW), lambda i: (0, 0))
    return pl.pallas_call(
        body, name=name, grid=(N // tr,),
        in_specs=[col(0), col(1), vec, vec, pl.BlockSpec((G, L, L), lambda i: (0, 0, 0)),
                  pl.BlockSpec((G, L, 1), lambda i: (0, 0, 0))],
        out_specs=col(0), out_shape=jax.ShapeDtypeStruct((N, SW), BF16),
        compiler_params=_params(("parallel",)),
    )(p, p, ln_g, ln_b, sgu_w, sgu_b.reshape(G, L, 1))


def _sgu_bwd(p, dcat, ln_g, ln_b, sgu_w, sgu_b, *, name, tr=512):
    N = p.shape[0]
    G, L = sgu_w.shape[0], sgu_w.shape[1]
    SW = ln_g.shape[1]
    dh = SW // G
    tr = _tile(N, tr, L)

    def body(u_ref, v_ref, dc_ref, g_ref, beta_ref, w_ref, b_ref,
             du_ref, dv_ref, dw_ref, db_ref, dg_ref, dbeta_ref, du_s, dvn_s):
        i = pl.program_id(0)
        pu = u_ref[...]
        pv = v_ref[...]
        u = _gelu(pu)
        vhat, rstd = _layernorm_parts(_gelu(pv))
        gg = g_ref[...]
        vn = (vhat * gg + beta_ref[...]).astype(BF16)
        dc = dc_ref[...]
        mask = _tril_mask(L)

        @pl.when(i == 0)
        def _():
            dw_ref[...] = jnp.zeros_like(dw_ref)
            db_ref[...] = jnp.zeros_like(db_ref)
            dg_ref[...] = jnp.zeros_like(dg_ref)
            dbeta_ref[...] = jnp.zeros_like(dbeta_ref)

        for gi in range(G):
            w = jnp.where(mask, w_ref[gi], 0.0).astype(BF16)
            bias = b_ref[gi]
            cols = slice(gi * dh, (gi + 1) * dh)
            dw_acc = jnp.zeros((L, L), F32)
            db_acc = jnp.zeros((L, 1), F32)
            for n in range(tr // L):
                rows = slice(n * L, (n + 1) * L)
                vb = vn[rows, cols]
                s = jnp.dot(w, vb, preferred_element_type=F32) + bias
                du_s[rows, cols] = dc[rows, cols] * s
                ds = dc[rows, cols] * u[rows, cols]
                db_acc += jnp.sum(ds, axis=1, keepdims=True)
                dsb = ds.astype(BF16)
                dw_acc += lax.dot_general(dsb, vb, (((1,), (1,)), ((), ())), preferred_element_type=F32)
                dvn_s[rows, cols] = lax.dot_general(w, dsb, (((0,), (0,)), ((), ())),
                                                    preferred_element_type=F32)
            dw_ref[gi] += jnp.where(mask, dw_acc, 0.0)
            db_ref[gi] += db_acc

        dvn = dvn_s[...]
        dg_ref[...] += jnp.sum(dvn * vhat, axis=0, keepdims=True)
        dbeta_ref[...] += jnp.sum(dvn, axis=0, keepdims=True)
        dvh = dvn * gg
        dv = rstd * (dvh - jnp.mean(dvh, axis=-1, keepdims=True)
                     - vhat * jnp.mean(dvh * vhat, axis=-1, keepdims=True))
        dv_ref[...] = (dv * _gelu_grad(pv)).astype(dv_ref.dtype)
        du_ref[...] = (du_s[...] * _gelu_grad(pu)).astype(du_ref.dtype)

    def col(k):
        return pl.BlockSpec((tr, SW), lambda i: (i, k))

    vec = pl.BlockSpec((1, SW), lambda i: (0, 0))
    wspec = pl.BlockSpec((G, L, L), lambda i: (0, 0, 0))
    bspec = pl.BlockSpec((G, L, 1), lambda i: (0, 0, 0))
    return pl.pallas_call(
        body, name=name, grid=(N // tr,),
        in_specs=[col(0), col(1), col(0), vec, vec, wspec, bspec],
        out_specs=[col(0), col(0), wspec, bspec, vec, vec],
        out_shape=[jax.ShapeDtypeStruct((N, SW), BF16), jax.ShapeDtypeStruct((N, SW), BF16),
                   jax.ShapeDtypeStruct((G, L, L), F32), jax.ShapeDtypeStruct((G, L, 1), F32),
                   jax.ShapeDtypeStruct((1, SW), F32), jax.ShapeDtypeStruct((1, SW), F32)],
        scratch_shapes=[pltpu.VMEM((tr, SW), F32), pltpu.VMEM((tr, SW), F32)],
        compiler_params=_params(("arbitrary",)),
    )(p, p, dcat, ln_g, ln_b, sgu_w, sgu_b.reshape(G, L, 1))


def _log_sigmoid_pair(z):
    ls = jnp.minimum(z, 0.0) - jnp.log(1.0 + jnp.exp(-jnp.abs(z)))
    return ls, ls - z


def _split_dot(x, m):
    hi = x.astype(BF16)
    lo = (x - hi.astype(F32)).astype(BF16)
    return (jnp.dot(hi, m, preferred_element_type=F32) + jnp.dot(lo, m, preferred_element_type=F32))


def _att_tiles(TB):
    r = lax.broadcasted_iota(jnp.int32, (TB, TB), 0)
    c = lax.broadcasted_iota(jnp.int32, (TB, TB), 1)
    return r, c


def _att_weights(qt, kb, strict, later, carry):
    z = lax.dot_general(qt, kb, (((1,), (1,)), ((), ())), preferred_element_type=F32)
    ls, lk = _log_sigmoid_pair(z)
    if strict is not None:
        lk = jnp.where(strict, lk, 0.0)
    suffix = _split_dot(lk, later) + carry
    a = jnp.exp(ls + suffix)
    if strict is not None:
        a = jnp.where(strict, a, 0.0)
    return ls, lk, suffix, a


def _att_heads(BH):
    return 2 if BH % 2 == 0 else 1


def _attn_fwd(q, k, v, *, name):
    BH, T, dh = q.shape
    TB = _tile(T, ATT_BLOCK)
    nb = T // TB
    assert nb <= LANES
    HP = _att_heads(BH)

    def body(q_ref, k_ref, v_ref, o_ref, c_ref):
        r, c = _att_tiles(TB)
        strict = c < r
        later = (r > c).astype(BF16)
        lane = lax.broadcasted_iota(jnp.int32, (TB, LANES), 1)

        def tile(hh, qt, j, carry, mask):
            k0 = pl.multiple_of(j * TB, TB)
            _, lk, suffix, a = _att_weights(qt, k_ref[hh, pl.ds(k0, TB), :], mask, later, carry)
            pv = jnp.dot(a.astype(BF16), v_ref[hh, pl.ds(k0, TB), :], preferred_element_type=F32)
            return pv, suffix[:, 0:1] + lk[:, 0:1]

        def qblock(i, _):
            q0 = pl.multiple_of(i * TB, TB)
            qts = [q_ref[hh, pl.ds(q0, TB), :] for hh in range(HP)]
            state = []
            for hh in range(HP):
                pv, carry = tile(hh, qts[hh], i, jnp.zeros((TB, 1), F32), strict)
                state += [pv, carry, jnp.zeros((TB, LANES), F32)]

            def kblock(jj, st):
                j = i - jj
                out = []
                for hh in range(HP):
                    acc, carry, cm = st[3 * hh:3 * hh + 3]
                    pv, new_carry = tile(hh, qts[hh], j, carry, None)
                    out += [acc + pv, new_carry, jnp.where(lane == j, carry, cm)]
                return tuple(out)

            st = lax.fori_loop(1, i + 1, kblock, tuple(state))
            for hh in range(HP):
                o_ref[hh, pl.ds(q0, TB), :] = st[3 * hh].astype(o_ref.dtype)
                c_ref[hh, pl.ds(q0, TB), :] = st[3 * hh + 2]
            return 0

        lax.fori_loop(0, nb, qblock, 0)

    blk = pl.BlockSpec((HP, T, dh), lambda g: (g, 0, 0))
    cblk = pl.BlockSpec((HP, T, LANES), lambda g: (g, 0, 0))
    return pl.pallas_call(
        body, name=name, grid=(BH // HP,), in_specs=[blk, blk, blk], out_specs=[blk, cblk],
        out_shape=[jax.ShapeDtypeStruct((BH, T, dh), BF16), jax.ShapeDtypeStruct((BH, T, LANES), F32)],
        compiler_params=_params(("parallel",)),
    )(q, k, v)


def _attn_bwd(q, k, v, do, carries, *, name):
    BH, T, dh = q.shape
    TB = _tile(T, ATT_BLOCK)
    nb = T // TB
    HP = _att_heads(BH)

    def body(q_ref, k_ref, v_ref, do_ref, c_ref, dq_ref, dk_ref, dv_ref):
        r, c = _att_tiles(TB)
        strict = c < r
        later = (r > c).astype(BF16)
        earlier = (r < c).astype(BF16)
        lane = lax.broadcasted_iota(jnp.int32, (TB, LANES), 1)
        dk_ref[...] = jnp.zeros_like(dk_ref)
        dv_ref[...] = jnp.zeros_like(dv_ref)

        def tile(hh, qt, dot, cm, j, before, mask):
            k0 = pl.multiple_of(j * TB, TB)
            kb = k_ref[hh, pl.ds(k0, TB), :]
            carry = jnp.sum(jnp.where(lane == j, cm, 0.0), axis=1, keepdims=True)
            ls, _, _, a = _att_weights(qt, kb, mask, later, carry)
            dv_ref[hh, pl.ds(k0, TB), :] += lax.dot_general(a.astype(BF16), dot, (((0,), (0,)), ((), ())),
                                                            preferred_element_type=F32)
            da = lax.dot_general(dot, v_ref[hh, pl.ds(k0, TB), :], (((1,), (1,)), ((), ())),
                                 preferred_element_type=F32)
            gl = a * da
            prefix = _split_dot(gl, earlier) + before
            dz = gl - jnp.exp(ls) * (gl + prefix)
            if mask is not None:
                dz = jnp.where(mask, dz, 0.0)
            dzb = dz.astype(BF16)
            dk_ref[hh, pl.ds(k0, TB), :] += lax.dot_general(dzb, qt, (((0,), (0,)), ((), ())),
                                                            preferred_element_type=F32)
            return jnp.dot(dzb, kb, preferred_element_type=F32), prefix[:, TB - 1:TB] + gl[:, TB - 1:TB]

        def qblock(i, _):
            q0 = pl.multiple_of(i * TB, TB)
            qts = [q_ref[hh, pl.ds(q0, TB), :] for hh in range(HP)]
            dots = [do_ref[hh, pl.ds(q0, TB), :] for hh in range(HP)]
            cms = [c_ref[hh, pl.ds(q0, TB), :] for hh in range(HP)]

            def kblock(j, st):
                out = []
                for hh in range(HP):
                    dq, before = st[2 * hh:2 * hh + 2]
                    part, new_before = tile(hh, qts[hh], dots[hh], cms[hh], j, before, None)
                    out += [dq + part, new_before]
                return tuple(out)

            st = lax.fori_loop(0, i, kblock, (jnp.zeros((TB, dh), F32), jnp.zeros((TB, 1), F32)) * HP)
            for hh in range(HP):
                part, _ = tile(hh, qts[hh], dots[hh], cms[hh], i, st[2 * hh + 1], strict)
                dq_ref[hh, pl.ds(q0, TB), :] = st[2 * hh] + part
            return 0

        lax.fori_loop(0, nb, qblock, 0)

    blk = pl.BlockSpec((HP, T, dh), lambda g: (g, 0, 0))
    cblk = pl.BlockSpec((HP, T, LANES), lambda g: (g, 0, 0))
    shp = jax.ShapeDtypeStruct((BH, T, dh), F32)
    return pl.pallas_call(
        body, name=name, grid=(BH // HP,), in_specs=[blk] * 4 + [cblk], out_specs=[blk] * 3, out_shape=[shp] * 3,
        compiler_params=_params(("parallel",)),
    )(q, k, v, do, carries)


def _adamw(w, gparts, m, v, *, name):
    R, C = w.shape
    P = gparts.shape[0]
    tr = _tile(R, max(SUBLANES, (1 << 19) // (C * P)), SUBLANES)

    def body(w_ref, g_ref, m_ref, v_ref, go_ref, d_ref, mo_ref, vo_ref):
        g = g_ref[0].astype(F32)
        for i in range(1, P):
            g = g + g_ref[i].astype(F32)
        m2 = ADAM_B1 * m_ref[...] + (1.0 - ADAM_B1) * g
        v2 = ADAM_B2 * v_ref[...] + (1.0 - ADAM_B2) * (g * g)
        m_hat = m2 / ADAM_C1
        v_hat = v2 / ADAM_C2
        go_ref[...] = g
        d_ref[...] = -ADAM_LR * (m_hat / (jnp.sqrt(v_hat) + ADAM_EPS) + ADAM_WD * w_ref[...])
        mo_ref[...] = m2
        vo_ref[...] = v2

    row = pl.BlockSpec((tr, C), lambda i: (i, 0))
    shp = jax.ShapeDtypeStruct((R, C), F32)
    return pl.pallas_call(
        body, name=name, grid=(R // tr,),
        in_specs=[row, pl.BlockSpec((P, tr, C), lambda i: (0, i, 0)), row, row],
        out_specs=[row] * 4, out_shape=[shp] * 4, compiler_params=_params(("parallel",)),
    )(w, gparts, m, v)


def _my_index():
    return 4 * lax.axis_index("x") + 2 * lax.axis_index("y") + lax.axis_index("c")


def _exchange(arrs, gather, *, name):
    n = len(arrs)

    def body(*refs):
        ins, outs = refs[:n], refs[n:2 * n]
        send_sems, recv_sems, local_sems = refs[2 * n:]
        x, y, c = lax.axis_index("x"), lax.axis_index("y"), lax.axis_index("c")
        me = 4 * x + 2 * y + c
        remote, local = [], []
        for a in range(n):
            own = ins[a] if gather[a] else ins[a].at[me]
            cp = pltpu.make_async_copy(own, outs[a].at[me], local_sems.at[a])
            cp.start()
            local.append(cp)
            for k in range(1, N_DEV):
                px = 1 - x if k & 4 else x
                py = 1 - y if k & 2 else y
                pc = 1 - c if k & 1 else c
                src = ins[a] if gather[a] else ins[a].at[4 * px + 2 * py + pc]
                cp = pltpu.make_async_remote_copy(
                    src_ref=src, dst_ref=outs[a].at[me],
                    send_sem=send_sems.at[a, k - 1], recv_sem=recv_sems.at[a, k - 1],
                    device_id=(px, py, pc), device_id_type=pl.DeviceIdType.MESH)
                cp.start()
                remote.append(cp)
        for cp in remote:
            cp.wait()
        for cp in local:
            cp.wait()

    hbm = pl.BlockSpec(memory_space=pltpu.HBM)
    out_shape = [jax.ShapeDtypeStruct(((N_DEV,) + a.shape) if g else a.shape, a.dtype)
                 for a, g in zip(arrs, gather)]
    return pl.pallas_call(
        body, name=name, in_specs=[hbm] * n, out_specs=[hbm] * n, out_shape=out_shape,
        scratch_shapes=[pltpu.SemaphoreType.DMA((n, N_DEV - 1)), pltpu.SemaphoreType.DMA((n, N_DEV - 1)),
                        pltpu.SemaphoreType.DMA((n,))],
    )(*arrs)


_HBM = pl.BlockSpec(memory_space=pltpu.HBM)


def _other_chips(x, y):
    return [(1 - x, y), (x, 1 - y), (1 - x, 1 - y)]


def _gather_two_level(arrs, *, name):
    n = len(arrs)

    def body(*refs):
        ins, outs = refs[:n], refs[n:2 * n]
        send_sems, recv_sems, local_sems = refs[2 * n:]
        x, y, c = lax.axis_index("x"), lax.axis_index("y"), lax.axis_index("c")
        me, sibling = (x, y, c), (x, y, 1 - c)
        chips = _other_chips(x, y)

        def slot(a, px, py, pc):
            return outs[a].at[4 * px + 2 * py + pc]

        def copy(a, k, block, to, src=None):
            return pltpu.make_async_remote_copy(
                src_ref=slot(a, *block) if src is None else src, dst_ref=slot(a, *block),
                send_sem=send_sems.at[a, k], recv_sem=recv_sems.at[a, k],
                device_id=to, device_id_type=pl.DeviceIdType.MESH)

        local, sends = [], []
        for a in range(n):
            cp = pltpu.make_async_copy(ins[a], slot(a, *me), local_sems.at[a])
            cp.start()
            local.append(cp)
            first = [copy(a, 0, me, sibling, src=ins[a])]
            first += [copy(a, 1 + j, me, (*chip, c), src=ins[a]) for j, chip in enumerate(chips)]
            for cp in first:
                cp.start()
            sends += first
        for j, chip in enumerate(chips):
            for a in range(n):
                copy(a, 1 + j, (*chip, c), me).wait_recv()
                cp = copy(a, 4 + j, (*chip, c), sibling)
                cp.start()
                sends.append(cp)
        for a in range(n):
            copy(a, 0, sibling, me).wait_recv()
            for j, chip in enumerate(chips):
                copy(a, 4 + j, (*chip, 1 - c), me).wait_recv()
        for cp in sends:
            cp.wait_send()
        for cp in local:
            cp.wait()

    return pl.pallas_call(
        body, name=name, in_specs=[_HBM] * n, out_specs=[_HBM] * n,
        out_shape=[jax.ShapeDtypeStruct((N_DEV,) + a.shape, a.dtype) for a in arrs],
        scratch_shapes=[pltpu.SemaphoreType.DMA((n, N_DEV - 1)), pltpu.SemaphoreType.DMA((n, N_DEV - 1)),
                        pltpu.SemaphoreType.DMA((n,))],
    )(*arrs)


def _sibling_swap(arrs, *, name):
    n = len(arrs)
    nchip = N_DEV // 2

    def body(*refs):
        ins, outs = refs[:n], refs[n:2 * n]
        send_sems, recv_sems = refs[2 * n:]
        x, y, c = lax.axis_index("x"), lax.axis_index("y"), lax.axis_index("c")
        copies = []
        for a in range(n):
            for k in range(nchip):
                cp = pltpu.make_async_remote_copy(
                    src_ref=ins[a].at[2 * k + 1 - c], dst_ref=outs[a].at[k],
                    send_sem=send_sems.at[a, k], recv_sem=recv_sems.at[a, k],
                    device_id=(x, y, 1 - c), device_id_type=pl.DeviceIdType.MESH)
                cp.start()
                copies.append(cp)
        for cp in copies:
            cp.wait()

    return pl.pallas_call(
        body, name=name, in_specs=[_HBM] * n, out_specs=[_HBM] * n,
        out_shape=[jax.ShapeDtypeStruct((nchip,) + a.shape[1:], a.dtype) for a in arrs],
        scratch_shapes=[pltpu.SemaphoreType.DMA((n, nchip)), pltpu.SemaphoreType.DMA((n, nchip))],
    )(*arrs)


def _chip_exchange(arrs, *, name):
    n = len(arrs)

    def body(*refs):
        ins, outs = refs[:n], refs[n:2 * n]
        send_sems, recv_sems, local_sems = refs[2 * n:]
        x, y, c = lax.axis_index("x"), lax.axis_index("y"), lax.axis_index("c")
        mine = 2 * x + y
        copies = []
        for a in range(n):
            cp = pltpu.make_async_copy(ins[a].at[mine], outs[a].at[mine], local_sems.at[a])
            cp.start()
            copies.append(cp)
            for j, (px, py) in enumerate(_other_chips(x, y)):
                cp = pltpu.make_async_remote_copy(
                    src_ref=ins[a].at[2 * px + py], dst_ref=outs[a].at[mine],
                    send_sem=send_sems.at[a, j], recv_sem=recv_sems.at[a, j],
                    device_id=(px, py, c), device_id_type=pl.DeviceIdType.MESH)
                cp.start()
                copies.append(cp)
        for cp in copies:
            cp.wait()

    return pl.pallas_call(
        body, name=name, in_specs=[_HBM] * n, out_specs=[_HBM] * n,
        out_shape=[jax.ShapeDtypeStruct(a.shape, a.dtype) for a in arrs],
        scratch_shapes=[pltpu.SemaphoreType.DMA((n, 3)), pltpu.SemaphoreType.DMA((n, 3)),
                        pltpu.SemaphoreType.DMA((n,))],
    )(*arrs)


def _pair_sum(a, b, *, name):
    P, R, C = a.shape
    tr = _tile(R, max(SUBLANES, (1 << 19) // C), SUBLANES)

    def body(a_ref, b_ref, o_ref):
        o_ref[...] = (a_ref[...].astype(F32) + b_ref[...].astype(F32)).astype(o_ref.dtype)

    blk = pl.BlockSpec((1, tr, C), lambda p, i: (p, i, 0))
    return pl.pallas_call(
        body, name=name, grid=(P, R // tr), in_specs=[blk, blk], out_specs=blk,
        out_shape=jax.ShapeDtypeStruct(a.shape, a.dtype), compiler_params=_params(("parallel", "parallel")),
    )(a, b)


def _heads_in(cols, B, T, H):
    t = cols.reshape(B, T, 3, H, SB_DH).transpose(2, 0, 3, 1, 4).reshape(3, B * H, T, SB_DH)
    scale = jnp.array([1.0 / math.sqrt(SB_DH), 1.0, 1.0], F32).reshape(3, 1, 1, 1)
    return (t * scale).astype(BF16)


def _heads_out(t, B, T, H):
    return t.reshape(B, H, T, SB_DH).transpose(0, 2, 1, 3).reshape(B * T, H * SB_DH)


def _mlp_fwd(h, g, w1, w2, tag):
    hn = _rmsnorm(h, g, name=f"mlp_norm_{tag}")
    z, act = _matmul(hn, w1, name=f"mlp_up_{tag}", epilogue=_ep_relu2, out_dtypes=(F32, BF16))
    out = _matmul(act, w2, name=f"mlp_down_{tag}", extras=(h,), epilogue=_ep_add)
    return out, (hn, z, act)


def _mlp_bwd(dout, h, g, w1, w2, saved, tag):
    hn, z, act = saved
    dw2 = _matmul(act, dout, ta=True, name=f"mlp_dw2_{tag}", out_dtypes=(GRAD_WIRE,))
    dz = _matmul(dout, w2, tb=True, name=f"mlp_dact_{tag}", extras=(z,), epilogue=_ep_relu2_bwd,
                 out_dtypes=(BF16,))
    dw1 = _matmul(hn, dz, ta=True, name=f"mlp_dw1_{tag}", out_dtypes=(GRAD_WIRE,))
    dhn = _matmul(dz, w1, tb=True, name=f"mlp_dhn_{tag}")
    dh, dg = _rmsnorm_bwd(dhn, h, g, dout, name=f"mlp_norm_bwd_{tag}")
    return dh, dg, dw1, dw2


def _local_step(x, target, W):
    B, T, D = x.shape
    N = B * T
    G = {}
    row = lambda vec: vec.reshape(1, -1)
    PW = W["pool_w"].shape[0] * W["pool_w"].shape[1]
    SW = W["sgu_norm_g"].shape[-1]
    H = (W["cd_w_in"].shape[1] - 2 * SW) // (3 * SB_DH)

    h0 = x.reshape(N, D)
    xn0 = _rmsnorm(h0, row(W["mix_norm_g"][0]), name="mix_norm_0")
    p0 = _matmul(xn0, W["ab_w_in"], name="ab_in")
    a_out = _pool_fwd(p0, W["pool_w"], W["pool_scale"], B, T, name="pool_fwd")
    b_out = _conv_fwd(p0, W["conv_w"], row(W["conv_b"]), B, T, PW, name="conv_fwd")
    cat0 = jnp.concatenate([a_out, b_out], axis=1)
    h1 = _matmul(cat0, W["ab_w_out"], name="ab_out", extras=(h0,), epilogue=_ep_add)
    h2, mlp0 = _mlp_fwd(h1, row(W["mlp_norm_g"][0]), W["mlp_w1"][0], W["mlp_w2"][0], 0)
    xn1 = _rmsnorm(h2, row(W["mix_norm_g"][1]), name="mix_norm_1")
    p1 = _matmul(xn1, W["cd_w_in"], name="cd_in")
    ln_g, ln_b = row(W["sgu_norm_g"]), row(W["sgu_norm_b"])
    c_out = _sgu_fwd(p1, ln_g, ln_b, W["sgu_w"], W["sgu_b"], name="sgu_fwd")
    qkv = _heads_in(p1[:, 2 * SW:], B, T, H)
    att, att_carries = _attn_fwd(qkv[0], qkv[1], qkv[2], name="attn_fwd")
    d_out = _heads_out(att, B, T, H)
    cat1 = jnp.concatenate([c_out, d_out], axis=1)
    h3 = _matmul(cat1, W["cd_w_out"], name="cd_out", extras=(h2,), epilogue=_ep_add)
    h4, mlp1 = _mlp_fwd(h3, row(W["mlp_norm_g"][1]), W["mlp_w1"][1], W["mlp_w2"][1], 1)

    loss, dh4, G["final_norm_g"] = _final_loss(h4, row(W["final_norm_g"]), target.reshape(N, D), name="final_loss")

    dh3, dmlp_g1, dw1_1, dw2_1 = _mlp_bwd(dh4, h3, row(W["mlp_norm_g"][1]), W["mlp_w1"][1], W["mlp_w2"][1], mlp1, 1)
    G["cd_w_out"] = _matmul(cat1, dh3, ta=True, name="cd_out_dw", out_dtypes=(GRAD_WIRE,))[None]
    dcat1 = _matmul(dh3, W["cd_w_out"], tb=True, name="cd_out_dx")
    dpu, dpv, G["sgu_w"], dsgu_b, G["sgu_norm_g"], G["sgu_norm_b"] = _sgu_bwd(
        p1, dcat1, ln_g, ln_b, W["sgu_w"], W["sgu_b"], name="sgu_bwd")
    G["sgu_b"] = dsgu_b.reshape(W["sgu_b"].shape)
    do = dcat1[:, SW:].reshape(B, T, H, SB_DH).transpose(0, 2, 1, 3).reshape(B * H, T, SB_DH).astype(BF16)
    dq, dk, dv = _attn_bwd(qkv[0], qkv[1], qkv[2], do, att_carries, name="attn_bwd")
    dq = dq * (1.0 / math.sqrt(SB_DH))
    dqkv = jnp.concatenate([_heads_out(t, B, T, H) for t in (dq, dk, dv)], axis=1).astype(BF16)
    dp1 = jnp.concatenate([dpu, dpv, dqkv], axis=1)
    G["cd_w_in"] = _matmul(xn1, dp1, ta=True, name="cd_in_dw", out_dtypes=(GRAD_WIRE,))[None]
    dxn1 = _matmul(dp1, W["cd_w_in"], tb=True, name="cd_in_dx")
    dh2, dmix_g1 = _rmsnorm_bwd(dxn1, h2, row(W["mix_norm_g"][1]), dh3, name="mix_norm_bwd_1")

    dh1, dmlp_g0, dw1_0, dw2_0 = _mlp_bwd(dh2, h1, row(W["mlp_norm_g"][0]), W["mlp_w1"][0], W["mlp_w2"][0], mlp0, 0)
    G["ab_w_out"] = _matmul(cat0, dh1, ta=True, name="ab_out_dw", out_dtypes=(GRAD_WIRE,))[None]
    dcat0 = _matmul(dh1, W["ab_w_out"], tb=True, name="ab_out_dx")
    da, G["pool_w"], dps = _pool_bwd(p0, dcat0, W["pool_w"], W["pool_scale"], B, T, name="pool_bwd")
    G["pool_scale"] = dps.reshape(W["pool_scale"].shape)
    dxb, dgb, dgc, G["conv_w"], dcb = _conv_bwd(p0, dcat0, W["conv_w"], row(W["conv_b"]), B, T, PW, PW,
                                                name="conv_bwd")
    G["conv_b"] = dcb.reshape(-1)
    dp0 = jnp.concatenate([da, dxb, dgb, dgc], axis=1)
    G["ab_w_in"] = _matmul(xn0, dp0, ta=True, name="ab_in_dw", out_dtypes=(GRAD_WIRE,))[None]
    dxn0 = _matmul(dp0, W["ab_w_in"], tb=True, name="ab_in_dx")
    dx, dmix_g0 = _rmsnorm_bwd(dxn0, h0, row(W["mix_norm_g"][0]), dh1, name="mix_norm_bwd_0")

    G["mix_norm_g"] = jnp.concatenate([dmix_g0, dmix_g1], axis=0)
    G["mlp_norm_g"] = jnp.concatenate([dmlp_g0, dmlp_g1], axis=0)
    G["mlp_w1"] = jnp.stack([dw1_0, dw1_1])
    G["mlp_w2"] = jnp.stack([dw2_0, dw2_1])
    G["final_norm_g"] = G["final_norm_g"].reshape(-1)
    G["sgu_norm_g"] = G["sgu_norm_g"].reshape(-1)
    G["sgu_norm_b"] = G["sgu_norm_b"].reshape(-1)
    return loss[0, 0], dx.reshape(B, T, D), G


_NAMES = ["mix_norm_g", "mlp_norm_g", "ab_w_in", "pool_w", "pool_scale", "conv_w", "conv_b", "ab_w_out",
          "cd_w_in", "sgu_norm_g", "sgu_norm_b", "sgu_w", "sgu_b", "cd_w_out", "mlp_w1", "mlp_w2",
          "final_norm_g"]
_COL_SHARDED = ["ab_w_in", "cd_w_in", "mlp_w1"]
_ROW_SHARDED = ["ab_w_out", "cd_w_out", "mlp_w2"]
_SMALL_SHARDED = ["conv_w", "sgu_norm_g", "sgu_norm_b"]
_REPLICATED = ["mix_norm_g", "mlp_norm_g", "pool_w", "pool_scale", "conv_b", "sgu_w", "sgu_b", "final_norm_g"]


def _pad_rows(a2d, mult=SUBLANES):
    pad = (-a2d.shape[0]) % mult
    return jnp.pad(a2d, ((0, pad), (0, 0))) if pad else a2d


def _pack(arrays):
    return _pad_rows(jnp.concatenate([a.reshape(-1, LANES) for a in arrays], axis=0))


def _unpack(packed, shapes):
    out, r = [], 0
    for s in shapes:
        n = math.prod(s) // LANES
        out.append(packed[r:r + n].reshape(s))
        r += n
    return out


def _small_shard_pack(arrays):
    rows = [jnp.pad(a.reshape(-1, a.shape[-1]), ((0, 0), (0, LANES - a.shape[-1]))) for a in arrays]
    return _pad_rows(jnp.concatenate(rows, axis=0))


def _cols_to_chunks(a):
    n = a.shape[-1] // N_DEV
    return jnp.moveaxis(a.reshape(a.shape[:-1] + (N_DEV, n)), -2, 0)


def _chunks_to_cols(a):
    t = jnp.moveaxis(a, 0, -2)
    return t.reshape(t.shape[:-2] + (t.shape[-2] * t.shape[-1],))


def _rows_to_chunks(a):
    r = a.shape[-2] // N_DEV
    return jnp.moveaxis(a.reshape(a.shape[:-2] + (N_DEV, r, a.shape[-1])), -3, 0)


def _chunks_to_rows(a):
    t = jnp.moveaxis(a, 0, -3)
    return t.reshape(t.shape[:-3] + (t.shape[-3] * t.shape[-2], t.shape[-1]))


def kernel(x, mix_norm_g, mlp_norm_g, ab_w_in, pool_w, pool_scale, conv_w, conv_b, ab_w_out, cd_w_in, sgu_norm_g, sgu_norm_b, sgu_w, sgu_b, cd_w_out, mlp_w1, mlp_w2, final_norm_g, loss_target, m_mix_norm_g, m_mlp_norm_g, m_ab_w_in, m_pool_w, m_pool_scale, m_conv_w, m_conv_b, m_ab_w_out, m_cd_w_in, m_sgu_norm_g, m_sgu_norm_b, m_sgu_w, m_sgu_b, m_cd_w_out, m_mlp_w1, m_mlp_w2, m_final_norm_g, v_mix_norm_g, v_mlp_norm_g, v_ab_w_in, v_pool_w, v_pool_scale, v_conv_w, v_conv_b, v_ab_w_out, v_cd_w_in, v_sgu_norm_g, v_sgu_norm_b, v_sgu_w, v_sgu_b, v_cd_w_out, v_mlp_w1, v_mlp_w2, v_final_norm_g):
    w = dict(zip(_NAMES, (mix_norm_g, mlp_norm_g, ab_w_in, pool_w, pool_scale, conv_w, conv_b, ab_w_out, cd_w_in,
                          sgu_norm_g, sgu_norm_b, sgu_w, sgu_b, cd_w_out, mlp_w1, mlp_w2, final_norm_g)))
    m = dict(zip(_NAMES, (m_mix_norm_g, m_mlp_norm_g, m_ab_w_in, m_pool_w, m_pool_scale, m_conv_w, m_conv_b,
                          m_ab_w_out, m_cd_w_in, m_sgu_norm_g, m_sgu_norm_b, m_sgu_w, m_sgu_b, m_cd_w_out,
                          m_mlp_w1, m_mlp_w2, m_final_norm_g)))
    v = dict(zip(_NAMES, (v_mix_norm_g, v_mlp_norm_g, v_ab_w_in, v_pool_w, v_pool_scale, v_conv_w, v_conv_b,
                          v_ab_w_out, v_cd_w_in, v_sgu_norm_g, v_sgu_norm_b, v_sgu_w, v_sgu_b, v_cd_w_out,
                          v_mlp_w1, v_mlp_w2, v_final_norm_g)))
    big = _COL_SHARDED + _ROW_SHARDED
    me = _my_index()

    small_sh = _small_shard_pack([w[n] for n in _SMALL_SHARDED])
    gathered = _gather_two_level([w[n].astype(BF16) for n in big] + [small_sh], name="gather_weights")
    W = {}
    for n, g in zip(big, gathered):
        W[n] = _chunks_to_cols(g) if n in _COL_SHARDED else _chunks_to_rows(g)
    for n in ("ab_w_in", "ab_w_out", "cd_w_in", "cd_w_out"):
        W[n] = W[n][0]
    small_full = gathered[-1]
    r = 0
    for n in _SMALL_SHARDED:
        rows, width = math.prod(w[n].shape[:-1]), w[n].shape[-1]
        W[n] = _chunks_to_cols(small_full[:, r:r + rows, :width])
        r += rows
    for n in _REPLICATED:
        W[n] = w[n]
    for n in ("pool_w", "pool_scale", "sgu_w", "sgu_b"):
        W[n] = W[n][0]

    loss_part, grad_x, G = _local_step(x, loss_target, W)

    flat = {n: (math.prod(w[n].shape[:-1]), w[n].shape[-1]) for n in big}
    parts = []
    for n in big:
        chunks = _cols_to_chunks(G[n]) if n in _COL_SHARDED else _rows_to_chunks(G[n])
        parts.append(chunks.reshape((N_DEV,) + flat[n]))
    from_sibling = _sibling_swap(parts, name="swap_grads")
    core = lax.axis_index("c")
    chip_parts = []
    for n, p8, got in zip(big, parts, from_sibling):
        own = lax.dynamic_index_in_dim(p8.reshape((N_DEV // 2, 2) + flat[n]), core, axis=1, keepdims=False)
        chip_parts.append(_pair_sum(own, got, name=f"pair_sum_{n}"))
    exchanged = _chip_exchange(chip_parts, name="exchange_grads")
    small_names = _REPLICATED + _SMALL_SHARDED
    small_grads = [G[n].reshape(-1) for n in small_names]
    loss_row = jnp.full((LANES,), loss_part, F32)
    small_pack = _pack(small_grads + [loss_row])
    small_parts = _exchange([small_pack], [True], name="gather_small_grads")[0]

    grads, deltas, new_m, new_v = {}, {}, {}, {}
    for n, gp in zip(big, exchanged):
        shp = w[n].shape
        outs = _adamw(w[n].reshape(flat[n]), gp, m[n].reshape(flat[n]), v[n].reshape(flat[n]), name=f"adamw_{n}")
        grads[n], deltas[n], new_m[n], new_v[n] = [o.reshape(shp) for o in outs]

    rep_shapes = [w[n].shape for n in _REPLICATED]
    rep_rows = sum(math.prod(s) for s in rep_shapes) // LANES
    small_sum_shapes = [(G[n].size,) for n in small_names] + [(LANES,)]
    zero_tail = [jnp.zeros((math.prod(s),), F32) for s in small_sum_shapes[len(_REPLICATED):]]
    w_pack = _pack([w[n] for n in _REPLICATED] + zero_tail)
    m_pack = _pack([m[n] for n in _REPLICATED] + zero_tail)
    v_pack = _pack([v[n] for n in _REPLICATED] + zero_tail)
    outs = _adamw(w_pack, small_parts, m_pack, v_pack, name="adamw_small")
    summed = _unpack(outs[0], small_sum_shapes)
    for i, n in enumerate(_REPLICATED):
        grads[n] = summed[i].reshape(w[n].shape)
    for dst, o in zip((deltas, new_m, new_v), outs[1:]):
        for n, val in zip(_REPLICATED, _unpack(o[:rep_rows], rep_shapes)):
            dst[n] = val
    loss = summed[-1][0]

    shard_g = []
    for i, n in enumerate(_SMALL_SHARDED):
        full = summed[len(_REPLICATED) + i].reshape(w[n].shape[:-1] + (-1,))
        width = w[n].shape[-1]
        shard_g.append(lax.dynamic_slice_in_dim(full, me * width, width, axis=full.ndim - 1))
    g_sh = _small_shard_pack(shard_g)
    m_sh = _small_shard_pack([m[n] for n in _SMALL_SHARDED])
    v_sh = _small_shard_pack([v[n] for n in _SMALL_SHARDED])
    outs = _adamw(small_sh, g_sh[None], m_sh, v_sh, name="adamw_small_sharded")
    r = 0
    for n in _SMALL_SHARDED:
        rows, width = math.prod(w[n].shape[:-1]), w[n].shape[-1]
        for dst, o in zip((grads, deltas, new_m, new_v), outs):
            dst[n] = o[r:r + rows, :width].reshape(w[n].shape)
        r += rows

    return (loss, grad_x, *[grads[n] for n in _NAMES], *[deltas[n] for n in _NAMES],
            *[new_m[n] for n in _NAMES], *[new_v[n] for n in _NAMES])
```

```python
import functools
import math

import jax
import jax.numpy as jnp
from jax import lax
from jax.experimental import pallas as pl
from jax.experimental.pallas import tpu as pltpu

F32 = jnp.float32
BF16 = jnp.bfloat16
GRAD_WIRE = jnp.bfloat16

NORM_EPS = 1e-6
ADAM_LR = 0.001
ADAM_B1 = 0.9
ADAM_B2 = 0.999
ADAM_EPS = 1e-08
ADAM_WD = 0.01
ADAM_STEP = 10
ADAM_C1 = 1.0 - ADAM_B1 ** ADAM_STEP
ADAM_C2 = 1.0 - ADAM_B2 ** ADAM_STEP

N_DEV = 8
LANES = 128
SUBLANES = 8
SB_DH = 64
ATT_BLOCK = 256
POOL_LOG_WINDOWS = 4
VMEM_LIMIT = 56 * 1024 * 1024


def _params(semantics=None):
    return pltpu.CompilerParams(dimension_semantics=semantics, vmem_limit_bytes=VMEM_LIMIT)


def _tile(dim, pref, unit=LANES):
    if dim <= pref:
        return dim
    t = (pref // unit) * unit
    while t >= unit:
        if dim % t == 0:
            return t
        t -= unit
    return dim


def _matmul(a, b, *, name, ta=False, tb=False, extras=(), epilogue=None, out_dtypes=(F32,),
            tm=1024, tn=1024, tk=1024):
    M, K = (a.shape[1], a.shape[0]) if ta else a.shape
    N = b.shape[0] if tb else b.shape[1]
    assert (b.shape[1] if tb else b.shape[0]) == K, (a.shape, b.shape)
    tm, tn, tk = _tile(M, tm), _tile(N, tn), _tile(K, tk)
    nk = K // tk
    dims = (((0 if ta else 1,), (1 if tb else 0,)), ((), ()))
    ne, no = len(extras), len(out_dtypes)

    def body(*refs):
        a_ref, b_ref = refs[0], refs[1]
        e_refs = refs[2:2 + ne]
        o_refs = refs[2 + ne:2 + ne + no]
        k = pl.program_id(2)

        def part():
            return lax.dot_general(a_ref[...].astype(BF16), b_ref[...].astype(BF16), dims,
                                   preferred_element_type=F32)

        def finish(acc):
            outs = epilogue(acc, *[e[...] for e in e_refs]) if epilogue is not None else (acc,)
            for o_ref, val in zip(o_refs, outs):
                o_ref[...] = val.astype(o_ref.dtype)

        if nk == 1:
            finish(part())
        else:
            acc_ref = refs[-1]

            @pl.when(k == 0)
            def _():
                acc_ref[...] = jnp.zeros_like(acc_ref)

            acc_ref[...] += part()

            @pl.when(k == nk - 1)
            def _():
                finish(acc_ref[...])

    a_spec = (pl.BlockSpec((tk, tm), lambda i, j, k: (k, i)) if ta
              else pl.BlockSpec((tm, tk), lambda i, j, k: (i, k)))
    b_spec = (pl.BlockSpec((tn, tk), lambda i, j, k: (j, k)) if tb
              else pl.BlockSpec((tk, tn), lambda i, j, k: (k, j)))
    o_spec = pl.BlockSpec((tm, tn), lambda i, j, k: (i, j))
    outs = pl.pallas_call(
        body,
        name=name,
        grid=(M // tm, N // tn, nk),
        in_specs=[a_spec, b_spec] + [o_spec] * ne,
        out_specs=[o_spec] * no,
        out_shape=[jax.ShapeDtypeStruct((M, N), dt) for dt in out_dtypes],
        scratch_shapes=[pltpu.VMEM((tm, tn), F32)] if nk > 1 else [],
        compiler_params=_params(("parallel", "parallel", "arbitrary")),
    )(a, b, *extras)
    return outs[0] if no == 1 else outs


def _ep_add(acc, res):
    return (acc + res,)


def _ep_relu2(acc):
    r = jnp.maximum(acc, 0.0)
    return (r * r,)


def _ep_relu2_bwd(acc, act):
    return (acc * (2.0 * jnp.sqrt(act.astype(F32))),)


def _rstd(x):
    return lax.rsqrt(jnp.mean(x * x, axis=-1, keepdims=True) + NORM_EPS)


def _rmsnorm(h, g, *, name, tr=512):
    N, D = h.shape
    tr = _tile(N, tr, SUBLANES)

    def body(h_ref, g_ref, o_ref):
        x = h_ref[...]
        o_ref[...] = ((x * _rstd(x)) * g_ref[...]).astype(o_ref.dtype)

    row = pl.BlockSpec((tr, D), lambda i: (i, 0))
    vec = pl.BlockSpec((1, D), lambda i: (0, 0))
    return pl.pallas_call(
        body, name=name, grid=(N // tr,), in_specs=[row, vec], out_specs=row,
        out_shape=jax.ShapeDtypeStruct((N, D), BF16), compiler_params=_params(("parallel",)),
    )(h, g)


def _rmsnorm_bwd(dy, h, g, dres, *, name, tr=512):
    N, D = h.shape
    tr = _tile(N, tr, SUBLANES)

    def body(dy_ref, h_ref, g_ref, r_ref, dh_ref, dhb_ref, dg_ref):
        i = pl.program_id(0)
        x = h_ref[...]
        d = dy_ref[...]
        r = _rstd(x)
        xh = x * r

        @pl.when(i == 0)
        def _():
            dg_ref[...] = jnp.zeros_like(dg_ref)

        dg_ref[...] += jnp.sum(d * xh, axis=0, keepdims=True)
        dxh = d * g_ref[...]
        dh = r_ref[...] + r * (dxh - xh * jnp.mean(dxh * xh, axis=-1, keepdims=True))
        dh_ref[...] = dh
        dhb_ref[...] = dh.astype(dhb_ref.dtype)

    row = pl.BlockSpec((tr, D), lambda i: (i, 0))
    vec = pl.BlockSpec((1, D), lambda i: (0, 0))
    return pl.pallas_call(
        body, name=name, grid=(N // tr,), in_specs=[row, row, vec, row], out_specs=[row, row, vec],
        out_shape=[jax.ShapeDtypeStruct((N, D), F32), jax.ShapeDtypeStruct((N, D), BF16),
                   jax.ShapeDtypeStruct((1, D), F32)],
        compiler_params=_params(("arbitrary",)),
    )(dy, h, g, dres)


def _final_loss(h, g, target, *, name, tr=512):
    N, D = h.shape
    tr = _tile(N, tr, SUBLANES)

    def body(h_ref, g_ref, t_ref, loss_ref, dh_ref, dhb_ref, dg_ref):
        i = pl.program_id(0)
        x = h_ref[...]
        gg = g_ref[...]
        r = _rstd(x)
        xh = x * r
        err = xh * gg - t_ref[...]

        @pl.when(i == 0)
        def _():
            dg_ref[...] = jnp.zeros_like(dg_ref)
            loss_ref[...] = jnp.zeros_like(loss_ref)

        per_row = jnp.mean(err * err, axis=-1, keepdims=True)
        loss_ref[...] += 0.5 * jnp.sum(per_row, axis=0, keepdims=True)
        dy = err * (1.0 / D)
        dg_ref[...] += jnp.sum(dy * xh, axis=0, keepdims=True)
        dxh = dy * gg
        dh = r * (dxh - xh * jnp.mean(dxh * xh, axis=-1, keepdims=True))
        dh_ref[...] = dh
        dhb_ref[...] = dh.astype(dhb_ref.dtype)

    row = pl.BlockSpec((tr, D), lambda i: (i, 0))
    vec = pl.BlockSpec((1, D), lambda i: (0, 0))
    lvec = pl.BlockSpec((1, LANES), lambda i: (0, 0))
    return pl.pallas_call(
        body, name=name, grid=(N // tr,), in_specs=[row, vec, row], out_specs=[lvec, row, row, vec],
        out_shape=[jax.ShapeDtypeStruct((1, LANES), F32), jax.ShapeDtypeStruct((N, D), F32),
                   jax.ShapeDtypeStruct((N, D), BF16), jax.ShapeDtypeStruct((1, D), F32)],
        compiler_params=_params(("arbitrary",)),
    )(h, g, target)


def _shift_down(x, s):
    t = lax.broadcasted_iota(jnp.int32, x.shape, 0)
    return jnp.where(t >= s, pltpu.roll(x, s, 0), 0.0)


def _shift_up(x, s):
    n = x.shape[0]
    t = lax.broadcasted_iota(jnp.int32, x.shape, 0)
    return jnp.where(t < n - s, pltpu.roll(x, n - s, 0), 0.0)


def _window_sum(x, g, shift):
    s = x + shift(x, 1)
    for k in range(1, POOL_LOG_WINDOWS):
        s = jnp.where(k <= g, s + shift(s, 2 ** k), s)
    return s


def _pool_count(shape, g):
    t = lax.broadcasted_iota(jnp.int32, shape, 0)
    return jnp.minimum(t + 1, lax.shift_left(jnp.int32(2), g)).astype(F32)


def _pool_fwd(p, pool_w, pool_scale, B, T, off, width, *, name):
    G, dh = pool_w.shape[0], pool_w.shape[1]
    assert G == POOL_LOG_WINDOWS and off % dh == 0
    base = off // dh

    def body(a_ref, w_ref, s_ref, o_ref):
        g = pl.program_id(0)
        a = a_ref[...]
        pooled = _window_sum(a, g, _shift_down) / _pool_count(a.shape, g) - a
        m = jnp.dot(pooled.astype(BF16), w_ref[0].astype(BF16), preferred_element_type=F32)
        o_ref[...] = (m * s_ref[0]).astype(o_ref.dtype)

    return pl.pallas_call(
        body, name=name, grid=(G, B),
        in_specs=[pl.BlockSpec((T, dh), lambda g, b: (b, base + g)),
                  pl.BlockSpec((1, dh, dh), lambda g, b: (g, 0, 0)),
                  pl.BlockSpec((1, 1, dh), lambda g, b: (g, 0, 0))],
        out_specs=pl.BlockSpec((T, dh), lambda g, b: (b, g)),
        out_shape=jax.ShapeDtypeStruct((B * T, width), BF16),
        compiler_params=_params(("parallel", "parallel")),
    )(p, pool_w, pool_scale.reshape(G, 1, dh))


def _pool_bwd(p, dcat, pool_w, pool_scale, B, T, off, *, name):
    G, dh = pool_w.shape[0], pool_w.shape[1]
    base = off // dh

    def body(a_ref, d_ref, w_ref, s_ref, da_ref, dw_ref, ds_ref):
        g = pl.program_id(0)
        b = pl.program_id(1)
        a = a_ref[...]
        d = d_ref[...]
        cnt = _pool_count(a.shape, g)
        pooled = (_window_sum(a, g, _shift_down) / cnt - a).astype(BF16)
        w = w_ref[0].astype(BF16)
        m = jnp.dot(pooled, w, preferred_element_type=F32)

        @pl.when(b == 0)
        def _():
            dw_ref[...] = jnp.zeros_like(dw_ref)
            ds_ref[...] = jnp.zeros_like(ds_ref)

        ds_ref[0] += jnp.sum(d * m, axis=0, keepdims=True)
        dm = (d * s_ref[0]).astype(BF16)
        dw_ref[0] += lax.dot_general(pooled, dm, (((0,), (0,)), ((), ())), preferred_element_type=F32)
        dpooled = lax.dot_general(dm, w, (((1,), (1,)), ((), ())), preferred_element_type=F32)
        da = _window_sum(dpooled / cnt, g, _shift_up) - dpooled
        da_ref[...] = da.astype(da_ref.dtype)

    pblk = pl.BlockSpec((T, dh), lambda g, b: (b, base + g))
    dblk = pl.BlockSpec((T, dh), lambda g, b: (b, g))
    wspec = pl.BlockSpec((1, dh, dh), lambda g, b: (g, 0, 0))
    sspec = pl.BlockSpec((1, 1, dh), lambda g, b: (g, 0, 0))
    return pl.pallas_call(
        body, name=name, grid=(G, B), in_specs=[pblk, dblk, wspec, sspec], out_specs=[pblk, wspec, sspec],
        out_shape=[jax.ShapeDtypeStruct((B * T, p.shape[1]), BF16), jax.ShapeDtypeStruct((G, dh, dh), F32),
                   jax.ShapeDtypeStruct((G, 1, dh), F32)],
        compiler_params=_params(("parallel", "arbitrary")),
    )(p, dcat, pool_w, pool_scale.reshape(G, 1, dh))


_ANY = pl.BlockSpec(memory_space=pl.ANY)


def _conv_fwd(p, cat, conv_w, conv_b, B, T, coff, *, name):
    CW = conv_w.shape[1]
    tc = LANES
    assert coff % tc == 0 and CW % tc == 0
    cbase = coff // tc

    def body(p_ref, cat_ref, w_ref, b_ref, o_ref):
        xb, gb, gc = p_ref[:, 0:tc], p_ref[:, tc:2 * tc], p_ref[:, 2 * tc:3 * tc]
        c = gc * xb
        w = w_ref[...]
        y = _shift_down(c, 2) * w[0:1] + _shift_down(c, 1) * w[1:2] + c * w[2:3] + b_ref[...]
        o_ref[...] = (gb * y).astype(o_ref.dtype)

    return pl.pallas_call(
        body, name=name, grid=(CW // tc, B),
        in_specs=[pl.BlockSpec((T, 3 * tc), lambda j, b: (b, j)), _ANY,
                  pl.BlockSpec((3, tc), lambda j, b: (0, j)), pl.BlockSpec((1, tc), lambda j, b: (0, j))],
        out_specs=pl.BlockSpec((T, tc), lambda j, b: (b, cbase + j)),
        out_shape=jax.ShapeDtypeStruct(cat.shape, cat.dtype), input_output_aliases={1: 0},
        compiler_params=_params(("parallel", "parallel")),
    )(p, cat, conv_w, conv_b)


def _conv_bwd(p, dcat, dp, conv_w, conv_b, B, T, coff, *, name):
    CW = conv_w.shape[1]
    tc = LANES
    assert coff % tc == 0
    cbase = coff // tc

    def body(p_ref, d_ref, dp_in_ref, w_ref, b_ref, dp_ref, dw_ref, db_ref):
        b = pl.program_id(1)
        xb, gb, gc = p_ref[:, 0:tc], p_ref[:, tc:2 * tc], p_ref[:, 2 * tc:3 * tc]
        d = d_ref[...]
        w = w_ref[...]
        c = gc * xb
        c1 = _shift_down(c, 1)
        c2 = _shift_down(c, 2)
        y = c2 * w[0:1] + c1 * w[1:2] + c * w[2:3] + b_ref[...]
        dy = d * gb
        dp_ref[:, tc:2 * tc] = (d * y).astype(dp_ref.dtype)

        @pl.when(b == 0)
        def _():
            dw_ref[...] = jnp.zeros_like(dw_ref)
            db_ref[...] = jnp.zeros_like(db_ref)

        db_ref[...] += jnp.sum(dy, axis=0, keepdims=True)
        dw_ref[0:1, :] += jnp.sum(dy * c2, axis=0, keepdims=True)
        dw_ref[1:2, :] += jnp.sum(dy * c1, axis=0, keepdims=True)
        dw_ref[2:3, :] += jnp.sum(dy * c, axis=0, keepdims=True)
        dc = dy * w[2:3] + _shift_up(dy, 1) * w[1:2] + _shift_up(dy, 2) * w[0:1]
        dp_ref[:, 2 * tc:3 * tc] = (dc * xb).astype(dp_ref.dtype)
        dp_ref[:, 0:tc] = (dc * gc).astype(dp_ref.dtype)

    wspec = pl.BlockSpec((3, tc), lambda j, b: (0, j))
    bspec = pl.BlockSpec((1, tc), lambda j, b: (0, j))
    pblk = pl.BlockSpec((T, 3 * tc), lambda j, b: (b, j))
    return pl.pallas_call(
        body, name=name, grid=(CW // tc, B),
        in_specs=[pblk, pl.BlockSpec((T, tc), lambda j, b: (b, cbase + j)), _ANY, wspec, bspec],
        out_specs=[pblk, wspec, bspec],
        out_shape=[jax.ShapeDtypeStruct(dp.shape, dp.dtype), jax.ShapeDtypeStruct((3, CW), F32),
                   jax.ShapeDtypeStruct((1, CW), F32)],
        input_output_aliases={2: 0},
        compiler_params=_params(("parallel", "arbitrary")),
    )(p, dcat, dp, conv_w, conv_b)


_SQRT_HALF = 0.7071067811865476
_INV_SQRT_2PI = 0.3989422804014327


def _gelu(x):
    return x * (lax.erf(x * _SQRT_HALF) + 1.0) * 0.5


def _gelu_grad(x):
    return 0.5 * (lax.erf(x * _SQRT_HALF) + 1.0) + x * (_INV_SQRT_2PI * jnp.exp(-0.5 * x * x))


def _layernorm_parts(v):
    mu = jnp.mean(v, axis=-1, keepdims=True)
    vc = v - mu
    rstd = lax.rsqrt(jnp.mean(vc * vc, axis=-1, keepdims=True) + NORM_EPS)
    return vc * rstd, rstd


def _tril_mask(L):
    r = lax.broadcasted_iota(jnp.int32, (L, L), 0)
    c = lax.broadcasted_iota(jnp.int32, (L, L), 1)
    return r >= c


def _sgu_fwd(p, ln_g, ln_b, sgu_w, sgu_b, off, width, *, name, tr=512):
    N = p.shape[0]
    G, L = sgu_w.shape[0], sgu_w.shape[1]
    SW = ln_g.shape[1]
    dh = SW // G
    assert off % SW == 0
    ub = off // SW
    tr = _tile(N, tr, L)
    assert tr % L == 0

    def body(u_ref, v_ref, g_ref, beta_ref, w_ref, b_ref, o_ref):
        u = _gelu(u_ref[...])
        vhat, _ = _layernorm_parts(_gelu(v_ref[...]))
        vn = (vhat * g_ref[...] + beta_ref[...]).astype(BF16)
        mask = _tril_mask(L)
        for gi in range(G):
            w = jnp.where(mask, w_ref[gi], 0.0).astype(BF16)
            bias = b_ref[gi]
            cols = slice(gi * dh, (gi + 1) * dh)
            for n in range(tr // L):
                rows = slice(n * L, (n + 1) * L)
                s = jnp.dot(w, vn[rows, cols], preferred_element_type=F32) + bias
                o_ref[rows, cols] = (u[rows, cols] * s).astype(o_ref.dtype)

    def col(k):
        return pl.BlockSpec((tr, SW), lambda i: (i, k))

    vec = pl.BlockSpec((1, SW), lambda i: (0, 0))
    return pl.pallas_call(
        body, name=name, grid=(N // tr,),
        in_specs=[col(ub), col(ub + 1), vec, vec, pl.BlockSpec((G, L, L), lambda i: (0, 0, 0)),
                  pl.BlockSpec((G, L, 1), lambda i: (0, 0, 0))],
        out_specs=col(0), out_shape=jax.ShapeDtypeStruct((N, width), BF16),
        compiler_params=_params(("parallel",)),
    )(p, p, ln_g, ln_b, sgu_w, sgu_b.reshape(G, L, 1))


def _sgu_bwd(p, dcat, ln_g, ln_b, sgu_w, sgu_b, off, *, name, tr=512):
    N = p.shape[0]
    G, L = sgu_w.shape[0], sgu_w.shape[1]
    SW = ln_g.shape[1]
    dh = SW // G
    assert off % SW == 0
    ub = off // SW
    tr = _tile(N, tr, L)

    def compute(i, u_ref, v_ref, dc_ref, g_ref, beta_ref, w_ref, b_ref,
                du_ref, dv_ref, dw_ref, db_ref, dg_ref, dbeta_ref, du_s, dvn_s):
        pu = u_ref[...]
        pv = v_ref[...]
        u = _gelu(pu)
        vhat, rstd = _layernorm_parts(_gelu(pv))
        gg = g_ref[...]
        vn = (vhat * gg + beta_ref[...]).astype(BF16)
        dc = dc_ref[...]
        mask = _tril_mask(L)

        @pl.when(i == 0)
        def _():
            dw_ref[...] = jnp.zeros_like(dw_ref)
            db_ref[...] = jnp.zeros_like(db_ref)
            dg_ref[...] = jnp.zeros_like(dg_ref)
            dbeta_ref[...] = jnp.zeros_like(dbeta_ref)

        for gi in range(G):
            w = jnp.where(mask, w_ref[gi], 0.0).astype(BF16)
            bias = b_ref[gi]
            cols = slice(gi * dh, (gi + 1) * dh)
            dw_acc = jnp.zeros((L, L), F32)
            db_acc = jnp.zeros((L, 1), F32)
            for n in range(tr // L):
                rows = slice(n * L, (n + 1) * L)
                vb = vn[rows, cols]
                s = jnp.dot(w, vb, preferred_element_type=F32) + bias
                du_s[rows, cols] = dc[rows, cols] * s
                ds = dc[rows, cols] * u[rows, cols]
                db_acc += jnp.sum(ds, axis=1, keepdims=True)
                dsb = ds.astype(BF16)
                dw_acc += lax.dot_general(dsb, vb, (((1,), (1,)), ((), ())), preferred_element_type=F32)
                dvn_s[rows, cols] = lax.dot_general(w, dsb, (((0,), (0,)), ((), ())),
                                                    preferred_element_type=F32)
            dw_ref[gi] += jnp.where(mask, dw_acc, 0.0)
            db_ref[gi] += db_acc

        dvn = dvn_s[...]
        dg_ref[...] += jnp.sum(dvn * vhat, axis=0, keepdims=True)
        dbeta_ref[...] += jnp.sum(dvn, axis=0, keepdims=True)
        dvh = dvn * gg
        dv = rstd * (dvh - jnp.mean(dvh, axis=-1, keepdims=True)
                     - vhat * jnp.mean(dvh * vhat, axis=-1, keepdims=True))
        dv_ref[...] = (dv * _gelu_grad(pv)).astype(dv_ref.dtype)
        du_ref[...] = (du_s[...] * _gelu_grad(pu)).astype(du_ref.dtype)

    def body(u_ref, v_ref, dc_ref, g_ref, beta_ref, w_ref, b_ref,
             dp_ref, dw_ref, db_ref, dg_ref, dbeta_ref, du_s, dvn_s, dv_s):
        i = pl.program_id(0)
        half = pl.program_id(1)

        @pl.when(half == 0)
        def _():
            compute(i, u_ref, v_ref, dc_ref, g_ref, beta_ref, w_ref, b_ref,
                    dp_ref, dv_s, dw_ref, db_ref, dg_ref, dbeta_ref, du_s, dvn_s)

        @pl.when(half == 1)
        def _():
            dp_ref[...] = dv_s[...]

    def col(k):
        return pl.BlockSpec((tr, SW), lambda i, half: (i, k))

    vec = pl.BlockSpec((1, SW), lambda i, half: (0, 0))
    wspec = pl.BlockSpec((G, L, L), lambda i, half: (0, 0, 0))
    bspec = pl.BlockSpec((G, L, 1), lambda i, half: (0, 0, 0))
    return pl.pallas_call(
        body, name=name, grid=(N // tr, 2),
        in_specs=[col(ub), col(ub + 1), col(0), vec, vec, wspec, bspec],
        out_specs=[pl.BlockSpec((tr, SW), lambda i, half: (i, ub + half)), wspec, bspec, vec, vec],
        out_shape=[jax.ShapeDtypeStruct((N, p.shape[1]), BF16),
                   jax.ShapeDtypeStruct((G, L, L), F32), jax.ShapeDtypeStruct((G, L, 1), F32),
                   jax.ShapeDtypeStruct((1, SW), F32), jax.ShapeDtypeStruct((1, SW), F32)],
        scratch_shapes=[pltpu.VMEM((tr, SW), F32), pltpu.VMEM((tr, SW), F32), pltpu.VMEM((tr, SW), BF16)],
        compiler_params=_params(("arbitrary", "arbitrary")),
    )(p, p, dcat, ln_g, ln_b, sgu_w, sgu_b.reshape(G, L, 1))


def _log_sigmoid_pair(z):
    ls = jnp.minimum(z, 0.0) - jnp.log(1.0 + jnp.exp(-jnp.abs(z)))
    return ls, ls - z


def _split_dot(x, m):
    hi = x.astype(BF16)
    lo = (x - hi.astype(F32)).astype(BF16)
    return (jnp.dot(hi, m, preferred_element_type=F32) + jnp.dot(lo, m, preferred_element_type=F32))


def _att_tiles(TB):
    r = lax.broadcasted_iota(jnp.int32, (TB, TB), 0)
    c = lax.broadcasted_iota(jnp.int32, (TB, TB), 1)
    return r, c


def _att_weights(qt, kb, strict, later, carry):
    z = lax.dot_general(qt, kb, (((1,), (1,)), ((), ())), preferred_element_type=F32)
    ls, lk = _log_sigmoid_pair(z)
    if strict is not None:
        lk = jnp.where(strict, lk, 0.0)
    suffix = _split_dot(lk, later) + carry
    a = jnp.exp(ls + suffix)
    if strict is not None:
        a = jnp.where(strict, a, 0.0)
    return ls, lk, suffix, a


HP = LANES // SB_DH
ATT_SCALE = 1.0 / math.sqrt(SB_DH)


def _stage_heads(src_ref, col0, dst_ref, T, scale=None):
    rows = _tile(T, 256, SUBLANES)

    def chunk(n, _):
        r0 = pl.multiple_of(n * rows, rows)
        for hh in range(HP):
            x = src_ref[pl.ds(r0, rows), col0 + hh * SB_DH:col0 + (hh + 1) * SB_DH]
            if scale is not None:
                x = x * scale
            dst_ref[hh, pl.ds(r0, rows), :] = x.astype(dst_ref.dtype)
        return 0

    lax.fori_loop(0, T // rows, chunk, 0)


def _attn_fwd(p, cat, B, T, coff, *, name):
    dh = SB_DH
    nhp = (cat.shape[1] - coff) // LANES
    TB = _tile(T, ATT_BLOCK)
    nb = T // TB
    assert nb <= LANES and coff % LANES == 0
    cbase = coff // LANES

    def body(p_ref, cat_ref, o_ref, c_ref, q_ref, k_ref, v_ref):
        _stage_heads(p_ref, 0, q_ref, T, ATT_SCALE)
        _stage_heads(p_ref, LANES, k_ref, T)
        _stage_heads(p_ref, 2 * LANES, v_ref, T)
        r, c = _att_tiles(TB)
        strict = c < r
        later = (r > c).astype(BF16)
        lane = lax.broadcasted_iota(jnp.int32, (TB, LANES), 1)

        def tile(hh, qt, j, carry, mask):
            k0 = pl.multiple_of(j * TB, TB)
            _, lk, suffix, a = _att_weights(qt, k_ref[hh, pl.ds(k0, TB), :], mask, later, carry)
            pv = jnp.dot(a.astype(BF16), v_ref[hh, pl.ds(k0, TB), :], preferred_element_type=F32)
            return pv, suffix[:, 0:1] + lk[:, 0:1]

        def qblock(i, _):
            q0 = pl.multiple_of(i * TB, TB)
            qts = [q_ref[hh, pl.ds(q0, TB), :] for hh in range(HP)]
            state = []
            for hh in range(HP):
                pv, carry = tile(hh, qts[hh], i, jnp.zeros((TB, 1), F32), strict)
                state += [pv, carry, jnp.zeros((TB, LANES), F32)]

            def kblock(jj, st):
                j = i - jj
                out = []
                for hh in range(HP):
                    acc, carry, cm = st[3 * hh:3 * hh + 3]
                    pv, new_carry = tile(hh, qts[hh], j, carry, None)
                    out += [acc + pv, new_carry, jnp.where(lane == j, carry, cm)]
                return tuple(out)

            st = lax.fori_loop(1, i + 1, kblock, tuple(state))
            for hh in range(HP):
                o_ref[pl.ds(q0, TB), hh * dh:(hh + 1) * dh] = st[3 * hh].astype(o_ref.dtype)
                c_ref[hh, pl.ds(q0, TB), :] = st[3 * hh + 2]
            return 0

        lax.fori_loop(0, nb, qblock, 0)

    staged = pltpu.VMEM((HP, T, dh), BF16)
    return pl.pallas_call(
        body, name=name, grid=(B, nhp),
        in_specs=[pl.BlockSpec((T, 3 * LANES), lambda b, hp: (b, hp)), _ANY],
        out_specs=[pl.BlockSpec((T, LANES), lambda b, hp: (b, cbase + hp)),
                   pl.BlockSpec((HP, T, LANES), lambda b, hp: (b * nhp + hp, 0, 0))],
        out_shape=[jax.ShapeDtypeStruct(cat.shape, cat.dtype),
                   jax.ShapeDtypeStruct((B * nhp * HP, T, LANES), F32)],
        input_output_aliases={1: 0}, scratch_shapes=[staged, staged, staged],
        compiler_params=_params(("parallel", "parallel")),
    )(p, cat)


def _attn_bwd(p, dcat, carries, dp, B, T, coff, *, name):
    dh = SB_DH
    nhp = (dcat.shape[1] - coff) // LANES
    TB = _tile(T, ATT_BLOCK)
    nb = T // TB
    cbase = coff // LANES

    def body(p_ref, d_ref, c_ref, dp_in_ref, dp_ref, q_ref, k_ref, v_ref, do_ref, dk_ref, dv_ref):
        _stage_heads(p_ref, 0, q_ref, T, ATT_SCALE)
        _stage_heads(p_ref, LANES, k_ref, T)
        _stage_heads(p_ref, 2 * LANES, v_ref, T)
        _stage_heads(d_ref, 0, do_ref, T)
        r, c = _att_tiles(TB)
        strict = c < r
        later = (r > c).astype(BF16)
        earlier = (r < c).astype(BF16)
        lane = lax.broadcasted_iota(jnp.int32, (TB, LANES), 1)
        dk_ref[...] = jnp.zeros_like(dk_ref)
        dv_ref[...] = jnp.zeros_like(dv_ref)

        def tile(hh, qt, dot, cm, j, before, mask):
            k0 = pl.multiple_of(j * TB, TB)
            kb = k_ref[hh, pl.ds(k0, TB), :]
            carry = jnp.sum(jnp.where(lane == j, cm, 0.0), axis=1, keepdims=True)
            ls, _, _, a = _att_weights(qt, kb, mask, later, carry)
            dv_ref[hh, pl.ds(k0, TB), :] += lax.dot_general(a.astype(BF16), dot, (((0,), (0,)), ((), ())),
                                                            preferred_element_type=F32)
            da = lax.dot_general(dot, v_ref[hh, pl.ds(k0, TB), :], (((1,), (1,)), ((), ())),
                                 preferred_element_type=F32)
            gl = a * da
            prefix = _split_dot(gl, earlier) + before
            dz = gl - jnp.exp(ls) * (gl + prefix)
            if mask is not None:
                dz = jnp.where(mask, dz, 0.0)
            dzb = dz.astype(BF16)
            dk_ref[hh, pl.ds(k0, TB), :] += lax.dot_general(dzb, qt, (((0,), (0,)), ((), ())),
                                                            preferred_element_type=F32)
            return jnp.dot(dzb, kb, preferred_element_type=F32), prefix[:, TB - 1:TB] + gl[:, TB - 1:TB]

        def qblock(i, _):
            q0 = pl.multiple_of(i * TB, TB)
            qts = [q_ref[hh, pl.ds(q0, TB), :] for hh in range(HP)]
            dots = [do_ref[hh, pl.ds(q0, TB), :] for hh in range(HP)]
            cms = [c_ref[hh, pl.ds(q0, TB), :] for hh in range(HP)]

            def kblock(j, st):
                out = []
                for hh in range(HP):
                    dq, before = st[2 * hh:2 * hh + 2]
                    part, new_before = tile(hh, qts[hh], dots[hh], cms[hh], j, before, None)
                    out += [dq + part, new_before]
                return tuple(out)

            st = lax.fori_loop(0, i, kblock, (jnp.zeros((TB, dh), F32), jnp.zeros((TB, 1), F32)) * HP)
            for hh in range(HP):
                part, _ = tile(hh, qts[hh], dots[hh], cms[hh], i, st[2 * hh + 1], strict)
                dq = (st[2 * hh] + part) * ATT_SCALE
                dp_ref[pl.ds(q0, TB), hh * dh:(hh + 1) * dh] = dq.astype(dp_ref.dtype)
            return 0

        lax.fori_loop(0, nb, qblock, 0)

        def write_back(n, _):
            r0 = pl.multiple_of(n * TB, TB)
            for hh in range(HP):
                dp_ref[pl.ds(r0, TB), LANES + hh * dh:LANES + (hh + 1) * dh] = (
                    dk_ref[hh, pl.ds(r0, TB), :].astype(dp_ref.dtype))
                dp_ref[pl.ds(r0, TB), 2 * LANES + hh * dh:2 * LANES + (hh + 1) * dh] = (
                    dv_ref[hh, pl.ds(r0, TB), :].astype(dp_ref.dtype))
            return 0

        lax.fori_loop(0, nb, write_back, 0)

    pblk = pl.BlockSpec((T, 3 * LANES), lambda b, hp: (b, hp))
    staged = pltpu.VMEM((HP, T, dh), BF16)
    accum = pltpu.VMEM((HP, T, dh), F32)
    return pl.pallas_call(
        body, name=name, grid=(B, nhp),
        in_specs=[pblk, pl.BlockSpec((T, LANES), lambda b, hp: (b, cbase + hp)),
                  pl.BlockSpec((HP, T, LANES), lambda b, hp: (b * nhp + hp, 0, 0)), _ANY],
        out_specs=pblk, out_shape=jax.ShapeDtypeStruct(dp.shape, dp.dtype), input_output_aliases={3: 0},
        scratch_shapes=[staged, staged, staged, staged, accum, accum],
        compiler_params=_params(("parallel", "parallel")),
    )(p, dcat, carries, dp)


def _adamw(w, gparts, m, v, *, name):
    R, C = w.shape
    P = gparts.shape[0]
    tr = _tile(R, max(SUBLANES, (1 << 19) // (C * P)), SUBLANES)

    def body(w_ref, g_ref, m_ref, v_ref, go_ref, d_ref, mo_ref, vo_ref):
        g = g_ref[0].astype(F32)
        for i in range(1, P):
            g = g + g_ref[i].astype(F32)
        m2 = ADAM_B1 * m_ref[...] + (1.0 - ADAM_B1) * g
        v2 = ADAM_B2 * v_ref[...] + (1.0 - ADAM_B2) * (g * g)
        m_hat = m2 / ADAM_C1
        v_hat = v2 / ADAM_C2
        go_ref[...] = g
        d_ref[...] = -ADAM_LR * (m_hat / (jnp.sqrt(v_hat) + ADAM_EPS) + ADAM_WD * w_ref[...])
        mo_ref[...] = m2
        vo_ref[...] = v2

    row = pl.BlockSpec((tr, C), lambda i: (i, 0))
    shp = jax.ShapeDtypeStruct((R, C), F32)
    return pl.pallas_call(
        body, name=name, grid=(R // tr,),
        in_specs=[row, pl.BlockSpec((P, tr, C), lambda i: (0, i, 0)), row, row],
        out_specs=[row] * 4, out_shape=[shp] * 4, compiler_params=_params(("parallel",)),
    )(w, gparts, m, v)


def _my_index():
    return 4 * lax.axis_index("x") + 2 * lax.axis_index("y") + lax.axis_index("c")


def _exchange(arrs, gather, *, name):
    n = len(arrs)

    def body(*refs):
        ins, outs = refs[:n], refs[n:2 * n]
        send_sems, recv_sems, local_sems = refs[2 * n:]
        x, y, c = lax.axis_index("x"), lax.axis_index("y"), lax.axis_index("c")
        me = 4 * x + 2 * y + c
        remote, local = [], []
        for a in range(n):
            own = ins[a] if gather[a] else ins[a].at[me]
            cp = pltpu.make_async_copy(own, outs[a].at[me], local_sems.at[a])
            cp.start()
            local.append(cp)
            for k in range(1, N_DEV):
                px = 1 - x if k & 4 else x
                py = 1 - y if k & 2 else y
                pc = 1 - c if k & 1 else c
                src = ins[a] if gather[a] else ins[a].at[4 * px + 2 * py + pc]
                cp = pltpu.make_async_remote_copy(
                    src_ref=src, dst_ref=outs[a].at[me],
                    send_sem=send_sems.at[a, k - 1], recv_sem=recv_sems.at[a, k - 1],
                    device_id=(px, py, pc), device_id_type=pl.DeviceIdType.MESH)
                cp.start()
                remote.append(cp)
        for cp in remote:
            cp.wait()
        for cp in local:
            cp.wait()

    hbm = pl.BlockSpec(memory_space=pltpu.HBM)
    out_shape = [jax.ShapeDtypeStruct(((N_DEV,) + a.shape) if g else a.shape, a.dtype)
                 for a, g in zip(arrs, gather)]
    return pl.pallas_call(
        body, name=name, in_specs=[hbm] * n, out_specs=[hbm] * n, out_shape=out_shape,
        scratch_shapes=[pltpu.SemaphoreType.DMA((n, N_DEV - 1)), pltpu.SemaphoreType.DMA((n, N_DEV - 1)),
                        pltpu.SemaphoreType.DMA((n,))],
    )(*arrs)


_HBM = pl.BlockSpec(memory_space=pltpu.HBM)


def _other_chips(x, y):
    return [(1 - x, y), (x, 1 - y), (1 - x, 1 - y)]


def _gather_two_level(arrs, *, name):
    n = len(arrs)

    def body(*refs):
        ins, outs = refs[:n], refs[n:2 * n]
        send_sems, recv_sems, local_sems = refs[2 * n:]
        x, y, c = lax.axis_index("x"), lax.axis_index("y"), lax.axis_index("c")
        me, sibling = (x, y, c), (x, y, 1 - c)
        chips = _other_chips(x, y)

        def slot(a, px, py, pc):
            return outs[a].at[4 * px + 2 * py + pc]

        def copy(a, k, block, to, src=None):
            return pltpu.make_async_remote_copy(
                src_ref=slot(a, *block) if src is None else src, dst_ref=slot(a, *block),
                send_sem=send_sems.at[a, k], recv_sem=recv_sems.at[a, k],
                device_id=to, device_id_type=pl.DeviceIdType.MESH)

        local, sends = [], []
        for a in range(n):
            cp = pltpu.make_async_copy(ins[a], slot(a, *me), local_sems.at[a])
            cp.start()
            local.append(cp)
            first = [copy(a, 0, me, sibling, src=ins[a])]
            first += [copy(a, 1 + j, me, (*chip, c), src=ins[a]) for j, chip in enumerate(chips)]
            for cp in first:
                cp.start()
            sends += first
        for j, chip in enumerate(chips):
            for a in range(n):
                copy(a, 1 + j, (*chip, c), me).wait_recv()
                cp = copy(a, 4 + j, (*chip, c), sibling)
                cp.start()
                sends.append(cp)
        for a in range(n):
            copy(a, 0, sibling, me).wait_recv()
            for j, chip in enumerate(chips):
                copy(a, 4 + j, (*chip, 1 - c), me).wait_recv()
        for cp in sends:
            cp.wait_send()
        for cp in local:
            cp.wait()

    return pl.pallas_call(
        body, name=name, in_specs=[_HBM] * n, out_specs=[_HBM] * n,
        out_shape=[jax.ShapeDtypeStruct((N_DEV,) + a.shape, a.dtype) for a in arrs],
        scratch_shapes=[pltpu.SemaphoreType.DMA((n, N_DEV - 1)), pltpu.SemaphoreType.DMA((n, N_DEV - 1)),
                        pltpu.SemaphoreType.DMA((n,))],
    )(*arrs)


def _sibling_swap(arrs, *, name):
    n = len(arrs)
    nchip = N_DEV // 2

    def body(*refs):
        ins, outs = refs[:n], refs[n:2 * n]
        send_sems, recv_sems = refs[2 * n:]
        x, y, c = lax.axis_index("x"), lax.axis_index("y"), lax.axis_index("c")
        copies = []
        for a in range(n):
            for k in range(nchip):
                cp = pltpu.make_async_remote_copy(
                    src_ref=ins[a].at[2 * k + 1 - c], dst_ref=outs[a].at[k],
                    send_sem=send_sems.at[a, k], recv_sem=recv_sems.at[a, k],
                    device_id=(x, y, 1 - c), device_id_type=pl.DeviceIdType.MESH)
                cp.start()
                copies.append(cp)
        for cp in copies:
            cp.wait()

    return pl.pallas_call(
        body, name=name, in_specs=[_HBM] * n, out_specs=[_HBM] * n,
        out_shape=[jax.ShapeDtypeStruct((nchip,) + a.shape[1:], a.dtype) for a in arrs],
        scratch_shapes=[pltpu.SemaphoreType.DMA((n, nchip)), pltpu.SemaphoreType.DMA((n, nchip))],
    )(*arrs)


def _chip_exchange(arrs, *, name):
    n = len(arrs)

    def body(*refs):
        ins, outs = refs[:n], refs[n:2 * n]
        send_sems, recv_sems, local_sems = refs[2 * n:]
        x, y, c = lax.axis_index("x"), lax.axis_index("y"), lax.axis_index("c")
        mine = 2 * x + y
        copies = []
        for a in range(n):
            cp = pltpu.make_async_copy(ins[a].at[mine], outs[a].at[mine], local_sems.at[a])
            cp.start()
            copies.append(cp)
            for j, (px, py) in enumerate(_other_chips(x, y)):
                cp = pltpu.make_async_remote_copy(
                    src_ref=ins[a].at[2 * px + py], dst_ref=outs[a].at[mine],
                    send_sem=send_sems.at[a, j], recv_sem=recv_sems.at[a, j],
                    device_id=(px, py, c), device_id_type=pl.DeviceIdType.MESH)
                cp.start()
                copies.append(cp)
        for cp in copies:
            cp.wait()

    return pl.pallas_call(
        body, name=name, in_specs=[_HBM] * n, out_specs=[_HBM] * n,
        out_shape=[jax.ShapeDtypeStruct(a.shape, a.dtype) for a in arrs],
        scratch_shapes=[pltpu.SemaphoreType.DMA((n, 3)), pltpu.SemaphoreType.DMA((n, 3)),
                        pltpu.SemaphoreType.DMA((n,))],
    )(*arrs)


def _pair_sum(a, b, *, name):
    P, R, C = a.shape
    tr = _tile(R, max(SUBLANES, (1 << 19) // C), SUBLANES)

    def body(a_ref, b_ref, o_ref):
        o_ref[...] = (a_ref[...].astype(F32) + b_ref[...].astype(F32)).astype(o_ref.dtype)

    blk = pl.BlockSpec((1, tr, C), lambda p, i: (p, i, 0))
    return pl.pallas_call(
        body, name=name, grid=(P, R // tr), in_specs=[blk, blk], out_specs=blk,
        out_shape=jax.ShapeDtypeStruct(a.shape, a.dtype), compiler_params=_params(("parallel", "parallel")),
    )(a, b)


def _group_in_cols(w, lead):
    X = (w.shape[-1] - lead) // 3
    g = w[..., lead:].reshape(w.shape[:-1] + (3, X // LANES, LANES))
    g = jnp.swapaxes(g, -3, -2).reshape(w.shape[:-1] + (3 * X,))
    return jnp.concatenate([g, w[..., :lead]], axis=-1)


def _ungroup_in_cols(w, lead):
    X = (w.shape[-1] - lead) // 3
    g = w[..., :3 * X].reshape(w.shape[:-1] + (X // LANES, 3, LANES))
    g = jnp.swapaxes(g, -3, -2).reshape(w.shape[:-1] + (3 * X,))
    return jnp.concatenate([w[..., 3 * X:], g], axis=-1)


def _mlp_fwd(h, g, w1, w2, tag):
    hn = _rmsnorm(h, g, name=f"mlp_norm_{tag}")
    act = _matmul(hn, w1, name=f"mlp_up_{tag}", epilogue=_ep_relu2, out_dtypes=(BF16,))
    out = _matmul(act, w2, name=f"mlp_down_{tag}", extras=(h,), epilogue=_ep_add)
    return out, (hn, act)


def _mlp_bwd(dout, dout_b, h, g, w1, w2, saved, tag):
    hn, act = saved
    dw2 = _matmul(act, dout_b, ta=True, name=f"mlp_dw2_{tag}", out_dtypes=(GRAD_WIRE,))
    dz = _matmul(dout_b, w2, tb=True, name=f"mlp_dact_{tag}", extras=(act,), epilogue=_ep_relu2_bwd,
                 out_dtypes=(BF16,))
    dw1 = _matmul(hn, dz, ta=True, name=f"mlp_dw1_{tag}", out_dtypes=(GRAD_WIRE,))
    dhn = _matmul(dz, w1, tb=True, name=f"mlp_dhn_{tag}")
    dh, dh_b, dg = _rmsnorm_bwd(dhn, h, g, dout, name=f"mlp_norm_bwd_{tag}")
    return dh, dh_b, dg, dw1, dw2


def _local_step(x, target, W):
    B, T, D = x.shape
    N = B * T
    G = {}
    row = lambda vec: vec.reshape(1, -1)
    PW = W["pool_w"].shape[0] * W["pool_w"].shape[1]
    CW = W["conv_b"].shape[-1]
    SW = W["sgu_norm_g"].shape[-1]
    HW = (W["cd_w_in"].shape[1] - 2 * SW) // 3

    h0 = x.reshape(N, D)
    xn0 = _rmsnorm(h0, row(W["mix_norm_g"][0]), name="mix_norm_0")
    p0 = _matmul(xn0, W["ab_w_in"], name="ab_in")
    cat0 = _pool_fwd(p0, W["pool_w"], W["pool_scale"], B, T, 3 * CW, PW + CW, name="pool_fwd")
    cat0 = _conv_fwd(p0, cat0, W["conv_w"], row(W["conv_b"]), B, T, PW, name="conv_fwd")
    h1 = _matmul(cat0, W["ab_w_out"], name="ab_out", extras=(h0,), epilogue=_ep_add)
    h2, mlp0 = _mlp_fwd(h1, row(W["mlp_norm_g"][0]), W["mlp_w1"][0], W["mlp_w2"][0], 0)
    xn1 = _rmsnorm(h2, row(W["mix_norm_g"][1]), name="mix_norm_1")
    p1 = _matmul(xn1, W["cd_w_in"], name="cd_in")
    ln_g, ln_b = row(W["sgu_norm_g"]), row(W["sgu_norm_b"])
    cat1 = _sgu_fwd(p1, ln_g, ln_b, W["sgu_w"], W["sgu_b"], 3 * HW, SW + HW, name="sgu_fwd")
    cat1, att_carries = _attn_fwd(p1, cat1, B, T, SW, name="attn_fwd")
    h3 = _matmul(cat1, W["cd_w_out"], name="cd_out", extras=(h2,), epilogue=_ep_add)
    h4, mlp1 = _mlp_fwd(h3, row(W["mlp_norm_g"][1]), W["mlp_w1"][1], W["mlp_w2"][1], 1)

    loss, dh4, dh4_b, G["final_norm_g"] = _final_loss(h4, row(W["final_norm_g"]), target.reshape(N, D),
                                                      name="final_loss")

    dh3, dh3_b, dmlp_g1, dw1_1, dw2_1 = _mlp_bwd(dh4, dh4_b, h3, row(W["mlp_norm_g"][1]), W["mlp_w1"][1],
                                                 W["mlp_w2"][1], mlp1, 1)
    G["cd_w_out"] = _matmul(cat1, dh3_b, ta=True, name="cd_out_dw", out_dtypes=(GRAD_WIRE,))[None]
    dcat1 = _matmul(dh3_b, W["cd_w_out"], tb=True, name="cd_out_dx")
    dp1, G["sgu_w"], dsgu_b, G["sgu_norm_g"], G["sgu_norm_b"] = _sgu_bwd(
        p1, dcat1, ln_g, ln_b, W["sgu_w"], W["sgu_b"], 3 * HW, name="sgu_bwd")
    G["sgu_b"] = dsgu_b.reshape(W["sgu_b"].shape)
    dp1 = _attn_bwd(p1, dcat1, att_carries, dp1, B, T, SW, name="attn_bwd")
    G["cd_w_in"] = _matmul(xn1, dp1, ta=True, name="cd_in_dw", out_dtypes=(GRAD_WIRE,))[None]
    dxn1 = _matmul(dp1, W["cd_w_in"], tb=True, name="cd_in_dx")
    dh2, dh2_b, dmix_g1 = _rmsnorm_bwd(dxn1, h2, row(W["mix_norm_g"][1]), dh3, name="mix_norm_bwd_1")

    dh1, dh1_b, dmlp_g0, dw1_0, dw2_0 = _mlp_bwd(dh2, dh2_b, h1, row(W["mlp_norm_g"][0]), W["mlp_w1"][0],
                                                 W["mlp_w2"][0], mlp0, 0)
    G["ab_w_out"] = _matmul(cat0, dh1_b, ta=True, name="ab_out_dw", out_dtypes=(GRAD_WIRE,))[None]
    dcat0 = _matmul(dh1_b, W["ab_w_out"], tb=True, name="ab_out_dx")
    dp0, G["pool_w"], dps = _pool_bwd(p0, dcat0, W["pool_w"], W["pool_scale"], B, T, 3 * CW, name="pool_bwd")
    G["pool_scale"] = dps.reshape(W["pool_scale"].shape)
    dp0, G["conv_w"], dcb = _conv_bwd(p0, dcat0, dp0, W["conv_w"], row(W["conv_b"]), B, T, PW, name="conv_bwd")
    G["conv_b"] = dcb.reshape(-1)
    G["ab_w_in"] = _matmul(xn0, dp0, ta=True, name="ab_in_dw", out_dtypes=(GRAD_WIRE,))[None]
    dxn0 = _matmul(dp0, W["ab_w_in"], tb=True, name="ab_in_dx")
    dx, _, dmix_g0 = _rmsnorm_bwd(dxn0, h0, row(W["mix_norm_g"][0]), dh1, name="mix_norm_bwd_0")

    G["mix_norm_g"] = jnp.concatenate([dmix_g0, dmix_g1], axis=0)
    G["mlp_norm_g"] = jnp.concatenate([dmlp_g0, dmlp_g1], axis=0)
    G["mlp_w1"] = jnp.stack([dw1_0, dw1_1])
    G["mlp_w2"] = jnp.stack([dw2_0, dw2_1])
    G["final_norm_g"] = G["final_norm_g"].reshape(-1)
    G["sgu_norm_g"] = G["sgu_norm_g"].reshape(-1)
    G["sgu_norm_b"] = G["sgu_norm_b"].reshape(-1)
    return loss[0, 0], dx.reshape(B, T, D), G


_NAMES = ["mix_norm_g", "mlp_norm_g", "ab_w_in", "pool_w", "pool_scale", "conv_w", "conv_b", "ab_w_out",
          "cd_w_in", "sgu_norm_g", "sgu_norm_b", "sgu_w", "sgu_b", "cd_w_out", "mlp_w1", "mlp_w2",
          "final_norm_g"]
_COL_SHARDED = ["ab_w_in", "cd_w_in", "mlp_w1"]
_ROW_SHARDED = ["ab_w_out", "cd_w_out", "mlp_w2"]
_SMALL_SHARDED = ["conv_w", "sgu_norm_g", "sgu_norm_b"]
_REPLICATED = ["mix_norm_g", "mlp_norm_g", "pool_w", "pool_scale", "conv_b", "sgu_w", "sgu_b", "final_norm_g"]


def _pad_rows(a2d, mult=SUBLANES):
    pad = (-a2d.shape[0]) % mult
    return jnp.pad(a2d, ((0, pad), (0, 0))) if pad else a2d


def _pack(arrays):
    return _pad_rows(jnp.concatenate([a.reshape(-1, LANES) for a in arrays], axis=0))


def _unpack(packed, shapes):
    out, r = [], 0
    for s in shapes:
        n = math.prod(s) // LANES
        out.append(packed[r:r + n].reshape(s))
        r += n
    return out


def _small_shard_pack(arrays):
    rows = [jnp.pad(a.reshape(-1, a.shape[-1]), ((0, 0), (0, LANES - a.shape[-1]))) for a in arrays]
    return _pad_rows(jnp.concatenate(rows, axis=0))


def _cols_to_chunks(a):
    n = a.shape[-1] // N_DEV
    return jnp.moveaxis(a.reshape(a.shape[:-1] + (N_DEV, n)), -2, 0)


def _chunks_to_cols(a):
    t = jnp.moveaxis(a, 0, -2)
    return t.reshape(t.shape[:-2] + (t.shape[-2] * t.shape[-1],))


def _rows_to_chunks(a):
    r = a.shape[-2] // N_DEV
    return jnp.moveaxis(a.reshape(a.shape[:-2] + (N_DEV, r, a.shape[-1])), -3, 0)


def _chunks_to_rows(a):
    t = jnp.moveaxis(a, 0, -3)
    return t.reshape(t.shape[:-3] + (t.shape[-3] * t.shape[-2], t.shape[-1]))


def kernel(x, mix_norm_g, mlp_norm_g, ab_w_in, pool_w, pool_scale, conv_w, conv_b, ab_w_out, cd_w_in, sgu_norm_g, sgu_norm_b, sgu_w, sgu_b, cd_w_out, mlp_w1, mlp_w2, final_norm_g, loss_target, m_mix_norm_g, m_mlp_norm_g, m_ab_w_in, m_pool_w, m_pool_scale, m_conv_w, m_conv_b, m_ab_w_out, m_cd_w_in, m_sgu_norm_g, m_sgu_norm_b, m_sgu_w, m_sgu_b, m_cd_w_out, m_mlp_w1, m_mlp_w2, m_final_norm_g, v_mix_norm_g, v_mlp_norm_g, v_ab_w_in, v_pool_w, v_pool_scale, v_conv_w, v_conv_b, v_ab_w_out, v_cd_w_in, v_sgu_norm_g, v_sgu_norm_b, v_sgu_w, v_sgu_b, v_cd_w_out, v_mlp_w1, v_mlp_w2, v_final_norm_g):
    w = dict(zip(_NAMES, (mix_norm_g, mlp_norm_g, ab_w_in, pool_w, pool_scale, conv_w, conv_b, ab_w_out, cd_w_in,
                          sgu_norm_g, sgu_norm_b, sgu_w, sgu_b, cd_w_out, mlp_w1, mlp_w2, final_norm_g)))
    m = dict(zip(_NAMES, (m_mix_norm_g, m_mlp_norm_g, m_ab_w_in, m_pool_w, m_pool_scale, m_conv_w, m_conv_b,
                          m_ab_w_out, m_cd_w_in, m_sgu_norm_g, m_sgu_norm_b, m_sgu_w, m_sgu_b, m_cd_w_out,
                          m_mlp_w1, m_mlp_w2, m_final_norm_g)))
    v = dict(zip(_NAMES, (v_mix_norm_g, v_mlp_norm_g, v_ab_w_in, v_pool_w, v_pool_scale, v_conv_w, v_conv_b,
                          v_ab_w_out, v_cd_w_in, v_sgu_norm_g, v_sgu_norm_b, v_sgu_w, v_sgu_b, v_cd_w_out,
                          v_mlp_w1, v_mlp_w2, v_final_norm_g)))
    big = _COL_SHARDED + _ROW_SHARDED
    me = _my_index()

    small_sh = _small_shard_pack([w[n] for n in _SMALL_SHARDED])
    gathered = _gather_two_level([w[n].astype(BF16) for n in big] + [small_sh], name="gather_weights")
    W = {}
    for n, g in zip(big, gathered):
        W[n] = _chunks_to_cols(g) if n in _COL_SHARDED else _chunks_to_rows(g)
    in_lead = {"ab_w_in": pool_w.shape[1] * pool_w.shape[2], "cd_w_in": 2 * sgu_norm_g.shape[-1] * N_DEV}
    for n in ("ab_w_in", "ab_w_out", "cd_w_in", "cd_w_out"):
        W[n] = _group_in_cols(W[n][0], in_lead[n]) if n in in_lead else W[n][0]
    small_full = gathered[-1]
    r = 0
    for n in _SMALL_SHARDED:
        rows, width = math.prod(w[n].shape[:-1]), w[n].shape[-1]
        W[n] = _chunks_to_cols(small_full[:, r:r + rows, :width])
        r += rows
    for n in _REPLICATED:
        W[n] = w[n]
    for n in ("pool_w", "pool_scale", "sgu_w", "sgu_b"):
        W[n] = W[n][0]

    loss_part, grad_x, G = _local_step(x, loss_target, W)

    flat = {n: (math.prod(w[n].shape[:-1]), w[n].shape[-1]) for n in big}
    parts = []
    for n in big:
        g = _ungroup_in_cols(G[n], in_lead[n]) if n in in_lead else G[n]
        chunks = _cols_to_chunks(g) if n in _COL_SHARDED else _rows_to_chunks(g)
        parts.append(chunks.reshape((N_DEV,) + flat[n]))
    from_sibling = _sibling_swap(parts, name="swap_grads")
    core = lax.axis_index("c")
    chip_parts = []
    for n, p8, got in zip(big, parts, from_sibling):
        own = lax.dynamic_index_in_dim(p8.reshape((N_DEV // 2, 2) + flat[n]), core, axis=1, keepdims=False)
        chip_parts.append(_pair_sum(own, got, name=f"pair_sum_{n}"))
    exchanged = _chip_exchange(chip_parts, name="exchange_grads")
    small_names = _REPLICATED + _SMALL_SHARDED
    small_grads = [G[n].reshape(-1) for n in small_names]
    loss_row = jnp.full((LANES,), loss_part, F32)
    small_pack = _pack(small_grads + [loss_row])
    small_parts = _exchange([small_pack], [True], name="gather_small_grads")[0]

    grads, deltas, new_m, new_v = {}, {}, {}, {}
    for n, gp in zip(big, exchanged):
        shp = w[n].shape
        outs = _adamw(w[n].reshape(flat[n]), gp, m[n].reshape(flat[n]), v[n].reshape(flat[n]), name=f"adamw_{n}")
        grads[n], deltas[n], new_m[n], new_v[n] = [o.reshape(shp) for o in outs]

    rep_shapes = [w[n].shape for n in _REPLICATED]
    rep_rows = sum(math.prod(s) for s in rep_shapes) // LANES
    small_sum_shapes = [(G[n].size,) for n in small_names] + [(LANES,)]
    zero_tail = [jnp.zeros((math.prod(s),), F32) for s in small_sum_shapes[len(_REPLICATED):]]
    w_pack = _pack([w[n] for n in _REPLICATED] + zero_tail)
    m_pack = _pack([m[n] for n in _REPLICATED] + zero_tail)
    v_pack = _pack([v[n] for n in _REPLICATED] + zero_tail)
    outs = _adamw(w_pack, small_parts, m_pack, v_pack, name="adamw_small")
    summed = _unpack(outs[0], small_sum_shapes)
    for i, n in enumerate(_REPLICATED):
        grads[n] = summed[i].reshape(w[n].shape)
    for dst, o in zip((deltas, new_m, new_v), outs[1:]):
        for n, val in zip(_REPLICATED, _unpack(o[:rep_rows], rep_shapes)):
            dst[n] = val
    loss = summed[-1][0]

    shard_g = []
    for i, n in enumerate(_SMALL_SHARDED):
        full = summed[len(_REPLICATED) + i].reshape(w[n].shape[:-1] + (-1,))
        width = w[n].shape[-1]
        shard_g.append(lax.dynamic_slice_in_dim(full, me * width, width, axis=full.ndim - 1))
    g_sh = _small_shard_pack(shard_g)
    m_sh = _small_shard_pack([m[n] for n in _SMALL_SHARDED])
    v_sh = _small_shard_pack([v[n] for n in _SMALL_SHARDED])
    outs = _adamw(small_sh, g_sh[None], m_sh, v_sh, name="adamw_small_sharded")
    r = 0
    for n in _SMALL_SHARDED:
        rows, width = math.prod(w[n].shape[:-1]), w[n].shape[-1]
        for dst, o in zip((grads, deltas, new_m, new_v), outs):
            dst[n] = o[r:r + rows, :width].reshape(w[n].shape)
        r += rows

    return (loss, grad_x, *[grads[n] for n in _NAMES], *[deltas[n] for n in _NAMES],
            *[new_m[n] for n in _NAMES], *[new_v[n] for n in _NAMES])
```

```python
import functools
import math

import jax
import jax.numpy as jnp
from jax import lax
from jax.experimental import pallas as pl
from jax.experimental.pallas import tpu as pltpu

F32 = jnp.float32
BF16 = jnp.bfloat16
GRAD_WIRE = jnp.bfloat16

NORM_EPS = 1e-6
ADAM_LR = 0.001
ADAM_B1 = 0.9
ADAM_B2 = 0.999
ADAM_EPS = 1e-08
ADAM_WD = 0.01
ADAM_STEP = 10
ADAM_C1 = 1.0 - ADAM_B1 ** ADAM_STEP
ADAM_C2 = 1.0 - ADAM_B2 ** ADAM_STEP

N_DEV = 8
LANES = 128
SUBLANES = 8
SB_DH = 64
ATT_BLOCK = 256
POOL_LOG_WINDOWS = 4
VMEM_LIMIT = 56 * 1024 * 1024


def _params(semantics=None):
    return pltpu.CompilerParams(dimension_semantics=semantics, vmem_limit_bytes=VMEM_LIMIT)


def _tile(dim, pref, unit=LANES):
    if dim <= pref:
        return dim
    t = (pref // unit) * unit
    while t >= unit:
        if dim % t == 0:
            return t
        t -= unit
    return dim


def _matmul(a, b, *, name, ta=False, tb=False, extras=(), epilogue=None, out_dtypes=(F32,),
            tm=1024, tn=1024, tk=1024):
    M, K = (a.shape[1], a.shape[0]) if ta else a.shape
    N = b.shape[0] if tb else b.shape[1]
    assert (b.shape[1] if tb else b.shape[0]) == K, (a.shape, b.shape)
    tm, tn, tk = _tile(M, tm), _tile(N, tn), _tile(K, tk)
    nk = K // tk
    dims = (((0 if ta else 1,), (1 if tb else 0,)), ((), ()))
    ne, no = len(extras), len(out_dtypes)

    def body(*refs):
        a_ref, b_ref = refs[0], refs[1]
        e_refs = refs[2:2 + ne]
        o_refs = refs[2 + ne:2 + ne + no]
        k = pl.program_id(2)

        def part():
            return lax.dot_general(a_ref[...].astype(BF16), b_ref[...].astype(BF16), dims,
                                   preferred_element_type=F32)

        def finish(acc):
            outs = epilogue(acc, *[e[...] for e in e_refs]) if epilogue is not None else (acc,)
            for o_ref, val in zip(o_refs, outs):
                o_ref[...] = val.astype(o_ref.dtype)

        if nk == 1:
            finish(part())
        else:
            acc_ref = refs[-1]

            @pl.when(k == 0)
            def _():
                acc_ref[...] = jnp.zeros_like(acc_ref)

            acc_ref[...] += part()

            @pl.when(k == nk - 1)
            def _():
                finish(acc_ref[...])

    a_spec = (pl.BlockSpec((tk, tm), lambda i, j, k: (k, i)) if ta
              else pl.BlockSpec((tm, tk), lambda i, j, k: (i, k)))
    b_spec = (pl.BlockSpec((tn, tk), lambda i, j, k: (j, k)) if tb
              else pl.BlockSpec((tk, tn), lambda i, j, k: (k, j)))
    o_spec = pl.BlockSpec((tm, tn), lambda i, j, k: (i, j))
    outs = pl.pallas_call(
        body,
        name=name,
        grid=(M // tm, N // tn, nk),
        in_specs=[a_spec, b_spec] + [o_spec] * ne,
        out_specs=[o_spec] * no,
        out_shape=[jax.ShapeDtypeStruct((M, N), dt) for dt in out_dtypes],
        scratch_shapes=[pltpu.VMEM((tm, tn), F32)] if nk > 1 else [],
        compiler_params=_params(("parallel", "parallel", "arbitrary")),
    )(a, b, *extras)
    return outs[0] if no == 1 else outs


def _ep_add(acc, res):
    return (acc + res,)


def _ep_relu2(acc):
    r = jnp.maximum(acc, 0.0)
    return (r * r,)


def _ep_relu2_bwd(acc, act):
    return (acc * (2.0 * jnp.sqrt(act.astype(F32))),)


def _rstd(x):
    return lax.rsqrt(jnp.mean(x * x, axis=-1, keepdims=True) + NORM_EPS)


def _rmsnorm(h, g, *, name, tr=512):
    N, D = h.shape
    tr = _tile(N, tr, SUBLANES)

    def body(h_ref, g_ref, o_ref):
        x = h_ref[...]
        o_ref[...] = ((x * _rstd(x)) * g_ref[...]).astype(o_ref.dtype)

    row = pl.BlockSpec((tr, D), lambda i: (i, 0))
    vec = pl.BlockSpec((1, D), lambda i: (0, 0))
    return pl.pallas_call(
        body, name=name, grid=(N // tr,), in_specs=[row, vec], out_specs=row,
        out_shape=jax.ShapeDtypeStruct((N, D), BF16), compiler_params=_params(("parallel",)),
    )(h, g)


def _rmsnorm_bwd(dy, h, g, dres, *, name, tr=512):
    N, D = h.shape
    tr = _tile(N, tr, SUBLANES)

    def body(dy_ref, h_ref, g_ref, r_ref, dh_ref, dhb_ref, dg_ref):
        i = pl.program_id(0)
        x = h_ref[...]
        d = dy_ref[...]
        r = _rstd(x)
        xh = x * r

        @pl.when(i == 0)
        def _():
            dg_ref[...] = jnp.zeros_like(dg_ref)

        dg_ref[...] += jnp.sum(d * xh, axis=0, keepdims=True)
        dxh = d * g_ref[...]
        dh = r_ref[...] + r * (dxh - xh * jnp.mean(dxh * xh, axis=-1, keepdims=True))
        dh_ref[...] = dh
        dhb_ref[...] = dh.astype(dhb_ref.dtype)

    row = pl.BlockSpec((tr, D), lambda i: (i, 0))
    vec = pl.BlockSpec((1, D), lambda i: (0, 0))
    return pl.pallas_call(
        body, name=name, grid=(N // tr,), in_specs=[row, row, vec, row], out_specs=[row, row, vec],
        out_shape=[jax.ShapeDtypeStruct((N, D), F32), jax.ShapeDtypeStruct((N, D), BF16),
                   jax.ShapeDtypeStruct((1, D), F32)],
        compiler_params=_params(("arbitrary",)),
    )(dy, h, g, dres)


def _final_loss(h, g, target, *, name, tr=512):
    N, D = h.shape
    tr = _tile(N, tr, SUBLANES)

    def body(h_ref, g_ref, t_ref, loss_ref, dh_ref, dhb_ref, dg_ref):
        i = pl.program_id(0)
        x = h_ref[...]
        gg = g_ref[...]
        r = _rstd(x)
        xh = x * r
        err = xh * gg - t_ref[...]

        @pl.when(i == 0)
        def _():
            dg_ref[...] = jnp.zeros_like(dg_ref)
            loss_ref[...] = jnp.zeros_like(loss_ref)

        per_row = jnp.mean(err * err, axis=-1, keepdims=True)
        loss_ref[...] += 0.5 * jnp.sum(per_row, axis=0, keepdims=True)
        dy = err * (1.0 / D)
        dg_ref[...] += jnp.sum(dy * xh, axis=0, keepdims=True)
        dxh = dy * gg
        dh = r * (dxh - xh * jnp.mean(dxh * xh, axis=-1, keepdims=True))
        dh_ref[...] = dh
        dhb_ref[...] = dh.astype(dhb_ref.dtype)

    row = pl.BlockSpec((tr, D), lambda i: (i, 0))
    vec = pl.BlockSpec((1, D), lambda i: (0, 0))
    lvec = pl.BlockSpec((1, LANES), lambda i: (0, 0))
    return pl.pallas_call(
        body, name=name, grid=(N // tr,), in_specs=[row, vec, row], out_specs=[lvec, row, row, vec],
        out_shape=[jax.ShapeDtypeStruct((1, LANES), F32), jax.ShapeDtypeStruct((N, D), F32),
                   jax.ShapeDtypeStruct((N, D), BF16), jax.ShapeDtypeStruct((1, D), F32)],
        compiler_params=_params(("arbitrary",)),
    )(h, g, target)


def _shift_down(x, s):
    t = lax.broadcasted_iota(jnp.int32, x.shape, 0)
    return jnp.where(t >= s, pltpu.roll(x, s, 0), 0.0)


def _shift_up(x, s):
    n = x.shape[0]
    t = lax.broadcasted_iota(jnp.int32, x.shape, 0)
    return jnp.where(t < n - s, pltpu.roll(x, n - s, 0), 0.0)


def _window_sum(x, g, shift):
    s = x + shift(x, 1)
    for k in range(1, POOL_LOG_WINDOWS):
        s = jnp.where(k <= g, s + shift(s, 2 ** k), s)
    return s


def _pool_count(shape, g):
    t = lax.broadcasted_iota(jnp.int32, shape, 0)
    return jnp.minimum(t + 1, lax.shift_left(jnp.int32(2), g)).astype(F32)


def _pool_fwd(p, pool_w, pool_scale, B, T, off, width, *, name):
    G, dh = pool_w.shape[0], pool_w.shape[1]
    assert G == POOL_LOG_WINDOWS and off % dh == 0
    base = off // dh

    def body(a_ref, w_ref, s_ref, o_ref):
        g = pl.program_id(0)
        a = a_ref[...]
        pooled = _window_sum(a, g, _shift_down) / _pool_count(a.shape, g) - a
        m = jnp.dot(pooled.astype(BF16), w_ref[0].astype(BF16), preferred_element_type=F32)
        o_ref[...] = (m * s_ref[0]).astype(o_ref.dtype)

    return pl.pallas_call(
        body, name=name, grid=(G, B),
        in_specs=[pl.BlockSpec((T, dh), lambda g, b: (b, base + g)),
                  pl.BlockSpec((1, dh, dh), lambda g, b: (g, 0, 0)),
                  pl.BlockSpec((1, 1, dh), lambda g, b: (g, 0, 0))],
        out_specs=pl.BlockSpec((T, dh), lambda g, b: (b, g)),
        out_shape=jax.ShapeDtypeStruct((B * T, width), BF16),
        compiler_params=_params(("parallel", "parallel")),
    )(p, pool_w, pool_scale.reshape(G, 1, dh))


def _pool_bwd(p, dcat, pool_w, pool_scale, B, T, off, *, name):
    G, dh = pool_w.shape[0], pool_w.shape[1]
    base = off // dh

    def body(a_ref, d_ref, w_ref, s_ref, da_ref, dw_ref, ds_ref):
        g = pl.program_id(0)
        b = pl.program_id(1)
        a = a_ref[...]
        d = d_ref[...]
        cnt = _pool_count(a.shape, g)
        pooled = (_window_sum(a, g, _shift_down) / cnt - a).astype(BF16)
        w = w_ref[0].astype(BF16)
        m = jnp.dot(pooled, w, preferred_element_type=F32)

        @pl.when(b == 0)
        def _():
            dw_ref[...] = jnp.zeros_like(dw_ref)
            ds_ref[...] = jnp.zeros_like(ds_ref)

        ds_ref[0] += jnp.sum(d * m, axis=0, keepdims=True)
        dm = (d * s_ref[0]).astype(BF16)
        dw_ref[0] += lax.dot_general(pooled, dm, (((0,), (0,)), ((), ())), preferred_element_type=F32)
        dpooled = lax.dot_general(dm, w, (((1,), (1,)), ((), ())), preferred_element_type=F32)
        da = _window_sum(dpooled / cnt, g, _shift_up) - dpooled
        da_ref[...] = da.astype(da_ref.dtype)

    pblk = pl.BlockSpec((T, dh), lambda g, b: (b, base + g))
    dblk = pl.BlockSpec((T, dh), lambda g, b: (b, g))
    wspec = pl.BlockSpec((1, dh, dh), lambda g, b: (g, 0, 0))
    sspec = pl.BlockSpec((1, 1, dh), lambda g, b: (g, 0, 0))
    return pl.pallas_call(
        body, name=name, grid=(G, B), in_specs=[pblk, dblk, wspec, sspec], out_specs=[pblk, wspec, sspec],
        out_shape=[jax.ShapeDtypeStruct((B * T, p.shape[1]), BF16), jax.ShapeDtypeStruct((G, dh, dh), F32),
                   jax.ShapeDtypeStruct((G, 1, dh), F32)],
        compiler_params=_params(("parallel", "arbitrary")),
    )(p, dcat, pool_w, pool_scale.reshape(G, 1, dh))


_ANY = pl.BlockSpec(memory_space=pl.ANY)


def _conv_fwd(p, cat, conv_w, conv_b, B, T, coff, *, name):
    CW = conv_w.shape[1]
    tc = LANES
    assert coff % tc == 0 and CW % tc == 0
    cbase = coff // tc

    def body(p_ref, cat_ref, w_ref, b_ref, o_ref):
        xb, gb, gc = p_ref[:, 0:tc], p_ref[:, tc:2 * tc], p_ref[:, 2 * tc:3 * tc]
        c = gc * xb
        w = w_ref[...]
        y = _shift_down(c, 2) * w[0:1] + _shift_down(c, 1) * w[1:2] + c * w[2:3] + b_ref[...]
        o_ref[...] = (gb * y).astype(o_ref.dtype)

    return pl.pallas_call(
        body, name=name, grid=(CW // tc, B),
        in_specs=[pl.BlockSpec((T, 3 * tc), lambda j, b: (b, j)), _ANY,
                  pl.BlockSpec((3, tc), lambda j, b: (0, j)), pl.BlockSpec((1, tc), lambda j, b: (0, j))],
        out_specs=pl.BlockSpec((T, tc), lambda j, b: (b, cbase + j)),
        out_shape=jax.ShapeDtypeStruct(cat.shape, cat.dtype), input_output_aliases={1: 0},
        compiler_params=_params(("parallel", "parallel")),
    )(p, cat, conv_w, conv_b)


def _conv_bwd(p, dcat, dp, conv_w, conv_b, B, T, coff, *, name):
    CW = conv_w.shape[1]
    tc = LANES
    assert coff % tc == 0
    cbase = coff // tc

    def body(p_ref, d_ref, dp_in_ref, w_ref, b_ref, dp_ref, dw_ref, db_ref):
        b = pl.program_id(1)
        xb, gb, gc = p_ref[:, 0:tc], p_ref[:, tc:2 * tc], p_ref[:, 2 * tc:3 * tc]
        d = d_ref[...]
        w = w_ref[...]
        c = gc * xb
        c1 = _shift_down(c, 1)
        c2 = _shift_down(c, 2)
        y = c2 * w[0:1] + c1 * w[1:2] + c * w[2:3] + b_ref[...]
        dy = d * gb
        dp_ref[:, tc:2 * tc] = (d * y).astype(dp_ref.dtype)

        @pl.when(b == 0)
        def _():
            dw_ref[...] = jnp.zeros_like(dw_ref)
            db_ref[...] = jnp.zeros_like(db_ref)

        db_ref[...] += jnp.sum(dy, axis=0, keepdims=True)
        dw_ref[0:1, :] += jnp.sum(dy * c2, axis=0, keepdims=True)
        dw_ref[1:2, :] += jnp.sum(dy * c1, axis=0, keepdims=True)
        dw_ref[2:3, :] += jnp.sum(dy * c, axis=0, keepdims=True)
        dc = dy * w[2:3] + _shift_up(dy, 1) * w[1:2] + _shift_up(dy, 2) * w[0:1]
        dp_ref[:, 2 * tc:3 * tc] = (dc * xb).astype(dp_ref.dtype)
        dp_ref[:, 0:tc] = (dc * gc).astype(dp_ref.dtype)

    wspec = pl.BlockSpec((3, tc), lambda j, b: (0, j))
    bspec = pl.BlockSpec((1, tc), lambda j, b: (0, j))
    pblk = pl.BlockSpec((T, 3 * tc), lambda j, b: (b, j))
    return pl.pallas_call(
        body, name=name, grid=(CW // tc, B),
        in_specs=[pblk, pl.BlockSpec((T, tc), lambda j, b: (b, cbase + j)), _ANY, wspec, bspec],
        out_specs=[pblk, wspec, bspec],
        out_shape=[jax.ShapeDtypeStruct(dp.shape, dp.dtype), jax.ShapeDtypeStruct((3, CW), F32),
                   jax.ShapeDtypeStruct((1, CW), F32)],
        input_output_aliases={2: 0},
        compiler_params=_params(("parallel", "arbitrary")),
    )(p, dcat, dp, conv_w, conv_b)


_SQRT_HALF = 0.7071067811865476
_INV_SQRT_2PI = 0.3989422804014327


def _gelu(x):
    return x * (lax.erf(x * _SQRT_HALF) + 1.0) * 0.5


def _gelu_grad(x):
    return 0.5 * (lax.erf(x * _SQRT_HALF) + 1.0) + x * (_INV_SQRT_2PI * jnp.exp(-0.5 * x * x))


def _layernorm_parts(v):
    mu = jnp.mean(v, axis=-1, keepdims=True)
    vc = v - mu
    rstd = lax.rsqrt(jnp.mean(vc * vc, axis=-1, keepdims=True) + NORM_EPS)
    return vc * rstd, rstd


def _tril_mask(L):
    r = lax.broadcasted_iota(jnp.int32, (L, L), 0)
    c = lax.broadcasted_iota(jnp.int32, (L, L), 1)
    return r >= c


def _sgu_fwd(p, ln_g, ln_b, sgu_w, sgu_b, off, width, *, name, tr=512):
    N = p.shape[0]
    G, L = sgu_w.shape[0], sgu_w.shape[1]
    SW = ln_g.shape[1]
    dh = SW // G
    assert off % SW == 0
    ub = off // SW
    tr = _tile(N, tr, L)
    assert tr % L == 0

    def body(u_ref, v_ref, g_ref, beta_ref, w_ref, b_ref, o_ref):
        u = _gelu(u_ref[...])
        vhat, _ = _layernorm_parts(_gelu(v_ref[...]))
        vn = (vhat * g_ref[...] + beta_ref[...]).astype(BF16)
        mask = _tril_mask(L)
        for gi in range(G):
            w = jnp.where(mask, w_ref[gi], 0.0).astype(BF16)
            bias = b_ref[gi]
            cols = slice(gi * dh, (gi + 1) * dh)
            for n in range(tr // L):
                rows = slice(n * L, (n + 1) * L)
                s = jnp.dot(w, vn[rows, cols], preferred_element_type=F32) + bias
                o_ref[rows, cols] = (u[rows, cols] * s).astype(o_ref.dtype)

    def col(k):
        return pl.BlockSpec((tr, SW), lambda i: (i, k))

    vec = pl.BlockSpec((1, SW), lambda i: (0, 0))
    return pl.pallas_call(
        body, name=name, grid=(N // tr,),
        in_specs=[col(ub), col(ub + 1), vec, vec, pl.BlockSpec((G, L, L), lambda i: (0, 0, 0)),
                  pl.BlockSpec((G, L, 1), lambda i: (0, 0, 0))],
        out_specs=col(0), out_shape=jax.ShapeDtypeStruct((N, width), BF16),
        compiler_params=_params(("parallel",)),
    )(p, p, ln_g, ln_b, sgu_w, sgu_b.reshape(G, L, 1))


def _sgu_bwd(p, dcat, ln_g, ln_b, sgu_w, sgu_b, off, *, name, tr=512):
    N = p.shape[0]
    G, L = sgu_w.shape[0], sgu_w.shape[1]
    SW = ln_g.shape[1]
    dh = SW // G
    assert off % SW == 0
    ub = off // SW
    tr = _tile(N, tr, L)

    def compute(i, u_ref, v_ref, dc_ref, g_ref, beta_ref, w_ref, b_ref,
                du_ref, dv_ref, dw_ref, db_ref, dg_ref, dbeta_ref, du_s, dvn_s):
        pu = u_ref[...]
        pv = v_ref[...]
        u = _gelu(pu)
        vhat, rstd = _layernorm_parts(_gelu(pv))
        gg = g_ref[...]
        vn = (vhat * gg + beta_ref[...]).astype(BF16)
        dc = dc_ref[...]
        mask = _tril_mask(L)

        @pl.when(i == 0)
        def _():
            dw_ref[...] = jnp.zeros_like(dw_ref)
            db_ref[...] = jnp.zeros_like(db_ref)
            dg_ref[...] = jnp.zeros_like(dg_ref)
            dbeta_ref[...] = jnp.zeros_like(dbeta_ref)

        for gi in range(G):
            w = jnp.where(mask, w_ref[gi], 0.0).astype(BF16)
            bias = b_ref[gi]
            cols = slice(gi * dh, (gi + 1) * dh)
            dw_acc = jnp.zeros((L, L), F32)
            db_acc = jnp.zeros((L, 1), F32)
            for n in range(tr // L):
                rows = slice(n * L, (n + 1) * L)
                vb = vn[rows, cols]
                s = jnp.dot(w, vb, preferred_element_type=F32) + bias
                du_s[rows, cols] = dc[rows, cols] * s
                ds = dc[rows, cols] * u[rows, cols]
                db_acc += jnp.sum(ds, axis=1, keepdims=True)
                dsb = ds.astype(BF16)
                dw_acc += lax.dot_general(dsb, vb, (((1,), (1,)), ((), ())), preferred_element_type=F32)
                dvn_s[rows, cols] = lax.dot_general(w, dsb, (((0,), (0,)), ((), ())),
                                                    preferred_element_type=F32)
            dw_ref[gi] += jnp.where(mask, dw_acc, 0.0)
            db_ref[gi] += db_acc

        dvn = dvn_s[...]
        dg_ref[...] += jnp.sum(dvn * vhat, axis=0, keepdims=True)
        dbeta_ref[...] += jnp.sum(dvn, axis=0, keepdims=True)
        dvh = dvn * gg
        dv = rstd * (dvh - jnp.mean(dvh, axis=-1, keepdims=True)
                     - vhat * jnp.mean(dvh * vhat, axis=-1, keepdims=True))
        dv_ref[...] = (dv * _gelu_grad(pv)).astype(dv_ref.dtype)
        du_ref[...] = (du_s[...] * _gelu_grad(pu)).astype(du_ref.dtype)

    def body(u_ref, v_ref, dc_ref, g_ref, beta_ref, w_ref, b_ref,
             dp_ref, dw_ref, db_ref, dg_ref, dbeta_ref, du_s, dvn_s, dv_s):
        i = pl.program_id(0)
        half = pl.program_id(1)

        @pl.when(half == 0)
        def _():
            compute(i, u_ref, v_ref, dc_ref, g_ref, beta_ref, w_ref, b_ref,
                    dp_ref, dv_s, dw_ref, db_ref, dg_ref, dbeta_ref, du_s, dvn_s)

        @pl.when(half == 1)
        def _():
            dp_ref[...] = dv_s[...]

    def col(k):
        return pl.BlockSpec((tr, SW), lambda i, half: (i, k))

    vec = pl.BlockSpec((1, SW), lambda i, half: (0, 0))
    wspec = pl.BlockSpec((G, L, L), lambda i, half: (0, 0, 0))
    bspec = pl.BlockSpec((G, L, 1), lambda i, half: (0, 0, 0))
    return pl.pallas_call(
        body, name=name, grid=(N // tr, 2),
        in_specs=[col(ub), col(ub + 1), col(0), vec, vec, wspec, bspec],
        out_specs=[pl.BlockSpec((tr, SW), lambda i, half: (i, ub + half)), wspec, bspec, vec, vec],
        out_shape=[jax.ShapeDtypeStruct((N, p.shape[1]), BF16),
                   jax.ShapeDtypeStruct((G, L, L), F32), jax.ShapeDtypeStruct((G, L, 1), F32),
                   jax.ShapeDtypeStruct((1, SW), F32), jax.ShapeDtypeStruct((1, SW), F32)],
        scratch_shapes=[pltpu.VMEM((tr, SW), F32), pltpu.VMEM((tr, SW), F32), pltpu.VMEM((tr, SW), BF16)],
        compiler_params=_params(("arbitrary", "arbitrary")),
    )(p, p, dcat, ln_g, ln_b, sgu_w, sgu_b.reshape(G, L, 1))


def _log_sigmoid_pair(z):
    ls = jnp.minimum(z, 0.0) - jnp.log(1.0 + jnp.exp(-jnp.abs(z)))
    return ls, ls - z


def _split_dot(x, m):
    hi = x.astype(BF16)
    lo = (x - hi.astype(F32)).astype(BF16)
    return (jnp.dot(hi, m, preferred_element_type=F32) + jnp.dot(lo, m, preferred_element_type=F32))


def _att_tiles(TB):
    r = lax.broadcasted_iota(jnp.int32, (TB, TB), 0)
    c = lax.broadcasted_iota(jnp.int32, (TB, TB), 1)
    return r, c


def _att_weights(qt, kb, strict, later, carry):
    z = lax.dot_general(qt, kb, (((1,), (1,)), ((), ())), preferred_element_type=F32)
    ls, lk = _log_sigmoid_pair(z)
    if strict is not None:
        lk = jnp.where(strict, lk, 0.0)
    suffix = _split_dot(lk, later) + carry
    a = jnp.exp(ls + suffix)
    if strict is not None:
        a = jnp.where(strict, a, 0.0)
    return ls, lk, suffix, a


HP = LANES // SB_DH
ATT_SCALE = 1.0 / math.sqrt(SB_DH)


def _stage_heads(src_ref, col0, dst_ref, T, scale=None):
    rows = _tile(T, 256, SUBLANES)

    def chunk(n, _):
        r0 = pl.multiple_of(n * rows, rows)
        for hh in range(HP):
            x = src_ref[pl.ds(r0, rows), col0 + hh * SB_DH:col0 + (hh + 1) * SB_DH]
            if scale is not None:
                x = x * scale
            dst_ref[hh, pl.ds(r0, rows), :] = x.astype(dst_ref.dtype)
        return 0

    lax.fori_loop(0, T // rows, chunk, 0)


def _attn_fwd(p, cat, B, T, coff, *, name):
    dh = SB_DH
    nhp = (cat.shape[1] - coff) // LANES
    TB = _tile(T, ATT_BLOCK)
    nb = T // TB
    assert nb <= LANES and coff % LANES == 0
    cbase = coff // LANES

    def body(p_ref, cat_ref, o_ref, c_ref, q_ref, k_ref, v_ref):
        _stage_heads(p_ref, 0, q_ref, T, ATT_SCALE)
        _stage_heads(p_ref, LANES, k_ref, T)
        _stage_heads(p_ref, 2 * LANES, v_ref, T)
        r, c = _att_tiles(TB)
        strict = c < r
        later = (r > c).astype(BF16)
        lane = lax.broadcasted_iota(jnp.int32, (TB, LANES), 1)

        def tile(hh, qt, j, carry, mask):
            k0 = pl.multiple_of(j * TB, TB)
            _, lk, suffix, a = _att_weights(qt, k_ref[hh, pl.ds(k0, TB), :], mask, later, carry)
            pv = jnp.dot(a.astype(BF16), v_ref[hh, pl.ds(k0, TB), :], preferred_element_type=F32)
            return pv, suffix[:, 0:1] + lk[:, 0:1]

        def qblock(i, _):
            q0 = pl.multiple_of(i * TB, TB)
            qts = [q_ref[hh, pl.ds(q0, TB), :] for hh in range(HP)]
            state = []
            for hh in range(HP):
                pv, carry = tile(hh, qts[hh], i, jnp.zeros((TB, 1), F32), strict)
                state += [pv, carry, jnp.zeros((TB, LANES), F32)]

            def kblock(jj, st):
                j = i - jj
                out = []
                for hh in range(HP):
                    acc, carry, cm = st[3 * hh:3 * hh + 3]
                    pv, new_carry = tile(hh, qts[hh], j, carry, None)
                    out += [acc + pv, new_carry, jnp.where(lane == j, carry, cm)]
                return tuple(out)

            st = lax.fori_loop(1, i + 1, kblock, tuple(state))
            for hh in range(HP):
                o_ref[pl.ds(q0, TB), hh * dh:(hh + 1) * dh] = st[3 * hh].astype(o_ref.dtype)
                c_ref[hh, pl.ds(q0, TB), :] = st[3 * hh + 2]
            return 0

        lax.fori_loop(0, nb, qblock, 0)

    staged = pltpu.VMEM((HP, T, dh), BF16)
    return pl.pallas_call(
        body, name=name, grid=(B, nhp),
        in_specs=[pl.BlockSpec((T, 3 * LANES), lambda b, hp: (b, hp)), _ANY],
        out_specs=[pl.BlockSpec((T, LANES), lambda b, hp: (b, cbase + hp)),
                   pl.BlockSpec((HP, T, LANES), lambda b, hp: (b * nhp + hp, 0, 0))],
        out_shape=[jax.ShapeDtypeStruct(cat.shape, cat.dtype),
                   jax.ShapeDtypeStruct((B * nhp * HP, T, LANES), F32)],
        input_output_aliases={1: 0}, scratch_shapes=[staged, staged, staged],
        compiler_params=_params(("parallel", "parallel")),
    )(p, cat)


def _attn_bwd(p, dcat, carries, dp, B, T, coff, *, name):
    dh = SB_DH
    nhp = (dcat.shape[1] - coff) // LANES
    TB = _tile(T, ATT_BLOCK)
    nb = T // TB
    cbase = coff // LANES

    def body(p_ref, d_ref, c_ref, dp_in_ref, dp_ref, q_ref, k_ref, v_ref, do_ref, dk_ref, dv_ref):
        _stage_heads(p_ref, 0, q_ref, T, ATT_SCALE)
        _stage_heads(p_ref, LANES, k_ref, T)
        _stage_heads(p_ref, 2 * LANES, v_ref, T)
        _stage_heads(d_ref, 0, do_ref, T)
        r, c = _att_tiles(TB)
        strict = c < r
        later = (r > c).astype(BF16)
        earlier = (r < c).astype(BF16)
        lane = lax.broadcasted_iota(jnp.int32, (TB, LANES), 1)
        dk_ref[...] = jnp.zeros_like(dk_ref)
        dv_ref[...] = jnp.zeros_like(dv_ref)

        def tile(hh, qt, dot, cm, j, before, mask):
            k0 = pl.multiple_of(j * TB, TB)
            kb = k_ref[hh, pl.ds(k0, TB), :]
            carry = jnp.sum(jnp.where(lane == j, cm, 0.0), axis=1, keepdims=True)
            ls, _, _, a = _att_weights(qt, kb, mask, later, carry)
            dv_ref[hh, pl.ds(k0, TB), :] += lax.dot_general(a.astype(BF16), dot, (((0,), (0,)), ((), ())),
                                                            preferred_element_type=F32)
            da = lax.dot_general(dot, v_ref[hh, pl.ds(k0, TB), :], (((1,), (1,)), ((), ())),
                                 preferred_element_type=F32)
            gl = a * da
            prefix = _split_dot(gl, earlier) + before
            dz = gl - jnp.exp(ls) * (gl + prefix)
            if mask is not None:
                dz = jnp.where(mask, dz, 0.0)
            dzb = dz.astype(BF16)
            dk_ref[hh, pl.ds(k0, TB), :] += lax.dot_general(dzb, qt, (((0,), (0,)), ((), ())),
                                                            preferred_element_type=F32)
            return jnp.dot(dzb, kb, preferred_element_type=F32), prefix[:, TB - 1:TB] + gl[:, TB - 1:TB]

        def qblock(i, _):
            q0 = pl.multiple_of(i * TB, TB)
            qts = [q_ref[hh, pl.ds(q0, TB), :] for hh in range(HP)]
            dots = [do_ref[hh, pl.ds(q0, TB), :] for hh in range(HP)]
            cms = [c_ref[hh, pl.ds(q0, TB), :] for hh in range(HP)]

            def kblock(j, st):
                out = []
                for hh in range(HP):
                    dq, before = st[2 * hh:2 * hh + 2]
                    part, new_before = tile(hh, qts[hh], dots[hh], cms[hh], j, before, None)
                    out += [dq + part, new_before]
                return tuple(out)

            st = lax.fori_loop(0, i, kblock, (jnp.zeros((TB, dh), F32), jnp.zeros((TB, 1), F32)) * HP)
            for hh in range(HP):
                part, _ = tile(hh, qts[hh], dots[hh], cms[hh], i, st[2 * hh + 1], strict)
                dq = (st[2 * hh] + part) * ATT_SCALE
                dp_ref[pl.ds(q0, TB), hh * dh:(hh + 1) * dh] = dq.astype(dp_ref.dtype)
            return 0

        lax.fori_loop(0, nb, qblock, 0)

        def write_back(n, _):
            r0 = pl.multiple_of(n * TB, TB)
            for hh in range(HP):
                dp_ref[pl.ds(r0, TB), LANES + hh * dh:LANES + (hh + 1) * dh] = (
                    dk_ref[hh, pl.ds(r0, TB), :].astype(dp_ref.dtype))
                dp_ref[pl.ds(r0, TB), 2 * LANES + hh * dh:2 * LANES + (hh + 1) * dh] = (
                    dv_ref[hh, pl.ds(r0, TB), :].astype(dp_ref.dtype))
            return 0

        lax.fori_loop(0, nb, write_back, 0)

    pblk = pl.BlockSpec((T, 3 * LANES), lambda b, hp: (b, hp))
    staged = pltpu.VMEM((HP, T, dh), BF16)
    accum = pltpu.VMEM((HP, T, dh), F32)
    return pl.pallas_call(
        body, name=name, grid=(B, nhp),
        in_specs=[pblk, pl.BlockSpec((T, LANES), lambda b, hp: (b, cbase + hp)),
                  pl.BlockSpec((HP, T, LANES), lambda b, hp: (b * nhp + hp, 0, 0)), _ANY],
        out_specs=pblk, out_shape=jax.ShapeDtypeStruct(dp.shape, dp.dtype), input_output_aliases={3: 0},
        scratch_shapes=[staged, staged, staged, staged, accum, accum],
        compiler_params=_params(("parallel", "parallel")),
    )(p, dcat, carries, dp)


def _adamw(w, gparts, m, v, *, name):
    R, C = w.shape
    P = gparts.shape[0]
    tr = _tile(R, max(SUBLANES, (1 << 19) // (C * P)), SUBLANES)

    def body(w_ref, g_ref, m_ref, v_ref, go_ref, d_ref, mo_ref, vo_ref):
        g = g_ref[0].astype(F32)
        for i in range(1, P):
            g = g + g_ref[i].astype(F32)
        m2 = ADAM_B1 * m_ref[...] + (1.0 - ADAM_B1) * g
        v2 = ADAM_B2 * v_ref[...] + (1.0 - ADAM_B2) * (g * g)
        m_hat = m2 / ADAM_C1
        v_hat = v2 / ADAM_C2
        go_ref[...] = g
        d_ref[...] = -ADAM_LR * (m_hat / (jnp.sqrt(v_hat) + ADAM_EPS) + ADAM_WD * w_ref[...])
        mo_ref[...] = m2
        vo_ref[...] = v2

    row = pl.BlockSpec((tr, C), lambda i: (i, 0))
    shp = jax.ShapeDtypeStruct((R, C), F32)
    return pl.pallas_call(
        body, name=name, grid=(R // tr,),
        in_specs=[row, pl.BlockSpec((P, tr, C), lambda i: (0, i, 0)), row, row],
        out_specs=[row] * 4, out_shape=[shp] * 4, compiler_params=_params(("parallel",)),
    )(w, gparts, m, v)


def _my_index():
    return 4 * lax.axis_index("x") + 2 * lax.axis_index("y") + lax.axis_index("c")


def _exchange(arrs, gather, *, name):
    n = len(arrs)

    def body(*refs):
        ins, outs = refs[:n], refs[n:2 * n]
        send_sems, recv_sems, local_sems = refs[2 * n:]
        x, y, c = lax.axis_index("x"), lax.axis_index("y"), lax.axis_index("c")
        me = 4 * x + 2 * y + c
        remote, local = [], []
        for a in range(n):
            own = ins[a] if gather[a] else ins[a].at[me]
            cp = pltpu.make_async_copy(own, outs[a].at[me], local_sems.at[a])
            cp.start()
            local.append(cp)
            for k in range(1, N_DEV):
                px = 1 - x if k & 4 else x
                py = 1 - y if k & 2 else y
                pc = 1 - c if k & 1 else c
                src = ins[a] if gather[a] else ins[a].at[4 * px + 2 * py + pc]
                cp = pltpu.make_async_remote_copy(
                    src_ref=src, dst_ref=outs[a].at[me],
                    send_sem=send_sems.at[a, k - 1], recv_sem=recv_sems.at[a, k - 1],
                    device_id=(px, py, pc), device_id_type=pl.DeviceIdType.MESH)
                cp.start()
                remote.append(cp)
        for cp in remote:
            cp.wait()
        for cp in local:
            cp.wait()

    hbm = pl.BlockSpec(memory_space=pltpu.HBM)
    out_shape = [jax.ShapeDtypeStruct(((N_DEV,) + a.shape) if g else a.shape, a.dtype)
                 for a, g in zip(arrs, gather)]
    return pl.pallas_call(
        body, name=name, in_specs=[hbm] * n, out_specs=[hbm] * n, out_shape=out_shape,
        scratch_shapes=[pltpu.SemaphoreType.DMA((n, N_DEV - 1)), pltpu.SemaphoreType.DMA((n, N_DEV - 1)),
                        pltpu.SemaphoreType.DMA((n,))],
    )(*arrs)


_HBM = pl.BlockSpec(memory_space=pltpu.HBM)


def _other_chips(x, y):
    return [(1 - x, y), (x, 1 - y), (1 - x, 1 - y)]


def _gather_two_level(arrs, *, name):
    n = len(arrs)

    def body(*refs):
        ins, outs = refs[:n], refs[n:2 * n]
        send_sems, recv_sems, local_sems = refs[2 * n:]
        x, y, c = lax.axis_index("x"), lax.axis_index("y"), lax.axis_index("c")
        me, sibling = (x, y, c), (x, y, 1 - c)
        chips = _other_chips(x, y)

        def slot(a, px, py, pc):
            return outs[a].at[4 * px + 2 * py + pc]

        def copy(a, k, block, to, src=None):
            return pltpu.make_async_remote_copy(
                src_ref=slot(a, *block) if src is None else src, dst_ref=slot(a, *block),
                send_sem=send_sems.at[a, k], recv_sem=recv_sems.at[a, k],
                device_id=to, device_id_type=pl.DeviceIdType.MESH)

        local, sends = [], []
        for a in range(n):
            cp = pltpu.make_async_copy(ins[a], slot(a, *me), local_sems.at[a])
            cp.start()
            local.append(cp)
            first = [copy(a, 0, me, sibling, src=ins[a])]
            first += [copy(a, 1 + j, me, (*chip, c), src=ins[a]) for j, chip in enumerate(chips)]
            for cp in first:
                cp.start()
            sends += first
        for j, chip in enumerate(chips):
            for a in range(n):
                copy(a, 1 + j, (*chip, c), me).wait_recv()
                cp = copy(a, 4 + j, (*chip, c), sibling)
                cp.start()
                sends.append(cp)
        for a in range(n):
            copy(a, 0, sibling, me).wait_recv()
            for j, chip in enumerate(chips):
                copy(a, 4 + j, (*chip, 1 - c), me).wait_recv()
        for cp in sends:
            cp.wait_send()
        for cp in local:
            cp.wait()

    return pl.pallas_call(
        body, name=name, in_specs=[_HBM] * n, out_specs=[_HBM] * n,
        out_shape=[jax.ShapeDtypeStruct((N_DEV,) + a.shape, a.dtype) for a in arrs],
        scratch_shapes=[pltpu.SemaphoreType.DMA((n, N_DEV - 1)), pltpu.SemaphoreType.DMA((n, N_DEV - 1)),
                        pltpu.SemaphoreType.DMA((n,))],
    )(*arrs)


_SEM = pl.BlockSpec(memory_space=pltpu.SEMAPHORE)
_SPLIT_COPY = pltpu.SideEffectType.DATAFLOW_SIDE_EFFECTING


def _peers(x, y, c):
    return [((1 - x if k & 4 else x), (1 - y if k & 2 else y), (1 - c if k & 1 else c)) for k in range(1, N_DEV)]


_SPLIT_SEMS = 2 * (N_DEV - 1) + 1


def _split_sems(sems, a):
    mine = sems[a * _SPLIT_SEMS:(a + 1) * _SPLIT_SEMS]
    return mine[:N_DEV - 1], mine[N_DEV - 1:2 * (N_DEV - 1)], mine[-1]


def _gather_start(arrs, *, name):
    n = len(arrs)
    ns = n * _SPLIT_SEMS

    def body(*refs):
        ins, lands = refs[:n], refs[n:2 * n]
        sems = refs[2 * n:2 * n + ns]
        token = refs[-1]
        x, y, c = lax.axis_index("x"), lax.axis_index("y"), lax.axis_index("c")
        me = 4 * x + 2 * y + c
        for a in range(n):
            send, recv, local = _split_sems(sems, a)
            pltpu.make_async_copy(ins[a], lands[a].at[me], local).start()
            for k, peer in enumerate(_peers(x, y, c)):
                pltpu.make_async_remote_copy(
                    src_ref=ins[a], dst_ref=lands[a].at[me], send_sem=send[k], recv_sem=recv[k],
                    device_id=peer, device_id_type=pl.DeviceIdType.MESH).start()
        token[...] = jnp.zeros_like(token)

    lands = [lax.empty((N_DEV,) + a.shape, a.dtype) for a in arrs]
    operands = [pltpu.with_memory_space_constraint(a, pltpu.HBM) for a in list(arrs) + lands]
    outs = pl.pallas_call(
        body, name=name, in_specs=[_HBM] * (2 * n),
        out_specs=[_SEM] * ns + [_HBM] * (2 * n) + [pl.BlockSpec(memory_space=pltpu.VMEM)],
        out_shape=[pltpu.SemaphoreType.DMA(())] * ns + [pltpu.HBM(a.shape, a.dtype) for a in operands]
        + [jax.ShapeDtypeStruct((SUBLANES, LANES), F32)],
        input_output_aliases={i: ns + i for i in range(2 * n)},
        compiler_params=pltpu.CompilerParams(has_side_effects=_SPLIT_COPY),
    )(*operands)
    return tuple(outs[:-1]), outs[-1]


def _gather_wait(handles, after, *, name):
    n = len(handles) // (_SPLIT_SEMS + 2)
    ns = n * _SPLIT_SEMS
    sems, thru = handles[:ns], handles[ns:]

    def body(*refs):
        ins, lands = refs[:n], refs[n:2 * n]
        sems = refs[2 * n:2 * n + ns]
        x, y, c = lax.axis_index("x"), lax.axis_index("y"), lax.axis_index("c")
        me = 4 * x + 2 * y + c
        for a in range(n):
            send, recv, local = _split_sems(sems, a)
            pltpu.make_async_copy(ins[a], lands[a].at[me], local).wait()
            for k, peer in enumerate(_peers(x, y, c)):
                cp = pltpu.make_async_remote_copy(
                    src_ref=ins[a], dst_ref=lands[a].at[me], send_sem=send[k], recv_sem=recv[k],
                    device_id=peer, device_id_type=pl.DeviceIdType.MESH)
                cp.wait_send()
                cp.wait_recv()

    outs = pl.pallas_call(
        body, name=name, in_specs=[_HBM] * (2 * n) + [_SEM] * ns + [_ANY], out_specs=[_HBM] * (2 * n),
        out_shape=[pltpu.HBM(a.shape, a.dtype) for a in thru],
        input_output_aliases={i: i for i in range(2 * n)},
        compiler_params=pltpu.CompilerParams(has_side_effects=_SPLIT_COPY),
    )(*thru, *sems, after)
    return outs[n:]


def _sibling_swap(arrs, *, name):
    n = len(arrs)
    nchip = N_DEV // 2

    def body(*refs):
        ins, outs = refs[:n], refs[n:2 * n]
        send_sems, recv_sems = refs[2 * n:]
        x, y, c = lax.axis_index("x"), lax.axis_index("y"), lax.axis_index("c")
        copies = []
        for a in range(n):
            for k in range(nchip):
                cp = pltpu.make_async_remote_copy(
                    src_ref=ins[a].at[2 * k + 1 - c], dst_ref=outs[a].at[k],
                    send_sem=send_sems.at[a, k], recv_sem=recv_sems.at[a, k],
                    device_id=(x, y, 1 - c), device_id_type=pl.DeviceIdType.MESH)
                cp.start()
                copies.append(cp)
        for cp in copies:
            cp.wait()

    return pl.pallas_call(
        body, name=name, in_specs=[_HBM] * n, out_specs=[_HBM] * n,
        out_shape=[jax.ShapeDtypeStruct((nchip,) + a.shape[1:], a.dtype) for a in arrs],
        scratch_shapes=[pltpu.SemaphoreType.DMA((n, nchip)), pltpu.SemaphoreType.DMA((n, nchip))],
    )(*arrs)


def _chip_exchange(arrs, *, name):
    n = len(arrs)

    def body(*refs):
        ins, outs = refs[:n], refs[n:2 * n]
        send_sems, recv_sems, local_sems = refs[2 * n:]
        x, y, c = lax.axis_index("x"), lax.axis_index("y"), lax.axis_index("c")
        mine = 2 * x + y
        copies = []
        for a in range(n):
            cp = pltpu.make_async_copy(ins[a].at[mine], outs[a].at[mine], local_sems.at[a])
            cp.start()
            copies.append(cp)
            for j, (px, py) in enumerate(_other_chips(x, y)):
                cp = pltpu.make_async_remote_copy(
                    src_ref=ins[a].at[2 * px + py], dst_ref=outs[a].at[mine],
                    send_sem=send_sems.at[a, j], recv_sem=recv_sems.at[a, j],
                    device_id=(px, py, c), device_id_type=pl.DeviceIdType.MESH)
                cp.start()
                copies.append(cp)
        for cp in copies:
            cp.wait()

    return pl.pallas_call(
        body, name=name, in_specs=[_HBM] * n, out_specs=[_HBM] * n,
        out_shape=[jax.ShapeDtypeStruct(a.shape, a.dtype) for a in arrs],
        scratch_shapes=[pltpu.SemaphoreType.DMA((n, 3)), pltpu.SemaphoreType.DMA((n, 3)),
                        pltpu.SemaphoreType.DMA((n,))],
    )(*arrs)


def _pair_sum(a, b, *, name):
    P, R, C = a.shape
    tr = _tile(R, max(SUBLANES, (1 << 19) // C), SUBLANES)

    def body(a_ref, b_ref, o_ref):
        o_ref[...] = (a_ref[...].astype(F32) + b_ref[...].astype(F32)).astype(o_ref.dtype)

    blk = pl.BlockSpec((1, tr, C), lambda p, i: (p, i, 0))
    return pl.pallas_call(
        body, name=name, grid=(P, R // tr), in_specs=[blk, blk], out_specs=blk,
        out_shape=jax.ShapeDtypeStruct(a.shape, a.dtype), compiler_params=_params(("parallel", "parallel")),
    )(a, b)


def _group_in_cols(w, lead):
    X = (w.shape[-1] - lead) // 3
    g = w[..., lead:].reshape(w.shape[:-1] + (3, X // LANES, LANES))
    g = jnp.swapaxes(g, -3, -2).reshape(w.shape[:-1] + (3 * X,))
    return jnp.concatenate([g, w[..., :lead]], axis=-1)


def _ungroup_in_cols(w, lead):
    X = (w.shape[-1] - lead) // 3
    g = w[..., :3 * X].reshape(w.shape[:-1] + (X // LANES, 3, LANES))
    g = jnp.swapaxes(g, -3, -2).reshape(w.shape[:-1] + (3 * X,))
    return jnp.concatenate([w[..., 3 * X:], g], axis=-1)


def _mlp_fwd(h, g, w1, w2, tag):
    hn = _rmsnorm(h, g, name=f"mlp_norm_{tag}")
    act = _matmul(hn, w1, name=f"mlp_up_{tag}", epilogue=_ep_relu2, out_dtypes=(BF16,))
    out = _matmul(act, w2, name=f"mlp_down_{tag}", extras=(h,), epilogue=_ep_add)
    return out, (hn, act)


def _mlp_bwd(dout, dout_b, h, g, w1, w2, saved, tag):
    hn, act = saved
    dw2 = _matmul(act, dout_b, ta=True, name=f"mlp_dw2_{tag}", out_dtypes=(GRAD_WIRE,))
    dz = _matmul(dout_b, w2, tb=True, name=f"mlp_dact_{tag}", extras=(act,), epilogue=_ep_relu2_bwd,
                 out_dtypes=(BF16,))
    dw1 = _matmul(hn, dz, ta=True, name=f"mlp_dw1_{tag}", out_dtypes=(GRAD_WIRE,))
    dhn = _matmul(dz, w1, tb=True, name=f"mlp_dhn_{tag}")
    dh, dh_b, dg = _rmsnorm_bwd(dhn, h, g, dout, name=f"mlp_norm_bwd_{tag}")
    return dh, dh_b, dg, dw1, dw2


class _Lazy:
    def __init__(self, handles, finish, name):
        self.handles, self.finish, self.name, self.done = handles, finish, name, None

    def take(self, after):
        if self.done is None:
            self.done = self.finish(_gather_wait(self.handles, after, name=self.name))
        return self.done


def _weight(W, n, after):
    if isinstance(W[n], _Lazy):
        W.update(W[n].take(after))
    return W[n]


def _local_step(x, target, W):
    B, T, D = x.shape
    N = B * T
    G = {}
    row = lambda vec: vec.reshape(1, -1)
    PW = W["pool_w"].shape[0] * W["pool_w"].shape[1]
    CW = W["conv_b"].shape[-1]
    SW = W["sgu_norm_g"].shape[-1]
    HW = W["att_width"]

    h0 = x.reshape(N, D)
    xn0 = _rmsnorm(h0, row(W["mix_norm_g"][0]), name="mix_norm_0")
    p0 = _matmul(xn0, W["ab_w_in"], name="ab_in")
    cat0 = _pool_fwd(p0, W["pool_w"], W["pool_scale"], B, T, 3 * CW, PW + CW, name="pool_fwd")
    cat0 = _conv_fwd(p0, cat0, W["conv_w"], row(W["conv_b"]), B, T, PW, name="conv_fwd")
    h1 = _matmul(cat0, W["ab_w_out"], name="ab_out", extras=(h0,), epilogue=_ep_add)
    h2, mlp0 = _mlp_fwd(h1, row(W["mlp_norm_g"][0]), _weight(W, "mlp_w1_0", h1), W["mlp_w2_0"], 0)
    xn1 = _rmsnorm(h2, row(W["mix_norm_g"][1]), name="mix_norm_1")
    p1 = _matmul(xn1, _weight(W, "cd_w_in", xn1), name="cd_in")
    ln_g, ln_b = row(W["sgu_norm_g"]), row(W["sgu_norm_b"])
    cat1 = _sgu_fwd(p1, ln_g, ln_b, W["sgu_w"], W["sgu_b"], 3 * HW, SW + HW, name="sgu_fwd")
    cat1, att_carries = _attn_fwd(p1, cat1, B, T, SW, name="attn_fwd")
    h3 = _matmul(cat1, W["cd_w_out"], name="cd_out", extras=(h2,), epilogue=_ep_add)
    h4, mlp1 = _mlp_fwd(h3, row(W["mlp_norm_g"][1]), _weight(W, "mlp_w1_1", h3), W["mlp_w2_1"], 1)

    loss, dh4, dh4_b, G["final_norm_g"] = _final_loss(h4, row(W["final_norm_g"]), target.reshape(N, D),
                                                      name="final_loss")

    dh3, dh3_b, dmlp_g1, dw1_1, dw2_1 = _mlp_bwd(dh4, dh4_b, h3, row(W["mlp_norm_g"][1]), W["mlp_w1_1"],
                                                 W["mlp_w2_1"], mlp1, 1)
    G["cd_w_out"] = _matmul(cat1, dh3_b, ta=True, name="cd_out_dw", out_dtypes=(GRAD_WIRE,))[None]
    dcat1 = _matmul(dh3_b, W["cd_w_out"], tb=True, name="cd_out_dx")
    dp1, G["sgu_w"], dsgu_b, G["sgu_norm_g"], G["sgu_norm_b"] = _sgu_bwd(
        p1, dcat1, ln_g, ln_b, W["sgu_w"], W["sgu_b"], 3 * HW, name="sgu_bwd")
    G["sgu_b"] = dsgu_b.reshape(W["sgu_b"].shape)
    dp1 = _attn_bwd(p1, dcat1, att_carries, dp1, B, T, SW, name="attn_bwd")
    G["cd_w_in"] = _matmul(xn1, dp1, ta=True, name="cd_in_dw", out_dtypes=(GRAD_WIRE,))[None]
    dxn1 = _matmul(dp1, W["cd_w_in"], tb=True, name="cd_in_dx")
    dh2, dh2_b, dmix_g1 = _rmsnorm_bwd(dxn1, h2, row(W["mix_norm_g"][1]), dh3, name="mix_norm_bwd_1")

    dh1, dh1_b, dmlp_g0, dw1_0, dw2_0 = _mlp_bwd(dh2, dh2_b, h1, row(W["mlp_norm_g"][0]), W["mlp_w1_0"],
                                                 W["mlp_w2_0"], mlp0, 0)
    G["ab_w_out"] = _matmul(cat0, dh1_b, ta=True, name="ab_out_dw", out_dtypes=(GRAD_WIRE,))[None]
    dcat0 = _matmul(dh1_b, W["ab_w_out"], tb=True, name="ab_out_dx")
    dp0, G["pool_w"], dps = _pool_bwd(p0, dcat0, W["pool_w"], W["pool_scale"], B, T, 3 * CW, name="pool_bwd")
    G["pool_scale"] = dps.reshape(W["pool_scale"].shape)
    dp0, G["conv_w"], dcb = _conv_bwd(p0, dcat0, dp0, W["conv_w"], row(W["conv_b"]), B, T, PW, name="conv_bwd")
    G["conv_b"] = dcb.reshape(-1)
    G["ab_w_in"] = _matmul(xn0, dp0, ta=True, name="ab_in_dw", out_dtypes=(GRAD_WIRE,))[None]
    dxn0 = _matmul(dp0, W["ab_w_in"], tb=True, name="ab_in_dx")
    dx, _, dmix_g0 = _rmsnorm_bwd(dxn0, h0, row(W["mix_norm_g"][0]), dh1, name="mix_norm_bwd_0")

    G["mix_norm_g"] = jnp.concatenate([dmix_g0, dmix_g1], axis=0)
    G["mlp_norm_g"] = jnp.concatenate([dmlp_g0, dmlp_g1], axis=0)
    G["mlp_w1"] = jnp.stack([dw1_0, dw1_1])
    G["mlp_w2"] = jnp.stack([dw2_0, dw2_1])
    G["final_norm_g"] = G["final_norm_g"].reshape(-1)
    G["sgu_norm_g"] = G["sgu_norm_g"].reshape(-1)
    G["sgu_norm_b"] = G["sgu_norm_b"].reshape(-1)
    return loss[0, 0], dx.reshape(B, T, D), G


_NAMES = ["mix_norm_g", "mlp_norm_g", "ab_w_in", "pool_w", "pool_scale", "conv_w", "conv_b", "ab_w_out",
          "cd_w_in", "sgu_norm_g", "sgu_norm_b", "sgu_w", "sgu_b", "cd_w_out", "mlp_w1", "mlp_w2",
          "final_norm_g"]
_COL_SHARDED = ["ab_w_in", "cd_w_in", "mlp_w1"]
_ROW_SHARDED = ["ab_w_out", "cd_w_out", "mlp_w2"]
_SMALL_SHARDED = ["conv_w", "sgu_norm_g", "sgu_norm_b"]
_REPLICATED = ["mix_norm_g", "mlp_norm_g", "pool_w", "pool_scale", "conv_b", "sgu_w", "sgu_b", "final_norm_g"]


def _pad_rows(a2d, mult=SUBLANES):
    pad = (-a2d.shape[0]) % mult
    return jnp.pad(a2d, ((0, pad), (0, 0))) if pad else a2d


def _pack(arrays):
    return _pad_rows(jnp.concatenate([a.reshape(-1, LANES) for a in arrays], axis=0))


def _unpack(packed, shapes):
    out, r = [], 0
    for s in shapes:
        n = math.prod(s) // LANES
        out.append(packed[r:r + n].reshape(s))
        r += n
    return out


def _small_shard_pack(arrays):
    rows = [jnp.pad(a.reshape(-1, a.shape[-1]), ((0, 0), (0, LANES - a.shape[-1]))) for a in arrays]
    return _pad_rows(jnp.concatenate(rows, axis=0))


def _cols_to_chunks(a):
    n = a.shape[-1] // N_DEV
    return jnp.moveaxis(a.reshape(a.shape[:-1] + (N_DEV, n)), -2, 0)


def _chunks_to_cols(a):
    t = jnp.moveaxis(a, 0, -2)
    return t.reshape(t.shape[:-2] + (t.shape[-2] * t.shape[-1],))


def _rows_to_chunks(a):
    r = a.shape[-2] // N_DEV
    return jnp.moveaxis(a.reshape(a.shape[:-2] + (N_DEV, r, a.shape[-1])), -3, 0)


def _chunks_to_rows(a):
    t = jnp.moveaxis(a, 0, -3)
    return t.reshape(t.shape[:-3] + (t.shape[-3] * t.shape[-2], t.shape[-1]))


def kernel(x, mix_norm_g, mlp_norm_g, ab_w_in, pool_w, pool_scale, conv_w, conv_b, ab_w_out, cd_w_in, sgu_norm_g, sgu_norm_b, sgu_w, sgu_b, cd_w_out, mlp_w1, mlp_w2, final_norm_g, loss_target, m_mix_norm_g, m_mlp_norm_g, m_ab_w_in, m_pool_w, m_pool_scale, m_conv_w, m_conv_b, m_ab_w_out, m_cd_w_in, m_sgu_norm_g, m_sgu_norm_b, m_sgu_w, m_sgu_b, m_cd_w_out, m_mlp_w1, m_mlp_w2, m_final_norm_g, v_mix_norm_g, v_mlp_norm_g, v_ab_w_in, v_pool_w, v_pool_scale, v_conv_w, v_conv_b, v_ab_w_out, v_cd_w_in, v_sgu_norm_g, v_sgu_norm_b, v_sgu_w, v_sgu_b, v_cd_w_out, v_mlp_w1, v_mlp_w2, v_final_norm_g):
    w = dict(zip(_NAMES, (mix_norm_g, mlp_norm_g, ab_w_in, pool_w, pool_scale, conv_w, conv_b, ab_w_out, cd_w_in,
                          sgu_norm_g, sgu_norm_b, sgu_w, sgu_b, cd_w_out, mlp_w1, mlp_w2, final_norm_g)))
    m = dict(zip(_NAMES, (m_mix_norm_g, m_mlp_norm_g, m_ab_w_in, m_pool_w, m_pool_scale, m_conv_w, m_conv_b,
                          m_ab_w_out, m_cd_w_in, m_sgu_norm_g, m_sgu_norm_b, m_sgu_w, m_sgu_b, m_cd_w_out,
                          m_mlp_w1, m_mlp_w2, m_final_norm_g)))
    v = dict(zip(_NAMES, (v_mix_norm_g, v_mlp_norm_g, v_ab_w_in, v_pool_w, v_pool_scale, v_conv_w, v_conv_b,
                          v_ab_w_out, v_cd_w_in, v_sgu_norm_g, v_sgu_norm_b, v_sgu_w, v_sgu_b, v_cd_w_out,
                          v_mlp_w1, v_mlp_w2, v_final_norm_g)))
    big = _COL_SHARDED + _ROW_SHARDED
    me = _my_index()

    small_sh = _small_shard_pack([w[n] for n in _SMALL_SHARDED])
    in_lead = {"ab_w_in": pool_w.shape[1] * pool_w.shape[2], "cd_w_in": 2 * sgu_norm_g.shape[-1] * N_DEV}
    shard = {"ab_w_in": ab_w_in[0], "ab_w_out": ab_w_out[0], "cd_w_in": cd_w_in[0], "cd_w_out": cd_w_out[0]}
    for layer in range(mlp_w1.shape[0]):
        shard[f"mlp_w1_{layer}"], shard[f"mlp_w2_{layer}"] = mlp_w1[layer], mlp_w2[layer]
    shard = {n: a.astype(BF16) for n, a in shard.items()}

    def whole(n, g):
        if n.endswith("_in") or n.startswith("mlp_w1"):
            full = _chunks_to_cols(g)
            return _group_in_cols(full, in_lead[n]) if n in in_lead else full
        return _chunks_to_rows(g)

    now = ["ab_w_in", "ab_w_out", "mlp_w1_0", "mlp_w2_0"]
    gathered = _gather_two_level([shard[n] for n in now] + [small_sh], name="gather_weights")
    W = {n: whole(n, g) for n, g in zip(now, gathered)}
    W["att_width"] = cd_w_out.shape[1] * N_DEV - sgu_norm_g.shape[-1] * N_DEV
    token = gathered[-1][0, :SUBLANES]
    for group, tag in ((["cd_w_in", "cd_w_out"], "cd"), (["mlp_w1_1", "mlp_w2_1"], "mlp1")):
        srcs, _ = lax.optimization_barrier(([shard[n] for n in group], token))
        handles, token = _gather_start(srcs, name=f"gather_{tag}_start")
        lazy = _Lazy(handles, lambda got, group=group: {n: whole(n, g) for n, g in zip(group, got)},
                     f"gather_{tag}_wait")
        for n in group:
            W[n] = lazy
    x, _ = lax.optimization_barrier((x, token))
    small_full = gathered[-1]
    r = 0
    for n in _SMALL_SHARDED:
        rows, width = math.prod(w[n].shape[:-1]), w[n].shape[-1]
        W[n] = _chunks_to_cols(small_full[:, r:r + rows, :width])
        r += rows
    for n in _REPLICATED:
        W[n] = w[n]
    for n in ("pool_w", "pool_scale", "sgu_w", "sgu_b"):
        W[n] = W[n][0]

    loss_part, grad_x, G = _local_step(x, loss_target, W)

    flat = {n: (math.prod(w[n].shape[:-1]), w[n].shape[-1]) for n in big}
    parts = []
    for n in big:
        g = _ungroup_in_cols(G[n], in_lead[n]) if n in in_lead else G[n]
        chunks = _cols_to_chunks(g) if n in _COL_SHARDED else _rows_to_chunks(g)
        parts.append(chunks.reshape((N_DEV,) + flat[n]))
    from_sibling = _sibling_swap(parts, name="swap_grads")
    core = lax.axis_index("c")
    chip_parts = []
    for n, p8, got in zip(big, parts, from_sibling):
        own = lax.dynamic_index_in_dim(p8.reshape((N_DEV // 2, 2) + flat[n]), core, axis=1, keepdims=False)
        chip_parts.append(_pair_sum(own, got, name=f"pair_sum_{n}"))
    exchanged = _chip_exchange(chip_parts, name="exchange_grads")
    small_names = _REPLICATED + _SMALL_SHARDED
    small_grads = [G[n].reshape(-1) for n in small_names]
    loss_row = jnp.full((LANES,), loss_part, F32)
    small_pack = _pack(small_grads + [loss_row])
    small_parts = _exchange([small_pack], [True], name="gather_small_grads")[0]

    grads, deltas, new_m, new_v = {}, {}, {}, {}
    for n, gp in zip(big, exchanged):
        shp = w[n].shape
        outs = _adamw(w[n].reshape(flat[n]), gp, m[n].reshape(flat[n]), v[n].reshape(flat[n]), name=f"adamw_{n}")
        grads[n], deltas[n], new_m[n], new_v[n] = [o.reshape(shp) for o in outs]

    rep_shapes = [w[n].shape for n in _REPLICATED]
    rep_rows = sum(math.prod(s) for s in rep_shapes) // LANES
    small_sum_shapes = [(G[n].size,) for n in small_names] + [(LANES,)]
    zero_tail = [jnp.zeros((math.prod(s),), F32) for s in small_sum_shapes[len(_REPLICATED):]]
    w_pack = _pack([w[n] for n in _REPLICATED] + zero_tail)
    m_pack = _pack([m[n] for n in _REPLICATED] + zero_tail)
    v_pack = _pack([v[n] for n in _REPLICATED] + zero_tail)
    outs = _adamw(w_pack, small_parts, m_pack, v_pack, name="adamw_small")
    summed = _unpack(outs[0], small_sum_shapes)
    for i, n in enumerate(_REPLICATED):
        grads[n] = summed[i].reshape(w[n].shape)
    for dst, o in zip((deltas, new_m, new_v), outs[1:]):
        for n, val in zip(_REPLICATED, _unpack(o[:rep_rows], rep_shapes)):
            dst[n] = val
    loss = summed[-1][0]

    shard_g = []
    for i, n in enumerate(_SMALL_SHARDED):
        full = summed[len(_REPLICATED) + i].reshape(w[n].shape[:-1] + (-1,))
        width = w[n].shape[-1]
        shard_g.append(lax.dynamic_slice_in_dim(full, me * width, width, axis=full.ndim - 1))
    g_sh = _small_shard_pack(shard_g)
    m_sh = _small_shard_pack([m[n] for n in _SMALL_SHARDED])
    v_sh = _small_shard_pack([v[n] for n in _SMALL_SHARDED])
    outs = _adamw(small_sh, g_sh[None], m_sh, v_sh, name="adamw_small_sharded")
    r = 0
    for n in _SMALL_SHARDED:
        rows, width = math.prod(w[n].shape[:-1]), w[n].shape[-1]
        for dst, o in zip((grads, deltas, new_m, new_v), outs):
            dst[n] = o[r:r + rows, :width].reshape(w[n].shape)
        r += rows

    return (loss, grad_x, *[grads[n] for n in _NAMES], *[deltas[n] for n in _NAMES],
            *[new_m[n] for n in _NAMES], *[new_v[n] for n in _NAMES])
```

```python
import functools
import math

import jax
import jax.numpy as jnp
from jax import lax
from jax.experimental import pallas as pl
from jax.experimental.pallas import tpu as pltpu

F32 = jnp.float32
BF16 = jnp.bfloat16
GRAD_WIRE = jnp.bfloat16

NORM_EPS = 1e-6
ADAM_LR = 0.001
ADAM_B1 = 0.9
ADAM_B2 = 0.999
ADAM_EPS = 1e-08
ADAM_WD = 0.01
ADAM_STEP = 10
ADAM_C1 = 1.0 - ADAM_B1 ** ADAM_STEP
ADAM_C2 = 1.0 - ADAM_B2 ** ADAM_STEP

N_DEV = 8
LANES = 128
SUBLANES = 8
SB_DH = 64
ATT_BLOCK = 256
POOL_LOG_WINDOWS = 4
VMEM_LIMIT = 56 * 1024 * 1024


def _params(semantics=None):
    return pltpu.CompilerParams(dimension_semantics=semantics, vmem_limit_bytes=VMEM_LIMIT)


def _tile(dim, pref, unit=LANES):
    if dim <= pref:
        return dim
    t = (pref // unit) * unit
    while t >= unit:
        if dim % t == 0:
            return t
        t -= unit
    return dim


def _matmul(a, b, *, name, ta=False, tb=False, extras=(), epilogue=None, out_dtypes=(F32,),
            tm=1024, tn=1024, tk=1024):
    M, K = (a.shape[1], a.shape[0]) if ta else a.shape
    N = b.shape[0] if tb else b.shape[1]
    assert (b.shape[1] if tb else b.shape[0]) == K, (a.shape, b.shape)
    tm, tn, tk = _tile(M, tm), _tile(N, tn), _tile(K, tk)
    nk = K // tk
    dims = (((0 if ta else 1,), (1 if tb else 0,)), ((), ()))
    ne, no = len(extras), len(out_dtypes)

    def body(*refs):
        a_ref, b_ref = refs[0], refs[1]
        e_refs = refs[2:2 + ne]
        o_refs = refs[2 + ne:2 + ne + no]
        k = pl.program_id(2)

        def part():
            return lax.dot_general(a_ref[...].astype(BF16), b_ref[...].astype(BF16), dims,
                                   preferred_element_type=F32)

        def finish(acc):
            outs = epilogue(acc, *[e[...] for e in e_refs]) if epilogue is not None else (acc,)
            for o_ref, val in zip(o_refs, outs):
                o_ref[...] = val.astype(o_ref.dtype)

        if nk == 1:
            finish(part())
        else:
            acc_ref = refs[-1]

            @pl.when(k == 0)
            def _():
                acc_ref[...] = jnp.zeros_like(acc_ref)

            acc_ref[...] += part()

            @pl.when(k == nk - 1)
            def _():
                finish(acc_ref[...])

    a_spec = (pl.BlockSpec((tk, tm), lambda i, j, k: (k, i)) if ta
              else pl.BlockSpec((tm, tk), lambda i, j, k: (i, k)))
    b_spec = (pl.BlockSpec((tn, tk), lambda i, j, k: (j, k)) if tb
              else pl.BlockSpec((tk, tn), lambda i, j, k: (k, j)))
    o_spec = pl.BlockSpec((tm, tn), lambda i, j, k: (i, j))
    outs = pl.pallas_call(
        body,
        name=name,
        grid=(M // tm, N // tn, nk),
        in_specs=[a_spec, b_spec] + [o_spec] * ne,
        out_specs=[o_spec] * no,
        out_shape=[jax.ShapeDtypeStruct((M, N), dt) for dt in out_dtypes],
        scratch_shapes=[pltpu.VMEM((tm, tn), F32)] if nk > 1 else [],
        compiler_params=_params(("parallel", "parallel", "arbitrary")),
    )(a, b, *extras)
    return outs[0] if no == 1 else outs


def _ep_add(acc, res):
    return (acc + res,)


def _ep_relu2(acc):
    r = jnp.maximum(acc, 0.0)
    return (r * r,)


def _ep_relu2_bwd(acc, act):
    return (acc * (2.0 * jnp.sqrt(act.astype(F32))),)


def _rstd(x):
    return lax.rsqrt(jnp.mean(x * x, axis=-1, keepdims=True) + NORM_EPS)


def _rmsnorm(h, g, *, name, tr=512):
    N, D = h.shape
    tr = _tile(N, tr, SUBLANES)

    def body(h_ref, g_ref, o_ref):
        x = h_ref[...]
        o_ref[...] = ((x * _rstd(x)) * g_ref[...]).astype(o_ref.dtype)

    row = pl.BlockSpec((tr, D), lambda i: (i, 0))
    vec = pl.BlockSpec((1, D), lambda i: (0, 0))
    return pl.pallas_call(
        body, name=name, grid=(N // tr,), in_specs=[row, vec], out_specs=row,
        out_shape=jax.ShapeDtypeStruct((N, D), BF16), compiler_params=_params(("parallel",)),
    )(h, g)


def _rmsnorm_bwd(dy, h, g, dres, *, name, tr=512):
    N, D = h.shape
    tr = _tile(N, tr, SUBLANES)

    def body(dy_ref, h_ref, g_ref, r_ref, dh_ref, dhb_ref, dg_ref):
        i = pl.program_id(0)
        x = h_ref[...]
        d = dy_ref[...]
        r = _rstd(x)
        xh = x * r

        @pl.when(i == 0)
        def _():
            dg_ref[...] = jnp.zeros_like(dg_ref)

        dg_ref[...] += jnp.sum(d * xh, axis=0, keepdims=True)
        dxh = d * g_ref[...]
        dh = r_ref[...] + r * (dxh - xh * jnp.mean(dxh * xh, axis=-1, keepdims=True))
        dh_ref[...] = dh
        dhb_ref[...] = dh.astype(dhb_ref.dtype)

    row = pl.BlockSpec((tr, D), lambda i: (i, 0))
    vec = pl.BlockSpec((1, D), lambda i: (0, 0))
    return pl.pallas_call(
        body, name=name, grid=(N // tr,), in_specs=[row, row, vec, row], out_specs=[row, row, vec],
        out_shape=[jax.ShapeDtypeStruct((N, D), F32), jax.ShapeDtypeStruct((N, D), BF16),
                   jax.ShapeDtypeStruct((1, D), F32)],
        compiler_params=_params(("arbitrary",)),
    )(dy, h, g, dres)


def _final_loss(h, g, target, *, name, tr=512):
    N, D = h.shape
    tr = _tile(N, tr, SUBLANES)

    def body(h_ref, g_ref, t_ref, loss_ref, dh_ref, dhb_ref, dg_ref):
        i = pl.program_id(0)
        x = h_ref[...]
        gg = g_ref[...]
        r = _rstd(x)
        xh = x * r
        err = xh * gg - t_ref[...]

        @pl.when(i == 0)
        def _():
            dg_ref[...] = jnp.zeros_like(dg_ref)
            loss_ref[...] = jnp.zeros_like(loss_ref)

        per_row = jnp.mean(err * err, axis=-1, keepdims=True)
        loss_ref[...] += 0.5 * jnp.sum(per_row, axis=0, keepdims=True)
        dy = err * (1.0 / D)
        dg_ref[...] += jnp.sum(dy * xh, axis=0, keepdims=True)
        dxh = dy * gg
        dh = r * (dxh - xh * jnp.mean(dxh * xh, axis=-1, keepdims=True))
        dh_ref[...] = dh
        dhb_ref[...] = dh.astype(dhb_ref.dtype)

    row = pl.BlockSpec((tr, D), lambda i: (i, 0))
    vec = pl.BlockSpec((1, D), lambda i: (0, 0))
    lvec = pl.BlockSpec((1, LANES), lambda i: (0, 0))
    return pl.pallas_call(
        body, name=name, grid=(N // tr,), in_specs=[row, vec, row], out_specs=[lvec, row, row, vec],
        out_shape=[jax.ShapeDtypeStruct((1, LANES), F32), jax.ShapeDtypeStruct((N, D), F32),
                   jax.ShapeDtypeStruct((N, D), BF16), jax.ShapeDtypeStruct((1, D), F32)],
        compiler_params=_params(("arbitrary",)),
    )(h, g, target)


def _shift_down(x, s):
    t = lax.broadcasted_iota(jnp.int32, x.shape, 0)
    return jnp.where(t >= s, pltpu.roll(x, s, 0), 0.0)


def _shift_up(x, s):
    n = x.shape[0]
    t = lax.broadcasted_iota(jnp.int32, x.shape, 0)
    return jnp.where(t < n - s, pltpu.roll(x, n - s, 0), 0.0)


def _window_sum(x, g, shift):
    s = x + shift(x, 1)
    for k in range(1, POOL_LOG_WINDOWS):
        s = jnp.where(k <= g, s + shift(s, 2 ** k), s)
    return s


def _pool_count(shape, g):
    t = lax.broadcasted_iota(jnp.int32, shape, 0)
    return jnp.minimum(t + 1, lax.shift_left(jnp.int32(2), g)).astype(F32)


def _pool_fwd(p, pool_w, pool_scale, B, T, off, width, *, name):
    G, dh = pool_w.shape[0], pool_w.shape[1]
    assert G == POOL_LOG_WINDOWS and off % dh == 0
    base = off // dh

    def body(a_ref, w_ref, s_ref, o_ref):
        g = pl.program_id(0)
        a = a_ref[...]
        pooled = _window_sum(a, g, _shift_down) / _pool_count(a.shape, g) - a
        m = jnp.dot(pooled.astype(BF16), w_ref[0].astype(BF16), preferred_element_type=F32)
        o_ref[...] = (m * s_ref[0]).astype(o_ref.dtype)

    return pl.pallas_call(
        body, name=name, grid=(G, B),
        in_specs=[pl.BlockSpec((T, dh), lambda g, b: (b, base + g)),
                  pl.BlockSpec((1, dh, dh), lambda g, b: (g, 0, 0)),
                  pl.BlockSpec((1, 1, dh), lambda g, b: (g, 0, 0))],
        out_specs=pl.BlockSpec((T, dh), lambda g, b: (b, g)),
        out_shape=jax.ShapeDtypeStruct((B * T, width), BF16),
        compiler_params=_params(("parallel", "parallel")),
    )(p, pool_w, pool_scale.reshape(G, 1, dh))


def _pool_bwd(p, dcat, pool_w, pool_scale, B, T, off, *, name):
    G, dh = pool_w.shape[0], pool_w.shape[1]
    base = off // dh

    def body(a_ref, d_ref, w_ref, s_ref, da_ref, dw_ref, ds_ref):
        g = pl.program_id(0)
        b = pl.program_id(1)
        a = a_ref[...]
        d = d_ref[...]
        cnt = _pool_count(a.shape, g)
        pooled = (_window_sum(a, g, _shift_down) / cnt - a).astype(BF16)
        w = w_ref[0].astype(BF16)
        m = jnp.dot(pooled, w, preferred_element_type=F32)

        @pl.when(b == 0)
        def _():
            dw_ref[...] = jnp.zeros_like(dw_ref)
            ds_ref[...] = jnp.zeros_like(ds_ref)

        ds_ref[0] += jnp.sum(d * m, axis=0, keepdims=True)
        dm = (d * s_ref[0]).astype(BF16)
        dw_ref[0] += lax.dot_general(pooled, dm, (((0,), (0,)), ((), ())), preferred_element_type=F32)
        dpooled = lax.dot_general(dm, w, (((1,), (1,)), ((), ())), preferred_element_type=F32)
        da = _window_sum(dpooled / cnt, g, _shift_up) - dpooled
        da_ref[...] = da.astype(da_ref.dtype)

    pblk = pl.BlockSpec((T, dh), lambda g, b: (b, base + g))
    dblk = pl.BlockSpec((T, dh), lambda g, b: (b, g))
    wspec = pl.BlockSpec((1, dh, dh), lambda g, b: (g, 0, 0))
    sspec = pl.BlockSpec((1, 1, dh), lambda g, b: (g, 0, 0))
    return pl.pallas_call(
        body, name=name, grid=(G, B), in_specs=[pblk, dblk, wspec, sspec], out_specs=[pblk, wspec, sspec],
        out_shape=[jax.ShapeDtypeStruct((B * T, p.shape[1]), BF16), jax.ShapeDtypeStruct((G, dh, dh), F32),
                   jax.ShapeDtypeStruct((G, 1, dh), F32)],
        compiler_params=_params(("parallel", "arbitrary")),
    )(p, dcat, pool_w, pool_scale.reshape(G, 1, dh))


_ANY = pl.BlockSpec(memory_space=pl.ANY)


def _conv_fwd(p, cat, conv_w, conv_b, B, T, coff, *, name):
    CW = conv_w.shape[1]
    tc = LANES
    assert coff % tc == 0 and CW % tc == 0
    cbase = coff // tc

    def body(p_ref, cat_ref, w_ref, b_ref, o_ref):
        xb, gb, gc = p_ref[:, 0:tc], p_ref[:, tc:2 * tc], p_ref[:, 2 * tc:3 * tc]
        c = gc * xb
        w = w_ref[...]
        y = _shift_down(c, 2) * w[0:1] + _shift_down(c, 1) * w[1:2] + c * w[2:3] + b_ref[...]
        o_ref[...] = (gb * y).astype(o_ref.dtype)

    return pl.pallas_call(
        body, name=name, grid=(CW // tc, B),
        in_specs=[pl.BlockSpec((T, 3 * tc), lambda j, b: (b, j)), _ANY,
                  pl.BlockSpec((3, tc), lambda j, b: (0, j)), pl.BlockSpec((1, tc), lambda j, b: (0, j))],
        out_specs=pl.BlockSpec((T, tc), lambda j, b: (b, cbase + j)),
        out_shape=jax.ShapeDtypeStruct(cat.shape, cat.dtype), input_output_aliases={1: 0},
        compiler_params=_params(("parallel", "parallel")),
    )(p, cat, conv_w, conv_b)


def _conv_bwd(p, dcat, dp, conv_w, conv_b, B, T, coff, *, name):
    CW = conv_w.shape[1]
    tc = LANES
    assert coff % tc == 0
    cbase = coff // tc

    def body(p_ref, d_ref, dp_in_ref, w_ref, b_ref, dp_ref, dw_ref, db_ref):
        b = pl.program_id(1)
        xb, gb, gc = p_ref[:, 0:tc], p_ref[:, tc:2 * tc], p_ref[:, 2 * tc:3 * tc]
        d = d_ref[...]
        w = w_ref[...]
        c = gc * xb
        c1 = _shift_down(c, 1)
        c2 = _shift_down(c, 2)
        y = c2 * w[0:1] + c1 * w[1:2] + c * w[2:3] + b_ref[...]
        dy = d * gb
        dp_ref[:, tc:2 * tc] = (d * y).astype(dp_ref.dtype)

        @pl.when(b == 0)
        def _():
            dw_ref[...] = jnp.zeros_like(dw_ref)
            db_ref[...] = jnp.zeros_like(db_ref)

        db_ref[...] += jnp.sum(dy, axis=0, keepdims=True)
        dw_ref[0:1, :] += jnp.sum(dy * c2, axis=0, keepdims=True)
        dw_ref[1:2, :] += jnp.sum(dy * c1, axis=0, keepdims=True)
        dw_ref[2:3, :] += jnp.sum(dy * c, axis=0, keepdims=True)
        dc = dy * w[2:3] + _shift_up(dy, 1) * w[1:2] + _shift_up(dy, 2) * w[0:1]
        dp_ref[:, 2 * tc:3 * tc] = (dc * xb).astype(dp_ref.dtype)
        dp_ref[:, 0:tc] = (dc * gc).astype(dp_ref.dtype)

    wspec = pl.BlockSpec((3, tc), lambda j, b: (0, j))
    bspec = pl.BlockSpec((1, tc), lambda j, b: (0, j))
    pblk = pl.BlockSpec((T, 3 * tc), lambda j, b: (b, j))
    return pl.pallas_call(
        body, name=name, grid=(CW // tc, B),
        in_specs=[pblk, pl.BlockSpec((T, tc), lambda j, b: (b, cbase + j)), _ANY, wspec, bspec],
        out_specs=[pblk, wspec, bspec],
        out_shape=[jax.ShapeDtypeStruct(dp.shape, dp.dtype), jax.ShapeDtypeStruct((3, CW), F32),
                   jax.ShapeDtypeStruct((1, CW), F32)],
        input_output_aliases={2: 0},
        compiler_params=_params(("parallel", "arbitrary")),
    )(p, dcat, dp, conv_w, conv_b)


_SQRT_HALF = 0.7071067811865476
_INV_SQRT_2PI = 0.3989422804014327


def _gelu(x):
    return x * (lax.erf(x * _SQRT_HALF) + 1.0) * 0.5


def _gelu_grad(x):
    return 0.5 * (lax.erf(x * _SQRT_HALF) + 1.0) + x * (_INV_SQRT_2PI * jnp.exp(-0.5 * x * x))


def _layernorm_parts(v):
    mu = jnp.mean(v, axis=-1, keepdims=True)
    vc = v - mu
    rstd = lax.rsqrt(jnp.mean(vc * vc, axis=-1, keepdims=True) + NORM_EPS)
    return vc * rstd, rstd


def _tril_mask(L):
    r = lax.broadcasted_iota(jnp.int32, (L, L), 0)
    c = lax.broadcasted_iota(jnp.int32, (L, L), 1)
    return r >= c


def _sgu_fwd(p, ln_g, ln_b, sgu_w, sgu_b, off, width, *, name, tr=512):
    N = p.shape[0]
    G, L = sgu_w.shape[0], sgu_w.shape[1]
    SW = ln_g.shape[1]
    dh = SW // G
    assert off % SW == 0
    ub = off // SW
    tr = _tile(N, tr, L)
    assert tr % L == 0

    def body(u_ref, v_ref, g_ref, beta_ref, w_ref, b_ref, o_ref):
        u = _gelu(u_ref[...])
        vhat, _ = _layernorm_parts(_gelu(v_ref[...]))
        vn = (vhat * g_ref[...] + beta_ref[...]).astype(BF16)
        mask = _tril_mask(L)
        for gi in range(G):
            w = jnp.where(mask, w_ref[gi], 0.0).astype(BF16)
            bias = b_ref[gi]
            cols = slice(gi * dh, (gi + 1) * dh)
            for n in range(tr // L):
                rows = slice(n * L, (n + 1) * L)
                s = jnp.dot(w, vn[rows, cols], preferred_element_type=F32) + bias
                o_ref[rows, cols] = (u[rows, cols] * s).astype(o_ref.dtype)

    def col(k):
        return pl.BlockSpec((tr, SW), lambda i: (i, k))

    vec = pl.BlockSpec((1, SW), lambda i: (0, 0))
    return pl.pallas_call(
        body, name=name, grid=(N // tr,),
        in_specs=[col(ub), col(ub + 1), vec, vec, pl.BlockSpec((G, L, L), lambda i: (0, 0, 0)),
                  pl.BlockSpec((G, L, 1), lambda i: (0, 0, 0))],
        out_specs=col(0), out_shape=jax.ShapeDtypeStruct((N, width), BF16),
        compiler_params=_params(("parallel",)),
    )(p, p, ln_g, ln_b, sgu_w, sgu_b.reshape(G, L, 1))


def _sgu_bwd(p, dcat, ln_g, ln_b, sgu_w, sgu_b, off, *, name, tr=512):
    N = p.shape[0]
    G, L = sgu_w.shape[0], sgu_w.shape[1]
    SW = ln_g.shape[1]
    dh = SW // G
    assert off % SW == 0
    ub = off // SW
    tr = _tile(N, tr, L)

    def compute(i, u_ref, v_ref, dc_ref, g_ref, beta_ref, w_ref, b_ref,
                du_ref, dv_ref, dw_ref, db_ref, dg_ref, dbeta_ref, du_s, dvn_s):
        pu = u_ref[...]
        pv = v_ref[...]
        u = _gelu(pu)
        vhat, rstd = _layernorm_parts(_gelu(pv))
        gg = g_ref[...]
        vn = (vhat * gg + beta_ref[...]).astype(BF16)
        dc = dc_ref[...]
        mask = _tril_mask(L)

        @pl.when(i == 0)
        def _():
            dw_ref[...] = jnp.zeros_like(dw_ref)
            db_ref[...] = jnp.zeros_like(db_ref)
            dg_ref[...] = jnp.zeros_like(dg_ref)
            dbeta_ref[...] = jnp.zeros_like(dbeta_ref)

        for gi in range(G):
            w = jnp.where(mask, w_ref[gi], 0.0).astype(BF16)
            bias = b_ref[gi]
            cols = slice(gi * dh, (gi + 1) * dh)
            dw_acc = jnp.zeros((L, L), F32)
            db_acc = jnp.zeros((L, 1), F32)
            for n in range(tr // L):
                rows = slice(n * L, (n + 1) * L)
                vb = vn[rows, cols]
                s = jnp.dot(w, vb, preferred_element_type=F32) + bias
                du_s[rows, cols] = dc[rows, cols] * s
                ds = dc[rows, cols] * u[rows, cols]
                db_acc += jnp.sum(ds, axis=1, keepdims=True)
                dsb = ds.astype(BF16)
                dw_acc += lax.dot_general(dsb, vb, (((1,), (1,)), ((), ())), preferred_element_type=F32)
                dvn_s[rows, cols] = lax.dot_general(w, dsb, (((0,), (0,)), ((), ())),
                                                    preferred_element_type=F32)
            dw_ref[gi] += jnp.where(mask, dw_acc, 0.0)
            db_ref[gi] += db_acc

        dvn = dvn_s[...]
        dg_ref[...] += jnp.sum(dvn * vhat, axis=0, keepdims=True)
        dbeta_ref[...] += jnp.sum(dvn, axis=0, keepdims=True)
        dvh = dvn * gg
        dv = rstd * (dvh - jnp.mean(dvh, axis=-1, keepdims=True)
                     - vhat * jnp.mean(dvh * vhat, axis=-1, keepdims=True))
        dv_ref[...] = (dv * _gelu_grad(pv)).astype(dv_ref.dtype)
        du_ref[...] = (du_s[...] * _gelu_grad(pu)).astype(du_ref.dtype)

    def body(u_ref, v_ref, dc_ref, g_ref, beta_ref, w_ref, b_ref,
             dp_ref, dw_ref, db_ref, dg_ref, dbeta_ref, du_s, dvn_s, dv_s):
        i = pl.program_id(0)
        half = pl.program_id(1)

        @pl.when(half == 0)
        def _():
            compute(i, u_ref, v_ref, dc_ref, g_ref, beta_ref, w_ref, b_ref,
                    dp_ref, dv_s, dw_ref, db_ref, dg_ref, dbeta_ref, du_s, dvn_s)

        @pl.when(half == 1)
        def _():
            dp_ref[...] = dv_s[...]

    def col(k):
        return pl.BlockSpec((tr, SW), lambda i, half: (i, k))

    vec = pl.BlockSpec((1, SW), lambda i, half: (0, 0))
    wspec = pl.BlockSpec((G, L, L), lambda i, half: (0, 0, 0))
    bspec = pl.BlockSpec((G, L, 1), lambda i, half: (0, 0, 0))
    return pl.pallas_call(
        body, name=name, grid=(N // tr, 2),
        in_specs=[col(ub), col(ub + 1), col(0), vec, vec, wspec, bspec],
        out_specs=[pl.BlockSpec((tr, SW), lambda i, half: (i, ub + half)), wspec, bspec, vec, vec],
        out_shape=[jax.ShapeDtypeStruct((N, p.shape[1]), BF16),
                   jax.ShapeDtypeStruct((G, L, L), F32), jax.ShapeDtypeStruct((G, L, 1), F32),
                   jax.ShapeDtypeStruct((1, SW), F32), jax.ShapeDtypeStruct((1, SW), F32)],
        scratch_shapes=[pltpu.VMEM((tr, SW), F32), pltpu.VMEM((tr, SW), F32), pltpu.VMEM((tr, SW), BF16)],
        compiler_params=_params(("arbitrary", "arbitrary")),
    )(p, p, dcat, ln_g, ln_b, sgu_w, sgu_b.reshape(G, L, 1))


def _log_sigmoid_pair(z):
    ls = jnp.minimum(z, 0.0) - jnp.log(1.0 + jnp.exp(-jnp.abs(z)))
    return ls, ls - z


def _split_dot(x, m):
    hi = x.astype(BF16)
    lo = (x - hi.astype(F32)).astype(BF16)
    return (jnp.dot(hi, m, preferred_element_type=F32) + jnp.dot(lo, m, preferred_element_type=F32))


def _att_tiles(TB):
    r = lax.broadcasted_iota(jnp.int32, (TB, TB), 0)
    c = lax.broadcasted_iota(jnp.int32, (TB, TB), 1)
    return r, c


def _att_weights(qt, kb, strict, later, carry):
    z = lax.dot_general(qt, kb, (((1,), (1,)), ((), ())), preferred_element_type=F32)
    ls, lk = _log_sigmoid_pair(z)
    if strict is not None:
        lk = jnp.where(strict, lk, 0.0)
    suffix = _split_dot(lk, later) + carry
    a = jnp.exp(ls + suffix)
    if strict is not None:
        a = jnp.where(strict, a, 0.0)
    return ls, lk, suffix, a


HP = LANES // SB_DH
ATT_SCALE = 1.0 / math.sqrt(SB_DH)


def _stage_heads(src_ref, col0, dst_ref, T, scale=None):
    rows = _tile(T, 256, SUBLANES)

    def chunk(n, _):
        r0 = pl.multiple_of(n * rows, rows)
        for hh in range(HP):
            x = src_ref[pl.ds(r0, rows), col0 + hh * SB_DH:col0 + (hh + 1) * SB_DH]
            if scale is not None:
                x = x * scale
            dst_ref[hh, pl.ds(r0, rows), :] = x.astype(dst_ref.dtype)
        return 0

    lax.fori_loop(0, T // rows, chunk, 0)


def _attn_fwd(p, cat, B, T, coff, *, name):
    dh = SB_DH
    nhp = (cat.shape[1] - coff) // LANES
    TB = _tile(T, ATT_BLOCK)
    nb = T // TB
    assert nb <= LANES and coff % LANES == 0
    cbase = coff // LANES

    def body(p_ref, cat_ref, o_ref, c_ref, q_ref, k_ref, v_ref):
        _stage_heads(p_ref, 0, q_ref, T, ATT_SCALE)
        _stage_heads(p_ref, LANES, k_ref, T)
        _stage_heads(p_ref, 2 * LANES, v_ref, T)
        r, c = _att_tiles(TB)
        strict = c < r
        later = (r > c).astype(BF16)
        lane = lax.broadcasted_iota(jnp.int32, (TB, LANES), 1)

        def tile(hh, qt, j, carry, mask):
            k0 = pl.multiple_of(j * TB, TB)
            _, lk, suffix, a = _att_weights(qt, k_ref[hh, pl.ds(k0, TB), :], mask, later, carry)
            pv = jnp.dot(a.astype(BF16), v_ref[hh, pl.ds(k0, TB), :], preferred_element_type=F32)
            return pv, suffix[:, 0:1] + lk[:, 0:1]

        def qblock(i, _):
            q0 = pl.multiple_of(i * TB, TB)
            qts = [q_ref[hh, pl.ds(q0, TB), :] for hh in range(HP)]
            state = []
            for hh in range(HP):
                pv, carry = tile(hh, qts[hh], i, jnp.zeros((TB, 1), F32), strict)
                state += [pv, carry, jnp.zeros((TB, LANES), F32)]

            def kblock(jj, st):
                j = i - jj
                out = []
                for hh in range(HP):
                    acc, carry, cm = st[3 * hh:3 * hh + 3]
                    pv, new_carry = tile(hh, qts[hh], j, carry, None)
                    out += [acc + pv, new_carry, jnp.where(lane == j, carry, cm)]
                return tuple(out)

            st = lax.fori_loop(1, i + 1, kblock, tuple(state))
            for hh in range(HP):
                o_ref[pl.ds(q0, TB), hh * dh:(hh + 1) * dh] = st[3 * hh].astype(o_ref.dtype)
                c_ref[hh, pl.ds(q0, TB), :] = st[3 * hh + 2]
            return 0

        lax.fori_loop(0, nb, qblock, 0)

    staged = pltpu.VMEM((HP, T, dh), BF16)
    return pl.pallas_call(
        body, name=name, grid=(B, nhp),
        in_specs=[pl.BlockSpec((T, 3 * LANES), lambda b, hp: (b, hp)), _ANY],
        out_specs=[pl.BlockSpec((T, LANES), lambda b, hp: (b, cbase + hp)),
                   pl.BlockSpec((HP, T, LANES), lambda b, hp: (b * nhp + hp, 0, 0))],
        out_shape=[jax.ShapeDtypeStruct(cat.shape, cat.dtype),
                   jax.ShapeDtypeStruct((B * nhp * HP, T, LANES), F32)],
        input_output_aliases={1: 0}, scratch_shapes=[staged, staged, staged],
        compiler_params=_params(("parallel", "parallel")),
    )(p, cat)


def _attn_bwd(p, dcat, carries, dp, B, T, coff, *, name):
    dh = SB_DH
    nhp = (dcat.shape[1] - coff) // LANES
    TB = _tile(T, ATT_BLOCK)
    nb = T // TB
    cbase = coff // LANES

    def body(p_ref, d_ref, c_ref, dp_in_ref, dp_ref, q_ref, k_ref, v_ref, do_ref, dk_ref, dv_ref):
        _stage_heads(p_ref, 0, q_ref, T, ATT_SCALE)
        _stage_heads(p_ref, LANES, k_ref, T)
        _stage_heads(p_ref, 2 * LANES, v_ref, T)
        _stage_heads(d_ref, 0, do_ref, T)
        r, c = _att_tiles(TB)
        strict = c < r
        later = (r > c).astype(BF16)
        earlier = (r < c).astype(BF16)
        lane = lax.broadcasted_iota(jnp.int32, (TB, LANES), 1)
        dk_ref[...] = jnp.zeros_like(dk_ref)
        dv_ref[...] = jnp.zeros_like(dv_ref)

        def tile(hh, qt, dot, cm, j, before, mask):
            k0 = pl.multiple_of(j * TB, TB)
            kb = k_ref[hh, pl.ds(k0, TB), :]
            carry = jnp.sum(jnp.where(lane == j, cm, 0.0), axis=1, keepdims=True)
            ls, _, _, a = _att_weights(qt, kb, mask, later, carry)
            dv_ref[hh, pl.ds(k0, TB), :] += lax.dot_general(a.astype(BF16), dot, (((0,), (0,)), ((), ())),
                                                            preferred_element_type=F32)
            da = lax.dot_general(dot, v_ref[hh, pl.ds(k0, TB), :], (((1,), (1,)), ((), ())),
                                 preferred_element_type=F32)
            gl = a * da
            prefix = _split_dot(gl, earlier) + before
            dz = gl - jnp.exp(ls) * (gl + prefix)
            if mask is not None:
                dz = jnp.where(mask, dz, 0.0)
            dzb = dz.astype(BF16)
            dk_ref[hh, pl.ds(k0, TB), :] += lax.dot_general(dzb, qt, (((0,), (0,)), ((), ())),
                                                            preferred_element_type=F32)
            return jnp.dot(dzb, kb, preferred_element_type=F32), prefix[:, TB - 1:TB] + gl[:, TB - 1:TB]

        def qblock(i, _):
            q0 = pl.multiple_of(i * TB, TB)
            qts = [q_ref[hh, pl.ds(q0, TB), :] for hh in range(HP)]
            dots = [do_ref[hh, pl.ds(q0, TB), :] for hh in range(HP)]
            cms = [c_ref[hh, pl.ds(q0, TB), :] for hh in range(HP)]

            def kblock(j, st):
                out = []
                for hh in range(HP):
                    dq, before = st[2 * hh:2 * hh + 2]
                    part, new_before = tile(hh, qts[hh], dots[hh], cms[hh], j, before, None)
                    out += [dq + part, new_before]
                return tuple(out)

            st = lax.fori_loop(0, i, kblock, (jnp.zeros((TB, dh), F32), jnp.zeros((TB, 1), F32)) * HP)
            for hh in range(HP):
                part, _ = tile(hh, qts[hh], dots[hh], cms[hh], i, st[2 * hh + 1], strict)
                dq = (st[2 * hh] + part) * ATT_SCALE
                dp_ref[pl.ds(q0, TB), hh * dh:(hh + 1) * dh] = dq.astype(dp_ref.dtype)
            return 0

        lax.fori_loop(0, nb, qblock, 0)

        def write_back(n, _):
            r0 = pl.multiple_of(n * TB, TB)
            for hh in range(HP):
                dp_ref[pl.ds(r0, TB), LANES + hh * dh:LANES + (hh + 1) * dh] = (
                    dk_ref[hh, pl.ds(r0, TB), :].astype(dp_ref.dtype))
                dp_ref[pl.ds(r0, TB), 2 * LANES + hh * dh:2 * LANES + (hh + 1) * dh] = (
                    dv_ref[hh, pl.ds(r0, TB), :].astype(dp_ref.dtype))
            return 0

        lax.fori_loop(0, nb, write_back, 0)

    pblk = pl.BlockSpec((T, 3 * LANES), lambda b, hp: (b, hp))
    staged = pltpu.VMEM((HP, T, dh), BF16)
    accum = pltpu.VMEM((HP, T, dh), F32)
    return pl.pallas_call(
        body, name=name, grid=(B, nhp),
        in_specs=[pblk, pl.BlockSpec((T, LANES), lambda b, hp: (b, cbase + hp)),
                  pl.BlockSpec((HP, T, LANES), lambda b, hp: (b * nhp + hp, 0, 0)), _ANY],
        out_specs=pblk, out_shape=jax.ShapeDtypeStruct(dp.shape, dp.dtype), input_output_aliases={3: 0},
        scratch_shapes=[staged, staged, staged, staged, accum, accum],
        compiler_params=_params(("parallel", "parallel")),
    )(p, dcat, carries, dp)


def _adamw(w, gparts, m, v, *, name):
    L, R, C = w.shape
    P = gparts[0].shape[0]
    assert len(gparts) == L
    tr = _tile(R, max(SUBLANES, (1 << 19) // (C * P)), SUBLANES)

    def body(*refs):
        w_ref, g_refs, (m_ref, v_ref) = refs[0], refs[1:1 + L], refs[1 + L:3 + L]
        go_ref, d_ref, mo_ref, vo_ref = refs[3 + L:]
        layer = pl.program_id(0)

        def update(g_ref):
            g = g_ref[0].astype(F32)
            for i in range(1, P):
                g = g + g_ref[i].astype(F32)
            m2 = ADAM_B1 * m_ref[...] + (1.0 - ADAM_B1) * g
            v2 = ADAM_B2 * v_ref[...] + (1.0 - ADAM_B2) * (g * g)
            m_hat = m2 / ADAM_C1
            v_hat = v2 / ADAM_C2
            go_ref[...] = g
            d_ref[...] = -ADAM_LR * (m_hat / (jnp.sqrt(v_hat) + ADAM_EPS) + ADAM_WD * w_ref[...])
            mo_ref[...] = m2
            vo_ref[...] = v2

        for l in range(L):
            pl.when(layer == l)(functools.partial(update, g_refs[l]))

    row = pl.BlockSpec((None, tr, C), lambda l, i: (l, i, 0))

    def part(mine):
        return pl.BlockSpec((P, tr, C), lambda l, i: (0, jnp.where(l == mine, i, 0), 0))

    shp = jax.ShapeDtypeStruct((L, R, C), F32)
    return pl.pallas_call(
        body, name=name, grid=(L, R // tr),
        in_specs=[row] + [part(l) for l in range(L)] + [row, row],
        out_specs=[row] * 4, out_shape=[shp] * 4, compiler_params=_params(("arbitrary", "arbitrary")),
    )(w, *gparts, m, v)


def _my_index():
    return 4 * lax.axis_index("x") + 2 * lax.axis_index("y") + lax.axis_index("c")


def _exchange(arrs, gather, *, name):
    n = len(arrs)

    def body(*refs):
        ins, outs = refs[:n], refs[n:2 * n]
        send_sems, recv_sems, local_sems = refs[2 * n:]
        x, y, c = lax.axis_index("x"), lax.axis_index("y"), lax.axis_index("c")
        me = 4 * x + 2 * y + c
        remote, local = [], []
        for a in range(n):
            own = ins[a] if gather[a] else ins[a].at[me]
            cp = pltpu.make_async_copy(own, outs[a].at[me], local_sems.at[a])
            cp.start()
            local.append(cp)
            for k in range(1, N_DEV):
                px = 1 - x if k & 4 else x
                py = 1 - y if k & 2 else y
                pc = 1 - c if k & 1 else c
                src = ins[a] if gather[a] else ins[a].at[4 * px + 2 * py + pc]
                cp = pltpu.make_async_remote_copy(
                    src_ref=src, dst_ref=outs[a].at[me],
                    send_sem=send_sems.at[a, k - 1], recv_sem=recv_sems.at[a, k - 1],
                    device_id=(px, py, pc), device_id_type=pl.DeviceIdType.MESH)
                cp.start()
                remote.append(cp)
        for cp in remote:
            cp.wait()
        for cp in local:
            cp.wait()

    hbm = pl.BlockSpec(memory_space=pltpu.HBM)
    out_shape = [jax.ShapeDtypeStruct(((N_DEV,) + a.shape) if g else a.shape, a.dtype)
                 for a, g in zip(arrs, gather)]
    return pl.pallas_call(
        body, name=name, in_specs=[hbm] * n, out_specs=[hbm] * n, out_shape=out_shape,
        scratch_shapes=[pltpu.SemaphoreType.DMA((n, N_DEV - 1)), pltpu.SemaphoreType.DMA((n, N_DEV - 1)),
                        pltpu.SemaphoreType.DMA((n,))],
    )(*arrs)


_HBM = pl.BlockSpec(memory_space=pltpu.HBM)


def _other_chips(x, y):
    return [(1 - x, y), (x, 1 - y), (1 - x, 1 - y)]


def _gather_two_level(arrs, *, name):
    n = len(arrs)

    def body(*refs):
        ins, outs = refs[:n], refs[n:2 * n]
        send_sems, recv_sems, local_sems = refs[2 * n:]
        x, y, c = lax.axis_index("x"), lax.axis_index("y"), lax.axis_index("c")
        me, sibling = (x, y, c), (x, y, 1 - c)
        chips = _other_chips(x, y)

        def slot(a, px, py, pc):
            return outs[a].at[4 * px + 2 * py + pc]

        def copy(a, k, block, to, src=None):
            return pltpu.make_async_remote_copy(
                src_ref=slot(a, *block) if src is None else src, dst_ref=slot(a, *block),
                send_sem=send_sems.at[a, k], recv_sem=recv_sems.at[a, k],
                device_id=to, device_id_type=pl.DeviceIdType.MESH)

        local, sends = [], []
        for a in range(n):
            cp = pltpu.make_async_copy(ins[a], slot(a, *me), local_sems.at[a])
            cp.start()
            local.append(cp)
            first = [copy(a, 0, me, sibling, src=ins[a])]
            first += [copy(a, 1 + j, me, (*chip, c), src=ins[a]) for j, chip in enumerate(chips)]
            for cp in first:
                cp.start()
            sends += first
        for j, chip in enumerate(chips):
            for a in range(n):
                copy(a, 1 + j, (*chip, c), me).wait_recv()
                cp = copy(a, 4 + j, (*chip, c), sibling)
                cp.start()
                sends.append(cp)
        for a in range(n):
            copy(a, 0, sibling, me).wait_recv()
            for j, chip in enumerate(chips):
                copy(a, 4 + j, (*chip, 1 - c), me).wait_recv()
        for cp in sends:
            cp.wait_send()
        for cp in local:
            cp.wait()

    return pl.pallas_call(
        body, name=name, in_specs=[_HBM] * n, out_specs=[_HBM] * n,
        out_shape=[jax.ShapeDtypeStruct((N_DEV,) + a.shape, a.dtype) for a in arrs],
        scratch_shapes=[pltpu.SemaphoreType.DMA((n, N_DEV - 1)), pltpu.SemaphoreType.DMA((n, N_DEV - 1)),
                        pltpu.SemaphoreType.DMA((n,))],
    )(*arrs)


_SEM = pl.BlockSpec(memory_space=pltpu.SEMAPHORE)
_SPLIT_COPY = pltpu.SideEffectType.DATAFLOW_SIDE_EFFECTING


def _peers(x, y, c):
    return [((1 - x if k & 4 else x), (1 - y if k & 2 else y), (1 - c if k & 1 else c)) for k in range(1, N_DEV)]


_SPLIT_SEMS = 2 * (N_DEV - 1) + 1


def _split_sems(sems, a):
    mine = sems[a * _SPLIT_SEMS:(a + 1) * _SPLIT_SEMS]
    return mine[:N_DEV - 1], mine[N_DEV - 1:2 * (N_DEV - 1)], mine[-1]


def _split_src(ref, scatter, index):
    return ref.at[index] if scatter else ref


def _gather_start(arrs, *, name, scatter=False):
    n = len(arrs)
    ns = n * _SPLIT_SEMS

    def body(*refs):
        ins, lands = refs[:n], refs[n:2 * n]
        sems = refs[2 * n:2 * n + ns]
        token = refs[-1]
        x, y, c = lax.axis_index("x"), lax.axis_index("y"), lax.axis_index("c")
        me = 4 * x + 2 * y + c
        for a in range(n):
            send, recv, local = _split_sems(sems, a)
            pltpu.make_async_copy(_split_src(ins[a], scatter, me), lands[a].at[me], local).start()
            for k, (px, py, pc) in enumerate(_peers(x, y, c)):
                pltpu.make_async_remote_copy(
                    src_ref=_split_src(ins[a], scatter, 4 * px + 2 * py + pc), dst_ref=lands[a].at[me],
                    send_sem=send[k], recv_sem=recv[k],
                    device_id=(px, py, pc), device_id_type=pl.DeviceIdType.MESH).start()
        token[...] = jnp.zeros_like(token)

    lands = [lax.empty(a.shape if scatter else (N_DEV,) + a.shape, a.dtype) for a in arrs]
    operands = [pltpu.with_memory_space_constraint(a, pltpu.HBM) for a in list(arrs) + lands]
    outs = pl.pallas_call(
        body, name=name, in_specs=[_HBM] * (2 * n),
        out_specs=[_SEM] * ns + [_HBM] * (2 * n) + [pl.BlockSpec(memory_space=pltpu.VMEM)],
        out_shape=[pltpu.SemaphoreType.DMA(())] * ns + [pltpu.HBM(a.shape, a.dtype) for a in operands]
        + [jax.ShapeDtypeStruct((SUBLANES, LANES), F32)],
        input_output_aliases={i: ns + i for i in range(2 * n)},
        compiler_params=pltpu.CompilerParams(has_side_effects=_SPLIT_COPY),
    )(*operands)
    return tuple(outs[:-1]), outs[-1]


def _gather_wait(handles, after, *, name, scatter=False):
    n = len(handles) // (_SPLIT_SEMS + 2)
    ns = n * _SPLIT_SEMS
    sems, thru = handles[:ns], handles[ns:]

    def body(*refs):
        ins, lands = refs[:n], refs[n:2 * n]
        sems = refs[2 * n:2 * n + ns]
        x, y, c = lax.axis_index("x"), lax.axis_index("y"), lax.axis_index("c")
        me = 4 * x + 2 * y + c
        for a in range(n):
            send, recv, local = _split_sems(sems, a)
            src = _split_src(ins[a], scatter, me)
            pltpu.make_async_copy(src, lands[a].at[me], local).wait()
            for k, peer in enumerate(_peers(x, y, c)):
                cp = pltpu.make_async_remote_copy(
                    src_ref=src, dst_ref=lands[a].at[me], send_sem=send[k], recv_sem=recv[k],
                    device_id=peer, device_id_type=pl.DeviceIdType.MESH)
                cp.wait_send()
                cp.wait_recv()

    outs = pl.pallas_call(
        body, name=name, in_specs=[_HBM] * (2 * n) + [_SEM] * ns + [_ANY], out_specs=[_HBM] * (2 * n),
        out_shape=[pltpu.HBM(a.shape, a.dtype) for a in thru],
        input_output_aliases={i: i for i in range(2 * n)},
        compiler_params=pltpu.CompilerParams(has_side_effects=_SPLIT_COPY),
    )(*thru, *sems, after)
    return outs[n:]


def _sibling_swap(arrs, *, name):
    n = len(arrs)
    nchip = N_DEV // 2

    def body(*refs):
        ins, outs = refs[:n], refs[n:2 * n]
        send_sems, recv_sems = refs[2 * n:]
        x, y, c = lax.axis_index("x"), lax.axis_index("y"), lax.axis_index("c")
        copies = []
        for a in range(n):
            for k in range(nchip):
                cp = pltpu.make_async_remote_copy(
                    src_ref=ins[a].at[2 * k + 1 - c], dst_ref=outs[a].at[k],
                    send_sem=send_sems.at[a, k], recv_sem=recv_sems.at[a, k],
                    device_id=(x, y, 1 - c), device_id_type=pl.DeviceIdType.MESH)
                cp.start()
                copies.append(cp)
        for cp in copies:
            cp.wait()

    return pl.pallas_call(
        body, name=name, in_specs=[_HBM] * n, out_specs=[_HBM] * n,
        out_shape=[jax.ShapeDtypeStruct((nchip,) + a.shape[1:], a.dtype) for a in arrs],
        scratch_shapes=[pltpu.SemaphoreType.DMA((n, nchip)), pltpu.SemaphoreType.DMA((n, nchip))],
    )(*arrs)


def _chip_exchange(arrs, *, name):
    n = len(arrs)

    def body(*refs):
        ins, outs = refs[:n], refs[n:2 * n]
        send_sems, recv_sems, local_sems = refs[2 * n:]
        x, y, c = lax.axis_index("x"), lax.axis_index("y"), lax.axis_index("c")
        mine = 2 * x + y
        copies = []
        for a in range(n):
            cp = pltpu.make_async_copy(ins[a].at[mine], outs[a].at[mine], local_sems.at[a])
            cp.start()
            copies.append(cp)
            for j, (px, py) in enumerate(_other_chips(x, y)):
                cp = pltpu.make_async_remote_copy(
                    src_ref=ins[a].at[2 * px + py], dst_ref=outs[a].at[mine],
                    send_sem=send_sems.at[a, j], recv_sem=recv_sems.at[a, j],
                    device_id=(px, py, c), device_id_type=pl.DeviceIdType.MESH)
                cp.start()
                copies.append(cp)
        for cp in copies:
            cp.wait()

    return pl.pallas_call(
        body, name=name, in_specs=[_HBM] * n, out_specs=[_HBM] * n,
        out_shape=[jax.ShapeDtypeStruct(a.shape, a.dtype) for a in arrs],
        scratch_shapes=[pltpu.SemaphoreType.DMA((n, 3)), pltpu.SemaphoreType.DMA((n, 3)),
                        pltpu.SemaphoreType.DMA((n,))],
    )(*arrs)


def _pair_sum(a, b, *, name):
    P, R, C = a.shape
    tr = _tile(R, max(SUBLANES, (1 << 19) // C), SUBLANES)

    def body(a_ref, b_ref, o_ref):
        o_ref[...] = (a_ref[...].astype(F32) + b_ref[...].astype(F32)).astype(o_ref.dtype)

    blk = pl.BlockSpec((1, tr, C), lambda p, i: (p, i, 0))
    return pl.pallas_call(
        body, name=name, grid=(P, R // tr), in_specs=[blk, blk], out_specs=blk,
        out_shape=jax.ShapeDtypeStruct(a.shape, a.dtype), compiler_params=_params(("parallel", "parallel")),
    )(a, b)


def _group_in_cols(w, lead):
    X = (w.shape[-1] - lead) // 3
    g = w[..., lead:].reshape(w.shape[:-1] + (3, X // LANES, LANES))
    g = jnp.swapaxes(g, -3, -2).reshape(w.shape[:-1] + (3 * X,))
    return jnp.concatenate([g, w[..., :lead]], axis=-1)


def _ungroup_in_cols(w, lead):
    X = (w.shape[-1] - lead) // 3
    g = w[..., :3 * X].reshape(w.shape[:-1] + (X // LANES, 3, LANES))
    g = jnp.swapaxes(g, -3, -2).reshape(w.shape[:-1] + (3 * X,))
    return jnp.concatenate([w[..., 3 * X:], g], axis=-1)


def _mlp_fwd(h, g, w1, w2, tag):
    hn = _rmsnorm(h, g, name=f"mlp_norm_{tag}")
    act = _matmul(hn, w1, name=f"mlp_up_{tag}", epilogue=_ep_relu2, out_dtypes=(BF16,))
    out = _matmul(act, w2, name=f"mlp_down_{tag}", extras=(h,), epilogue=_ep_add)
    return out, (hn, act)


def _mlp_bwd(dout, dout_b, h, g, w1, w2, saved, tag):
    hn, act = saved
    dw2 = _matmul(act, dout_b, ta=True, name=f"mlp_dw2_{tag}", out_dtypes=(GRAD_WIRE,))
    dz = _matmul(dout_b, w2, tb=True, name=f"mlp_dact_{tag}", extras=(act,), epilogue=_ep_relu2_bwd,
                 out_dtypes=(BF16,))
    dw1 = _matmul(hn, dz, ta=True, name=f"mlp_dw1_{tag}", out_dtypes=(GRAD_WIRE,))
    dhn = _matmul(dz, w1, tb=True, name=f"mlp_dhn_{tag}")
    dh, dh_b, dg = _rmsnorm_bwd(dhn, h, g, dout, name=f"mlp_norm_bwd_{tag}")
    return dh, dh_b, dg, dw1, dw2


class _Lazy:
    def __init__(self, handles, finish, name):
        self.handles, self.finish, self.name, self.done = handles, finish, name, None

    def take(self, after):
        if self.done is None:
            self.done = self.finish(_gather_wait(self.handles, after, name=self.name))
        return self.done


def _weight(W, n, after):
    if isinstance(W[n], _Lazy):
        W.update(W[n].take(after))
    return W[n]


def _local_step(x, target, W, emit=None):
    B, T, D = x.shape
    N = B * T
    G = {}
    row = lambda vec: vec.reshape(1, -1)

    def sent(nxt, grads):
        if emit is None:
            return nxt
        return lax.optimization_barrier((nxt, emit(grads)))[0]

    PW = W["pool_w"].shape[0] * W["pool_w"].shape[1]
    CW = W["conv_b"].shape[-1]
    SW = W["sgu_norm_g"].shape[-1]
    HW = W["att_width"]

    h0 = x.reshape(N, D)
    xn0 = _rmsnorm(h0, row(W["mix_norm_g"][0]), name="mix_norm_0")
    p0 = _matmul(xn0, W["ab_w_in"], name="ab_in")
    cat0 = _pool_fwd(p0, W["pool_w"], W["pool_scale"], B, T, 3 * CW, PW + CW, name="pool_fwd")
    cat0 = _conv_fwd(p0, cat0, W["conv_w"], row(W["conv_b"]), B, T, PW, name="conv_fwd")
    h1 = _matmul(cat0, W["ab_w_out"], name="ab_out", extras=(h0,), epilogue=_ep_add)
    h2, mlp0 = _mlp_fwd(h1, row(W["mlp_norm_g"][0]), _weight(W, "mlp_w1_0", h1), W["mlp_w2_0"], 0)
    xn1 = _rmsnorm(h2, row(W["mix_norm_g"][1]), name="mix_norm_1")
    p1 = _matmul(xn1, _weight(W, "cd_w_in", xn1), name="cd_in")
    ln_g, ln_b = row(W["sgu_norm_g"]), row(W["sgu_norm_b"])
    cat1 = _sgu_fwd(p1, ln_g, ln_b, W["sgu_w"], W["sgu_b"], 3 * HW, SW + HW, name="sgu_fwd")
    cat1, att_carries = _attn_fwd(p1, cat1, B, T, SW, name="attn_fwd")
    h3 = _matmul(cat1, W["cd_w_out"], name="cd_out", extras=(h2,), epilogue=_ep_add)
    h4, mlp1 = _mlp_fwd(h3, row(W["mlp_norm_g"][1]), _weight(W, "mlp_w1_1", h3), W["mlp_w2_1"], 1)

    loss, dh4, dh4_b, G["final_norm_g"] = _final_loss(h4, row(W["final_norm_g"]), target.reshape(N, D),
                                                      name="final_loss")

    dh3, dh3_b, dmlp_g1, dw1_1, dw2_1 = _mlp_bwd(dh4, dh4_b, h3, row(W["mlp_norm_g"][1]), W["mlp_w1_1"],
                                                 W["mlp_w2_1"], mlp1, 1)
    dh3_b = sent(dh3_b, {"mlp_w1_1": dw1_1, "mlp_w2_1": dw2_1})
    G["cd_w_out"] = _matmul(cat1, dh3_b, ta=True, name="cd_out_dw", out_dtypes=(GRAD_WIRE,))[None]
    dcat1 = _matmul(dh3_b, W["cd_w_out"], tb=True, name="cd_out_dx")
    dp1, G["sgu_w"], dsgu_b, G["sgu_norm_g"], G["sgu_norm_b"] = _sgu_bwd(
        p1, dcat1, ln_g, ln_b, W["sgu_w"], W["sgu_b"], 3 * HW, name="sgu_bwd")
    G["sgu_b"] = dsgu_b.reshape(W["sgu_b"].shape)
    dp1 = _attn_bwd(p1, dcat1, att_carries, dp1, B, T, SW, name="attn_bwd")
    G["cd_w_in"] = _matmul(xn1, dp1, ta=True, name="cd_in_dw", out_dtypes=(GRAD_WIRE,))[None]
    dp1 = sent(dp1, {"cd_w_out": G["cd_w_out"][0], "cd_w_in": G["cd_w_in"][0]})
    dxn1 =_matmul(dp1, W["cd_w_in"], tb=True, name="cd_in_dx")
    dh2, dh2_b, dmix_g1 = _rmsnorm_bwd(dxn1, h2, row(W["mix_norm_g"][1]), dh3, name="mix_norm_bwd_1")

    dh1, dh1_b, dmlp_g0, dw1_0, dw2_0 = _mlp_bwd(dh2, dh2_b, h1, row(W["mlp_norm_g"][0]), W["mlp_w1_0"],
                                                 W["mlp_w2_0"], mlp0, 0)
    dh1_b = sent(dh1_b, {"mlp_w1_0": dw1_0, "mlp_w2_0": dw2_0})
    G["ab_w_out"] =_matmul(cat0, dh1_b, ta=True, name="ab_out_dw", out_dtypes=(GRAD_WIRE,))[None]
    dcat0 = _matmul(dh1_b, W["ab_w_out"], tb=True, name="ab_out_dx")
    dp0, G["pool_w"], dps = _pool_bwd(p0, dcat0, W["pool_w"], W["pool_scale"], B, T, 3 * CW, name="pool_bwd")
    G["pool_scale"] = dps.reshape(W["pool_scale"].shape)
    dp0, G["conv_w"], dcb = _conv_bwd(p0, dcat0, dp0, W["conv_w"], row(W["conv_b"]), B, T, PW, name="conv_bwd")
    G["conv_b"] = dcb.reshape(-1)
    G["ab_w_in"] = _matmul(xn0, dp0, ta=True, name="ab_in_dw", out_dtypes=(GRAD_WIRE,))[None]
    dp0 = sent(dp0, {"ab_w_out": G["ab_w_out"][0], "ab_w_in": G["ab_w_in"][0]})
    dxn0 =_matmul(dp0, W["ab_w_in"], tb=True, name="ab_in_dx")
    dx, _, dmix_g0 = _rmsnorm_bwd(dxn0, h0, row(W["mix_norm_g"][0]), dh1, name="mix_norm_bwd_0")

    G["mix_norm_g"] = jnp.concatenate([dmix_g0, dmix_g1], axis=0)
    G["mlp_norm_g"] = jnp.concatenate([dmlp_g0, dmlp_g1], axis=0)
    G["mlp_w1"] = jnp.stack([dw1_0, dw1_1])
    G["mlp_w2"] = jnp.stack([dw2_0, dw2_1])
    G["final_norm_g"] = G["final_norm_g"].reshape(-1)
    G["sgu_norm_g"] = G["sgu_norm_g"].reshape(-1)
    G["sgu_norm_b"] = G["sgu_norm_b"].reshape(-1)
    return loss[0, 0], dx.reshape(B, T, D), G


_NAMES = ["mix_norm_g", "mlp_norm_g", "ab_w_in", "pool_w", "pool_scale", "conv_w", "conv_b", "ab_w_out",
          "cd_w_in", "sgu_norm_g", "sgu_norm_b", "sgu_w", "sgu_b", "cd_w_out", "mlp_w1", "mlp_w2",
          "final_norm_g"]
_COL_SHARDED = ["ab_w_in", "cd_w_in", "mlp_w1"]
_ROW_SHARDED = ["ab_w_out", "cd_w_out", "mlp_w2"]
_SMALL_SHARDED = ["conv_w", "sgu_norm_g", "sgu_norm_b"]
_REPLICATED = ["mix_norm_g", "mlp_norm_g", "pool_w", "pool_scale", "conv_b", "sgu_w", "sgu_b", "final_norm_g"]


def _pad_rows(a2d, mult=SUBLANES):
    pad = (-a2d.shape[0]) % mult
    return jnp.pad(a2d, ((0, pad), (0, 0))) if pad else a2d


def _pack(arrays):
    return _pad_rows(jnp.concatenate([a.reshape(-1, LANES) for a in arrays], axis=0))


def _unpack(packed, shapes):
    out, r = [], 0
    for s in shapes:
        n = math.prod(s) // LANES
        out.append(packed[r:r + n].reshape(s))
        r += n
    return out


def _small_shard_pack(arrays):
    rows = [jnp.pad(a.reshape(-1, a.shape[-1]), ((0, 0), (0, LANES - a.shape[-1]))) for a in arrays]
    return _pad_rows(jnp.concatenate(rows, axis=0))


def _cols_to_chunks(a):
    n = a.shape[-1] // N_DEV
    return jnp.moveaxis(a.reshape(a.shape[:-1] + (N_DEV, n)), -2, 0)


def _chunks_to_cols(a):
    t = jnp.moveaxis(a, 0, -2)
    return t.reshape(t.shape[:-2] + (t.shape[-2] * t.shape[-1],))


def _rows_to_chunks(a):
    r = a.shape[-2] // N_DEV
    return jnp.moveaxis(a.reshape(a.shape[:-2] + (N_DEV, r, a.shape[-1])), -3, 0)


def _chunks_to_rows(a):
    t = jnp.moveaxis(a, 0, -3)
    return t.reshape(t.shape[:-3] + (t.shape[-3] * t.shape[-2], t.shape[-1]))


def kernel(x, mix_norm_g, mlp_norm_g, ab_w_in, pool_w, pool_scale, conv_w, conv_b, ab_w_out, cd_w_in, sgu_norm_g, sgu_norm_b, sgu_w, sgu_b, cd_w_out, mlp_w1, mlp_w2, final_norm_g, loss_target, m_mix_norm_g, m_mlp_norm_g, m_ab_w_in, m_pool_w, m_pool_scale, m_conv_w, m_conv_b, m_ab_w_out, m_cd_w_in, m_sgu_norm_g, m_sgu_norm_b, m_sgu_w, m_sgu_b, m_cd_w_out, m_mlp_w1, m_mlp_w2, m_final_norm_g, v_mix_norm_g, v_mlp_norm_g, v_ab_w_in, v_pool_w, v_pool_scale, v_conv_w, v_conv_b, v_ab_w_out, v_cd_w_in, v_sgu_norm_g, v_sgu_norm_b, v_sgu_w, v_sgu_b, v_cd_w_out, v_mlp_w1, v_mlp_w2, v_final_norm_g):
    w = dict(zip(_NAMES, (mix_norm_g, mlp_norm_g, ab_w_in, pool_w, pool_scale, conv_w, conv_b, ab_w_out, cd_w_in,
                          sgu_norm_g, sgu_norm_b, sgu_w, sgu_b, cd_w_out, mlp_w1, mlp_w2, final_norm_g)))
    m = dict(zip(_NAMES, (m_mix_norm_g, m_mlp_norm_g, m_ab_w_in, m_pool_w, m_pool_scale, m_conv_w, m_conv_b,
                          m_ab_w_out, m_cd_w_in, m_sgu_norm_g, m_sgu_norm_b, m_sgu_w, m_sgu_b, m_cd_w_out,
                          m_mlp_w1, m_mlp_w2, m_final_norm_g)))
    v = dict(zip(_NAMES, (v_mix_norm_g, v_mlp_norm_g, v_ab_w_in, v_pool_w, v_pool_scale, v_conv_w, v_conv_b,
                          v_ab_w_out, v_cd_w_in, v_sgu_norm_g, v_sgu_norm_b, v_sgu_w, v_sgu_b, v_cd_w_out,
                          v_mlp_w1, v_mlp_w2, v_final_norm_g)))
    big = _COL_SHARDED + _ROW_SHARDED
    me = _my_index()

    small_sh = _small_shard_pack([w[n] for n in _SMALL_SHARDED])
    in_lead = {"ab_w_in": pool_w.shape[1] * pool_w.shape[2], "cd_w_in": 2 * sgu_norm_g.shape[-1] * N_DEV}
    shard = {"ab_w_in": ab_w_in[0], "ab_w_out": ab_w_out[0], "cd_w_in": cd_w_in[0], "cd_w_out": cd_w_out[0]}
    for layer in range(mlp_w1.shape[0]):
        shard[f"mlp_w1_{layer}"], shard[f"mlp_w2_{layer}"] = mlp_w1[layer], mlp_w2[layer]
    shard = {n: a.astype(BF16) for n, a in shard.items()}

    def whole(n, g):
        if n.endswith("_in") or n.startswith("mlp_w1"):
            full = _chunks_to_cols(g)
            return _group_in_cols(full, in_lead[n]) if n in in_lead else full
        return _chunks_to_rows(g)

    now = ["ab_w_in", "ab_w_out", "mlp_w1_0", "mlp_w2_0"]
    gathered = _gather_two_level([shard[n] for n in now] + [small_sh], name="gather_weights")
    W = {n: whole(n, g) for n, g in zip(now, gathered)}
    W["att_width"] = cd_w_out.shape[1] * N_DEV - sgu_norm_g.shape[-1] * N_DEV
    token = gathered[-1][0, :SUBLANES]
    for group, tag in ((["cd_w_in", "cd_w_out"], "cd"), (["mlp_w1_1", "mlp_w2_1"], "mlp1")):
        srcs, _ = lax.optimization_barrier(([shard[n] for n in group], token))
        handles, token = _gather_start(srcs, name=f"gather_{tag}_start")
        lazy = _Lazy(handles, lambda got, group=group: {n: whole(n, g) for n, g in zip(group, got)},
                     f"gather_{tag}_wait")
        for n in group:
            W[n] = lazy
    x, _ = lax.optimization_barrier((x, token))
    small_full = gathered[-1]
    r = 0
    for n in _SMALL_SHARDED:
        rows, width = math.prod(w[n].shape[:-1]), w[n].shape[-1]
        W[n] = _chunks_to_cols(small_full[:, r:r + rows, :width])
        r += rows
    for n in _REPLICATED:
        W[n] = w[n]
    for n in ("pool_w", "pool_scale", "sgu_w", "sgu_b"):
        W[n] = W[n][0]

    pending = []

    def emit(grads):
        names = list(grads)
        parts = []
        for n in names:
            g = _ungroup_in_cols(grads[n], in_lead[n]) if n in in_lead else grads[n]
            parts.append(_cols_to_chunks(g) if n.endswith("_in") or n.startswith("mlp_w1") else _rows_to_chunks(g))
        handles, sent_token = _gather_start(parts, name=f"grads_{names[0]}_start", scatter=True)
        pending.append((names, handles))
        return sent_token

    loss_part, grad_x, G = _local_step(x, loss_target, W, emit)

    landed = {}
    for names, handles in pending:
        got = _gather_wait(handles, grad_x, name=f"grads_{names[0]}_wait", scatter=True)
        landed.update(zip(names, got))
    small_names = _REPLICATED + _SMALL_SHARDED
    small_grads = [G[n].reshape(-1) for n in small_names]
    loss_row = jnp.full((LANES,), loss_part, F32)
    small_pack = _pack(small_grads + [loss_row])
    small_parts = _exchange([small_pack], [True], name="gather_small_grads")[0]

    grads, deltas, new_m, new_v = {}, {}, {}, {}
    for n in big:
        layers = [landed[f"{n}_{l}"] for l in range(w[n].shape[0])] if n.startswith("mlp") else [landed[n]]
        grads[n], deltas[n], new_m[n], new_v[n] = _adamw(w[n], layers, m[n], v[n], name=f"adamw_{n}")

    rep_shapes = [w[n].shape for n in _REPLICATED]
    rep_rows = sum(math.prod(s) for s in rep_shapes) // LANES
    small_sum_shapes = [(G[n].size,) for n in small_names] + [(LANES,)]
    zero_tail = [jnp.zeros((math.prod(s),), F32) for s in small_sum_shapes[len(_REPLICATED):]]
    w_pack = _pack([w[n] for n in _REPLICATED] + zero_tail)
    m_pack = _pack([m[n] for n in _REPLICATED] + zero_tail)
    v_pack = _pack([v[n] for n in _REPLICATED] + zero_tail)
    outs = [o[0] for o in _adamw(w_pack[None], [small_parts], m_pack[None], v_pack[None], name="adamw_small")]
    summed = _unpack(outs[0], small_sum_shapes)
    for i, n in enumerate(_REPLICATED):
        grads[n] = summed[i].reshape(w[n].shape)
    for dst, o in zip((deltas, new_m, new_v), outs[1:]):
        for n, val in zip(_REPLICATED, _unpack(o[:rep_rows], rep_shapes)):
            dst[n] = val
    loss = summed[-1][0]

    shard_g = []
    for i, n in enumerate(_SMALL_SHARDED):
        full = summed[len(_REPLICATED) + i].reshape(w[n].shape[:-1] + (-1,))
        width = w[n].shape[-1]
        shard_g.append(lax.dynamic_slice_in_dim(full, me * width, width, axis=full.ndim - 1))
    g_sh = _small_shard_pack(shard_g)
    m_sh = _small_shard_pack([m[n] for n in _SMALL_SHARDED])
    v_sh = _small_shard_pack([v[n] for n in _SMALL_SHARDED])
    outs = [o[0] for o in _adamw(small_sh[None], [g_sh[None]], m_sh[None], v_sh[None], name="adamw_small_sharded")]
    r = 0
    for n in _SMALL_SHARDED:
        rows, width = math.prod(w[n].shape[:-1]), w[n].shape[-1]
        for dst, o in zip((grads, deltas, new_m, new_v), outs):
            dst[n] = o[r:r + rows, :width].reshape(w[n].shape)
        r += rows

    return (loss, grad_x, *[grads[n] for n in _NAMES], *[deltas[n] for n in _NAMES],
            *[new_m[n] for n in _NAMES], *[new_v[n] for n in _NAMES])
```

```python
import functools
import math

import jax
import jax.numpy as jnp
from jax import lax
from jax.experimental import pallas as pl
from jax.experimental.pallas import tpu as pltpu

F32 = jnp.float32
BF16 = jnp.bfloat16
GRAD_WIRE = jnp.bfloat16

NORM_EPS = 1e-6
ADAM_LR = 0.001
ADAM_B1 = 0.9
ADAM_B2 = 0.999
ADAM_EPS = 1e-08
ADAM_WD = 0.01
ADAM_STEP = 10
ADAM_C1 = 1.0 - ADAM_B1 ** ADAM_STEP
ADAM_C2 = 1.0 - ADAM_B2 ** ADAM_STEP

N_DEV = 8
LANES = 128
SUBLANES = 8
SB_DH = 64
ATT_BLOCK = 256
POOL_LOG_WINDOWS = 4
VMEM_LIMIT = 56 * 1024 * 1024


def _params(semantics=None):
    return pltpu.CompilerParams(dimension_semantics=semantics, vmem_limit_bytes=VMEM_LIMIT)


def _tile(dim, pref, unit=LANES):
    if dim <= pref:
        return dim
    t = (pref // unit) * unit
    while t >= unit:
        if dim % t == 0:
            return t
        t -= unit
    return dim


def _matmul(a, b, *, name, ta=False, tb=False, extras=(), epilogue=None, out_dtypes=(F32,),
            tm=1024, tn=1024, tk=1024):
    M, K = (a.shape[1], a.shape[0]) if ta else a.shape
    N = b.shape[0] if tb else b.shape[1]
    assert (b.shape[1] if tb else b.shape[0]) == K, (a.shape, b.shape)
    tm, tn, tk = _tile(M, tm), _tile(N, tn), _tile(K, tk)
    nk = K // tk
    dims = (((0 if ta else 1,), (1 if tb else 0,)), ((), ()))
    ne, no = len(extras), len(out_dtypes)

    def body(*refs):
        a_ref, b_ref = refs[0], refs[1]
        e_refs = refs[2:2 + ne]
        o_refs = refs[2 + ne:2 + ne + no]
        k = pl.program_id(2)

        def part():
            return lax.dot_general(a_ref[...].astype(BF16), b_ref[...].astype(BF16), dims,
                                   preferred_element_type=F32)

        def finish(acc):
            outs = epilogue(acc, *[e[...] for e in e_refs]) if epilogue is not None else (acc,)
            for o_ref, val in zip(o_refs, outs):
                o_ref[...] = val.astype(o_ref.dtype)

        if nk == 1:
            finish(part())
        else:
            acc_ref = refs[-1]

            @pl.when(k == 0)
            def _():
                acc_ref[...] = jnp.zeros_like(acc_ref)

            acc_ref[...] += part()

            @pl.when(k == nk - 1)
            def _():
                finish(acc_ref[...])

    a_spec = (pl.BlockSpec((tk, tm), lambda i, j, k: (k, i)) if ta
              else pl.BlockSpec((tm, tk), lambda i, j, k: (i, k)))
    b_spec = (pl.BlockSpec((tn, tk), lambda i, j, k: (j, k)) if tb
              else pl.BlockSpec((tk, tn), lambda i, j, k: (k, j)))
    o_spec = pl.BlockSpec((tm, tn), lambda i, j, k: (i, j))
    outs = pl.pallas_call(
        body,
        name=name,
        grid=(M // tm, N // tn, nk),
        in_specs=[a_spec, b_spec] + [o_spec] * ne,
        out_specs=[o_spec] * no,
        out_shape=[jax.ShapeDtypeStruct((M, N), dt) for dt in out_dtypes],
        scratch_shapes=[pltpu.VMEM((tm, tn), F32)] if nk > 1 else [],
        compiler_params=_params(("parallel", "parallel", "arbitrary")),
    )(a, b, *extras)
    return outs[0] if no == 1 else outs


def _ep_add(acc, res):
    return (acc + res,)


def _ep_relu2(acc):
    r = jnp.maximum(acc, 0.0)
    return (r * r,)


def _ep_relu2_bwd(acc, act):
    return (acc * (2.0 * jnp.sqrt(act.astype(F32))),)


def _rstd(x):
    return lax.rsqrt(jnp.mean(x * x, axis=-1, keepdims=True) + NORM_EPS)


def _rmsnorm(h, g, *, name, tr=512):
    N, D = h.shape
    tr = _tile(N, tr, SUBLANES)

    def body(h_ref, g_ref, o_ref):
        x = h_ref[...]
        o_ref[...] = ((x * _rstd(x)) * g_ref[...]).astype(o_ref.dtype)

    row = pl.BlockSpec((tr, D), lambda i: (i, 0))
    vec = pl.BlockSpec((1, D), lambda i: (0, 0))
    return pl.pallas_call(
        body, name=name, grid=(N // tr,), in_specs=[row, vec], out_specs=row,
        out_shape=jax.ShapeDtypeStruct((N, D), BF16), compiler_params=_params(("parallel",)),
    )(h, g)


def _rmsnorm_bwd(dy, h, g, dres, *, name, tr=512):
    N, D = h.shape
    tr = _tile(N, tr, SUBLANES)

    def body(dy_ref, h_ref, g_ref, r_ref, dh_ref, dhb_ref, dg_ref):
        i = pl.program_id(0)
        x = h_ref[...]
        d = dy_ref[...]
        r = _rstd(x)
        xh = x * r

        @pl.when(i == 0)
        def _():
            dg_ref[...] = jnp.zeros_like(dg_ref)

        dg_ref[...] += jnp.sum(d * xh, axis=0, keepdims=True)
        dxh = d * g_ref[...]
        dh = r_ref[...] + r * (dxh - xh * jnp.mean(dxh * xh, axis=-1, keepdims=True))
        dh_ref[...] = dh
        dhb_ref[...] = dh.astype(dhb_ref.dtype)

    row = pl.BlockSpec((tr, D), lambda i: (i, 0))
    vec = pl.BlockSpec((1, D), lambda i: (0, 0))
    return pl.pallas_call(
        body, name=name, grid=(N // tr,), in_specs=[row, row, vec, row], out_specs=[row, row, vec],
        out_shape=[jax.ShapeDtypeStruct((N, D), F32), jax.ShapeDtypeStruct((N, D), BF16),
                   jax.ShapeDtypeStruct((1, D), F32)],
        compiler_params=_params(("arbitrary",)),
    )(dy, h, g, dres)


def _final_loss(h, g, target, *, name, tr=512):
    N, D = h.shape
    tr = _tile(N, tr, SUBLANES)

    def body(h_ref, g_ref, t_ref, loss_ref, dh_ref, dhb_ref, dg_ref):
        i = pl.program_id(0)
        x = h_ref[...]
        gg = g_ref[...]
        r = _rstd(x)
        xh = x * r
        err = xh * gg - t_ref[...]

        @pl.when(i == 0)
        def _():
            dg_ref[...] = jnp.zeros_like(dg_ref)
            loss_ref[...] = jnp.zeros_like(loss_ref)

        per_row = jnp.mean(err * err, axis=-1, keepdims=True)
        loss_ref[...] += 0.5 * jnp.sum(per_row, axis=0, keepdims=True)
        dy = err * (1.0 / D)
        dg_ref[...] += jnp.sum(dy * xh, axis=0, keepdims=True)
        dxh = dy * gg
        dh = r * (dxh - xh * jnp.mean(dxh * xh, axis=-1, keepdims=True))
        dh_ref[...] = dh
        dhb_ref[...] = dh.astype(dhb_ref.dtype)

    row = pl.BlockSpec((tr, D), lambda i: (i, 0))
    vec = pl.BlockSpec((1, D), lambda i: (0, 0))
    lvec = pl.BlockSpec((1, LANES), lambda i: (0, 0))
    return pl.pallas_call(
        body, name=name, grid=(N // tr,), in_specs=[row, vec, row], out_specs=[lvec, row, row, vec],
        out_shape=[jax.ShapeDtypeStruct((1, LANES), F32), jax.ShapeDtypeStruct((N, D), F32),
                   jax.ShapeDtypeStruct((N, D), BF16), jax.ShapeDtypeStruct((1, D), F32)],
        compiler_params=_params(("arbitrary",)),
    )(h, g, target)


def _shift_down(x, s):
    t = lax.broadcasted_iota(jnp.int32, x.shape, 0)
    return jnp.where(t >= s, pltpu.roll(x, s, 0), 0.0)


def _shift_up(x, s):
    n = x.shape[0]
    t = lax.broadcasted_iota(jnp.int32, x.shape, 0)
    return jnp.where(t < n - s, pltpu.roll(x, n - s, 0), 0.0)


def _window_sum(x, g, shift):
    s = x + shift(x, 1)
    for k in range(1, POOL_LOG_WINDOWS):
        s = jnp.where(k <= g, s + shift(s, 2 ** k), s)
    return s


def _pool_count(shape, g):
    t = lax.broadcasted_iota(jnp.int32, shape, 0)
    return jnp.minimum(t + 1, lax.shift_left(jnp.int32(2), g)).astype(F32)


def _pool_fwd(p, pool_w, pool_scale, B, T, off, width, *, name):
    G, dh = pool_w.shape[0], pool_w.shape[1]
    assert G == POOL_LOG_WINDOWS and off % dh == 0
    base = off // dh

    def body(a_ref, w_ref, s_ref, o_ref):
        g = pl.program_id(0)
        a = a_ref[...]
        pooled = _window_sum(a, g, _shift_down) / _pool_count(a.shape, g) - a
        m = jnp.dot(pooled.astype(BF16), w_ref[0].astype(BF16), preferred_element_type=F32)
        o_ref[...] = (m * s_ref[0]).astype(o_ref.dtype)

    return pl.pallas_call(
        body, name=name, grid=(G, B),
        in_specs=[pl.BlockSpec((T, dh), lambda g, b: (b, base + g)),
                  pl.BlockSpec((1, dh, dh), lambda g, b: (g, 0, 0)),
                  pl.BlockSpec((1, 1, dh), lambda g, b: (g, 0, 0))],
        out_specs=pl.BlockSpec((T, dh), lambda g, b: (b, g)),
        out_shape=jax.ShapeDtypeStruct((B * T, width), BF16),
        compiler_params=_params(("parallel", "parallel")),
    )(p, pool_w, pool_scale.reshape(G, 1, dh))


def _pool_bwd(p, dcat, pool_w, pool_scale, B, T, off, *, name):
    G, dh = pool_w.shape[0], pool_w.shape[1]
    base = off // dh

    def body(a_ref, d_ref, w_ref, s_ref, da_ref, dw_ref, ds_ref):
        g = pl.program_id(0)
        b = pl.program_id(1)
        a = a_ref[...]
        d = d_ref[...]
        cnt = _pool_count(a.shape, g)
        pooled = (_window_sum(a, g, _shift_down) / cnt - a).astype(BF16)
        w = w_ref[0].astype(BF16)
        m = jnp.dot(pooled, w, preferred_element_type=F32)

        @pl.when(b == 0)
        def _():
            dw_ref[...] = jnp.zeros_like(dw_ref)
            ds_ref[...] = jnp.zeros_like(ds_ref)

        ds_ref[0] += jnp.sum(d * m, axis=0, keepdims=True)
        dm = (d * s_ref[0]).astype(BF16)
        dw_ref[0] += lax.dot_general(pooled, dm, (((0,), (0,)), ((), ())), preferred_element_type=F32)
        dpooled = lax.dot_general(dm, w, (((1,), (1,)), ((), ())), preferred_element_type=F32)
        da = _window_sum(dpooled / cnt, g, _shift_up) - dpooled
        da_ref[...] = da.astype(da_ref.dtype)

    pblk = pl.BlockSpec((T, dh), lambda g, b: (b, base + g))
    dblk = pl.BlockSpec((T, dh), lambda g, b: (b, g))
    wspec = pl.BlockSpec((1, dh, dh), lambda g, b: (g, 0, 0))
    sspec = pl.BlockSpec((1, 1, dh), lambda g, b: (g, 0, 0))
    return pl.pallas_call(
        body, name=name, grid=(G, B), in_specs=[pblk, dblk, wspec, sspec], out_specs=[pblk, wspec, sspec],
        out_shape=[jax.ShapeDtypeStruct((B * T, p.shape[1]), BF16), jax.ShapeDtypeStruct((G, dh, dh), F32),
                   jax.ShapeDtypeStruct((G, 1, dh), F32)],
        compiler_params=_params(("parallel", "arbitrary")),
    )(p, dcat, pool_w, pool_scale.reshape(G, 1, dh))


_ANY = pl.BlockSpec(memory_space=pl.ANY)


def _conv_fwd(p, cat, conv_w, conv_b, B, T, coff, *, name):
    CW = conv_w.shape[1]
    tc = LANES
    assert coff % tc == 0 and CW % tc == 0
    cbase = coff // tc

    def body(p_ref, cat_ref, w_ref, b_ref, o_ref):
        xb, gb, gc = p_ref[:, 0:tc], p_ref[:, tc:2 * tc], p_ref[:, 2 * tc:3 * tc]
        c = gc * xb
        w = w_ref[...]
        y = _shift_down(c, 2) * w[0:1] + _shift_down(c, 1) * w[1:2] + c * w[2:3] + b_ref[...]
        o_ref[...] = (gb * y).astype(o_ref.dtype)

    return pl.pallas_call(
        body, name=name, grid=(CW // tc, B),
        in_specs=[pl.BlockSpec((T, 3 * tc), lambda j, b: (b, j)), _ANY,
                  pl.BlockSpec((3, tc), lambda j, b: (0, j)), pl.BlockSpec((1, tc), lambda j, b: (0, j))],
        out_specs=pl.BlockSpec((T, tc), lambda j, b: (b, cbase + j)),
        out_shape=jax.ShapeDtypeStruct(cat.shape, cat.dtype), input_output_aliases={1: 0},
        compiler_params=_params(("parallel", "parallel")),
    )(p, cat, conv_w, conv_b)


def _conv_bwd(p, dcat, dp, conv_w, conv_b, B, T, coff, *, name):
    CW = conv_w.shape[1]
    tc = LANES
    assert coff % tc == 0
    cbase = coff // tc

    def body(p_ref, d_ref, dp_in_ref, w_ref, b_ref, dp_ref, dw_ref, db_ref):
        b = pl.program_id(1)
        xb, gb, gc = p_ref[:, 0:tc], p_ref[:, tc:2 * tc], p_ref[:, 2 * tc:3 * tc]
        d = d_ref[...]
        w = w_ref[...]
        c = gc * xb
        c1 = _shift_down(c, 1)
        c2 = _shift_down(c, 2)
        y = c2 * w[0:1] + c1 * w[1:2] + c * w[2:3] + b_ref[...]
        dy = d * gb
        dp_ref[:, tc:2 * tc] = (d * y).astype(dp_ref.dtype)

        @pl.when(b == 0)
        def _():
            dw_ref[...] = jnp.zeros_like(dw_ref)
            db_ref[...] = jnp.zeros_like(db_ref)

        db_ref[...] += jnp.sum(dy, axis=0, keepdims=True)
        dw_ref[0:1, :] += jnp.sum(dy * c2, axis=0, keepdims=True)
        dw_ref[1:2, :] += jnp.sum(dy * c1, axis=0, keepdims=True)
        dw_ref[2:3, :] += jnp.sum(dy * c, axis=0, keepdims=True)
        dc = dy * w[2:3] + _shift_up(dy, 1) * w[1:2] + _shift_up(dy, 2) * w[0:1]
        dp_ref[:, 2 * tc:3 * tc] = (dc * xb).astype(dp_ref.dtype)
        dp_ref[:, 0:tc] = (dc * gc).astype(dp_ref.dtype)

    wspec = pl.BlockSpec((3, tc), lambda j, b: (0, j))
    bspec = pl.BlockSpec((1, tc), lambda j, b: (0, j))
    pblk = pl.BlockSpec((T, 3 * tc), lambda j, b: (b, j))
    return pl.pallas_call(
        body, name=name, grid=(CW // tc, B),
        in_specs=[pblk, pl.BlockSpec((T, tc), lambda j, b: (b, cbase + j)), _ANY, wspec, bspec],
        out_specs=[pblk, wspec, bspec],
        out_shape=[jax.ShapeDtypeStruct(dp.shape, dp.dtype), jax.ShapeDtypeStruct((3, CW), F32),
                   jax.ShapeDtypeStruct((1, CW), F32)],
        input_output_aliases={2: 0},
        compiler_params=_params(("parallel", "arbitrary")),
    )(p, dcat, dp, conv_w, conv_b)


_SQRT_HALF = 0.7071067811865476
_INV_SQRT_2PI = 0.3989422804014327


def _gelu(x):
    return x * (lax.erf(x * _SQRT_HALF) + 1.0) * 0.5


def _gelu_grad(x):
    return 0.5 * (lax.erf(x * _SQRT_HALF) + 1.0) + x * (_INV_SQRT_2PI * jnp.exp(-0.5 * x * x))


def _layernorm_parts(v):
    mu = jnp.mean(v, axis=-1, keepdims=True)
    vc = v - mu
    rstd = lax.rsqrt(jnp.mean(vc * vc, axis=-1, keepdims=True) + NORM_EPS)
    return vc * rstd, rstd


def _tril_mask(L):
    r = lax.broadcasted_iota(jnp.int32, (L, L), 0)
    c = lax.broadcasted_iota(jnp.int32, (L, L), 1)
    return r >= c


def _sgu_fwd(p, ln_g, ln_b, sgu_w, sgu_b, off, width, *, name, tr=512):
    N = p.shape[0]
    G, L = sgu_w.shape[0], sgu_w.shape[1]
    SW = ln_g.shape[1]
    dh = SW // G
    assert off % SW == 0
    ub = off // SW
    tr = _tile(N, tr, L)
    assert tr % L == 0

    def body(u_ref, v_ref, g_ref, beta_ref, w_ref, b_ref, o_ref):
        u = _gelu(u_ref[...])
        vhat, _ = _layernorm_parts(_gelu(v_ref[...]))
        vn = (vhat * g_ref[...] + beta_ref[...]).astype(BF16)
        mask = _tril_mask(L)
        for gi in range(G):
            w = jnp.where(mask, w_ref[gi], 0.0).astype(BF16)
            bias = b_ref[gi]
            cols = slice(gi * dh, (gi + 1) * dh)
            for n in range(tr // L):
                rows = slice(n * L, (n + 1) * L)
                s = jnp.dot(w, vn[rows, cols], preferred_element_type=F32) + bias
                o_ref[rows, cols] = (u[rows, cols] * s).astype(o_ref.dtype)

    def col(k):
        return pl.BlockSpec((tr, SW), lambda i: (i, k))

    vec = pl.BlockSpec((1, SW), lambda i: (0, 0))
    return pl.pallas_call(
        body, name=name, grid=(N // tr,),
        in_specs=[col(ub), col(ub + 1), vec, vec, pl.BlockSpec((G, L, L), lambda i: (0, 0, 0)),
                  pl.BlockSpec((G, L, 1), lambda i: (0, 0, 0))],
        out_specs=col(0), out_shape=jax.ShapeDtypeStruct((N, width), BF16),
        compiler_params=_params(("parallel",)),
    )(p, p, ln_g, ln_b, sgu_w, sgu_b.reshape(G, L, 1))


def _sgu_bwd(p, dcat, ln_g, ln_b, sgu_w, sgu_b, off, *, name, tr=512):
    N = p.shape[0]
    G, L = sgu_w.shape[0], sgu_w.shape[1]
    SW = ln_g.shape[1]
    dh = SW // G
    assert off % SW == 0
    ub = off // SW
    tr = _tile(N, tr, L)

    def compute(i, u_ref, v_ref, dc_ref, g_ref, beta_ref, w_ref, b_ref,
                du_ref, dv_ref, dw_ref, db_ref, dg_ref, dbeta_ref, du_s, dvn_s):
        pu = u_ref[...]
        pv = v_ref[...]
        u = _gelu(pu)
        vhat, rstd = _layernorm_parts(_gelu(pv))
        gg = g_ref[...]
        vn = (vhat * gg + beta_ref[...]).astype(BF16)
        dc = dc_ref[...]
        mask = _tril_mask(L)

        @pl.when(i == 0)
        def _():
            dw_ref[...] = jnp.zeros_like(dw_ref)
            db_ref[...] = jnp.zeros_like(db_ref)
            dg_ref[...] = jnp.zeros_like(dg_ref)
            dbeta_ref[...] = jnp.zeros_like(dbeta_ref)

        for gi in range(G):
            w = jnp.where(mask, w_ref[gi], 0.0).astype(BF16)
            bias = b_ref[gi]
            cols = slice(gi * dh, (gi + 1) * dh)
            dw_acc = jnp.zeros((L, L), F32)
            db_acc = jnp.zeros((L, 1), F32)
            for n in range(tr // L):
                rows = slice(n * L, (n + 1) * L)
                vb = vn[rows, cols]
                s = jnp.dot(w, vb, preferred_element_type=F32) + bias
                du_s[rows, cols] = dc[rows, cols] * s
                ds = dc[rows, cols] * u[rows, cols]
                db_acc += jnp.sum(ds, axis=1, keepdims=True)
                dsb = ds.astype(BF16)
                dw_acc += lax.dot_general(dsb, vb, (((1,), (1,)), ((), ())), preferred_element_type=F32)
                dvn_s[rows, cols] = lax.dot_general(w, dsb, (((0,), (0,)), ((), ())),
                                                    preferred_element_type=F32)
            dw_ref[gi] += jnp.where(mask, dw_acc, 0.0)
            db_ref[gi] += db_acc

        dvn = dvn_s[...]
        dg_ref[...] += jnp.sum(dvn * vhat, axis=0, keepdims=True)
        dbeta_ref[...] += jnp.sum(dvn, axis=0, keepdims=True)
        dvh = dvn * gg
        dv = rstd * (dvh - jnp.mean(dvh, axis=-1, keepdims=True)
                     - vhat * jnp.mean(dvh * vhat, axis=-1, keepdims=True))
        dv_ref[...] = (dv * _gelu_grad(pv)).astype(dv_ref.dtype)
        du_ref[...] = (du_s[...] * _gelu_grad(pu)).astype(du_ref.dtype)

    def body(u_ref, v_ref, dc_ref, g_ref, beta_ref, w_ref, b_ref,
             dp_ref, dw_ref, db_ref, dg_ref, dbeta_ref, du_s, dvn_s, dv_s):
        i = pl.program_id(0)
        half = pl.program_id(1)

        @pl.when(half == 0)
        def _():
            compute(i, u_ref, v_ref, dc_ref, g_ref, beta_ref, w_ref, b_ref,
                    dp_ref, dv_s, dw_ref, db_ref, dg_ref, dbeta_ref, du_s, dvn_s)

        @pl.when(half == 1)
        def _():
            dp_ref[...] = dv_s[...]

    def col(k):
        return pl.BlockSpec((tr, SW), lambda i, half: (i, k))

    vec = pl.BlockSpec((1, SW), lambda i, half: (0, 0))
    wspec = pl.BlockSpec((G, L, L), lambda i, half: (0, 0, 0))
    bspec = pl.BlockSpec((G, L, 1), lambda i, half: (0, 0, 0))
    return pl.pallas_call(
        body, name=name, grid=(N // tr, 2),
        in_specs=[col(ub), col(ub + 1), col(0), vec, vec, wspec, bspec],
        out_specs=[pl.BlockSpec((tr, SW), lambda i, half: (i, ub + half)), wspec, bspec, vec, vec],
        out_shape=[jax.ShapeDtypeStruct((N, p.shape[1]), BF16),
                   jax.ShapeDtypeStruct((G, L, L), F32), jax.ShapeDtypeStruct((G, L, 1), F32),
                   jax.ShapeDtypeStruct((1, SW), F32), jax.ShapeDtypeStruct((1, SW), F32)],
        scratch_shapes=[pltpu.VMEM((tr, SW), F32), pltpu.VMEM((tr, SW), F32), pltpu.VMEM((tr, SW), BF16)],
        compiler_params=_params(("arbitrary", "arbitrary")),
    )(p, p, dcat, ln_g, ln_b, sgu_w, sgu_b.reshape(G, L, 1))


def _log_sigmoid_pair(z):
    ls = jnp.minimum(z, 0.0) - jnp.log(1.0 + jnp.exp(-jnp.abs(z)))
    return ls, ls - z


def _split_dot(x, m):
    hi = x.astype(BF16)
    lo = (x - hi.astype(F32)).astype(BF16)
    return (jnp.dot(hi, m, preferred_element_type=F32) + jnp.dot(lo, m, preferred_element_type=F32))


def _att_tiles(TB):
    r = lax.broadcasted_iota(jnp.int32, (TB, TB), 0)
    c = lax.broadcasted_iota(jnp.int32, (TB, TB), 1)
    return r, c


def _att_weights(qt, kb, strict, later, carry):
    z = lax.dot_general(qt, kb, (((1,), (1,)), ((), ())), preferred_element_type=F32)
    ls, lk = _log_sigmoid_pair(z)
    if strict is not None:
        lk = jnp.where(strict, lk, 0.0)
    suffix = _split_dot(lk, later) + carry
    a = jnp.exp(ls + suffix)
    if strict is not None:
        a = jnp.where(strict, a, 0.0)
    return ls, lk, suffix, a


HP = LANES // SB_DH
ATT_SCALE = 1.0 / math.sqrt(SB_DH)


def _stage_heads(src_ref, col0, dst_ref, T, scale=None):
    rows = _tile(T, 256, SUBLANES)

    def chunk(n, _):
        r0 = pl.multiple_of(n * rows, rows)
        for hh in range(HP):
            x = src_ref[pl.ds(r0, rows), col0 + hh * SB_DH:col0 + (hh + 1) * SB_DH]
            if scale is not None:
                x = x * scale
            dst_ref[hh, pl.ds(r0, rows), :] = x.astype(dst_ref.dtype)
        return 0

    lax.fori_loop(0, T // rows, chunk, 0)


def _attn_fwd(p, cat, B, T, coff, *, name):
    dh = SB_DH
    nhp = (cat.shape[1] - coff) // LANES
    TB = _tile(T, ATT_BLOCK)
    nb = T // TB
    assert nb <= LANES and coff % LANES == 0
    cbase = coff // LANES

    def body(p_ref, cat_ref, o_ref, c_ref, q_ref, k_ref, v_ref):
        _stage_heads(p_ref, 0, q_ref, T, ATT_SCALE)
        _stage_heads(p_ref, LANES, k_ref, T)
        _stage_heads(p_ref, 2 * LANES, v_ref, T)
        r, c = _att_tiles(TB)
        strict = c < r
        later = (r > c).astype(BF16)
        lane = lax.broadcasted_iota(jnp.int32, (TB, LANES), 1)

        def tile(hh, qt, j, carry, mask):
            k0 = pl.multiple_of(j * TB, TB)
            _, lk, suffix, a = _att_weights(qt, k_ref[hh, pl.ds(k0, TB), :], mask, later, carry)
            pv = jnp.dot(a.astype(BF16), v_ref[hh, pl.ds(k0, TB), :], preferred_element_type=F32)
            return pv, suffix[:, 0:1] + lk[:, 0:1]

        def qblock(i, _):
            q0 = pl.multiple_of(i * TB, TB)
            qts = [q_ref[hh, pl.ds(q0, TB), :] for hh in range(HP)]
            state = []
            for hh in range(HP):
                pv, carry = tile(hh, qts[hh], i, jnp.zeros((TB, 1), F32), strict)
                state += [pv, carry, jnp.zeros((TB, LANES), F32)]

            def kblock(jj, st):
                j = i - jj
                out = []
                for hh in range(HP):
                    acc, carry, cm = st[3 * hh:3 * hh + 3]
                    pv, new_carry = tile(hh, qts[hh], j, carry, None)
                    out += [acc + pv, new_carry, jnp.where(lane == j, carry, cm)]
                return tuple(out)

            st = lax.fori_loop(1, i + 1, kblock, tuple(state))
            for hh in range(HP):
                o_ref[pl.ds(q0, TB), hh * dh:(hh + 1) * dh] = st[3 * hh].astype(o_ref.dtype)
                c_ref[hh, pl.ds(q0, TB), :] = st[3 * hh + 2]
            return 0

        lax.fori_loop(0, nb, qblock, 0)

    staged = pltpu.VMEM((HP, T, dh), BF16)
    return pl.pallas_call(
        body, name=name, grid=(B, nhp),
        in_specs=[pl.BlockSpec((T, 3 * LANES), lambda b, hp: (b, hp)), _ANY],
        out_specs=[pl.BlockSpec((T, LANES), lambda b, hp: (b, cbase + hp)),
                   pl.BlockSpec((HP, T, LANES), lambda b, hp: (b * nhp + hp, 0, 0))],
        out_shape=[jax.ShapeDtypeStruct(cat.shape, cat.dtype),
                   jax.ShapeDtypeStruct((B * nhp * HP, T, LANES), F32)],
        input_output_aliases={1: 0}, scratch_shapes=[staged, staged, staged],
        compiler_params=_params(("parallel", "parallel")),
    )(p, cat)


def _attn_bwd(p, dcat, carries, dp, B, T, coff, *, name):
    dh = SB_DH
    nhp = (dcat.shape[1] - coff) // LANES
    TB = _tile(T, ATT_BLOCK)
    nb = T // TB
    cbase = coff // LANES

    def body(p_ref, d_ref, c_ref, dp_in_ref, dp_ref, q_ref, k_ref, v_ref, do_ref, dk_ref, dv_ref):
        _stage_heads(p_ref, 0, q_ref, T, ATT_SCALE)
        _stage_heads(p_ref, LANES, k_ref, T)
        _stage_heads(p_ref, 2 * LANES, v_ref, T)
        _stage_heads(d_ref, 0, do_ref, T)
        r, c = _att_tiles(TB)
        strict = c < r
        later = (r > c).astype(BF16)
        earlier = (r < c).astype(BF16)
        lane = lax.broadcasted_iota(jnp.int32, (TB, LANES), 1)
        dk_ref[...] = jnp.zeros_like(dk_ref)
        dv_ref[...] = jnp.zeros_like(dv_ref)

        def tile(hh, qt, dot, cm, j, before, mask):
            k0 = pl.multiple_of(j * TB, TB)
            kb = k_ref[hh, pl.ds(k0, TB), :]
            carry = jnp.sum(jnp.where(lane == j, cm, 0.0), axis=1, keepdims=True)
            ls, _, _, a = _att_weights(qt, kb, mask, later, carry)
            dv_ref[hh, pl.ds(k0, TB), :] += lax.dot_general(a.astype(BF16), dot, (((0,), (0,)), ((), ())),
                                                            preferred_element_type=F32)
            da = lax.dot_general(dot, v_ref[hh, pl.ds(k0, TB), :], (((1,), (1,)), ((), ())),
                                 preferred_element_type=F32)
            gl = a * da
            prefix = _split_dot(gl, earlier) + before
            dz = gl - jnp.exp(ls) * (gl + prefix)
            if mask is not None:
                dz = jnp.where(mask, dz, 0.0)
            dzb = dz.astype(BF16)
            dk_ref[hh, pl.ds(k0, TB), :] += lax.dot_general(dzb, qt, (((0,), (0,)), ((), ())),
                                                            preferred_element_type=F32)
            return jnp.dot(dzb, kb, preferred_element_type=F32), prefix[:, TB - 1:TB] + gl[:, TB - 1:TB]

        def qblock(i, _):
            q0 = pl.multiple_of(i * TB, TB)
            qts = [q_ref[hh, pl.ds(q0, TB), :] for hh in range(HP)]
            dots = [do_ref[hh, pl.ds(q0, TB), :] for hh in range(HP)]
            cms = [c_ref[hh, pl.ds(q0, TB), :] for hh in range(HP)]

            def kblock(j, st):
                out = []
                for hh in range(HP):
                    dq, before = st[2 * hh:2 * hh + 2]
                    part, new_before = tile(hh, qts[hh], dots[hh], cms[hh], j, before, None)
                    out += [dq + part, new_before]
                return tuple(out)

            st = lax.fori_loop(0, i, kblock, (jnp.zeros((TB, dh), F32), jnp.zeros((TB, 1), F32)) * HP)
            for hh in range(HP):
                part, _ = tile(hh, qts[hh], dots[hh], cms[hh], i, st[2 * hh + 1], strict)
                dq = (st[2 * hh] + part) * ATT_SCALE
                dp_ref[pl.ds(q0, TB), hh * dh:(hh + 1) * dh] = dq.astype(dp_ref.dtype)
            return 0

        lax.fori_loop(0, nb, qblock, 0)

        def write_back(n, _):
            r0 = pl.multiple_of(n * TB, TB)
            for hh in range(HP):
                dp_ref[pl.ds(r0, TB), LANES + hh * dh:LANES + (hh + 1) * dh] = (
                    dk_ref[hh, pl.ds(r0, TB), :].astype(dp_ref.dtype))
                dp_ref[pl.ds(r0, TB), 2 * LANES + hh * dh:2 * LANES + (hh + 1) * dh] = (
                    dv_ref[hh, pl.ds(r0, TB), :].astype(dp_ref.dtype))
            return 0

        lax.fori_loop(0, nb, write_back, 0)

    pblk = pl.BlockSpec((T, 3 * LANES), lambda b, hp: (b, hp))
    staged = pltpu.VMEM((HP, T, dh), BF16)
    accum = pltpu.VMEM((HP, T, dh), F32)
    return pl.pallas_call(
        body, name=name, grid=(B, nhp),
        in_specs=[pblk, pl.BlockSpec((T, LANES), lambda b, hp: (b, cbase + hp)),
                  pl.BlockSpec((HP, T, LANES), lambda b, hp: (b * nhp + hp, 0, 0)), _ANY],
        out_specs=pblk, out_shape=jax.ShapeDtypeStruct(dp.shape, dp.dtype), input_output_aliases={3: 0},
        scratch_shapes=[staged, staged, staged, staged, accum, accum],
        compiler_params=_params(("parallel", "parallel")),
    )(p, dcat, carries, dp)


def _adamw(w, gparts, m, v, *, name):
    L, R, C = w.shape
    P = gparts[0].shape[0]
    assert len(gparts) == L
    tr = _tile(R, max(SUBLANES, (1 << 19) // (C * P)), SUBLANES)

    def body(*refs):
        w_ref, g_refs, (m_ref, v_ref) = refs[0], refs[1:1 + L], refs[1 + L:3 + L]
        go_ref, d_ref, mo_ref, vo_ref = refs[3 + L:]
        layer = pl.program_id(0)

        def update(g_ref):
            g = g_ref[0].astype(F32)
            for i in range(1, P):
                g = g + g_ref[i].astype(F32)
            m2 = ADAM_B1 * m_ref[...] + (1.0 - ADAM_B1) * g
            v2 = ADAM_B2 * v_ref[...] + (1.0 - ADAM_B2) * (g * g)
            m_hat = m2 / ADAM_C1
            v_hat = v2 / ADAM_C2
            go_ref[...] = g
            d_ref[...] = -ADAM_LR * (m_hat / (jnp.sqrt(v_hat) + ADAM_EPS) + ADAM_WD * w_ref[...])
            mo_ref[...] = m2
            vo_ref[...] = v2

        for l in range(L):
            pl.when(layer == l)(functools.partial(update, g_refs[l]))

    row = pl.BlockSpec((None, tr, C), lambda l, i: (l, i, 0))

    def part(mine):
        return pl.BlockSpec((P, tr, C), lambda l, i: (0, jnp.where(l == mine, i, 0), 0))

    shp = jax.ShapeDtypeStruct((L, R, C), F32)
    return pl.pallas_call(
        body, name=name, grid=(L, R // tr),
        in_specs=[row] + [part(l) for l in range(L)] + [row, row],
        out_specs=[row] * 4, out_shape=[shp] * 4, compiler_params=_params(("arbitrary", "arbitrary")),
    )(w, *gparts, m, v)


def _my_index():
    return 4 * lax.axis_index("x") + 2 * lax.axis_index("y") + lax.axis_index("c")


def _exchange(arrs, gather, *, name):
    n = len(arrs)

    def body(*refs):
        ins, outs = refs[:n], refs[n:2 * n]
        send_sems, recv_sems, local_sems = refs[2 * n:]
        x, y, c = lax.axis_index("x"), lax.axis_index("y"), lax.axis_index("c")
        me = 4 * x + 2 * y + c
        remote, local = [], []
        for a in range(n):
            own = ins[a] if gather[a] else ins[a].at[me]
            cp = pltpu.make_async_copy(own, outs[a].at[me], local_sems.at[a])
            cp.start()
            local.append(cp)
            for k in range(1, N_DEV):
                px = 1 - x if k & 4 else x
                py = 1 - y if k & 2 else y
                pc = 1 - c if k & 1 else c
                src = ins[a] if gather[a] else ins[a].at[4 * px + 2 * py + pc]
                cp = pltpu.make_async_remote_copy(
                    src_ref=src, dst_ref=outs[a].at[me],
                    send_sem=send_sems.at[a, k - 1], recv_sem=recv_sems.at[a, k - 1],
                    device_id=(px, py, pc), device_id_type=pl.DeviceIdType.MESH)
                cp.start()
                remote.append(cp)
        for cp in remote:
            cp.wait()
        for cp in local:
            cp.wait()

    hbm = pl.BlockSpec(memory_space=pltpu.HBM)
    out_shape = [jax.ShapeDtypeStruct(((N_DEV,) + a.shape) if g else a.shape, a.dtype)
                 for a, g in zip(arrs, gather)]
    return pl.pallas_call(
        body, name=name, in_specs=[hbm] * n, out_specs=[hbm] * n, out_shape=out_shape,
        scratch_shapes=[pltpu.SemaphoreType.DMA((n, N_DEV - 1)), pltpu.SemaphoreType.DMA((n, N_DEV - 1)),
                        pltpu.SemaphoreType.DMA((n,))],
    )(*arrs)


_HBM = pl.BlockSpec(memory_space=pltpu.HBM)


def _other_chips(x, y):
    return [(1 - x, y), (x, 1 - y), (1 - x, 1 - y)]


def _gather_two_level(arrs, *, name):
    n = len(arrs)

    def body(*refs):
        ins, outs = refs[:n], refs[n:2 * n]
        send_sems, recv_sems, local_sems = refs[2 * n:]
        x, y, c = lax.axis_index("x"), lax.axis_index("y"), lax.axis_index("c")
        me, sibling = (x, y, c), (x, y, 1 - c)
        chips = _other_chips(x, y)

        def slot(a, px, py, pc):
            return outs[a].at[4 * px + 2 * py + pc]

        def copy(a, k, block, to, src=None):
            return pltpu.make_async_remote_copy(
                src_ref=slot(a, *block) if src is None else src, dst_ref=slot(a, *block),
                send_sem=send_sems.at[a, k], recv_sem=recv_sems.at[a, k],
                device_id=to, device_id_type=pl.DeviceIdType.MESH)

        local, sends = [], []
        for a in range(n):
            cp = pltpu.make_async_copy(ins[a], slot(a, *me), local_sems.at[a])
            cp.start()
            local.append(cp)
            first = [copy(a, 0, me, sibling, src=ins[a])]
            first += [copy(a, 1 + j, me, (*chip, c), src=ins[a]) for j, chip in enumerate(chips)]
            for cp in first:
                cp.start()
            sends += first
        for j, chip in enumerate(chips):
            for a in range(n):
                copy(a, 1 + j, (*chip, c), me).wait_recv()
                cp = copy(a, 4 + j, (*chip, c), sibling)
                cp.start()
                sends.append(cp)
        for a in range(n):
            copy(a, 0, sibling, me).wait_recv()
            for j, chip in enumerate(chips):
                copy(a, 4 + j, (*chip, 1 - c), me).wait_recv()
        for cp in sends:
            cp.wait_send()
        for cp in local:
            cp.wait()

    return pl.pallas_call(
        body, name=name, in_specs=[_HBM] * n, out_specs=[_HBM] * n,
        out_shape=[jax.ShapeDtypeStruct((N_DEV,) + a.shape, a.dtype) for a in arrs],
        scratch_shapes=[pltpu.SemaphoreType.DMA((n, N_DEV - 1)), pltpu.SemaphoreType.DMA((n, N_DEV - 1)),
                        pltpu.SemaphoreType.DMA((n,))],
    )(*arrs)


_SEM = pl.BlockSpec(memory_space=pltpu.SEMAPHORE)
_SPLIT_COPY = pltpu.SideEffectType.DATAFLOW_SIDE_EFFECTING


def _peers(x, y, c):
    return [((1 - x if k & 4 else x), (1 - y if k & 2 else y), (1 - c if k & 1 else c)) for k in range(1, N_DEV)]


_SPLIT_SEMS = 2 * (N_DEV - 1) + 1


def _split_sems(sems, a):
    mine = sems[a * _SPLIT_SEMS:(a + 1) * _SPLIT_SEMS]
    return mine[:N_DEV - 1], mine[N_DEV - 1:2 * (N_DEV - 1)], mine[-1]


def _split_src(ref, scatter, index):
    return ref.at[index] if scatter else ref


def _gather_start(arrs, carry, *, name, scatter=False):
    n = len(arrs)
    ns = n * _SPLIT_SEMS

    def body(*refs):
        ins, lands = refs[:n], refs[n:2 * n]
        sems = refs[2 * n + 1:2 * n + 1 + ns]
        x, y, c = lax.axis_index("x"), lax.axis_index("y"), lax.axis_index("c")
        me = 4 * x + 2 * y + c
        for a in range(n):
            send, recv, local = _split_sems(sems, a)
            pltpu.make_async_copy(_split_src(ins[a], scatter, me), lands[a].at[me], local).start()
            for k, (px, py, pc) in enumerate(_peers(x, y, c)):
                pltpu.make_async_remote_copy(
                    src_ref=_split_src(ins[a], scatter, 4 * px + 2 * py + pc), dst_ref=lands[a].at[me],
                    send_sem=send[k], recv_sem=recv[k],
                    device_id=(px, py, pc), device_id_type=pl.DeviceIdType.MESH).start()

    lands = [lax.empty(a.shape if scatter else (N_DEV,) + a.shape, a.dtype) for a in arrs]
    operands = [pltpu.with_memory_space_constraint(a, pltpu.HBM) for a in list(arrs) + lands + [carry]]
    outs = pl.pallas_call(
        body, name=name, in_specs=[_HBM] * (2 * n + 1), out_specs=[_SEM] * ns + [_HBM] * (2 * n + 1),
        out_shape=[pltpu.SemaphoreType.DMA(())] * ns + [pltpu.HBM(a.shape, a.dtype) for a in operands],
        input_output_aliases={i: ns + i for i in range(2 * n + 1)},
        compiler_params=pltpu.CompilerParams(has_side_effects=_SPLIT_COPY),
    )(*operands)
    return tuple(outs[:-1]), outs[-1]


def _gather_wait(handles, after, *, name, scatter=False):
    n = len(handles) // (_SPLIT_SEMS + 2)
    ns = n * _SPLIT_SEMS
    sems, thru = handles[:ns], handles[ns:]

    def body(*refs):
        ins, lands = refs[:n], refs[n:2 * n]
        sems = refs[2 * n:2 * n + ns]
        x, y, c = lax.axis_index("x"), lax.axis_index("y"), lax.axis_index("c")
        me = 4 * x + 2 * y + c
        for a in range(n):
            send, recv, local = _split_sems(sems, a)
            src = _split_src(ins[a], scatter, me)
            pltpu.make_async_copy(src, lands[a].at[me], local).wait()
            for k, peer in enumerate(_peers(x, y, c)):
                cp = pltpu.make_async_remote_copy(
                    src_ref=src, dst_ref=lands[a].at[me], send_sem=send[k], recv_sem=recv[k],
                    device_id=peer, device_id_type=pl.DeviceIdType.MESH)
                cp.wait_send()
                cp.wait_recv()

    outs = pl.pallas_call(
        body, name=name, in_specs=[_HBM] * (2 * n) + [_SEM] * ns + [_ANY], out_specs=[_HBM] * (2 * n),
        out_shape=[pltpu.HBM(a.shape, a.dtype) for a in thru],
        input_output_aliases={i: i for i in range(2 * n)},
        compiler_params=pltpu.CompilerParams(has_side_effects=_SPLIT_COPY),
    )(*thru, *sems, after)
    return outs[n:]


def _sibling_swap(arrs, *, name):
    n = len(arrs)
    nchip = N_DEV // 2

    def body(*refs):
        ins, outs = refs[:n], refs[n:2 * n]
        send_sems, recv_sems = refs[2 * n:]
        x, y, c = lax.axis_index("x"), lax.axis_index("y"), lax.axis_index("c")
        copies = []
        for a in range(n):
            for k in range(nchip):
                cp = pltpu.make_async_remote_copy(
                    src_ref=ins[a].at[2 * k + 1 - c], dst_ref=outs[a].at[k],
                    send_sem=send_sems.at[a, k], recv_sem=recv_sems.at[a, k],
                    device_id=(x, y, 1 - c), device_id_type=pl.DeviceIdType.MESH)
                cp.start()
                copies.append(cp)
        for cp in copies:
            cp.wait()

    return pl.pallas_call(
        body, name=name, in_specs=[_HBM] * n, out_specs=[_HBM] * n,
        out_shape=[jax.ShapeDtypeStruct((nchip,) + a.shape[1:], a.dtype) for a in arrs],
        scratch_shapes=[pltpu.SemaphoreType.DMA((n, nchip)), pltpu.SemaphoreType.DMA((n, nchip))],
    )(*arrs)


def _chip_exchange(arrs, *, name):
    n = len(arrs)

    def body(*refs):
        ins, outs = refs[:n], refs[n:2 * n]
        send_sems, recv_sems, local_sems = refs[2 * n:]
        x, y, c = lax.axis_index("x"), lax.axis_index("y"), lax.axis_index("c")
        mine = 2 * x + y
        copies = []
        for a in range(n):
            cp = pltpu.make_async_copy(ins[a].at[mine], outs[a].at[mine], local_sems.at[a])
            cp.start()
            copies.append(cp)
            for j, (px, py) in enumerate(_other_chips(x, y)):
                cp = pltpu.make_async_remote_copy(
                    src_ref=ins[a].at[2 * px + py], dst_ref=outs[a].at[mine],
                    send_sem=send_sems.at[a, j], recv_sem=recv_sems.at[a, j],
                    device_id=(px, py, c), device_id_type=pl.DeviceIdType.MESH)
                cp.start()
                copies.append(cp)
        for cp in copies:
            cp.wait()

    return pl.pallas_call(
        body, name=name, in_specs=[_HBM] * n, out_specs=[_HBM] * n,
        out_shape=[jax.ShapeDtypeStruct(a.shape, a.dtype) for a in arrs],
        scratch_shapes=[pltpu.SemaphoreType.DMA((n, 3)), pltpu.SemaphoreType.DMA((n, 3)),
                        pltpu.SemaphoreType.DMA((n,))],
    )(*arrs)


def _pair_sum(a, b, *, name):
    P, R, C = a.shape
    tr = _tile(R, max(SUBLANES, (1 << 19) // C), SUBLANES)

    def body(a_ref, b_ref, o_ref):
        o_ref[...] = (a_ref[...].astype(F32) + b_ref[...].astype(F32)).astype(o_ref.dtype)

    blk = pl.BlockSpec((1, tr, C), lambda p, i: (p, i, 0))
    return pl.pallas_call(
        body, name=name, grid=(P, R // tr), in_specs=[blk, blk], out_specs=blk,
        out_shape=jax.ShapeDtypeStruct(a.shape, a.dtype), compiler_params=_params(("parallel", "parallel")),
    )(a, b)


def _group_in_cols(w, lead):
    X = (w.shape[-1] - lead) // 3
    g = w[..., lead:].reshape(w.shape[:-1] + (3, X // LANES, LANES))
    g = jnp.swapaxes(g, -3, -2).reshape(w.shape[:-1] + (3 * X,))
    return jnp.concatenate([g, w[..., :lead]], axis=-1)


def _ungroup_in_cols(w, lead):
    X = (w.shape[-1] - lead) // 3
    g = w[..., :3 * X].reshape(w.shape[:-1] + (X // LANES, 3, LANES))
    g = jnp.swapaxes(g, -3, -2).reshape(w.shape[:-1] + (3 * X,))
    return jnp.concatenate([w[..., 3 * X:], g], axis=-1)


def _mlp_fwd(h, g, w1, w2, tag):
    hn = _rmsnorm(h, g, name=f"mlp_norm_{tag}")
    act = _matmul(hn, w1, name=f"mlp_up_{tag}", epilogue=_ep_relu2, out_dtypes=(BF16,))
    out = _matmul(act, w2, name=f"mlp_down_{tag}", extras=(h,), epilogue=_ep_add)
    return out, (hn, act)


def _mlp_bwd(dout, dout_b, h, g, w1, w2, saved, tag):
    hn, act = saved
    dw2 = _matmul(act, dout_b, ta=True, name=f"mlp_dw2_{tag}", out_dtypes=(GRAD_WIRE,))
    dz = _matmul(dout_b, w2, tb=True, name=f"mlp_dact_{tag}", extras=(act,), epilogue=_ep_relu2_bwd,
                 out_dtypes=(BF16,))
    dw1 = _matmul(hn, dz, ta=True, name=f"mlp_dw1_{tag}", out_dtypes=(GRAD_WIRE,))
    dhn = _matmul(dz, w1, tb=True, name=f"mlp_dhn_{tag}")
    dh, dh_b, dg = _rmsnorm_bwd(dhn, h, g, dout, name=f"mlp_norm_bwd_{tag}")
    return dh, dh_b, dg, dw1, dw2


class _Lazy:
    def __init__(self, handles, finish, name):
        self.handles, self.finish, self.name, self.done = handles, finish, name, None

    def take(self, after):
        if self.done is None:
            self.done = self.finish(_gather_wait(self.handles, after, name=self.name))
        return self.done


def _weight(W, n, after):
    if isinstance(W[n], _Lazy):
        W.update(W[n].take(after))
    return W[n]


def _local_step(x, target, W, emit=None):
    B, T, D = x.shape
    N = B * T
    G = {}
    row = lambda vec: vec.reshape(1, -1)

    def sent(nxt, grads):
        return nxt if emit is None else emit(grads, nxt)

    PW = W["pool_w"].shape[0] * W["pool_w"].shape[1]
    CW = W["conv_b"].shape[-1]
    SW = W["sgu_norm_g"].shape[-1]
    HW = W["att_width"]

    h0 = x.reshape(N, D)
    xn0 = _rmsnorm(h0, row(W["mix_norm_g"][0]), name="mix_norm_0")
    p0 = _matmul(xn0, W["ab_w_in"], name="ab_in")
    cat0 = _pool_fwd(p0, W["pool_w"], W["pool_scale"], B, T, 3 * CW, PW + CW, name="pool_fwd")
    cat0 = _conv_fwd(p0, cat0, W["conv_w"], row(W["conv_b"]), B, T, PW, name="conv_fwd")
    h1 = _matmul(cat0, W["ab_w_out"], name="ab_out", extras=(h0,), epilogue=_ep_add)
    h2, mlp0 = _mlp_fwd(h1, row(W["mlp_norm_g"][0]), _weight(W, "mlp_w1_0", h1), W["mlp_w2_0"], 0)
    xn1 = _rmsnorm(h2, row(W["mix_norm_g"][1]), name="mix_norm_1")
    p1 = _matmul(xn1, _weight(W, "cd_w_in", xn1), name="cd_in")
    ln_g, ln_b = row(W["sgu_norm_g"]), row(W["sgu_norm_b"])
    cat1 = _sgu_fwd(p1, ln_g, ln_b, W["sgu_w"], W["sgu_b"], 3 * HW, SW + HW, name="sgu_fwd")
    cat1, att_carries = _attn_fwd(p1, cat1, B, T, SW, name="attn_fwd")
    h3 = _matmul(cat1, W["cd_w_out"], name="cd_out", extras=(h2,), epilogue=_ep_add)
    h4, mlp1 = _mlp_fwd(h3, row(W["mlp_norm_g"][1]), _weight(W, "mlp_w1_1", h3), W["mlp_w2_1"], 1)

    loss, dh4, dh4_b, G["final_norm_g"] = _final_loss(h4, row(W["final_norm_g"]), target.reshape(N, D),
                                                      name="final_loss")

    dh3, dh3_b, dmlp_g1, dw1_1, dw2_1 = _mlp_bwd(dh4, dh4_b, h3, row(W["mlp_norm_g"][1]), W["mlp_w1_1"],
                                                 W["mlp_w2_1"], mlp1, 1)
    dh3_b = sent(dh3_b, {"mlp_w1_1": dw1_1, "mlp_w2_1": dw2_1})
    G["cd_w_out"] = _matmul(cat1, dh3_b, ta=True, name="cd_out_dw", out_dtypes=(GRAD_WIRE,))[None]
    dcat1 = _matmul(dh3_b, W["cd_w_out"], tb=True, name="cd_out_dx")
    dp1, G["sgu_w"], dsgu_b, G["sgu_norm_g"], G["sgu_norm_b"] = _sgu_bwd(
        p1, dcat1, ln_g, ln_b, W["sgu_w"], W["sgu_b"], 3 * HW, name="sgu_bwd")
    G["sgu_b"] = dsgu_b.reshape(W["sgu_b"].shape)
    dp1 = _attn_bwd(p1, dcat1, att_carries, dp1, B, T, SW, name="attn_bwd")
    G["cd_w_in"] = _matmul(xn1, dp1, ta=True, name="cd_in_dw", out_dtypes=(GRAD_WIRE,))[None]
    dp1 = sent(dp1, {"cd_w_out": G["cd_w_out"][0], "cd_w_in": G["cd_w_in"][0]})
    dxn1 =_matmul(dp1, W["cd_w_in"], tb=True, name="cd_in_dx")
    dh2, dh2_b, dmix_g1 = _rmsnorm_bwd(dxn1, h2, row(W["mix_norm_g"][1]), dh3, name="mix_norm_bwd_1")

    dh1, dh1_b, dmlp_g0, dw1_0, dw2_0 = _mlp_bwd(dh2, dh2_b, h1, row(W["mlp_norm_g"][0]), W["mlp_w1_0"],
                                                 W["mlp_w2_0"], mlp0, 0)
    dh1_b = sent(dh1_b, {"mlp_w1_0": dw1_0, "mlp_w2_0": dw2_0})
    G["ab_w_out"] =_matmul(cat0, dh1_b, ta=True, name="ab_out_dw", out_dtypes=(GRAD_WIRE,))[None]
    dcat0 = _matmul(dh1_b, W["ab_w_out"], tb=True, name="ab_out_dx")
    dp0, G["pool_w"], dps = _pool_bwd(p0, dcat0, W["pool_w"], W["pool_scale"], B, T, 3 * CW, name="pool_bwd")
    G["pool_scale"] = dps.reshape(W["pool_scale"].shape)
    dp0, G["conv_w"], dcb = _conv_bwd(p0, dcat0, dp0, W["conv_w"], row(W["conv_b"]), B, T, PW, name="conv_bwd")
    G["conv_b"] = dcb.reshape(-1)
    G["ab_w_in"] = _matmul(xn0, dp0, ta=True, name="ab_in_dw", out_dtypes=(GRAD_WIRE,))[None]
    dp0 = sent(dp0, {"ab_w_out": G["ab_w_out"][0], "ab_w_in": G["ab_w_in"][0]})
    dxn0 =_matmul(dp0, W["ab_w_in"], tb=True, name="ab_in_dx")
    dx, _, dmix_g0 = _rmsnorm_bwd(dxn0, h0, row(W["mix_norm_g"][0]), dh1, name="mix_norm_bwd_0")

    G["mix_norm_g"] = jnp.concatenate([dmix_g0, dmix_g1], axis=0)
    G["mlp_norm_g"] = jnp.concatenate([dmlp_g0, dmlp_g1], axis=0)
    G["mlp_w1"] = jnp.stack([dw1_0, dw1_1])
    G["mlp_w2"] = jnp.stack([dw2_0, dw2_1])
    G["final_norm_g"] = G["final_norm_g"].reshape(-1)
    G["sgu_norm_g"] = G["sgu_norm_g"].reshape(-1)
    G["sgu_norm_b"] = G["sgu_norm_b"].reshape(-1)
    return loss[0, 0], dx.reshape(B, T, D), G


_NAMES = ["mix_norm_g", "mlp_norm_g", "ab_w_in", "pool_w", "pool_scale", "conv_w", "conv_b", "ab_w_out",
          "cd_w_in", "sgu_norm_g", "sgu_norm_b", "sgu_w", "sgu_b", "cd_w_out", "mlp_w1", "mlp_w2",
          "final_norm_g"]
_COL_SHARDED = ["ab_w_in", "cd_w_in", "mlp_w1"]
_ROW_SHARDED = ["ab_w_out", "cd_w_out", "mlp_w2"]
_SMALL_SHARDED = ["conv_w", "sgu_norm_g", "sgu_norm_b"]
_REPLICATED = ["mix_norm_g", "mlp_norm_g", "pool_w", "pool_scale", "conv_b", "sgu_w", "sgu_b", "final_norm_g"]


def _pad_rows(a2d, mult=SUBLANES):
    pad = (-a2d.shape[0]) % mult
    return jnp.pad(a2d, ((0, pad), (0, 0))) if pad else a2d


def _pack(arrays):
    return _pad_rows(jnp.concatenate([a.reshape(-1, LANES) for a in arrays], axis=0))


def _unpack(packed, shapes):
    out, r = [], 0
    for s in shapes:
        n = math.prod(s) // LANES
        out.append(packed[r:r + n].reshape(s))
        r += n
    return out


def _small_shard_pack(arrays):
    rows = [jnp.pad(a.reshape(-1, a.shape[-1]), ((0, 0), (0, LANES - a.shape[-1]))) for a in arrays]
    return _pad_rows(jnp.concatenate(rows, axis=0))


def _cols_to_chunks(a):
    n = a.shape[-1] // N_DEV
    return jnp.moveaxis(a.reshape(a.shape[:-1] + (N_DEV, n)), -2, 0)


def _chunks_to_cols(a):
    t = jnp.moveaxis(a, 0, -2)
    return t.reshape(t.shape[:-2] + (t.shape[-2] * t.shape[-1],))


def _rows_to_chunks(a):
    r = a.shape[-2] // N_DEV
    return jnp.moveaxis(a.reshape(a.shape[:-2] + (N_DEV, r, a.shape[-1])), -3, 0)


def _chunks_to_rows(a):
    t = jnp.moveaxis(a, 0, -3)
    return t.reshape(t.shape[:-3] + (t.shape[-3] * t.shape[-2], t.shape[-1]))


def kernel(x, mix_norm_g, mlp_norm_g, ab_w_in, pool_w, pool_scale, conv_w, conv_b, ab_w_out, cd_w_in, sgu_norm_g, sgu_norm_b, sgu_w, sgu_b, cd_w_out, mlp_w1, mlp_w2, final_norm_g, loss_target, m_mix_norm_g, m_mlp_norm_g, m_ab_w_in, m_pool_w, m_pool_scale, m_conv_w, m_conv_b, m_ab_w_out, m_cd_w_in, m_sgu_norm_g, m_sgu_norm_b, m_sgu_w, m_sgu_b, m_cd_w_out, m_mlp_w1, m_mlp_w2, m_final_norm_g, v_mix_norm_g, v_mlp_norm_g, v_ab_w_in, v_pool_w, v_pool_scale, v_conv_w, v_conv_b, v_ab_w_out, v_cd_w_in, v_sgu_norm_g, v_sgu_norm_b, v_sgu_w, v_sgu_b, v_cd_w_out, v_mlp_w1, v_mlp_w2, v_final_norm_g):
    w = dict(zip(_NAMES, (mix_norm_g, mlp_norm_g, ab_w_in, pool_w, pool_scale, conv_w, conv_b, ab_w_out, cd_w_in,
                          sgu_norm_g, sgu_norm_b, sgu_w, sgu_b, cd_w_out, mlp_w1, mlp_w2, final_norm_g)))
    m = dict(zip(_NAMES, (m_mix_norm_g, m_mlp_norm_g, m_ab_w_in, m_pool_w, m_pool_scale, m_conv_w, m_conv_b,
                          m_ab_w_out, m_cd_w_in, m_sgu_norm_g, m_sgu_norm_b, m_sgu_w, m_sgu_b, m_cd_w_out,
                          m_mlp_w1, m_mlp_w2, m_final_norm_g)))
    v = dict(zip(_NAMES, (v_mix_norm_g, v_mlp_norm_g, v_ab_w_in, v_pool_w, v_pool_scale, v_conv_w, v_conv_b,
                          v_ab_w_out, v_cd_w_in, v_sgu_norm_g, v_sgu_norm_b, v_sgu_w, v_sgu_b, v_cd_w_out,
                          v_mlp_w1, v_mlp_w2, v_final_norm_g)))
    big = _COL_SHARDED + _ROW_SHARDED
    me = _my_index()

    small_sh = _small_shard_pack([w[n] for n in _SMALL_SHARDED])
    in_lead = {"ab_w_in": pool_w.shape[1] * pool_w.shape[2], "cd_w_in": 2 * sgu_norm_g.shape[-1] * N_DEV}
    shard = {"ab_w_in": ab_w_in[0], "ab_w_out": ab_w_out[0], "cd_w_in": cd_w_in[0], "cd_w_out": cd_w_out[0]}
    for layer in range(mlp_w1.shape[0]):
        shard[f"mlp_w1_{layer}"], shard[f"mlp_w2_{layer}"] = mlp_w1[layer], mlp_w2[layer]
    shard = {n: a.astype(BF16) for n, a in shard.items()}

    def whole(n, g):
        if n.endswith("_in") or n.startswith("mlp_w1"):
            full = _chunks_to_cols(g)
            return _group_in_cols(full, in_lead[n]) if n in in_lead else full
        return _chunks_to_rows(g)

    now = ["ab_w_in", "ab_w_out", "mlp_w1_0", "mlp_w2_0"]
    gathered = _gather_two_level([shard[n] for n in now] + [small_sh], name="gather_weights")
    W = {n: whole(n, g) for n, g in zip(now, gathered)}
    W["att_width"] = cd_w_out.shape[1] * N_DEV - sgu_norm_g.shape[-1] * N_DEV
    for group, tag in ((["cd_w_in", "cd_w_out"], "cd"), (["mlp_w1_1", "mlp_w2_1"], "mlp1")):
        handles, W["ab_w_in"] = _gather_start([shard[n] for n in group], W["ab_w_in"], name=f"gather_{tag}_start")
        lazy = _Lazy(handles, lambda got, group=group: {n: whole(n, g) for n, g in zip(group, got)},
                     f"gather_{tag}_wait")
        for n in group:
            W[n] = lazy
    small_full = gathered[-1]
    r = 0
    for n in _SMALL_SHARDED:
        rows, width = math.prod(w[n].shape[:-1]), w[n].shape[-1]
        W[n] = _chunks_to_cols(small_full[:, r:r + rows, :width])
        r += rows
    for n in _REPLICATED:
        W[n] = w[n]
    for n in ("pool_w", "pool_scale", "sgu_w", "sgu_b"):
        W[n] = W[n][0]

    pending = []

    def emit(grads, carry):
        names = list(grads)
        parts = []
        for n in names:
            g = _ungroup_in_cols(grads[n], in_lead[n]) if n in in_lead else grads[n]
            parts.append(_cols_to_chunks(g) if n.endswith("_in") or n.startswith("mlp_w1") else _rows_to_chunks(g))
        handles, carry = _gather_start(parts, carry, name=f"grads_{names[0]}_start", scatter=True)
        pending.append((names, handles))
        return carry

    loss_part, grad_x, G = _local_step(x, loss_target, W, emit)

    landed = {}
    for names, handles in pending:
        got = _gather_wait(handles, grad_x, name=f"grads_{names[0]}_wait", scatter=True)
        landed.update(zip(names, got))
    small_names = _REPLICATED + _SMALL_SHARDED
    small_grads = [G[n].reshape(-1) for n in small_names]
    loss_row = jnp.full((LANES,), loss_part, F32)
    small_pack = _pack(small_grads + [loss_row])
    small_parts = _exchange([small_pack], [True], name="gather_small_grads")[0]

    grads, deltas, new_m, new_v = {}, {}, {}, {}
    for n in big:
        layers = [landed[f"{n}_{l}"] for l in range(w[n].shape[0])] if n.startswith("mlp") else [landed[n]]
        grads[n], deltas[n], new_m[n], new_v[n] = _adamw(w[n], layers, m[n], v[n], name=f"adamw_{n}")

    rep_shapes = [w[n].shape for n in _REPLICATED]
    rep_rows = sum(math.prod(s) for s in rep_shapes) // LANES
    small_sum_shapes = [(G[n].size,) for n in small_names] + [(LANES,)]
    zero_tail = [jnp.zeros((math.prod(s),), F32) for s in small_sum_shapes[len(_REPLICATED):]]
    w_pack = _pack([w[n] for n in _REPLICATED] + zero_tail)
    m_pack = _pack([m[n] for n in _REPLICATED] + zero_tail)
    v_pack = _pack([v[n] for n in _REPLICATED] + zero_tail)
    outs = [o[0] for o in _adamw(w_pack[None], [small_parts], m_pack[None], v_pack[None], name="adamw_small")]
    summed = _unpack(outs[0], small_sum_shapes)
    for i, n in enumerate(_REPLICATED):
        grads[n] = summed[i].reshape(w[n].shape)
    for dst, o in zip((deltas, new_m, new_v), outs[1:]):
        for n, val in zip(_REPLICATED, _unpack(o[:rep_rows], rep_shapes)):
            dst[n] = val
    loss = summed[-1][0]

    shard_g = []
    for i, n in enumerate(_SMALL_SHARDED):
        full = summed[len(_REPLICATED) + i].reshape(w[n].shape[:-1] + (-1,))
        width = w[n].shape[-1]
        shard_g.append(lax.dynamic_slice_in_dim(full, me * width, width, axis=full.ndim - 1))
    g_sh = _small_shard_pack(shard_g)
    m_sh = _small_shard_pack([m[n] for n in _SMALL_SHARDED])
    v_sh = _small_shard_pack([v[n] for n in _SMALL_SHARDED])
    outs = [o[0] for o in _adamw(small_sh[None], [g_sh[None]], m_sh[None], v_sh[None], name="adamw_small_sharded")]
    r = 0
    for n in _SMALL_SHARDED:
        rows, width = math.prod(w[n].shape[:-1]), w[n].shape[-1]
        for dst, o in zip((grads, deltas, new_m, new_v), outs):
            dst[n] = o[r:r + rows, :width].reshape(w[n].shape)
        r += rows

    return (loss, grad_x, *[grads[n] for n in _NAMES], *[deltas[n] for n in _NAMES],
            *[new_m[n] for n in _NAMES], *[new_v[n] for n in _NAMES])
```

```python
import functools
import math

import jax
import jax.numpy as jnp
from jax import lax
from jax.experimental import pallas as pl
from jax.experimental.pallas import tpu as pltpu

F32 = jnp.float32
BF16 = jnp.bfloat16
GRAD_WIRE = jnp.bfloat16

NORM_EPS = 1e-6
ADAM_LR = 0.001
ADAM_B1 = 0.9
ADAM_B2 = 0.999
ADAM_EPS = 1e-08
ADAM_WD = 0.01
ADAM_STEP = 10
ADAM_C1 = 1.0 - ADAM_B1 ** ADAM_STEP
ADAM_C2 = 1.0 - ADAM_B2 ** ADAM_STEP

N_DEV = 8
LANES = 128
SUBLANES = 8
SB_DH = 64
ATT_BLOCK = 256
POOL_LOG_WINDOWS = 4
VMEM_LIMIT = 56 * 1024 * 1024


def _params(semantics=None):
    return pltpu.CompilerParams(dimension_semantics=semantics, vmem_limit_bytes=VMEM_LIMIT)


def _tile(dim, pref, unit=LANES):
    if dim <= pref:
        return dim
    t = (pref // unit) * unit
    while t >= unit:
        if dim % t == 0:
            return t
        t -= unit
    return dim


def _matmul(a, b, *, name, ta=False, tb=False, extras=(), epilogue=None, out_dtypes=(F32,),
            tm=1024, tn=1024, tk=2048):
    M, K = (a.shape[1], a.shape[0]) if ta else a.shape
    N = b.shape[0] if tb else b.shape[1]
    assert (b.shape[1] if tb else b.shape[0]) == K, (a.shape, b.shape)
    tm, tn, tk = _tile(M, tm), _tile(N, tn), _tile(K, tk)
    nk = K // tk
    dims = (((0 if ta else 1,), (1 if tb else 0,)), ((), ()))
    ne, no = len(extras), len(out_dtypes)

    def body(*refs):
        a_ref, b_ref = refs[0], refs[1]
        e_refs = refs[2:2 + ne]
        o_refs = refs[2 + ne:2 + ne + no]
        k = pl.program_id(2)

        def part():
            return lax.dot_general(a_ref[...].astype(BF16), b_ref[...].astype(BF16), dims,
                                   preferred_element_type=F32)

        def finish(acc):
            outs = epilogue(acc, *[e[...] for e in e_refs]) if epilogue is not None else (acc,)
            for o_ref, val in zip(o_refs, outs):
                o_ref[...] = val.astype(o_ref.dtype)

        if nk == 1:
            finish(part())
        else:
            acc_ref = refs[-1]

            @pl.when(k == 0)
            def _():
                acc_ref[...] = jnp.zeros_like(acc_ref)

            acc_ref[...] += part()

            @pl.when(k == nk - 1)
            def _():
                finish(acc_ref[...])

    a_spec = (pl.BlockSpec((tk, tm), lambda i, j, k: (k, i)) if ta
              else pl.BlockSpec((tm, tk), lambda i, j, k: (i, k)))
    b_spec = (pl.BlockSpec((tn, tk), lambda i, j, k: (j, k)) if tb
              else pl.BlockSpec((tk, tn), lambda i, j, k: (k, j)))
    o_spec = pl.BlockSpec((tm, tn), lambda i, j, k: (i, j))
    outs = pl.pallas_call(
        body,
        name=name,
        grid=(M // tm, N // tn, nk),
        in_specs=[a_spec, b_spec] + [o_spec] * ne,
        out_specs=[o_spec] * no,
        out_shape=[jax.ShapeDtypeStruct((M, N), dt) for dt in out_dtypes],
        scratch_shapes=[pltpu.VMEM((tm, tn), F32)] if nk > 1 else [],
        compiler_params=_params(("parallel", "parallel", "arbitrary")),
    )(a, b, *extras)
    return outs[0] if no == 1 else outs


def _ep_add(acc, res):
    return (acc + res,)


def _ep_relu2(acc):
    r = jnp.maximum(acc, 0.0)
    return (r * r,)


def _ep_relu2_bwd(acc, act):
    return (acc * (2.0 * jnp.sqrt(act.astype(F32))),)


def _rstd(x):
    return lax.rsqrt(jnp.mean(x * x, axis=-1, keepdims=True) + NORM_EPS)


def _rmsnorm(h, g, *, name, tr=512):
    N, D = h.shape
    tr = _tile(N, tr, SUBLANES)

    def body(h_ref, g_ref, o_ref):
        x = h_ref[...]
        o_ref[...] = ((x * _rstd(x)) * g_ref[...]).astype(o_ref.dtype)

    row = pl.BlockSpec((tr, D), lambda i: (i, 0))
    vec = pl.BlockSpec((1, D), lambda i: (0, 0))
    return pl.pallas_call(
        body, name=name, grid=(N // tr,), in_specs=[row, vec], out_specs=row,
        out_shape=jax.ShapeDtypeStruct((N, D), BF16), compiler_params=_params(("parallel",)),
    )(h, g)


def _rmsnorm_bwd(dy, h, g, dres, *, name, tr=512):
    N, D = h.shape
    tr = _tile(N, tr, SUBLANES)

    def body(dy_ref, h_ref, g_ref, r_ref, dh_ref, dhb_ref, dg_ref):
        i = pl.program_id(0)
        x = h_ref[...]
        d = dy_ref[...]
        r = _rstd(x)
        xh = x * r

        @pl.when(i == 0)
        def _():
            dg_ref[...] = jnp.zeros_like(dg_ref)

        dg_ref[...] += jnp.sum(d * xh, axis=0, keepdims=True)
        dxh = d * g_ref[...]
        dh = r_ref[...] + r * (dxh - xh * jnp.mean(dxh * xh, axis=-1, keepdims=True))
        dh_ref[...] = dh
        dhb_ref[...] = dh.astype(dhb_ref.dtype)

    row = pl.BlockSpec((tr, D), lambda i: (i, 0))
    vec = pl.BlockSpec((1, D), lambda i: (0, 0))
    return pl.pallas_call(
        body, name=name, grid=(N // tr,), in_specs=[row, row, vec, row], out_specs=[row, row, vec],
        out_shape=[jax.ShapeDtypeStruct((N, D), F32), jax.ShapeDtypeStruct((N, D), BF16),
                   jax.ShapeDtypeStruct((1, D), F32)],
        compiler_params=_params(("arbitrary",)),
    )(dy, h, g, dres)


def _final_loss(h, g, target, *, name, tr=512):
    N, D = h.shape
    tr = _tile(N, tr, SUBLANES)

    def body(h_ref, g_ref, t_ref, loss_ref, dh_ref, dhb_ref, dg_ref):
        i = pl.program_id(0)
        x = h_ref[...]
        gg = g_ref[...]
        r = _rstd(x)
        xh = x * r
        err = xh * gg - t_ref[...]

        @pl.when(i == 0)
        def _():
            dg_ref[...] = jnp.zeros_like(dg_ref)
            loss_ref[...] = jnp.zeros_like(loss_ref)

        per_row = jnp.mean(err * err, axis=-1, keepdims=True)
        loss_ref[...] += 0.5 * jnp.sum(per_row, axis=0, keepdims=True)
        dy = err * (1.0 / D)
        dg_ref[...] += jnp.sum(dy * xh, axis=0, keepdims=True)
        dxh = dy * gg
        dh = r * (dxh - xh * jnp.mean(dxh * xh, axis=-1, keepdims=True))
        dh_ref[...] = dh
        dhb_ref[...] = dh.astype(dhb_ref.dtype)

    row = pl.BlockSpec((tr, D), lambda i: (i, 0))
    vec = pl.BlockSpec((1, D), lambda i: (0, 0))
    lvec = pl.BlockSpec((1, LANES), lambda i: (0, 0))
    return pl.pallas_call(
        body, name=name, grid=(N // tr,), in_specs=[row, vec, row], out_specs=[lvec, row, row, vec],
        out_shape=[jax.ShapeDtypeStruct((1, LANES), F32), jax.ShapeDtypeStruct((N, D), F32),
                   jax.ShapeDtypeStruct((N, D), BF16), jax.ShapeDtypeStruct((1, D), F32)],
        compiler_params=_params(("arbitrary",)),
    )(h, g, target)


def _shift_down(x, s):
    t = lax.broadcasted_iota(jnp.int32, x.shape, 0)
    return jnp.where(t >= s, pltpu.roll(x, s, 0), 0.0)


def _shift_up(x, s):
    n = x.shape[0]
    t = lax.broadcasted_iota(jnp.int32, x.shape, 0)
    return jnp.where(t < n - s, pltpu.roll(x, n - s, 0), 0.0)


def _window_sum(x, g, shift):
    s = x + shift(x, 1)
    for k in range(1, POOL_LOG_WINDOWS):
        s = jnp.where(k <= g, s + shift(s, 2 ** k), s)
    return s


def _pool_count(shape, g):
    t = lax.broadcasted_iota(jnp.int32, shape, 0)
    return jnp.minimum(t + 1, lax.shift_left(jnp.int32(2), g)).astype(F32)


def _pool_fwd(p, pool_w, pool_scale, B, T, off, width, *, name):
    G, dh = pool_w.shape[0], pool_w.shape[1]
    assert G == POOL_LOG_WINDOWS and off % dh == 0
    base = off // dh

    def body(a_ref, w_ref, s_ref, o_ref):
        g = pl.program_id(0)
        a = a_ref[...]
        pooled = _window_sum(a, g, _shift_down) / _pool_count(a.shape, g) - a
        m = jnp.dot(pooled.astype(BF16), w_ref[0].astype(BF16), preferred_element_type=F32)
        o_ref[...] = (m * s_ref[0]).astype(o_ref.dtype)

    return pl.pallas_call(
        body, name=name, grid=(G, B),
        in_specs=[pl.BlockSpec((T, dh), lambda g, b: (b, base + g)),
                  pl.BlockSpec((1, dh, dh), lambda g, b: (g, 0, 0)),
                  pl.BlockSpec((1, 1, dh), lambda g, b: (g, 0, 0))],
        out_specs=pl.BlockSpec((T, dh), lambda g, b: (b, g)),
        out_shape=jax.ShapeDtypeStruct((B * T, width), BF16),
        compiler_params=_params(("parallel", "parallel")),
    )(p, pool_w, pool_scale.reshape(G, 1, dh))


def _pool_bwd(p, dcat, pool_w, pool_scale, B, T, off, *, name):
    G, dh = pool_w.shape[0], pool_w.shape[1]
    base = off // dh

    def body(a_ref, d_ref, w_ref, s_ref, da_ref, dw_ref, ds_ref):
        g = pl.program_id(0)
        b = pl.program_id(1)
        a = a_ref[...]
        d = d_ref[...]
        cnt = _pool_count(a.shape, g)
        pooled = (_window_sum(a, g, _shift_down) / cnt - a).astype(BF16)
        w = w_ref[0].astype(BF16)
        m = jnp.dot(pooled, w, preferred_element_type=F32)

        @pl.when(b == 0)
        def _():
            dw_ref[...] = jnp.zeros_like(dw_ref)
            ds_ref[...] = jnp.zeros_like(ds_ref)

        ds_ref[0] += jnp.sum(d * m, axis=0, keepdims=True)
        dm = (d * s_ref[0]).astype(BF16)
        dw_ref[0] += lax.dot_general(pooled, dm, (((0,), (0,)), ((), ())), preferred_element_type=F32)
        dpooled = lax.dot_general(dm, w, (((1,), (1,)), ((), ())), preferred_element_type=F32)
        da = _window_sum(dpooled / cnt, g, _shift_up) - dpooled
        da_ref[...] = da.astype(da_ref.dtype)

    pblk = pl.BlockSpec((T, dh), lambda g, b: (b, base + g))
    dblk = pl.BlockSpec((T, dh), lambda g, b: (b, g))
    wspec = pl.BlockSpec((1, dh, dh), lambda g, b: (g, 0, 0))
    sspec = pl.BlockSpec((1, 1, dh), lambda g, b: (g, 0, 0))
    return pl.pallas_call(
        body, name=name, grid=(G, B), in_specs=[pblk, dblk, wspec, sspec], out_specs=[pblk, wspec, sspec],
        out_shape=[jax.ShapeDtypeStruct((B * T, p.shape[1]), BF16), jax.ShapeDtypeStruct((G, dh, dh), F32),
                   jax.ShapeDtypeStruct((G, 1, dh), F32)],
        compiler_params=_params(("parallel", "arbitrary")),
    )(p, dcat, pool_w, pool_scale.reshape(G, 1, dh))


_ANY = pl.BlockSpec(memory_space=pl.ANY)


def _conv_fwd(p, cat, conv_w, conv_b, B, T, coff, *, name):
    CW = conv_w.shape[1]
    tc = LANES
    assert coff % tc == 0 and CW % tc == 0
    cbase = coff // tc

    def body(p_ref, cat_ref, w_ref, b_ref, o_ref):
        xb, gb, gc = p_ref[:, 0:tc], p_ref[:, tc:2 * tc], p_ref[:, 2 * tc:3 * tc]
        c = gc * xb
        w = w_ref[...]
        y = _shift_down(c, 2) * w[0:1] + _shift_down(c, 1) * w[1:2] + c * w[2:3] + b_ref[...]
        o_ref[...] = (gb * y).astype(o_ref.dtype)

    return pl.pallas_call(
        body, name=name, grid=(CW // tc, B),
        in_specs=[pl.BlockSpec((T, 3 * tc), lambda j, b: (b, j)), _ANY,
                  pl.BlockSpec((3, tc), lambda j, b: (0, j)), pl.BlockSpec((1, tc), lambda j, b: (0, j))],
        out_specs=pl.BlockSpec((T, tc), lambda j, b: (b, cbase + j)),
        out_shape=jax.ShapeDtypeStruct(cat.shape, cat.dtype), input_output_aliases={1: 0},
        compiler_params=_params(("parallel", "parallel")),
    )(p, cat, conv_w, conv_b)


def _conv_bwd(p, dcat, dp, conv_w, conv_b, B, T, coff, *, name):
    CW = conv_w.shape[1]
    tc = LANES
    assert coff % tc == 0
    cbase = coff // tc

    def body(p_ref, d_ref, dp_in_ref, w_ref, b_ref, dp_ref, dw_ref, db_ref):
        b = pl.program_id(1)
        xb, gb, gc = p_ref[:, 0:tc], p_ref[:, tc:2 * tc], p_ref[:, 2 * tc:3 * tc]
        d = d_ref[...]
        w = w_ref[...]
        c = gc * xb
        c1 = _shift_down(c, 1)
        c2 = _shift_down(c, 2)
        y = c2 * w[0:1] + c1 * w[1:2] + c * w[2:3] + b_ref[...]
        dy = d * gb
        dp_ref[:, tc:2 * tc] = (d * y).astype(dp_ref.dtype)

        @pl.when(b == 0)
        def _():
            dw_ref[...] = jnp.zeros_like(dw_ref)
            db_ref[...] = jnp.zeros_like(db_ref)

        db_ref[...] += jnp.sum(dy, axis=0, keepdims=True)
        dw_ref[0:1, :] += jnp.sum(dy * c2, axis=0, keepdims=True)
        dw_ref[1:2, :] += jnp.sum(dy * c1, axis=0, keepdims=True)
        dw_ref[2:3, :] += jnp.sum(dy * c, axis=0, keepdims=True)
        dc = dy * w[2:3] + _shift_up(dy, 1) * w[1:2] + _shift_up(dy, 2) * w[0:1]
        dp_ref[:, 2 * tc:3 * tc] = (dc * xb).astype(dp_ref.dtype)
        dp_ref[:, 0:tc] = (dc * gc).astype(dp_ref.dtype)

    wspec = pl.BlockSpec((3, tc), lambda j, b: (0, j))
    bspec = pl.BlockSpec((1, tc), lambda j, b: (0, j))
    pblk = pl.BlockSpec((T, 3 * tc), lambda j, b: (b, j))
    return pl.pallas_call(
        body, name=name, grid=(CW // tc, B),
        in_specs=[pblk, pl.BlockSpec((T, tc), lambda j, b: (b, cbase + j)), _ANY, wspec, bspec],
        out_specs=[pblk, wspec, bspec],
        out_shape=[jax.ShapeDtypeStruct(dp.shape, dp.dtype), jax.ShapeDtypeStruct((3, CW), F32),
                   jax.ShapeDtypeStruct((1, CW), F32)],
        input_output_aliases={2: 0},
        compiler_params=_params(("parallel", "arbitrary")),
    )(p, dcat, dp, conv_w, conv_b)


_SQRT_HALF = 0.7071067811865476
_INV_SQRT_2PI = 0.3989422804014327


def _gelu(x):
    return x * (lax.erf(x * _SQRT_HALF) + 1.0) * 0.5


def _gelu_grad(x):
    return 0.5 * (lax.erf(x * _SQRT_HALF) + 1.0) + x * (_INV_SQRT_2PI * jnp.exp(-0.5 * x * x))


def _layernorm_parts(v):
    mu = jnp.mean(v, axis=-1, keepdims=True)
    vc = v - mu
    rstd = lax.rsqrt(jnp.mean(vc * vc, axis=-1, keepdims=True) + NORM_EPS)
    return vc * rstd, rstd


def _tril_mask(L):
    r = lax.broadcasted_iota(jnp.int32, (L, L), 0)
    c = lax.broadcasted_iota(jnp.int32, (L, L), 1)
    return r >= c


def _sgu_fwd(p, ln_g, ln_b, sgu_w, sgu_b, off, width, *, name, tr=512):
    N = p.shape[0]
    G, L = sgu_w.shape[0], sgu_w.shape[1]
    SW = ln_g.shape[1]
    dh = SW // G
    assert off % SW == 0
    ub = off // SW
    tr = _tile(N, tr, L)
    assert tr % L == 0

    def body(u_ref, v_ref, g_ref, beta_ref, w_ref, b_ref, o_ref):
        u = _gelu(u_ref[...])
        vhat, _ = _layernorm_parts(_gelu(v_ref[...]))
        vn = (vhat * g_ref[...] + beta_ref[...]).astype(BF16)
        mask = _tril_mask(L)
        for gi in range(G):
            w = jnp.where(mask, w_ref[gi], 0.0).astype(BF16)
            bias = b_ref[gi]
            cols = slice(gi * dh, (gi + 1) * dh)
            for n in range(tr // L):
                rows = slice(n * L, (n + 1) * L)
                s = jnp.dot(w, vn[rows, cols], preferred_element_type=F32) + bias
                o_ref[rows, cols] = (u[rows, cols] * s).astype(o_ref.dtype)

    def col(k):
        return pl.BlockSpec((tr, SW), lambda i: (i, k))

    vec = pl.BlockSpec((1, SW), lambda i: (0, 0))
    return pl.pallas_call(
        body, name=name, grid=(N // tr,),
        in_specs=[col(ub), col(ub + 1), vec, vec, pl.BlockSpec((G, L, L), lambda i: (0, 0, 0)),
                  pl.BlockSpec((G, L, 1), lambda i: (0, 0, 0))],
        out_specs=col(0), out_shape=jax.ShapeDtypeStruct((N, width), BF16),
        compiler_params=_params(("parallel",)),
    )(p, p, ln_g, ln_b, sgu_w, sgu_b.reshape(G, L, 1))


def _sgu_bwd(p, dcat, ln_g, ln_b, sgu_w, sgu_b, off, *, name, tr=512):
    N = p.shape[0]
    G, L = sgu_w.shape[0], sgu_w.shape[1]
    SW = ln_g.shape[1]
    dh = SW // G
    assert off % SW == 0
    ub = off // SW
    tr = _tile(N, tr, L)

    def compute(i, u_ref, v_ref, dc_ref, g_ref, beta_ref, w_ref, b_ref,
                du_ref, dv_ref, dw_ref, db_ref, dg_ref, dbeta_ref, du_s, dvn_s):
        pu = u_ref[...]
        pv = v_ref[...]
        u = _gelu(pu)
        vhat, rstd = _layernorm_parts(_gelu(pv))
        gg = g_ref[...]
        vn = (vhat * gg + beta_ref[...]).astype(BF16)
        dc = dc_ref[...]
        mask = _tril_mask(L)

        @pl.when(i == 0)
        def _():
            dw_ref[...] = jnp.zeros_like(dw_ref)
            db_ref[...] = jnp.zeros_like(db_ref)
            dg_ref[...] = jnp.zeros_like(dg_ref)
            dbeta_ref[...] = jnp.zeros_like(dbeta_ref)

        for gi in range(G):
            w = jnp.where(mask, w_ref[gi], 0.0).astype(BF16)
            bias = b_ref[gi]
            cols = slice(gi * dh, (gi + 1) * dh)
            dw_acc = jnp.zeros((L, L), F32)
            db_acc = jnp.zeros((L, 1), F32)
            for n in range(tr // L):
                rows = slice(n * L, (n + 1) * L)
                vb = vn[rows, cols]
                s = jnp.dot(w, vb, preferred_element_type=F32) + bias
                du_s[rows, cols] = dc[rows, cols] * s
                ds = dc[rows, cols] * u[rows, cols]
                db_acc += jnp.sum(ds, axis=1, keepdims=True)
                dsb = ds.astype(BF16)
                dw_acc += lax.dot_general(dsb, vb, (((1,), (1,)), ((), ())), preferred_element_type=F32)
                dvn_s[rows, cols] = lax.dot_general(w, dsb, (((0,), (0,)), ((), ())),
                                                    preferred_element_type=F32)
            dw_ref[gi] += jnp.where(mask, dw_acc, 0.0)
            db_ref[gi] += db_acc

        dvn = dvn_s[...]
        dg_ref[...] += jnp.sum(dvn * vhat, axis=0, keepdims=True)
        dbeta_ref[...] += jnp.sum(dvn, axis=0, keepdims=True)
        dvh = dvn * gg
        dv = rstd * (dvh - jnp.mean(dvh, axis=-1, keepdims=True)
                     - vhat * jnp.mean(dvh * vhat, axis=-1, keepdims=True))
        dv_ref[...] = (dv * _gelu_grad(pv)).astype(dv_ref.dtype)
        du_ref[...] = (du_s[...] * _gelu_grad(pu)).astype(du_ref.dtype)

    def body(u_ref, v_ref, dc_ref, g_ref, beta_ref, w_ref, b_ref,
             dp_ref, dw_ref, db_ref, dg_ref, dbeta_ref, du_s, dvn_s, dv_s):
        i = pl.program_id(0)
        half = pl.program_id(1)

        @pl.when(half == 0)
        def _():
            compute(i, u_ref, v_ref, dc_ref, g_ref, beta_ref, w_ref, b_ref,
                    dp_ref, dv_s, dw_ref, db_ref, dg_ref, dbeta_ref, du_s, dvn_s)

        @pl.when(half == 1)
        def _():
            dp_ref[...] = dv_s[...]

    def col(k):
        return pl.BlockSpec((tr, SW), lambda i, half: (i, k))

    vec = pl.BlockSpec((1, SW), lambda i, half: (0, 0))
    wspec = pl.BlockSpec((G, L, L), lambda i, half: (0, 0, 0))
    bspec = pl.BlockSpec((G, L, 1), lambda i, half: (0, 0, 0))
    return pl.pallas_call(
        body, name=name, grid=(N // tr, 2),
        in_specs=[col(ub), col(ub + 1), col(0), vec, vec, wspec, bspec],
        out_specs=[pl.BlockSpec((tr, SW), lambda i, half: (i, ub + half)), wspec, bspec, vec, vec],
        out_shape=[jax.ShapeDtypeStruct((N, p.shape[1]), BF16),
                   jax.ShapeDtypeStruct((G, L, L), F32), jax.ShapeDtypeStruct((G, L, 1), F32),
                   jax.ShapeDtypeStruct((1, SW), F32), jax.ShapeDtypeStruct((1, SW), F32)],
        scratch_shapes=[pltpu.VMEM((tr, SW), F32), pltpu.VMEM((tr, SW), F32), pltpu.VMEM((tr, SW), BF16)],
        compiler_params=_params(("arbitrary", "arbitrary")),
    )(p, p, dcat, ln_g, ln_b, sgu_w, sgu_b.reshape(G, L, 1))


def _log_sigmoid_pair(z):
    ls = jnp.minimum(z, 0.0) - jnp.log(1.0 + jnp.exp(-jnp.abs(z)))
    return ls, ls - z


def _split_dot(x, m):
    hi = x.astype(BF16)
    lo = (x - hi.astype(F32)).astype(BF16)
    return (jnp.dot(hi, m, preferred_element_type=F32) + jnp.dot(lo, m, preferred_element_type=F32))


def _att_tiles(TB):
    r = lax.broadcasted_iota(jnp.int32, (TB, TB), 0)
    c = lax.broadcasted_iota(jnp.int32, (TB, TB), 1)
    return r, c


def _att_weights(qt, kb, strict, later, carry):
    z = lax.dot_general(qt, kb, (((1,), (1,)), ((), ())), preferred_element_type=F32)
    ls, lk = _log_sigmoid_pair(z)
    if strict is not None:
        lk = jnp.where(strict, lk, 0.0)
    suffix = _split_dot(lk, later) + carry
    a = jnp.exp(ls + suffix)
    if strict is not None:
        a = jnp.where(strict, a, 0.0)
    return ls, lk, suffix, a


HEADS_PER_BLOCK = LANES // SB_DH
ATT_GROUPS = 2
HP = ATT_GROUPS * HEADS_PER_BLOCK
ATT_SCALE = 1.0 / math.sqrt(SB_DH)


def _head_col(hh, part):
    return (hh // HEADS_PER_BLOCK) * 3 * LANES + part * LANES + (hh % HEADS_PER_BLOCK) * SB_DH


def _stage_heads(src_ref, part, dst_ref, T, scale=None):
    rows = _tile(T, 256, SUBLANES)

    def chunk(n, _):
        r0 = pl.multiple_of(n * rows, rows)
        for hh in range(HP):
            col = hh * SB_DH if part is None else _head_col(hh, part)
            x = src_ref[pl.ds(r0, rows), col:col + SB_DH]
            if scale is not None:
                x = x * scale
            dst_ref[hh, pl.ds(r0, rows), :] = x.astype(dst_ref.dtype)
        return 0

    lax.fori_loop(0, T // rows, chunk, 0)


def _attn_fwd(p, cat, B, T, coff, *, name):
    dh = SB_DH
    owidth = ATT_GROUPS * LANES
    nsteps = (cat.shape[1] - coff) // owidth
    TB = _tile(T, ATT_BLOCK)
    nb = T // TB
    assert nb <= LANES and coff % owidth == 0 and (cat.shape[1] - coff) % owidth == 0
    cbase = coff // owidth

    def body(p_ref, cat_ref, o_ref, c_ref, q_ref, k_ref, v_ref):
        _stage_heads(p_ref, 0, q_ref, T, ATT_SCALE)
        _stage_heads(p_ref, 1, k_ref, T)
        _stage_heads(p_ref, 2, v_ref, T)
        r, c = _att_tiles(TB)
        strict = c < r
        later = (r > c).astype(BF16)
        lane = lax.broadcasted_iota(jnp.int32, (TB, LANES), 1)

        def tile(hh, qt, j, carry, mask):
            k0 = pl.multiple_of(j * TB, TB)
            _, lk, suffix, a = _att_weights(qt, k_ref[hh, pl.ds(k0, TB), :], mask, later, carry)
            pv = jnp.dot(a.astype(BF16), v_ref[hh, pl.ds(k0, TB), :], preferred_element_type=F32)
            return pv, suffix[:, 0:1] + lk[:, 0:1]

        def qblock(i, _):
            q0 = pl.multiple_of(i * TB, TB)
            qts = [q_ref[hh, pl.ds(q0, TB), :] for hh in range(HP)]
            state = []
            for hh in range(HP):
                pv, carry = tile(hh, qts[hh], i, jnp.zeros((TB, 1), F32), strict)
                state += [pv, carry, jnp.zeros((TB, LANES), F32)]

            def kblock(jj, st):
                j = i - jj
                out = []
                for hh in range(HP):
                    acc, carry, cm = st[3 * hh:3 * hh + 3]
                    pv, new_carry = tile(hh, qts[hh], j, carry, None)
                    out += [acc + pv, new_carry, jnp.where(lane == j, carry, cm)]
                return tuple(out)

            st = lax.fori_loop(1, i + 1, kblock, tuple(state))
            for hh in range(HP):
                o_ref[pl.ds(q0, TB), hh * dh:(hh + 1) * dh] = st[3 * hh].astype(o_ref.dtype)
                c_ref[hh, pl.ds(q0, TB), :] = st[3 * hh + 2]
            return 0

        lax.fori_loop(0, nb, qblock, 0)

    staged = pltpu.VMEM((HP, T, dh), BF16)
    return pl.pallas_call(
        body, name=name, grid=(B, nsteps),
        in_specs=[pl.BlockSpec((T, 3 * owidth), lambda b, s: (b, s)), _ANY],
        out_specs=[pl.BlockSpec((T, owidth), lambda b, s: (b, cbase + s)),
                   pl.BlockSpec((HP, T, LANES), lambda b, s: (b * nsteps + s, 0, 0))],
        out_shape=[jax.ShapeDtypeStruct(cat.shape, cat.dtype),
                   jax.ShapeDtypeStruct((B * nsteps * HP, T, LANES), F32)],
        input_output_aliases={1: 0}, scratch_shapes=[staged, staged, staged],
        compiler_params=_params(("parallel", "parallel")),
    )(p, cat)


def _attn_bwd(p, dcat, carries, dp, B, T, coff, *, name):
    dh = SB_DH
    owidth = ATT_GROUPS * LANES
    nsteps = (dcat.shape[1] - coff) // owidth
    TB = _tile(T, ATT_BLOCK)
    nb = T // TB
    cbase = coff // owidth

    def body(p_ref, d_ref, c_ref, dp_in_ref, dp_ref, q_ref, k_ref, v_ref, do_ref, dk_ref, dv_ref):
        _stage_heads(p_ref, 0, q_ref, T, ATT_SCALE)
        _stage_heads(p_ref, 1, k_ref, T)
        _stage_heads(p_ref, 2, v_ref, T)
        _stage_heads(d_ref, None, do_ref, T)
        r, c = _att_tiles(TB)
        strict = c < r
        later = (r > c).astype(BF16)
        earlier = (r < c).astype(BF16)
        lane = lax.broadcasted_iota(jnp.int32, (TB, LANES), 1)
        dk_ref[...] = jnp.zeros_like(dk_ref)
        dv_ref[...] = jnp.zeros_like(dv_ref)

        def tile(hh, qt, dot, cm, j, before, mask):
            k0 = pl.multiple_of(j * TB, TB)
            kb = k_ref[hh, pl.ds(k0, TB), :]
            carry = jnp.sum(jnp.where(lane == j, cm, 0.0), axis=1, keepdims=True)
            ls, _, _, a = _att_weights(qt, kb, mask, later, carry)
            dv_ref[hh, pl.ds(k0, TB), :] += lax.dot_general(a.astype(BF16), dot, (((0,), (0,)), ((), ())),
                                                            preferred_element_type=F32)
            da = lax.dot_general(dot, v_ref[hh, pl.ds(k0, TB), :], (((1,), (1,)), ((), ())),
                                 preferred_element_type=F32)
            gl = a * da
            prefix = _split_dot(gl, earlier) + before
            dz = gl - jnp.exp(ls) * (gl + prefix)
            if mask is not None:
                dz = jnp.where(mask, dz, 0.0)
            dzb = dz.astype(BF16)
            dk_ref[hh, pl.ds(k0, TB), :] += lax.dot_general(dzb, qt, (((0,), (0,)), ((), ())),
                                                            preferred_element_type=F32)
            return jnp.dot(dzb, kb, preferred_element_type=F32), prefix[:, TB - 1:TB] + gl[:, TB - 1:TB]

        def qblock(i, _):
            q0 = pl.multiple_of(i * TB, TB)
            qts = [q_ref[hh, pl.ds(q0, TB), :] for hh in range(HP)]
            dots = [do_ref[hh, pl.ds(q0, TB), :] for hh in range(HP)]
            cms = [c_ref[hh, pl.ds(q0, TB), :] for hh in range(HP)]

            def kblock(j, st):
                out = []
                for hh in range(HP):
                    dq, before = st[2 * hh:2 * hh + 2]
                    part, new_before = tile(hh, qts[hh], dots[hh], cms[hh], j, before, None)
                    out += [dq + part, new_before]
                return tuple(out)

            st = lax.fori_loop(0, i, kblock, (jnp.zeros((TB, dh), F32), jnp.zeros((TB, 1), F32)) * HP)
            for hh in range(HP):
                part, _ = tile(hh, qts[hh], dots[hh], cms[hh], i, st[2 * hh + 1], strict)
                dq = (st[2 * hh] + part) * ATT_SCALE
                dp_ref[pl.ds(q0, TB), _head_col(hh, 0):_head_col(hh, 0) + dh] = dq.astype(dp_ref.dtype)
            return 0

        lax.fori_loop(0, nb, qblock, 0)

        def write_back(n, _):
            r0 = pl.multiple_of(n * TB, TB)
            for hh in range(HP):
                dp_ref[pl.ds(r0, TB), _head_col(hh, 1):_head_col(hh, 1) + dh] = (
                    dk_ref[hh, pl.ds(r0, TB), :].astype(dp_ref.dtype))
                dp_ref[pl.ds(r0, TB), _head_col(hh, 2):_head_col(hh, 2) + dh] = (
                    dv_ref[hh, pl.ds(r0, TB), :].astype(dp_ref.dtype))
            return 0

        lax.fori_loop(0, nb, write_back, 0)

    pblk = pl.BlockSpec((T, 3 * owidth), lambda b, s: (b, s))
    staged = pltpu.VMEM((HP, T, dh), BF16)
    accum = pltpu.VMEM((HP, T, dh), F32)
    return pl.pallas_call(
        body, name=name, grid=(B, nsteps),
        in_specs=[pblk, pl.BlockSpec((T, owidth), lambda b, s: (b, cbase + s)),
                  pl.BlockSpec((HP, T, LANES), lambda b, s: (b * nsteps + s, 0, 0)), _ANY],
        out_specs=pblk, out_shape=jax.ShapeDtypeStruct(dp.shape, dp.dtype), input_output_aliases={3: 0},
        scratch_shapes=[staged, staged, staged, staged, accum, accum],
        compiler_params=_params(("parallel", "parallel")),
    )(p, dcat, carries, dp)


def _adamw(w, gparts, m, v, *, name):
    L, R, C = w.shape
    P = gparts[0].shape[0]
    assert len(gparts) == L
    tr = _tile(R, max(SUBLANES, (1 << 19) // (C * P)), SUBLANES)

    def body(*refs):
        w_ref, g_refs, (m_ref, v_ref) = refs[0], refs[1:1 + L], refs[1 + L:3 + L]
        go_ref, d_ref, mo_ref, vo_ref = refs[3 + L:]
        layer = pl.program_id(0)

        def update(g_ref):
            g = g_ref[0].astype(F32)
            for i in range(1, P):
                g = g + g_ref[i].astype(F32)
            m2 = ADAM_B1 * m_ref[...] + (1.0 - ADAM_B1) * g
            v2 = ADAM_B2 * v_ref[...] + (1.0 - ADAM_B2) * (g * g)
            m_hat = m2 / ADAM_C1
            v_hat = v2 / ADAM_C2
            go_ref[...] = g
            d_ref[...] = -ADAM_LR * (m_hat / (jnp.sqrt(v_hat) + ADAM_EPS) + ADAM_WD * w_ref[...])
            mo_ref[...] = m2
            vo_ref[...] = v2

        for l in range(L):
            pl.when(layer == l)(functools.partial(update, g_refs[l]))

    row = pl.BlockSpec((None, tr, C), lambda l, i: (l, i, 0))

    def part(mine):
        return pl.BlockSpec((P, tr, C), lambda l, i: (0, jnp.where(l == mine, i, 0), 0))

    shp = jax.ShapeDtypeStruct((L, R, C), F32)
    return pl.pallas_call(
        body, name=name, grid=(L, R // tr),
        in_specs=[row] + [part(l) for l in range(L)] + [row, row],
        out_specs=[row] * 4, out_shape=[shp] * 4, compiler_params=_params(("arbitrary", "arbitrary")),
    )(w, *gparts, m, v)


def _my_index():
    return 4 * lax.axis_index("x") + 2 * lax.axis_index("y") + lax.axis_index("c")


def _exchange(arrs, gather, *, name):
    n = len(arrs)

    def body(*refs):
        ins, outs = refs[:n], refs[n:2 * n]
        send_sems, recv_sems, local_sems = refs[2 * n:]
        x, y, c = lax.axis_index("x"), lax.axis_index("y"), lax.axis_index("c")
        me = 4 * x + 2 * y + c
        remote, local = [], []
        for a in range(n):
            own = ins[a] if gather[a] else ins[a].at[me]
            cp = pltpu.make_async_copy(own, outs[a].at[me], local_sems.at[a])
            cp.start()
            local.append(cp)
            for k in range(1, N_DEV):
                px = 1 - x if k & 4 else x
                py = 1 - y if k & 2 else y
                pc = 1 - c if k & 1 else c
                src = ins[a] if gather[a] else ins[a].at[4 * px + 2 * py + pc]
                cp = pltpu.make_async_remote_copy(
                    src_ref=src, dst_ref=outs[a].at[me],
                    send_sem=send_sems.at[a, k - 1], recv_sem=recv_sems.at[a, k - 1],
                    device_id=(px, py, pc), device_id_type=pl.DeviceIdType.MESH)
                cp.start()
                remote.append(cp)
        for cp in remote:
            cp.wait()
        for cp in local:
            cp.wait()

    hbm = pl.BlockSpec(memory_space=pltpu.HBM)
    out_shape = [jax.ShapeDtypeStruct(((N_DEV,) + a.shape) if g else a.shape, a.dtype)
                 for a, g in zip(arrs, gather)]
    return pl.pallas_call(
        body, name=name, in_specs=[hbm] * n, out_specs=[hbm] * n, out_shape=out_shape,
        scratch_shapes=[pltpu.SemaphoreType.DMA((n, N_DEV - 1)), pltpu.SemaphoreType.DMA((n, N_DEV - 1)),
                        pltpu.SemaphoreType.DMA((n,))],
    )(*arrs)


_HBM = pl.BlockSpec(memory_space=pltpu.HBM)


def _other_chips(x, y):
    return [(1 - x, y), (x, 1 - y), (1 - x, 1 - y)]


def _gather_two_level(arrs, *, name):
    n = len(arrs)

    def body(*refs):
        ins, outs = refs[:n], refs[n:2 * n]
        send_sems, recv_sems, local_sems = refs[2 * n:]
        x, y, c = lax.axis_index("x"), lax.axis_index("y"), lax.axis_index("c")
        me, sibling = (x, y, c), (x, y, 1 - c)
        chips = _other_chips(x, y)

        def slot(a, px, py, pc):
            return outs[a].at[4 * px + 2 * py + pc]

        def copy(a, k, block, to, src=None):
            return pltpu.make_async_remote_copy(
                src_ref=slot(a, *block) if src is None else src, dst_ref=slot(a, *block),
                send_sem=send_sems.at[a, k], recv_sem=recv_sems.at[a, k],
                device_id=to, device_id_type=pl.DeviceIdType.MESH)

        local, sends = [], []
        for a in range(n):
            cp = pltpu.make_async_copy(ins[a], slot(a, *me), local_sems.at[a])
            cp.start()
            local.append(cp)
            first = [copy(a, 0, me, sibling, src=ins[a])]
            first += [copy(a, 1 + j, me, (*chip, c), src=ins[a]) for j, chip in enumerate(chips)]
            for cp in first:
                cp.start()
            sends += first
        for j, chip in enumerate(chips):
            for a in range(n):
                copy(a, 1 + j, (*chip, c), me).wait_recv()
                cp = copy(a, 4 + j, (*chip, c), sibling)
                cp.start()
                sends.append(cp)
        for a in range(n):
            copy(a, 0, sibling, me).wait_recv()
            for j, chip in enumerate(chips):
                copy(a, 4 + j, (*chip, 1 - c), me).wait_recv()
        for cp in sends:
            cp.wait_send()
        for cp in local:
            cp.wait()

    return pl.pallas_call(
        body, name=name, in_specs=[_HBM] * n, out_specs=[_HBM] * n,
        out_shape=[jax.ShapeDtypeStruct((N_DEV,) + a.shape, a.dtype) for a in arrs],
        scratch_shapes=[pltpu.SemaphoreType.DMA((n, N_DEV - 1)), pltpu.SemaphoreType.DMA((n, N_DEV - 1)),
                        pltpu.SemaphoreType.DMA((n,))],
    )(*arrs)


_SEM = pl.BlockSpec(memory_space=pltpu.SEMAPHORE)
_SPLIT_COPY = pltpu.SideEffectType.DATAFLOW_SIDE_EFFECTING


def _peers(x, y, c):
    return [((1 - x if k & 4 else x), (1 - y if k & 2 else y), (1 - c if k & 1 else c)) for k in range(1, N_DEV)]


_SPLIT_SEMS = 2 * (N_DEV - 1) + 1


def _split_sems(sems, a):
    mine = sems[a * _SPLIT_SEMS:(a + 1) * _SPLIT_SEMS]
    return mine[:N_DEV - 1], mine[N_DEV - 1:2 * (N_DEV - 1)], mine[-1]


def _split_src(ref, scatter, index):
    return ref.at[index] if scatter else ref


def _gather_start(arrs, carry, *, name, scatter=False):
    n = len(arrs)
    ns = n * _SPLIT_SEMS

    def body(*refs):
        ins, lands = refs[:n], refs[n:2 * n]
        sems = refs[2 * n + 1:2 * n + 1 + ns]
        x, y, c = lax.axis_index("x"), lax.axis_index("y"), lax.axis_index("c")
        me = 4 * x + 2 * y + c
        for a in range(n):
            send, recv, local = _split_sems(sems, a)
            pltpu.make_async_copy(_split_src(ins[a], scatter, me), lands[a].at[me], local).start()
            for k, (px, py, pc) in enumerate(_peers(x, y, c)):
                pltpu.make_async_remote_copy(
                    src_ref=_split_src(ins[a], scatter, 4 * px + 2 * py + pc), dst_ref=lands[a].at[me],
                    send_sem=send[k], recv_sem=recv[k],
                    device_id=(px, py, pc), device_id_type=pl.DeviceIdType.MESH).start()

    lands = [lax.empty(a.shape if scatter else (N_DEV,) + a.shape, a.dtype) for a in arrs]
    operands = [pltpu.with_memory_space_constraint(a, pltpu.HBM) for a in list(arrs) + lands + [carry]]
    outs = pl.pallas_call(
        body, name=name, in_specs=[_HBM] * (2 * n + 1), out_specs=[_SEM] * ns + [_HBM] * (2 * n + 1),
        out_shape=[pltpu.SemaphoreType.DMA(())] * ns + [pltpu.HBM(a.shape, a.dtype) for a in operands],
        input_output_aliases={i: ns + i for i in range(2 * n + 1)},
        compiler_params=pltpu.CompilerParams(has_side_effects=_SPLIT_COPY),
    )(*operands)
    return tuple(outs[:-1]), outs[-1]


def _gather_wait(handles, after, *, name, scatter=False):
    n = len(handles) // (_SPLIT_SEMS + 2)
    ns = n * _SPLIT_SEMS
    sems, thru = handles[:ns], handles[ns:]

    def body(*refs):
        ins, lands = refs[:n], refs[n:2 * n]
        sems = refs[2 * n:2 * n + ns]
        x, y, c = lax.axis_index("x"), lax.axis_index("y"), lax.axis_index("c")
        me = 4 * x + 2 * y + c
        for a in range(n):
            send, recv, local = _split_sems(sems, a)
            src = _split_src(ins[a], scatter, me)
            pltpu.make_async_copy(src, lands[a].at[me], local).wait()
            for k, peer in enumerate(_peers(x, y, c)):
                cp = pltpu.make_async_remote_copy(
                    src_ref=src, dst_ref=lands[a].at[me], send_sem=send[k], recv_sem=recv[k],
                    device_id=peer, device_id_type=pl.DeviceIdType.MESH)
                cp.wait_send()
                cp.wait_recv()

    outs = pl.pallas_call(
        body, name=name, in_specs=[_HBM] * (2 * n) + [_SEM] * ns + [_ANY], out_specs=[_HBM] * (2 * n),
        out_shape=[pltpu.HBM(a.shape, a.dtype) for a in thru],
        input_output_aliases={i: i for i in range(2 * n)},
        compiler_params=pltpu.CompilerParams(has_side_effects=_SPLIT_COPY),
    )(*thru, *sems, after)
    return outs[n:]


def _sibling_swap(arrs, *, name):
    n = len(arrs)
    nchip = N_DEV // 2

    def body(*refs):
        ins, outs = refs[:n], refs[n:2 * n]
        send_sems, recv_sems = refs[2 * n:]
        x, y, c = lax.axis_index("x"), lax.axis_index("y"), lax.axis_index("c")
        copies = []
        for a in range(n):
            for k in range(nchip):
                cp = pltpu.make_async_remote_copy(
                    src_ref=ins[a].at[2 * k + 1 - c], dst_ref=outs[a].at[k],
                    send_sem=send_sems.at[a, k], recv_sem=recv_sems.at[a, k],
                    device_id=(x, y, 1 - c), device_id_type=pl.DeviceIdType.MESH)
                cp.start()
                copies.append(cp)
        for cp in copies:
            cp.wait()

    return pl.pallas_call(
        body, name=name, in_specs=[_HBM] * n, out_specs=[_HBM] * n,
        out_shape=[jax.ShapeDtypeStruct((nchip,) + a.shape[1:], a.dtype) for a in arrs],
        scratch_shapes=[pltpu.SemaphoreType.DMA((n, nchip)), pltpu.SemaphoreType.DMA((n, nchip))],
    )(*arrs)


def _chip_exchange(arrs, *, name):
    n = len(arrs)

    def body(*refs):
        ins, outs = refs[:n], refs[n:2 * n]
        send_sems, recv_sems, local_sems = refs[2 * n:]
        x, y, c = lax.axis_index("x"), lax.axis_index("y"), lax.axis_index("c")
        mine = 2 * x + y
        copies = []
        for a in range(n):
            cp = pltpu.make_async_copy(ins[a].at[mine], outs[a].at[mine], local_sems.at[a])
            cp.start()
            copies.append(cp)
            for j, (px, py) in enumerate(_other_chips(x, y)):
                cp = pltpu.make_async_remote_copy(
                    src_ref=ins[a].at[2 * px + py], dst_ref=outs[a].at[mine],
                    send_sem=send_sems.at[a, j], recv_sem=recv_sems.at[a, j],
                    device_id=(px, py, c), device_id_type=pl.DeviceIdType.MESH)
                cp.start()
                copies.append(cp)
        for cp in copies:
            cp.wait()

    return pl.pallas_call(
        body, name=name, in_specs=[_HBM] * n, out_specs=[_HBM] * n,
        out_shape=[jax.ShapeDtypeStruct(a.shape, a.dtype) for a in arrs],
        scratch_shapes=[pltpu.SemaphoreType.DMA((n, 3)), pltpu.SemaphoreType.DMA((n, 3)),
                        pltpu.SemaphoreType.DMA((n,))],
    )(*arrs)


def _pair_sum(a, b, *, name):
    P, R, C = a.shape
    tr = _tile(R, max(SUBLANES, (1 << 19) // C), SUBLANES)

    def body(a_ref, b_ref, o_ref):
        o_ref[...] = (a_ref[...].astype(F32) + b_ref[...].astype(F32)).astype(o_ref.dtype)

    blk = pl.BlockSpec((1, tr, C), lambda p, i: (p, i, 0))
    return pl.pallas_call(
        body, name=name, grid=(P, R // tr), in_specs=[blk, blk], out_specs=blk,
        out_shape=jax.ShapeDtypeStruct(a.shape, a.dtype), compiler_params=_params(("parallel", "parallel")),
    )(a, b)


def _group_in_cols(w, lead):
    X = (w.shape[-1] - lead) // 3
    g = w[..., lead:].reshape(w.shape[:-1] + (3, X // LANES, LANES))
    g = jnp.swapaxes(g, -3, -2).reshape(w.shape[:-1] + (3 * X,))
    return jnp.concatenate([g, w[..., :lead]], axis=-1)


def _ungroup_in_cols(w, lead):
    X = (w.shape[-1] - lead) // 3
    g = w[..., :3 * X].reshape(w.shape[:-1] + (X // LANES, 3, LANES))
    g = jnp.swapaxes(g, -3, -2).reshape(w.shape[:-1] + (3 * X,))
    return jnp.concatenate([w[..., 3 * X:], g], axis=-1)


def _mlp_fwd(h, g, W, tag):
    hn = _rmsnorm(h, g, name=f"mlp_norm_{tag}")
    act = _matmul(hn, _weight(W, f"mlp_w1_{tag}", hn), name=f"mlp_up_{tag}", epilogue=_ep_relu2,
                  out_dtypes=(BF16,))
    out = _matmul(act, _weight(W, f"mlp_w2_{tag}", act), name=f"mlp_down_{tag}", extras=(h,), epilogue=_ep_add)
    return out, (hn, act)


def _mlp_bwd(dout, dout_b, h, g, w1, w2, saved, tag, sent):
    hn, act = saved
    dw2 = _matmul(act, dout_b, ta=True, name=f"mlp_dw2_{tag}", out_dtypes=(GRAD_WIRE,))
    dout_b = sent(dout_b, {f"mlp_w2_{tag}": dw2})
    dz = _matmul(dout_b, w2, tb=True, name=f"mlp_dact_{tag}", extras=(act,), epilogue=_ep_relu2_bwd,
                 out_dtypes=(BF16,))
    dw1 = _matmul(hn, dz, ta=True, name=f"mlp_dw1_{tag}", out_dtypes=(GRAD_WIRE,))
    dz = sent(dz, {f"mlp_w1_{tag}": dw1})
    dhn = _matmul(dz, w1, tb=True, name=f"mlp_dhn_{tag}")
    dh, dh_b, dg = _rmsnorm_bwd(dhn, h, g, dout, name=f"mlp_norm_bwd_{tag}")
    return dh, dh_b, dg, dw1, dw2


class _Lazy:
    def __init__(self, handles, finish, name):
        self.handles, self.finish, self.name, self.done = handles, finish, name, None

    def take(self, after):
        if self.done is None:
            self.done = self.finish(_gather_wait(self.handles, after, name=self.name))
        return self.done


def _weight(W, n, after):
    if isinstance(W[n], _Lazy):
        W.update(W[n].take(after))
    return W[n]


def _local_step(x, target, W, emit=None):
    B, T, D = x.shape
    N = B * T
    G = {}
    row = lambda vec: vec.reshape(1, -1)

    def sent(nxt, grads):
        return nxt if emit is None else emit(grads, nxt)

    PW = W["pool_w"].shape[0] * W["pool_w"].shape[1]
    CW = W["conv_b"].shape[-1]
    SW = W["sgu_norm_g"].shape[-1]
    HW = W["att_width"]

    h0 = x.reshape(N, D)
    xn0 = _rmsnorm(h0, row(W["mix_norm_g"][0]), name="mix_norm_0")
    p0 = _matmul(xn0, W["ab_w_in"], name="ab_in")
    cat0 = _pool_fwd(p0, W["pool_w"], W["pool_scale"], B, T, 3 * CW, PW + CW, name="pool_fwd")
    cat0 = _conv_fwd(p0, cat0, W["conv_w"], row(W["conv_b"]), B, T, PW, name="conv_fwd")
    h1 = _matmul(cat0, W["ab_w_out"], name="ab_out", extras=(h0,), epilogue=_ep_add)
    h2, mlp0 = _mlp_fwd(h1, row(W["mlp_norm_g"][0]), W, 0)
    xn1 = _rmsnorm(h2, row(W["mix_norm_g"][1]), name="mix_norm_1")
    p1 = _matmul(xn1, _weight(W, "cd_w_in", xn1), name="cd_in")
    ln_g, ln_b = row(W["sgu_norm_g"]), row(W["sgu_norm_b"])
    cat1 = _sgu_fwd(p1, ln_g, ln_b, W["sgu_w"], W["sgu_b"], 3 * HW, SW + HW, name="sgu_fwd")
    cat1, att_carries = _attn_fwd(p1, cat1, B, T, SW, name="attn_fwd")
    h3 = _matmul(cat1, W["cd_w_out"], name="cd_out", extras=(h2,), epilogue=_ep_add)
    h4, mlp1 = _mlp_fwd(h3, row(W["mlp_norm_g"][1]), W, 1)

    loss, dh4, dh4_b, G["final_norm_g"] = _final_loss(h4, row(W["final_norm_g"]), target.reshape(N, D),
                                                      name="final_loss")

    dh3, dh3_b, dmlp_g1, dw1_1, dw2_1 = _mlp_bwd(dh4, dh4_b, h3, row(W["mlp_norm_g"][1]), W["mlp_w1_1"],
                                                 W["mlp_w2_1"], mlp1, 1, sent)
    G["cd_w_out"] = _matmul(cat1, dh3_b, ta=True, name="cd_out_dw", out_dtypes=(GRAD_WIRE,))[None]
    dh3_b = sent(dh3_b, {"cd_w_out": G["cd_w_out"][0]})
    dcat1 = _matmul(dh3_b, W["cd_w_out"], tb=True, name="cd_out_dx")
    dp1, G["sgu_w"], dsgu_b, G["sgu_norm_g"], G["sgu_norm_b"] = _sgu_bwd(
        p1, dcat1, ln_g, ln_b, W["sgu_w"], W["sgu_b"], 3 * HW, name="sgu_bwd")
    G["sgu_b"] = dsgu_b.reshape(W["sgu_b"].shape)
    dp1 = _attn_bwd(p1, dcat1, att_carries, dp1, B, T, SW, name="attn_bwd")
    G["cd_w_in"] = _matmul(xn1, dp1, ta=True, name="cd_in_dw", out_dtypes=(GRAD_WIRE,))[None]
    dp1 = sent(dp1, {"cd_w_in": G["cd_w_in"][0]})
    dxn1 = _matmul(dp1, W["cd_w_in"], tb=True, name="cd_in_dx")
    dh2, dh2_b, dmix_g1 = _rmsnorm_bwd(dxn1, h2, row(W["mix_norm_g"][1]), dh3, name="mix_norm_bwd_1")

    dh1, dh1_b, dmlp_g0, dw1_0, dw2_0 = _mlp_bwd(dh2, dh2_b, h1, row(W["mlp_norm_g"][0]), W["mlp_w1_0"],
                                                 W["mlp_w2_0"], mlp0, 0, sent)
    G["ab_w_out"] = _matmul(cat0, dh1_b, ta=True, name="ab_out_dw", out_dtypes=(GRAD_WIRE,))[None]
    dh1_b = sent(dh1_b, {"ab_w_out": G["ab_w_out"][0]})
    dcat0 = _matmul(dh1_b, W["ab_w_out"], tb=True, name="ab_out_dx")
    dp0, G["pool_w"], dps = _pool_bwd(p0, dcat0, W["pool_w"], W["pool_scale"], B, T, 3 * CW, name="pool_bwd")
    G["pool_scale"] = dps.reshape(W["pool_scale"].shape)
    dp0, G["conv_w"], dcb = _conv_bwd(p0, dcat0, dp0, W["conv_w"], row(W["conv_b"]), B, T, PW, name="conv_bwd")
    G["conv_b"] = dcb.reshape(-1)
    G["ab_w_in"] = _matmul(xn0, dp0, ta=True, name="ab_in_dw", out_dtypes=(GRAD_WIRE,))[None]
    dp0 = sent(dp0, {"ab_w_in": G["ab_w_in"][0]})
    dxn0 = _matmul(dp0, W["ab_w_in"], tb=True, name="ab_in_dx")
    dx, _, dmix_g0 = _rmsnorm_bwd(dxn0, h0, row(W["mix_norm_g"][0]), dh1, name="mix_norm_bwd_0")

    G["mix_norm_g"] = jnp.concatenate([dmix_g0, dmix_g1], axis=0)
    G["mlp_norm_g"] = jnp.concatenate([dmlp_g0, dmlp_g1], axis=0)
    G["mlp_w1"] = jnp.stack([dw1_0, dw1_1])
    G["mlp_w2"] = jnp.stack([dw2_0, dw2_1])
    G["final_norm_g"] = G["final_norm_g"].reshape(-1)
    G["sgu_norm_g"] = G["sgu_norm_g"].reshape(-1)
    G["sgu_norm_b"] = G["sgu_norm_b"].reshape(-1)
    return loss[0, 0], dx.reshape(B, T, D), G


_NAMES = ["mix_norm_g", "mlp_norm_g", "ab_w_in", "pool_w", "pool_scale", "conv_w", "conv_b", "ab_w_out",
          "cd_w_in", "sgu_norm_g", "sgu_norm_b", "sgu_w", "sgu_b", "cd_w_out", "mlp_w1", "mlp_w2",
          "final_norm_g"]
_COL_SHARDED = ["ab_w_in", "cd_w_in", "mlp_w1"]
_ROW_SHARDED = ["ab_w_out", "cd_w_out", "mlp_w2"]
_SMALL_SHARDED = ["conv_w", "sgu_norm_g", "sgu_norm_b"]
_REPLICATED = ["mix_norm_g", "mlp_norm_g", "pool_w", "pool_scale", "conv_b", "sgu_w", "sgu_b", "final_norm_g"]


def _pad_rows(a2d, mult=SUBLANES):
    pad = (-a2d.shape[0]) % mult
    return jnp.pad(a2d, ((0, pad), (0, 0))) if pad else a2d


def _pack(arrays):
    return _pad_rows(jnp.concatenate([a.reshape(-1, LANES) for a in arrays], axis=0))


def _unpack(packed, shapes):
    out, r = [], 0
    for s in shapes:
        n = math.prod(s) // LANES
        out.append(packed[r:r + n].reshape(s))
        r += n
    return out


def _small_shard_pack(arrays):
    rows = [jnp.pad(a.reshape(-1, a.shape[-1]), ((0, 0), (0, LANES - a.shape[-1]))) for a in arrays]
    return _pad_rows(jnp.concatenate(rows, axis=0))


def _cols_to_chunks(a):
    n = a.shape[-1] // N_DEV
    return jnp.moveaxis(a.reshape(a.shape[:-1] + (N_DEV, n)), -2, 0)


def _chunks_to_cols(a):
    t = jnp.moveaxis(a, 0, -2)
    return t.reshape(t.shape[:-2] + (t.shape[-2] * t.shape[-1],))


def _rows_to_chunks(a):
    r = a.shape[-2] // N_DEV
    return jnp.moveaxis(a.reshape(a.shape[:-2] + (N_DEV, r, a.shape[-1])), -3, 0)


def _chunks_to_rows(a):
    t = jnp.moveaxis(a, 0, -3)
    return t.reshape(t.shape[:-3] + (t.shape[-3] * t.shape[-2], t.shape[-1]))


def kernel(x, mix_norm_g, mlp_norm_g, ab_w_in, pool_w, pool_scale, conv_w, conv_b, ab_w_out, cd_w_in, sgu_norm_g, sgu_norm_b, sgu_w, sgu_b, cd_w_out, mlp_w1, mlp_w2, final_norm_g, loss_target, m_mix_norm_g, m_mlp_norm_g, m_ab_w_in, m_pool_w, m_pool_scale, m_conv_w, m_conv_b, m_ab_w_out, m_cd_w_in, m_sgu_norm_g, m_sgu_norm_b, m_sgu_w, m_sgu_b, m_cd_w_out, m_mlp_w1, m_mlp_w2, m_final_norm_g, v_mix_norm_g, v_mlp_norm_g, v_ab_w_in, v_pool_w, v_pool_scale, v_conv_w, v_conv_b, v_ab_w_out, v_cd_w_in, v_sgu_norm_g, v_sgu_norm_b, v_sgu_w, v_sgu_b, v_cd_w_out, v_mlp_w1, v_mlp_w2, v_final_norm_g):
    w = dict(zip(_NAMES, (mix_norm_g, mlp_norm_g, ab_w_in, pool_w, pool_scale, conv_w, conv_b, ab_w_out, cd_w_in,
                          sgu_norm_g, sgu_norm_b, sgu_w, sgu_b, cd_w_out, mlp_w1, mlp_w2, final_norm_g)))
    m = dict(zip(_NAMES, (m_mix_norm_g, m_mlp_norm_g, m_ab_w_in, m_pool_w, m_pool_scale, m_conv_w, m_conv_b,
                          m_ab_w_out, m_cd_w_in, m_sgu_norm_g, m_sgu_norm_b, m_sgu_w, m_sgu_b, m_cd_w_out,
                          m_mlp_w1, m_mlp_w2, m_final_norm_g)))
    v = dict(zip(_NAMES, (v_mix_norm_g, v_mlp_norm_g, v_ab_w_in, v_pool_w, v_pool_scale, v_conv_w, v_conv_b,
                          v_ab_w_out, v_cd_w_in, v_sgu_norm_g, v_sgu_norm_b, v_sgu_w, v_sgu_b, v_cd_w_out,
                          v_mlp_w1, v_mlp_w2, v_final_norm_g)))
    big = _COL_SHARDED + _ROW_SHARDED
    me = _my_index()

    small_sh = _small_shard_pack([w[n] for n in _SMALL_SHARDED])
    in_lead = {"ab_w_in": pool_w.shape[1] * pool_w.shape[2], "cd_w_in": 2 * sgu_norm_g.shape[-1] * N_DEV}
    shard = {"ab_w_in": ab_w_in[0], "ab_w_out": ab_w_out[0], "cd_w_in": cd_w_in[0], "cd_w_out": cd_w_out[0]}
    for layer in range(mlp_w1.shape[0]):
        shard[f"mlp_w1_{layer}"], shard[f"mlp_w2_{layer}"] = mlp_w1[layer], mlp_w2[layer]
    shard = {n: a.astype(BF16) for n, a in shard.items()}

    def whole(n, g):
        if n.endswith("_in") or n.startswith("mlp_w1"):
            full = _chunks_to_cols(g)
            return _group_in_cols(full, in_lead[n]) if n in in_lead else full
        return _chunks_to_rows(g)

    W = {"att_width": cd_w_out.shape[1] * N_DEV - sgu_norm_g.shape[-1] * N_DEV}
    later = [(["mlp_w1_0"], "mlp0_up"), (["mlp_w2_0"], "mlp0_down"), (["cd_w_in", "cd_w_out"], "cd"),
             (["mlp_w1_1", "mlp_w2_1"], "mlp1")]

    def start(idx, carry):
        group, tag = later[idx]
        handles, carry = _gather_start([shard[n] for n in group], carry, name=f"gather_{tag}_start")

        def finish(got):
            done = {n: whole(n, g) for n, g in zip(group, got)}
            if idx + 1 < len(later):
                done[group[0]] = start(idx + 1, done[group[0]])
            return done

        W.update({n: _Lazy(handles, finish, f"gather_{tag}_wait") for n in group})
        return carry

    now = ["ab_w_in", "ab_w_out"]
    gathered = _gather_two_level([shard[n] for n in now] + [small_sh], name="gather_weights")
    W.update({n: whole(n, g) for n, g in zip(now, gathered)})
    W["ab_w_in"] = start(0, W["ab_w_in"])
    small_full = gathered[-1]
    r = 0
    for n in _SMALL_SHARDED:
        rows, width = math.prod(w[n].shape[:-1]), w[n].shape[-1]
        W[n] = _chunks_to_cols(small_full[:, r:r + rows, :width])
        r += rows
    for n in _REPLICATED:
        W[n] = w[n]
    for n in ("pool_w", "pool_scale", "sgu_w", "sgu_b"):
        W[n] = W[n][0]

    pending = []

    def emit(grads, carry):
        names = list(grads)
        parts = []
        for n in names:
            g = _ungroup_in_cols(grads[n], in_lead[n]) if n in in_lead else grads[n]
            parts.append(_cols_to_chunks(g) if n.endswith("_in") or n.startswith("mlp_w1") else _rows_to_chunks(g))
        handles, carry = _gather_start(parts, carry, name=f"grads_{names[0]}_start", scatter=True)
        pending.append((names, handles))
        return carry

    loss_part, grad_x, G = _local_step(x, loss_target, W, emit)

    landed = {}
    for names, handles in pending:
        got = _gather_wait(handles, grad_x, name=f"grads_{names[0]}_wait", scatter=True)
        landed.update(zip(names, got))
    small_names = _REPLICATED + _SMALL_SHARDED
    small_grads = [G[n].reshape(-1) for n in small_names]
    loss_row = jnp.full((LANES,), loss_part, F32)
    small_pack = _pack(small_grads + [loss_row])
    small_parts = _exchange([small_pack], [True], name="gather_small_grads")[0]

    grads, deltas, new_m, new_v = {}, {}, {}, {}
    for n in big:
        layers = [landed[f"{n}_{l}"] for l in range(w[n].shape[0])] if n.startswith("mlp") else [landed[n]]
        grads[n], deltas[n], new_m[n], new_v[n] = _adamw(w[n], layers, m[n], v[n], name=f"adamw_{n}")

    rep_shapes = [w[n].shape for n in _REPLICATED]
    rep_rows = sum(math.prod(s) for s in rep_shapes) // LANES
    small_sum_shapes = [(G[n].size,) for n in small_names] + [(LANES,)]
    zero_tail = [jnp.zeros((math.prod(s),), F32) for s in small_sum_shapes[len(_REPLICATED):]]
    w_pack = _pack([w[n] for n in _REPLICATED] + zero_tail)
    m_pack = _pack([m[n] for n in _REPLICATED] + zero_tail)
    v_pack = _pack([v[n] for n in _REPLICATED] + zero_tail)
    outs = [o[0] for o in _adamw(w_pack[None], [small_parts], m_pack[None], v_pack[None], name="adamw_small")]
    summed = _unpack(outs[0], small_sum_shapes)
    for i, n in enumerate(_REPLICATED):
        grads[n] = summed[i].reshape(w[n].shape)
    for dst, o in zip((deltas, new_m, new_v), outs[1:]):
        for n, val in zip(_REPLICATED, _unpack(o[:rep_rows], rep_shapes)):
            dst[n] = val
    loss = summed[-1][0]

    shard_g = []
    for i, n in enumerate(_SMALL_SHARDED):
        full = summed[len(_REPLICATED) + i].reshape(w[n].shape[:-1] + (-1,))
        width = w[n].shape[-1]
        shard_g.append(lax.dynamic_slice_in_dim(full, me * width, width, axis=full.ndim - 1))
    g_sh = _small_shard_pack(shard_g)
    m_sh = _small_shard_pack([m[n] for n in _SMALL_SHARDED])
    v_sh = _small_shard_pack([v[n] for n in _SMALL_SHARDED])
    outs = [o[0] for o in _adamw(small_sh[None], [g_sh[None]], m_sh[None], v_sh[None], name="adamw_small_sharded")]
    r = 0
    for n in _SMALL_SHARDED:
        rows, width = math.prod(w[n].shape[:-1]), w[n].shape[-1]
        for dst, o in zip((grads, deltas, new_m, new_v), outs):
            dst[n] = o[r:r + rows, :width].reshape(w[n].shape)
        r += rows

    return (loss, grad_x, *[grads[n] for n in _NAMES], *[deltas[n] for n in _NAMES],
            *[new_m[n] for n in _NAMES], *[new_v[n] for n in _NAMES])
```

```python
import functools
import math

import jax
import jax.numpy as jnp
from jax import lax
from jax.experimental import pallas as pl
from jax.experimental.pallas import tpu as pltpu

F32 = jnp.float32
BF16 = jnp.bfloat16
GRAD_WIRE = jnp.bfloat16

NORM_EPS = 1e-6
ADAM_LR = 0.001
ADAM_B1 = 0.9
ADAM_B2 = 0.999
ADAM_EPS = 1e-08
ADAM_WD = 0.01
ADAM_STEP = 10
ADAM_C1 = 1.0 - ADAM_B1 ** ADAM_STEP
ADAM_C2 = 1.0 - ADAM_B2 ** ADAM_STEP

N_DEV = 8
LANES = 128
SUBLANES = 8
SB_DH = 64
ATT_BLOCK = 256
POOL_LOG_WINDOWS = 4
VMEM_LIMIT = 56 * 1024 * 1024


def _params(semantics=None):
    return pltpu.CompilerParams(dimension_semantics=semantics, vmem_limit_bytes=VMEM_LIMIT)


def _tile(dim, pref, unit=LANES):
    if dim <= pref:
        return dim
    t = (pref // unit) * unit
    while t >= unit:
        if dim % t == 0:
            return t
        t -= unit
    return dim


def _matmul(a, b, *, name, ta=False, tb=False, extras=(), epilogue=None, out_dtypes=(F32,),
            tm=1024, tn=1024, tk=2048):
    M, K = (a.shape[1], a.shape[0]) if ta else a.shape
    N = b.shape[0] if tb else b.shape[1]
    assert (b.shape[1] if tb else b.shape[0]) == K, (a.shape, b.shape)
    tm, tn, tk = _tile(M, tm), _tile(N, tn), _tile(K, tk)
    nk = K // tk
    dims = (((0 if ta else 1,), (1 if tb else 0,)), ((), ()))
    ne, no = len(extras), len(out_dtypes)

    def body(*refs):
        a_ref, b_ref = refs[0], refs[1]
        e_refs = refs[2:2 + ne]
        o_refs = refs[2 + ne:2 + ne + no]
        k = pl.program_id(2)

        def part():
            return lax.dot_general(a_ref[...].astype(BF16), b_ref[...].astype(BF16), dims,
                                   preferred_element_type=F32)

        def finish(acc):
            outs = epilogue(acc, *[e[...] for e in e_refs]) if epilogue is not None else (acc,)
            for o_ref, val in zip(o_refs, outs):
                o_ref[...] = val.astype(o_ref.dtype)

        if nk == 1:
            finish(part())
        else:
            acc_ref = refs[-1]

            @pl.when(k == 0)
            def _():
                acc_ref[...] = jnp.zeros_like(acc_ref)

            acc_ref[...] += part()

            @pl.when(k == nk - 1)
            def _():
                finish(acc_ref[...])

    a_spec = (pl.BlockSpec((tk, tm), lambda i, j, k: (k, i)) if ta
              else pl.BlockSpec((tm, tk), lambda i, j, k: (i, k)))
    b_spec = (pl.BlockSpec((tn, tk), lambda i, j, k: (j, k)) if tb
              else pl.BlockSpec((tk, tn), lambda i, j, k: (k, j)))
    o_spec = pl.BlockSpec((tm, tn), lambda i, j, k: (i, j))
    outs = pl.pallas_call(
        body,
        name=name,
        grid=(M // tm, N // tn, nk),
        in_specs=[a_spec, b_spec] + [o_spec] * ne,
        out_specs=[o_spec] * no,
        out_shape=[jax.ShapeDtypeStruct((M, N), dt) for dt in out_dtypes],
        scratch_shapes=[pltpu.VMEM((tm, tn), F32)] if nk > 1 else [],
        compiler_params=_params(("parallel", "parallel", "arbitrary")),
    )(a, b, *extras)
    return outs[0] if no == 1 else outs


def _ep_add(acc, res):
    return (acc + res,)


def _ep_relu2(acc):
    r = jnp.maximum(acc, 0.0)
    return (r * r,)


def _ep_relu2_bwd(acc, act):
    return (acc * (2.0 * jnp.sqrt(act.astype(F32))),)


def _rstd(x):
    return lax.rsqrt(jnp.mean(x * x, axis=-1, keepdims=True) + NORM_EPS)


def _rmsnorm(h, g, *, name, tr=512):
    N, D = h.shape
    tr = _tile(N, tr, SUBLANES)

    def body(h_ref, g_ref, o_ref):
        x = h_ref[...]
        o_ref[...] = ((x * _rstd(x)) * g_ref[...]).astype(o_ref.dtype)

    row = pl.BlockSpec((tr, D), lambda i: (i, 0))
    vec = pl.BlockSpec((1, D), lambda i: (0, 0))
    return pl.pallas_call(
        body, name=name, grid=(N // tr,), in_specs=[row, vec], out_specs=row,
        out_shape=jax.ShapeDtypeStruct((N, D), BF16), compiler_params=_params(("parallel",)),
    )(h, g)


def _rmsnorm_bwd(dy, h, g, dres, *, name, tr=512):
    N, D = h.shape
    tr = _tile(N, tr, SUBLANES)

    def body(dy_ref, h_ref, g_ref, r_ref, dh_ref, dhb_ref, dg_ref):
        i = pl.program_id(0)
        x = h_ref[...]
        d = dy_ref[...]
        r = _rstd(x)
        xh = x * r

        @pl.when(i == 0)
        def _():
            dg_ref[...] = jnp.zeros_like(dg_ref)

        dg_ref[...] += jnp.sum(d * xh, axis=0, keepdims=True)
        dxh = d * g_ref[...]
        dh = r_ref[...] + r * (dxh - xh * jnp.mean(dxh * xh, axis=-1, keepdims=True))
        dh_ref[...] = dh
        dhb_ref[...] = dh.astype(dhb_ref.dtype)

    row = pl.BlockSpec((tr, D), lambda i: (i, 0))
    vec = pl.BlockSpec((1, D), lambda i: (0, 0))
    return pl.pallas_call(
        body, name=name, grid=(N // tr,), in_specs=[row, row, vec, row], out_specs=[row, row, vec],
        out_shape=[jax.ShapeDtypeStruct((N, D), F32), jax.ShapeDtypeStruct((N, D), BF16),
                   jax.ShapeDtypeStruct((1, D), F32)],
        compiler_params=_params(("arbitrary",)),
    )(dy, h, g, dres)


def _final_loss(h, g, target, *, name, tr=512):
    N, D = h.shape
    tr = _tile(N, tr, SUBLANES)

    def body(h_ref, g_ref, t_ref, loss_ref, dh_ref, dhb_ref, dg_ref):
        i = pl.program_id(0)
        x = h_ref[...]
        gg = g_ref[...]
        r = _rstd(x)
        xh = x * r
        err = xh * gg - t_ref[...]

        @pl.when(i == 0)
        def _():
            dg_ref[...] = jnp.zeros_like(dg_ref)
            loss_ref[...] = jnp.zeros_like(loss_ref)

        per_row = jnp.mean(err * err, axis=-1, keepdims=True)
        loss_ref[...] += 0.5 * jnp.sum(per_row, axis=0, keepdims=True)
        dy = err * (1.0 / D)
        dg_ref[...] += jnp.sum(dy * xh, axis=0, keepdims=True)
        dxh = dy * gg
        dh = r * (dxh - xh * jnp.mean(dxh * xh, axis=-1, keepdims=True))
        dh_ref[...] = dh
        dhb_ref[...] = dh.astype(dhb_ref.dtype)

    row = pl.BlockSpec((tr, D), lambda i: (i, 0))
    vec = pl.BlockSpec((1, D), lambda i: (0, 0))
    lvec = pl.BlockSpec((1, LANES), lambda i: (0, 0))
    return pl.pallas_call(
        body, name=name, grid=(N // tr,), in_specs=[row, vec, row], out_specs=[lvec, row, row, vec],
        out_shape=[jax.ShapeDtypeStruct((1, LANES), F32), jax.ShapeDtypeStruct((N, D), F32),
                   jax.ShapeDtypeStruct((N, D), BF16), jax.ShapeDtypeStruct((1, D), F32)],
        compiler_params=_params(("arbitrary",)),
    )(h, g, target)


def _shift_down(x, s):
    t = lax.broadcasted_iota(jnp.int32, x.shape, 0)
    return jnp.where(t >= s, pltpu.roll(x, s, 0), 0.0)


def _shift_up(x, s):
    n = x.shape[0]
    t = lax.broadcasted_iota(jnp.int32, x.shape, 0)
    return jnp.where(t < n - s, pltpu.roll(x, n - s, 0), 0.0)


def _window_sum(x, g, shift):
    s = x + shift(x, 1)
    for k in range(1, POOL_LOG_WINDOWS):
        s = jnp.where(k <= g, s + shift(s, 2 ** k), s)
    return s


def _pool_count(shape, g):
    t = lax.broadcasted_iota(jnp.int32, shape, 0)
    return jnp.minimum(t + 1, lax.shift_left(jnp.int32(2), g)).astype(F32)


def _pool_fwd(p, pool_w, pool_scale, B, T, off, width, *, name):
    G, dh = pool_w.shape[0], pool_w.shape[1]
    assert G == POOL_LOG_WINDOWS and off % dh == 0
    base = off // dh

    def body(a_ref, w_ref, s_ref, o_ref):
        g = pl.program_id(0)
        a = a_ref[...]
        pooled = _window_sum(a, g, _shift_down) / _pool_count(a.shape, g) - a
        m = jnp.dot(pooled.astype(BF16), w_ref[0].astype(BF16), preferred_element_type=F32)
        o_ref[...] = (m * s_ref[0]).astype(o_ref.dtype)

    return pl.pallas_call(
        body, name=name, grid=(G, B),
        in_specs=[pl.BlockSpec((T, dh), lambda g, b: (b, base + g)),
                  pl.BlockSpec((1, dh, dh), lambda g, b: (g, 0, 0)),
                  pl.BlockSpec((1, 1, dh), lambda g, b: (g, 0, 0))],
        out_specs=pl.BlockSpec((T, dh), lambda g, b: (b, g)),
        out_shape=jax.ShapeDtypeStruct((B * T, width), BF16),
        compiler_params=_params(("parallel", "parallel")),
    )(p, pool_w, pool_scale.reshape(G, 1, dh))


def _pool_bwd(p, dcat, pool_w, pool_scale, B, T, off, *, name):
    G, dh = pool_w.shape[0], pool_w.shape[1]
    base = off // dh

    def body(a_ref, d_ref, w_ref, s_ref, da_ref, dw_ref, ds_ref):
        g = pl.program_id(0)
        b = pl.program_id(1)
        a = a_ref[...]
        d = d_ref[...]
        cnt = _pool_count(a.shape, g)
        pooled = (_window_sum(a, g, _shift_down) / cnt - a).astype(BF16)
        w = w_ref[0].astype(BF16)
        m = jnp.dot(pooled, w, preferred_element_type=F32)

        @pl.when(b == 0)
        def _():
            dw_ref[...] = jnp.zeros_like(dw_ref)
            ds_ref[...] = jnp.zeros_like(ds_ref)

        ds_ref[0] += jnp.sum(d * m, axis=0, keepdims=True)
        dm = (d * s_ref[0]).astype(BF16)
        dw_ref[0] += lax.dot_general(pooled, dm, (((0,), (0,)), ((), ())), preferred_element_type=F32)
        dpooled = lax.dot_general(dm, w, (((1,), (1,)), ((), ())), preferred_element_type=F32)
        da = _window_sum(dpooled / cnt, g, _shift_up) - dpooled
        da_ref[...] = da.astype(da_ref.dtype)

    pblk = pl.BlockSpec((T, dh), lambda g, b: (b, base + g))
    dblk = pl.BlockSpec((T, dh), lambda g, b: (b, g))
    wspec = pl.BlockSpec((1, dh, dh), lambda g, b: (g, 0, 0))
    sspec = pl.BlockSpec((1, 1, dh), lambda g, b: (g, 0, 0))
    return pl.pallas_call(
        body, name=name, grid=(G, B), in_specs=[pblk, dblk, wspec, sspec], out_specs=[pblk, wspec, sspec],
        out_shape=[jax.ShapeDtypeStruct((B * T, p.shape[1]), BF16), jax.ShapeDtypeStruct((G, dh, dh), F32),
                   jax.ShapeDtypeStruct((G, 1, dh), F32)],
        compiler_params=_params(("parallel", "arbitrary")),
    )(p, dcat, pool_w, pool_scale.reshape(G, 1, dh))


_ANY = pl.BlockSpec(memory_space=pl.ANY)


def _conv_fwd(p, cat, conv_w, conv_b, B, T, coff, *, name):
    CW = conv_w.shape[1]
    tc = LANES
    assert coff % tc == 0 and CW % tc == 0
    cbase = coff // tc

    def body(p_ref, cat_ref, w_ref, b_ref, o_ref):
        xb, gb, gc = p_ref[:, 0:tc], p_ref[:, tc:2 * tc], p_ref[:, 2 * tc:3 * tc]
        c = gc * xb
        w = w_ref[...]
        y = _shift_down(c, 2) * w[0:1] + _shift_down(c, 1) * w[1:2] + c * w[2:3] + b_ref[...]
        o_ref[...] = (gb * y).astype(o_ref.dtype)

    return pl.pallas_call(
        body, name=name, grid=(CW // tc, B),
        in_specs=[pl.BlockSpec((T, 3 * tc), lambda j, b: (b, j)), _ANY,
                  pl.BlockSpec((3, tc), lambda j, b: (0, j)), pl.BlockSpec((1, tc), lambda j, b: (0, j))],
        out_specs=pl.BlockSpec((T, tc), lambda j, b: (b, cbase + j)),
        out_shape=jax.ShapeDtypeStruct(cat.shape, cat.dtype), input_output_aliases={1: 0},
        compiler_params=_params(("parallel", "parallel")),
    )(p, cat, conv_w, conv_b)


def _conv_bwd(p, dcat, dp, conv_w, conv_b, B, T, coff, *, name):
    CW = conv_w.shape[1]
    tc = LANES
    assert coff % tc == 0
    cbase = coff // tc

    def body(p_ref, d_ref, dp_in_ref, w_ref, b_ref, dp_ref, dw_ref, db_ref):
        b = pl.program_id(1)
        xb, gb, gc = p_ref[:, 0:tc], p_ref[:, tc:2 * tc], p_ref[:, 2 * tc:3 * tc]
        d = d_ref[...]
        w = w_ref[...]
        c = gc * xb
        c1 = _shift_down(c, 1)
        c2 = _shift_down(c, 2)
        y = c2 * w[0:1] + c1 * w[1:2] + c * w[2:3] + b_ref[...]
        dy = d * gb
        dp_ref[:, tc:2 * tc] = (d * y).astype(dp_ref.dtype)

        @pl.when(b == 0)
        def _():
            dw_ref[...] = jnp.zeros_like(dw_ref)
            db_ref[...] = jnp.zeros_like(db_ref)

        db_ref[...] += jnp.sum(dy, axis=0, keepdims=True)
        dw_ref[0:1, :] += jnp.sum(dy * c2, axis=0, keepdims=True)
        dw_ref[1:2, :] += jnp.sum(dy * c1, axis=0, keepdims=True)
        dw_ref[2:3, :] += jnp.sum(dy * c, axis=0, keepdims=True)
        dc = dy * w[2:3] + _shift_up(dy, 1) * w[1:2] + _shift_up(dy, 2) * w[0:1]
        dp_ref[:, 2 * tc:3 * tc] = (dc * xb).astype(dp_ref.dtype)
        dp_ref[:, 0:tc] = (dc * gc).astype(dp_ref.dtype)

    wspec = pl.BlockSpec((3, tc), lambda j, b: (0, j))
    bspec = pl.BlockSpec((1, tc), lambda j, b: (0, j))
    pblk = pl.BlockSpec((T, 3 * tc), lambda j, b: (b, j))
    return pl.pallas_call(
        body, name=name, grid=(CW // tc, B),
        in_specs=[pblk, pl.BlockSpec((T, tc), lambda j, b: (b, cbase + j)), _ANY, wspec, bspec],
        out_specs=[pblk, wspec, bspec],
        out_shape=[jax.ShapeDtypeStruct(dp.shape, dp.dtype), jax.ShapeDtypeStruct((3, CW), F32),
                   jax.ShapeDtypeStruct((1, CW), F32)],
        input_output_aliases={2: 0},
        compiler_params=_params(("parallel", "arbitrary")),
    )(p, dcat, dp, conv_w, conv_b)


_SQRT_HALF = 0.7071067811865476
_INV_SQRT_2PI = 0.3989422804014327


def _gelu(x):
    return x * (lax.erf(x * _SQRT_HALF) + 1.0) * 0.5


def _gelu_grad(x):
    return 0.5 * (lax.erf(x * _SQRT_HALF) + 1.0) + x * (_INV_SQRT_2PI * jnp.exp(-0.5 * x * x))


def _layernorm_parts(v):
    mu = jnp.mean(v, axis=-1, keepdims=True)
    vc = v - mu
    rstd = lax.rsqrt(jnp.mean(vc * vc, axis=-1, keepdims=True) + NORM_EPS)
    return vc * rstd, rstd


def _tril_mask(L):
    r = lax.broadcasted_iota(jnp.int32, (L, L), 0)
    c = lax.broadcasted_iota(jnp.int32, (L, L), 1)
    return r >= c


def _sgu_fwd(p, ln_g, ln_b, sgu_w, sgu_b, off, width, *, name, tr=512):
    N = p.shape[0]
    G, L = sgu_w.shape[0], sgu_w.shape[1]
    SW = ln_g.shape[1]
    dh = SW // G
    assert off % SW == 0
    ub = off // SW
    tr = _tile(N, tr, L)
    assert tr % L == 0

    def body(u_ref, v_ref, g_ref, beta_ref, w_ref, b_ref, o_ref):
        u = _gelu(u_ref[...])
        vhat, _ = _layernorm_parts(_gelu(v_ref[...]))
        vn = (vhat * g_ref[...] + beta_ref[...]).astype(BF16)
        mask = _tril_mask(L)
        for gi in range(G):
            w = jnp.where(mask, w_ref[gi], 0.0).astype(BF16)
            bias = b_ref[gi]
            cols = slice(gi * dh, (gi + 1) * dh)
            for n in range(tr // L):
                rows = slice(n * L, (n + 1) * L)
                s = jnp.dot(w, vn[rows, cols], preferred_element_type=F32) + bias
                o_ref[rows, cols] = (u[rows, cols] * s).astype(o_ref.dtype)

    def col(k):
        return pl.BlockSpec((tr, SW), lambda i: (i, k))

    vec = pl.BlockSpec((1, SW), lambda i: (0, 0))
    return pl.pallas_call(
        body, name=name, grid=(N // tr,),
        in_specs=[col(ub), col(ub + 1), vec, vec, pl.BlockSpec((G, L, L), lambda i: (0, 0, 0)),
                  pl.BlockSpec((G, L, 1), lambda i: (0, 0, 0))],
        out_specs=col(0), out_shape=jax.ShapeDtypeStruct((N, width), BF16),
        compiler_params=_params(("parallel",)),
    )(p, p, ln_g, ln_b, sgu_w, sgu_b.reshape(G, L, 1))


def _sgu_bwd(p, dcat, ln_g, ln_b, sgu_w, sgu_b, off, *, name, tr=512):
    N = p.shape[0]
    G, L = sgu_w.shape[0], sgu_w.shape[1]
    SW = ln_g.shape[1]
    dh = SW // G
    assert off % SW == 0
    ub = off // SW
    tr = _tile(N, tr, L)

    def compute(i, u_ref, v_ref, dc_ref, g_ref, beta_ref, w_ref, b_ref,
                du_ref, dv_ref, dw_ref, db_ref, dg_ref, dbeta_ref, du_s, dvn_s):
        pu = u_ref[...]
        pv = v_ref[...]
        u = _gelu(pu)
        vhat, rstd = _layernorm_parts(_gelu(pv))
        gg = g_ref[...]
        vn = (vhat * gg + beta_ref[...]).astype(BF16)
        dc = dc_ref[...]
        mask = _tril_mask(L)

        @pl.when(i == 0)
        def _():
            dw_ref[...] = jnp.zeros_like(dw_ref)
            db_ref[...] = jnp.zeros_like(db_ref)
            dg_ref[...] = jnp.zeros_like(dg_ref)
            dbeta_ref[...] = jnp.zeros_like(dbeta_ref)

        for gi in range(G):
            w = jnp.where(mask, w_ref[gi], 0.0).astype(BF16)
            bias = b_ref[gi]
            cols = slice(gi * dh, (gi + 1) * dh)
            dw_acc = jnp.zeros((L, L), F32)
            db_acc = jnp.zeros((L, 1), F32)
            for n in range(tr // L):
                rows = slice(n * L, (n + 1) * L)
                vb = vn[rows, cols]
                s = jnp.dot(w, vb, preferred_element_type=F32) + bias
                du_s[rows, cols] = dc[rows, cols] * s
                ds = dc[rows, cols] * u[rows, cols]
                db_acc += jnp.sum(ds, axis=1, keepdims=True)
                dsb = ds.astype(BF16)
                dw_acc += lax.dot_general(dsb, vb, (((1,), (1,)), ((), ())), preferred_element_type=F32)
                dvn_s[rows, cols] = lax.dot_general(w, dsb, (((0,), (0,)), ((), ())),
                                                    preferred_element_type=F32)
            dw_ref[gi] += jnp.where(mask, dw_acc, 0.0)
            db_ref[gi] += db_acc

        dvn = dvn_s[...]
        dg_ref[...] += jnp.sum(dvn * vhat, axis=0, keepdims=True)
        dbeta_ref[...] += jnp.sum(dvn, axis=0, keepdims=True)
        dvh = dvn * gg
        dv = rstd * (dvh - jnp.mean(dvh, axis=-1, keepdims=True)
                     - vhat * jnp.mean(dvh * vhat, axis=-1, keepdims=True))
        dv_ref[...] = (dv * _gelu_grad(pv)).astype(dv_ref.dtype)
        du_ref[...] = (du_s[...] * _gelu_grad(pu)).astype(du_ref.dtype)

    def body(u_ref, v_ref, dc_ref, g_ref, beta_ref, w_ref, b_ref,
             dp_ref, dw_ref, db_ref, dg_ref, dbeta_ref, du_s, dvn_s, dv_s):
        i = pl.program_id(0)
        half = pl.program_id(1)

        @pl.when(half == 0)
        def _():
            compute(i, u_ref, v_ref, dc_ref, g_ref, beta_ref, w_ref, b_ref,
                    dp_ref, dv_s, dw_ref, db_ref, dg_ref, dbeta_ref, du_s, dvn_s)

        @pl.when(half == 1)
        def _():
            dp_ref[...] = dv_s[...]

    def col(k):
        return pl.BlockSpec((tr, SW), lambda i, half: (i, k))

    vec = pl.BlockSpec((1, SW), lambda i, half: (0, 0))
    wspec = pl.BlockSpec((G, L, L), lambda i, half: (0, 0, 0))
    bspec = pl.BlockSpec((G, L, 1), lambda i, half: (0, 0, 0))
    return pl.pallas_call(
        body, name=name, grid=(N // tr, 2),
        in_specs=[col(ub), col(ub + 1), col(0), vec, vec, wspec, bspec],
        out_specs=[pl.BlockSpec((tr, SW), lambda i, half: (i, ub + half)), wspec, bspec, vec, vec],
        out_shape=[jax.ShapeDtypeStruct((N, p.shape[1]), BF16),
                   jax.ShapeDtypeStruct((G, L, L), F32), jax.ShapeDtypeStruct((G, L, 1), F32),
                   jax.ShapeDtypeStruct((1, SW), F32), jax.ShapeDtypeStruct((1, SW), F32)],
        scratch_shapes=[pltpu.VMEM((tr, SW), F32), pltpu.VMEM((tr, SW), F32), pltpu.VMEM((tr, SW), BF16)],
        compiler_params=_params(("arbitrary", "arbitrary")),
    )(p, p, dcat, ln_g, ln_b, sgu_w, sgu_b.reshape(G, L, 1))


def _log_sigmoid_pair(z):
    ls = jnp.minimum(z, 0.0) - jnp.log(1.0 + jnp.exp(-jnp.abs(z)))
    return ls, ls - z


def _split_dot(x, m):
    hi = x.astype(BF16)
    lo = (x - hi.astype(F32)).astype(BF16)
    return (jnp.dot(hi, m, preferred_element_type=F32) + jnp.dot(lo, m, preferred_element_type=F32))


def _att_tiles(TB):
    r = lax.broadcasted_iota(jnp.int32, (TB, TB), 0)
    c = lax.broadcasted_iota(jnp.int32, (TB, TB), 1)
    return r, c


def _att_weights(qt, kb, strict, later, carry):
    z = lax.dot_general(qt, kb, (((1,), (1,)), ((), ())), preferred_element_type=F32)
    ls, lk = _log_sigmoid_pair(z)
    if strict is not None:
        lk = jnp.where(strict, lk, 0.0)
    suffix = _split_dot(lk, later) + carry
    a = jnp.exp(ls + suffix)
    if strict is not None:
        a = jnp.where(strict, a, 0.0)
    return ls, lk, suffix, a


HEADS_PER_BLOCK = LANES // SB_DH
ATT_GROUPS = 2
HP = ATT_GROUPS * HEADS_PER_BLOCK
ATT_SCALE = 1.0 / math.sqrt(SB_DH)


def _head_col(hh, part):
    return (hh // HEADS_PER_BLOCK) * 3 * LANES + part * LANES + (hh % HEADS_PER_BLOCK) * SB_DH


def _stage_heads(src_ref, part, dst_ref, T, scale=None):
    rows = _tile(T, 256, SUBLANES)

    def chunk(n, _):
        r0 = pl.multiple_of(n * rows, rows)
        for hh in range(HP):
            col = hh * SB_DH if part is None else _head_col(hh, part)
            x = src_ref[pl.ds(r0, rows), col:col + SB_DH]
            if scale is not None:
                x = x * scale
            dst_ref[hh, pl.ds(r0, rows), :] = x.astype(dst_ref.dtype)
        return 0

    lax.fori_loop(0, T // rows, chunk, 0)


def _attn_fwd(p, cat, B, T, coff, *, name):
    dh = SB_DH
    owidth = ATT_GROUPS * LANES
    nsteps = (cat.shape[1] - coff) // owidth
    TB = _tile(T, ATT_BLOCK)
    nb = T // TB
    assert nb <= LANES and coff % owidth == 0 and (cat.shape[1] - coff) % owidth == 0
    cbase = coff // owidth

    def body(p_ref, cat_ref, o_ref, c_ref, q_ref, k_ref, v_ref):
        _stage_heads(p_ref, 0, q_ref, T, ATT_SCALE)
        _stage_heads(p_ref, 1, k_ref, T)
        _stage_heads(p_ref, 2, v_ref, T)
        r, c = _att_tiles(TB)
        strict = c < r
        later = (r > c).astype(BF16)
        lane = lax.broadcasted_iota(jnp.int32, (TB, LANES), 1)

        def tile(hh, qt, j, carry, mask):
            k0 = pl.multiple_of(j * TB, TB)
            _, lk, suffix, a = _att_weights(qt, k_ref[hh, pl.ds(k0, TB), :], mask, later, carry)
            pv = jnp.dot(a.astype(BF16), v_ref[hh, pl.ds(k0, TB), :], preferred_element_type=F32)
            return pv, suffix[:, 0:1] + lk[:, 0:1]

        def qblock(i, _):
            q0 = pl.multiple_of(i * TB, TB)
            qts = [q_ref[hh, pl.ds(q0, TB), :] for hh in range(HP)]
            state = []
            for hh in range(HP):
                pv, carry = tile(hh, qts[hh], i, jnp.zeros((TB, 1), F32), strict)
                state += [pv, carry, jnp.zeros((TB, LANES), F32)]

            def kblock(jj, st):
                j = i - jj
                out = []
                for hh in range(HP):
                    acc, carry, cm = st[3 * hh:3 * hh + 3]
                    pv, new_carry = tile(hh, qts[hh], j, carry, None)
                    out += [acc + pv, new_carry, jnp.where(lane == j, carry, cm)]
                return tuple(out)

            st = lax.fori_loop(1, i + 1, kblock, tuple(state))
            for hh in range(HP):
                o_ref[pl.ds(q0, TB), hh * dh:(hh + 1) * dh] = st[3 * hh].astype(o_ref.dtype)
                c_ref[hh, pl.ds(q0, TB), :] = st[3 * hh + 2]
            return 0

        lax.fori_loop(0, nb, qblock, 0)

    staged = pltpu.VMEM((HP, T, dh), BF16)
    return pl.pallas_call(
        body, name=name, grid=(B, nsteps),
        in_specs=[pl.BlockSpec((T, 3 * owidth), lambda b, s: (b, s)), _ANY],
        out_specs=[pl.BlockSpec((T, owidth), lambda b, s: (b, cbase + s)),
                   pl.BlockSpec((HP, T, LANES), lambda b, s: (b * nsteps + s, 0, 0))],
        out_shape=[jax.ShapeDtypeStruct(cat.shape, cat.dtype),
                   jax.ShapeDtypeStruct((B * nsteps * HP, T, LANES), F32)],
        input_output_aliases={1: 0}, scratch_shapes=[staged, staged, staged],
        compiler_params=_params(("parallel", "parallel")),
    )(p, cat)


def _attn_bwd(p, dcat, carries, dp, B, T, coff, *, name):
    dh = SB_DH
    owidth = ATT_GROUPS * LANES
    nsteps = (dcat.shape[1] - coff) // owidth
    TB = _tile(T, ATT_BLOCK)
    nb = T // TB
    cbase = coff // owidth

    def body(p_ref, d_ref, c_ref, dp_in_ref, dp_ref, q_ref, k_ref, v_ref, do_ref, dk_ref, dv_ref):
        _stage_heads(p_ref, 0, q_ref, T, ATT_SCALE)
        _stage_heads(p_ref, 1, k_ref, T)
        _stage_heads(p_ref, 2, v_ref, T)
        _stage_heads(d_ref, None, do_ref, T)
        r, c = _att_tiles(TB)
        strict = c < r
        later = (r > c).astype(BF16)
        earlier = (r < c).astype(BF16)
        lane = lax.broadcasted_iota(jnp.int32, (TB, LANES), 1)
        dk_ref[...] = jnp.zeros_like(dk_ref)
        dv_ref[...] = jnp.zeros_like(dv_ref)

        def tile(hh, qt, dot, cm, j, before, mask):
            k0 = pl.multiple_of(j * TB, TB)
            kb = k_ref[hh, pl.ds(k0, TB), :]
            carry = jnp.sum(jnp.where(lane == j, cm, 0.0), axis=1, keepdims=True)
            ls, _, _, a = _att_weights(qt, kb, mask, later, carry)
            dv_ref[hh, pl.ds(k0, TB), :] += lax.dot_general(a.astype(BF16), dot, (((0,), (0,)), ((), ())),
                                                            preferred_element_type=F32)
            da = lax.dot_general(dot, v_ref[hh, pl.ds(k0, TB), :], (((1,), (1,)), ((), ())),
                                 preferred_element_type=F32)
            gl = a * da
            prefix = _split_dot(gl, earlier) + before
            dz = gl - jnp.exp(ls) * (gl + prefix)
            if mask is not None:
                dz = jnp.where(mask, dz, 0.0)
            dzb = dz.astype(BF16)
            dk_ref[hh, pl.ds(k0, TB), :] += lax.dot_general(dzb, qt, (((0,), (0,)), ((), ())),
                                                            preferred_element_type=F32)
            return jnp.dot(dzb, kb, preferred_element_type=F32), prefix[:, TB - 1:TB] + gl[:, TB - 1:TB]

        def qblock(i, _):
            q0 = pl.multiple_of(i * TB, TB)
            qts = [q_ref[hh, pl.ds(q0, TB), :] for hh in range(HP)]
            dots = [do_ref[hh, pl.ds(q0, TB), :] for hh in range(HP)]
            cms = [c_ref[hh, pl.ds(q0, TB), :] for hh in range(HP)]

            def kblock(j, st):
                out = []
                for hh in range(HP):
                    dq, before = st[2 * hh:2 * hh + 2]
                    part, new_before = tile(hh, qts[hh], dots[hh], cms[hh], j, before, None)
                    out += [dq + part, new_before]
                return tuple(out)

            st = lax.fori_loop(0, i, kblock, (jnp.zeros((TB, dh), F32), jnp.zeros((TB, 1), F32)) * HP)
            for hh in range(HP):
                part, _ = tile(hh, qts[hh], dots[hh], cms[hh], i, st[2 * hh + 1], strict)
                dq = (st[2 * hh] + part) * ATT_SCALE
                dp_ref[pl.ds(q0, TB), _head_col(hh, 0):_head_col(hh, 0) + dh] = dq.astype(dp_ref.dtype)
            return 0

        lax.fori_loop(0, nb, qblock, 0)

        def write_back(n, _):
            r0 = pl.multiple_of(n * TB, TB)
            for hh in range(HP):
                dp_ref[pl.ds(r0, TB), _head_col(hh, 1):_head_col(hh, 1) + dh] = (
                    dk_ref[hh, pl.ds(r0, TB), :].astype(dp_ref.dtype))
                dp_ref[pl.ds(r0, TB), _head_col(hh, 2):_head_col(hh, 2) + dh] = (
                    dv_ref[hh, pl.ds(r0, TB), :].astype(dp_ref.dtype))
            return 0

        lax.fori_loop(0, nb, write_back, 0)

    pblk = pl.BlockSpec((T, 3 * owidth), lambda b, s: (b, s))
    staged = pltpu.VMEM((HP, T, dh), BF16)
    accum = pltpu.VMEM((HP, T, dh), F32)
    return pl.pallas_call(
        body, name=name, grid=(B, nsteps),
        in_specs=[pblk, pl.BlockSpec((T, owidth), lambda b, s: (b, cbase + s)),
                  pl.BlockSpec((HP, T, LANES), lambda b, s: (b * nsteps + s, 0, 0)), _ANY],
        out_specs=pblk, out_shape=jax.ShapeDtypeStruct(dp.shape, dp.dtype), input_output_aliases={3: 0},
        scratch_shapes=[staged, staged, staged, staged, accum, accum],
        compiler_params=_params(("parallel", "parallel")),
    )(p, dcat, carries, dp)


def _adamw(w, gparts, m, v, *, name):
    L, R, C = w.shape
    P = gparts[0].shape[0]
    assert len(gparts) == L
    tr = _tile(R, max(SUBLANES, (1 << 19) // (C * P)), SUBLANES)

    def body(*refs):
        w_ref, g_refs, (m_ref, v_ref) = refs[0], refs[1:1 + L], refs[1 + L:3 + L]
        go_ref, d_ref, mo_ref, vo_ref = refs[3 + L:]
        layer = pl.program_id(0)

        def update(g_ref):
            g = g_ref[0].astype(F32)
            for i in range(1, P):
                g = g + g_ref[i].astype(F32)
            m2 = ADAM_B1 * m_ref[...] + (1.0 - ADAM_B1) * g
            v2 = ADAM_B2 * v_ref[...] + (1.0 - ADAM_B2) * (g * g)
            m_hat = m2 / ADAM_C1
            v_hat = v2 / ADAM_C2
            go_ref[...] = g
            d_ref[...] = -ADAM_LR * (m_hat / (jnp.sqrt(v_hat) + ADAM_EPS) + ADAM_WD * w_ref[...])
            mo_ref[...] = m2
            vo_ref[...] = v2

        for l in range(L):
            pl.when(layer == l)(functools.partial(update, g_refs[l]))

    row = pl.BlockSpec((None, tr, C), lambda l, i: (l, i, 0))

    def part(mine):
        return pl.BlockSpec((P, tr, C), lambda l, i: (0, jnp.where(l == mine, i, 0), 0))

    shp = jax.ShapeDtypeStruct((L, R, C), F32)
    return pl.pallas_call(
        body, name=name, grid=(L, R // tr),
        in_specs=[row] + [part(l) for l in range(L)] + [row, row],
        out_specs=[row] * 4, out_shape=[shp] * 4, compiler_params=_params(("arbitrary", "arbitrary")),
    )(w, *gparts, m, v)


def _my_index():
    return 4 * lax.axis_index("x") + 2 * lax.axis_index("y") + lax.axis_index("c")


def _exchange(arrs, gather, *, name):
    n = len(arrs)

    def body(*refs):
        ins, outs = refs[:n], refs[n:2 * n]
        send_sems, recv_sems, local_sems = refs[2 * n:]
        x, y, c = lax.axis_index("x"), lax.axis_index("y"), lax.axis_index("c")
        me = 4 * x + 2 * y + c
        remote, local = [], []
        for a in range(n):
            own = ins[a] if gather[a] else ins[a].at[me]
            cp = pltpu.make_async_copy(own, outs[a].at[me], local_sems.at[a])
            cp.start()
            local.append(cp)
            for k in range(1, N_DEV):
                px = 1 - x if k & 4 else x
                py = 1 - y if k & 2 else y
                pc = 1 - c if k & 1 else c
                src = ins[a] if gather[a] else ins[a].at[4 * px + 2 * py + pc]
                cp = pltpu.make_async_remote_copy(
                    src_ref=src, dst_ref=outs[a].at[me],
                    send_sem=send_sems.at[a, k - 1], recv_sem=recv_sems.at[a, k - 1],
                    device_id=(px, py, pc), device_id_type=pl.DeviceIdType.MESH)
                cp.start()
                remote.append(cp)
        for cp in remote:
            cp.wait()
        for cp in local:
            cp.wait()

    hbm = pl.BlockSpec(memory_space=pltpu.HBM)
    out_shape = [jax.ShapeDtypeStruct(((N_DEV,) + a.shape) if g else a.shape, a.dtype)
                 for a, g in zip(arrs, gather)]
    return pl.pallas_call(
        body, name=name, in_specs=[hbm] * n, out_specs=[hbm] * n, out_shape=out_shape,
        scratch_shapes=[pltpu.SemaphoreType.DMA((n, N_DEV - 1)), pltpu.SemaphoreType.DMA((n, N_DEV - 1)),
                        pltpu.SemaphoreType.DMA((n,))],
    )(*arrs)


_HBM = pl.BlockSpec(memory_space=pltpu.HBM)


def _other_chips(x, y):
    return [(1 - x, y), (x, 1 - y), (1 - x, 1 - y)]


def _gather_two_level(arrs, *, name):
    n = len(arrs)

    def body(*refs):
        ins, outs = refs[:n], refs[n:2 * n]
        send_sems, recv_sems, local_sems = refs[2 * n:]
        x, y, c = lax.axis_index("x"), lax.axis_index("y"), lax.axis_index("c")
        me, sibling = (x, y, c), (x, y, 1 - c)
        chips = _other_chips(x, y)

        def slot(a, px, py, pc):
            return outs[a].at[4 * px + 2 * py + pc]

        def copy(a, k, block, to, src=None):
            return pltpu.make_async_remote_copy(
                src_ref=slot(a, *block) if src is None else src, dst_ref=slot(a, *block),
                send_sem=send_sems.at[a, k], recv_sem=recv_sems.at[a, k],
                device_id=to, device_id_type=pl.DeviceIdType.MESH)

        local, sends = [], []
        for a in range(n):
            cp = pltpu.make_async_copy(ins[a], slot(a, *me), local_sems.at[a])
            cp.start()
            local.append(cp)
            first = [copy(a, 0, me, sibling, src=ins[a])]
            first += [copy(a, 1 + j, me, (*chip, c), src=ins[a]) for j, chip in enumerate(chips)]
            for cp in first:
                cp.start()
            sends += first
        for j, chip in enumerate(chips):
            for a in range(n):
                copy(a, 1 + j, (*chip, c), me).wait_recv()
                cp = copy(a, 4 + j, (*chip, c), sibling)
                cp.start()
                sends.append(cp)
        for a in range(n):
            copy(a, 0, sibling, me).wait_recv()
            for j, chip in enumerate(chips):
                copy(a, 4 + j, (*chip, 1 - c), me).wait_recv()
        for cp in sends:
            cp.wait_send()
        for cp in local:
            cp.wait()

    return pl.pallas_call(
        body, name=name, in_specs=[_HBM] * n, out_specs=[_HBM] * n,
        out_shape=[jax.ShapeDtypeStruct((N_DEV,) + a.shape, a.dtype) for a in arrs],
        scratch_shapes=[pltpu.SemaphoreType.DMA((n, N_DEV - 1)), pltpu.SemaphoreType.DMA((n, N_DEV - 1)),
                        pltpu.SemaphoreType.DMA((n,))],
    )(*arrs)


_SEM = pl.BlockSpec(memory_space=pltpu.SEMAPHORE)
_SPLIT_COPY = pltpu.SideEffectType.DATAFLOW_SIDE_EFFECTING


def _peers(x, y, c):
    return [((1 - x if k & 4 else x), (1 - y if k & 2 else y), (1 - c if k & 1 else c)) for k in range(1, N_DEV)]


_SPLIT_SEMS = 2 * (N_DEV - 1) + 1


def _split_sems(sems, a):
    mine = sems[a * _SPLIT_SEMS:(a + 1) * _SPLIT_SEMS]
    return mine[:N_DEV - 1], mine[N_DEV - 1:2 * (N_DEV - 1)], mine[-1]


def _split_src(ref, scatter, index):
    return ref.at[index] if scatter else ref


def _gather_start(arrs, carry, *, name, scatter=False):
    n = len(arrs)
    ns = n * _SPLIT_SEMS

    def body(*refs):
        ins, lands = refs[:n], refs[n:2 * n]
        sems = refs[2 * n + 1:2 * n + 1 + ns]
        x, y, c = lax.axis_index("x"), lax.axis_index("y"), lax.axis_index("c")
        me = 4 * x + 2 * y + c
        for a in range(n):
            send, recv, local = _split_sems(sems, a)
            pltpu.make_async_copy(_split_src(ins[a], scatter, me), lands[a].at[me], local).start()
            for k, (px, py, pc) in enumerate(_peers(x, y, c)):
                pltpu.make_async_remote_copy(
                    src_ref=_split_src(ins[a], scatter, 4 * px + 2 * py + pc), dst_ref=lands[a].at[me],
                    send_sem=send[k], recv_sem=recv[k],
                    device_id=(px, py, pc), device_id_type=pl.DeviceIdType.MESH).start()

    lands = [lax.empty(a.shape if scatter else (N_DEV,) + a.shape, a.dtype) for a in arrs]
    operands = [pltpu.with_memory_space_constraint(a, pltpu.HBM) for a in list(arrs) + lands + [carry]]
    outs = pl.pallas_call(
        body, name=name, in_specs=[_HBM] * (2 * n + 1), out_specs=[_SEM] * ns + [_HBM] * (2 * n + 1),
        out_shape=[pltpu.SemaphoreType.DMA(())] * ns + [pltpu.HBM(a.shape, a.dtype) for a in operands],
        input_output_aliases={i: ns + i for i in range(2 * n + 1)},
        compiler_params=pltpu.CompilerParams(has_side_effects=_SPLIT_COPY),
    )(*operands)
    return tuple(outs[:-1]), outs[-1]


def _gather_wait(handles, after, *, name, scatter=False):
    n = len(handles) // (_SPLIT_SEMS + 2)
    ns = n * _SPLIT_SEMS
    sems, thru = handles[:ns], handles[ns:]

    def body(*refs):
        ins, lands = refs[:n], refs[n:2 * n]
        sems = refs[2 * n:2 * n + ns]
        x, y, c = lax.axis_index("x"), lax.axis_index("y"), lax.axis_index("c")
        me = 4 * x + 2 * y + c
        for a in range(n):
            send, recv, local = _split_sems(sems, a)
            src = _split_src(ins[a], scatter, me)
            pltpu.make_async_copy(src, lands[a].at[me], local).wait()
            for k, peer in enumerate(_peers(x, y, c)):
                cp = pltpu.make_async_remote_copy(
                    src_ref=src, dst_ref=lands[a].at[me], send_sem=send[k], recv_sem=recv[k],
                    device_id=peer, device_id_type=pl.DeviceIdType.MESH)
                cp.wait_send()
                cp.wait_recv()

    outs = pl.pallas_call(
        body, name=name, in_specs=[_HBM] * (2 * n) + [_SEM] * ns + [_ANY], out_specs=[_HBM] * (2 * n),
        out_shape=[pltpu.HBM(a.shape, a.dtype) for a in thru],
        input_output_aliases={i: i for i in range(2 * n)},
        compiler_params=pltpu.CompilerParams(has_side_effects=_SPLIT_COPY),
    )(*thru, *sems, after)
    return outs[n:]


def _sibling_swap(arrs, *, name):
    n = len(arrs)
    nchip = N_DEV // 2

    def body(*refs):
        ins, outs = refs[:n], refs[n:2 * n]
        send_sems, recv_sems = refs[2 * n:]
        x, y, c = lax.axis_index("x"), lax.axis_index("y"), lax.axis_index("c")
        copies = []
        for a in range(n):
            for k in range(nchip):
                cp = pltpu.make_async_remote_copy(
                    src_ref=ins[a].at[2 * k + 1 - c], dst_ref=outs[a].at[k],
                    send_sem=send_sems.at[a, k], recv_sem=recv_sems.at[a, k],
                    device_id=(x, y, 1 - c), device_id_type=pl.DeviceIdType.MESH)
                cp.start()
                copies.append(cp)
        for cp in copies:
            cp.wait()

    return pl.pallas_call(
        body, name=name, in_specs=[_HBM] * n, out_specs=[_HBM] * n,
        out_shape=[jax.ShapeDtypeStruct((nchip,) + a.shape[1:], a.dtype) for a in arrs],
        scratch_shapes=[pltpu.SemaphoreType.DMA((n, nchip)), pltpu.SemaphoreType.DMA((n, nchip))],
    )(*arrs)


def _chip_exchange(arrs, *, name):
    n = len(arrs)

    def body(*refs):
        ins, outs = refs[:n], refs[n:2 * n]
        send_sems, recv_sems, local_sems = refs[2 * n:]
        x, y, c = lax.axis_index("x"), lax.axis_index("y"), lax.axis_index("c")
        mine = 2 * x + y
        copies = []
        for a in range(n):
            cp = pltpu.make_async_copy(ins[a].at[mine], outs[a].at[mine], local_sems.at[a])
            cp.start()
            copies.append(cp)
            for j, (px, py) in enumerate(_other_chips(x, y)):
                cp = pltpu.make_async_remote_copy(
                    src_ref=ins[a].at[2 * px + py], dst_ref=outs[a].at[mine],
                    send_sem=send_sems.at[a, j], recv_sem=recv_sems.at[a, j],
                    device_id=(px, py, c), device_id_type=pl.DeviceIdType.MESH)
                cp.start()
                copies.append(cp)
        for cp in copies:
            cp.wait()

    return pl.pallas_call(
        body, name=name, in_specs=[_HBM] * n, out_specs=[_HBM] * n,
        out_shape=[jax.ShapeDtypeStruct(a.shape, a.dtype) for a in arrs],
        scratch_shapes=[pltpu.SemaphoreType.DMA((n, 3)), pltpu.SemaphoreType.DMA((n, 3)),
                        pltpu.SemaphoreType.DMA((n,))],
    )(*arrs)


def _pair_sum(a, b, *, name):
    P, R, C = a.shape
    tr = _tile(R, max(SUBLANES, (1 << 19) // C), SUBLANES)

    def body(a_ref, b_ref, o_ref):
        o_ref[...] = (a_ref[...].astype(F32) + b_ref[...].astype(F32)).astype(o_ref.dtype)

    blk = pl.BlockSpec((1, tr, C), lambda p, i: (p, i, 0))
    return pl.pallas_call(
        body, name=name, grid=(P, R // tr), in_specs=[blk, blk], out_specs=blk,
        out_shape=jax.ShapeDtypeStruct(a.shape, a.dtype), compiler_params=_params(("parallel", "parallel")),
    )(a, b)


def _group_in_cols(w, lead):
    X = (w.shape[-1] - lead) // 3
    g = w[..., lead:].reshape(w.shape[:-1] + (3, X // LANES, LANES))
    g = jnp.swapaxes(g, -3, -2).reshape(w.shape[:-1] + (3 * X,))
    return jnp.concatenate([g, w[..., :lead]], axis=-1)


def _ungroup_in_cols(w, lead):
    X = (w.shape[-1] - lead) // 3
    g = w[..., :3 * X].reshape(w.shape[:-1] + (X // LANES, 3, LANES))
    g = jnp.swapaxes(g, -3, -2).reshape(w.shape[:-1] + (3 * X,))
    return jnp.concatenate([w[..., 3 * X:], g], axis=-1)


def _mlp_fwd(h, g, W, tag):
    hn = _rmsnorm(h, g, name=f"mlp_norm_{tag}")
    act = _matmul(hn, _weight(W, f"mlp_w1_{tag}", hn), name=f"mlp_up_{tag}", epilogue=_ep_relu2,
                  out_dtypes=(BF16,))
    out = _matmul(act, _weight(W, f"mlp_w2_{tag}", act), name=f"mlp_down_{tag}", extras=(h,), epilogue=_ep_add)
    return out, (hn, act)


def _mlp_bwd(dout, dout_b, h, g, w1, w2, saved, tag, sent):
    hn, act = saved
    dw2 = _matmul(act, dout_b, ta=True, name=f"mlp_dw2_{tag}", out_dtypes=(GRAD_WIRE,))
    dout_b = sent(dout_b, {f"mlp_w2_{tag}": dw2})
    dz = _matmul(dout_b, w2, tb=True, name=f"mlp_dact_{tag}", extras=(act,), epilogue=_ep_relu2_bwd,
                 out_dtypes=(BF16,))
    dw1 = _matmul(hn, dz, ta=True, name=f"mlp_dw1_{tag}", out_dtypes=(GRAD_WIRE,))
    dz = sent(dz, {f"mlp_w1_{tag}": dw1})
    dhn = _matmul(dz, w1, tb=True, name=f"mlp_dhn_{tag}")
    dh, dh_b, dg = _rmsnorm_bwd(dhn, h, g, dout, name=f"mlp_norm_bwd_{tag}")
    return dh, dh_b, dg, dw1, dw2


class _Lazy:
    def __init__(self, handles, finish, name):
        self.handles, self.finish, self.name, self.done = handles, finish, name, None

    def take(self, after):
        if self.done is None:
            self.done = self.finish(_gather_wait(self.handles, after, name=self.name))
        return self.done


def _weight(W, n, after):
    if isinstance(W[n], _Lazy):
        W.update(W[n].take(after))
    return W[n]


def _local_step(x, target, W, emit=None):
    B, T, D = x.shape
    N = B * T
    G = {}
    row = lambda vec: vec.reshape(1, -1)

    def sent(nxt, grads):
        return nxt if emit is None else emit(grads, nxt)

    PW = W["pool_w"].shape[0] * W["pool_w"].shape[1]
    CW = W["conv_b"].shape[-1]
    SW = W["sgu_norm_g"].shape[-1]
    HW = W["att_width"]

    h0 = x.reshape(N, D)
    xn0 = _rmsnorm(h0, row(W["mix_norm_g"][0]), name="mix_norm_0")
    p0 = _matmul(xn0, W["ab_w_in"], name="ab_in")
    cat0 = _pool_fwd(p0, W["pool_w"], W["pool_scale"], B, T, 3 * CW, PW + CW, name="pool_fwd")
    cat0 = _conv_fwd(p0, cat0, W["conv_w"], row(W["conv_b"]), B, T, PW, name="conv_fwd")
    h1 = _matmul(cat0, W["ab_w_out"], name="ab_out", extras=(h0,), epilogue=_ep_add)
    h2, mlp0 = _mlp_fwd(h1, row(W["mlp_norm_g"][0]), W, 0)
    xn1 = _rmsnorm(h2, row(W["mix_norm_g"][1]), name="mix_norm_1")
    p1 = _matmul(xn1, _weight(W, "cd_w_in", xn1), name="cd_in")
    ln_g, ln_b = row(W["sgu_norm_g"]), row(W["sgu_norm_b"])
    cat1 = _sgu_fwd(p1, ln_g, ln_b, W["sgu_w"], W["sgu_b"], 3 * HW, SW + HW, name="sgu_fwd")
    cat1, att_carries = _attn_fwd(p1, cat1, B, T, SW, name="attn_fwd")
    h3 = _matmul(cat1, W["cd_w_out"], name="cd_out", extras=(h2,), epilogue=_ep_add)
    h4, mlp1 = _mlp_fwd(h3, row(W["mlp_norm_g"][1]), W, 1)

    loss, dh4, dh4_b, G["final_norm_g"] = _final_loss(h4, row(W["final_norm_g"]), target.reshape(N, D),
                                                      name="final_loss")

    dh3, dh3_b, dmlp_g1, dw1_1, dw2_1 = _mlp_bwd(dh4, dh4_b, h3, row(W["mlp_norm_g"][1]), W["mlp_w1_1"],
                                                 W["mlp_w2_1"], mlp1, 1, sent)
    G["cd_w_out"] = _matmul(cat1, dh3_b, ta=True, name="cd_out_dw", out_dtypes=(GRAD_WIRE,))[None]
    dh3_b = sent(dh3_b, {"cd_w_out": G["cd_w_out"][0]})
    dcat1 = _matmul(dh3_b, W["cd_w_out"], tb=True, name="cd_out_dx")
    dp1, G["sgu_w"], dsgu_b, G["sgu_norm_g"], G["sgu_norm_b"] = _sgu_bwd(
        p1, dcat1, ln_g, ln_b, W["sgu_w"], W["sgu_b"], 3 * HW, name="sgu_bwd")
    G["sgu_b"] = dsgu_b.reshape(W["sgu_b"].shape)
    dp1 = _attn_bwd(p1, dcat1, att_carries, dp1, B, T, SW, name="attn_bwd")
    G["cd_w_in"] = _matmul(xn1, dp1, ta=True, name="cd_in_dw", out_dtypes=(GRAD_WIRE,))[None]
    dp1 = sent(dp1, {"cd_w_in": G["cd_w_in"][0]})
    dxn1 = _matmul(dp1, W["cd_w_in"], tb=True, name="cd_in_dx")
    dh2, dh2_b, dmix_g1 = _rmsnorm_bwd(dxn1, h2, row(W["mix_norm_g"][1]), dh3, name="mix_norm_bwd_1")

    dh1, dh1_b, dmlp_g0, dw1_0, dw2_0 = _mlp_bwd(dh2, dh2_b, h1, row(W["mlp_norm_g"][0]), W["mlp_w1_0"],
                                                 W["mlp_w2_0"], mlp0, 0, sent)
    G["ab_w_out"] = _matmul(cat0, dh1_b, ta=True, name="ab_out_dw", out_dtypes=(GRAD_WIRE,))[None]
    dh1_b = sent(dh1_b, {"ab_w_out": G["ab_w_out"][0]})
    dcat0 = _matmul(dh1_b, W["ab_w_out"], tb=True, name="ab_out_dx")
    dp0, G["pool_w"], dps = _pool_bwd(p0, dcat0, W["pool_w"], W["pool_scale"], B, T, 3 * CW, name="pool_bwd")
    G["pool_scale"] = dps.reshape(W["pool_scale"].shape)
    dp0, G["conv_w"], dcb = _conv_bwd(p0, dcat0, dp0, W["conv_w"], row(W["conv_b"]), B, T, PW, name="conv_bwd")
    G["conv_b"] = dcb.reshape(-1)
    G["ab_w_in"] = _matmul(xn0, dp0, ta=True, name="ab_in_dw", out_dtypes=(GRAD_WIRE,))[None]
    dp0 = sent(dp0, {"ab_w_in": G["ab_w_in"][0]})
    dxn0 = _matmul(dp0, W["ab_w_in"], tb=True, name="ab_in_dx")
    dx, _, dmix_g0 = _rmsnorm_bwd(dxn0, h0, row(W["mix_norm_g"][0]), dh1, name="mix_norm_bwd_0")

    G["mix_norm_g"] = jnp.concatenate([dmix_g0, dmix_g1], axis=0)
    G["mlp_norm_g"] = jnp.concatenate([dmlp_g0, dmlp_g1], axis=0)
    G["mlp_w1"] = jnp.stack([dw1_0, dw1_1])
    G["mlp_w2"] = jnp.stack([dw2_0, dw2_1])
    G["final_norm_g"] = G["final_norm_g"].reshape(-1)
    G["sgu_norm_g"] = G["sgu_norm_g"].reshape(-1)
    G["sgu_norm_b"] = G["sgu_norm_b"].reshape(-1)
    return loss[0, 0], dx.reshape(B, T, D), G


_NAMES = ["mix_norm_g", "mlp_norm_g", "ab_w_in", "pool_w", "pool_scale", "conv_w", "conv_b", "ab_w_out",
          "cd_w_in", "sgu_norm_g", "sgu_norm_b", "sgu_w", "sgu_b", "cd_w_out", "mlp_w1", "mlp_w2",
          "final_norm_g"]
_COL_SHARDED = ["ab_w_in", "cd_w_in", "mlp_w1"]
_ROW_SHARDED = ["ab_w_out", "cd_w_out", "mlp_w2"]
_SMALL_SHARDED = ["conv_w", "sgu_norm_g", "sgu_norm_b"]
_REPLICATED = ["mix_norm_g", "mlp_norm_g", "pool_w", "pool_scale", "conv_b", "sgu_w", "sgu_b", "final_norm_g"]


def _pad_rows(a2d, mult=SUBLANES):
    pad = (-a2d.shape[0]) % mult
    return jnp.pad(a2d, ((0, pad), (0, 0))) if pad else a2d


def _pack(arrays):
    return _pad_rows(jnp.concatenate([a.reshape(-1, LANES) for a in arrays], axis=0))


def _unpack(packed, shapes):
    out, r = [], 0
    for s in shapes:
        n = math.prod(s) // LANES
        out.append(packed[r:r + n].reshape(s))
        r += n
    return out


def _small_shard_pack(arrays):
    rows = [jnp.pad(a.reshape(-1, a.shape[-1]), ((0, 0), (0, LANES - a.shape[-1]))) for a in arrays]
    return _pad_rows(jnp.concatenate(rows, axis=0))


def _cols_to_chunks(a):
    n = a.shape[-1] // N_DEV
    return jnp.moveaxis(a.reshape(a.shape[:-1] + (N_DEV, n)), -2, 0)


def _chunks_to_cols(a):
    t = jnp.moveaxis(a, 0, -2)
    return t.reshape(t.shape[:-2] + (t.shape[-2] * t.shape[-1],))


def _rows_to_chunks(a):
    r = a.shape[-2] // N_DEV
    return jnp.moveaxis(a.reshape(a.shape[:-2] + (N_DEV, r, a.shape[-1])), -3, 0)


def _chunks_to_rows(a):
    t = jnp.moveaxis(a, 0, -3)
    return t.reshape(t.shape[:-3] + (t.shape[-3] * t.shape[-2], t.shape[-1]))


def kernel(x, mix_norm_g, mlp_norm_g, ab_w_in, pool_w, pool_scale, conv_w, conv_b, ab_w_out, cd_w_in, sgu_norm_g, sgu_norm_b, sgu_w, sgu_b, cd_w_out, mlp_w1, mlp_w2, final_norm_g, loss_target, m_mix_norm_g, m_mlp_norm_g, m_ab_w_in, m_pool_w, m_pool_scale, m_conv_w, m_conv_b, m_ab_w_out, m_cd_w_in, m_sgu_norm_g, m_sgu_norm_b, m_sgu_w, m_sgu_b, m_cd_w_out, m_mlp_w1, m_mlp_w2, m_final_norm_g, v_mix_norm_g, v_mlp_norm_g, v_ab_w_in, v_pool_w, v_pool_scale, v_conv_w, v_conv_b, v_ab_w_out, v_cd_w_in, v_sgu_norm_g, v_sgu_norm_b, v_sgu_w, v_sgu_b, v_cd_w_out, v_mlp_w1, v_mlp_w2, v_final_norm_g):
    w = dict(zip(_NAMES, (mix_norm_g, mlp_norm_g, ab_w_in, pool_w, pool_scale, conv_w, conv_b, ab_w_out, cd_w_in,
                          sgu_norm_g, sgu_norm_b, sgu_w, sgu_b, cd_w_out, mlp_w1, mlp_w2, final_norm_g)))
    m = dict(zip(_NAMES, (m_mix_norm_g, m_mlp_norm_g, m_ab_w_in, m_pool_w, m_pool_scale, m_conv_w, m_conv_b,
                          m_ab_w_out, m_cd_w_in, m_sgu_norm_g, m_sgu_norm_b, m_sgu_w, m_sgu_b, m_cd_w_out,
                          m_mlp_w1, m_mlp_w2, m_final_norm_g)))
    v = dict(zip(_NAMES, (v_mix_norm_g, v_mlp_norm_g, v_ab_w_in, v_pool_w, v_pool_scale, v_conv_w, v_conv_b,
                          v_ab_w_out, v_cd_w_in, v_sgu_norm_g, v_sgu_norm_b, v_sgu_w, v_sgu_b, v_cd_w_out,
                          v_mlp_w1, v_mlp_w2, v_final_norm_g)))
    big = _COL_SHARDED + _ROW_SHARDED
    me = _my_index()

    small_sh = _small_shard_pack([w[n] for n in _SMALL_SHARDED])
    in_lead = {"ab_w_in": pool_w.shape[1] * pool_w.shape[2], "cd_w_in": 2 * sgu_norm_g.shape[-1] * N_DEV}
    shard = {"ab_w_in": ab_w_in[0], "ab_w_out": ab_w_out[0], "cd_w_in": cd_w_in[0], "cd_w_out": cd_w_out[0]}
    for layer in range(mlp_w1.shape[0]):
        shard[f"mlp_w1_{layer}"], shard[f"mlp_w2_{layer}"] = mlp_w1[layer], mlp_w2[layer]
    shard = {n: a.astype(BF16) for n, a in shard.items()}

    def whole(n, g):
        if n.endswith("_in") or n.startswith("mlp_w1"):
            full = _chunks_to_cols(g)
            return _group_in_cols(full, in_lead[n]) if n in in_lead else full
        return _chunks_to_rows(g)

    W = {"att_width": cd_w_out.shape[1] * N_DEV - sgu_norm_g.shape[-1] * N_DEV}
    later = [(["mlp_w2_0"], "mlp0_down"), (["cd_w_in", "cd_w_out"], "cd"), (["mlp_w1_1", "mlp_w2_1"], "mlp1")]

    def start(idx, carry, then=None):
        group, tag = later[idx]
        handles, carry = _gather_start([shard[n] for n in group], carry, name=f"gather_{tag}_start")

        def finish(got):
            done = {n: whole(n, g) for n, g in zip(group, got)}
            if then is not None:
                done[group[0]] = start(then, done[group[0]])
            return done

        W.update({n: _Lazy(handles, finish, f"gather_{tag}_wait") for n in group})
        return carry

    now = ["ab_w_in", "ab_w_out", "mlp_w1_0"]
    gathered = _gather_two_level([shard[n] for n in now] + [small_sh], name="gather_weights")
    W.update({n: whole(n, g) for n, g in zip(now, gathered)})
    W["ab_w_in"] = start(1, start(0, W["ab_w_in"], then=2))
    small_full = gathered[-1]
    r = 0
    for n in _SMALL_SHARDED:
        rows, width = math.prod(w[n].shape[:-1]), w[n].shape[-1]
        W[n] = _chunks_to_cols(small_full[:, r:r + rows, :width])
        r += rows
    for n in _REPLICATED:
        W[n] = w[n]
    for n in ("pool_w", "pool_scale", "sgu_w", "sgu_b"):
        W[n] = W[n][0]

    pending = []

    def emit(grads, carry):
        names = list(grads)
        parts = []
        for n in names:
            g = _ungroup_in_cols(grads[n], in_lead[n]) if n in in_lead else grads[n]
            parts.append(_cols_to_chunks(g) if n.endswith("_in") or n.startswith("mlp_w1") else _rows_to_chunks(g))
        handles, carry = _gather_start(parts, carry, name=f"grads_{names[0]}_start", scatter=True)
        pending.append((names, handles))
        return carry

    loss_part, grad_x, G = _local_step(x, loss_target, W, emit)

    landed = {}
    for names, handles in pending:
        got = _gather_wait(handles, grad_x, name=f"grads_{names[0]}_wait", scatter=True)
        landed.update(zip(names, got))
    small_names = _REPLICATED + _SMALL_SHARDED
    small_grads = [G[n].reshape(-1) for n in small_names]
    loss_row = jnp.full((LANES,), loss_part, F32)
    small_pack = _pack(small_grads + [loss_row])
    small_parts = _exchange([small_pack], [True], name="gather_small_grads")[0]

    grads, deltas, new_m, new_v = {}, {}, {}, {}
    for n in big:
        layers = [landed[f"{n}_{l}"] for l in range(w[n].shape[0])] if n.startswith("mlp") else [landed[n]]
        grads[n], deltas[n], new_m[n], new_v[n] = _adamw(w[n], layers, m[n], v[n], name=f"adamw_{n}")

    rep_shapes = [w[n].shape for n in _REPLICATED]
    rep_rows = sum(math.prod(s) for s in rep_shapes) // LANES
    small_sum_shapes = [(G[n].size,) for n in small_names] + [(LANES,)]
    zero_tail = [jnp.zeros((math.prod(s),), F32) for s in small_sum_shapes[len(_REPLICATED):]]
    w_pack = _pack([w[n] for n in _REPLICATED] + zero_tail)
    m_pack = _pack([m[n] for n in _REPLICATED] + zero_tail)
    v_pack = _pack([v[n] for n in _REPLICATED] + zero_tail)
    outs = [o[0] for o in _adamw(w_pack[None], [small_parts], m_pack[None], v_pack[None], name="adamw_small")]
    summed = _unpack(outs[0], small_sum_shapes)
    for i, n in enumerate(_REPLICATED):
        grads[n] = summed[i].reshape(w[n].shape)
    for dst, o in zip((deltas, new_m, new_v), outs[1:]):
        for n, val in zip(_REPLICATED, _unpack(o[:rep_rows], rep_shapes)):
            dst[n] = val
    loss = summed[-1][0]

    shard_g = []
    for i, n in enumerate(_SMALL_SHARDED):
        full = summed[len(_REPLICATED) + i].reshape(w[n].shape[:-1] + (-1,))
        width = w[n].shape[-1]
        shard_g.append(lax.dynamic_slice_in_dim(full, me * width, width, axis=full.ndim - 1))
    g_sh = _small_shard_pack(shard_g)
    m_sh = _small_shard_pack([m[n] for n in _SMALL_SHARDED])
    v_sh = _small_shard_pack([v[n] for n in _SMALL_SHARDED])
    outs = [o[0] for o in _adamw(small_sh[None], [g_sh[None]], m_sh[None], v_sh[None], name="adamw_small_sharded")]
    r = 0
    for n in _SMALL_SHARDED:
        rows, width = math.prod(w[n].shape[:-1]), w[n].shape[-1]
        for dst, o in zip((grads, deltas, new_m, new_v), outs):
            dst[n] = o[r:r + rows, :width].reshape(w[n].shape)
        r += rows

    return (loss, grad_x, *[grads[n] for n in _NAMES], *[deltas[n] for n in _NAMES],
            *[new_m[n] for n in _NAMES], *[new_v[n] for n in _NAMES])
```

```python
import functools
import math

import jax
import jax.numpy as jnp
from jax import lax
from jax.experimental import pallas as pl
from jax.experimental.pallas import tpu as pltpu

F32 = jnp.float32
BF16 = jnp.bfloat16
GRAD_WIRE = jnp.bfloat16

NORM_EPS = 1e-6
ADAM_LR = 0.001
ADAM_B1 = 0.9
ADAM_B2 = 0.999
ADAM_EPS = 1e-08
ADAM_WD = 0.01
ADAM_STEP = 10
ADAM_C1 = 1.0 - ADAM_B1 ** ADAM_STEP
ADAM_C2 = 1.0 - ADAM_B2 ** ADAM_STEP

N_DEV = 8
LANES = 128
SUBLANES = 8
SB_DH = 64
ATT_BLOCK = 256
POOL_LOG_WINDOWS = 4
VMEM_LIMIT = 56 * 1024 * 1024


def _params(semantics=None):
    return pltpu.CompilerParams(dimension_semantics=semantics, vmem_limit_bytes=VMEM_LIMIT)


def _tile(dim, pref, unit=LANES):
    if dim <= pref:
        return dim
    t = (pref // unit) * unit
    while t >= unit:
        if dim % t == 0:
            return t
        t -= unit
    return dim


def _matmul(a, b, *, name, ta=False, tb=False, extras=(), epilogue=None, out_dtypes=(F32,),
            tm=1024, tn=1024, tk=2048):
    M, K = (a.shape[1], a.shape[0]) if ta else a.shape
    N = b.shape[0] if tb else b.shape[1]
    assert (b.shape[1] if tb else b.shape[0]) == K, (a.shape, b.shape)
    tm, tn, tk = _tile(M, tm), _tile(N, tn), _tile(K, tk)
    nk = K // tk
    dims = (((0 if ta else 1,), (1 if tb else 0,)), ((), ()))
    ne, no = len(extras), len(out_dtypes)

    def body(*refs):
        a_ref, b_ref = refs[0], refs[1]
        e_refs = refs[2:2 + ne]
        o_refs = refs[2 + ne:2 + ne + no]
        k = pl.program_id(2)

        def part():
            return lax.dot_general(a_ref[...].astype(BF16), b_ref[...].astype(BF16), dims,
                                   preferred_element_type=F32)

        def finish(acc):
            outs = epilogue(acc, *[e[...] for e in e_refs]) if epilogue is not None else (acc,)
            for o_ref, val in zip(o_refs, outs):
                o_ref[...] = val.astype(o_ref.dtype)

        if nk == 1:
            finish(part())
        else:
            acc_ref = refs[-1]

            @pl.when(k == 0)
            def _():
                acc_ref[...] = jnp.zeros_like(acc_ref)

            acc_ref[...] += part()

            @pl.when(k == nk - 1)
            def _():
                finish(acc_ref[...])

    a_spec = (pl.BlockSpec((tk, tm), lambda i, j, k: (k, i)) if ta
              else pl.BlockSpec((tm, tk), lambda i, j, k: (i, k)))
    b_spec = (pl.BlockSpec((tn, tk), lambda i, j, k: (j, k)) if tb
              else pl.BlockSpec((tk, tn), lambda i, j, k: (k, j)))
    o_spec = pl.BlockSpec((tm, tn), lambda i, j, k: (i, j))
    outs = pl.pallas_call(
        body,
        name=name,
        grid=(M // tm, N // tn, nk),
        in_specs=[a_spec, b_spec] + [o_spec] * ne,
        out_specs=[o_spec] * no,
        out_shape=[jax.ShapeDtypeStruct((M, N), dt) for dt in out_dtypes],
        scratch_shapes=[pltpu.VMEM((tm, tn), F32)] if nk > 1 else [],
        compiler_params=_params(("parallel", "parallel", "arbitrary")),
    )(a, b, *extras)
    return outs[0] if no == 1 else outs


def _ep_add(acc, res):
    return (acc + res,)


def _ep_relu2(acc):
    r = jnp.maximum(acc, 0.0)
    return (r * r,)


def _ep_relu2_bwd(acc, act):
    return (acc * (2.0 * jnp.sqrt(act.astype(F32))),)


def _rstd(x):
    return lax.rsqrt(jnp.mean(x * x, axis=-1, keepdims=True) + NORM_EPS)


def _rmsnorm(h, g, *, name, tr=512):
    N, D = h.shape
    tr = _tile(N, tr, SUBLANES)

    def body(h_ref, g_ref, o_ref):
        x = h_ref[...]
        o_ref[...] = ((x * _rstd(x)) * g_ref[...]).astype(o_ref.dtype)

    row = pl.BlockSpec((tr, D), lambda i: (i, 0))
    vec = pl.BlockSpec((1, D), lambda i: (0, 0))
    return pl.pallas_call(
        body, name=name, grid=(N // tr,), in_specs=[row, vec], out_specs=row,
        out_shape=jax.ShapeDtypeStruct((N, D), BF16), compiler_params=_params(("parallel",)),
    )(h, g)


def _rmsnorm_bwd(dy, h, g, dres, *, name, tr=512):
    N, D = h.shape
    tr = _tile(N, tr, SUBLANES)

    def body(dy_ref, h_ref, g_ref, r_ref, dh_ref, dhb_ref, dg_ref):
        i = pl.program_id(0)
        x = h_ref[...]
        d = dy_ref[...]
        r = _rstd(x)
        xh = x * r

        @pl.when(i == 0)
        def _():
            dg_ref[...] = jnp.zeros_like(dg_ref)

        dg_ref[...] += jnp.sum(d * xh, axis=0, keepdims=True)
        dxh = d * g_ref[...]
        dh = r_ref[...] + r * (dxh - xh * jnp.mean(dxh * xh, axis=-1, keepdims=True))
        dh_ref[...] = dh
        dhb_ref[...] = dh.astype(dhb_ref.dtype)

    row = pl.BlockSpec((tr, D), lambda i: (i, 0))
    vec = pl.BlockSpec((1, D), lambda i: (0, 0))
    return pl.pallas_call(
        body, name=name, grid=(N // tr,), in_specs=[row, row, vec, row], out_specs=[row, row, vec],
        out_shape=[jax.ShapeDtypeStruct((N, D), F32), jax.ShapeDtypeStruct((N, D), BF16),
                   jax.ShapeDtypeStruct((1, D), F32)],
        compiler_params=_params(("arbitrary",)),
    )(dy, h, g, dres)


def _final_loss(h, g, target, *, name, tr=512):
    N, D = h.shape
    tr = _tile(N, tr, SUBLANES)

    def body(h_ref, g_ref, t_ref, loss_ref, dh_ref, dhb_ref, dg_ref):
        i = pl.program_id(0)
        x = h_ref[...]
        gg = g_ref[...]
        r = _rstd(x)
        xh = x * r
        err = xh * gg - t_ref[...]

        @pl.when(i == 0)
        def _():
            dg_ref[...] = jnp.zeros_like(dg_ref)
            loss_ref[...] = jnp.zeros_like(loss_ref)

        per_row = jnp.mean(err * err, axis=-1, keepdims=True)
        loss_ref[...] += 0.5 * jnp.sum(per_row, axis=0, keepdims=True)
        dy = err * (1.0 / D)
        dg_ref[...] += jnp.sum(dy * xh, axis=0, keepdims=True)
        dxh = dy * gg
        dh = r * (dxh - xh * jnp.mean(dxh * xh, axis=-1, keepdims=True))
        dh_ref[...] = dh
        dhb_ref[...] = dh.astype(dhb_ref.dtype)

    row = pl.BlockSpec((tr, D), lambda i: (i, 0))
    vec = pl.BlockSpec((1, D), lambda i: (0, 0))
    lvec = pl.BlockSpec((1, LANES), lambda i: (0, 0))
    return pl.pallas_call(
        body, name=name, grid=(N // tr,), in_specs=[row, vec, row], out_specs=[lvec, row, row, vec],
        out_shape=[jax.ShapeDtypeStruct((1, LANES), F32), jax.ShapeDtypeStruct((N, D), F32),
                   jax.ShapeDtypeStruct((N, D), BF16), jax.ShapeDtypeStruct((1, D), F32)],
        compiler_params=_params(("arbitrary",)),
    )(h, g, target)


def _shift_down(x, s):
    t = lax.broadcasted_iota(jnp.int32, x.shape, 0)
    return jnp.where(t >= s, pltpu.roll(x, s, 0), 0.0)


def _shift_up(x, s):
    n = x.shape[0]
    t = lax.broadcasted_iota(jnp.int32, x.shape, 0)
    return jnp.where(t < n - s, pltpu.roll(x, n - s, 0), 0.0)


def _window_sum(x, g, shift):
    s = x + shift(x, 1)
    for k in range(1, POOL_LOG_WINDOWS):
        s = jnp.where(k <= g, s + shift(s, 2 ** k), s)
    return s


def _pool_count(shape, g):
    t = lax.broadcasted_iota(jnp.int32, shape, 0)
    return jnp.minimum(t + 1, lax.shift_left(jnp.int32(2), g)).astype(F32)


def _pool_fwd(p, pool_w, pool_scale, B, T, off, width, *, name):
    G, dh = pool_w.shape[0], pool_w.shape[1]
    assert G == POOL_LOG_WINDOWS and off % dh == 0
    base = off // dh

    def body(a_ref, w_ref, s_ref, o_ref):
        g = pl.program_id(0)
        a = a_ref[...]
        pooled = _window_sum(a, g, _shift_down) / _pool_count(a.shape, g) - a
        m = jnp.dot(pooled.astype(BF16), w_ref[0].astype(BF16), preferred_element_type=F32)
        o_ref[...] = (m * s_ref[0]).astype(o_ref.dtype)

    return pl.pallas_call(
        body, name=name, grid=(G, B),
        in_specs=[pl.BlockSpec((T, dh), lambda g, b: (b, base + g)),
                  pl.BlockSpec((1, dh, dh), lambda g, b: (g, 0, 0)),
                  pl.BlockSpec((1, 1, dh), lambda g, b: (g, 0, 0))],
        out_specs=pl.BlockSpec((T, dh), lambda g, b: (b, g)),
        out_shape=jax.ShapeDtypeStruct((B * T, width), BF16),
        compiler_params=_params(("parallel", "parallel")),
    )(p, pool_w, pool_scale.reshape(G, 1, dh))


def _pool_bwd(p, dcat, pool_w, pool_scale, B, T, off, *, name):
    G, dh = pool_w.shape[0], pool_w.shape[1]
    base = off // dh

    def body(a_ref, d_ref, w_ref, s_ref, da_ref, dw_ref, ds_ref):
        g = pl.program_id(0)
        b = pl.program_id(1)
        a = a_ref[...]
        d = d_ref[...]
        cnt = _pool_count(a.shape, g)
        pooled = (_window_sum(a, g, _shift_down) / cnt - a).astype(BF16)
        w = w_ref[0].astype(BF16)
        m = jnp.dot(pooled, w, preferred_element_type=F32)

        @pl.when(b == 0)
        def _():
            dw_ref[...] = jnp.zeros_like(dw_ref)
            ds_ref[...] = jnp.zeros_like(ds_ref)

        ds_ref[0] += jnp.sum(d * m, axis=0, keepdims=True)
        dm = (d * s_ref[0]).astype(BF16)
        dw_ref[0] += lax.dot_general(pooled, dm, (((0,), (0,)), ((), ())), preferred_element_type=F32)
        dpooled = lax.dot_general(dm, w, (((1,), (1,)), ((), ())), preferred_element_type=F32)
        da = _window_sum(dpooled / cnt, g, _shift_up) - dpooled
        da_ref[...] = da.astype(da_ref.dtype)

    pblk = pl.BlockSpec((T, dh), lambda g, b: (b, base + g))
    dblk = pl.BlockSpec((T, dh), lambda g, b: (b, g))
    wspec = pl.BlockSpec((1, dh, dh), lambda g, b: (g, 0, 0))
    sspec = pl.BlockSpec((1, 1, dh), lambda g, b: (g, 0, 0))
    return pl.pallas_call(
        body, name=name, grid=(G, B), in_specs=[pblk, dblk, wspec, sspec], out_specs=[pblk, wspec, sspec],
        out_shape=[jax.ShapeDtypeStruct((B * T, p.shape[1]), BF16), jax.ShapeDtypeStruct((G, dh, dh), F32),
                   jax.ShapeDtypeStruct((G, 1, dh), F32)],
        compiler_params=_params(("parallel", "arbitrary")),
    )(p, dcat, pool_w, pool_scale.reshape(G, 1, dh))


_ANY = pl.BlockSpec(memory_space=pl.ANY)


def _conv_fwd(p, cat, conv_w, conv_b, B, T, coff, *, name):
    CW = conv_w.shape[1]
    tc = LANES
    assert coff % tc == 0 and CW % tc == 0
    cbase = coff // tc

    def body(p_ref, cat_ref, w_ref, b_ref, o_ref):
        xb, gb, gc = p_ref[:, 0:tc], p_ref[:, tc:2 * tc], p_ref[:, 2 * tc:3 * tc]
        c = gc * xb
        w = w_ref[...]
        y = _shift_down(c, 2) * w[0:1] + _shift_down(c, 1) * w[1:2] + c * w[2:3] + b_ref[...]
        o_ref[...] = (gb * y).astype(o_ref.dtype)

    return pl.pallas_call(
        body, name=name, grid=(CW // tc, B),
        in_specs=[pl.BlockSpec((T, 3 * tc), lambda j, b: (b, j)), _ANY,
                  pl.BlockSpec((3, tc), lambda j, b: (0, j)), pl.BlockSpec((1, tc), lambda j, b: (0, j))],
        out_specs=pl.BlockSpec((T, tc), lambda j, b: (b, cbase + j)),
        out_shape=jax.ShapeDtypeStruct(cat.shape, cat.dtype), input_output_aliases={1: 0},
        compiler_params=_params(("parallel", "parallel")),
    )(p, cat, conv_w, conv_b)


def _conv_bwd(p, dcat, dp, conv_w, conv_b, B, T, coff, *, name):
    CW = conv_w.shape[1]
    tc = LANES
    assert coff % tc == 0
    cbase = coff // tc

    def body(p_ref, d_ref, dp_in_ref, w_ref, b_ref, dp_ref, dw_ref, db_ref):
        b = pl.program_id(1)
        xb, gb, gc = p_ref[:, 0:tc], p_ref[:, tc:2 * tc], p_ref[:, 2 * tc:3 * tc]
        d = d_ref[...]
        w = w_ref[...]
        c = gc * xb
        c1 = _shift_down(c, 1)
        c2 = _shift_down(c, 2)
        y = c2 * w[0:1] + c1 * w[1:2] + c * w[2:3] + b_ref[...]
        dy = d * gb
        dp_ref[:, tc:2 * tc] = (d * y).astype(dp_ref.dtype)

        @pl.when(b == 0)
        def _():
            dw_ref[...] = jnp.zeros_like(dw_ref)
            db_ref[...] = jnp.zeros_like(db_ref)

        db_ref[...] += jnp.sum(dy, axis=0, keepdims=True)
        dw_ref[0:1, :] += jnp.sum(dy * c2, axis=0, keepdims=True)
        dw_ref[1:2, :] += jnp.sum(dy * c1, axis=0, keepdims=True)
        dw_ref[2:3, :] += jnp.sum(dy * c, axis=0, keepdims=True)
        dc = dy * w[2:3] + _shift_up(dy, 1) * w[1:2] + _shift_up(dy, 2) * w[0:1]
        dp_ref[:, 2 * tc:3 * tc] = (dc * xb).astype(dp_ref.dtype)
        dp_ref[:, 0:tc] = (dc * gc).astype(dp_ref.dtype)

    wspec = pl.BlockSpec((3, tc), lambda j, b: (0, j))
    bspec = pl.BlockSpec((1, tc), lambda j, b: (0, j))
    pblk = pl.BlockSpec((T, 3 * tc), lambda j, b: (b, j))
    return pl.pallas_call(
        body, name=name, grid=(CW // tc, B),
        in_specs=[pblk, pl.BlockSpec((T, tc), lambda j, b: (b, cbase + j)), _ANY, wspec, bspec],
        out_specs=[pblk, wspec, bspec],
        out_shape=[jax.ShapeDtypeStruct(dp.shape, dp.dtype), jax.ShapeDtypeStruct((3, CW), F32),
                   jax.ShapeDtypeStruct((1, CW), F32)],
        input_output_aliases={2: 0},
        compiler_params=_params(("parallel", "arbitrary")),
    )(p, dcat, dp, conv_w, conv_b)


_SQRT_HALF = 0.7071067811865476
_INV_SQRT_2PI = 0.3989422804014327


def _gelu(x):
    return x * (lax.erf(x * _SQRT_HALF) + 1.0) * 0.5


def _gelu_grad(x):
    return 0.5 * (lax.erf(x * _SQRT_HALF) + 1.0) + x * (_INV_SQRT_2PI * jnp.exp(-0.5 * x * x))


def _layernorm_parts(v):
    mu = jnp.mean(v, axis=-1, keepdims=True)
    vc = v - mu
    rstd = lax.rsqrt(jnp.mean(vc * vc, axis=-1, keepdims=True) + NORM_EPS)
    return vc * rstd, rstd


def _tril_mask(L):
    r = lax.broadcasted_iota(jnp.int32, (L, L), 0)
    c = lax.broadcasted_iota(jnp.int32, (L, L), 1)
    return r >= c


def _sgu_fwd(p, ln_g, ln_b, sgu_w, sgu_b, off, width, *, name, tr=512):
    N = p.shape[0]
    G, L = sgu_w.shape[0], sgu_w.shape[1]
    SW = ln_g.shape[1]
    dh = SW // G
    assert off % SW == 0
    ub = off // SW
    tr = _tile(N, tr, L)
    assert tr % L == 0

    def body(u_ref, v_ref, g_ref, beta_ref, w_ref, b_ref, o_ref):
        u = _gelu(u_ref[...])
        vhat, _ = _layernorm_parts(_gelu(v_ref[...]))
        vn = (vhat * g_ref[...] + beta_ref[...]).astype(BF16)
        mask = _tril_mask(L)
        for gi in range(G):
            w = jnp.where(mask, w_ref[gi], 0.0).astype(BF16)
            bias = b_ref[gi]
            cols = slice(gi * dh, (gi + 1) * dh)
            for n in range(tr // L):
                rows = slice(n * L, (n + 1) * L)
                s = jnp.dot(w, vn[rows, cols], preferred_element_type=F32) + bias
                o_ref[rows, cols] = (u[rows, cols] * s).astype(o_ref.dtype)

    def col(k):
        return pl.BlockSpec((tr, SW), lambda i: (i, k))

    vec = pl.BlockSpec((1, SW), lambda i: (0, 0))
    return pl.pallas_call(
        body, name=name, grid=(N // tr,),
        in_specs=[col(ub), col(ub + 1), vec, vec, pl.BlockSpec((G, L, L), lambda i: (0, 0, 0)),
                  pl.BlockSpec((G, L, 1), lambda i: (0, 0, 0))],
        out_specs=col(0), out_shape=jax.ShapeDtypeStruct((N, width), BF16),
        compiler_params=_params(("parallel",)),
    )(p, p, ln_g, ln_b, sgu_w, sgu_b.reshape(G, L, 1))


def _sgu_bwd(p, dcat, ln_g, ln_b, sgu_w, sgu_b, off, *, name, tr=512):
    N = p.shape[0]
    G, L = sgu_w.shape[0], sgu_w.shape[1]
    SW = ln_g.shape[1]
    dh = SW // G
    assert off % SW == 0
    ub = off // SW
    tr = _tile(N, tr, L)

    def compute(i, u_ref, v_ref, dc_ref, g_ref, beta_ref, w_ref, b_ref,
                du_ref, dv_ref, dw_ref, db_ref, dg_ref, dbeta_ref, du_s, dvn_s):
        pu = u_ref[...]
        pv = v_ref[...]
        u = _gelu(pu)
        vhat, rstd = _layernorm_parts(_gelu(pv))
        gg = g_ref[...]
        vn = (vhat * gg + beta_ref[...]).astype(BF16)
        dc = dc_ref[...]
        mask = _tril_mask(L)

        @pl.when(i == 0)
        def _():
            dw_ref[...] = jnp.zeros_like(dw_ref)
            db_ref[...] = jnp.zeros_like(db_ref)
            dg_ref[...] = jnp.zeros_like(dg_ref)
            dbeta_ref[...] = jnp.zeros_like(dbeta_ref)

        for gi in range(G):
            w = jnp.where(mask, w_ref[gi], 0.0).astype(BF16)
            bias = b_ref[gi]
            cols = slice(gi * dh, (gi + 1) * dh)
            dw_acc = jnp.zeros((L, L), F32)
            db_acc = jnp.zeros((L, 1), F32)
            for n in range(tr // L):
                rows = slice(n * L, (n + 1) * L)
                vb = vn[rows, cols]
                s = jnp.dot(w, vb, preferred_element_type=F32) + bias
                du_s[rows, cols] = dc[rows, cols] * s
                ds = dc[rows, cols] * u[rows, cols]
                db_acc += jnp.sum(ds, axis=1, keepdims=True)
                dsb = ds.astype(BF16)
                dw_acc += lax.dot_general(dsb, vb, (((1,), (1,)), ((), ())), preferred_element_type=F32)
                dvn_s[rows, cols] = lax.dot_general(w, dsb, (((0,), (0,)), ((), ())),
                                                    preferred_element_type=F32)
            dw_ref[gi] += jnp.where(mask, dw_acc, 0.0)
            db_ref[gi] += db_acc

        dvn = dvn_s[...]
        dg_ref[...] += jnp.sum(dvn * vhat, axis=0, keepdims=True)
        dbeta_ref[...] += jnp.sum(dvn, axis=0, keepdims=True)
        dvh = dvn * gg
        dv = rstd * (dvh - jnp.mean(dvh, axis=-1, keepdims=True)
                     - vhat * jnp.mean(dvh * vhat, axis=-1, keepdims=True))
        dv_ref[...] = (dv * _gelu_grad(pv)).astype(dv_ref.dtype)
        du_ref[...] = (du_s[...] * _gelu_grad(pu)).astype(du_ref.dtype)

    def body(u_ref, v_ref, dc_ref, g_ref, beta_ref, w_ref, b_ref,
             dp_ref, dw_ref, db_ref, dg_ref, dbeta_ref, du_s, dvn_s, dv_s):
        i = pl.program_id(0)
        half = pl.program_id(1)

        @pl.when(half == 0)
        def _():
            compute(i, u_ref, v_ref, dc_ref, g_ref, beta_ref, w_ref, b_ref,
                    dp_ref, dv_s, dw_ref, db_ref, dg_ref, dbeta_ref, du_s, dvn_s)

        @pl.when(half == 1)
        def _():
            dp_ref[...] = dv_s[...]

    def col(k):
        return pl.BlockSpec((tr, SW), lambda i, half: (i, k))

    vec = pl.BlockSpec((1, SW), lambda i, half: (0, 0))
    wspec = pl.BlockSpec((G, L, L), lambda i, half: (0, 0, 0))
    bspec = pl.BlockSpec((G, L, 1), lambda i, half: (0, 0, 0))
    return pl.pallas_call(
        body, name=name, grid=(N // tr, 2),
        in_specs=[col(ub), col(ub + 1), col(0), vec, vec, wspec, bspec],
        out_specs=[pl.BlockSpec((tr, SW), lambda i, half: (i, ub + half)), wspec, bspec, vec, vec],
        out_shape=[jax.ShapeDtypeStruct((N, p.shape[1]), BF16),
                   jax.ShapeDtypeStruct((G, L, L), F32), jax.ShapeDtypeStruct((G, L, 1), F32),
                   jax.ShapeDtypeStruct((1, SW), F32), jax.ShapeDtypeStruct((1, SW), F32)],
        scratch_shapes=[pltpu.VMEM((tr, SW), F32), pltpu.VMEM((tr, SW), F32), pltpu.VMEM((tr, SW), BF16)],
        compiler_params=_params(("arbitrary", "arbitrary")),
    )(p, p, dcat, ln_g, ln_b, sgu_w, sgu_b.reshape(G, L, 1))


def _log_sigmoid_pair(z):
    ls = jnp.minimum(z, 0.0) - jnp.log(1.0 + jnp.exp(-jnp.abs(z)))
    return ls, ls - z


def _split_dot(x, m):
    hi = x.astype(BF16)
    lo = (x - hi.astype(F32)).astype(BF16)
    return (jnp.dot(hi, m, preferred_element_type=F32) + jnp.dot(lo, m, preferred_element_type=F32))


def _stacked_cumsum(xs, m):
    his = [x.astype(BF16) for x in xs]
    los = [(x - h.astype(F32)).astype(BF16) for x, h in zip(xs, his)]
    n, rows = len(xs), xs[0].shape[0]
    s = jnp.dot(jnp.concatenate(his + los, axis=0), m, preferred_element_type=F32)
    return [s[i * rows:(i + 1) * rows] + s[(n + i) * rows:(n + i + 1) * rows] for i in range(n)]


def _att_logits(qts, kbs, strict):
    ls, lks = [], []
    for qt, kb in zip(qts, kbs):
        l, lk = _log_sigmoid_pair(lax.dot_general(qt, kb, (((1,), (1,)), ((), ())), preferred_element_type=F32))
        ls.append(l)
        lks.append(lk if strict is None else jnp.where(strict, lk, 0.0))
    return ls, lks


def _att_tiles(TB):
    r = lax.broadcasted_iota(jnp.int32, (TB, TB), 0)
    c = lax.broadcasted_iota(jnp.int32, (TB, TB), 1)
    return r, c


def _att_weights(qt, kb, strict, later, carry):
    z = lax.dot_general(qt, kb, (((1,), (1,)), ((), ())), preferred_element_type=F32)
    ls, lk = _log_sigmoid_pair(z)
    if strict is not None:
        lk = jnp.where(strict, lk, 0.0)
    suffix = _split_dot(lk, later) + carry
    a = jnp.exp(ls + suffix)
    if strict is not None:
        a = jnp.where(strict, a, 0.0)
    return ls, lk, suffix, a


HEADS_PER_BLOCK = LANES // SB_DH
ATT_GROUPS = 2
HP = ATT_GROUPS * HEADS_PER_BLOCK
ATT_SCALE = 1.0 / math.sqrt(SB_DH)


def _head_col(hh, part):
    return (hh // HEADS_PER_BLOCK) * 3 * LANES + part * LANES + (hh % HEADS_PER_BLOCK) * SB_DH


def _stage_heads(src_ref, part, dst_ref, T, scale=None):
    rows = _tile(T, 256, SUBLANES)

    def chunk(n, _):
        r0 = pl.multiple_of(n * rows, rows)
        for hh in range(HP):
            col = hh * SB_DH if part is None else _head_col(hh, part)
            x = src_ref[pl.ds(r0, rows), col:col + SB_DH]
            if scale is not None:
                x = x * scale
            dst_ref[hh, pl.ds(r0, rows), :] = x.astype(dst_ref.dtype)
        return 0

    lax.fori_loop(0, T // rows, chunk, 0)


def _attn_fwd(p, cat, B, T, coff, *, name):
    dh = SB_DH
    owidth = ATT_GROUPS * LANES
    nsteps = (cat.shape[1] - coff) // owidth
    TB = _tile(T, ATT_BLOCK)
    nb = T // TB
    assert nb <= LANES and coff % owidth == 0 and (cat.shape[1] - coff) % owidth == 0
    cbase = coff // owidth

    def body(p_ref, cat_ref, o_ref, c_ref, q_ref, k_ref, v_ref):
        _stage_heads(p_ref, 0, q_ref, T, ATT_SCALE)
        _stage_heads(p_ref, 1, k_ref, T)
        _stage_heads(p_ref, 2, v_ref, T)
        r, c = _att_tiles(TB)
        strict = c < r
        later = (r > c).astype(BF16)
        lane = lax.broadcasted_iota(jnp.int32, (TB, LANES), 1)

        def tiles(qts, j, carries, mask):
            k0 = pl.multiple_of(j * TB, TB)
            ls, lks = _att_logits(qts, [k_ref[hh, pl.ds(k0, TB), :] for hh in range(HP)], mask)
            sums = _stacked_cumsum(lks, later)
            out = []
            for hh in range(HP):
                suffix = sums[hh] + carries[hh]
                a = jnp.exp(ls[hh] + suffix)
                if mask is not None:
                    a = jnp.where(mask, a, 0.0)
                pv = jnp.dot(a.astype(BF16), v_ref[hh, pl.ds(k0, TB), :], preferred_element_type=F32)
                out.append((pv, suffix[:, 0:1] + lks[hh][:, 0:1]))
            return out

        def qblock(i, _):
            q0 = pl.multiple_of(i * TB, TB)
            qts = [q_ref[hh, pl.ds(q0, TB), :] for hh in range(HP)]
            state = []
            for pv, carry in tiles(qts, i, [jnp.zeros((TB, 1), F32)] * HP, strict):
                state += [pv, carry, jnp.zeros((TB, LANES), F32)]

            def kblock(jj, st):
                j = i - jj
                out = []
                for hh, (pv, new_carry) in enumerate(tiles(qts, j, st[1::3], None)):
                    acc, carry, cm = st[3 * hh:3 * hh + 3]
                    out += [acc + pv, new_carry, jnp.where(lane == j, carry, cm)]
                return tuple(out)

            st = lax.fori_loop(1, i + 1, kblock, tuple(state))
            for hh in range(HP):
                o_ref[pl.ds(q0, TB), hh * dh:(hh + 1) * dh] = st[3 * hh].astype(o_ref.dtype)
                c_ref[hh, pl.ds(q0, TB), :] = st[3 * hh + 2]
            return 0

        lax.fori_loop(0, nb, qblock, 0)

    staged = pltpu.VMEM((HP, T, dh), BF16)
    return pl.pallas_call(
        body, name=name, grid=(B, nsteps),
        in_specs=[pl.BlockSpec((T, 3 * owidth), lambda b, s: (b, s)), _ANY],
        out_specs=[pl.BlockSpec((T, owidth), lambda b, s: (b, cbase + s)),
                   pl.BlockSpec((HP, T, LANES), lambda b, s: (b * nsteps + s, 0, 0))],
        out_shape=[jax.ShapeDtypeStruct(cat.shape, cat.dtype),
                   jax.ShapeDtypeStruct((B * nsteps * HP, T, LANES), F32)],
        input_output_aliases={1: 0}, scratch_shapes=[staged, staged, staged],
        compiler_params=_params(("parallel", "parallel")),
    )(p, cat)


def _attn_bwd(p, dcat, carries, dp, B, T, coff, *, name):
    dh = SB_DH
    owidth = ATT_GROUPS * LANES
    nsteps = (dcat.shape[1] - coff) // owidth
    TB = _tile(T, ATT_BLOCK)
    nb = T // TB
    cbase = coff // owidth

    def body(p_ref, d_ref, c_ref, dp_in_ref, dp_ref, q_ref, k_ref, v_ref, do_ref, dk_ref, dv_ref):
        _stage_heads(p_ref, 0, q_ref, T, ATT_SCALE)
        _stage_heads(p_ref, 1, k_ref, T)
        _stage_heads(p_ref, 2, v_ref, T)
        _stage_heads(d_ref, None, do_ref, T)
        r, c = _att_tiles(TB)
        strict = c < r
        later = (r > c).astype(BF16)
        earlier = (r < c).astype(BF16)
        lane = lax.broadcasted_iota(jnp.int32, (TB, LANES), 1)
        dk_ref[...] = jnp.zeros_like(dk_ref)
        dv_ref[...] = jnp.zeros_like(dv_ref)

        def tiles(qts, dots, cms, j, befores, mask):
            k0 = pl.multiple_of(j * TB, TB)
            kbs = [k_ref[hh, pl.ds(k0, TB), :] for hh in range(HP)]
            ls, lks = _att_logits(qts, kbs, mask)
            sums = _stacked_cumsum(lks, later)
            gls = []
            for hh in range(HP):
                carry = jnp.sum(jnp.where(lane == j, cms[hh], 0.0), axis=1, keepdims=True)
                a = jnp.exp(ls[hh] + sums[hh] + carry)
                if mask is not None:
                    a = jnp.where(mask, a, 0.0)
                dv_ref[hh, pl.ds(k0, TB), :] += lax.dot_general(a.astype(BF16), dots[hh], (((0,), (0,)), ((), ())),
                                                                preferred_element_type=F32)
                da = lax.dot_general(dots[hh], v_ref[hh, pl.ds(k0, TB), :], (((1,), (1,)), ((), ())),
                                     preferred_element_type=F32)
                gls.append(a * da)
            pres = _stacked_cumsum(gls, earlier)
            out = []
            for hh in range(HP):
                prefix = pres[hh] + befores[hh]
                dz = gls[hh] - jnp.exp(ls[hh]) * (gls[hh] + prefix)
                if mask is not None:
                    dz = jnp.where(mask, dz, 0.0)
                dzb = dz.astype(BF16)
                dk_ref[hh, pl.ds(k0, TB), :] += lax.dot_general(dzb, qts[hh], (((0,), (0,)), ((), ())),
                                                                preferred_element_type=F32)
                out.append((jnp.dot(dzb, kbs[hh], preferred_element_type=F32),
                            prefix[:, TB - 1:TB] + gls[hh][:, TB - 1:TB]))
            return out

        def qblock(i, _):
            q0 = pl.multiple_of(i * TB, TB)
            qts = [q_ref[hh, pl.ds(q0, TB), :] for hh in range(HP)]
            dots = [do_ref[hh, pl.ds(q0, TB), :] for hh in range(HP)]
            cms = [c_ref[hh, pl.ds(q0, TB), :] for hh in range(HP)]

            def kblock(j, st):
                out = []
                for hh, (part, new_before) in enumerate(tiles(qts, dots, cms, j, st[1::2], None)):
                    out += [st[2 * hh] + part, new_before]
                return tuple(out)

            st = lax.fori_loop(0, i, kblock, (jnp.zeros((TB, dh), F32), jnp.zeros((TB, 1), F32)) * HP)
            for hh, (part, _) in enumerate(tiles(qts, dots, cms, i, st[1::2], strict)):
                dq = (st[2 * hh] + part) * ATT_SCALE
                dp_ref[pl.ds(q0, TB), _head_col(hh, 0):_head_col(hh, 0) + dh] = dq.astype(dp_ref.dtype)
            return 0

        lax.fori_loop(0, nb, qblock, 0)

        def write_back(n, _):
            r0 = pl.multiple_of(n * TB, TB)
            for hh in range(HP):
                dp_ref[pl.ds(r0, TB), _head_col(hh, 1):_head_col(hh, 1) + dh] = (
                    dk_ref[hh, pl.ds(r0, TB), :].astype(dp_ref.dtype))
                dp_ref[pl.ds(r0, TB), _head_col(hh, 2):_head_col(hh, 2) + dh] = (
                    dv_ref[hh, pl.ds(r0, TB), :].astype(dp_ref.dtype))
            return 0

        lax.fori_loop(0, nb, write_back, 0)

    pblk = pl.BlockSpec((T, 3 * owidth), lambda b, s: (b, s))
    staged = pltpu.VMEM((HP, T, dh), BF16)
    accum = pltpu.VMEM((HP, T, dh), F32)
    return pl.pallas_call(
        body, name=name, grid=(B, nsteps),
        in_specs=[pblk, pl.BlockSpec((T, owidth), lambda b, s: (b, cbase + s)),
                  pl.BlockSpec((HP, T, LANES), lambda b, s: (b * nsteps + s, 0, 0)), _ANY],
        out_specs=pblk, out_shape=jax.ShapeDtypeStruct(dp.shape, dp.dtype), input_output_aliases={3: 0},
        scratch_shapes=[staged, staged, staged, staged, accum, accum],
        compiler_params=_params(("parallel", "parallel")),
    )(p, dcat, carries, dp)


def _adamw(w, gparts, m, v, *, name):
    L, R, C = w.shape
    P = gparts[0].shape[0]
    assert len(gparts) == L
    tr = _tile(R, max(SUBLANES, (1 << 19) // (C * P)), SUBLANES)

    def body(*refs):
        w_ref, g_refs, (m_ref, v_ref) = refs[0], refs[1:1 + L], refs[1 + L:3 + L]
        go_ref, d_ref, mo_ref, vo_ref = refs[3 + L:]
        layer = pl.program_id(0)

        def update(g_ref):
            g = g_ref[0].astype(F32)
            for i in range(1, P):
                g = g + g_ref[i].astype(F32)
            m2 = ADAM_B1 * m_ref[...] + (1.0 - ADAM_B1) * g
            v2 = ADAM_B2 * v_ref[...] + (1.0 - ADAM_B2) * (g * g)
            m_hat = m2 / ADAM_C1
            v_hat = v2 / ADAM_C2
            go_ref[...] = g
            d_ref[...] = -ADAM_LR * (m_hat / (jnp.sqrt(v_hat) + ADAM_EPS) + ADAM_WD * w_ref[...])
            mo_ref[...] = m2
            vo_ref[...] = v2

        for l in range(L):
            pl.when(layer == l)(functools.partial(update, g_refs[l]))

    row = pl.BlockSpec((None, tr, C), lambda l, i: (l, i, 0))

    def part(mine):
        return pl.BlockSpec((P, tr, C), lambda l, i: (0, jnp.where(l == mine, i, 0), 0))

    shp = jax.ShapeDtypeStruct((L, R, C), F32)
    return pl.pallas_call(
        body, name=name, grid=(L, R // tr),
        in_specs=[row] + [part(l) for l in range(L)] + [row, row],
        out_specs=[row] * 4, out_shape=[shp] * 4, compiler_params=_params(("arbitrary", "arbitrary")),
    )(w, *gparts, m, v)


def _my_index():
    return 4 * lax.axis_index("x") + 2 * lax.axis_index("y") + lax.axis_index("c")


def _exchange(arrs, gather, *, name):
    n = len(arrs)

    def body(*refs):
        ins, outs = refs[:n], refs[n:2 * n]
        send_sems, recv_sems, local_sems = refs[2 * n:]
        x, y, c = lax.axis_index("x"), lax.axis_index("y"), lax.axis_index("c")
        me = 4 * x + 2 * y + c
        remote, local = [], []
        for a in range(n):
            own = ins[a] if gather[a] else ins[a].at[me]
            cp = pltpu.make_async_copy(own, outs[a].at[me], local_sems.at[a])
            cp.start()
            local.append(cp)
            for k in range(1, N_DEV):
                px = 1 - x if k & 4 else x
                py = 1 - y if k & 2 else y
                pc = 1 - c if k & 1 else c
                src = ins[a] if gather[a] else ins[a].at[4 * px + 2 * py + pc]
                cp = pltpu.make_async_remote_copy(
                    src_ref=src, dst_ref=outs[a].at[me],
                    send_sem=send_sems.at[a, k - 1], recv_sem=recv_sems.at[a, k - 1],
                    device_id=(px, py, pc), device_id_type=pl.DeviceIdType.MESH)
                cp.start()
                remote.append(cp)
        for cp in remote:
            cp.wait()
        for cp in local:
            cp.wait()

    hbm = pl.BlockSpec(memory_space=pltpu.HBM)
    out_shape = [jax.ShapeDtypeStruct(((N_DEV,) + a.shape) if g else a.shape, a.dtype)
                 for a, g in zip(arrs, gather)]
    return pl.pallas_call(
        body, name=name, in_specs=[hbm] * n, out_specs=[hbm] * n, out_shape=out_shape,
        scratch_shapes=[pltpu.SemaphoreType.DMA((n, N_DEV - 1)), pltpu.SemaphoreType.DMA((n, N_DEV - 1)),
                        pltpu.SemaphoreType.DMA((n,))],
    )(*arrs)


_HBM = pl.BlockSpec(memory_space=pltpu.HBM)


def _other_chips(x, y):
    return [(1 - x, y), (x, 1 - y), (1 - x, 1 - y)]


def _gather_two_level(arrs, *, name):
    n = len(arrs)

    def body(*refs):
        ins, outs = refs[:n], refs[n:2 * n]
        send_sems, recv_sems, local_sems = refs[2 * n:]
        x, y, c = lax.axis_index("x"), lax.axis_index("y"), lax.axis_index("c")
        me, sibling = (x, y, c), (x, y, 1 - c)
        chips = _other_chips(x, y)

        def slot(a, px, py, pc):
            return outs[a].at[4 * px + 2 * py + pc]

        def copy(a, k, block, to, src=None):
            return pltpu.make_async_remote_copy(
                src_ref=slot(a, *block) if src is None else src, dst_ref=slot(a, *block),
                send_sem=send_sems.at[a, k], recv_sem=recv_sems.at[a, k],
                device_id=to, device_id_type=pl.DeviceIdType.MESH)

        local, sends = [], []
        for a in range(n):
            cp = pltpu.make_async_copy(ins[a], slot(a, *me), local_sems.at[a])
            cp.start()
            local.append(cp)
            first = [copy(a, 0, me, sibling, src=ins[a])]
            first += [copy(a, 1 + j, me, (*chip, c), src=ins[a]) for j, chip in enumerate(chips)]
            for cp in first:
                cp.start()
            sends += first
        for j, chip in enumerate(chips):
            for a in range(n):
                copy(a, 1 + j, (*chip, c), me).wait_recv()
                cp = copy(a, 4 + j, (*chip, c), sibling)
                cp.start()
                sends.append(cp)
        for a in range(n):
            copy(a, 0, sibling, me).wait_recv()
            for j, chip in enumerate(chips):
                copy(a, 4 + j, (*chip, 1 - c), me).wait_recv()
        for cp in sends:
            cp.wait_send()
        for cp in local:
            cp.wait()

    return pl.pallas_call(
        body, name=name, in_specs=[_HBM] * n, out_specs=[_HBM] * n,
        out_shape=[jax.ShapeDtypeStruct((N_DEV,) + a.shape, a.dtype) for a in arrs],
        scratch_shapes=[pltpu.SemaphoreType.DMA((n, N_DEV - 1)), pltpu.SemaphoreType.DMA((n, N_DEV - 1)),
                        pltpu.SemaphoreType.DMA((n,))],
    )(*arrs)


_SEM = pl.BlockSpec(memory_space=pltpu.SEMAPHORE)
_SPLIT_COPY = pltpu.SideEffectType.DATAFLOW_SIDE_EFFECTING


def _peers(x, y, c):
    return [((1 - x if k & 4 else x), (1 - y if k & 2 else y), (1 - c if k & 1 else c)) for k in range(1, N_DEV)]


_SPLIT_SEMS = 2 * (N_DEV - 1) + 1


def _split_sems(sems, a):
    mine = sems[a * _SPLIT_SEMS:(a + 1) * _SPLIT_SEMS]
    return mine[:N_DEV - 1], mine[N_DEV - 1:2 * (N_DEV - 1)], mine[-1]


def _split_src(ref, scatter, index):
    return ref.at[index] if scatter else ref


def _gather_start(arrs, carry, *, name, scatter=False):
    n = len(arrs)
    ns = n * _SPLIT_SEMS

    def body(*refs):
        ins, lands = refs[:n], refs[n:2 * n]
        sems = refs[2 * n + 1:2 * n + 1 + ns]
        x, y, c = lax.axis_index("x"), lax.axis_index("y"), lax.axis_index("c")
        me = 4 * x + 2 * y + c
        for a in range(n):
            send, recv, local = _split_sems(sems, a)
            pltpu.make_async_copy(_split_src(ins[a], scatter, me), lands[a].at[me], local).start()
            for k, (px, py, pc) in enumerate(_peers(x, y, c)):
                pltpu.make_async_remote_copy(
                    src_ref=_split_src(ins[a], scatter, 4 * px + 2 * py + pc), dst_ref=lands[a].at[me],
                    send_sem=send[k], recv_sem=recv[k],
                    device_id=(px, py, pc), device_id_type=pl.DeviceIdType.MESH).start()

    lands = [lax.empty(a.shape if scatter else (N_DEV,) + a.shape, a.dtype) for a in arrs]
    operands = [pltpu.with_memory_space_constraint(a, pltpu.HBM) for a in list(arrs) + lands + [carry]]
    outs = pl.pallas_call(
        body, name=name, in_specs=[_HBM] * (2 * n + 1), out_specs=[_SEM] * ns + [_HBM] * (2 * n + 1),
        out_shape=[pltpu.SemaphoreType.DMA(())] * ns + [pltpu.HBM(a.shape, a.dtype) for a in operands],
        input_output_aliases={i: ns + i for i in range(2 * n + 1)},
        compiler_params=pltpu.CompilerParams(has_side_effects=_SPLIT_COPY),
    )(*operands)
    return tuple(outs[:-1]), outs[-1]


def _gather_wait(handles, after, *, name, scatter=False):
    n = len(handles) // (_SPLIT_SEMS + 2)
    ns = n * _SPLIT_SEMS
    sems, thru = handles[:ns], handles[ns:]

    def body(*refs):
        ins, lands = refs[:n], refs[n:2 * n]
        sems = refs[2 * n:2 * n + ns]
        x, y, c = lax.axis_index("x"), lax.axis_index("y"), lax.axis_index("c")
        me = 4 * x + 2 * y + c
        for a in range(n):
            send, recv, local = _split_sems(sems, a)
            src = _split_src(ins[a], scatter, me)
            pltpu.make_async_copy(src, lands[a].at[me], local).wait()
            for k, peer in enumerate(_peers(x, y, c)):
                cp = pltpu.make_async_remote_copy(
                    src_ref=src, dst_ref=lands[a].at[me], send_sem=send[k], recv_sem=recv[k],
                    device_id=peer, device_id_type=pl.DeviceIdType.MESH)
                cp.wait_send()
                cp.wait_recv()

    outs = pl.pallas_call(
        body, name=name, in_specs=[_HBM] * (2 * n) + [_SEM] * ns + [_ANY], out_specs=[_HBM] * (2 * n),
        out_shape=[pltpu.HBM(a.shape, a.dtype) for a in thru],
        input_output_aliases={i: i for i in range(2 * n)},
        compiler_params=pltpu.CompilerParams(has_side_effects=_SPLIT_COPY),
    )(*thru, *sems, after)
    return outs[n:]


def _sibling_swap(arrs, *, name):
    n = len(arrs)
    nchip = N_DEV // 2

    def body(*refs):
        ins, outs = refs[:n], refs[n:2 * n]
        send_sems, recv_sems = refs[2 * n:]
        x, y, c = lax.axis_index("x"), lax.axis_index("y"), lax.axis_index("c")
        copies = []
        for a in range(n):
            for k in range(nchip):
                cp = pltpu.make_async_remote_copy(
                    src_ref=ins[a].at[2 * k + 1 - c], dst_ref=outs[a].at[k],
                    send_sem=send_sems.at[a, k], recv_sem=recv_sems.at[a, k],
                    device_id=(x, y, 1 - c), device_id_type=pl.DeviceIdType.MESH)
                cp.start()
                copies.append(cp)
        for cp in copies:
            cp.wait()

    return pl.pallas_call(
        body, name=name, in_specs=[_HBM] * n, out_specs=[_HBM] * n,
        out_shape=[jax.ShapeDtypeStruct((nchip,) + a.shape[1:], a.dtype) for a in arrs],
        scratch_shapes=[pltpu.SemaphoreType.DMA((n, nchip)), pltpu.SemaphoreType.DMA((n, nchip))],
    )(*arrs)


def _chip_exchange(arrs, *, name):
    n = len(arrs)

    def body(*refs):
        ins, outs = refs[:n], refs[n:2 * n]
        send_sems, recv_sems, local_sems = refs[2 * n:]
        x, y, c = lax.axis_index("x"), lax.axis_index("y"), lax.axis_index("c")
        mine = 2 * x + y
        copies = []
        for a in range(n):
            cp = pltpu.make_async_copy(ins[a].at[mine], outs[a].at[mine], local_sems.at[a])
            cp.start()
            copies.append(cp)
            for j, (px, py) in enumerate(_other_chips(x, y)):
                cp = pltpu.make_async_remote_copy(
                    src_ref=ins[a].at[2 * px + py], dst_ref=outs[a].at[mine],
                    send_sem=send_sems.at[a, j], recv_sem=recv_sems.at[a, j],
                    device_id=(px, py, c), device_id_type=pl.DeviceIdType.MESH)
                cp.start()
                copies.append(cp)
        for cp in copies:
            cp.wait()

    return pl.pallas_call(
        body, name=name, in_specs=[_HBM] * n, out_specs=[_HBM] * n,
        out_shape=[jax.ShapeDtypeStruct(a.shape, a.dtype) for a in arrs],
        scratch_shapes=[pltpu.SemaphoreType.DMA((n, 3)), pltpu.SemaphoreType.DMA((n, 3)),
                        pltpu.SemaphoreType.DMA((n,))],
    )(*arrs)


def _pair_sum(a, b, *, name):
    P, R, C = a.shape
    tr = _tile(R, max(SUBLANES, (1 << 19) // C), SUBLANES)

    def body(a_ref, b_ref, o_ref):
        o_ref[...] = (a_ref[...].astype(F32) + b_ref[...].astype(F32)).astype(o_ref.dtype)

    blk = pl.BlockSpec((1, tr, C), lambda p, i: (p, i, 0))
    return pl.pallas_call(
        body, name=name, grid=(P, R // tr), in_specs=[blk, blk], out_specs=blk,
        out_shape=jax.ShapeDtypeStruct(a.shape, a.dtype), compiler_params=_params(("parallel", "parallel")),
    )(a, b)


def _group_in_cols(w, lead):
    X = (w.shape[-1] - lead) // 3
    g = w[..., lead:].reshape(w.shape[:-1] + (3, X // LANES, LANES))
    g = jnp.swapaxes(g, -3, -2).reshape(w.shape[:-1] + (3 * X,))
    return jnp.concatenate([g, w[..., :lead]], axis=-1)


def _ungroup_in_cols(w, lead):
    X = (w.shape[-1] - lead) // 3
    g = w[..., :3 * X].reshape(w.shape[:-1] + (X // LANES, 3, LANES))
    g = jnp.swapaxes(g, -3, -2).reshape(w.shape[:-1] + (3 * X,))
    return jnp.concatenate([w[..., 3 * X:], g], axis=-1)


def _mlp_fwd(h, g, W, tag):
    hn = _rmsnorm(h, g, name=f"mlp_norm_{tag}")
    act = _matmul(hn, _weight(W, f"mlp_w1_{tag}", hn), name=f"mlp_up_{tag}", epilogue=_ep_relu2,
                  out_dtypes=(BF16,))
    out = _matmul(act, _weight(W, f"mlp_w2_{tag}", act), name=f"mlp_down_{tag}", extras=(h,), epilogue=_ep_add)
    return out, (hn, act)


def _mlp_bwd(dout, dout_b, h, g, w1, w2, saved, tag, sent):
    hn, act = saved
    dw2 = _matmul(act, dout_b, ta=True, name=f"mlp_dw2_{tag}", out_dtypes=(GRAD_WIRE,))
    dout_b = sent(dout_b, {f"mlp_w2_{tag}": dw2})
    dz = _matmul(dout_b, w2, tb=True, name=f"mlp_dact_{tag}", extras=(act,), epilogue=_ep_relu2_bwd,
                 out_dtypes=(BF16,))
    dw1 = _matmul(hn, dz, ta=True, name=f"mlp_dw1_{tag}", out_dtypes=(GRAD_WIRE,))
    dz = sent(dz, {f"mlp_w1_{tag}": dw1})
    dhn = _matmul(dz, w1, tb=True, name=f"mlp_dhn_{tag}")
    dh, dh_b, dg = _rmsnorm_bwd(dhn, h, g, dout, name=f"mlp_norm_bwd_{tag}")
    return dh, dh_b, dg, dw1, dw2


class _Lazy:
    def __init__(self, handles, finish, name):
        self.handles, self.finish, self.name, self.done = handles, finish, name, None

    def take(self, after):
        if self.done is None:
            self.done = self.finish(_gather_wait(self.handles, after, name=self.name))
        return self.done


def _weight(W, n, after):
    if isinstance(W[n], _Lazy):
        W.update(W[n].take(after))
    return W[n]


def _local_step(x, target, W, emit=None):
    B, T, D = x.shape
    N = B * T
    G = {}
    row = lambda vec: vec.reshape(1, -1)

    def sent(nxt, grads):
        return nxt if emit is None else emit(grads, nxt)

    PW = W["pool_w"].shape[0] * W["pool_w"].shape[1]
    CW = W["conv_b"].shape[-1]
    SW = W["sgu_norm_g"].shape[-1]
    HW = W["att_width"]

    h0 = x.reshape(N, D)
    xn0 = _rmsnorm(h0, row(W["mix_norm_g"][0]), name="mix_norm_0")
    p0 = _matmul(xn0, W["ab_w_in"], name="ab_in")
    cat0 = _pool_fwd(p0, W["pool_w"], W["pool_scale"], B, T, 3 * CW, PW + CW, name="pool_fwd")
    cat0 = _conv_fwd(p0, cat0, W["conv_w"], row(W["conv_b"]), B, T, PW, name="conv_fwd")
    h1 = _matmul(cat0, W["ab_w_out"], name="ab_out", extras=(h0,), epilogue=_ep_add)
    h2, mlp0 = _mlp_fwd(h1, row(W["mlp_norm_g"][0]), W, 0)
    xn1 = _rmsnorm(h2, row(W["mix_norm_g"][1]), name="mix_norm_1")
    p1 = _matmul(xn1, _weight(W, "cd_w_in", xn1), name="cd_in")
    ln_g, ln_b = row(W["sgu_norm_g"]), row(W["sgu_norm_b"])
    cat1 = _sgu_fwd(p1, ln_g, ln_b, W["sgu_w"], W["sgu_b"], 3 * HW, SW + HW, name="sgu_fwd")
    cat1, att_carries = _attn_fwd(p1, cat1, B, T, SW, name="attn_fwd")
    h3 = _matmul(cat1, W["cd_w_out"], name="cd_out", extras=(h2,), epilogue=_ep_add)
    h4, mlp1 = _mlp_fwd(h3, row(W["mlp_norm_g"][1]), W, 1)

    loss, dh4, dh4_b, G["final_norm_g"] = _final_loss(h4, row(W["final_norm_g"]), target.reshape(N, D),
                                                      name="final_loss")

    dh3, dh3_b, dmlp_g1, dw1_1, dw2_1 = _mlp_bwd(dh4, dh4_b, h3, row(W["mlp_norm_g"][1]), W["mlp_w1_1"],
                                                 W["mlp_w2_1"], mlp1, 1, sent)
    G["cd_w_out"] = _matmul(cat1, dh3_b, ta=True, name="cd_out_dw", out_dtypes=(GRAD_WIRE,))[None]
    dh3_b = sent(dh3_b, {"cd_w_out": G["cd_w_out"][0]})
    dcat1 = _matmul(dh3_b, W["cd_w_out"], tb=True, name="cd_out_dx")
    dp1, G["sgu_w"], dsgu_b, G["sgu_norm_g"], G["sgu_norm_b"] = _sgu_bwd(
        p1, dcat1, ln_g, ln_b, W["sgu_w"], W["sgu_b"], 3 * HW, name="sgu_bwd")
    G["sgu_b"] = dsgu_b.reshape(W["sgu_b"].shape)
    dp1 = _attn_bwd(p1, dcat1, att_carries, dp1, B, T, SW, name="attn_bwd")
    G["cd_w_in"] = _matmul(xn1, dp1, ta=True, name="cd_in_dw", out_dtypes=(GRAD_WIRE,))[None]
    dp1 = sent(dp1, {"cd_w_in": G["cd_w_in"][0]})
    dxn1 = _matmul(dp1, W["cd_w_in"], tb=True, name="cd_in_dx")
    dh2, dh2_b, dmix_g1 = _rmsnorm_bwd(dxn1, h2, row(W["mix_norm_g"][1]), dh3, name="mix_norm_bwd_1")

    dh1, dh1_b, dmlp_g0, dw1_0, dw2_0 = _mlp_bwd(dh2, dh2_b, h1, row(W["mlp_norm_g"][0]), W["mlp_w1_0"],
                                                 W["mlp_w2_0"], mlp0, 0, sent)
    G["ab_w_out"] = _matmul(cat0, dh1_b, ta=True, name="ab_out_dw", out_dtypes=(GRAD_WIRE,))[None]
    dh1_b = sent(dh1_b, {"ab_w_out": G["ab_w_out"][0]})
    dcat0 = _matmul(dh1_b, W["ab_w_out"], tb=True, name="ab_out_dx")
    dp0, G["pool_w"], dps = _pool_bwd(p0, dcat0, W["pool_w"], W["pool_scale"], B, T, 3 * CW, name="pool_bwd")
    G["pool_scale"] = dps.reshape(W["pool_scale"].shape)
    dp0, G["conv_w"], dcb = _conv_bwd(p0, dcat0, dp0, W["conv_w"], row(W["conv_b"]), B, T, PW, name="conv_bwd")
    G["conv_b"] = dcb.reshape(-1)
    G["ab_w_in"] = _matmul(xn0, dp0, ta=True, name="ab_in_dw", out_dtypes=(GRAD_WIRE,))[None]
    dp0 = sent(dp0, {"ab_w_in": G["ab_w_in"][0]})
    dxn0 = _matmul(dp0, W["ab_w_in"], tb=True, name="ab_in_dx")
    dx, _, dmix_g0 = _rmsnorm_bwd(dxn0, h0, row(W["mix_norm_g"][0]), dh1, name="mix_norm_bwd_0")

    G["mix_norm_g"] = jnp.concatenate([dmix_g0, dmix_g1], axis=0)
    G["mlp_norm_g"] = jnp.concatenate([dmlp_g0, dmlp_g1], axis=0)
    G["mlp_w1"] = jnp.stack([dw1_0, dw1_1])
    G["mlp_w2"] = jnp.stack([dw2_0, dw2_1])
    G["final_norm_g"] = G["final_norm_g"].reshape(-1)
    G["sgu_norm_g"] = G["sgu_norm_g"].reshape(-1)
    G["sgu_norm_b"] = G["sgu_norm_b"].reshape(-1)
    return loss[0, 0], dx.reshape(B, T, D), G


_NAMES = ["mix_norm_g", "mlp_norm_g", "ab_w_in", "pool_w", "pool_scale", "conv_w", "conv_b", "ab_w_out",
          "cd_w_in", "sgu_norm_g", "sgu_norm_b", "sgu_w", "sgu_b", "cd_w_out", "mlp_w1", "mlp_w2",
          "final_norm_g"]
_COL_SHARDED = ["ab_w_in", "cd_w_in", "mlp_w1"]
_ROW_SHARDED = ["ab_w_out", "cd_w_out", "mlp_w2"]
_SMALL_SHARDED = ["conv_w", "sgu_norm_g", "sgu_norm_b"]
_REPLICATED = ["mix_norm_g", "mlp_norm_g", "pool_w", "pool_scale", "conv_b", "sgu_w", "sgu_b", "final_norm_g"]


def _pad_rows(a2d, mult=SUBLANES):
    pad = (-a2d.shape[0]) % mult
    return jnp.pad(a2d, ((0, pad), (0, 0))) if pad else a2d


def _pack(arrays):
    return _pad_rows(jnp.concatenate([a.reshape(-1, LANES) for a in arrays], axis=0))


def _unpack(packed, shapes):
    out, r = [], 0
    for s in shapes:
        n = math.prod(s) // LANES
        out.append(packed[r:r + n].reshape(s))
        r += n
    return out


def _small_shard_pack(arrays):
    rows = [jnp.pad(a.reshape(-1, a.shape[-1]), ((0, 0), (0, LANES - a.shape[-1]))) for a in arrays]
    return _pad_rows(jnp.concatenate(rows, axis=0))


def _cols_to_chunks(a):
    n = a.shape[-1] // N_DEV
    return jnp.moveaxis(a.reshape(a.shape[:-1] + (N_DEV, n)), -2, 0)


def _chunks_to_cols(a):
    t = jnp.moveaxis(a, 0, -2)
    return t.reshape(t.shape[:-2] + (t.shape[-2] * t.shape[-1],))


def _rows_to_chunks(a):
    r = a.shape[-2] // N_DEV
    return jnp.moveaxis(a.reshape(a.shape[:-2] + (N_DEV, r, a.shape[-1])), -3, 0)


def _chunks_to_rows(a):
    t = jnp.moveaxis(a, 0, -3)
    return t.reshape(t.shape[:-3] + (t.shape[-3] * t.shape[-2], t.shape[-1]))


def kernel(x, mix_norm_g, mlp_norm_g, ab_w_in, pool_w, pool_scale, conv_w, conv_b, ab_w_out, cd_w_in, sgu_norm_g, sgu_norm_b, sgu_w, sgu_b, cd_w_out, mlp_w1, mlp_w2, final_norm_g, loss_target, m_mix_norm_g, m_mlp_norm_g, m_ab_w_in, m_pool_w, m_pool_scale, m_conv_w, m_conv_b, m_ab_w_out, m_cd_w_in, m_sgu_norm_g, m_sgu_norm_b, m_sgu_w, m_sgu_b, m_cd_w_out, m_mlp_w1, m_mlp_w2, m_final_norm_g, v_mix_norm_g, v_mlp_norm_g, v_ab_w_in, v_pool_w, v_pool_scale, v_conv_w, v_conv_b, v_ab_w_out, v_cd_w_in, v_sgu_norm_g, v_sgu_norm_b, v_sgu_w, v_sgu_b, v_cd_w_out, v_mlp_w1, v_mlp_w2, v_final_norm_g):
    w = dict(zip(_NAMES, (mix_norm_g, mlp_norm_g, ab_w_in, pool_w, pool_scale, conv_w, conv_b, ab_w_out, cd_w_in,
                          sgu_norm_g, sgu_norm_b, sgu_w, sgu_b, cd_w_out, mlp_w1, mlp_w2, final_norm_g)))
    m = dict(zip(_NAMES, (m_mix_norm_g, m_mlp_norm_g, m_ab_w_in, m_pool_w, m_pool_scale, m_conv_w, m_conv_b,
                          m_ab_w_out, m_cd_w_in, m_sgu_norm_g, m_sgu_norm_b, m_sgu_w, m_sgu_b, m_cd_w_out,
                          m_mlp_w1, m_mlp_w2, m_final_norm_g)))
    v = dict(zip(_NAMES, (v_mix_norm_g, v_mlp_norm_g, v_ab_w_in, v_pool_w, v_pool_scale, v_conv_w, v_conv_b,
                          v_ab_w_out, v_cd_w_in, v_sgu_norm_g, v_sgu_norm_b, v_sgu_w, v_sgu_b, v_cd_w_out,
                          v_mlp_w1, v_mlp_w2, v_final_norm_g)))
    big = _COL_SHARDED + _ROW_SHARDED
    me = _my_index()

    small_sh = _small_shard_pack([w[n] for n in _SMALL_SHARDED])
    in_lead = {"ab_w_in": pool_w.shape[1] * pool_w.shape[2], "cd_w_in": 2 * sgu_norm_g.shape[-1] * N_DEV}
    shard = {"ab_w_in": ab_w_in[0], "ab_w_out": ab_w_out[0], "cd_w_in": cd_w_in[0], "cd_w_out": cd_w_out[0]}
    for layer in range(mlp_w1.shape[0]):
        shard[f"mlp_w1_{layer}"], shard[f"mlp_w2_{layer}"] = mlp_w1[layer], mlp_w2[layer]
    shard = {n: a.astype(BF16) for n, a in shard.items()}

    def whole(n, g):
        if n.endswith("_in") or n.startswith("mlp_w1"):
            full = _chunks_to_cols(g)
            return _group_in_cols(full, in_lead[n]) if n in in_lead else full
        return _chunks_to_rows(g)

    W = {"att_width": cd_w_out.shape[1] * N_DEV - sgu_norm_g.shape[-1] * N_DEV}
    later = [(["mlp_w2_0"], "mlp0_down"), (["cd_w_in", "cd_w_out"], "cd"), (["mlp_w1_1", "mlp_w2_1"], "mlp1")]

    def start(idx, carry, then=None):
        group, tag = later[idx]
        handles, carry = _gather_start([shard[n] for n in group], carry, name=f"gather_{tag}_start")

        def finish(got):
            done = {n: whole(n, g) for n, g in zip(group, got)}
            if then is not None:
                done[group[0]] = start(then, done[group[0]])
            return done

        W.update({n: _Lazy(handles, finish, f"gather_{tag}_wait") for n in group})
        return carry

    now = ["ab_w_in", "ab_w_out", "mlp_w1_0"]
    gathered = _gather_two_level([shard[n] for n in now] + [small_sh], name="gather_weights")
    W.update({n: whole(n, g) for n, g in zip(now, gathered)})
    W["ab_w_in"] = start(1, start(0, W["ab_w_in"], then=2))
    small_full = gathered[-1]
    r = 0
    for n in _SMALL_SHARDED:
        rows, width = math.prod(w[n].shape[:-1]), w[n].shape[-1]
        W[n] = _chunks_to_cols(small_full[:, r:r + rows, :width])
        r += rows
    for n in _REPLICATED:
        W[n] = w[n]
    for n in ("pool_w", "pool_scale", "sgu_w", "sgu_b"):
        W[n] = W[n][0]

    pending = []

    def emit(grads, carry):
        names = list(grads)
        parts = []
        for n in names:
            g = _ungroup_in_cols(grads[n], in_lead[n]) if n in in_lead else grads[n]
            parts.append(_cols_to_chunks(g) if n.endswith("_in") or n.startswith("mlp_w1") else _rows_to_chunks(g))
        handles, carry = _gather_start(parts, carry, name=f"grads_{names[0]}_start", scatter=True)
        pending.append((names, handles))
        return carry

    loss_part, grad_x, G = _local_step(x, loss_target, W, emit)

    landed = {}
    for names, handles in pending:
        got = _gather_wait(handles, grad_x, name=f"grads_{names[0]}_wait", scatter=True)
        landed.update(zip(names, got))
    small_names = _REPLICATED + _SMALL_SHARDED
    small_grads = [G[n].reshape(-1) for n in small_names]
    loss_row = jnp.full((LANES,), loss_part, F32)
    small_pack = _pack(small_grads + [loss_row])
    small_parts = _exchange([small_pack], [True], name="gather_small_grads")[0]

    grads, deltas, new_m, new_v = {}, {}, {}, {}
    for n in big:
        layers = [landed[f"{n}_{l}"] for l in range(w[n].shape[0])] if n.startswith("mlp") else [landed[n]]
        grads[n], deltas[n], new_m[n], new_v[n] = _adamw(w[n], layers, m[n], v[n], name=f"adamw_{n}")

    rep_shapes = [w[n].shape for n in _REPLICATED]
    rep_rows = sum(math.prod(s) for s in rep_shapes) // LANES
    small_sum_shapes = [(G[n].size,) for n in small_names] + [(LANES,)]
    zero_tail = [jnp.zeros((math.prod(s),), F32) for s in small_sum_shapes[len(_REPLICATED):]]
    w_pack = _pack([w[n] for n in _REPLICATED] + zero_tail)
    m_pack = _pack([m[n] for n in _REPLICATED] + zero_tail)
    v_pack = _pack([v[n] for n in _REPLICATED] + zero_tail)
    outs = [o[0] for o in _adamw(w_pack[None], [small_parts], m_pack[None], v_pack[None], name="adamw_small")]
    summed = _unpack(outs[0], small_sum_shapes)
    for i, n in enumerate(_REPLICATED):
        grads[n] = summed[i].reshape(w[n].shape)
    for dst, o in zip((deltas, new_m, new_v), outs[1:]):
        for n, val in zip(_REPLICATED, _unpack(o[:rep_rows], rep_shapes)):
            dst[n] = val
    loss = summed[-1][0]

    shard_g = []
    for i, n in enumerate(_SMALL_SHARDED):
        full = summed[len(_REPLICATED) + i].reshape(w[n].shape[:-1] + (-1,))
        width = w[n].shape[-1]
        shard_g.append(lax.dynamic_slice_in_dim(full, me * width, width, axis=full.ndim - 1))
    g_sh = _small_shard_pack(shard_g)
    m_sh = _small_shard_pack([m[n] for n in _SMALL_SHARDED])
    v_sh = _small_shard_pack([v[n] for n in _SMALL_SHARDED])
    outs = [o[0] for o in _adamw(small_sh[None], [g_sh[None]], m_sh[None], v_sh[None], name="adamw_small_sharded")]
    r = 0
    for n in _SMALL_SHARDED:
        rows, width = math.prod(w[n].shape[:-1]), w[n].shape[-1]
        for dst, o in zip((grads, deltas, new_m, new_v), outs):
            dst[n] = o[r:r + rows, :width].reshape(w[n].shape)
        r += rows

    return (loss, grad_x, *[grads[n] for n in _NAMES], *[deltas[n] for n in _NAMES],
            *[new_m[n] for n in _NAMES], *[new_v[n] for n in _NAMES])
```

```python
import functools
import math

import jax
import jax.numpy as jnp
from jax import lax
from jax.experimental import pallas as pl
from jax.experimental.pallas import tpu as pltpu

F32 = jnp.float32
BF16 = jnp.bfloat16
GRAD_WIRE = jnp.bfloat16

NORM_EPS = 1e-6
ADAM_LR = 0.001
ADAM_B1 = 0.9
ADAM_B2 = 0.999
ADAM_EPS = 1e-08
ADAM_WD = 0.01
ADAM_STEP = 10
ADAM_C1 = 1.0 - ADAM_B1 ** ADAM_STEP
ADAM_C2 = 1.0 - ADAM_B2 ** ADAM_STEP

N_DEV = 8
LANES = 128
SUBLANES = 8
SB_DH = 64
ATT_BLOCK = 256
POOL_LOG_WINDOWS = 4
VMEM_LIMIT = 56 * 1024 * 1024


def _params(semantics=None):
    return pltpu.CompilerParams(dimension_semantics=semantics, vmem_limit_bytes=VMEM_LIMIT)


def _tile(dim, pref, unit=LANES):
    if dim <= pref:
        return dim
    t = (pref // unit) * unit
    while t >= unit:
        if dim % t == 0:
            return t
        t -= unit
    return dim


def _matmul(a, b, *, name, ta=False, tb=False, extras=(), epilogue=None, out_dtypes=(F32,),
            tm=1024, tn=1024, tk=2048, b_chunks=False, out_chunks=0):
    M, K = (a.shape[1], a.shape[0]) if ta else a.shape
    if b_chunks:
        C, rows, n = b.shape
        N, tn, tk = (rows, tn, n) if tb else (C * n, n, tk)
        assert (C * n if tb else rows) == K, (a.shape, b.shape)
    else:
        N = b.shape[0] if tb else b.shape[1]
        assert (b.shape[1] if tb else b.shape[0]) == K, (a.shape, b.shape)
    if out_chunks:
        assert not extras and N % out_chunks == 0
        tn = N // out_chunks
    tm, tn, tk = _tile(M, tm), _tile(N, tn), _tile(K, tk)
    nk = K // tk
    dims = (((0 if ta else 1,), (1 if tb else 0,)), ((), ()))
    ne, no = len(extras), len(out_dtypes)

    def body(*refs):
        a_ref, b_ref = refs[0], refs[1]
        e_refs = refs[2:2 + ne]
        o_refs = refs[2 + ne:2 + ne + no]
        k = pl.program_id(2)

        def part():
            return lax.dot_general(a_ref[...].astype(BF16), b_ref[...].astype(BF16), dims,
                                   preferred_element_type=F32)

        def finish(acc):
            outs = epilogue(acc, *[e[...] for e in e_refs]) if epilogue is not None else (acc,)
            for o_ref, val in zip(o_refs, outs):
                o_ref[...] = val.astype(o_ref.dtype)

        if nk == 1:
            finish(part())
        else:
            acc_ref = refs[-1]

            @pl.when(k == 0)
            def _():
                acc_ref[...] = jnp.zeros_like(acc_ref)

            acc_ref[...] += part()

            @pl.when(k == nk - 1)
            def _():
                finish(acc_ref[...])

    a_spec = (pl.BlockSpec((tk, tm), lambda i, j, k: (k, i)) if ta
              else pl.BlockSpec((tm, tk), lambda i, j, k: (i, k)))
    if b_chunks:
        b_spec = (pl.BlockSpec((None, tn, tk), lambda i, j, k: (k, j, 0)) if tb
                  else pl.BlockSpec((None, tk, tn), lambda i, j, k: (j, k, 0)))
    else:
        b_spec = (pl.BlockSpec((tn, tk), lambda i, j, k: (j, k)) if tb
                  else pl.BlockSpec((tk, tn), lambda i, j, k: (k, j)))
    if out_chunks:
        o_spec = pl.BlockSpec((None, tm, tn), lambda i, j, k: (j, i, 0))
        o_shape = (out_chunks, M, tn)
    else:
        o_spec = pl.BlockSpec((tm, tn), lambda i, j, k: (i, j))
        o_shape = (M, N)
    outs = pl.pallas_call(
        body,
        name=name,
        grid=(M // tm, N // tn, nk),
        in_specs=[a_spec, b_spec] + [o_spec] * ne,
        out_specs=[o_spec] * no,
        out_shape=[jax.ShapeDtypeStruct(o_shape, dt) for dt in out_dtypes],
        scratch_shapes=[pltpu.VMEM((tm, tn), F32)] if nk > 1 else [],
        compiler_params=_params(("parallel", "parallel", "arbitrary")),
    )(a, b, *extras)
    return outs[0] if no == 1 else outs


def _ep_add(acc, res):
    return (acc + res,)


def _ep_relu2(acc):
    r = jnp.maximum(acc, 0.0)
    return (r * r,)


def _ep_relu2_bwd(acc, act):
    return (acc * (2.0 * jnp.sqrt(act.astype(F32))),)


def _rstd(x):
    return lax.rsqrt(jnp.mean(x * x, axis=-1, keepdims=True) + NORM_EPS)


def _rmsnorm(h, g, *, name, tr=512):
    N, D = h.shape
    tr = _tile(N, tr, SUBLANES)

    def body(h_ref, g_ref, o_ref):
        x = h_ref[...]
        o_ref[...] = ((x * _rstd(x)) * g_ref[...]).astype(o_ref.dtype)

    row = pl.BlockSpec((tr, D), lambda i: (i, 0))
    vec = pl.BlockSpec((1, D), lambda i: (0, 0))
    return pl.pallas_call(
        body, name=name, grid=(N // tr,), in_specs=[row, vec], out_specs=row,
        out_shape=jax.ShapeDtypeStruct((N, D), BF16), compiler_params=_params(("parallel",)),
    )(h, g)


def _rmsnorm_bwd(dy, h, g, dres, *, name, tr=512):
    N, D = h.shape
    tr = _tile(N, tr, SUBLANES)

    def body(dy_ref, h_ref, g_ref, r_ref, dh_ref, dhb_ref, dg_ref):
        i = pl.program_id(0)
        x = h_ref[...]
        d = dy_ref[...]
        r = _rstd(x)
        xh = x * r

        @pl.when(i == 0)
        def _():
            dg_ref[...] = jnp.zeros_like(dg_ref)

        dg_ref[...] += jnp.sum(d * xh, axis=0, keepdims=True)
        dxh = d * g_ref[...]
        dh = r_ref[...] + r * (dxh - xh * jnp.mean(dxh * xh, axis=-1, keepdims=True))
        dh_ref[...] = dh
        dhb_ref[...] = dh.astype(dhb_ref.dtype)

    row = pl.BlockSpec((tr, D), lambda i: (i, 0))
    vec = pl.BlockSpec((1, D), lambda i: (0, 0))
    return pl.pallas_call(
        body, name=name, grid=(N // tr,), in_specs=[row, row, vec, row], out_specs=[row, row, vec],
        out_shape=[jax.ShapeDtypeStruct((N, D), F32), jax.ShapeDtypeStruct((N, D), BF16),
                   jax.ShapeDtypeStruct((1, D), F32)],
        compiler_params=_params(("arbitrary",)),
    )(dy, h, g, dres)


def _final_loss(h, g, target, *, name, tr=512):
    N, D = h.shape
    tr = _tile(N, tr, SUBLANES)

    def body(h_ref, g_ref, t_ref, loss_ref, dh_ref, dhb_ref, dg_ref):
        i = pl.program_id(0)
        x = h_ref[...]
        gg = g_ref[...]
        r = _rstd(x)
        xh = x * r
        err = xh * gg - t_ref[...]

        @pl.when(i == 0)
        def _():
            dg_ref[...] = jnp.zeros_like(dg_ref)
            loss_ref[...] = jnp.zeros_like(loss_ref)

        per_row = jnp.mean(err * err, axis=-1, keepdims=True)
        loss_ref[...] += 0.5 * jnp.sum(per_row, axis=0, keepdims=True)
        dy = err * (1.0 / D)
        dg_ref[...] += jnp.sum(dy * xh, axis=0, keepdims=True)
        dxh = dy * gg
        dh = r * (dxh - xh * jnp.mean(dxh * xh, axis=-1, keepdims=True))
        dh_ref[...] = dh
        dhb_ref[...] = dh.astype(dhb_ref.dtype)

    row = pl.BlockSpec((tr, D), lambda i: (i, 0))
    vec = pl.BlockSpec((1, D), lambda i: (0, 0))
    lvec = pl.BlockSpec((1, LANES), lambda i: (0, 0))
    return pl.pallas_call(
        body, name=name, grid=(N // tr,), in_specs=[row, vec, row], out_specs=[lvec, row, row, vec],
        out_shape=[jax.ShapeDtypeStruct((1, LANES), F32), jax.ShapeDtypeStruct((N, D), F32),
                   jax.ShapeDtypeStruct((N, D), BF16), jax.ShapeDtypeStruct((1, D), F32)],
        compiler_params=_params(("arbitrary",)),
    )(h, g, target)


def _shift_down(x, s):
    t = lax.broadcasted_iota(jnp.int32, x.shape, 0)
    return jnp.where(t >= s, pltpu.roll(x, s, 0), 0.0)


def _shift_up(x, s):
    n = x.shape[0]
    t = lax.broadcasted_iota(jnp.int32, x.shape, 0)
    return jnp.where(t < n - s, pltpu.roll(x, n - s, 0), 0.0)


def _window_sum(x, g, shift):
    s = x + shift(x, 1)
    for k in range(1, POOL_LOG_WINDOWS):
        s = jnp.where(k <= g, s + shift(s, 2 ** k), s)
    return s


def _pool_count(shape, g):
    t = lax.broadcasted_iota(jnp.int32, shape, 0)
    return jnp.minimum(t + 1, lax.shift_left(jnp.int32(2), g)).astype(F32)


def _pool_fwd(p, pool_w, pool_scale, B, T, off, width, *, name):
    G, dh = pool_w.shape[0], pool_w.shape[1]
    assert G == POOL_LOG_WINDOWS and off % dh == 0
    base = off // dh

    def body(a_ref, w_ref, s_ref, o_ref):
        g = pl.program_id(0)
        a = a_ref[...]
        pooled = _window_sum(a, g, _shift_down) / _pool_count(a.shape, g) - a
        m = jnp.dot(pooled.astype(BF16), w_ref[0].astype(BF16), preferred_element_type=F32)
        o_ref[...] = (m * s_ref[0]).astype(o_ref.dtype)

    return pl.pallas_call(
        body, name=name, grid=(G, B),
        in_specs=[pl.BlockSpec((T, dh), lambda g, b: (b, base + g)),
                  pl.BlockSpec((1, dh, dh), lambda g, b: (g, 0, 0)),
                  pl.BlockSpec((1, 1, dh), lambda g, b: (g, 0, 0))],
        out_specs=pl.BlockSpec((T, dh), lambda g, b: (b, g)),
        out_shape=jax.ShapeDtypeStruct((B * T, width), BF16),
        compiler_params=_params(("parallel", "parallel")),
    )(p, pool_w, pool_scale.reshape(G, 1, dh))


def _pool_bwd(p, dcat, pool_w, pool_scale, B, T, off, *, name):
    G, dh = pool_w.shape[0], pool_w.shape[1]
    base = off // dh

    def body(a_ref, d_ref, w_ref, s_ref, da_ref, dw_ref, ds_ref):
        g = pl.program_id(0)
        b = pl.program_id(1)
        a = a_ref[...]
        d = d_ref[...]
        cnt = _pool_count(a.shape, g)
        pooled = (_window_sum(a, g, _shift_down) / cnt - a).astype(BF16)
        w = w_ref[0].astype(BF16)
        m = jnp.dot(pooled, w, preferred_element_type=F32)

        @pl.when(b == 0)
        def _():
            dw_ref[...] = jnp.zeros_like(dw_ref)
            ds_ref[...] = jnp.zeros_like(ds_ref)

        ds_ref[0] += jnp.sum(d * m, axis=0, keepdims=True)
        dm = (d * s_ref[0]).astype(BF16)
        dw_ref[0] += lax.dot_general(pooled, dm, (((0,), (0,)), ((), ())), preferred_element_type=F32)
        dpooled = lax.dot_general(dm, w, (((1,), (1,)), ((), ())), preferred_element_type=F32)
        da = _window_sum(dpooled / cnt, g, _shift_up) - dpooled
        da_ref[...] = da.astype(da_ref.dtype)

    pblk = pl.BlockSpec((T, dh), lambda g, b: (b, base + g))
    dblk = pl.BlockSpec((T, dh), lambda g, b: (b, g))
    wspec = pl.BlockSpec((1, dh, dh), lambda g, b: (g, 0, 0))
    sspec = pl.BlockSpec((1, 1, dh), lambda g, b: (g, 0, 0))
    return pl.pallas_call(
        body, name=name, grid=(G, B), in_specs=[pblk, dblk, wspec, sspec], out_specs=[pblk, wspec, sspec],
        out_shape=[jax.ShapeDtypeStruct((B * T, p.shape[1]), BF16), jax.ShapeDtypeStruct((G, dh, dh), F32),
                   jax.ShapeDtypeStruct((G, 1, dh), F32)],
        compiler_params=_params(("parallel", "arbitrary")),
    )(p, dcat, pool_w, pool_scale.reshape(G, 1, dh))


_ANY = pl.BlockSpec(memory_space=pl.ANY)


def _conv_fwd(p, cat, conv_w, conv_b, B, T, coff, *, name):
    CW = conv_w.shape[1]
    tc = LANES
    assert coff % tc == 0 and CW % tc == 0
    cbase = coff // tc

    def body(p_ref, cat_ref, w_ref, b_ref, o_ref):
        xb, gb, gc = p_ref[:, 0:tc], p_ref[:, tc:2 * tc], p_ref[:, 2 * tc:3 * tc]
        c = gc * xb
        w = w_ref[...]
        y = _shift_down(c, 2) * w[0:1] + _shift_down(c, 1) * w[1:2] + c * w[2:3] + b_ref[...]
        o_ref[...] = (gb * y).astype(o_ref.dtype)

    return pl.pallas_call(
        body, name=name, grid=(CW // tc, B),
        in_specs=[pl.BlockSpec((T, 3 * tc), lambda j, b: (b, j)), _ANY,
                  pl.BlockSpec((3, tc), lambda j, b: (0, j)), pl.BlockSpec((1, tc), lambda j, b: (0, j))],
        out_specs=pl.BlockSpec((T, tc), lambda j, b: (b, cbase + j)),
        out_shape=jax.ShapeDtypeStruct(cat.shape, cat.dtype), input_output_aliases={1: 0},
        compiler_params=_params(("parallel", "parallel")),
    )(p, cat, conv_w, conv_b)


def _conv_bwd(p, dcat, dp, conv_w, conv_b, B, T, coff, *, name):
    CW = conv_w.shape[1]
    tc = LANES
    assert coff % tc == 0
    cbase = coff // tc

    def body(p_ref, d_ref, dp_in_ref, w_ref, b_ref, dp_ref, dw_ref, db_ref):
        b = pl.program_id(1)
        xb, gb, gc = p_ref[:, 0:tc], p_ref[:, tc:2 * tc], p_ref[:, 2 * tc:3 * tc]
        d = d_ref[...]
        w = w_ref[...]
        c = gc * xb
        c1 = _shift_down(c, 1)
        c2 = _shift_down(c, 2)
        y = c2 * w[0:1] + c1 * w[1:2] + c * w[2:3] + b_ref[...]
        dy = d * gb
        dp_ref[:, tc:2 * tc] = (d * y).astype(dp_ref.dtype)

        @pl.when(b == 0)
        def _():
            dw_ref[...] = jnp.zeros_like(dw_ref)
            db_ref[...] = jnp.zeros_like(db_ref)

        db_ref[...] += jnp.sum(dy, axis=0, keepdims=True)
        dw_ref[0:1, :] += jnp.sum(dy * c2, axis=0, keepdims=True)
        dw_ref[1:2, :] += jnp.sum(dy * c1, axis=0, keepdims=True)
        dw_ref[2:3, :] += jnp.sum(dy * c, axis=0, keepdims=True)
        dc = dy * w[2:3] + _shift_up(dy, 1) * w[1:2] + _shift_up(dy, 2) * w[0:1]
        dp_ref[:, 2 * tc:3 * tc] = (dc * xb).astype(dp_ref.dtype)
        dp_ref[:, 0:tc] = (dc * gc).astype(dp_ref.dtype)

    wspec = pl.BlockSpec((3, tc), lambda j, b: (0, j))
    bspec = pl.BlockSpec((1, tc), lambda j, b: (0, j))
    pblk = pl.BlockSpec((T, 3 * tc), lambda j, b: (b, j))
    return pl.pallas_call(
        body, name=name, grid=(CW // tc, B),
        in_specs=[pblk, pl.BlockSpec((T, tc), lambda j, b: (b, cbase + j)), _ANY, wspec, bspec],
        out_specs=[pblk, wspec, bspec],
        out_shape=[jax.ShapeDtypeStruct(dp.shape, dp.dtype), jax.ShapeDtypeStruct((3, CW), F32),
                   jax.ShapeDtypeStruct((1, CW), F32)],
        input_output_aliases={2: 0},
        compiler_params=_params(("parallel", "arbitrary")),
    )(p, dcat, dp, conv_w, conv_b)


_SQRT_HALF = 0.7071067811865476
_INV_SQRT_2PI = 0.3989422804014327


def _gelu(x):
    return x * (lax.erf(x * _SQRT_HALF) + 1.0) * 0.5


def _gelu_grad(x):
    return 0.5 * (lax.erf(x * _SQRT_HALF) + 1.0) + x * (_INV_SQRT_2PI * jnp.exp(-0.5 * x * x))


def _layernorm_parts(v):
    mu = jnp.mean(v, axis=-1, keepdims=True)
    vc = v - mu
    rstd = lax.rsqrt(jnp.mean(vc * vc, axis=-1, keepdims=True) + NORM_EPS)
    return vc * rstd, rstd


def _tril_mask(L):
    r = lax.broadcasted_iota(jnp.int32, (L, L), 0)
    c = lax.broadcasted_iota(jnp.int32, (L, L), 1)
    return r >= c


def _sgu_fwd(p, ln_g, ln_b, sgu_w, sgu_b, off, width, *, name, tr=512):
    N = p.shape[0]
    G, L = sgu_w.shape[0], sgu_w.shape[1]
    SW = ln_g.shape[1]
    dh = SW // G
    assert off % SW == 0
    ub = off // SW
    tr = _tile(N, tr, L)
    assert tr % L == 0

    def body(u_ref, v_ref, g_ref, beta_ref, w_ref, b_ref, o_ref):
        u = _gelu(u_ref[...])
        vhat, _ = _layernorm_parts(_gelu(v_ref[...]))
        vn = (vhat * g_ref[...] + beta_ref[...]).astype(BF16)
        mask = _tril_mask(L)
        for gi in range(G):
            w = jnp.where(mask, w_ref[gi], 0.0).astype(BF16)
            bias = b_ref[gi]
            cols = slice(gi * dh, (gi + 1) * dh)
            for n in range(tr // L):
                rows = slice(n * L, (n + 1) * L)
                s = jnp.dot(w, vn[rows, cols], preferred_element_type=F32) + bias
                o_ref[rows, cols] = (u[rows, cols] * s).astype(o_ref.dtype)

    def col(k):
        return pl.BlockSpec((tr, SW), lambda i: (i, k))

    vec = pl.BlockSpec((1, SW), lambda i: (0, 0))
    return pl.pallas_call(
        body, name=name, grid=(N // tr,),
        in_specs=[col(ub), col(ub + 1), vec, vec, pl.BlockSpec((G, L, L), lambda i: (0, 0, 0)),
                  pl.BlockSpec((G, L, 1), lambda i: (0, 0, 0))],
        out_specs=col(0), out_shape=jax.ShapeDtypeStruct((N, width), BF16),
        compiler_params=_params(("parallel",)),
    )(p, p, ln_g, ln_b, sgu_w, sgu_b.reshape(G, L, 1))


def _sgu_bwd(p, dcat, ln_g, ln_b, sgu_w, sgu_b, off, *, name, tr=512):
    N = p.shape[0]
    G, L = sgu_w.shape[0], sgu_w.shape[1]
    SW = ln_g.shape[1]
    dh = SW // G
    assert off % SW == 0
    ub = off // SW
    tr = _tile(N, tr, L)

    def compute(i, u_ref, v_ref, dc_ref, g_ref, beta_ref, w_ref, b_ref,
                du_ref, dv_ref, dw_ref, db_ref, dg_ref, dbeta_ref, du_s, dvn_s):
        pu = u_ref[...]
        pv = v_ref[...]
        u = _gelu(pu)
        vhat, rstd = _layernorm_parts(_gelu(pv))
        gg = g_ref[...]
        vn = (vhat * gg + beta_ref[...]).astype(BF16)
        dc = dc_ref[...]
        mask = _tril_mask(L)

        @pl.when(i == 0)
        def _():
            dw_ref[...] = jnp.zeros_like(dw_ref)
            db_ref[...] = jnp.zeros_like(db_ref)
            dg_ref[...] = jnp.zeros_like(dg_ref)
            dbeta_ref[...] = jnp.zeros_like(dbeta_ref)

        for gi in range(G):
            w = jnp.where(mask, w_ref[gi], 0.0).astype(BF16)
            bias = b_ref[gi]
            cols = slice(gi * dh, (gi + 1) * dh)
            dw_acc = jnp.zeros((L, L), F32)
            db_acc = jnp.zeros((L, 1), F32)
            for n in range(tr // L):
                rows = slice(n * L, (n + 1) * L)
                vb = vn[rows, cols]
                s = jnp.dot(w, vb, preferred_element_type=F32) + bias
                du_s[rows, cols] = dc[rows, cols] * s
                ds = dc[rows, cols] * u[rows, cols]
                db_acc += jnp.sum(ds, axis=1, keepdims=True)
                dsb = ds.astype(BF16)
                dw_acc += lax.dot_general(dsb, vb, (((1,), (1,)), ((), ())), preferred_element_type=F32)
                dvn_s[rows, cols] = lax.dot_general(w, dsb, (((0,), (0,)), ((), ())),
                                                    preferred_element_type=F32)
            dw_ref[gi] += jnp.where(mask, dw_acc, 0.0)
            db_ref[gi] += db_acc

        dvn = dvn_s[...]
        dg_ref[...] += jnp.sum(dvn * vhat, axis=0, keepdims=True)
        dbeta_ref[...] += jnp.sum(dvn, axis=0, keepdims=True)
        dvh = dvn * gg
        dv = rstd * (dvh - jnp.mean(dvh, axis=-1, keepdims=True)
                     - vhat * jnp.mean(dvh * vhat, axis=-1, keepdims=True))
        dv_ref[...] = (dv * _gelu_grad(pv)).astype(dv_ref.dtype)
        du_ref[...] = (du_s[...] * _gelu_grad(pu)).astype(du_ref.dtype)

    def body(u_ref, v_ref, dc_ref, g_ref, beta_ref, w_ref, b_ref,
             dp_ref, dw_ref, db_ref, dg_ref, dbeta_ref, du_s, dvn_s, dv_s):
        i = pl.program_id(0)
        half = pl.program_id(1)

        @pl.when(half == 0)
        def _():
            compute(i, u_ref, v_ref, dc_ref, g_ref, beta_ref, w_ref, b_ref,
                    dp_ref, dv_s, dw_ref, db_ref, dg_ref, dbeta_ref, du_s, dvn_s)

        @pl.when(half == 1)
        def _():
            dp_ref[...] = dv_s[...]

    def col(k):
        return pl.BlockSpec((tr, SW), lambda i, half: (i, k))

    vec = pl.BlockSpec((1, SW), lambda i, half: (0, 0))
    wspec = pl.BlockSpec((G, L, L), lambda i, half: (0, 0, 0))
    bspec = pl.BlockSpec((G, L, 1), lambda i, half: (0, 0, 0))
    return pl.pallas_call(
        body, name=name, grid=(N // tr, 2),
        in_specs=[col(ub), col(ub + 1), col(0), vec, vec, wspec, bspec],
        out_specs=[pl.BlockSpec((tr, SW), lambda i, half: (i, ub + half)), wspec, bspec, vec, vec],
        out_shape=[jax.ShapeDtypeStruct((N, p.shape[1]), BF16),
                   jax.ShapeDtypeStruct((G, L, L), F32), jax.ShapeDtypeStruct((G, L, 1), F32),
                   jax.ShapeDtypeStruct((1, SW), F32), jax.ShapeDtypeStruct((1, SW), F32)],
        scratch_shapes=[pltpu.VMEM((tr, SW), F32), pltpu.VMEM((tr, SW), F32), pltpu.VMEM((tr, SW), BF16)],
        compiler_params=_params(("arbitrary", "arbitrary")),
    )(p, p, dcat, ln_g, ln_b, sgu_w, sgu_b.reshape(G, L, 1))


def _log_sigmoid_pair(z):
    ls = jnp.minimum(z, 0.0) - jnp.log(1.0 + jnp.exp(-jnp.abs(z)))
    return ls, ls - z


def _split_dot(x, m):
    hi = x.astype(BF16)
    lo = (x - hi.astype(F32)).astype(BF16)
    return (jnp.dot(hi, m, preferred_element_type=F32) + jnp.dot(lo, m, preferred_element_type=F32))


def _stacked_cumsum(xs, m):
    his = [x.astype(BF16) for x in xs]
    los = [(x - h.astype(F32)).astype(BF16) for x, h in zip(xs, his)]
    n, rows = len(xs), xs[0].shape[0]
    s = jnp.dot(jnp.concatenate(his + los, axis=0), m, preferred_element_type=F32)
    return [s[i * rows:(i + 1) * rows] + s[(n + i) * rows:(n + i + 1) * rows] for i in range(n)]


def _att_logits(qts, kbs, strict):
    ls, lks = [], []
    for qt, kb in zip(qts, kbs):
        l, lk = _log_sigmoid_pair(lax.dot_general(qt, kb, (((1,), (1,)), ((), ())), preferred_element_type=F32))
        ls.append(l)
        lks.append(lk if strict is None else jnp.where(strict, lk, 0.0))
    return ls, lks


def _att_tiles(TB):
    r = lax.broadcasted_iota(jnp.int32, (TB, TB), 0)
    c = lax.broadcasted_iota(jnp.int32, (TB, TB), 1)
    return r, c


def _att_weights(qt, kb, strict, later, carry):
    z = lax.dot_general(qt, kb, (((1,), (1,)), ((), ())), preferred_element_type=F32)
    ls, lk = _log_sigmoid_pair(z)
    if strict is not None:
        lk = jnp.where(strict, lk, 0.0)
    suffix = _split_dot(lk, later) + carry
    a = jnp.exp(ls + suffix)
    if strict is not None:
        a = jnp.where(strict, a, 0.0)
    return ls, lk, suffix, a


HEADS_PER_BLOCK = LANES // SB_DH
ATT_GROUPS = 2
HP = ATT_GROUPS * HEADS_PER_BLOCK
ATT_SCALE = 1.0 / math.sqrt(SB_DH)


def _head_col(hh, part):
    return (hh // HEADS_PER_BLOCK) * 3 * LANES + part * LANES + (hh % HEADS_PER_BLOCK) * SB_DH


def _stage_heads(src_ref, part, dst_ref, T, scale=None):
    rows = _tile(T, 256, SUBLANES)

    def chunk(n, _):
        r0 = pl.multiple_of(n * rows, rows)
        for hh in range(HP):
            col = hh * SB_DH if part is None else _head_col(hh, part)
            x = src_ref[pl.ds(r0, rows), col:col + SB_DH]
            if scale is not None:
                x = x * scale
            dst_ref[hh, pl.ds(r0, rows), :] = x.astype(dst_ref.dtype)
        return 0

    lax.fori_loop(0, T // rows, chunk, 0)


def _attn_fwd(p, cat, B, T, coff, *, name):
    dh = SB_DH
    owidth = ATT_GROUPS * LANES
    nsteps = (cat.shape[1] - coff) // owidth
    TB = _tile(T, ATT_BLOCK)
    nb = T // TB
    assert nb <= LANES and coff % owidth == 0 and (cat.shape[1] - coff) % owidth == 0
    cbase = coff // owidth

    def body(p_ref, cat_ref, o_ref, c_ref, q_ref, k_ref, v_ref):
        _stage_heads(p_ref, 0, q_ref, T, ATT_SCALE)
        _stage_heads(p_ref, 1, k_ref, T)
        _stage_heads(p_ref, 2, v_ref, T)
        r, c = _att_tiles(TB)
        strict = c < r
        later = (r > c).astype(BF16)
        lane = lax.broadcasted_iota(jnp.int32, (TB, LANES), 1)

        def tiles(qts, j, carries, mask):
            k0 = pl.multiple_of(j * TB, TB)
            ls, lks = _att_logits(qts, [k_ref[hh, pl.ds(k0, TB), :] for hh in range(HP)], mask)
            sums = _stacked_cumsum(lks, later)
            out = []
            for hh in range(HP):
                suffix = sums[hh] + carries[hh]
                a = jnp.exp(ls[hh] + suffix)
                if mask is not None:
                    a = jnp.where(mask, a, 0.0)
                pv = jnp.dot(a.astype(BF16), v_ref[hh, pl.ds(k0, TB), :], preferred_element_type=F32)
                out.append((pv, suffix[:, 0:1] + lks[hh][:, 0:1]))
            return out

        def qblock(i, _):
            q0 = pl.multiple_of(i * TB, TB)
            qts = [q_ref[hh, pl.ds(q0, TB), :] for hh in range(HP)]
            state = []
            for pv, carry in tiles(qts, i, [jnp.zeros((TB, 1), F32)] * HP, strict):
                state += [pv, carry, jnp.zeros((TB, LANES), F32)]

            def kblock(jj, st):
                j = i - jj
                out = []
                for hh, (pv, new_carry) in enumerate(tiles(qts, j, st[1::3], None)):
                    acc, carry, cm = st[3 * hh:3 * hh + 3]
                    out += [acc + pv, new_carry, jnp.where(lane == j, carry, cm)]
                return tuple(out)

            st = lax.fori_loop(1, i + 1, kblock, tuple(state))
            for hh in range(HP):
                o_ref[pl.ds(q0, TB), hh * dh:(hh + 1) * dh] = st[3 * hh].astype(o_ref.dtype)
                c_ref[hh, pl.ds(q0, TB), :] = st[3 * hh + 2]
            return 0

        lax.fori_loop(0, nb, qblock, 0)

    staged = pltpu.VMEM((HP, T, dh), BF16)
    return pl.pallas_call(
        body, name=name, grid=(B, nsteps),
        in_specs=[pl.BlockSpec((T, 3 * owidth), lambda b, s: (b, s)), _ANY],
        out_specs=[pl.BlockSpec((T, owidth), lambda b, s: (b, cbase + s)),
                   pl.BlockSpec((HP, T, LANES), lambda b, s: (b * nsteps + s, 0, 0))],
        out_shape=[jax.ShapeDtypeStruct(cat.shape, cat.dtype),
                   jax.ShapeDtypeStruct((B * nsteps * HP, T, LANES), F32)],
        input_output_aliases={1: 0}, scratch_shapes=[staged, staged, staged],
        compiler_params=_params(("parallel", "parallel")),
    )(p, cat)


def _attn_bwd(p, dcat, carries, dp, B, T, coff, *, name):
    dh = SB_DH
    owidth = ATT_GROUPS * LANES
    nsteps = (dcat.shape[1] - coff) // owidth
    TB = _tile(T, ATT_BLOCK)
    nb = T // TB
    cbase = coff // owidth

    def body(p_ref, d_ref, c_ref, dp_in_ref, dp_ref, q_ref, k_ref, v_ref, do_ref, dk_ref, dv_ref):
        _stage_heads(p_ref, 0, q_ref, T, ATT_SCALE)
        _stage_heads(p_ref, 1, k_ref, T)
        _stage_heads(p_ref, 2, v_ref, T)
        _stage_heads(d_ref, None, do_ref, T)
        r, c = _att_tiles(TB)
        strict = c < r
        later = (r > c).astype(BF16)
        earlier = (r < c).astype(BF16)
        lane = lax.broadcasted_iota(jnp.int32, (TB, LANES), 1)
        dk_ref[...] = jnp.zeros_like(dk_ref)
        dv_ref[...] = jnp.zeros_like(dv_ref)

        def tiles(qts, dots, cms, j, befores, mask):
            k0 = pl.multiple_of(j * TB, TB)
            kbs = [k_ref[hh, pl.ds(k0, TB), :] for hh in range(HP)]
            ls, lks = _att_logits(qts, kbs, mask)
            sums = _stacked_cumsum(lks, later)
            gls = []
            for hh in range(HP):
                carry = jnp.sum(jnp.where(lane == j, cms[hh], 0.0), axis=1, keepdims=True)
                a = jnp.exp(ls[hh] + sums[hh] + carry)
                if mask is not None:
                    a = jnp.where(mask, a, 0.0)
                dv_ref[hh, pl.ds(k0, TB), :] += lax.dot_general(a.astype(BF16), dots[hh], (((0,), (0,)), ((), ())),
                                                                preferred_element_type=F32)
                da = lax.dot_general(dots[hh], v_ref[hh, pl.ds(k0, TB), :], (((1,), (1,)), ((), ())),
                                     preferred_element_type=F32)
                gls.append(a * da)
            pres = _stacked_cumsum(gls, earlier)
            out = []
            for hh in range(HP):
                prefix = pres[hh] + befores[hh]
                dz = gls[hh] - jnp.exp(ls[hh]) * (gls[hh] + prefix)
                if mask is not None:
                    dz = jnp.where(mask, dz, 0.0)
                dzb = dz.astype(BF16)
                dk_ref[hh, pl.ds(k0, TB), :] += lax.dot_general(dzb, qts[hh], (((0,), (0,)), ((), ())),
                                                                preferred_element_type=F32)
                out.append((jnp.dot(dzb, kbs[hh], preferred_element_type=F32),
                            prefix[:, TB - 1:TB] + gls[hh][:, TB - 1:TB]))
            return out

        def qblock(i, _):
            q0 = pl.multiple_of(i * TB, TB)
            qts = [q_ref[hh, pl.ds(q0, TB), :] for hh in range(HP)]
            dots = [do_ref[hh, pl.ds(q0, TB), :] for hh in range(HP)]
            cms = [c_ref[hh, pl.ds(q0, TB), :] for hh in range(HP)]

            def kblock(j, st):
                out = []
                for hh, (part, new_before) in enumerate(tiles(qts, dots, cms, j, st[1::2], None)):
                    out += [st[2 * hh] + part, new_before]
                return tuple(out)

            st = lax.fori_loop(0, i, kblock, (jnp.zeros((TB, dh), F32), jnp.zeros((TB, 1), F32)) * HP)
            for hh, (part, _) in enumerate(tiles(qts, dots, cms, i, st[1::2], strict)):
                dq = (st[2 * hh] + part) * ATT_SCALE
                dp_ref[pl.ds(q0, TB), _head_col(hh, 0):_head_col(hh, 0) + dh] = dq.astype(dp_ref.dtype)
            return 0

        lax.fori_loop(0, nb, qblock, 0)

        def write_back(n, _):
            r0 = pl.multiple_of(n * TB, TB)
            for hh in range(HP):
                dp_ref[pl.ds(r0, TB), _head_col(hh, 1):_head_col(hh, 1) + dh] = (
                    dk_ref[hh, pl.ds(r0, TB), :].astype(dp_ref.dtype))
                dp_ref[pl.ds(r0, TB), _head_col(hh, 2):_head_col(hh, 2) + dh] = (
                    dv_ref[hh, pl.ds(r0, TB), :].astype(dp_ref.dtype))
            return 0

        lax.fori_loop(0, nb, write_back, 0)

    pblk = pl.BlockSpec((T, 3 * owidth), lambda b, s: (b, s))
    staged = pltpu.VMEM((HP, T, dh), BF16)
    accum = pltpu.VMEM((HP, T, dh), F32)
    return pl.pallas_call(
        body, name=name, grid=(B, nsteps),
        in_specs=[pblk, pl.BlockSpec((T, owidth), lambda b, s: (b, cbase + s)),
                  pl.BlockSpec((HP, T, LANES), lambda b, s: (b * nsteps + s, 0, 0)), _ANY],
        out_specs=pblk, out_shape=jax.ShapeDtypeStruct(dp.shape, dp.dtype), input_output_aliases={3: 0},
        scratch_shapes=[staged, staged, staged, staged, accum, accum],
        compiler_params=_params(("parallel", "parallel")),
    )(p, dcat, carries, dp)


def _adamw(w, gparts, m, v, *, name):
    L, R, C = w.shape
    P = gparts[0].shape[0]
    assert len(gparts) == L
    tr = _tile(R, max(SUBLANES, (1 << 19) // (C * P)), SUBLANES)

    def body(*refs):
        w_ref, g_refs, (m_ref, v_ref) = refs[0], refs[1:1 + L], refs[1 + L:3 + L]
        go_ref, d_ref, mo_ref, vo_ref = refs[3 + L:]
        layer = pl.program_id(0)

        def update(g_ref):
            g = g_ref[0].astype(F32)
            for i in range(1, P):
                g = g + g_ref[i].astype(F32)
            m2 = ADAM_B1 * m_ref[...] + (1.0 - ADAM_B1) * g
            v2 = ADAM_B2 * v_ref[...] + (1.0 - ADAM_B2) * (g * g)
            m_hat = m2 / ADAM_C1
            v_hat = v2 / ADAM_C2
            go_ref[...] = g
            d_ref[...] = -ADAM_LR * (m_hat / (jnp.sqrt(v_hat) + ADAM_EPS) + ADAM_WD * w_ref[...])
            mo_ref[...] = m2
            vo_ref[...] = v2

        for l in range(L):
            pl.when(layer == l)(functools.partial(update, g_refs[l]))

    row = pl.BlockSpec((None, tr, C), lambda l, i: (l, i, 0))

    def part(mine):
        return pl.BlockSpec((P, tr, C), lambda l, i: (0, jnp.where(l == mine, i, 0), 0))

    shp = jax.ShapeDtypeStruct((L, R, C), F32)
    return pl.pallas_call(
        body, name=name, grid=(L, R // tr),
        in_specs=[row] + [part(l) for l in range(L)] + [row, row],
        out_specs=[row] * 4, out_shape=[shp] * 4, compiler_params=_params(("arbitrary", "arbitrary")),
    )(w, *gparts, m, v)


def _my_index():
    return 4 * lax.axis_index("x") + 2 * lax.axis_index("y") + lax.axis_index("c")


def _exchange(arrs, gather, *, name):
    n = len(arrs)

    def body(*refs):
        ins, outs = refs[:n], refs[n:2 * n]
        send_sems, recv_sems, local_sems = refs[2 * n:]
        x, y, c = lax.axis_index("x"), lax.axis_index("y"), lax.axis_index("c")
        me = 4 * x + 2 * y + c
        remote, local = [], []
        for a in range(n):
            own = ins[a] if gather[a] else ins[a].at[me]
            cp = pltpu.make_async_copy(own, outs[a].at[me], local_sems.at[a])
            cp.start()
            local.append(cp)
            for k in range(1, N_DEV):
                px = 1 - x if k & 4 else x
                py = 1 - y if k & 2 else y
                pc = 1 - c if k & 1 else c
                src = ins[a] if gather[a] else ins[a].at[4 * px + 2 * py + pc]
                cp = pltpu.make_async_remote_copy(
                    src_ref=src, dst_ref=outs[a].at[me],
                    send_sem=send_sems.at[a, k - 1], recv_sem=recv_sems.at[a, k - 1],
                    device_id=(px, py, pc), device_id_type=pl.DeviceIdType.MESH)
                cp.start()
                remote.append(cp)
        for cp in remote:
            cp.wait()
        for cp in local:
            cp.wait()

    hbm = pl.BlockSpec(memory_space=pltpu.HBM)
    out_shape = [jax.ShapeDtypeStruct(((N_DEV,) + a.shape) if g else a.shape, a.dtype)
                 for a, g in zip(arrs, gather)]
    return pl.pallas_call(
        body, name=name, in_specs=[hbm] * n, out_specs=[hbm] * n, out_shape=out_shape,
        scratch_shapes=[pltpu.SemaphoreType.DMA((n, N_DEV - 1)), pltpu.SemaphoreType.DMA((n, N_DEV - 1)),
                        pltpu.SemaphoreType.DMA((n,))],
    )(*arrs)


_HBM = pl.BlockSpec(memory_space=pltpu.HBM)


def _other_chips(x, y):
    return [(1 - x, y), (x, 1 - y), (1 - x, 1 - y)]


def _gather_two_level(arrs, *, name):
    n = len(arrs)

    def body(*refs):
        ins, outs = refs[:n], refs[n:2 * n]
        send_sems, recv_sems, local_sems = refs[2 * n:]
        x, y, c = lax.axis_index("x"), lax.axis_index("y"), lax.axis_index("c")
        me, sibling = (x, y, c), (x, y, 1 - c)
        chips = _other_chips(x, y)

        def slot(a, px, py, pc):
            return outs[a].at[4 * px + 2 * py + pc]

        def copy(a, k, block, to, src=None):
            return pltpu.make_async_remote_copy(
                src_ref=slot(a, *block) if src is None else src, dst_ref=slot(a, *block),
                send_sem=send_sems.at[a, k], recv_sem=recv_sems.at[a, k],
                device_id=to, device_id_type=pl.DeviceIdType.MESH)

        local, sends = [], []
        for a in range(n):
            cp = pltpu.make_async_copy(ins[a], slot(a, *me), local_sems.at[a])
            cp.start()
            local.append(cp)
            first = [copy(a, 0, me, sibling, src=ins[a])]
            first += [copy(a, 1 + j, me, (*chip, c), src=ins[a]) for j, chip in enumerate(chips)]
            for cp in first:
                cp.start()
            sends += first
        for j, chip in enumerate(chips):
            for a in range(n):
                copy(a, 1 + j, (*chip, c), me).wait_recv()
                cp = copy(a, 4 + j, (*chip, c), sibling)
                cp.start()
                sends.append(cp)
        for a in range(n):
            copy(a, 0, sibling, me).wait_recv()
            for j, chip in enumerate(chips):
                copy(a, 4 + j, (*chip, 1 - c), me).wait_recv()
        for cp in sends:
            cp.wait_send()
        for cp in local:
            cp.wait()

    return pl.pallas_call(
        body, name=name, in_specs=[_HBM] * n, out_specs=[_HBM] * n,
        out_shape=[jax.ShapeDtypeStruct((N_DEV,) + a.shape, a.dtype) for a in arrs],
        scratch_shapes=[pltpu.SemaphoreType.DMA((n, N_DEV - 1)), pltpu.SemaphoreType.DMA((n, N_DEV - 1)),
                        pltpu.SemaphoreType.DMA((n,))],
    )(*arrs)


_SEM = pl.BlockSpec(memory_space=pltpu.SEMAPHORE)
_SPLIT_COPY = pltpu.SideEffectType.DATAFLOW_SIDE_EFFECTING


def _peers(x, y, c):
    return [((1 - x if k & 4 else x), (1 - y if k & 2 else y), (1 - c if k & 1 else c)) for k in range(1, N_DEV)]


_SPLIT_SEMS = 2 * (N_DEV - 1) + 1


def _split_sems(sems, a):
    mine = sems[a * _SPLIT_SEMS:(a + 1) * _SPLIT_SEMS]
    return mine[:N_DEV - 1], mine[N_DEV - 1:2 * (N_DEV - 1)], mine[-1]


def _split_src(ref, scatter, index):
    return ref.at[index] if scatter else ref


def _gather_start(arrs, carry, *, name, scatter=False):
    n = len(arrs)
    ns = n * _SPLIT_SEMS

    def body(*refs):
        ins, lands = refs[:n], refs[n:2 * n]
        sems = refs[2 * n + 1:2 * n + 1 + ns]
        x, y, c = lax.axis_index("x"), lax.axis_index("y"), lax.axis_index("c")
        me = 4 * x + 2 * y + c
        for a in range(n):
            send, recv, local = _split_sems(sems, a)
            pltpu.make_async_copy(_split_src(ins[a], scatter, me), lands[a].at[me], local).start()
            for k, (px, py, pc) in enumerate(_peers(x, y, c)):
                pltpu.make_async_remote_copy(
                    src_ref=_split_src(ins[a], scatter, 4 * px + 2 * py + pc), dst_ref=lands[a].at[me],
                    send_sem=send[k], recv_sem=recv[k],
                    device_id=(px, py, pc), device_id_type=pl.DeviceIdType.MESH).start()

    lands = [lax.empty(a.shape if scatter else (N_DEV,) + a.shape, a.dtype) for a in arrs]
    operands = [pltpu.with_memory_space_constraint(a, pltpu.HBM) for a in list(arrs) + lands + [carry]]
    outs = pl.pallas_call(
        body, name=name, in_specs=[_HBM] * (2 * n + 1), out_specs=[_SEM] * ns + [_HBM] * (2 * n + 1),
        out_shape=[pltpu.SemaphoreType.DMA(())] * ns + [pltpu.HBM(a.shape, a.dtype) for a in operands],
        input_output_aliases={i: ns + i for i in range(2 * n + 1)},
        compiler_params=pltpu.CompilerParams(has_side_effects=_SPLIT_COPY),
    )(*operands)
    return tuple(outs[:-1]), outs[-1]


def _gather_wait(handles, after, *, name, scatter=False):
    n = len(handles) // (_SPLIT_SEMS + 2)
    ns = n * _SPLIT_SEMS
    sems, thru = handles[:ns], handles[ns:]

    def body(*refs):
        ins, lands = refs[:n], refs[n:2 * n]
        sems = refs[2 * n:2 * n + ns]
        x, y, c = lax.axis_index("x"), lax.axis_index("y"), lax.axis_index("c")
        me = 4 * x + 2 * y + c
        for a in range(n):
            send, recv, local = _split_sems(sems, a)
            src = _split_src(ins[a], scatter, me)
            pltpu.make_async_copy(src, lands[a].at[me], local).wait()
            for k, peer in enumerate(_peers(x, y, c)):
                cp = pltpu.make_async_remote_copy(
                    src_ref=src, dst_ref=lands[a].at[me], send_sem=send[k], recv_sem=recv[k],
                    device_id=peer, device_id_type=pl.DeviceIdType.MESH)
                cp.wait_send()
                cp.wait_recv()

    outs = pl.pallas_call(
        body, name=name, in_specs=[_HBM] * (2 * n) + [_SEM] * ns + [_ANY], out_specs=[_HBM] * (2 * n),
        out_shape=[pltpu.HBM(a.shape, a.dtype) for a in thru],
        input_output_aliases={i: i for i in range(2 * n)},
        compiler_params=pltpu.CompilerParams(has_side_effects=_SPLIT_COPY),
    )(*thru, *sems, after)
    return outs[n:]


def _sibling_swap(arrs, *, name):
    n = len(arrs)
    nchip = N_DEV // 2

    def body(*refs):
        ins, outs = refs[:n], refs[n:2 * n]
        send_sems, recv_sems = refs[2 * n:]
        x, y, c = lax.axis_index("x"), lax.axis_index("y"), lax.axis_index("c")
        copies = []
        for a in range(n):
            for k in range(nchip):
                cp = pltpu.make_async_remote_copy(
                    src_ref=ins[a].at[2 * k + 1 - c], dst_ref=outs[a].at[k],
                    send_sem=send_sems.at[a, k], recv_sem=recv_sems.at[a, k],
                    device_id=(x, y, 1 - c), device_id_type=pl.DeviceIdType.MESH)
                cp.start()
                copies.append(cp)
        for cp in copies:
            cp.wait()

    return pl.pallas_call(
        body, name=name, in_specs=[_HBM] * n, out_specs=[_HBM] * n,
        out_shape=[jax.ShapeDtypeStruct((nchip,) + a.shape[1:], a.dtype) for a in arrs],
        scratch_shapes=[pltpu.SemaphoreType.DMA((n, nchip)), pltpu.SemaphoreType.DMA((n, nchip))],
    )(*arrs)


def _chip_exchange(arrs, *, name):
    n = len(arrs)

    def body(*refs):
        ins, outs = refs[:n], refs[n:2 * n]
        send_sems, recv_sems, local_sems = refs[2 * n:]
        x, y, c = lax.axis_index("x"), lax.axis_index("y"), lax.axis_index("c")
        mine = 2 * x + y
        copies = []
        for a in range(n):
            cp = pltpu.make_async_copy(ins[a].at[mine], outs[a].at[mine], local_sems.at[a])
            cp.start()
            copies.append(cp)
            for j, (px, py) in enumerate(_other_chips(x, y)):
                cp = pltpu.make_async_remote_copy(
                    src_ref=ins[a].at[2 * px + py], dst_ref=outs[a].at[mine],
                    send_sem=send_sems.at[a, j], recv_sem=recv_sems.at[a, j],
                    device_id=(px, py, c), device_id_type=pl.DeviceIdType.MESH)
                cp.start()
                copies.append(cp)
        for cp in copies:
            cp.wait()

    return pl.pallas_call(
        body, name=name, in_specs=[_HBM] * n, out_specs=[_HBM] * n,
        out_shape=[jax.ShapeDtypeStruct(a.shape, a.dtype) for a in arrs],
        scratch_shapes=[pltpu.SemaphoreType.DMA((n, 3)), pltpu.SemaphoreType.DMA((n, 3)),
                        pltpu.SemaphoreType.DMA((n,))],
    )(*arrs)


def _pair_sum(a, b, *, name):
    P, R, C = a.shape
    tr = _tile(R, max(SUBLANES, (1 << 19) // C), SUBLANES)

    def body(a_ref, b_ref, o_ref):
        o_ref[...] = (a_ref[...].astype(F32) + b_ref[...].astype(F32)).astype(o_ref.dtype)

    blk = pl.BlockSpec((1, tr, C), lambda p, i: (p, i, 0))
    return pl.pallas_call(
        body, name=name, grid=(P, R // tr), in_specs=[blk, blk], out_specs=blk,
        out_shape=jax.ShapeDtypeStruct(a.shape, a.dtype), compiler_params=_params(("parallel", "parallel")),
    )(a, b)


def _group_in_cols(w, lead):
    X = (w.shape[-1] - lead) // 3
    g = w[..., lead:].reshape(w.shape[:-1] + (3, X // LANES, LANES))
    g = jnp.swapaxes(g, -3, -2).reshape(w.shape[:-1] + (3 * X,))
    return jnp.concatenate([g, w[..., :lead]], axis=-1)


def _ungroup_in_cols(w, lead):
    X = (w.shape[-1] - lead) // 3
    g = w[..., :3 * X].reshape(w.shape[:-1] + (X // LANES, 3, LANES))
    g = jnp.swapaxes(g, -3, -2).reshape(w.shape[:-1] + (3 * X,))
    return jnp.concatenate([w[..., 3 * X:], g], axis=-1)


def _mlp_fwd(h, g, W, tag):
    hn = _rmsnorm(h, g, name=f"mlp_norm_{tag}")
    w1 = _weight(W, f"mlp_w1_{tag}", hn)
    act = _matmul(hn, w1, name=f"mlp_up_{tag}", epilogue=_ep_relu2, out_dtypes=(BF16,), b_chunks=True, tm=2048)
    out = _matmul(act, _weight(W, f"mlp_w2_{tag}", act), name=f"mlp_down_{tag}", extras=(h,), epilogue=_ep_add)
    return out, (hn, act, _chunks_to_cols(w1))


def _mlp_bwd(dout, dout_b, h, g, w2, saved, tag, sent):
    hn, act, w1 = saved
    dw2 = _matmul(act, dout_b, ta=True, name=f"mlp_dw2_{tag}", out_dtypes=(GRAD_WIRE,))
    dout_b = sent(dout_b, {f"mlp_w2_{tag}": dw2})
    dz = _matmul(dout_b, w2, tb=True, name=f"mlp_dact_{tag}", extras=(act,), epilogue=_ep_relu2_bwd,
                 out_dtypes=(BF16,))
    dw1 = _matmul(hn, dz, ta=True, name=f"mlp_dw1_{tag}", out_dtypes=(GRAD_WIRE,), out_chunks=N_DEV, tk=4096)
    dz = sent(dz, {f"mlp_w1_{tag}": dw1})
    dhn = _matmul(dz, w1, tb=True, name=f"mlp_dhn_{tag}")
    dh, dh_b, dg = _rmsnorm_bwd(dhn, h, g, dout, name=f"mlp_norm_bwd_{tag}")
    return dh, dh_b, dg, dw1, dw2


class _Lazy:
    def __init__(self, handles, finish, name):
        self.handles, self.finish, self.name, self.done = handles, finish, name, None

    def take(self, after):
        if self.done is None:
            self.done = self.finish(_gather_wait(self.handles, after, name=self.name))
        return self.done


def _weight(W, n, after):
    if isinstance(W[n], _Lazy):
        W.update(W[n].take(after))
    return W[n]


def _local_step(x, target, W, emit=None):
    B, T, D = x.shape
    N = B * T
    G = {}
    row = lambda vec: vec.reshape(1, -1)

    def sent(nxt, grads):
        return nxt if emit is None else emit(grads, nxt)

    PW = W["pool_w"].shape[0] * W["pool_w"].shape[1]
    CW = W["conv_b"].shape[-1]
    SW = W["sgu_norm_g"].shape[-1]
    HW = W["att_width"]

    h0 = x.reshape(N, D)
    xn0 = _rmsnorm(h0, row(W["mix_norm_g"][0]), name="mix_norm_0")
    p0 = _matmul(xn0, W["ab_w_in"], name="ab_in")
    cat0 = _pool_fwd(p0, W["pool_w"], W["pool_scale"], B, T, 3 * CW, PW + CW, name="pool_fwd")
    cat0 = _conv_fwd(p0, cat0, W["conv_w"], row(W["conv_b"]), B, T, PW, name="conv_fwd")
    h1 = _matmul(cat0, W["ab_w_out"], name="ab_out", extras=(h0,), epilogue=_ep_add)
    h2, mlp0 = _mlp_fwd(h1, row(W["mlp_norm_g"][0]), W, 0)
    xn1 = _rmsnorm(h2, row(W["mix_norm_g"][1]), name="mix_norm_1")
    p1 = _matmul(xn1, _weight(W, "cd_w_in", xn1), name="cd_in")
    ln_g, ln_b = row(W["sgu_norm_g"]), row(W["sgu_norm_b"])
    cat1 = _sgu_fwd(p1, ln_g, ln_b, W["sgu_w"], W["sgu_b"], 3 * HW, SW + HW, name="sgu_fwd")
    cat1, att_carries = _attn_fwd(p1, cat1, B, T, SW, name="attn_fwd")
    h3 = _matmul(cat1, W["cd_w_out"], name="cd_out", extras=(h2,), epilogue=_ep_add)
    h4, mlp1 = _mlp_fwd(h3, row(W["mlp_norm_g"][1]), W, 1)

    loss, dh4, dh4_b, G["final_norm_g"] = _final_loss(h4, row(W["final_norm_g"]), target.reshape(N, D),
                                                      name="final_loss")

    dh3, dh3_b, dmlp_g1, dw1_1, dw2_1 = _mlp_bwd(dh4, dh4_b, h3, row(W["mlp_norm_g"][1]), W["mlp_w2_1"], mlp1, 1,
                                                 sent)
    G["cd_w_out"] = _matmul(cat1, dh3_b, ta=True, name="cd_out_dw", out_dtypes=(GRAD_WIRE,))[None]
    dh3_b = sent(dh3_b, {"cd_w_out": G["cd_w_out"][0]})
    dcat1 = _matmul(dh3_b, W["cd_w_out"], tb=True, name="cd_out_dx")
    dp1, G["sgu_w"], dsgu_b, G["sgu_norm_g"], G["sgu_norm_b"] = _sgu_bwd(
        p1, dcat1, ln_g, ln_b, W["sgu_w"], W["sgu_b"], 3 * HW, name="sgu_bwd")
    G["sgu_b"] = dsgu_b.reshape(W["sgu_b"].shape)
    dp1 = _attn_bwd(p1, dcat1, att_carries, dp1, B, T, SW, name="attn_bwd")
    G["cd_w_in"] = _matmul(xn1, dp1, ta=True, name="cd_in_dw", out_dtypes=(GRAD_WIRE,))[None]
    dp1 = sent(dp1, {"cd_w_in": G["cd_w_in"][0]})
    dxn1 = _matmul(dp1, W["cd_w_in"], tb=True, name="cd_in_dx")
    dh2, dh2_b, dmix_g1 = _rmsnorm_bwd(dxn1, h2, row(W["mix_norm_g"][1]), dh3, name="mix_norm_bwd_1")

    dh1, dh1_b, dmlp_g0, dw1_0, dw2_0 = _mlp_bwd(dh2, dh2_b, h1, row(W["mlp_norm_g"][0]), W["mlp_w2_0"], mlp0, 0,
                                                 sent)
    G["ab_w_out"] = _matmul(cat0, dh1_b, ta=True, name="ab_out_dw", out_dtypes=(GRAD_WIRE,))[None]
    dh1_b = sent(dh1_b, {"ab_w_out": G["ab_w_out"][0]})
    dcat0 = _matmul(dh1_b, W["ab_w_out"], tb=True, name="ab_out_dx")
    dp0, G["pool_w"], dps = _pool_bwd(p0, dcat0, W["pool_w"], W["pool_scale"], B, T, 3 * CW, name="pool_bwd")
    G["pool_scale"] = dps.reshape(W["pool_scale"].shape)
    dp0, G["conv_w"], dcb = _conv_bwd(p0, dcat0, dp0, W["conv_w"], row(W["conv_b"]), B, T, PW, name="conv_bwd")
    G["conv_b"] = dcb.reshape(-1)
    G["ab_w_in"] = _matmul(xn0, dp0, ta=True, name="ab_in_dw", out_dtypes=(GRAD_WIRE,))[None]
    dp0 = sent(dp0, {"ab_w_in": G["ab_w_in"][0]})
    dxn0 = _matmul(dp0, W["ab_w_in"], tb=True, name="ab_in_dx")
    dx, _, dmix_g0 = _rmsnorm_bwd(dxn0, h0, row(W["mix_norm_g"][0]), dh1, name="mix_norm_bwd_0")

    G["mix_norm_g"] = jnp.concatenate([dmix_g0, dmix_g1], axis=0)
    G["mlp_norm_g"] = jnp.concatenate([dmlp_g0, dmlp_g1], axis=0)
    G["mlp_w1"] = jnp.stack([_chunks_to_cols(dw1_0), _chunks_to_cols(dw1_1)])
    G["mlp_w2"] = jnp.stack([dw2_0, dw2_1])
    G["final_norm_g"] = G["final_norm_g"].reshape(-1)
    G["sgu_norm_g"] = G["sgu_norm_g"].reshape(-1)
    G["sgu_norm_b"] = G["sgu_norm_b"].reshape(-1)
    return loss[0, 0], dx.reshape(B, T, D), G


_NAMES = ["mix_norm_g", "mlp_norm_g", "ab_w_in", "pool_w", "pool_scale", "conv_w", "conv_b", "ab_w_out",
          "cd_w_in", "sgu_norm_g", "sgu_norm_b", "sgu_w", "sgu_b", "cd_w_out", "mlp_w1", "mlp_w2",
          "final_norm_g"]
_COL_SHARDED = ["ab_w_in", "cd_w_in", "mlp_w1"]
_ROW_SHARDED = ["ab_w_out", "cd_w_out", "mlp_w2"]
_SMALL_SHARDED = ["conv_w", "sgu_norm_g", "sgu_norm_b"]
_REPLICATED = ["mix_norm_g", "mlp_norm_g", "pool_w", "pool_scale", "conv_b", "sgu_w", "sgu_b", "final_norm_g"]


def _pad_rows(a2d, mult=SUBLANES):
    pad = (-a2d.shape[0]) % mult
    return jnp.pad(a2d, ((0, pad), (0, 0))) if pad else a2d


def _pack(arrays):
    return _pad_rows(jnp.concatenate([a.reshape(-1, LANES) for a in arrays], axis=0))


def _unpack(packed, shapes):
    out, r = [], 0
    for s in shapes:
        n = math.prod(s) // LANES
        out.append(packed[r:r + n].reshape(s))
        r += n
    return out


def _small_shard_pack(arrays):
    rows = [jnp.pad(a.reshape(-1, a.shape[-1]), ((0, 0), (0, LANES - a.shape[-1]))) for a in arrays]
    return _pad_rows(jnp.concatenate(rows, axis=0))


def _cols_to_chunks(a):
    n = a.shape[-1] // N_DEV
    return jnp.moveaxis(a.reshape(a.shape[:-1] + (N_DEV, n)), -2, 0)


def _chunks_to_cols(a):
    t = jnp.moveaxis(a, 0, -2)
    return t.reshape(t.shape[:-2] + (t.shape[-2] * t.shape[-1],))


def _rows_to_chunks(a):
    r = a.shape[-2] // N_DEV
    return jnp.moveaxis(a.reshape(a.shape[:-2] + (N_DEV, r, a.shape[-1])), -3, 0)


def _chunks_to_rows(a):
    t = jnp.moveaxis(a, 0, -3)
    return t.reshape(t.shape[:-3] + (t.shape[-3] * t.shape[-2], t.shape[-1]))


def kernel(x, mix_norm_g, mlp_norm_g, ab_w_in, pool_w, pool_scale, conv_w, conv_b, ab_w_out, cd_w_in, sgu_norm_g, sgu_norm_b, sgu_w, sgu_b, cd_w_out, mlp_w1, mlp_w2, final_norm_g, loss_target, m_mix_norm_g, m_mlp_norm_g, m_ab_w_in, m_pool_w, m_pool_scale, m_conv_w, m_conv_b, m_ab_w_out, m_cd_w_in, m_sgu_norm_g, m_sgu_norm_b, m_sgu_w, m_sgu_b, m_cd_w_out, m_mlp_w1, m_mlp_w2, m_final_norm_g, v_mix_norm_g, v_mlp_norm_g, v_ab_w_in, v_pool_w, v_pool_scale, v_conv_w, v_conv_b, v_ab_w_out, v_cd_w_in, v_sgu_norm_g, v_sgu_norm_b, v_sgu_w, v_sgu_b, v_cd_w_out, v_mlp_w1, v_mlp_w2, v_final_norm_g):
    w = dict(zip(_NAMES, (mix_norm_g, mlp_norm_g, ab_w_in, pool_w, pool_scale, conv_w, conv_b, ab_w_out, cd_w_in,
                          sgu_norm_g, sgu_norm_b, sgu_w, sgu_b, cd_w_out, mlp_w1, mlp_w2, final_norm_g)))
    m = dict(zip(_NAMES, (m_mix_norm_g, m_mlp_norm_g, m_ab_w_in, m_pool_w, m_pool_scale, m_conv_w, m_conv_b,
                          m_ab_w_out, m_cd_w_in, m_sgu_norm_g, m_sgu_norm_b, m_sgu_w, m_sgu_b, m_cd_w_out,
                          m_mlp_w1, m_mlp_w2, m_final_norm_g)))
    v = dict(zip(_NAMES, (v_mix_norm_g, v_mlp_norm_g, v_ab_w_in, v_pool_w, v_pool_scale, v_conv_w, v_conv_b,
                          v_ab_w_out, v_cd_w_in, v_sgu_norm_g, v_sgu_norm_b, v_sgu_w, v_sgu_b, v_cd_w_out,
                          v_mlp_w1, v_mlp_w2, v_final_norm_g)))
    big = _COL_SHARDED + _ROW_SHARDED
    me = _my_index()

    small_sh = _small_shard_pack([w[n] for n in _SMALL_SHARDED])
    in_lead = {"ab_w_in": pool_w.shape[1] * pool_w.shape[2], "cd_w_in": 2 * sgu_norm_g.shape[-1] * N_DEV}
    shard = {"ab_w_in": ab_w_in[0], "ab_w_out": ab_w_out[0], "cd_w_in": cd_w_in[0], "cd_w_out": cd_w_out[0]}
    for layer in range(mlp_w1.shape[0]):
        shard[f"mlp_w1_{layer}"], shard[f"mlp_w2_{layer}"] = mlp_w1[layer], mlp_w2[layer]
    shard = {n: a.astype(BF16) for n, a in shard.items()}

    def whole(n, g):
        if n.startswith("mlp_w1"):
            return g
        if n in in_lead:
            return _group_in_cols(_chunks_to_cols(g), in_lead[n])
        return _chunks_to_rows(g)

    W = {"att_width": cd_w_out.shape[1] * N_DEV - sgu_norm_g.shape[-1] * N_DEV}
    later = [(["mlp_w2_0"], "mlp0_down"), (["cd_w_in", "cd_w_out"], "cd"), (["mlp_w1_1", "mlp_w2_1"], "mlp1")]

    def start(idx, carry, then=None):
        group, tag = later[idx]
        handles, carry = _gather_start([shard[n] for n in group], carry, name=f"gather_{tag}_start")

        def finish(got):
            done = {n: whole(n, g) for n, g in zip(group, got)}
            if then is not None:
                done[group[0]] = start(then, done[group[0]])
            return done

        W.update({n: _Lazy(handles, finish, f"gather_{tag}_wait") for n in group})
        return carry

    now = ["ab_w_in", "ab_w_out", "mlp_w1_0"]
    gathered = _gather_two_level([shard[n] for n in now] + [small_sh], name="gather_weights")
    W.update({n: whole(n, g) for n, g in zip(now, gathered)})
    W["ab_w_in"] = start(1, start(0, W["ab_w_in"], then=2))
    small_full = gathered[-1]
    r = 0
    for n in _SMALL_SHARDED:
        rows, width = math.prod(w[n].shape[:-1]), w[n].shape[-1]
        W[n] = _chunks_to_cols(small_full[:, r:r + rows, :width])
        r += rows
    for n in _REPLICATED:
        W[n] = w[n]
    for n in ("pool_w", "pool_scale", "sgu_w", "sgu_b"):
        W[n] = W[n][0]

    pending = []

    def emit(grads, carry):
        names = list(grads)
        parts = []
        for n in names:
            if n.startswith("mlp_w1"):
                parts.append(grads[n])
            elif n in in_lead:
                parts.append(_cols_to_chunks(_ungroup_in_cols(grads[n], in_lead[n])))
            else:
                parts.append(_rows_to_chunks(grads[n]))
        handles, carry = _gather_start(parts, carry, name=f"grads_{names[0]}_start", scatter=True)
        pending.append((names, handles))
        return carry

    loss_part, grad_x, G = _local_step(x, loss_target, W, emit)

    landed = {}
    for names, handles in pending:
        got = _gather_wait(handles, grad_x, name=f"grads_{names[0]}_wait", scatter=True)
        landed.update(zip(names, got))
    small_names = _REPLICATED + _SMALL_SHARDED
    small_grads = [G[n].reshape(-1) for n in small_names]
    loss_row = jnp.full((LANES,), loss_part, F32)
    small_pack = _pack(small_grads + [loss_row])
    small_parts = _exchange([small_pack], [True], name="gather_small_grads")[0]

    grads, deltas, new_m, new_v = {}, {}, {}, {}
    for n in big:
        layers = [landed[f"{n}_{l}"] for l in range(w[n].shape[0])] if n.startswith("mlp") else [landed[n]]
        grads[n], deltas[n], new_m[n], new_v[n] = _adamw(w[n], layers, m[n], v[n], name=f"adamw_{n}")

    rep_shapes = [w[n].shape for n in _REPLICATED]
    rep_rows = sum(math.prod(s) for s in rep_shapes) // LANES
    small_sum_shapes = [(G[n].size,) for n in small_names] + [(LANES,)]
    zero_tail = [jnp.zeros((math.prod(s),), F32) for s in small_sum_shapes[len(_REPLICATED):]]
    w_pack = _pack([w[n] for n in _REPLICATED] + zero_tail)
    m_pack = _pack([m[n] for n in _REPLICATED] + zero_tail)
    v_pack = _pack([v[n] for n in _REPLICATED] + zero_tail)
    outs = [o[0] for o in _adamw(w_pack[None], [small_parts], m_pack[None], v_pack[None], name="adamw_small")]
    summed = _unpack(outs[0], small_sum_shapes)
    for i, n in enumerate(_REPLICATED):
        grads[n] = summed[i].reshape(w[n].shape)
    for dst, o in zip((deltas, new_m, new_v), outs[1:]):
        for n, val in zip(_REPLICATED, _unpack(o[:rep_rows], rep_shapes)):
            dst[n] = val
    loss = summed[-1][0]

    shard_g = []
    for i, n in enumerate(_SMALL_SHARDED):
        full = summed[len(_REPLICATED) + i].reshape(w[n].shape[:-1] + (-1,))
        width = w[n].shape[-1]
        shard_g.append(lax.dynamic_slice_in_dim(full, me * width, width, axis=full.ndim - 1))
    g_sh = _small_shard_pack(shard_g)
    m_sh = _small_shard_pack([m[n] for n in _SMALL_SHARDED])
    v_sh = _small_shard_pack([v[n] for n in _SMALL_SHARDED])
    outs = [o[0] for o in _adamw(small_sh[None], [g_sh[None]], m_sh[None], v_sh[None], name="adamw_small_sharded")]
    r = 0
    for n in _SMALL_SHARDED:
        rows, width = math.prod(w[n].shape[:-1]), w[n].shape[-1]
        for dst, o in zip((grads, deltas, new_m, new_v), outs):
            dst[n] = o[r:r + rows, :width].reshape(w[n].shape)
        r += rows

    return (loss, grad_x, *[grads[n] for n in _NAMES], *[deltas[n] for n in _NAMES],
            *[new_m[n] for n in _NAMES], *[new_v[n] for n in _NAMES])
```

```python
import functools
import math

import jax
import jax.numpy as jnp
from jax import lax
from jax.experimental import pallas as pl
from jax.experimental.pallas import tpu as pltpu

F32 = jnp.float32
BF16 = jnp.bfloat16
GRAD_WIRE = jnp.bfloat16

NORM_EPS = 1e-6
ADAM_LR = 0.001
ADAM_B1 = 0.9
ADAM_B2 = 0.999
ADAM_EPS = 1e-08
ADAM_WD = 0.01
ADAM_STEP = 10
ADAM_C1 = 1.0 - ADAM_B1 ** ADAM_STEP
ADAM_C2 = 1.0 - ADAM_B2 ** ADAM_STEP

N_DEV = 8
LANES = 128
SUBLANES = 8
SB_DH = 64
ATT_BLOCK = 256
POOL_LOG_WINDOWS = 4
VMEM_LIMIT = 56 * 1024 * 1024


def _params(semantics=None):
    return pltpu.CompilerParams(dimension_semantics=semantics, vmem_limit_bytes=VMEM_LIMIT)


def _tile(dim, pref, unit=LANES):
    if dim <= pref:
        return dim
    t = (pref // unit) * unit
    while t >= unit:
        if dim % t == 0:
            return t
        t -= unit
    return dim


def _matmul(a, b, *, name, ta=False, tb=False, extras=(), epilogue=None, out_dtypes=(F32,),
            tm=1024, tn=1024, tk=2048, b_chunks=False, out_chunks=0):
    M, K = (a.shape[1], a.shape[0]) if ta else a.shape
    if b_chunks:
        C, rows, n = b.shape
        N, tn, tk = (rows, tn, n) if tb else (C * n, n, tk)
        assert (C * n if tb else rows) == K, (a.shape, b.shape)
    else:
        N = b.shape[0] if tb else b.shape[1]
        assert (b.shape[1] if tb else b.shape[0]) == K, (a.shape, b.shape)
    if out_chunks:
        assert not extras and N % out_chunks == 0
        tn = N // out_chunks
    tm, tn, tk = _tile(M, tm), _tile(N, tn), _tile(K, tk)
    nk = K // tk
    dims = (((0 if ta else 1,), (1 if tb else 0,)), ((), ()))
    ne, no = len(extras), len(out_dtypes)

    def body(*refs):
        a_ref, b_ref = refs[0], refs[1]
        e_refs = refs[2:2 + ne]
        o_refs = refs[2 + ne:2 + ne + no]
        k = pl.program_id(2)

        def part():
            return lax.dot_general(a_ref[...].astype(BF16), b_ref[...].astype(BF16), dims,
                                   preferred_element_type=F32)

        def finish(acc):
            outs = epilogue(acc, *[e[...] for e in e_refs]) if epilogue is not None else (acc,)
            for o_ref, val in zip(o_refs, outs):
                o_ref[...] = val.astype(o_ref.dtype)

        if nk == 1:
            finish(part())
        else:
            acc_ref = refs[-1]

            @pl.when(k == 0)
            def _():
                acc_ref[...] = jnp.zeros_like(acc_ref)

            acc_ref[...] += part()

            @pl.when(k == nk - 1)
            def _():
                finish(acc_ref[...])

    a_spec = (pl.BlockSpec((tk, tm), lambda i, j, k: (k, i)) if ta
              else pl.BlockSpec((tm, tk), lambda i, j, k: (i, k)))
    if b_chunks:
        b_spec = (pl.BlockSpec((None, tn, tk), lambda i, j, k: (k, j, 0)) if tb
                  else pl.BlockSpec((None, tk, tn), lambda i, j, k: (j, k, 0)))
    else:
        b_spec = (pl.BlockSpec((tn, tk), lambda i, j, k: (j, k)) if tb
                  else pl.BlockSpec((tk, tn), lambda i, j, k: (k, j)))
    if out_chunks:
        o_spec = pl.BlockSpec((None, tm, tn), lambda i, j, k: (j, i, 0))
        o_shape = (out_chunks, M, tn)
    else:
        o_spec = pl.BlockSpec((tm, tn), lambda i, j, k: (i, j))
        o_shape = (M, N)
    outs = pl.pallas_call(
        body,
        name=name,
        grid=(M // tm, N // tn, nk),
        in_specs=[a_spec, b_spec] + [o_spec] * ne,
        out_specs=[o_spec] * no,
        out_shape=[jax.ShapeDtypeStruct(o_shape, dt) for dt in out_dtypes],
        scratch_shapes=[pltpu.VMEM((tm, tn), F32)] if nk > 1 else [],
        compiler_params=_params(("parallel", "parallel", "arbitrary")),
    )(a, b, *extras)
    return outs[0] if no == 1 else outs


def _ep_add(acc, res):
    return (acc + res,)


def _ep_relu2(acc):
    r = jnp.maximum(acc, 0.0)
    return (r * r,)


def _ep_relu2_bwd(acc, act):
    return (acc * (2.0 * jnp.sqrt(act.astype(F32))),)


def _rstd(x):
    return lax.rsqrt(jnp.mean(x * x, axis=-1, keepdims=True) + NORM_EPS)


def _rmsnorm(h, g, *, name, tr=512):
    N, D = h.shape
    tr = _tile(N, tr, SUBLANES)

    def body(h_ref, g_ref, o_ref):
        x = h_ref[...]
        o_ref[...] = ((x * _rstd(x)) * g_ref[...]).astype(o_ref.dtype)

    row = pl.BlockSpec((tr, D), lambda i: (i, 0))
    vec = pl.BlockSpec((1, D), lambda i: (0, 0))
    return pl.pallas_call(
        body, name=name, grid=(N // tr,), in_specs=[row, vec], out_specs=row,
        out_shape=jax.ShapeDtypeStruct((N, D), BF16), compiler_params=_params(("parallel",)),
    )(h, g)


def _rmsnorm_bwd(dy, h, g, dres, *, name, tr=512):
    N, D = h.shape
    tr = _tile(N, tr, SUBLANES)

    def body(dy_ref, h_ref, g_ref, r_ref, dh_ref, dhb_ref, dg_ref):
        i = pl.program_id(0)
        x = h_ref[...]
        d = dy_ref[...]
        r = _rstd(x)
        xh = x * r

        @pl.when(i == 0)
        def _():
            dg_ref[...] = jnp.zeros_like(dg_ref)

        dg_ref[...] += jnp.sum(d * xh, axis=0, keepdims=True)
        dxh = d * g_ref[...]
        dh = r_ref[...] + r * (dxh - xh * jnp.mean(dxh * xh, axis=-1, keepdims=True))
        dh_ref[...] = dh
        dhb_ref[...] = dh.astype(dhb_ref.dtype)

    row = pl.BlockSpec((tr, D), lambda i: (i, 0))
    vec = pl.BlockSpec((1, D), lambda i: (0, 0))
    return pl.pallas_call(
        body, name=name, grid=(N // tr,), in_specs=[row, row, vec, row], out_specs=[row, row, vec],
        out_shape=[jax.ShapeDtypeStruct((N, D), F32), jax.ShapeDtypeStruct((N, D), BF16),
                   jax.ShapeDtypeStruct((1, D), F32)],
        compiler_params=_params(("arbitrary",)),
    )(dy, h, g, dres)


def _final_loss(h, g, target, *, name, tr=512):
    N, D = h.shape
    tr = _tile(N, tr, SUBLANES)

    def body(h_ref, g_ref, t_ref, loss_ref, dh_ref, dhb_ref, dg_ref):
        i = pl.program_id(0)
        x = h_ref[...]
        gg = g_ref[...]
        r = _rstd(x)
        xh = x * r
        err = xh * gg - t_ref[...]

        @pl.when(i == 0)
        def _():
            dg_ref[...] = jnp.zeros_like(dg_ref)
            loss_ref[...] = jnp.zeros_like(loss_ref)

        per_row = jnp.mean(err * err, axis=-1, keepdims=True)
        loss_ref[...] += 0.5 * jnp.sum(per_row, axis=0, keepdims=True)
        dy = err * (1.0 / D)
        dg_ref[...] += jnp.sum(dy * xh, axis=0, keepdims=True)
        dxh = dy * gg
        dh = r * (dxh - xh * jnp.mean(dxh * xh, axis=-1, keepdims=True))
        dh_ref[...] = dh
        dhb_ref[...] = dh.astype(dhb_ref.dtype)

    row = pl.BlockSpec((tr, D), lambda i: (i, 0))
    vec = pl.BlockSpec((1, D), lambda i: (0, 0))
    lvec = pl.BlockSpec((1, LANES), lambda i: (0, 0))
    return pl.pallas_call(
        body, name=name, grid=(N // tr,), in_specs=[row, vec, row], out_specs=[lvec, row, row, vec],
        out_shape=[jax.ShapeDtypeStruct((1, LANES), F32), jax.ShapeDtypeStruct((N, D), F32),
                   jax.ShapeDtypeStruct((N, D), BF16), jax.ShapeDtypeStruct((1, D), F32)],
        compiler_params=_params(("arbitrary",)),
    )(h, g, target)


def _shift_down(x, s):
    t = lax.broadcasted_iota(jnp.int32, x.shape, 0)
    return jnp.where(t >= s, pltpu.roll(x, s, 0), 0.0)


def _shift_up(x, s):
    n = x.shape[0]
    t = lax.broadcasted_iota(jnp.int32, x.shape, 0)
    return jnp.where(t < n - s, pltpu.roll(x, n - s, 0), 0.0)


def _window_sum(x, g, shift):
    s = x + shift(x, 1)
    for k in range(1, POOL_LOG_WINDOWS):
        s = jnp.where(k <= g, s + shift(s, 2 ** k), s)
    return s


def _pool_count(shape, g):
    t = lax.broadcasted_iota(jnp.int32, shape, 0)
    return jnp.minimum(t + 1, lax.shift_left(jnp.int32(2), g)).astype(F32)


def _pool_fwd(p, pool_w, pool_scale, B, T, off, width, *, name):
    G, dh = pool_w.shape[0], pool_w.shape[1]
    assert G == POOL_LOG_WINDOWS and off % dh == 0
    base = off // dh

    def body(a_ref, w_ref, s_ref, o_ref):
        g = pl.program_id(0)
        a = a_ref[...]
        pooled = _window_sum(a, g, _shift_down) / _pool_count(a.shape, g) - a
        m = jnp.dot(pooled.astype(BF16), w_ref[0].astype(BF16), preferred_element_type=F32)
        o_ref[...] = (m * s_ref[0]).astype(o_ref.dtype)

    return pl.pallas_call(
        body, name=name, grid=(G, B),
        in_specs=[pl.BlockSpec((T, dh), lambda g, b: (b, base + g)),
                  pl.BlockSpec((1, dh, dh), lambda g, b: (g, 0, 0)),
                  pl.BlockSpec((1, 1, dh), lambda g, b: (g, 0, 0))],
        out_specs=pl.BlockSpec((T, dh), lambda g, b: (b, g)),
        out_shape=jax.ShapeDtypeStruct((B * T, width), BF16),
        compiler_params=_params(("parallel", "parallel")),
    )(p, pool_w, pool_scale.reshape(G, 1, dh))


def _pool_bwd(p, dcat, pool_w, pool_scale, B, T, off, *, name):
    G, dh = pool_w.shape[0], pool_w.shape[1]
    base = off // dh

    def body(a_ref, d_ref, w_ref, s_ref, da_ref, dw_ref, ds_ref):
        g = pl.program_id(0)
        b = pl.program_id(1)
        a = a_ref[...]
        d = d_ref[...]
        cnt = _pool_count(a.shape, g)
        pooled = (_window_sum(a, g, _shift_down) / cnt - a).astype(BF16)
        w = w_ref[0].astype(BF16)
        m = jnp.dot(pooled, w, preferred_element_type=F32)

        @pl.when(b == 0)
        def _():
            dw_ref[...] = jnp.zeros_like(dw_ref)
            ds_ref[...] = jnp.zeros_like(ds_ref)

        ds_ref[0] += jnp.sum(d * m, axis=0, keepdims=True)
        dm = (d * s_ref[0]).astype(BF16)
        dw_ref[0] += lax.dot_general(pooled, dm, (((0,), (0,)), ((), ())), preferred_element_type=F32)
        dpooled = lax.dot_general(dm, w, (((1,), (1,)), ((), ())), preferred_element_type=F32)
        da = _window_sum(dpooled / cnt, g, _shift_up) - dpooled
        da_ref[...] = da.astype(da_ref.dtype)

    pblk = pl.BlockSpec((T, dh), lambda g, b: (b, base + g))
    dblk = pl.BlockSpec((T, dh), lambda g, b: (b, g))
    wspec = pl.BlockSpec((1, dh, dh), lambda g, b: (g, 0, 0))
    sspec = pl.BlockSpec((1, 1, dh), lambda g, b: (g, 0, 0))
    return pl.pallas_call(
        body, name=name, grid=(G, B), in_specs=[pblk, dblk, wspec, sspec], out_specs=[pblk, wspec, sspec],
        out_shape=[jax.ShapeDtypeStruct((B * T, p.shape[1]), BF16), jax.ShapeDtypeStruct((G, dh, dh), F32),
                   jax.ShapeDtypeStruct((G, 1, dh), F32)],
        compiler_params=_params(("parallel", "arbitrary")),
    )(p, dcat, pool_w, pool_scale.reshape(G, 1, dh))


_ANY = pl.BlockSpec(memory_space=pl.ANY)


def _conv_fwd(p, cat, conv_w, conv_b, B, T, coff, *, name):
    CW = conv_w.shape[1]
    tc = LANES
    assert coff % tc == 0 and CW % tc == 0
    cbase = coff // tc

    def body(p_ref, cat_ref, w_ref, b_ref, o_ref):
        xb, gb, gc = p_ref[:, 0:tc], p_ref[:, tc:2 * tc], p_ref[:, 2 * tc:3 * tc]
        c = gc * xb
        w = w_ref[...]
        y = _shift_down(c, 2) * w[0:1] + _shift_down(c, 1) * w[1:2] + c * w[2:3] + b_ref[...]
        o_ref[...] = (gb * y).astype(o_ref.dtype)

    return pl.pallas_call(
        body, name=name, grid=(CW // tc, B),
        in_specs=[pl.BlockSpec((T, 3 * tc), lambda j, b: (b, j)), _ANY,
                  pl.BlockSpec((3, tc), lambda j, b: (0, j)), pl.BlockSpec((1, tc), lambda j, b: (0, j))],
        out_specs=pl.BlockSpec((T, tc), lambda j, b: (b, cbase + j)),
        out_shape=jax.ShapeDtypeStruct(cat.shape, cat.dtype), input_output_aliases={1: 0},
        compiler_params=_params(("parallel", "parallel")),
    )(p, cat, conv_w, conv_b)


def _conv_bwd(p, dcat, dp, conv_w, conv_b, B, T, coff, *, name):
    CW = conv_w.shape[1]
    tc = LANES
    assert coff % tc == 0
    cbase = coff // tc

    def body(p_ref, d_ref, dp_in_ref, w_ref, b_ref, dp_ref, dw_ref, db_ref):
        b = pl.program_id(1)
        xb, gb, gc = p_ref[:, 0:tc], p_ref[:, tc:2 * tc], p_ref[:, 2 * tc:3 * tc]
        d = d_ref[...]
        w = w_ref[...]
        c = gc * xb
        c1 = _shift_down(c, 1)
        c2 = _shift_down(c, 2)
        y = c2 * w[0:1] + c1 * w[1:2] + c * w[2:3] + b_ref[...]
        dy = d * gb
        dp_ref[:, tc:2 * tc] = (d * y).astype(dp_ref.dtype)

        @pl.when(b == 0)
        def _():
            dw_ref[...] = jnp.zeros_like(dw_ref)
            db_ref[...] = jnp.zeros_like(db_ref)

        db_ref[...] += jnp.sum(dy, axis=0, keepdims=True)
        dw_ref[0:1, :] += jnp.sum(dy * c2, axis=0, keepdims=True)
        dw_ref[1:2, :] += jnp.sum(dy * c1, axis=0, keepdims=True)
        dw_ref[2:3, :] += jnp.sum(dy * c, axis=0, keepdims=True)
        dc = dy * w[2:3] + _shift_up(dy, 1) * w[1:2] + _shift_up(dy, 2) * w[0:1]
        dp_ref[:, 2 * tc:3 * tc] = (dc * xb).astype(dp_ref.dtype)
        dp_ref[:, 0:tc] = (dc * gc).astype(dp_ref.dtype)

    wspec = pl.BlockSpec((3, tc), lambda j, b: (0, j))
    bspec = pl.BlockSpec((1, tc), lambda j, b: (0, j))
    pblk = pl.BlockSpec((T, 3 * tc), lambda j, b: (b, j))
    return pl.pallas_call(
        body, name=name, grid=(CW // tc, B),
        in_specs=[pblk, pl.BlockSpec((T, tc), lambda j, b: (b, cbase + j)), _ANY, wspec, bspec],
        out_specs=[pblk, wspec, bspec],
        out_shape=[jax.ShapeDtypeStruct(dp.shape, dp.dtype), jax.ShapeDtypeStruct((3, CW), F32),
                   jax.ShapeDtypeStruct((1, CW), F32)],
        input_output_aliases={2: 0},
        compiler_params=_params(("parallel", "arbitrary")),
    )(p, dcat, dp, conv_w, conv_b)


_SQRT_HALF = 0.7071067811865476
_INV_SQRT_2PI = 0.3989422804014327


def _gelu(x):
    return x * (lax.erf(x * _SQRT_HALF) + 1.0) * 0.5


def _gelu_grad(x):
    return 0.5 * (lax.erf(x * _SQRT_HALF) + 1.0) + x * (_INV_SQRT_2PI * jnp.exp(-0.5 * x * x))


def _layernorm_parts(v):
    mu = jnp.mean(v, axis=-1, keepdims=True)
    vc = v - mu
    rstd = lax.rsqrt(jnp.mean(vc * vc, axis=-1, keepdims=True) + NORM_EPS)
    return vc * rstd, rstd


def _tril_mask(L):
    r = lax.broadcasted_iota(jnp.int32, (L, L), 0)
    c = lax.broadcasted_iota(jnp.int32, (L, L), 1)
    return r >= c


def _sgu_fwd(p, ln_g, ln_b, sgu_w, sgu_b, off, width, *, name, tr=512):
    N = p.shape[0]
    G, L = sgu_w.shape[0], sgu_w.shape[1]
    SW = ln_g.shape[1]
    dh = SW // G
    assert off % SW == 0
    ub = off // SW
    tr = _tile(N, tr, L)
    assert tr % L == 0

    def body(u_ref, v_ref, g_ref, beta_ref, w_ref, b_ref, o_ref):
        u = _gelu(u_ref[...])
        vhat, _ = _layernorm_parts(_gelu(v_ref[...]))
        vn = (vhat * g_ref[...] + beta_ref[...]).astype(BF16)
        mask = _tril_mask(L)
        for gi in range(G):
            w = jnp.where(mask, w_ref[gi], 0.0).astype(BF16)
            bias = b_ref[gi]
            cols = slice(gi * dh, (gi + 1) * dh)
            for n in range(tr // L):
                rows = slice(n * L, (n + 1) * L)
                s = jnp.dot(w, vn[rows, cols], preferred_element_type=F32) + bias
                o_ref[rows, cols] = (u[rows, cols] * s).astype(o_ref.dtype)

    def col(k):
        return pl.BlockSpec((tr, SW), lambda i: (i, k))

    vec = pl.BlockSpec((1, SW), lambda i: (0, 0))
    return pl.pallas_call(
        body, name=name, grid=(N // tr,),
        in_specs=[col(ub), col(ub + 1), vec, vec, pl.BlockSpec((G, L, L), lambda i: (0, 0, 0)),
                  pl.BlockSpec((G, L, 1), lambda i: (0, 0, 0))],
        out_specs=col(0), out_shape=jax.ShapeDtypeStruct((N, width), BF16),
        compiler_params=_params(("parallel",)),
    )(p, p, ln_g, ln_b, sgu_w, sgu_b.reshape(G, L, 1))


def _sgu_bwd(p, dcat, ln_g, ln_b, sgu_w, sgu_b, off, *, name, tr=512):
    N = p.shape[0]
    G, L = sgu_w.shape[0], sgu_w.shape[1]
    SW = ln_g.shape[1]
    dh = SW // G
    assert off % SW == 0
    ub = off // SW
    tr = _tile(N, tr, L)

    def compute(i, u_ref, v_ref, dc_ref, g_ref, beta_ref, w_ref, b_ref,
                du_ref, dv_ref, dw_ref, db_ref, dg_ref, dbeta_ref, du_s, dvn_s):
        pu = u_ref[...]
        pv = v_ref[...]
        u = _gelu(pu)
        vhat, rstd = _layernorm_parts(_gelu(pv))
        gg = g_ref[...]
        vn = (vhat * gg + beta_ref[...]).astype(BF16)
        dc = dc_ref[...]
        mask = _tril_mask(L)

        @pl.when(i == 0)
        def _():
            dw_ref[...] = jnp.zeros_like(dw_ref)
            db_ref[...] = jnp.zeros_like(db_ref)
            dg_ref[...] = jnp.zeros_like(dg_ref)
            dbeta_ref[...] = jnp.zeros_like(dbeta_ref)

        for gi in range(G):
            w = jnp.where(mask, w_ref[gi], 0.0).astype(BF16)
            bias = b_ref[gi]
            cols = slice(gi * dh, (gi + 1) * dh)
            dw_acc = jnp.zeros((L, L), F32)
            db_acc = jnp.zeros((L, 1), F32)
            for n in range(tr // L):
                rows = slice(n * L, (n + 1) * L)
                vb = vn[rows, cols]
                s = jnp.dot(w, vb, preferred_element_type=F32) + bias
                du_s[rows, cols] = dc[rows, cols] * s
                ds = dc[rows, cols] * u[rows, cols]
                db_acc += jnp.sum(ds, axis=1, keepdims=True)
                dsb = ds.astype(BF16)
                dw_acc += lax.dot_general(dsb, vb, (((1,), (1,)), ((), ())), preferred_element_type=F32)
                dvn_s[rows, cols] = lax.dot_general(w, dsb, (((0,), (0,)), ((), ())),
                                                    preferred_element_type=F32)
            dw_ref[gi] += jnp.where(mask, dw_acc, 0.0)
            db_ref[gi] += db_acc

        dvn = dvn_s[...]
        dg_ref[...] += jnp.sum(dvn * vhat, axis=0, keepdims=True)
        dbeta_ref[...] += jnp.sum(dvn, axis=0, keepdims=True)
        dvh = dvn * gg
        dv = rstd * (dvh - jnp.mean(dvh, axis=-1, keepdims=True)
                     - vhat * jnp.mean(dvh * vhat, axis=-1, keepdims=True))
        dv_ref[...] = (dv * _gelu_grad(pv)).astype(dv_ref.dtype)
        du_ref[...] = (du_s[...] * _gelu_grad(pu)).astype(du_ref.dtype)

    def body(u_ref, v_ref, dc_ref, g_ref, beta_ref, w_ref, b_ref,
             dp_ref, dw_ref, db_ref, dg_ref, dbeta_ref, du_s, dvn_s, dv_s):
        i = pl.program_id(0)
        half = pl.program_id(1)

        @pl.when(half == 0)
        def _():
            compute(i, u_ref, v_ref, dc_ref, g_ref, beta_ref, w_ref, b_ref,
                    dp_ref, dv_s, dw_ref, db_ref, dg_ref, dbeta_ref, du_s, dvn_s)

        @pl.when(half == 1)
        def _():
            dp_ref[...] = dv_s[...]

    def col(k):
        return pl.BlockSpec((tr, SW), lambda i, half: (i, k))

    vec = pl.BlockSpec((1, SW), lambda i, half: (0, 0))
    wspec = pl.BlockSpec((G, L, L), lambda i, half: (0, 0, 0))
    bspec = pl.BlockSpec((G, L, 1), lambda i, half: (0, 0, 0))
    return pl.pallas_call(
        body, name=name, grid=(N // tr, 2),
        in_specs=[col(ub), col(ub + 1), col(0), vec, vec, wspec, bspec],
        out_specs=[pl.BlockSpec((tr, SW), lambda i, half: (i, ub + half)), wspec, bspec, vec, vec],
        out_shape=[jax.ShapeDtypeStruct((N, p.shape[1]), BF16),
                   jax.ShapeDtypeStruct((G, L, L), F32), jax.ShapeDtypeStruct((G, L, 1), F32),
                   jax.ShapeDtypeStruct((1, SW), F32), jax.ShapeDtypeStruct((1, SW), F32)],
        scratch_shapes=[pltpu.VMEM((tr, SW), F32), pltpu.VMEM((tr, SW), F32), pltpu.VMEM((tr, SW), BF16)],
        compiler_params=_params(("arbitrary", "arbitrary")),
    )(p, p, dcat, ln_g, ln_b, sgu_w, sgu_b.reshape(G, L, 1))


def _log_sigmoid_pair(z):
    ls = jnp.minimum(z, 0.0) - jnp.log(1.0 + jnp.exp(-jnp.abs(z)))
    return ls, ls - z


def _stacked_cumsum(xs, m):
    his = [x.astype(BF16) for x in xs]
    los = [(x - h.astype(F32)).astype(BF16) for x, h in zip(xs, his)]
    n, rows = len(xs), xs[0].shape[0]
    s = jnp.dot(jnp.concatenate(his + los, axis=0), m, preferred_element_type=F32)
    return [s[i * rows:(i + 1) * rows] + s[(n + i) * rows:(n + i + 1) * rows] for i in range(n)]


def _att_logits(qts, kbs, strict):
    ls, lks = [], []
    for qt, kb in zip(qts, kbs):
        l, lk = _log_sigmoid_pair(lax.dot_general(qt, kb, (((1,), (1,)), ((), ())), preferred_element_type=F32))
        ls.append(l)
        lks.append(lk if strict is None else jnp.where(strict, lk, 0.0))
    return ls, lks


def _att_tiles(TB):
    r = lax.broadcasted_iota(jnp.int32, (TB, TB), 0)
    c = lax.broadcasted_iota(jnp.int32, (TB, TB), 1)
    return r, c


HEADS_PER_BLOCK = LANES // SB_DH
ATT_GROUPS = 2
HP = ATT_GROUPS * HEADS_PER_BLOCK
ATT_SCALE = 1.0 / math.sqrt(SB_DH)


def _head_col(hh, part):
    return (hh // HEADS_PER_BLOCK) * 3 * LANES + part * LANES + (hh % HEADS_PER_BLOCK) * SB_DH


def _stage_heads(src_ref, part, dst_ref, T, scale=None):
    rows = _tile(T, 256, SUBLANES)

    def chunk(n, _):
        r0 = pl.multiple_of(n * rows, rows)
        for hh in range(HP):
            col = hh * SB_DH if part is None else _head_col(hh, part)
            x = src_ref[pl.ds(r0, rows), col:col + SB_DH]
            if scale is not None:
                x = x * scale
            dst_ref[hh, pl.ds(r0, rows), :] = x.astype(dst_ref.dtype)
        return 0

    lax.fori_loop(0, T // rows, chunk, 0)


def _attn_fwd(p, cat, B, T, coff, *, name):
    dh = SB_DH
    owidth = ATT_GROUPS * LANES
    nsteps = (cat.shape[1] - coff) // owidth
    TB = _tile(T, ATT_BLOCK)
    nb = T // TB
    assert nb <= LANES and coff % owidth == 0 and (cat.shape[1] - coff) % owidth == 0
    cbase = coff // owidth

    def body(p_ref, cat_ref, o_ref, c_ref, q_ref, k_ref, v_ref):
        _stage_heads(p_ref, 0, q_ref, T, ATT_SCALE)
        _stage_heads(p_ref, 1, k_ref, T)
        _stage_heads(p_ref, 2, v_ref, T)
        r, c = _att_tiles(TB)
        strict = c < r
        later = (r > c).astype(BF16)
        lane = lax.broadcasted_iota(jnp.int32, (TB, LANES), 1)

        def tiles(qts, j, carries, mask):
            k0 = pl.multiple_of(j * TB, TB)
            ls, lks = _att_logits(qts, [k_ref[hh, pl.ds(k0, TB), :] for hh in range(HP)], mask)
            sums = _stacked_cumsum(lks, later)
            out = []
            for hh in range(HP):
                suffix = sums[hh] + carries[hh]
                a = jnp.exp(ls[hh] + suffix)
                if mask is not None:
                    a = jnp.where(mask, a, 0.0)
                pv = jnp.dot(a.astype(BF16), v_ref[hh, pl.ds(k0, TB), :], preferred_element_type=F32)
                out.append((pv, suffix[:, 0:1] + lks[hh][:, 0:1]))
            return out

        def qblock(i, _):
            q0 = pl.multiple_of(i * TB, TB)
            qts = [q_ref[hh, pl.ds(q0, TB), :] for hh in range(HP)]
            state = []
            for pv, carry in tiles(qts, i, [jnp.zeros((TB, 1), F32)] * HP, strict):
                state += [pv, carry, jnp.zeros((TB, LANES), F32)]

            def kblock(jj, st):
                j = i - jj
                out = []
                for hh, (pv, new_carry) in enumerate(tiles(qts, j, st[1::3], None)):
                    acc, carry, cm = st[3 * hh:3 * hh + 3]
                    out += [acc + pv, new_carry, jnp.where(lane == j, carry, cm)]
                return tuple(out)

            st = lax.fori_loop(1, i + 1, kblock, tuple(state))
            for hh in range(HP):
                o_ref[pl.ds(q0, TB), hh * dh:(hh + 1) * dh] = st[3 * hh].astype(o_ref.dtype)
                c_ref[hh, pl.ds(q0, TB), :] = st[3 * hh + 2]
            return 0

        lax.fori_loop(0, nb, qblock, 0)

    staged = pltpu.VMEM((HP, T, dh), BF16)
    return pl.pallas_call(
        body, name=name, grid=(B, nsteps),
        in_specs=[pl.BlockSpec((T, 3 * owidth), lambda b, s: (b, s)), _ANY],
        out_specs=[pl.BlockSpec((T, owidth), lambda b, s: (b, cbase + s)),
                   pl.BlockSpec((HP, T, LANES), lambda b, s: (b * nsteps + s, 0, 0))],
        out_shape=[jax.ShapeDtypeStruct(cat.shape, cat.dtype),
                   jax.ShapeDtypeStruct((B * nsteps * HP, T, LANES), F32)],
        input_output_aliases={1: 0}, scratch_shapes=[staged, staged, staged],
        compiler_params=_params(("parallel", "parallel")),
    )(p, cat)


def _attn_bwd(p, dcat, carries, dp, B, T, coff, *, name):
    dh = SB_DH
    owidth = ATT_GROUPS * LANES
    nsteps = (dcat.shape[1] - coff) // owidth
    TB = _tile(T, ATT_BLOCK)
    nb = T // TB
    cbase = coff // owidth

    def body(p_ref, d_ref, c_ref, dp_in_ref, dp_ref, q_ref, k_ref, v_ref, do_ref, dk_ref, dv_ref):
        _stage_heads(p_ref, 0, q_ref, T, ATT_SCALE)
        _stage_heads(p_ref, 1, k_ref, T)
        _stage_heads(p_ref, 2, v_ref, T)
        _stage_heads(d_ref, None, do_ref, T)
        r, c = _att_tiles(TB)
        strict = c < r
        later = (r > c).astype(BF16)
        earlier = (r < c).astype(BF16)
        lane = lax.broadcasted_iota(jnp.int32, (TB, LANES), 1)
        dk_ref[...] = jnp.zeros_like(dk_ref)
        dv_ref[...] = jnp.zeros_like(dv_ref)

        def tiles(qts, dots, cms, j, befores, mask):
            k0 = pl.multiple_of(j * TB, TB)
            kbs = [k_ref[hh, pl.ds(k0, TB), :] for hh in range(HP)]
            ls, lks = _att_logits(qts, kbs, mask)
            sums = _stacked_cumsum(lks, later)
            gls = []
            for hh in range(HP):
                carry = jnp.sum(jnp.where(lane == j, cms[hh], 0.0), axis=1, keepdims=True)
                a = jnp.exp(ls[hh] + sums[hh] + carry)
                if mask is not None:
                    a = jnp.where(mask, a, 0.0)
                dv_ref[hh, pl.ds(k0, TB), :] += lax.dot_general(a.astype(BF16), dots[hh], (((0,), (0,)), ((), ())),
                                                                preferred_element_type=F32)
                da = lax.dot_general(dots[hh], v_ref[hh, pl.ds(k0, TB), :], (((1,), (1,)), ((), ())),
                                     preferred_element_type=F32)
                gls.append(a * da)
            pres = _stacked_cumsum(gls, earlier)
            out = []
            for hh in range(HP):
                prefix = pres[hh] + befores[hh]
                dz = gls[hh] - jnp.exp(ls[hh]) * (gls[hh] + prefix)
                if mask is not None:
                    dz = jnp.where(mask, dz, 0.0)
                dzb = dz.astype(BF16)
                dk_ref[hh, pl.ds(k0, TB), :] += lax.dot_general(dzb, qts[hh], (((0,), (0,)), ((), ())),
                                                                preferred_element_type=F32)
                out.append((jnp.dot(dzb, kbs[hh], preferred_element_type=F32),
                            prefix[:, TB - 1:TB] + gls[hh][:, TB - 1:TB]))
            return out

        def qblock(i, _):
            q0 = pl.multiple_of(i * TB, TB)
            qts = [q_ref[hh, pl.ds(q0, TB), :] for hh in range(HP)]
            dots = [do_ref[hh, pl.ds(q0, TB), :] for hh in range(HP)]
            cms = [c_ref[hh, pl.ds(q0, TB), :] for hh in range(HP)]

            def kblock(j, st):
                out = []
                for hh, (part, new_before) in enumerate(tiles(qts, dots, cms, j, st[1::2], None)):
                    out += [st[2 * hh] + part, new_before]
                return tuple(out)

            st = lax.fori_loop(0, i, kblock, (jnp.zeros((TB, dh), F32), jnp.zeros((TB, 1), F32)) * HP)
            for hh, (part, _) in enumerate(tiles(qts, dots, cms, i, st[1::2], strict)):
                dq = (st[2 * hh] + part) * ATT_SCALE
                dp_ref[pl.ds(q0, TB), _head_col(hh, 0):_head_col(hh, 0) + dh] = dq.astype(dp_ref.dtype)
            return 0

        lax.fori_loop(0, nb, qblock, 0)

        def write_back(n, _):
            r0 = pl.multiple_of(n * TB, TB)
            for hh in range(HP):
                dp_ref[pl.ds(r0, TB), _head_col(hh, 1):_head_col(hh, 1) + dh] = (
                    dk_ref[hh, pl.ds(r0, TB), :].astype(dp_ref.dtype))
                dp_ref[pl.ds(r0, TB), _head_col(hh, 2):_head_col(hh, 2) + dh] = (
                    dv_ref[hh, pl.ds(r0, TB), :].astype(dp_ref.dtype))
            return 0

        lax.fori_loop(0, nb, write_back, 0)

    pblk = pl.BlockSpec((T, 3 * owidth), lambda b, s: (b, s))
    staged = pltpu.VMEM((HP, T, dh), BF16)
    accum = pltpu.VMEM((HP, T, dh), F32)
    return pl.pallas_call(
        body, name=name, grid=(B, nsteps),
        in_specs=[pblk, pl.BlockSpec((T, owidth), lambda b, s: (b, cbase + s)),
                  pl.BlockSpec((HP, T, LANES), lambda b, s: (b * nsteps + s, 0, 0)), _ANY],
        out_specs=pblk, out_shape=jax.ShapeDtypeStruct(dp.shape, dp.dtype), input_output_aliases={3: 0},
        scratch_shapes=[staged, staged, staged, staged, accum, accum],
        compiler_params=_params(("parallel", "parallel")),
    )(p, dcat, carries, dp)


def _adamw(w, gparts, m, v, *, name):
    L, R, C = w.shape
    P = gparts[0].shape[0]
    assert len(gparts) == L
    tr = _tile(R, max(SUBLANES, (1 << 21) // (C * P)), SUBLANES)

    def body(*refs):
        w_ref, g_refs, (m_ref, v_ref) = refs[0], refs[1:1 + L], refs[1 + L:3 + L]
        go_ref, d_ref, mo_ref, vo_ref = refs[3 + L:]
        layer = pl.program_id(0)

        def update(g_ref):
            g = g_ref[0].astype(F32)
            for i in range(1, P):
                g = g + g_ref[i].astype(F32)
            m2 = ADAM_B1 * m_ref[...] + (1.0 - ADAM_B1) * g
            v2 = ADAM_B2 * v_ref[...] + (1.0 - ADAM_B2) * (g * g)
            m_hat = m2 / ADAM_C1
            v_hat = v2 / ADAM_C2
            go_ref[...] = g
            d_ref[...] = -ADAM_LR * (m_hat / (jnp.sqrt(v_hat) + ADAM_EPS) + ADAM_WD * w_ref[...])
            mo_ref[...] = m2
            vo_ref[...] = v2

        for l in range(L):
            pl.when(layer == l)(functools.partial(update, g_refs[l]))

    row = pl.BlockSpec((None, tr, C), lambda l, i: (l, i, 0))

    def part(mine):
        return pl.BlockSpec((P, tr, C), lambda l, i: (0, jnp.where(l == mine, i, 0), 0))

    shp = jax.ShapeDtypeStruct((L, R, C), F32)
    return pl.pallas_call(
        body, name=name, grid=(L, R // tr),
        in_specs=[row] + [part(l) for l in range(L)] + [row, row],
        out_specs=[row] * 4, out_shape=[shp] * 4, compiler_params=_params(("arbitrary", "arbitrary")),
    )(w, *gparts, m, v)


def _my_index():
    return 4 * lax.axis_index("x") + 2 * lax.axis_index("y") + lax.axis_index("c")


def _exchange(arrs, gather, *, name):
    n = len(arrs)

    def body(*refs):
        ins, outs = refs[:n], refs[n:2 * n]
        send_sems, recv_sems, local_sems = refs[2 * n:]
        x, y, c = lax.axis_index("x"), lax.axis_index("y"), lax.axis_index("c")
        me = 4 * x + 2 * y + c
        remote, local = [], []
        for a in range(n):
            own = ins[a] if gather[a] else ins[a].at[me]
            cp = pltpu.make_async_copy(own, outs[a].at[me], local_sems.at[a])
            cp.start()
            local.append(cp)
            for k in range(1, N_DEV):
                px = 1 - x if k & 4 else x
                py = 1 - y if k & 2 else y
                pc = 1 - c if k & 1 else c
                src = ins[a] if gather[a] else ins[a].at[4 * px + 2 * py + pc]
                cp = pltpu.make_async_remote_copy(
                    src_ref=src, dst_ref=outs[a].at[me],
                    send_sem=send_sems.at[a, k - 1], recv_sem=recv_sems.at[a, k - 1],
                    device_id=(px, py, pc), device_id_type=pl.DeviceIdType.MESH)
                cp.start()
                remote.append(cp)
        for cp in remote:
            cp.wait()
        for cp in local:
            cp.wait()

    hbm = pl.BlockSpec(memory_space=pltpu.HBM)
    out_shape = [jax.ShapeDtypeStruct(((N_DEV,) + a.shape) if g else a.shape, a.dtype)
                 for a, g in zip(arrs, gather)]
    return pl.pallas_call(
        body, name=name, in_specs=[hbm] * n, out_specs=[hbm] * n, out_shape=out_shape,
        scratch_shapes=[pltpu.SemaphoreType.DMA((n, N_DEV - 1)), pltpu.SemaphoreType.DMA((n, N_DEV - 1)),
                        pltpu.SemaphoreType.DMA((n,))],
    )(*arrs)


_HBM = pl.BlockSpec(memory_space=pltpu.HBM)


def _other_chips(x, y):
    return [(1 - x, y), (x, 1 - y), (1 - x, 1 - y)]


def _gather_two_level(arrs, *, name):
    n = len(arrs)

    def body(*refs):
        ins, outs = refs[:n], refs[n:2 * n]
        send_sems, recv_sems, local_sems = refs[2 * n:]
        x, y, c = lax.axis_index("x"), lax.axis_index("y"), lax.axis_index("c")
        me, sibling = (x, y, c), (x, y, 1 - c)
        chips = _other_chips(x, y)

        def slot(a, px, py, pc):
            return outs[a].at[4 * px + 2 * py + pc]

        def copy(a, k, block, to, src=None):
            return pltpu.make_async_remote_copy(
                src_ref=slot(a, *block) if src is None else src, dst_ref=slot(a, *block),
                send_sem=send_sems.at[a, k], recv_sem=recv_sems.at[a, k],
                device_id=to, device_id_type=pl.DeviceIdType.MESH)

        local, sends = [], []
        for a in range(n):
            cp = pltpu.make_async_copy(ins[a], slot(a, *me), local_sems.at[a])
            cp.start()
            local.append(cp)
            first = [copy(a, 0, me, sibling, src=ins[a])]
            first += [copy(a, 1 + j, me, (*chip, c), src=ins[a]) for j, chip in enumerate(chips)]
            for cp in first:
                cp.start()
            sends += first
        for j, chip in enumerate(chips):
            for a in range(n):
                copy(a, 1 + j, (*chip, c), me).wait_recv()
                cp = copy(a, 4 + j, (*chip, c), sibling)
                cp.start()
                sends.append(cp)
        for a in range(n):
            copy(a, 0, sibling, me).wait_recv()
            for j, chip in enumerate(chips):
                copy(a, 4 + j, (*chip, 1 - c), me).wait_recv()
        for cp in sends:
            cp.wait_send()
        for cp in local:
            cp.wait()

    return pl.pallas_call(
        body, name=name, in_specs=[_HBM] * n, out_specs=[_HBM] * n,
        out_shape=[jax.ShapeDtypeStruct((N_DEV,) + a.shape, a.dtype) for a in arrs],
        scratch_shapes=[pltpu.SemaphoreType.DMA((n, N_DEV - 1)), pltpu.SemaphoreType.DMA((n, N_DEV - 1)),
                        pltpu.SemaphoreType.DMA((n,))],
    )(*arrs)


_SEM = pl.BlockSpec(memory_space=pltpu.SEMAPHORE)
_SPLIT_COPY = pltpu.SideEffectType.DATAFLOW_SIDE_EFFECTING


def _peers(x, y, c):
    return [((1 - x if k & 4 else x), (1 - y if k & 2 else y), (1 - c if k & 1 else c)) for k in range(1, N_DEV)]


_SPLIT_SEMS = 2 * (N_DEV - 1) + 1


def _split_sems(sems, a):
    mine = sems[a * _SPLIT_SEMS:(a + 1) * _SPLIT_SEMS]
    return mine[:N_DEV - 1], mine[N_DEV - 1:2 * (N_DEV - 1)], mine[-1]


def _split_src(ref, scatter, index):
    return ref.at[index] if scatter else ref


def _gather_start(arrs, carry, *, name, scatter=False):
    n = len(arrs)
    ns = n * _SPLIT_SEMS

    def body(*refs):
        ins, lands = refs[:n], refs[n:2 * n]
        sems = refs[2 * n + 1:2 * n + 1 + ns]
        x, y, c = lax.axis_index("x"), lax.axis_index("y"), lax.axis_index("c")
        me = 4 * x + 2 * y + c
        for a in range(n):
            send, recv, local = _split_sems(sems, a)
            pltpu.make_async_copy(_split_src(ins[a], scatter, me), lands[a].at[me], local).start()
            for k, (px, py, pc) in enumerate(_peers(x, y, c)):
                pltpu.make_async_remote_copy(
                    src_ref=_split_src(ins[a], scatter, 4 * px + 2 * py + pc), dst_ref=lands[a].at[me],
                    send_sem=send[k], recv_sem=recv[k],
                    device_id=(px, py, pc), device_id_type=pl.DeviceIdType.MESH).start()

    lands = [lax.empty(a.shape if scatter else (N_DEV,) + a.shape, a.dtype) for a in arrs]
    operands = [pltpu.with_memory_space_constraint(a, pltpu.HBM) for a in list(arrs) + lands + [carry]]
    outs = pl.pallas_call(
        body, name=name, in_specs=[_HBM] * (2 * n + 1), out_specs=[_SEM] * ns + [_HBM] * (2 * n + 1),
        out_shape=[pltpu.SemaphoreType.DMA(())] * ns + [pltpu.HBM(a.shape, a.dtype) for a in operands],
        input_output_aliases={i: ns + i for i in range(2 * n + 1)},
        compiler_params=pltpu.CompilerParams(has_side_effects=_SPLIT_COPY),
    )(*operands)
    return tuple(outs[:-1]), outs[-1]


def _gather_wait(handles, after, *, name, scatter=False):
    n = len(handles) // (_SPLIT_SEMS + 2)
    ns = n * _SPLIT_SEMS
    sems, thru = handles[:ns], handles[ns:]

    def body(*refs):
        ins, lands = refs[:n], refs[n:2 * n]
        sems = refs[2 * n:2 * n + ns]
        x, y, c = lax.axis_index("x"), lax.axis_index("y"), lax.axis_index("c")
        me = 4 * x + 2 * y + c
        for a in range(n):
            send, recv, local = _split_sems(sems, a)
            src = _split_src(ins[a], scatter, me)
            pltpu.make_async_copy(src, lands[a].at[me], local).wait()
            for k, peer in enumerate(_peers(x, y, c)):
                cp = pltpu.make_async_remote_copy(
                    src_ref=src, dst_ref=lands[a].at[me], send_sem=send[k], recv_sem=recv[k],
                    device_id=peer, device_id_type=pl.DeviceIdType.MESH)
                cp.wait_send()
                cp.wait_recv()

    outs = pl.pallas_call(
        body, name=name, in_specs=[_HBM] * (2 * n) + [_SEM] * ns + [_ANY], out_specs=[_HBM] * (2 * n),
        out_shape=[pltpu.HBM(a.shape, a.dtype) for a in thru],
        input_output_aliases={i: i for i in range(2 * n)},
        compiler_params=pltpu.CompilerParams(has_side_effects=_SPLIT_COPY),
    )(*thru, *sems, after)
    return outs[n:]


def _group_in_cols(w, lead):
    X = (w.shape[-1] - lead) // 3
    g = w[..., lead:].reshape(w.shape[:-1] + (3, X // LANES, LANES))
    g = jnp.swapaxes(g, -3, -2).reshape(w.shape[:-1] + (3 * X,))
    return jnp.concatenate([g, w[..., :lead]], axis=-1)


def _ungroup_in_cols(w, lead):
    X = (w.shape[-1] - lead) // 3
    g = w[..., :3 * X].reshape(w.shape[:-1] + (X // LANES, 3, LANES))
    g = jnp.swapaxes(g, -3, -2).reshape(w.shape[:-1] + (3 * X,))
    return jnp.concatenate([w[..., 3 * X:], g], axis=-1)


def _mlp_fwd(h, g, W, tag):
    hn = _rmsnorm(h, g, name=f"mlp_norm_{tag}")
    w1 = _weight(W, f"mlp_w1_{tag}", hn)
    act = _matmul(hn, w1, name=f"mlp_up_{tag}", epilogue=_ep_relu2, out_dtypes=(BF16,), b_chunks=True, tm=2048)
    out = _matmul(act, _weight(W, f"mlp_w2_{tag}", act), name=f"mlp_down_{tag}", extras=(h,), epilogue=_ep_add)
    return out, (hn, act, _chunks_to_cols(w1))


def _mlp_bwd(dout, dout_b, h, g, w2, saved, tag, sent):
    hn, act, w1 = saved
    dw2 = _matmul(act, dout_b, ta=True, name=f"mlp_dw2_{tag}", out_dtypes=(GRAD_WIRE,))
    dout_b = sent(dout_b, {f"mlp_w2_{tag}": dw2})
    dz = _matmul(dout_b, w2, tb=True, name=f"mlp_dact_{tag}", extras=(act,), epilogue=_ep_relu2_bwd,
                 out_dtypes=(BF16,))
    dw1 = _matmul(hn, dz, ta=True, name=f"mlp_dw1_{tag}", out_dtypes=(GRAD_WIRE,), out_chunks=N_DEV, tk=4096)
    dz = sent(dz, {f"mlp_w1_{tag}": dw1})
    dhn = _matmul(dz, w1, tb=True, name=f"mlp_dhn_{tag}")
    dh, dh_b, dg = _rmsnorm_bwd(dhn, h, g, dout, name=f"mlp_norm_bwd_{tag}")
    return dh, dh_b, dg, dw1, dw2


class _Lazy:
    def __init__(self, handles, finish, name):
        self.handles, self.finish, self.name, self.done = handles, finish, name, None

    def take(self, after):
        if self.done is None:
            self.done = self.finish(_gather_wait(self.handles, after, name=self.name))
        return self.done


def _weight(W, n, after):
    if isinstance(W[n], _Lazy):
        W.update(W[n].take(after))
    return W[n]


def _local_step(x, target, W, emit=None):
    B, T, D = x.shape
    N = B * T
    G = {}
    row = lambda vec: vec.reshape(1, -1)

    def sent(nxt, grads):
        return nxt if emit is None else emit(grads, nxt)

    PW = W["pool_w"].shape[0] * W["pool_w"].shape[1]
    CW = W["conv_b"].shape[-1]
    SW = W["sgu_norm_g"].shape[-1]
    HW = W["att_width"]

    h0 = x.reshape(N, D)
    xn0 = _rmsnorm(h0, row(W["mix_norm_g"][0]), name="mix_norm_0")
    p0 = _matmul(xn0, W["ab_w_in"], name="ab_in")
    cat0 = _pool_fwd(p0, W["pool_w"], W["pool_scale"], B, T, 3 * CW, PW + CW, name="pool_fwd")
    cat0 = _conv_fwd(p0, cat0, W["conv_w"], row(W["conv_b"]), B, T, PW, name="conv_fwd")
    h1 = _matmul(cat0, W["ab_w_out"], name="ab_out", extras=(h0,), epilogue=_ep_add)
    h2, mlp0 = _mlp_fwd(h1, row(W["mlp_norm_g"][0]), W, 0)
    xn1 = _rmsnorm(h2, row(W["mix_norm_g"][1]), name="mix_norm_1")
    p1 = _matmul(xn1, _weight(W, "cd_w_in", xn1), name="cd_in")
    ln_g, ln_b = row(W["sgu_norm_g"]), row(W["sgu_norm_b"])
    cat1 = _sgu_fwd(p1, ln_g, ln_b, W["sgu_w"], W["sgu_b"], 3 * HW, SW + HW, name="sgu_fwd")
    cat1, att_carries = _attn_fwd(p1, cat1, B, T, SW, name="attn_fwd")
    h3 = _matmul(cat1, W["cd_w_out"], name="cd_out", extras=(h2,), epilogue=_ep_add)
    h4, mlp1 = _mlp_fwd(h3, row(W["mlp_norm_g"][1]), W, 1)

    loss, dh4, dh4_b, G["final_norm_g"] = _final_loss(h4, row(W["final_norm_g"]), target.reshape(N, D),
                                                      name="final_loss")

    dh3, dh3_b, dmlp_g1, dw1_1, dw2_1 = _mlp_bwd(dh4, dh4_b, h3, row(W["mlp_norm_g"][1]), W["mlp_w2_1"], mlp1, 1,
                                                 sent)
    G["cd_w_out"] = _matmul(cat1, dh3_b, ta=True, name="cd_out_dw", out_dtypes=(GRAD_WIRE,))[None]
    dh3_b = sent(dh3_b, {"cd_w_out": G["cd_w_out"][0]})
    dcat1 = _matmul(dh3_b, W["cd_w_out"], tb=True, name="cd_out_dx")
    dp1, G["sgu_w"], dsgu_b, G["sgu_norm_g"], G["sgu_norm_b"] = _sgu_bwd(
        p1, dcat1, ln_g, ln_b, W["sgu_w"], W["sgu_b"], 3 * HW, name="sgu_bwd")
    G["sgu_b"] = dsgu_b.reshape(W["sgu_b"].shape)
    dp1 = sent(dp1, {_EARLY_SMALL: G[_EARLY_SMALL]})
    dp1 = _attn_bwd(p1, dcat1, att_carries, dp1, B, T, SW, name="attn_bwd")
    G["cd_w_in"] = _matmul(xn1, dp1, ta=True, name="cd_in_dw", out_dtypes=(GRAD_WIRE,))[None]
    dp1 = sent(dp1, {"cd_w_in": G["cd_w_in"][0]})
    dxn1 = _matmul(dp1, W["cd_w_in"], tb=True, name="cd_in_dx")
    dh2, dh2_b, dmix_g1 = _rmsnorm_bwd(dxn1, h2, row(W["mix_norm_g"][1]), dh3, name="mix_norm_bwd_1")

    dh1, dh1_b, dmlp_g0, dw1_0, dw2_0 = _mlp_bwd(dh2, dh2_b, h1, row(W["mlp_norm_g"][0]), W["mlp_w2_0"], mlp0, 0,
                                                 sent)
    G["ab_w_out"] = _matmul(cat0, dh1_b, ta=True, name="ab_out_dw", out_dtypes=(GRAD_WIRE,))[None]
    dh1_b = sent(dh1_b, {"ab_w_out": G["ab_w_out"][0]})
    dcat0 = _matmul(dh1_b, W["ab_w_out"], tb=True, name="ab_out_dx")
    dp0, G["pool_w"], dps = _pool_bwd(p0, dcat0, W["pool_w"], W["pool_scale"], B, T, 3 * CW, name="pool_bwd")
    G["pool_scale"] = dps.reshape(W["pool_scale"].shape)
    dp0, G["conv_w"], dcb = _conv_bwd(p0, dcat0, dp0, W["conv_w"], row(W["conv_b"]), B, T, PW, name="conv_bwd")
    G["conv_b"] = dcb.reshape(-1)
    G["ab_w_in"] = _matmul(xn0, dp0, ta=True, name="ab_in_dw", out_dtypes=(GRAD_WIRE,))[None]
    dp0 = sent(dp0, {"ab_w_in": G["ab_w_in"][0]})
    dxn0 = _matmul(dp0, W["ab_w_in"], tb=True, name="ab_in_dx")
    dx, _, dmix_g0 = _rmsnorm_bwd(dxn0, h0, row(W["mix_norm_g"][0]), dh1, name="mix_norm_bwd_0")

    G["mix_norm_g"] = jnp.concatenate([dmix_g0, dmix_g1], axis=0)
    G["mlp_norm_g"] = jnp.concatenate([dmlp_g0, dmlp_g1], axis=0)
    G["mlp_w1"] = jnp.stack([_chunks_to_cols(dw1_0), _chunks_to_cols(dw1_1)])
    G["mlp_w2"] = jnp.stack([dw2_0, dw2_1])
    G["final_norm_g"] = G["final_norm_g"].reshape(-1)
    G["sgu_norm_g"] = G["sgu_norm_g"].reshape(-1)
    G["sgu_norm_b"] = G["sgu_norm_b"].reshape(-1)
    return loss[0, 0], dx.reshape(B, T, D), G


_NAMES = ["mix_norm_g", "mlp_norm_g", "ab_w_in", "pool_w", "pool_scale", "conv_w", "conv_b", "ab_w_out",
          "cd_w_in", "sgu_norm_g", "sgu_norm_b", "sgu_w", "sgu_b", "cd_w_out", "mlp_w1", "mlp_w2",
          "final_norm_g"]
_COL_SHARDED = ["ab_w_in", "cd_w_in", "mlp_w1"]
_ROW_SHARDED = ["ab_w_out", "cd_w_out", "mlp_w2"]
_SMALL_SHARDED = ["conv_w", "sgu_norm_g", "sgu_norm_b"]
_REPLICATED = ["sgu_w", "mix_norm_g", "mlp_norm_g", "pool_w", "pool_scale", "conv_b", "sgu_b", "final_norm_g"]
_EARLY_SMALL = "sgu_w"


def _pad_rows(a2d, mult=SUBLANES):
    pad = (-a2d.shape[0]) % mult
    return jnp.pad(a2d, ((0, pad), (0, 0))) if pad else a2d


def _pack(arrays):
    return _pad_rows(jnp.concatenate([a.reshape(-1, LANES) for a in arrays], axis=0))


def _unpack(packed, shapes):
    out, r = [], 0
    for s in shapes:
        n = math.prod(s) // LANES
        out.append(packed[r:r + n].reshape(s))
        r += n
    return out


def _small_shard_pack(arrays):
    rows = [jnp.pad(a.reshape(-1, a.shape[-1]), ((0, 0), (0, LANES - a.shape[-1]))) for a in arrays]
    return _pad_rows(jnp.concatenate(rows, axis=0))


def _cols_to_chunks(a):
    n = a.shape[-1] // N_DEV
    return jnp.moveaxis(a.reshape(a.shape[:-1] + (N_DEV, n)), -2, 0)


def _chunks_to_cols(a):
    t = jnp.moveaxis(a, 0, -2)
    return t.reshape(t.shape[:-2] + (t.shape[-2] * t.shape[-1],))


def _rows_to_chunks(a):
    r = a.shape[-2] // N_DEV
    return jnp.moveaxis(a.reshape(a.shape[:-2] + (N_DEV, r, a.shape[-1])), -3, 0)


def _chunks_to_rows(a):
    t = jnp.moveaxis(a, 0, -3)
    return t.reshape(t.shape[:-3] + (t.shape[-3] * t.shape[-2], t.shape[-1]))


def kernel(x, mix_norm_g, mlp_norm_g, ab_w_in, pool_w, pool_scale, conv_w, conv_b, ab_w_out, cd_w_in, sgu_norm_g, sgu_norm_b, sgu_w, sgu_b, cd_w_out, mlp_w1, mlp_w2, final_norm_g, loss_target, m_mix_norm_g, m_mlp_norm_g, m_ab_w_in, m_pool_w, m_pool_scale, m_conv_w, m_conv_b, m_ab_w_out, m_cd_w_in, m_sgu_norm_g, m_sgu_norm_b, m_sgu_w, m_sgu_b, m_cd_w_out, m_mlp_w1, m_mlp_w2, m_final_norm_g, v_mix_norm_g, v_mlp_norm_g, v_ab_w_in, v_pool_w, v_pool_scale, v_conv_w, v_conv_b, v_ab_w_out, v_cd_w_in, v_sgu_norm_g, v_sgu_norm_b, v_sgu_w, v_sgu_b, v_cd_w_out, v_mlp_w1, v_mlp_w2, v_final_norm_g):
    w = dict(zip(_NAMES, (mix_norm_g, mlp_norm_g, ab_w_in, pool_w, pool_scale, conv_w, conv_b, ab_w_out, cd_w_in,
                          sgu_norm_g, sgu_norm_b, sgu_w, sgu_b, cd_w_out, mlp_w1, mlp_w2, final_norm_g)))
    m = dict(zip(_NAMES, (m_mix_norm_g, m_mlp_norm_g, m_ab_w_in, m_pool_w, m_pool_scale, m_conv_w, m_conv_b,
                          m_ab_w_out, m_cd_w_in, m_sgu_norm_g, m_sgu_norm_b, m_sgu_w, m_sgu_b, m_cd_w_out,
                          m_mlp_w1, m_mlp_w2, m_final_norm_g)))
    v = dict(zip(_NAMES, (v_mix_norm_g, v_mlp_norm_g, v_ab_w_in, v_pool_w, v_pool_scale, v_conv_w, v_conv_b,
                          v_ab_w_out, v_cd_w_in, v_sgu_norm_g, v_sgu_norm_b, v_sgu_w, v_sgu_b, v_cd_w_out,
                          v_mlp_w1, v_mlp_w2, v_final_norm_g)))
    big = _COL_SHARDED + _ROW_SHARDED
    me = _my_index()

    small_sh = _small_shard_pack([w[n] for n in _SMALL_SHARDED])
    in_lead = {"ab_w_in": pool_w.shape[1] * pool_w.shape[2], "cd_w_in": 2 * sgu_norm_g.shape[-1] * N_DEV}
    shard = {"ab_w_in": ab_w_in[0], "ab_w_out": ab_w_out[0], "cd_w_in": cd_w_in[0], "cd_w_out": cd_w_out[0]}
    for layer in range(mlp_w1.shape[0]):
        shard[f"mlp_w1_{layer}"], shard[f"mlp_w2_{layer}"] = mlp_w1[layer], mlp_w2[layer]
    shard = {n: a.astype(BF16) for n, a in shard.items()}

    def whole(n, g):
        if n.startswith("mlp_w1"):
            return g
        if n in in_lead:
            return _group_in_cols(_chunks_to_cols(g), in_lead[n])
        return _chunks_to_rows(g)

    W = {"att_width": cd_w_out.shape[1] * N_DEV - sgu_norm_g.shape[-1] * N_DEV}
    later = [(["mlp_w2_0"], "mlp0_down"), (["cd_w_in", "cd_w_out"], "cd"), (["mlp_w1_1", "mlp_w2_1"], "mlp1")]

    def start(idx, carry, then=None):
        group, tag = later[idx]
        handles, carry = _gather_start([shard[n] for n in group], carry, name=f"gather_{tag}_start")

        def finish(got):
            done = {n: whole(n, g) for n, g in zip(group, got)}
            if then is not None:
                done[group[0]] = start(then, done[group[0]])
            return done

        W.update({n: _Lazy(handles, finish, f"gather_{tag}_wait") for n in group})
        return carry

    now = ["ab_w_in", "ab_w_out", "mlp_w1_0"]
    gathered = _gather_two_level([shard[n] for n in now] + [small_sh], name="gather_weights")
    W.update({n: whole(n, g) for n, g in zip(now, gathered)})
    W["ab_w_in"] = start(1, start(0, W["ab_w_in"], then=2))
    small_full = gathered[-1]
    r = 0
    for n in _SMALL_SHARDED:
        rows, width = math.prod(w[n].shape[:-1]), w[n].shape[-1]
        W[n] = _chunks_to_cols(small_full[:, r:r + rows, :width])
        r += rows
    for n in _REPLICATED:
        W[n] = w[n]
    for n in ("pool_w", "pool_scale", "sgu_w", "sgu_b"):
        W[n] = W[n][0]

    pending, early = [], []

    def emit(grads, carry):
        names = list(grads)
        if names == [_EARLY_SMALL]:
            part = grads[_EARLY_SMALL].reshape(-1, LANES)
            handles, carry = _gather_start([part], carry, name=f"grads_{_EARLY_SMALL}_start")
            early.append(handles)
            return carry
        parts = []
        for n in names:
            if n.startswith("mlp_w1"):
                parts.append(grads[n])
            elif n in in_lead:
                parts.append(_cols_to_chunks(_ungroup_in_cols(grads[n], in_lead[n])))
            else:
                parts.append(_rows_to_chunks(grads[n]))
        handles, carry = _gather_start(parts, carry, name=f"grads_{names[0]}_start", scatter=True)
        pending.append((names, handles))
        return carry

    loss_part, grad_x, G = _local_step(x, loss_target, W, emit)

    landed = {}
    for names, handles in pending:
        got = _gather_wait(handles, grad_x, name=f"grads_{names[0]}_wait", scatter=True)
        landed.update(zip(names, got))
    small_names = _REPLICATED + _SMALL_SHARDED
    assert small_names[0] == _EARLY_SMALL and (G[_EARLY_SMALL].size // LANES) % SUBLANES == 0
    early_parts = _gather_wait(early[0], grad_x, name=f"grads_{_EARLY_SMALL}_wait")[0]
    late_grads = [G[n].reshape(-1) for n in small_names[1:]]
    loss_row = jnp.full((LANES,), loss_part, F32)
    late_parts = _exchange([_pack(late_grads + [loss_row])], [True], name="gather_small_grads")[0]
    small_parts = jnp.concatenate([early_parts, late_parts], axis=1)

    grads, deltas, new_m, new_v = {}, {}, {}, {}
    for n in big:
        layers = [landed[f"{n}_{l}"] for l in range(w[n].shape[0])] if n.startswith("mlp") else [landed[n]]
        grads[n], deltas[n], new_m[n], new_v[n] = _adamw(w[n], layers, m[n], v[n], name=f"adamw_{n}")

    rep_shapes = [w[n].shape for n in _REPLICATED]
    rep_rows = sum(math.prod(s) for s in rep_shapes) // LANES
    small_sum_shapes = [(G[n].size,) for n in small_names] + [(LANES,)]
    zero_tail = [jnp.zeros((math.prod(s),), F32) for s in small_sum_shapes[len(_REPLICATED):]]
    w_pack = _pack([w[n] for n in _REPLICATED] + zero_tail)
    m_pack = _pack([m[n] for n in _REPLICATED] + zero_tail)
    v_pack = _pack([v[n] for n in _REPLICATED] + zero_tail)
    outs = [o[0] for o in _adamw(w_pack[None], [small_parts], m_pack[None], v_pack[None], name="adamw_small")]
    summed = _unpack(outs[0], small_sum_shapes)
    for i, n in enumerate(_REPLICATED):
        grads[n] = summed[i].reshape(w[n].shape)
    for dst, o in zip((deltas, new_m, new_v), outs[1:]):
        for n, val in zip(_REPLICATED, _unpack(o[:rep_rows], rep_shapes)):
            dst[n] = val
    loss = summed[-1][0]

    shard_g = []
    for i, n in enumerate(_SMALL_SHARDED):
        full = summed[len(_REPLICATED) + i].reshape(w[n].shape[:-1] + (-1,))
        width = w[n].shape[-1]
        shard_g.append(lax.dynamic_slice_in_dim(full, me * width, width, axis=full.ndim - 1))
    g_sh = _small_shard_pack(shard_g)
    m_sh = _small_shard_pack([m[n] for n in _SMALL_SHARDED])
    v_sh = _small_shard_pack([v[n] for n in _SMALL_SHARDED])
    outs = [o[0] for o in _adamw(small_sh[None], [g_sh[None]], m_sh[None], v_sh[None], name="adamw_small_sharded")]
    r = 0
    for n in _SMALL_SHARDED:
        rows, width = math.prod(w[n].shape[:-1]), w[n].shape[-1]
        for dst, o in zip((grads, deltas, new_m, new_v), outs):
            dst[n] = o[r:r + rows, :width].reshape(w[n].shape)
        r += rows

    return (loss, grad_x, *[grads[n] for n in _NAMES], *[deltas[n] for n in _NAMES],
            *[new_m[n] for n in _NAMES], *[new_v[n] for n in _NAMES])
```

```python
import functools
import math

import jax
import jax.numpy as jnp
from jax import lax
from jax.experimental import pallas as pl
from jax.experimental.pallas import tpu as pltpu

F32 = jnp.float32
BF16 = jnp.bfloat16
GRAD_WIRE = jnp.bfloat16

NORM_EPS = 1e-6
ADAM_LR = 0.001
ADAM_B1 = 0.9
ADAM_B2 = 0.999
ADAM_EPS = 1e-08
ADAM_WD = 0.01
ADAM_STEP = 10
ADAM_C1 = 1.0 - ADAM_B1 ** ADAM_STEP
ADAM_C2 = 1.0 - ADAM_B2 ** ADAM_STEP

N_DEV = 8
LANES = 128
SUBLANES = 8
SB_DH = 64
ATT_BLOCK = 256
POOL_LOG_WINDOWS = 4
VMEM_LIMIT = 56 * 1024 * 1024


def _params(semantics=None):
    return pltpu.CompilerParams(dimension_semantics=semantics, vmem_limit_bytes=VMEM_LIMIT)


def _tile(dim, pref, unit=LANES):
    if dim <= pref:
        return dim
    t = (pref // unit) * unit
    while t >= unit:
        if dim % t == 0:
            return t
        t -= unit
    return dim


def _matmul(a, b, *, name, ta=False, tb=False, extras=(), rows=(), epilogue=None, out_dtypes=(F32,),
            tm=1024, tn=1024, tk=2048, b_chunks=False, out_chunks=0):
    M, K = (a.shape[1], a.shape[0]) if ta else a.shape
    if b_chunks:
        C, b_rows, n = b.shape
        N, tn, tk = (b_rows, tn, n) if tb else (C * n, n, tk)
        assert (C * n if tb else b_rows) == K, (a.shape, b.shape)
    else:
        N = b.shape[0] if tb else b.shape[1]
        assert (b.shape[1] if tb else b.shape[0]) == K, (a.shape, b.shape)
    if out_chunks:
        assert not extras and N % out_chunks == 0
        tn = N // out_chunks
    tm, tn, tk = _tile(M, tm), _tile(N, tn), _tile(K, tk)
    nk = K // tk
    dims = (((0 if ta else 1,), (1 if tb else 0,)), ((), ()))
    ne, no = len(extras) + len(rows), len(out_dtypes)

    def body(*refs):
        a_ref, b_ref = refs[0], refs[1]
        e_refs = refs[2:2 + ne]
        o_refs = refs[2 + ne:2 + ne + no]
        k = pl.program_id(2)

        def part():
            return lax.dot_general(a_ref[...].astype(BF16), b_ref[...].astype(BF16), dims,
                                   preferred_element_type=F32)

        def finish(acc):
            outs = epilogue(acc, *[e[...] for e in e_refs]) if epilogue is not None else (acc,)
            for o_ref, val in zip(o_refs, outs):
                o_ref[...] = val.astype(o_ref.dtype)

        if nk == 1:
            finish(part())
        else:
            acc_ref = refs[-1]

            @pl.when(k == 0)
            def _():
                acc_ref[...] = jnp.zeros_like(acc_ref)

            acc_ref[...] += part()

            @pl.when(k == nk - 1)
            def _():
                finish(acc_ref[...])

    a_spec = (pl.BlockSpec((tk, tm), lambda i, j, k: (k, i)) if ta
              else pl.BlockSpec((tm, tk), lambda i, j, k: (i, k)))
    if b_chunks:
        b_spec = (pl.BlockSpec((None, tn, tk), lambda i, j, k: (k, j, 0)) if tb
                  else pl.BlockSpec((None, tk, tn), lambda i, j, k: (j, k, 0)))
    else:
        b_spec = (pl.BlockSpec((tn, tk), lambda i, j, k: (j, k)) if tb
                  else pl.BlockSpec((tk, tn), lambda i, j, k: (k, j)))
    if out_chunks:
        o_spec = pl.BlockSpec((None, tm, tn), lambda i, j, k: (j, i, 0))
        o_shape = (out_chunks, M, tn)
    else:
        o_spec = pl.BlockSpec((tm, tn), lambda i, j, k: (i, j))
        o_shape = (M, N)
    outs = pl.pallas_call(
        body,
        name=name,
        grid=(M // tm, N // tn, nk),
        in_specs=[a_spec, b_spec] + [o_spec] * len(extras)
        + [pl.BlockSpec((1, tn), lambda i, j, k: (0, j))] * len(rows),
        out_specs=[o_spec] * no,
        out_shape=[jax.ShapeDtypeStruct(o_shape, dt) for dt in out_dtypes],
        scratch_shapes=[pltpu.VMEM((tm, tn), F32)] if nk > 1 else [],
        compiler_params=_params(("parallel", "parallel", "arbitrary")),
    )(a, b, *extras, *rows)
    return outs[0] if no == 1 else outs


def _ep_add(acc, res):
    return (acc + res,)


def _ep_add_norm(acc, res, g):
    h = acc + res
    return h, (h * _rstd(h)) * g


def _ep_relu2(acc):
    r = jnp.maximum(acc, 0.0)
    return (r * r,)


def _ep_relu2_bwd(acc, act):
    return (acc * (2.0 * jnp.sqrt(act.astype(F32))),)


def _rstd(x):
    return lax.rsqrt(jnp.mean(x * x, axis=-1, keepdims=True) + NORM_EPS)


def _rmsnorm(h, g, *, name, tr=512):
    N, D = h.shape
    tr = _tile(N, tr, SUBLANES)

    def body(h_ref, g_ref, o_ref):
        x = h_ref[...]
        o_ref[...] = ((x * _rstd(x)) * g_ref[...]).astype(o_ref.dtype)

    row = pl.BlockSpec((tr, D), lambda i: (i, 0))
    vec = pl.BlockSpec((1, D), lambda i: (0, 0))
    return pl.pallas_call(
        body, name=name, grid=(N // tr,), in_specs=[row, vec], out_specs=row,
        out_shape=jax.ShapeDtypeStruct((N, D), BF16), compiler_params=_params(("parallel",)),
    )(h, g)


def _rmsnorm_bwd(dy, h, g, dres, *, name, tr=512):
    N, D = h.shape
    tr = _tile(N, tr, SUBLANES)

    def body(dy_ref, h_ref, g_ref, r_ref, dh_ref, dhb_ref, dg_ref):
        i = pl.program_id(0)
        x = h_ref[...]
        d = dy_ref[...]
        r = _rstd(x)
        xh = x * r

        @pl.when(i == 0)
        def _():
            dg_ref[...] = jnp.zeros_like(dg_ref)

        dg_ref[...] += jnp.sum(d * xh, axis=0, keepdims=True)
        dxh = d * g_ref[...]
        dh = r_ref[...] + r * (dxh - xh * jnp.mean(dxh * xh, axis=-1, keepdims=True))
        dh_ref[...] = dh
        dhb_ref[...] = dh.astype(dhb_ref.dtype)

    row = pl.BlockSpec((tr, D), lambda i: (i, 0))
    vec = pl.BlockSpec((1, D), lambda i: (0, 0))
    return pl.pallas_call(
        body, name=name, grid=(N // tr,), in_specs=[row, row, vec, row], out_specs=[row, row, vec],
        out_shape=[jax.ShapeDtypeStruct((N, D), F32), jax.ShapeDtypeStruct((N, D), BF16),
                   jax.ShapeDtypeStruct((1, D), F32)],
        compiler_params=_params(("arbitrary",)),
    )(dy, h, g, dres)


def _final_loss(h, g, target, *, name, tr=512):
    N, D = h.shape
    tr = _tile(N, tr, SUBLANES)

    def body(h_ref, g_ref, t_ref, loss_ref, dh_ref, dhb_ref, dg_ref):
        i = pl.program_id(0)
        x = h_ref[...]
        gg = g_ref[...]
        r = _rstd(x)
        xh = x * r
        err = xh * gg - t_ref[...]

        @pl.when(i == 0)
        def _():
            dg_ref[...] = jnp.zeros_like(dg_ref)
            loss_ref[...] = jnp.zeros_like(loss_ref)

        per_row = jnp.mean(err * err, axis=-1, keepdims=True)
        loss_ref[...] += 0.5 * jnp.sum(per_row, axis=0, keepdims=True)
        dy = err * (1.0 / D)
        dg_ref[...] += jnp.sum(dy * xh, axis=0, keepdims=True)
        dxh = dy * gg
        dh = r * (dxh - xh * jnp.mean(dxh * xh, axis=-1, keepdims=True))
        dh_ref[...] = dh
        dhb_ref[...] = dh.astype(dhb_ref.dtype)

    row = pl.BlockSpec((tr, D), lambda i: (i, 0))
    vec = pl.BlockSpec((1, D), lambda i: (0, 0))
    lvec = pl.BlockSpec((1, LANES), lambda i: (0, 0))
    return pl.pallas_call(
        body, name=name, grid=(N // tr,), in_specs=[row, vec, row], out_specs=[lvec, row, row, vec],
        out_shape=[jax.ShapeDtypeStruct((1, LANES), F32), jax.ShapeDtypeStruct((N, D), F32),
                   jax.ShapeDtypeStruct((N, D), BF16), jax.ShapeDtypeStruct((1, D), F32)],
        compiler_params=_params(("arbitrary",)),
    )(h, g, target)


def _shift_down(x, s):
    t = lax.broadcasted_iota(jnp.int32, x.shape, 0)
    return jnp.where(t >= s, pltpu.roll(x, s, 0), 0.0)


def _shift_up(x, s):
    n = x.shape[0]
    t = lax.broadcasted_iota(jnp.int32, x.shape, 0)
    return jnp.where(t < n - s, pltpu.roll(x, n - s, 0), 0.0)


def _window_sum(x, g, shift):
    s = x + shift(x, 1)
    for k in range(1, POOL_LOG_WINDOWS):
        s = jnp.where(k <= g, s + shift(s, 2 ** k), s)
    return s


def _pool_count(shape, g):
    t = lax.broadcasted_iota(jnp.int32, shape, 0)
    return jnp.minimum(t + 1, lax.shift_left(jnp.int32(2), g)).astype(F32)


def _pool_fwd(p, pool_w, pool_scale, B, T, off, width, *, name):
    G, dh = pool_w.shape[0], pool_w.shape[1]
    assert G == POOL_LOG_WINDOWS and off % dh == 0
    base = off // dh

    def body(a_ref, w_ref, s_ref, o_ref):
        g = pl.program_id(0)
        a = a_ref[...]
        pooled = _window_sum(a, g, _shift_down) / _pool_count(a.shape, g) - a
        m = jnp.dot(pooled.astype(BF16), w_ref[0].astype(BF16), preferred_element_type=F32)
        o_ref[...] = (m * s_ref[0]).astype(o_ref.dtype)

    return pl.pallas_call(
        body, name=name, grid=(G, B),
        in_specs=[pl.BlockSpec((T, dh), lambda g, b: (b, base + g)),
                  pl.BlockSpec((1, dh, dh), lambda g, b: (g, 0, 0)),
                  pl.BlockSpec((1, 1, dh), lambda g, b: (g, 0, 0))],
        out_specs=pl.BlockSpec((T, dh), lambda g, b: (b, g)),
        out_shape=jax.ShapeDtypeStruct((B * T, width), BF16),
        compiler_params=_params(("parallel", "parallel")),
    )(p, pool_w, pool_scale.reshape(G, 1, dh))


def _pool_bwd(p, dcat, pool_w, pool_scale, B, T, off, *, name):
    G, dh = pool_w.shape[0], pool_w.shape[1]
    base = off // dh

    def body(a_ref, d_ref, w_ref, s_ref, da_ref, dw_ref, ds_ref):
        g = pl.program_id(0)
        b = pl.program_id(1)
        a = a_ref[...]
        d = d_ref[...]
        cnt = _pool_count(a.shape, g)
        pooled = (_window_sum(a, g, _shift_down) / cnt - a).astype(BF16)
        w = w_ref[0].astype(BF16)
        m = jnp.dot(pooled, w, preferred_element_type=F32)

        @pl.when(b == 0)
        def _():
            dw_ref[...] = jnp.zeros_like(dw_ref)
            ds_ref[...] = jnp.zeros_like(ds_ref)

        ds_ref[0] += jnp.sum(d * m, axis=0, keepdims=True)
        dm = (d * s_ref[0]).astype(BF16)
        dw_ref[0] += lax.dot_general(pooled, dm, (((0,), (0,)), ((), ())), preferred_element_type=F32)
        dpooled = lax.dot_general(dm, w, (((1,), (1,)), ((), ())), preferred_element_type=F32)
        da = _window_sum(dpooled / cnt, g, _shift_up) - dpooled
        da_ref[...] = da.astype(da_ref.dtype)

    pblk = pl.BlockSpec((T, dh), lambda g, b: (b, base + g))
    dblk = pl.BlockSpec((T, dh), lambda g, b: (b, g))
    wspec = pl.BlockSpec((1, dh, dh), lambda g, b: (g, 0, 0))
    sspec = pl.BlockSpec((1, 1, dh), lambda g, b: (g, 0, 0))
    return pl.pallas_call(
        body, name=name, grid=(G, B), in_specs=[pblk, dblk, wspec, sspec], out_specs=[pblk, wspec, sspec],
        out_shape=[jax.ShapeDtypeStruct((B * T, p.shape[1]), BF16), jax.ShapeDtypeStruct((G, dh, dh), F32),
                   jax.ShapeDtypeStruct((G, 1, dh), F32)],
        compiler_params=_params(("parallel", "arbitrary")),
    )(p, dcat, pool_w, pool_scale.reshape(G, 1, dh))


_ANY = pl.BlockSpec(memory_space=pl.ANY)


def _conv_fwd(p, cat, conv_w, conv_b, B, T, coff, *, name):
    CW = conv_w.shape[1]
    tc = LANES
    assert coff % tc == 0 and CW % tc == 0
    cbase = coff // tc

    def body(p_ref, cat_ref, w_ref, b_ref, o_ref):
        xb, gb, gc = p_ref[:, 0:tc], p_ref[:, tc:2 * tc], p_ref[:, 2 * tc:3 * tc]
        c = gc * xb
        w = w_ref[...]
        y = _shift_down(c, 2) * w[0:1] + _shift_down(c, 1) * w[1:2] + c * w[2:3] + b_ref[...]
        o_ref[...] = (gb * y).astype(o_ref.dtype)

    return pl.pallas_call(
        body, name=name, grid=(CW // tc, B),
        in_specs=[pl.BlockSpec((T, 3 * tc), lambda j, b: (b, j)), _ANY,
                  pl.BlockSpec((3, tc), lambda j, b: (0, j)), pl.BlockSpec((1, tc), lambda j, b: (0, j))],
        out_specs=pl.BlockSpec((T, tc), lambda j, b: (b, cbase + j)),
        out_shape=jax.ShapeDtypeStruct(cat.shape, cat.dtype), input_output_aliases={1: 0},
        compiler_params=_params(("parallel", "parallel")),
    )(p, cat, conv_w, conv_b)


def _conv_bwd(p, dcat, dp, conv_w, conv_b, B, T, coff, *, name):
    CW = conv_w.shape[1]
    tc = LANES
    assert coff % tc == 0
    cbase = coff // tc

    def body(p_ref, d_ref, dp_in_ref, w_ref, b_ref, dp_ref, dw_ref, db_ref):
        b = pl.program_id(1)
        xb, gb, gc = p_ref[:, 0:tc], p_ref[:, tc:2 * tc], p_ref[:, 2 * tc:3 * tc]
        d = d_ref[...]
        w = w_ref[...]
        c = gc * xb
        c1 = _shift_down(c, 1)
        c2 = _shift_down(c, 2)
        y = c2 * w[0:1] + c1 * w[1:2] + c * w[2:3] + b_ref[...]
        dy = d * gb
        dp_ref[:, tc:2 * tc] = (d * y).astype(dp_ref.dtype)

        @pl.when(b == 0)
        def _():
            dw_ref[...] = jnp.zeros_like(dw_ref)
            db_ref[...] = jnp.zeros_like(db_ref)

        db_ref[...] += jnp.sum(dy, axis=0, keepdims=True)
        dw_ref[0:1, :] += jnp.sum(dy * c2, axis=0, keepdims=True)
        dw_ref[1:2, :] += jnp.sum(dy * c1, axis=0, keepdims=True)
        dw_ref[2:3, :] += jnp.sum(dy * c, axis=0, keepdims=True)
        dc = dy * w[2:3] + _shift_up(dy, 1) * w[1:2] + _shift_up(dy, 2) * w[0:1]
        dp_ref[:, 2 * tc:3 * tc] = (dc * xb).astype(dp_ref.dtype)
        dp_ref[:, 0:tc] = (dc * gc).astype(dp_ref.dtype)

    wspec = pl.BlockSpec((3, tc), lambda j, b: (0, j))
    bspec = pl.BlockSpec((1, tc), lambda j, b: (0, j))
    pblk = pl.BlockSpec((T, 3 * tc), lambda j, b: (b, j))
    return pl.pallas_call(
        body, name=name, grid=(CW // tc, B),
        in_specs=[pblk, pl.BlockSpec((T, tc), lambda j, b: (b, cbase + j)), _ANY, wspec, bspec],
        out_specs=[pblk, wspec, bspec],
        out_shape=[jax.ShapeDtypeStruct(dp.shape, dp.dtype), jax.ShapeDtypeStruct((3, CW), F32),
                   jax.ShapeDtypeStruct((1, CW), F32)],
        input_output_aliases={2: 0},
        compiler_params=_params(("parallel", "arbitrary")),
    )(p, dcat, dp, conv_w, conv_b)


_SQRT_HALF = 0.7071067811865476
_INV_SQRT_2PI = 0.3989422804014327


def _gelu(x):
    return x * (lax.erf(x * _SQRT_HALF) + 1.0) * 0.5


def _gelu_grad(x):
    return 0.5 * (lax.erf(x * _SQRT_HALF) + 1.0) + x * (_INV_SQRT_2PI * jnp.exp(-0.5 * x * x))


def _layernorm_parts(v):
    mu = jnp.mean(v, axis=-1, keepdims=True)
    vc = v - mu
    rstd = lax.rsqrt(jnp.mean(vc * vc, axis=-1, keepdims=True) + NORM_EPS)
    return vc * rstd, rstd


def _tril_mask(L):
    r = lax.broadcasted_iota(jnp.int32, (L, L), 0)
    c = lax.broadcasted_iota(jnp.int32, (L, L), 1)
    return r >= c


def _sgu_fwd(p, ln_g, ln_b, sgu_w, sgu_b, off, width, *, name, tr=512):
    N = p.shape[0]
    G, L = sgu_w.shape[0], sgu_w.shape[1]
    SW = ln_g.shape[1]
    dh = SW // G
    assert off % SW == 0
    ub = off // SW
    tr = _tile(N, tr, L)
    assert tr % L == 0

    def body(u_ref, v_ref, g_ref, beta_ref, w_ref, b_ref, o_ref):
        u = _gelu(u_ref[...])
        vhat, _ = _layernorm_parts(_gelu(v_ref[...]))
        vn = (vhat * g_ref[...] + beta_ref[...]).astype(BF16)
        mask = _tril_mask(L)
        for gi in range(G):
            w = jnp.where(mask, w_ref[gi], 0.0).astype(BF16)
            bias = b_ref[gi]
            cols = slice(gi * dh, (gi + 1) * dh)
            for n in range(tr // L):
                rows = slice(n * L, (n + 1) * L)
                s = jnp.dot(w, vn[rows, cols], preferred_element_type=F32) + bias
                o_ref[rows, cols] = (u[rows, cols] * s).astype(o_ref.dtype)

    def col(k):
        return pl.BlockSpec((tr, SW), lambda i: (i, k))

    vec = pl.BlockSpec((1, SW), lambda i: (0, 0))
    return pl.pallas_call(
        body, name=name, grid=(N // tr,),
        in_specs=[col(ub), col(ub + 1), vec, vec, pl.BlockSpec((G, L, L), lambda i: (0, 0, 0)),
                  pl.BlockSpec((G, L, 1), lambda i: (0, 0, 0))],
        out_specs=col(0), out_shape=jax.ShapeDtypeStruct((N, width), BF16),
        compiler_params=_params(("parallel",)),
    )(p, p, ln_g, ln_b, sgu_w, sgu_b.reshape(G, L, 1))


def _sgu_bwd(p, dcat, ln_g, ln_b, sgu_w, sgu_b, off, *, name, tr=512):
    N = p.shape[0]
    G, L = sgu_w.shape[0], sgu_w.shape[1]
    SW = ln_g.shape[1]
    dh = SW // G
    assert off % SW == 0
    ub = off // SW
    tr = _tile(N, tr, L)

    def compute(i, u_ref, v_ref, dc_ref, g_ref, beta_ref, w_ref, b_ref,
                du_ref, dv_ref, dw_ref, db_ref, dg_ref, dbeta_ref, du_s, dvn_s):
        pu = u_ref[...]
        pv = v_ref[...]
        u = _gelu(pu)
        vhat, rstd = _layernorm_parts(_gelu(pv))
        gg = g_ref[...]
        vn = (vhat * gg + beta_ref[...]).astype(BF16)
        dc = dc_ref[...]
        mask = _tril_mask(L)

        @pl.when(i == 0)
        def _():
            dw_ref[...] = jnp.zeros_like(dw_ref)
            db_ref[...] = jnp.zeros_like(db_ref)
            dg_ref[...] = jnp.zeros_like(dg_ref)
            dbeta_ref[...] = jnp.zeros_like(dbeta_ref)

        for gi in range(G):
            w = jnp.where(mask, w_ref[gi], 0.0).astype(BF16)
            bias = b_ref[gi]
            cols = slice(gi * dh, (gi + 1) * dh)
            dw_acc = jnp.zeros((L, L), F32)
            db_acc = jnp.zeros((L, 1), F32)
            for n in range(tr // L):
                rows = slice(n * L, (n + 1) * L)
                vb = vn[rows, cols]
                s = jnp.dot(w, vb, preferred_element_type=F32) + bias
                du_s[rows, cols] = dc[rows, cols] * s
                ds = dc[rows, cols] * u[rows, cols]
                db_acc += jnp.sum(ds, axis=1, keepdims=True)
                dsb = ds.astype(BF16)
                dw_acc += lax.dot_general(dsb, vb, (((1,), (1,)), ((), ())), preferred_element_type=F32)
                dvn_s[rows, cols] = lax.dot_general(w, dsb, (((0,), (0,)), ((), ())),
                                                    preferred_element_type=F32)
            dw_ref[gi] += jnp.where(mask, dw_acc, 0.0)
            db_ref[gi] += db_acc

        dvn = dvn_s[...]
        dg_ref[...] += jnp.sum(dvn * vhat, axis=0, keepdims=True)
        dbeta_ref[...] += jnp.sum(dvn, axis=0, keepdims=True)
        dvh = dvn * gg
        dv = rstd * (dvh - jnp.mean(dvh, axis=-1, keepdims=True)
                     - vhat * jnp.mean(dvh * vhat, axis=-1, keepdims=True))
        dv_ref[...] = (dv * _gelu_grad(pv)).astype(dv_ref.dtype)
        du_ref[...] = (du_s[...] * _gelu_grad(pu)).astype(du_ref.dtype)

    def body(u_ref, v_ref, dc_ref, g_ref, beta_ref, w_ref, b_ref,
             dp_ref, dw_ref, db_ref, dg_ref, dbeta_ref, du_s, dvn_s, dv_s):
        i = pl.program_id(0)
        half = pl.program_id(1)

        @pl.when(half == 0)
        def _():
            compute(i, u_ref, v_ref, dc_ref, g_ref, beta_ref, w_ref, b_ref,
                    dp_ref, dv_s, dw_ref, db_ref, dg_ref, dbeta_ref, du_s, dvn_s)

        @pl.when(half == 1)
        def _():
            dp_ref[...] = dv_s[...]

    def col(k):
        return pl.BlockSpec((tr, SW), lambda i, half: (i, k))

    vec = pl.BlockSpec((1, SW), lambda i, half: (0, 0))
    wspec = pl.BlockSpec((G, L, L), lambda i, half: (0, 0, 0))
    bspec = pl.BlockSpec((G, L, 1), lambda i, half: (0, 0, 0))
    return pl.pallas_call(
        body, name=name, grid=(N // tr, 2),
        in_specs=[col(ub), col(ub + 1), col(0), vec, vec, wspec, bspec],
        out_specs=[pl.BlockSpec((tr, SW), lambda i, half: (i, ub + half)), wspec, bspec, vec, vec],
        out_shape=[jax.ShapeDtypeStruct((N, p.shape[1]), BF16),
                   jax.ShapeDtypeStruct((G, L, L), F32), jax.ShapeDtypeStruct((G, L, 1), F32),
                   jax.ShapeDtypeStruct((1, SW), F32), jax.ShapeDtypeStruct((1, SW), F32)],
        scratch_shapes=[pltpu.VMEM((tr, SW), F32), pltpu.VMEM((tr, SW), F32), pltpu.VMEM((tr, SW), BF16)],
        compiler_params=_params(("arbitrary", "arbitrary")),
    )(p, p, dcat, ln_g, ln_b, sgu_w, sgu_b.reshape(G, L, 1))


def _log_sigmoid_pair(z):
    ls = jnp.minimum(z, 0.0) - jnp.log(1.0 + jnp.exp(-jnp.abs(z)))
    return ls, ls - z


def _stacked_cumsum(xs, m):
    his = [x.astype(BF16) for x in xs]
    los = [(x - h.astype(F32)).astype(BF16) for x, h in zip(xs, his)]
    n, rows = len(xs), xs[0].shape[0]
    s = jnp.dot(jnp.concatenate(his + los, axis=0), m, preferred_element_type=F32)
    return [s[i * rows:(i + 1) * rows] + s[(n + i) * rows:(n + i + 1) * rows] for i in range(n)]


def _att_logits(qts, kbs, strict):
    ls, lks = [], []
    for qt, kb in zip(qts, kbs):
        l, lk = _log_sigmoid_pair(lax.dot_general(qt, kb, (((1,), (1,)), ((), ())), preferred_element_type=F32))
        ls.append(l)
        lks.append(lk if strict is None else jnp.where(strict, lk, 0.0))
    return ls, lks


def _att_tiles(TB):
    r = lax.broadcasted_iota(jnp.int32, (TB, TB), 0)
    c = lax.broadcasted_iota(jnp.int32, (TB, TB), 1)
    return r, c


HEADS_PER_BLOCK = LANES // SB_DH
ATT_GROUPS = 2
HP = ATT_GROUPS * HEADS_PER_BLOCK
ATT_SCALE = 1.0 / math.sqrt(SB_DH)


def _head_col(hh, part):
    return (hh // HEADS_PER_BLOCK) * 3 * LANES + part * LANES + (hh % HEADS_PER_BLOCK) * SB_DH


def _stage_heads(src_ref, part, dst_ref, T, scale=None):
    rows = _tile(T, 256, SUBLANES)

    def chunk(n, _):
        r0 = pl.multiple_of(n * rows, rows)
        for hh in range(HP):
            col = hh * SB_DH if part is None else _head_col(hh, part)
            x = src_ref[pl.ds(r0, rows), col:col + SB_DH]
            if scale is not None:
                x = x * scale
            dst_ref[hh, pl.ds(r0, rows), :] = x.astype(dst_ref.dtype)
        return 0

    lax.fori_loop(0, T // rows, chunk, 0)


def _attn_fwd(p, cat, B, T, coff, *, name):
    dh = SB_DH
    owidth = ATT_GROUPS * LANES
    nsteps = (cat.shape[1] - coff) // owidth
    TB = _tile(T, ATT_BLOCK)
    nb = T // TB
    assert nb <= LANES and coff % owidth == 0 and (cat.shape[1] - coff) % owidth == 0
    cbase = coff // owidth

    def body(p_ref, cat_ref, o_ref, c_ref, q_ref, k_ref, v_ref):
        _stage_heads(p_ref, 0, q_ref, T, ATT_SCALE)
        _stage_heads(p_ref, 1, k_ref, T)
        _stage_heads(p_ref, 2, v_ref, T)
        r, c = _att_tiles(TB)
        strict = c < r
        later = (r > c).astype(BF16)
        lane = lax.broadcasted_iota(jnp.int32, (TB, LANES), 1)

        def tiles(qts, j, carries, mask):
            k0 = pl.multiple_of(j * TB, TB)
            ls, lks = _att_logits(qts, [k_ref[hh, pl.ds(k0, TB), :] for hh in range(HP)], mask)
            sums = _stacked_cumsum(lks, later)
            out = []
            for hh in range(HP):
                suffix = sums[hh] + carries[hh]
                a = jnp.exp(ls[hh] + suffix)
                if mask is not None:
                    a = jnp.where(mask, a, 0.0)
                pv = jnp.dot(a.astype(BF16), v_ref[hh, pl.ds(k0, TB), :], preferred_element_type=F32)
                out.append((pv, suffix[:, 0:1] + lks[hh][:, 0:1]))
            return out

        def qblock(i, _):
            q0 = pl.multiple_of(i * TB, TB)
            qts = [q_ref[hh, pl.ds(q0, TB), :] for hh in range(HP)]
            state = []
            for pv, carry in tiles(qts, i, [jnp.zeros((TB, 1), F32)] * HP, strict):
                state += [pv, carry, jnp.zeros((TB, LANES), F32)]

            def kblock(jj, st):
                j = i - jj
                out = []
                for hh, (pv, new_carry) in enumerate(tiles(qts, j, st[1::3], None)):
                    acc, carry, cm = st[3 * hh:3 * hh + 3]
                    out += [acc + pv, new_carry, jnp.where(lane == j, carry, cm)]
                return tuple(out)

            st = lax.fori_loop(1, i + 1, kblock, tuple(state))
            for hh in range(HP):
                o_ref[pl.ds(q0, TB), hh * dh:(hh + 1) * dh] = st[3 * hh].astype(o_ref.dtype)
                c_ref[hh, pl.ds(q0, TB), :] = st[3 * hh + 2]
            return 0

        lax.fori_loop(0, nb, qblock, 0)

    staged = pltpu.VMEM((HP, T, dh), BF16)
    return pl.pallas_call(
        body, name=name, grid=(B, nsteps),
        in_specs=[pl.BlockSpec((T, 3 * owidth), lambda b, s: (b, s)), _ANY],
        out_specs=[pl.BlockSpec((T, owidth), lambda b, s: (b, cbase + s)),
                   pl.BlockSpec((HP, T, LANES), lambda b, s: (b * nsteps + s, 0, 0))],
        out_shape=[jax.ShapeDtypeStruct(cat.shape, cat.dtype),
                   jax.ShapeDtypeStruct((B * nsteps * HP, T, LANES), F32)],
        input_output_aliases={1: 0}, scratch_shapes=[staged, staged, staged],
        compiler_params=_params(("parallel", "parallel")),
    )(p, cat)


def _attn_bwd(p, dcat, carries, dp, B, T, coff, *, name):
    dh = SB_DH
    owidth = ATT_GROUPS * LANES
    nsteps = (dcat.shape[1] - coff) // owidth
    TB = _tile(T, ATT_BLOCK)
    nb = T // TB
    cbase = coff // owidth

    def body(p_ref, d_ref, c_ref, dp_in_ref, dp_ref, q_ref, k_ref, v_ref, do_ref, dk_ref, dv_ref):
        _stage_heads(p_ref, 0, q_ref, T, ATT_SCALE)
        _stage_heads(p_ref, 1, k_ref, T)
        _stage_heads(p_ref, 2, v_ref, T)
        _stage_heads(d_ref, None, do_ref, T)
        r, c = _att_tiles(TB)
        strict = c < r
        later = (r > c).astype(BF16)
        earlier = (r < c).astype(BF16)
        lane = lax.broadcasted_iota(jnp.int32, (TB, LANES), 1)
        dk_ref[...] = jnp.zeros_like(dk_ref)
        dv_ref[...] = jnp.zeros_like(dv_ref)

        def tiles(qts, dots, cms, j, befores, mask):
            k0 = pl.multiple_of(j * TB, TB)
            kbs = [k_ref[hh, pl.ds(k0, TB), :] for hh in range(HP)]
            ls, lks = _att_logits(qts, kbs, mask)
            sums = _stacked_cumsum(lks, later)
            gls = []
            for hh in range(HP):
                carry = jnp.sum(jnp.where(lane == j, cms[hh], 0.0), axis=1, keepdims=True)
                a = jnp.exp(ls[hh] + sums[hh] + carry)
                if mask is not None:
                    a = jnp.where(mask, a, 0.0)
                dv_ref[hh, pl.ds(k0, TB), :] += lax.dot_general(a.astype(BF16), dots[hh], (((0,), (0,)), ((), ())),
                                                                preferred_element_type=F32)
                da = lax.dot_general(dots[hh], v_ref[hh, pl.ds(k0, TB), :], (((1,), (1,)), ((), ())),
                                     preferred_element_type=F32)
                gls.append(a * da)
            pres = _stacked_cumsum(gls, earlier)
            out = []
            for hh in range(HP):
                prefix = pres[hh] + befores[hh]
                dz = gls[hh] - jnp.exp(ls[hh]) * (gls[hh] + prefix)
                if mask is not None:
                    dz = jnp.where(mask, dz, 0.0)
                dzb = dz.astype(BF16)
                dk_ref[hh, pl.ds(k0, TB), :] += lax.dot_general(dzb, qts[hh], (((0,), (0,)), ((), ())),
                                                                preferred_element_type=F32)
                out.append((jnp.dot(dzb, kbs[hh], preferred_element_type=F32),
                            prefix[:, TB - 1:TB] + gls[hh][:, TB - 1:TB]))
            return out

        def qblock(i, _):
            q0 = pl.multiple_of(i * TB, TB)
            qts = [q_ref[hh, pl.ds(q0, TB), :] for hh in range(HP)]
            dots = [do_ref[hh, pl.ds(q0, TB), :] for hh in range(HP)]
            cms = [c_ref[hh, pl.ds(q0, TB), :] for hh in range(HP)]

            def kblock(j, st):
                out = []
                for hh, (part, new_before) in enumerate(tiles(qts, dots, cms, j, st[1::2], None)):
                    out += [st[2 * hh] + part, new_before]
                return tuple(out)

            st = lax.fori_loop(0, i, kblock, (jnp.zeros((TB, dh), F32), jnp.zeros((TB, 1), F32)) * HP)
            for hh, (part, _) in enumerate(tiles(qts, dots, cms, i, st[1::2], strict)):
                dq = (st[2 * hh] + part) * ATT_SCALE
                dp_ref[pl.ds(q0, TB), _head_col(hh, 0):_head_col(hh, 0) + dh] = dq.astype(dp_ref.dtype)
            return 0

        lax.fori_loop(0, nb, qblock, 0)

        def write_back(n, _):
            r0 = pl.multiple_of(n * TB, TB)
            for hh in range(HP):
                dp_ref[pl.ds(r0, TB), _head_col(hh, 1):_head_col(hh, 1) + dh] = (
                    dk_ref[hh, pl.ds(r0, TB), :].astype(dp_ref.dtype))
                dp_ref[pl.ds(r0, TB), _head_col(hh, 2):_head_col(hh, 2) + dh] = (
                    dv_ref[hh, pl.ds(r0, TB), :].astype(dp_ref.dtype))
            return 0

        lax.fori_loop(0, nb, write_back, 0)

    pblk = pl.BlockSpec((T, 3 * owidth), lambda b, s: (b, s))
    staged = pltpu.VMEM((HP, T, dh), BF16)
    accum = pltpu.VMEM((HP, T, dh), F32)
    return pl.pallas_call(
        body, name=name, grid=(B, nsteps),
        in_specs=[pblk, pl.BlockSpec((T, owidth), lambda b, s: (b, cbase + s)),
                  pl.BlockSpec((HP, T, LANES), lambda b, s: (b * nsteps + s, 0, 0)), _ANY],
        out_specs=pblk, out_shape=jax.ShapeDtypeStruct(dp.shape, dp.dtype), input_output_aliases={3: 0},
        scratch_shapes=[staged, staged, staged, staged, accum, accum],
        compiler_params=_params(("parallel", "parallel")),
    )(p, dcat, carries, dp)


def _adamw(w, gparts, m, v, *, name):
    L, R, C = w.shape
    P = gparts[0].shape[0]
    assert len(gparts) == L
    tr = _tile(R, max(SUBLANES, (1 << 21) // (C * P)), SUBLANES)

    def body(*refs):
        w_ref, g_refs, (m_ref, v_ref) = refs[0], refs[1:1 + L], refs[1 + L:3 + L]
        go_ref, d_ref, mo_ref, vo_ref = refs[3 + L:]
        layer = pl.program_id(0)

        def update(g_ref):
            g = g_ref[0].astype(F32)
            for i in range(1, P):
                g = g + g_ref[i].astype(F32)
            m2 = ADAM_B1 * m_ref[...] + (1.0 - ADAM_B1) * g
            v2 = ADAM_B2 * v_ref[...] + (1.0 - ADAM_B2) * (g * g)
            m_hat = m2 / ADAM_C1
            v_hat = v2 / ADAM_C2
            go_ref[...] = g
            d_ref[...] = -ADAM_LR * (m_hat / (jnp.sqrt(v_hat) + ADAM_EPS) + ADAM_WD * w_ref[...])
            mo_ref[...] = m2
            vo_ref[...] = v2

        for l in range(L):
            pl.when(layer == l)(functools.partial(update, g_refs[l]))

    row = pl.BlockSpec((None, tr, C), lambda l, i: (l, i, 0))

    def part(mine):
        return pl.BlockSpec((P, tr, C), lambda l, i: (0, jnp.where(l == mine, i, 0), 0))

    shp = jax.ShapeDtypeStruct((L, R, C), F32)
    return pl.pallas_call(
        body, name=name, grid=(L, R // tr),
        in_specs=[row] + [part(l) for l in range(L)] + [row, row],
        out_specs=[row] * 4, out_shape=[shp] * 4, compiler_params=_params(("arbitrary", "arbitrary")),
    )(w, *gparts, m, v)


def _my_index():
    return 4 * lax.axis_index("x") + 2 * lax.axis_index("y") + lax.axis_index("c")


def _exchange(arrs, gather, *, name):
    n = len(arrs)

    def body(*refs):
        ins, outs = refs[:n], refs[n:2 * n]
        send_sems, recv_sems, local_sems = refs[2 * n:]
        x, y, c = lax.axis_index("x"), lax.axis_index("y"), lax.axis_index("c")
        me = 4 * x + 2 * y + c
        remote, local = [], []
        for a in range(n):
            own = ins[a] if gather[a] else ins[a].at[me]
            cp = pltpu.make_async_copy(own, outs[a].at[me], local_sems.at[a])
            cp.start()
            local.append(cp)
            for k in range(1, N_DEV):
                px = 1 - x if k & 4 else x
                py = 1 - y if k & 2 else y
                pc = 1 - c if k & 1 else c
                src = ins[a] if gather[a] else ins[a].at[4 * px + 2 * py + pc]
                cp = pltpu.make_async_remote_copy(
                    src_ref=src, dst_ref=outs[a].at[me],
                    send_sem=send_sems.at[a, k - 1], recv_sem=recv_sems.at[a, k - 1],
                    device_id=(px, py, pc), device_id_type=pl.DeviceIdType.MESH)
                cp.start()
                remote.append(cp)
        for cp in remote:
            cp.wait()
        for cp in local:
            cp.wait()

    hbm = pl.BlockSpec(memory_space=pltpu.HBM)
    out_shape = [jax.ShapeDtypeStruct(((N_DEV,) + a.shape) if g else a.shape, a.dtype)
                 for a, g in zip(arrs, gather)]
    return pl.pallas_call(
        body, name=name, in_specs=[hbm] * n, out_specs=[hbm] * n, out_shape=out_shape,
        scratch_shapes=[pltpu.SemaphoreType.DMA((n, N_DEV - 1)), pltpu.SemaphoreType.DMA((n, N_DEV - 1)),
                        pltpu.SemaphoreType.DMA((n,))],
    )(*arrs)


_HBM = pl.BlockSpec(memory_space=pltpu.HBM)


def _other_chips(x, y):
    return [(1 - x, y), (x, 1 - y), (1 - x, 1 - y)]


def _gather_two_level(arrs, *, name):
    n = len(arrs)

    def body(*refs):
        ins, outs = refs[:n], refs[n:2 * n]
        send_sems, recv_sems, local_sems = refs[2 * n:]
        x, y, c = lax.axis_index("x"), lax.axis_index("y"), lax.axis_index("c")
        me, sibling = (x, y, c), (x, y, 1 - c)
        chips = _other_chips(x, y)

        def slot(a, px, py, pc):
            return outs[a].at[4 * px + 2 * py + pc]

        def copy(a, k, block, to, src=None):
            return pltpu.make_async_remote_copy(
                src_ref=slot(a, *block) if src is None else src, dst_ref=slot(a, *block),
                send_sem=send_sems.at[a, k], recv_sem=recv_sems.at[a, k],
                device_id=to, device_id_type=pl.DeviceIdType.MESH)

        local, sends = [], []
        for a in range(n):
            cp = pltpu.make_async_copy(ins[a], slot(a, *me), local_sems.at[a])
            cp.start()
            local.append(cp)
            first = [copy(a, 0, me, sibling, src=ins[a])]
            first += [copy(a, 1 + j, me, (*chip, c), src=ins[a]) for j, chip in enumerate(chips)]
            for cp in first:
                cp.start()
            sends += first
        for j, chip in enumerate(chips):
            for a in range(n):
                copy(a, 1 + j, (*chip, c), me).wait_recv()
                cp = copy(a, 4 + j, (*chip, c), sibling)
                cp.start()
                sends.append(cp)
        for a in range(n):
            copy(a, 0, sibling, me).wait_recv()
            for j, chip in enumerate(chips):
                copy(a, 4 + j, (*chip, 1 - c), me).wait_recv()
        for cp in sends:
            cp.wait_send()
        for cp in local:
            cp.wait()

    return pl.pallas_call(
        body, name=name, in_specs=[_HBM] * n, out_specs=[_HBM] * n,
        out_shape=[jax.ShapeDtypeStruct((N_DEV,) + a.shape, a.dtype) for a in arrs],
        scratch_shapes=[pltpu.SemaphoreType.DMA((n, N_DEV - 1)), pltpu.SemaphoreType.DMA((n, N_DEV - 1)),
                        pltpu.SemaphoreType.DMA((n,))],
    )(*arrs)


_SEM = pl.BlockSpec(memory_space=pltpu.SEMAPHORE)
_SPLIT_COPY = pltpu.SideEffectType.DATAFLOW_SIDE_EFFECTING


def _peers(x, y, c):
    return [((1 - x if k & 4 else x), (1 - y if k & 2 else y), (1 - c if k & 1 else c)) for k in range(1, N_DEV)]


_SPLIT_SEMS = 2 * (N_DEV - 1) + 1


def _split_sems(sems, a):
    mine = sems[a * _SPLIT_SEMS:(a + 1) * _SPLIT_SEMS]
    return mine[:N_DEV - 1], mine[N_DEV - 1:2 * (N_DEV - 1)], mine[-1]


def _split_src(ref, scatter, index):
    return ref.at[index] if scatter else ref


def _gather_start(arrs, carry, *, name, scatter=False):
    n = len(arrs)
    ns = n * _SPLIT_SEMS

    def body(*refs):
        ins, lands = refs[:n], refs[n:2 * n]
        sems = refs[2 * n + 1:2 * n + 1 + ns]
        x, y, c = lax.axis_index("x"), lax.axis_index("y"), lax.axis_index("c")
        me = 4 * x + 2 * y + c
        for a in range(n):
            send, recv, local = _split_sems(sems, a)
            pltpu.make_async_copy(_split_src(ins[a], scatter, me), lands[a].at[me], local).start()
            for k, (px, py, pc) in enumerate(_peers(x, y, c)):
                pltpu.make_async_remote_copy(
                    src_ref=_split_src(ins[a], scatter, 4 * px + 2 * py + pc), dst_ref=lands[a].at[me],
                    send_sem=send[k], recv_sem=recv[k],
                    device_id=(px, py, pc), device_id_type=pl.DeviceIdType.MESH).start()

    lands = [lax.empty(a.shape if scatter else (N_DEV,) + a.shape, a.dtype) for a in arrs]
    operands = [pltpu.with_memory_space_constraint(a, pltpu.HBM) for a in list(arrs) + lands + [carry]]
    outs = pl.pallas_call(
        body, name=name, in_specs=[_HBM] * (2 * n + 1), out_specs=[_SEM] * ns + [_HBM] * (2 * n + 1),
        out_shape=[pltpu.SemaphoreType.DMA(())] * ns + [pltpu.HBM(a.shape, a.dtype) for a in operands],
        input_output_aliases={i: ns + i for i in range(2 * n + 1)},
        compiler_params=pltpu.CompilerParams(has_side_effects=_SPLIT_COPY),
    )(*operands)
    return tuple(outs[:-1]), outs[-1]


def _gather_wait(handles, after, *, name, scatter=False):
    n = len(handles) // (_SPLIT_SEMS + 2)
    ns = n * _SPLIT_SEMS
    sems, thru = handles[:ns], handles[ns:]

    def body(*refs):
        ins, lands = refs[:n], refs[n:2 * n]
        sems = refs[2 * n:2 * n + ns]
        x, y, c = lax.axis_index("x"), lax.axis_index("y"), lax.axis_index("c")
        me = 4 * x + 2 * y + c
        for a in range(n):
            send, recv, local = _split_sems(sems, a)
            src = _split_src(ins[a], scatter, me)
            pltpu.make_async_copy(src, lands[a].at[me], local).wait()
            for k, peer in enumerate(_peers(x, y, c)):
                cp = pltpu.make_async_remote_copy(
                    src_ref=src, dst_ref=lands[a].at[me], send_sem=send[k], recv_sem=recv[k],
                    device_id=peer, device_id_type=pl.DeviceIdType.MESH)
                cp.wait_send()
                cp.wait_recv()

    outs = pl.pallas_call(
        body, name=name, in_specs=[_HBM] * (2 * n) + [_SEM] * ns + [_ANY], out_specs=[_HBM] * (2 * n),
        out_shape=[pltpu.HBM(a.shape, a.dtype) for a in thru],
        input_output_aliases={i: i for i in range(2 * n)},
        compiler_params=pltpu.CompilerParams(has_side_effects=_SPLIT_COPY),
    )(*thru, *sems, after)
    return outs[n:]


def _group_in_cols(w, lead):
    X = (w.shape[-1] - lead) // 3
    g = w[..., lead:].reshape(w.shape[:-1] + (3, X // LANES, LANES))
    g = jnp.swapaxes(g, -3, -2).reshape(w.shape[:-1] + (3 * X,))
    return jnp.concatenate([g, w[..., :lead]], axis=-1)


def _ungroup_in_cols(w, lead):
    X = (w.shape[-1] - lead) // 3
    g = w[..., :3 * X].reshape(w.shape[:-1] + (X // LANES, 3, LANES))
    g = jnp.swapaxes(g, -3, -2).reshape(w.shape[:-1] + (3 * X,))
    return jnp.concatenate([w[..., 3 * X:], g], axis=-1)


def _residual_matmul(a, w, res, gain, *, name):
    if gain is None:
        return _matmul(a, w, name=name, extras=(res,), epilogue=_ep_add), None
    return _matmul(a, w, name=name, extras=(res,), rows=(gain,), epilogue=_ep_add_norm, out_dtypes=(F32, BF16),
                   tn=res.shape[1])


def _mlp_fwd(h, hn, W, tag, next_gain=None):
    w1 = _weight(W, f"mlp_w1_{tag}", hn)
    act = _matmul(hn, w1, name=f"mlp_up_{tag}", epilogue=_ep_relu2, out_dtypes=(BF16,), b_chunks=True, tm=2048)
    out, out_n = _residual_matmul(act, _weight(W, f"mlp_w2_{tag}", act), h, next_gain, name=f"mlp_down_{tag}")
    return out, out_n, (hn, act, _chunks_to_cols(w1))


def _mlp_bwd(dout, dout_b, h, g, w2, saved, tag, sent):
    hn, act, w1 = saved
    dw2 = _matmul(act, dout_b, ta=True, name=f"mlp_dw2_{tag}", out_dtypes=(GRAD_WIRE,))
    dout_b = sent(dout_b, {f"mlp_w2_{tag}": dw2})
    dz = _matmul(dout_b, w2, tb=True, name=f"mlp_dact_{tag}", extras=(act,), epilogue=_ep_relu2_bwd,
                 out_dtypes=(BF16,))
    dw1 = _matmul(hn, dz, ta=True, name=f"mlp_dw1_{tag}", out_dtypes=(GRAD_WIRE,), out_chunks=N_DEV, tk=4096)
    dz = sent(dz, {f"mlp_w1_{tag}": dw1})
    dhn = _matmul(dz, w1, tb=True, name=f"mlp_dhn_{tag}")
    dh, dh_b, dg = _rmsnorm_bwd(dhn, h, g, dout, name=f"mlp_norm_bwd_{tag}")
    return dh, dh_b, dg, dw1, dw2


class _Lazy:
    def __init__(self, handles, finish, name):
        self.handles, self.finish, self.name, self.done = handles, finish, name, None

    def take(self, after):
        if self.done is None:
            self.done = self.finish(_gather_wait(self.handles, after, name=self.name))
        return self.done


def _weight(W, n, after):
    if isinstance(W[n], _Lazy):
        W.update(W[n].take(after))
    return W[n]


def _local_step(x, target, W, emit=None):
    B, T, D = x.shape
    N = B * T
    G = {}
    row = lambda vec: vec.reshape(1, -1)

    def sent(nxt, grads):
        return nxt if emit is None else emit(grads, nxt)

    PW = W["pool_w"].shape[0] * W["pool_w"].shape[1]
    CW = W["conv_b"].shape[-1]
    SW = W["sgu_norm_g"].shape[-1]
    HW = W["att_width"]

    h0 = x.reshape(N, D)
    xn0 = _rmsnorm(h0, row(W["mix_norm_g"][0]), name="mix_norm_0")
    p0 = _matmul(xn0, W["ab_w_in"], name="ab_in")
    cat0 = _pool_fwd(p0, W["pool_w"], W["pool_scale"], B, T, 3 * CW, PW + CW, name="pool_fwd")
    cat0 = _conv_fwd(p0, cat0, W["conv_w"], row(W["conv_b"]), B, T, PW, name="conv_fwd")
    h1, hn0 = _residual_matmul(cat0, W["ab_w_out"], h0, row(W["mlp_norm_g"][0]), name="ab_out")
    h2, xn1, mlp0 = _mlp_fwd(h1, hn0, W, 0, row(W["mix_norm_g"][1]))
    p1 = _matmul(xn1, _weight(W, "cd_w_in", xn1), name="cd_in")
    ln_g, ln_b = row(W["sgu_norm_g"]), row(W["sgu_norm_b"])
    cat1 = _sgu_fwd(p1, ln_g, ln_b, W["sgu_w"], W["sgu_b"], 3 * HW, SW + HW, name="sgu_fwd")
    cat1, att_carries = _attn_fwd(p1, cat1, B, T, SW, name="attn_fwd")
    h3, hn1 = _residual_matmul(cat1, W["cd_w_out"], h2, row(W["mlp_norm_g"][1]), name="cd_out")
    h4, _, mlp1 = _mlp_fwd(h3, hn1, W, 1)

    loss, dh4, dh4_b, G["final_norm_g"] = _final_loss(h4, row(W["final_norm_g"]), target.reshape(N, D),
                                                      name="final_loss")

    dh3, dh3_b, dmlp_g1, dw1_1, dw2_1 = _mlp_bwd(dh4, dh4_b, h3, row(W["mlp_norm_g"][1]), W["mlp_w2_1"], mlp1, 1,
                                                 sent)
    G["cd_w_out"] = _matmul(cat1, dh3_b, ta=True, name="cd_out_dw", out_dtypes=(GRAD_WIRE,))[None]
    dh3_b = sent(dh3_b, {"cd_w_out": G["cd_w_out"][0]})
    dcat1 = _matmul(dh3_b, W["cd_w_out"], tb=True, name="cd_out_dx")
    dp1, G["sgu_w"], dsgu_b, G["sgu_norm_g"], G["sgu_norm_b"] = _sgu_bwd(
        p1, dcat1, ln_g, ln_b, W["sgu_w"], W["sgu_b"], 3 * HW, name="sgu_bwd")
    G["sgu_b"] = dsgu_b.reshape(W["sgu_b"].shape)
    dp1 = sent(dp1, {"sgu_w": G["sgu_w"]})
    dp1 = _attn_bwd(p1, dcat1, att_carries, dp1, B, T, SW, name="attn_bwd")
    G["cd_w_in"] = _matmul(xn1, dp1, ta=True, name="cd_in_dw", out_dtypes=(GRAD_WIRE,))[None]
    dp1 = sent(dp1, {"cd_w_in": G["cd_w_in"][0]})
    dxn1 = _matmul(dp1, W["cd_w_in"], tb=True, name="cd_in_dx")
    dh2, dh2_b, dmix_g1 = _rmsnorm_bwd(dxn1, h2, row(W["mix_norm_g"][1]), dh3, name="mix_norm_bwd_1")

    dh1, dh1_b, dmlp_g0, dw1_0, dw2_0 = _mlp_bwd(dh2, dh2_b, h1, row(W["mlp_norm_g"][0]), W["mlp_w2_0"], mlp0, 0,
                                                 sent)
    G["ab_w_out"] = _matmul(cat0, dh1_b, ta=True, name="ab_out_dw", out_dtypes=(GRAD_WIRE,))[None]
    dh1_b = sent(dh1_b, {"ab_w_out": G["ab_w_out"][0]})
    dcat0 = _matmul(dh1_b, W["ab_w_out"], tb=True, name="ab_out_dx")
    dp0, G["pool_w"], dps = _pool_bwd(p0, dcat0, W["pool_w"], W["pool_scale"], B, T, 3 * CW, name="pool_bwd")
    dp0 = sent(dp0, {"pool_w": G["pool_w"]})
    G["pool_scale"] = dps.reshape(W["pool_scale"].shape)
    dp0, G["conv_w"], dcb = _conv_bwd(p0, dcat0, dp0, W["conv_w"], row(W["conv_b"]), B, T, PW, name="conv_bwd")
    G["conv_b"] = dcb.reshape(-1)
    G["ab_w_in"] = _matmul(xn0, dp0, ta=True, name="ab_in_dw", out_dtypes=(GRAD_WIRE,))[None]
    dp0 = sent(dp0, {"ab_w_in": G["ab_w_in"][0]})
    dxn0 = _matmul(dp0, W["ab_w_in"], tb=True, name="ab_in_dx")
    dx, _, dmix_g0 = _rmsnorm_bwd(dxn0, h0, row(W["mix_norm_g"][0]), dh1, name="mix_norm_bwd_0")

    G["mix_norm_g"] = jnp.concatenate([dmix_g0, dmix_g1], axis=0)
    G["mlp_norm_g"] = jnp.concatenate([dmlp_g0, dmlp_g1], axis=0)
    G["mlp_w1"] = jnp.stack([_chunks_to_cols(dw1_0), _chunks_to_cols(dw1_1)])
    G["mlp_w2"] = jnp.stack([dw2_0, dw2_1])
    G["final_norm_g"] = G["final_norm_g"].reshape(-1)
    G["sgu_norm_g"] = G["sgu_norm_g"].reshape(-1)
    G["sgu_norm_b"] = G["sgu_norm_b"].reshape(-1)
    return loss[0, 0], dx.reshape(B, T, D), G


_NAMES = ["mix_norm_g", "mlp_norm_g", "ab_w_in", "pool_w", "pool_scale", "conv_w", "conv_b", "ab_w_out",
          "cd_w_in", "sgu_norm_g", "sgu_norm_b", "sgu_w", "sgu_b", "cd_w_out", "mlp_w1", "mlp_w2",
          "final_norm_g"]
_COL_SHARDED = ["ab_w_in", "cd_w_in", "mlp_w1"]
_ROW_SHARDED = ["ab_w_out", "cd_w_out", "mlp_w2"]
_SMALL_SHARDED = ["conv_w", "sgu_norm_g", "sgu_norm_b"]
_REPLICATED = ["sgu_w", "pool_w", "mix_norm_g", "mlp_norm_g", "pool_scale", "conv_b", "sgu_b", "final_norm_g"]
_EARLY_SMALL = _REPLICATED[:2]


def _pad_rows(a2d, mult=SUBLANES):
    pad = (-a2d.shape[0]) % mult
    return jnp.pad(a2d, ((0, pad), (0, 0))) if pad else a2d


def _pack(arrays):
    return _pad_rows(jnp.concatenate([a.reshape(-1, LANES) for a in arrays], axis=0))


def _unpack(packed, shapes):
    out, r = [], 0
    for s in shapes:
        n = math.prod(s) // LANES
        out.append(packed[r:r + n].reshape(s))
        r += n
    return out


def _small_shard_pack(arrays):
    rows = [jnp.pad(a.reshape(-1, a.shape[-1]), ((0, 0), (0, LANES - a.shape[-1]))) for a in arrays]
    return _pad_rows(jnp.concatenate(rows, axis=0))


def _cols_to_chunks(a):
    n = a.shape[-1] // N_DEV
    return jnp.moveaxis(a.reshape(a.shape[:-1] + (N_DEV, n)), -2, 0)


def _chunks_to_cols(a):
    t = jnp.moveaxis(a, 0, -2)
    return t.reshape(t.shape[:-2] + (t.shape[-2] * t.shape[-1],))


def _rows_to_chunks(a):
    r = a.shape[-2] // N_DEV
    return jnp.moveaxis(a.reshape(a.shape[:-2] + (N_DEV, r, a.shape[-1])), -3, 0)


def _chunks_to_rows(a):
    t = jnp.moveaxis(a, 0, -3)
    return t.reshape(t.shape[:-3] + (t.shape[-3] * t.shape[-2], t.shape[-1]))


def kernel(x, mix_norm_g, mlp_norm_g, ab_w_in, pool_w, pool_scale, conv_w, conv_b, ab_w_out, cd_w_in, sgu_norm_g, sgu_norm_b, sgu_w, sgu_b, cd_w_out, mlp_w1, mlp_w2, final_norm_g, loss_target, m_mix_norm_g, m_mlp_norm_g, m_ab_w_in, m_pool_w, m_pool_scale, m_conv_w, m_conv_b, m_ab_w_out, m_cd_w_in, m_sgu_norm_g, m_sgu_norm_b, m_sgu_w, m_sgu_b, m_cd_w_out, m_mlp_w1, m_mlp_w2, m_final_norm_g, v_mix_norm_g, v_mlp_norm_g, v_ab_w_in, v_pool_w, v_pool_scale, v_conv_w, v_conv_b, v_ab_w_out, v_cd_w_in, v_sgu_norm_g, v_sgu_norm_b, v_sgu_w, v_sgu_b, v_cd_w_out, v_mlp_w1, v_mlp_w2, v_final_norm_g):
    w = dict(zip(_NAMES, (mix_norm_g, mlp_norm_g, ab_w_in, pool_w, pool_scale, conv_w, conv_b, ab_w_out, cd_w_in,
                          sgu_norm_g, sgu_norm_b, sgu_w, sgu_b, cd_w_out, mlp_w1, mlp_w2, final_norm_g)))
    m = dict(zip(_NAMES, (m_mix_norm_g, m_mlp_norm_g, m_ab_w_in, m_pool_w, m_pool_scale, m_conv_w, m_conv_b,
                          m_ab_w_out, m_cd_w_in, m_sgu_norm_g, m_sgu_norm_b, m_sgu_w, m_sgu_b, m_cd_w_out,
                          m_mlp_w1, m_mlp_w2, m_final_norm_g)))
    v = dict(zip(_NAMES, (v_mix_norm_g, v_mlp_norm_g, v_ab_w_in, v_pool_w, v_pool_scale, v_conv_w, v_conv_b,
                          v_ab_w_out, v_cd_w_in, v_sgu_norm_g, v_sgu_norm_b, v_sgu_w, v_sgu_b, v_cd_w_out,
                          v_mlp_w1, v_mlp_w2, v_final_norm_g)))
    big = _COL_SHARDED + _ROW_SHARDED
    me = _my_index()

    small_sh = _small_shard_pack([w[n] for n in _SMALL_SHARDED])
    in_lead = {"ab_w_in": pool_w.shape[1] * pool_w.shape[2], "cd_w_in": 2 * sgu_norm_g.shape[-1] * N_DEV}
    shard = {"ab_w_in": ab_w_in[0], "ab_w_out": ab_w_out[0], "cd_w_in": cd_w_in[0], "cd_w_out": cd_w_out[0]}
    for layer in range(mlp_w1.shape[0]):
        shard[f"mlp_w1_{layer}"], shard[f"mlp_w2_{layer}"] = mlp_w1[layer], mlp_w2[layer]
    shard = {n: a.astype(BF16) for n, a in shard.items()}

    def whole(n, g):
        if n.startswith("mlp_w1"):
            return g
        if n in in_lead:
            return _group_in_cols(_chunks_to_cols(g), in_lead[n])
        return _chunks_to_rows(g)

    W = {"att_width": cd_w_out.shape[1] * N_DEV - sgu_norm_g.shape[-1] * N_DEV}
    later = [(["mlp_w2_0"], "mlp0_down"), (["cd_w_in", "cd_w_out"], "cd"), (["mlp_w1_1", "mlp_w2_1"], "mlp1")]

    def start(idx, carry, then=None):
        group, tag = later[idx]
        handles, carry = _gather_start([shard[n] for n in group], carry, name=f"gather_{tag}_start")

        def finish(got):
            done = {n: whole(n, g) for n, g in zip(group, got)}
            if then is not None:
                done[group[0]] = start(then, done[group[0]])
            return done

        W.update({n: _Lazy(handles, finish, f"gather_{tag}_wait") for n in group})
        return carry

    now = ["ab_w_in", "ab_w_out", "mlp_w1_0"]
    gathered = _gather_two_level([shard[n] for n in now] + [small_sh], name="gather_weights")
    W.update({n: whole(n, g) for n, g in zip(now, gathered)})
    W["ab_w_in"] = start(1, start(0, W["ab_w_in"], then=2))
    small_full = gathered[-1]
    r = 0
    for n in _SMALL_SHARDED:
        rows, width = math.prod(w[n].shape[:-1]), w[n].shape[-1]
        W[n] = _chunks_to_cols(small_full[:, r:r + rows, :width])
        r += rows
    for n in _REPLICATED:
        W[n] = w[n]
    for n in ("pool_w", "pool_scale", "sgu_w", "sgu_b"):
        W[n] = W[n][0]

    pending, early = [], {}

    def emit(grads, carry):
        names = list(grads)
        if names[0] in _EARLY_SMALL:
            part = grads[names[0]].reshape(-1, LANES)
            assert part.shape[0] % SUBLANES == 0
            early[names[0]], carry = _gather_start([part], carry, name=f"grads_{names[0]}_start")
            return carry
        parts = []
        for n in names:
            if n.startswith("mlp_w1"):
                parts.append(grads[n])
            elif n in in_lead:
                parts.append(_cols_to_chunks(_ungroup_in_cols(grads[n], in_lead[n])))
            else:
                parts.append(_rows_to_chunks(grads[n]))
        handles, carry = _gather_start(parts, carry, name=f"grads_{names[0]}_start", scatter=True)
        pending.append((names, handles))
        return carry

    loss_part, grad_x, G = _local_step(x, loss_target, W, emit)

    landed = {}
    for names, handles in pending:
        got = _gather_wait(handles, grad_x, name=f"grads_{names[0]}_wait", scatter=True)
        landed.update(zip(names, got))
    small_names = _REPLICATED + _SMALL_SHARDED
    early_parts = [_gather_wait(early[n], grad_x, name=f"grads_{n}_wait")[0] for n in _EARLY_SMALL]
    late_grads = [G[n].reshape(-1) for n in small_names[len(_EARLY_SMALL):]]
    loss_row = jnp.full((LANES,), loss_part, F32)
    late_parts = _exchange([_pack(late_grads + [loss_row])], [True], name="gather_small_grads")[0]
    small_parts = jnp.concatenate(early_parts + [late_parts], axis=1)

    grads, deltas, new_m, new_v = {}, {}, {}, {}
    for n in big:
        layers = [landed[f"{n}_{l}"] for l in range(w[n].shape[0])] if n.startswith("mlp") else [landed[n]]
        grads[n], deltas[n], new_m[n], new_v[n] = _adamw(w[n], layers, m[n], v[n], name=f"adamw_{n}")

    rep_shapes = [w[n].shape for n in _REPLICATED]
    rep_rows = sum(math.prod(s) for s in rep_shapes) // LANES
    small_sum_shapes = [(G[n].size,) for n in small_names] + [(LANES,)]
    zero_tail = [jnp.zeros((math.prod(s),), F32) for s in small_sum_shapes[len(_REPLICATED):]]
    w_pack = _pack([w[n] for n in _REPLICATED] + zero_tail)
    m_pack = _pack([m[n] for n in _REPLICATED] + zero_tail)
    v_pack = _pack([v[n] for n in _REPLICATED] + zero_tail)
    outs = [o[0] for o in _adamw(w_pack[None], [small_parts], m_pack[None], v_pack[None], name="adamw_small")]
    summed = _unpack(outs[0], small_sum_shapes)
    for i, n in enumerate(_REPLICATED):
        grads[n] = summed[i].reshape(w[n].shape)
    for dst, o in zip((deltas, new_m, new_v), outs[1:]):
        for n, val in zip(_REPLICATED, _unpack(o[:rep_rows], rep_shapes)):
            dst[n] = val
    loss = summed[-1][0]

    shard_g = []
    for i, n in enumerate(_SMALL_SHARDED):
        full = summed[len(_REPLICATED) + i].reshape(w[n].shape[:-1] + (-1,))
        width = w[n].shape[-1]
        shard_g.append(lax.dynamic_slice_in_dim(full, me * width, width, axis=full.ndim - 1))
    g_sh = _small_shard_pack(shard_g)
    m_sh = _small_shard_pack([m[n] for n in _SMALL_SHARDED])
    v_sh = _small_shard_pack([v[n] for n in _SMALL_SHARDED])
    outs = [o[0] for o in _adamw(small_sh[None], [g_sh[None]], m_sh[None], v_sh[None], name="adamw_small_sharded")]
    r = 0
    for n in _SMALL_SHARDED:
        rows, width = math.prod(w[n].shape[:-1]), w[n].shape[-1]
        for dst, o in zip((grads, deltas, new_m, new_v), outs):
            dst[n] = o[r:r + rows, :width].reshape(w[n].shape)
        r += rows

    return (loss, grad_x, *[grads[n] for n in _NAMES], *[deltas[n] for n in _NAMES],
            *[new_m[n] for n in _NAMES], *[new_v[n] for n in _NAMES])
```

```python
import functools
import math

import jax
import jax.numpy as jnp
from jax import lax
from jax.experimental import pallas as pl
from jax.experimental.pallas import tpu as pltpu

F32 = jnp.float32
BF16 = jnp.bfloat16
GRAD_WIRE = jnp.bfloat16

NORM_EPS = 1e-6
ADAM_LR = 0.001
ADAM_B1 = 0.9
ADAM_B2 = 0.999
ADAM_EPS = 1e-08
ADAM_WD = 0.01
ADAM_STEP = 10
ADAM_C1 = 1.0 - ADAM_B1 ** ADAM_STEP
ADAM_C2 = 1.0 - ADAM_B2 ** ADAM_STEP

N_DEV = 8
LANES = 128
SUBLANES = 8
SB_DH = 64
ATT_BLOCK = 256
POOL_LOG_WINDOWS = 4
VMEM_LIMIT = 56 * 1024 * 1024


def _params(semantics=None):
    return pltpu.CompilerParams(dimension_semantics=semantics, vmem_limit_bytes=VMEM_LIMIT)


def _tile(dim, pref, unit=LANES):
    if dim <= pref:
        return dim
    t = (pref // unit) * unit
    while t >= unit:
        if dim % t == 0:
            return t
        t -= unit
    return dim


def _matmul(a, b, *, name, ta=False, tb=False, extras=(), rows=(), epilogue=None, out_dtypes=(F32,),
            col_sums=0, tm=1024, tn=1024, tk=2048, b_chunks=False, out_chunks=0):
    M, K = (a.shape[1], a.shape[0]) if ta else a.shape
    if b_chunks:
        C, b_rows, n = b.shape
        N, tn, tk = (b_rows, tn, n) if tb else (C * n, n, tk)
        assert (C * n if tb else b_rows) == K, (a.shape, b.shape)
    else:
        N = b.shape[0] if tb else b.shape[1]
        assert (b.shape[1] if tb else b.shape[0]) == K, (a.shape, b.shape)
    if out_chunks:
        assert not extras and N % out_chunks == 0
        tn = N // out_chunks
    tm, tn, tk = _tile(M, tm), _tile(N, tn), _tile(K, tk)
    nk = K // tk
    dims = (((0 if ta else 1,), (1 if tb else 0,)), ((), ()))
    ne, no = len(extras) + len(rows), len(out_dtypes)
    nfull = no - col_sums
    assert col_sums == 0 or tn == N

    def body(*refs):
        a_ref, b_ref = refs[0], refs[1]
        e_refs = refs[2:2 + ne]
        o_refs = refs[2 + ne:2 + ne + no]
        i = pl.program_id(0)
        k = pl.program_id(2)

        def part():
            return lax.dot_general(a_ref[...].astype(BF16), b_ref[...].astype(BF16), dims,
                                   preferred_element_type=F32)

        def finish(acc):
            outs = epilogue(acc, *[e[...] for e in e_refs]) if epilogue is not None else (acc,)
            for o_ref, val in zip(o_refs[:nfull], outs):
                o_ref[...] = val.astype(o_ref.dtype)
            for o_ref, val in zip(o_refs[nfull:], outs[nfull:]):
                @pl.when(i == 0)
                def _():
                    o_ref[...] = jnp.zeros_like(o_ref)

                o_ref[...] += val

        if nk == 1:
            finish(part())
        else:
            acc_ref = refs[-1]

            @pl.when(k == 0)
            def _():
                acc_ref[...] = jnp.zeros_like(acc_ref)

            acc_ref[...] += part()

            @pl.when(k == nk - 1)
            def _():
                finish(acc_ref[...])

    a_spec = (pl.BlockSpec((tk, tm), lambda i, j, k: (k, i)) if ta
              else pl.BlockSpec((tm, tk), lambda i, j, k: (i, k)))
    if b_chunks:
        b_spec = (pl.BlockSpec((None, tn, tk), lambda i, j, k: (k, j, 0)) if tb
                  else pl.BlockSpec((None, tk, tn), lambda i, j, k: (j, k, 0)))
    else:
        b_spec = (pl.BlockSpec((tn, tk), lambda i, j, k: (j, k)) if tb
                  else pl.BlockSpec((tk, tn), lambda i, j, k: (k, j)))
    if out_chunks:
        o_spec = pl.BlockSpec((None, tm, tn), lambda i, j, k: (j, i, 0))
        o_shape = (out_chunks, M, tn)
    else:
        o_spec = pl.BlockSpec((tm, tn), lambda i, j, k: (i, j))
        o_shape = (M, N)
    r_spec = pl.BlockSpec((1, tn), lambda i, j, k: (0, j))
    outs = pl.pallas_call(
        body,
        name=name,
        grid=(M // tm, N // tn, nk),
        in_specs=[a_spec, b_spec] + [o_spec] * len(extras) + [r_spec] * len(rows),
        out_specs=[o_spec] * nfull + [r_spec] * col_sums,
        out_shape=[jax.ShapeDtypeStruct(o_shape, dt) for dt in out_dtypes[:nfull]]
        + [jax.ShapeDtypeStruct((1, N), dt) for dt in out_dtypes[nfull:]],
        scratch_shapes=[pltpu.VMEM((tm, tn), F32)] if nk > 1 else [],
        compiler_params=_params(("arbitrary" if col_sums else "parallel", "parallel", "arbitrary")),
    )(a, b, *extras, *rows)
    return outs[0] if no == 1 else outs


def _ep_add(acc, res):
    return (acc + res,)


def _ep_add_norm(acc, res, g):
    h = acc + res
    return h, (h * _rstd(h)) * g


def _ep_norm_bwd(acc, h, dres, g):
    r = _rstd(h)
    xh = h * r
    dxh = acc * g
    dh = dres + r * (dxh - xh * jnp.mean(dxh * xh, axis=-1, keepdims=True))
    return dh, dh, jnp.sum(acc * xh, axis=0, keepdims=True)


def _norm_bwd_matmul(dy_in, w, h, g, dres, *, name):
    return _matmul(dy_in, w, tb=True, name=name, extras=(h, dres), rows=(g,), epilogue=_ep_norm_bwd,
                   out_dtypes=(F32, BF16, F32), col_sums=1, tm=512, tn=h.shape[1])


def _ep_relu2(acc):
    r = jnp.maximum(acc, 0.0)
    return (r * r,)


def _ep_relu2_bwd(acc, act):
    return (acc * (2.0 * jnp.sqrt(act.astype(F32))),)


def _rstd(x):
    return lax.rsqrt(jnp.mean(x * x, axis=-1, keepdims=True) + NORM_EPS)


def _rmsnorm(h, g, *, name, tr=512):
    N, D = h.shape
    tr = _tile(N, tr, SUBLANES)

    def body(h_ref, g_ref, o_ref):
        x = h_ref[...]
        o_ref[...] = ((x * _rstd(x)) * g_ref[...]).astype(o_ref.dtype)

    row = pl.BlockSpec((tr, D), lambda i: (i, 0))
    vec = pl.BlockSpec((1, D), lambda i: (0, 0))
    return pl.pallas_call(
        body, name=name, grid=(N // tr,), in_specs=[row, vec], out_specs=row,
        out_shape=jax.ShapeDtypeStruct((N, D), BF16), compiler_params=_params(("parallel",)),
    )(h, g)


def _final_loss(h, g, target, *, name, tr=512):
    N, D = h.shape
    tr = _tile(N, tr, SUBLANES)

    def body(h_ref, g_ref, t_ref, loss_ref, dh_ref, dhb_ref, dg_ref):
        i = pl.program_id(0)
        x = h_ref[...]
        gg = g_ref[...]
        r = _rstd(x)
        xh = x * r
        err = xh * gg - t_ref[...]

        @pl.when(i == 0)
        def _():
            dg_ref[...] = jnp.zeros_like(dg_ref)
            loss_ref[...] = jnp.zeros_like(loss_ref)

        per_row = jnp.mean(err * err, axis=-1, keepdims=True)
        loss_ref[...] += 0.5 * jnp.sum(per_row, axis=0, keepdims=True)
        dy = err * (1.0 / D)
        dg_ref[...] += jnp.sum(dy * xh, axis=0, keepdims=True)
        dxh = dy * gg
        dh = r * (dxh - xh * jnp.mean(dxh * xh, axis=-1, keepdims=True))
        dh_ref[...] = dh
        dhb_ref[...] = dh.astype(dhb_ref.dtype)

    row = pl.BlockSpec((tr, D), lambda i: (i, 0))
    vec = pl.BlockSpec((1, D), lambda i: (0, 0))
    lvec = pl.BlockSpec((1, LANES), lambda i: (0, 0))
    return pl.pallas_call(
        body, name=name, grid=(N // tr,), in_specs=[row, vec, row], out_specs=[lvec, row, row, vec],
        out_shape=[jax.ShapeDtypeStruct((1, LANES), F32), jax.ShapeDtypeStruct((N, D), F32),
                   jax.ShapeDtypeStruct((N, D), BF16), jax.ShapeDtypeStruct((1, D), F32)],
        compiler_params=_params(("arbitrary",)),
    )(h, g, target)


def _shift_down(x, s):
    t = lax.broadcasted_iota(jnp.int32, x.shape, 0)
    return jnp.where(t >= s, pltpu.roll(x, s, 0), 0.0)


def _shift_up(x, s):
    n = x.shape[0]
    t = lax.broadcasted_iota(jnp.int32, x.shape, 0)
    return jnp.where(t < n - s, pltpu.roll(x, n - s, 0), 0.0)


def _window_sum(x, g, shift):
    s = x + shift(x, 1)
    for k in range(1, POOL_LOG_WINDOWS):
        s = jnp.where(k <= g, s + shift(s, 2 ** k), s)
    return s


def _pool_count(shape, g):
    t = lax.broadcasted_iota(jnp.int32, shape, 0)
    return jnp.minimum(t + 1, lax.shift_left(jnp.int32(2), g)).astype(F32)


def _pool_fwd(p, pool_w, pool_scale, B, T, off, width, *, name):
    G, dh = pool_w.shape[0], pool_w.shape[1]
    assert G == POOL_LOG_WINDOWS and off % dh == 0
    base = off // dh

    def body(a_ref, w_ref, s_ref, o_ref):
        g = pl.program_id(0)
        a = a_ref[...]
        pooled = _window_sum(a, g, _shift_down) / _pool_count(a.shape, g) - a
        m = jnp.dot(pooled.astype(BF16), w_ref[0].astype(BF16), preferred_element_type=F32)
        o_ref[...] = (m * s_ref[0]).astype(o_ref.dtype)

    return pl.pallas_call(
        body, name=name, grid=(G, B),
        in_specs=[pl.BlockSpec((T, dh), lambda g, b: (b, base + g)),
                  pl.BlockSpec((1, dh, dh), lambda g, b: (g, 0, 0)),
                  pl.BlockSpec((1, 1, dh), lambda g, b: (g, 0, 0))],
        out_specs=pl.BlockSpec((T, dh), lambda g, b: (b, g)),
        out_shape=jax.ShapeDtypeStruct((B * T, width), BF16),
        compiler_params=_params(("parallel", "parallel")),
    )(p, pool_w, pool_scale.reshape(G, 1, dh))


def _pool_bwd(p, dcat, pool_w, pool_scale, B, T, off, *, name):
    G, dh = pool_w.shape[0], pool_w.shape[1]
    base = off // dh

    def body(a_ref, d_ref, w_ref, s_ref, da_ref, dw_ref, ds_ref):
        g = pl.program_id(0)
        b = pl.program_id(1)
        a = a_ref[...]
        d = d_ref[...]
        cnt = _pool_count(a.shape, g)
        pooled = (_window_sum(a, g, _shift_down) / cnt - a).astype(BF16)
        w = w_ref[0].astype(BF16)
        m = jnp.dot(pooled, w, preferred_element_type=F32)

        @pl.when(b == 0)
        def _():
            dw_ref[...] = jnp.zeros_like(dw_ref)
            ds_ref[...] = jnp.zeros_like(ds_ref)

        ds_ref[0] += jnp.sum(d * m, axis=0, keepdims=True)
        dm = (d * s_ref[0]).astype(BF16)
        dw_ref[0] += lax.dot_general(pooled, dm, (((0,), (0,)), ((), ())), preferred_element_type=F32)
        dpooled = lax.dot_general(dm, w, (((1,), (1,)), ((), ())), preferred_element_type=F32)
        da = _window_sum(dpooled / cnt, g, _shift_up) - dpooled
        da_ref[...] = da.astype(da_ref.dtype)

    pblk = pl.BlockSpec((T, dh), lambda g, b: (b, base + g))
    dblk = pl.BlockSpec((T, dh), lambda g, b: (b, g))
    wspec = pl.BlockSpec((1, dh, dh), lambda g, b: (g, 0, 0))
    sspec = pl.BlockSpec((1, 1, dh), lambda g, b: (g, 0, 0))
    return pl.pallas_call(
        body, name=name, grid=(G, B), in_specs=[pblk, dblk, wspec, sspec], out_specs=[pblk, wspec, sspec],
        out_shape=[jax.ShapeDtypeStruct((B * T, p.shape[1]), BF16), jax.ShapeDtypeStruct((G, dh, dh), F32),
                   jax.ShapeDtypeStruct((G, 1, dh), F32)],
        compiler_params=_params(("parallel", "arbitrary")),
    )(p, dcat, pool_w, pool_scale.reshape(G, 1, dh))


_ANY = pl.BlockSpec(memory_space=pl.ANY)


def _conv_fwd(p, cat, conv_w, conv_b, B, T, coff, *, name):
    CW = conv_w.shape[1]
    tc = LANES
    assert coff % tc == 0 and CW % tc == 0
    cbase = coff // tc

    def body(p_ref, cat_ref, w_ref, b_ref, o_ref):
        xb, gb, gc = p_ref[:, 0:tc], p_ref[:, tc:2 * tc], p_ref[:, 2 * tc:3 * tc]
        c = gc * xb
        w = w_ref[...]
        y = _shift_down(c, 2) * w[0:1] + _shift_down(c, 1) * w[1:2] + c * w[2:3] + b_ref[...]
        o_ref[...] = (gb * y).astype(o_ref.dtype)

    return pl.pallas_call(
        body, name=name, grid=(CW // tc, B),
        in_specs=[pl.BlockSpec((T, 3 * tc), lambda j, b: (b, j)), _ANY,
                  pl.BlockSpec((3, tc), lambda j, b: (0, j)), pl.BlockSpec((1, tc), lambda j, b: (0, j))],
        out_specs=pl.BlockSpec((T, tc), lambda j, b: (b, cbase + j)),
        out_shape=jax.ShapeDtypeStruct(cat.shape, cat.dtype), input_output_aliases={1: 0},
        compiler_params=_params(("parallel", "parallel")),
    )(p, cat, conv_w, conv_b)


def _conv_bwd(p, dcat, dp, conv_w, conv_b, B, T, coff, *, name):
    CW = conv_w.shape[1]
    tc = LANES
    assert coff % tc == 0
    cbase = coff // tc

    def body(p_ref, d_ref, dp_in_ref, w_ref, b_ref, dp_ref, dw_ref, db_ref):
        b = pl.program_id(1)
        xb, gb, gc = p_ref[:, 0:tc], p_ref[:, tc:2 * tc], p_ref[:, 2 * tc:3 * tc]
        d = d_ref[...]
        w = w_ref[...]
        c = gc * xb
        c1 = _shift_down(c, 1)
        c2 = _shift_down(c, 2)
        y = c2 * w[0:1] + c1 * w[1:2] + c * w[2:3] + b_ref[...]
        dy = d * gb
        dp_ref[:, tc:2 * tc] = (d * y).astype(dp_ref.dtype)

        @pl.when(b == 0)
        def _():
            dw_ref[...] = jnp.zeros_like(dw_ref)
            db_ref[...] = jnp.zeros_like(db_ref)

        db_ref[...] += jnp.sum(dy, axis=0, keepdims=True)
        dw_ref[0:1, :] += jnp.sum(dy * c2, axis=0, keepdims=True)
        dw_ref[1:2, :] += jnp.sum(dy * c1, axis=0, keepdims=True)
        dw_ref[2:3, :] += jnp.sum(dy * c, axis=0, keepdims=True)
        dc = dy * w[2:3] + _shift_up(dy, 1) * w[1:2] + _shift_up(dy, 2) * w[0:1]
        dp_ref[:, 2 * tc:3 * tc] = (dc * xb).astype(dp_ref.dtype)
        dp_ref[:, 0:tc] = (dc * gc).astype(dp_ref.dtype)

    wspec = pl.BlockSpec((3, tc), lambda j, b: (0, j))
    bspec = pl.BlockSpec((1, tc), lambda j, b: (0, j))
    pblk = pl.BlockSpec((T, 3 * tc), lambda j, b: (b, j))
    return pl.pallas_call(
        body, name=name, grid=(CW // tc, B),
        in_specs=[pblk, pl.BlockSpec((T, tc), lambda j, b: (b, cbase + j)), _ANY, wspec, bspec],
        out_specs=[pblk, wspec, bspec],
        out_shape=[jax.ShapeDtypeStruct(dp.shape, dp.dtype), jax.ShapeDtypeStruct((3, CW), F32),
                   jax.ShapeDtypeStruct((1, CW), F32)],
        input_output_aliases={2: 0},
        compiler_params=_params(("parallel", "arbitrary")),
    )(p, dcat, dp, conv_w, conv_b)


_SQRT_HALF = 0.7071067811865476
_INV_SQRT_2PI = 0.3989422804014327


def _gelu(x):
    return x * (lax.erf(x * _SQRT_HALF) + 1.0) * 0.5


def _gelu_grad(x):
    return 0.5 * (lax.erf(x * _SQRT_HALF) + 1.0) + x * (_INV_SQRT_2PI * jnp.exp(-0.5 * x * x))


def _layernorm_parts(v):
    mu = jnp.mean(v, axis=-1, keepdims=True)
    vc = v - mu
    rstd = lax.rsqrt(jnp.mean(vc * vc, axis=-1, keepdims=True) + NORM_EPS)
    return vc * rstd, rstd


def _tril_mask(L):
    r = lax.broadcasted_iota(jnp.int32, (L, L), 0)
    c = lax.broadcasted_iota(jnp.int32, (L, L), 1)
    return r >= c


def _sgu_fwd(p, ln_g, ln_b, sgu_w, sgu_b, off, width, *, name, tr=512):
    N = p.shape[0]
    G, L = sgu_w.shape[0], sgu_w.shape[1]
    SW = ln_g.shape[1]
    dh = SW // G
    assert off % SW == 0
    ub = off // SW
    tr = _tile(N, tr, L)
    assert tr % L == 0

    def body(u_ref, v_ref, g_ref, beta_ref, w_ref, b_ref, o_ref):
        u = _gelu(u_ref[...])
        vhat, _ = _layernorm_parts(_gelu(v_ref[...]))
        vn = (vhat * g_ref[...] + beta_ref[...]).astype(BF16)
        mask = _tril_mask(L)
        for gi in range(G):
            w = jnp.where(mask, w_ref[gi], 0.0).astype(BF16)
            bias = b_ref[gi]
            cols = slice(gi * dh, (gi + 1) * dh)
            for n in range(tr // L):
                rows = slice(n * L, (n + 1) * L)
                s = jnp.dot(w, vn[rows, cols], preferred_element_type=F32) + bias
                o_ref[rows, cols] = (u[rows, cols] * s).astype(o_ref.dtype)

    def col(k):
        return pl.BlockSpec((tr, SW), lambda i: (i, k))

    vec = pl.BlockSpec((1, SW), lambda i: (0, 0))
    return pl.pallas_call(
        body, name=name, grid=(N // tr,),
        in_specs=[col(ub), col(ub + 1), vec, vec, pl.BlockSpec((G, L, L), lambda i: (0, 0, 0)),
                  pl.BlockSpec((G, L, 1), lambda i: (0, 0, 0))],
        out_specs=col(0), out_shape=jax.ShapeDtypeStruct((N, width), BF16),
        compiler_params=_params(("parallel",)),
    )(p, p, ln_g, ln_b, sgu_w, sgu_b.reshape(G, L, 1))


def _sgu_bwd(p, dcat, ln_g, ln_b, sgu_w, sgu_b, off, *, name, tr=512):
    N = p.shape[0]
    G, L = sgu_w.shape[0], sgu_w.shape[1]
    SW = ln_g.shape[1]
    dh = SW // G
    assert off % SW == 0
    ub = off // SW
    tr = _tile(N, tr, L)

    def compute(i, u_ref, v_ref, dc_ref, g_ref, beta_ref, w_ref, b_ref,
                du_ref, dv_ref, dw_ref, db_ref, dg_ref, dbeta_ref, du_s, dvn_s):
        pu = u_ref[...]
        pv = v_ref[...]
        u = _gelu(pu)
        vhat, rstd = _layernorm_parts(_gelu(pv))
        gg = g_ref[...]
        vn = (vhat * gg + beta_ref[...]).astype(BF16)
        dc = dc_ref[...]
        mask = _tril_mask(L)

        @pl.when(i == 0)
        def _():
            dw_ref[...] = jnp.zeros_like(dw_ref)
            db_ref[...] = jnp.zeros_like(db_ref)
            dg_ref[...] = jnp.zeros_like(dg_ref)
            dbeta_ref[...] = jnp.zeros_like(dbeta_ref)

        for gi in range(G):
            w = jnp.where(mask, w_ref[gi], 0.0).astype(BF16)
            bias = b_ref[gi]
            cols = slice(gi * dh, (gi + 1) * dh)
            dw_acc = jnp.zeros((L, L), F32)
            db_acc = jnp.zeros((L, 1), F32)
            for n in range(tr // L):
                rows = slice(n * L, (n + 1) * L)
                vb = vn[rows, cols]
                s = jnp.dot(w, vb, preferred_element_type=F32) + bias
                du_s[rows, cols] = dc[rows, cols] * s
                ds = dc[rows, cols] * u[rows, cols]
                db_acc += jnp.sum(ds, axis=1, keepdims=True)
                dsb = ds.astype(BF16)
                dw_acc += lax.dot_general(dsb, vb, (((1,), (1,)), ((), ())), preferred_element_type=F32)
                dvn_s[rows, cols] = lax.dot_general(w, dsb, (((0,), (0,)), ((), ())),
                                                    preferred_element_type=F32)
            dw_ref[gi] += jnp.where(mask, dw_acc, 0.0)
            db_ref[gi] += db_acc

        dvn = dvn_s[...]
        dg_ref[...] += jnp.sum(dvn * vhat, axis=0, keepdims=True)
        dbeta_ref[...] += jnp.sum(dvn, axis=0, keepdims=True)
        dvh = dvn * gg
        dv = rstd * (dvh - jnp.mean(dvh, axis=-1, keepdims=True)
                     - vhat * jnp.mean(dvh * vhat, axis=-1, keepdims=True))
        dv_ref[...] = (dv * _gelu_grad(pv)).astype(dv_ref.dtype)
        du_ref[...] = (du_s[...] * _gelu_grad(pu)).astype(du_ref.dtype)

    def body(u_ref, v_ref, dc_ref, g_ref, beta_ref, w_ref, b_ref,
             dp_ref, dw_ref, db_ref, dg_ref, dbeta_ref, du_s, dvn_s, dv_s):
        i = pl.program_id(0)
        half = pl.program_id(1)

        @pl.when(half == 0)
        def _():
            compute(i, u_ref, v_ref, dc_ref, g_ref, beta_ref, w_ref, b_ref,
                    dp_ref, dv_s, dw_ref, db_ref, dg_ref, dbeta_ref, du_s, dvn_s)

        @pl.when(half == 1)
        def _():
            dp_ref[...] = dv_s[...]

    def col(k):
        return pl.BlockSpec((tr, SW), lambda i, half: (i, k))

    vec = pl.BlockSpec((1, SW), lambda i, half: (0, 0))
    wspec = pl.BlockSpec((G, L, L), lambda i, half: (0, 0, 0))
    bspec = pl.BlockSpec((G, L, 1), lambda i, half: (0, 0, 0))
    return pl.pallas_call(
        body, name=name, grid=(N // tr, 2),
        in_specs=[col(ub), col(ub + 1), col(0), vec, vec, wspec, bspec],
        out_specs=[pl.BlockSpec((tr, SW), lambda i, half: (i, ub + half)), wspec, bspec, vec, vec],
        out_shape=[jax.ShapeDtypeStruct((N, p.shape[1]), BF16),
                   jax.ShapeDtypeStruct((G, L, L), F32), jax.ShapeDtypeStruct((G, L, 1), F32),
                   jax.ShapeDtypeStruct((1, SW), F32), jax.ShapeDtypeStruct((1, SW), F32)],
        scratch_shapes=[pltpu.VMEM((tr, SW), F32), pltpu.VMEM((tr, SW), F32), pltpu.VMEM((tr, SW), BF16)],
        compiler_params=_params(("arbitrary", "arbitrary")),
    )(p, p, dcat, ln_g, ln_b, sgu_w, sgu_b.reshape(G, L, 1))


def _log_sigmoid_pair(z):
    ls = jnp.minimum(z, 0.0) - jnp.log(1.0 + jnp.exp(-jnp.abs(z)))
    return ls, ls - z


def _stacked_cumsum(xs, m):
    his = [x.astype(BF16) for x in xs]
    los = [(x - h.astype(F32)).astype(BF16) for x, h in zip(xs, his)]
    n, rows = len(xs), xs[0].shape[0]
    s = jnp.dot(jnp.concatenate(his + los, axis=0), m, preferred_element_type=F32)
    return [s[i * rows:(i + 1) * rows] + s[(n + i) * rows:(n + i + 1) * rows] for i in range(n)]


def _att_logits(qts, kbs, strict):
    ls, lks = [], []
    for qt, kb in zip(qts, kbs):
        l, lk = _log_sigmoid_pair(lax.dot_general(qt, kb, (((1,), (1,)), ((), ())), preferred_element_type=F32))
        ls.append(l)
        lks.append(lk if strict is None else jnp.where(strict, lk, 0.0))
    return ls, lks


def _att_tiles(TB):
    r = lax.broadcasted_iota(jnp.int32, (TB, TB), 0)
    c = lax.broadcasted_iota(jnp.int32, (TB, TB), 1)
    return r, c


HEADS_PER_BLOCK = LANES // SB_DH
ATT_GROUPS = 2
HP = ATT_GROUPS * HEADS_PER_BLOCK
ATT_SCALE = 1.0 / math.sqrt(SB_DH)


def _head_col(hh, part):
    return (hh // HEADS_PER_BLOCK) * 3 * LANES + part * LANES + (hh % HEADS_PER_BLOCK) * SB_DH


def _stage_heads(src_ref, part, dst_ref, T, scale=None):
    rows = _tile(T, 256, SUBLANES)

    def chunk(n, _):
        r0 = pl.multiple_of(n * rows, rows)
        for hh in range(HP):
            col = hh * SB_DH if part is None else _head_col(hh, part)
            x = src_ref[pl.ds(r0, rows), col:col + SB_DH]
            if scale is not None:
                x = x * scale
            dst_ref[hh, pl.ds(r0, rows), :] = x.astype(dst_ref.dtype)
        return 0

    lax.fori_loop(0, T // rows, chunk, 0)


def _attn_fwd(p, cat, B, T, coff, *, name):
    dh = SB_DH
    owidth = ATT_GROUPS * LANES
    nsteps = (cat.shape[1] - coff) // owidth
    TB = _tile(T, ATT_BLOCK)
    nb = T // TB
    assert nb <= LANES and coff % owidth == 0 and (cat.shape[1] - coff) % owidth == 0
    cbase = coff // owidth

    def body(p_ref, cat_ref, o_ref, c_ref, q_ref, k_ref, v_ref):
        _stage_heads(p_ref, 0, q_ref, T, ATT_SCALE)
        _stage_heads(p_ref, 1, k_ref, T)
        _stage_heads(p_ref, 2, v_ref, T)
        r, c = _att_tiles(TB)
        strict = c < r
        later = (r > c).astype(BF16)
        lane = lax.broadcasted_iota(jnp.int32, (TB, LANES), 1)

        def tiles(qts, j, carries, mask):
            k0 = pl.multiple_of(j * TB, TB)
            ls, lks = _att_logits(qts, [k_ref[hh, pl.ds(k0, TB), :] for hh in range(HP)], mask)
            sums = _stacked_cumsum(lks, later)
            out = []
            for hh in range(HP):
                suffix = sums[hh] + carries[hh]
                a = jnp.exp(ls[hh] + suffix)
                if mask is not None:
                    a = jnp.where(mask, a, 0.0)
                pv = jnp.dot(a.astype(BF16), v_ref[hh, pl.ds(k0, TB), :], preferred_element_type=F32)
                out.append((pv, suffix[:, 0:1] + lks[hh][:, 0:1]))
            return out

        def qblock(i, _):
            q0 = pl.multiple_of(i * TB, TB)
            qts = [q_ref[hh, pl.ds(q0, TB), :] for hh in range(HP)]
            state = []
            for pv, carry in tiles(qts, i, [jnp.zeros((TB, 1), F32)] * HP, strict):
                state += [pv, carry, jnp.zeros((TB, LANES), F32)]

            def kblock(jj, st):
                j = i - jj
                out = []
                for hh, (pv, new_carry) in enumerate(tiles(qts, j, st[1::3], None)):
                    acc, carry, cm = st[3 * hh:3 * hh + 3]
                    out += [acc + pv, new_carry, jnp.where(lane == j, carry, cm)]
                return tuple(out)

            st = lax.fori_loop(1, i + 1, kblock, tuple(state))
            for hh in range(HP):
                o_ref[pl.ds(q0, TB), hh * dh:(hh + 1) * dh] = st[3 * hh].astype(o_ref.dtype)
                c_ref[hh, pl.ds(q0, TB), :] = st[3 * hh + 2]
            return 0

        lax.fori_loop(0, nb, qblock, 0)

    staged = pltpu.VMEM((HP, T, dh), BF16)
    return pl.pallas_call(
        body, name=name, grid=(B, nsteps),
        in_specs=[pl.BlockSpec((T, 3 * owidth), lambda b, s: (b, s)), _ANY],
        out_specs=[pl.BlockSpec((T, owidth), lambda b, s: (b, cbase + s)),
                   pl.BlockSpec((HP, T, LANES), lambda b, s: (b * nsteps + s, 0, 0))],
        out_shape=[jax.ShapeDtypeStruct(cat.shape, cat.dtype),
                   jax.ShapeDtypeStruct((B * nsteps * HP, T, LANES), F32)],
        input_output_aliases={1: 0}, scratch_shapes=[staged, staged, staged],
        compiler_params=_params(("parallel", "parallel")),
    )(p, cat)


def _attn_bwd(p, dcat, carries, dp, B, T, coff, *, name):
    dh = SB_DH
    owidth = ATT_GROUPS * LANES
    nsteps = (dcat.shape[1] - coff) // owidth
    TB = _tile(T, ATT_BLOCK)
    nb = T // TB
    cbase = coff // owidth

    def body(p_ref, d_ref, c_ref, dp_in_ref, dp_ref, q_ref, k_ref, v_ref, do_ref, dk_ref, dv_ref):
        _stage_heads(p_ref, 0, q_ref, T, ATT_SCALE)
        _stage_heads(p_ref, 1, k_ref, T)
        _stage_heads(p_ref, 2, v_ref, T)
        _stage_heads(d_ref, None, do_ref, T)
        r, c = _att_tiles(TB)
        strict = c < r
        later = (r > c).astype(BF16)
        earlier = (r < c).astype(BF16)
        lane = lax.broadcasted_iota(jnp.int32, (TB, LANES), 1)
        dk_ref[...] = jnp.zeros_like(dk_ref)
        dv_ref[...] = jnp.zeros_like(dv_ref)

        def tiles(qts, dots, cms, j, befores, mask):
            k0 = pl.multiple_of(j * TB, TB)
            kbs = [k_ref[hh, pl.ds(k0, TB), :] for hh in range(HP)]
            ls, lks = _att_logits(qts, kbs, mask)
            sums = _stacked_cumsum(lks, later)
            gls = []
            for hh in range(HP):
                carry = jnp.sum(jnp.where(lane == j, cms[hh], 0.0), axis=1, keepdims=True)
                a = jnp.exp(ls[hh] + sums[hh] + carry)
                if mask is not None:
                    a = jnp.where(mask, a, 0.0)
                dv_ref[hh, pl.ds(k0, TB), :] += lax.dot_general(a.astype(BF16), dots[hh], (((0,), (0,)), ((), ())),
                                                                preferred_element_type=F32)
                da = lax.dot_general(dots[hh], v_ref[hh, pl.ds(k0, TB), :], (((1,), (1,)), ((), ())),
                                     preferred_element_type=F32)
                gls.append(a * da)
            pres = _stacked_cumsum(gls, earlier)
            out = []
            for hh in range(HP):
                prefix = pres[hh] + befores[hh]
                dz = gls[hh] - jnp.exp(ls[hh]) * (gls[hh] + prefix)
                if mask is not None:
                    dz = jnp.where(mask, dz, 0.0)
                dzb = dz.astype(BF16)
                dk_ref[hh, pl.ds(k0, TB), :] += lax.dot_general(dzb, qts[hh], (((0,), (0,)), ((), ())),
                                                                preferred_element_type=F32)
                out.append((jnp.dot(dzb, kbs[hh], preferred_element_type=F32),
                            prefix[:, TB - 1:TB] + gls[hh][:, TB - 1:TB]))
            return out

        def qblock(i, _):
            q0 = pl.multiple_of(i * TB, TB)
            qts = [q_ref[hh, pl.ds(q0, TB), :] for hh in range(HP)]
            dots = [do_ref[hh, pl.ds(q0, TB), :] for hh in range(HP)]
            cms = [c_ref[hh, pl.ds(q0, TB), :] for hh in range(HP)]

            def kblock(j, st):
                out = []
                for hh, (part, new_before) in enumerate(tiles(qts, dots, cms, j, st[1::2], None)):
                    out += [st[2 * hh] + part, new_before]
                return tuple(out)

            st = lax.fori_loop(0, i, kblock, (jnp.zeros((TB, dh), F32), jnp.zeros((TB, 1), F32)) * HP)
            for hh, (part, _) in enumerate(tiles(qts, dots, cms, i, st[1::2], strict)):
                dq = (st[2 * hh] + part) * ATT_SCALE
                dp_ref[pl.ds(q0, TB), _head_col(hh, 0):_head_col(hh, 0) + dh] = dq.astype(dp_ref.dtype)
            return 0

        lax.fori_loop(0, nb, qblock, 0)

        def write_back(n, _):
            r0 = pl.multiple_of(n * TB, TB)
            for hh in range(HP):
                dp_ref[pl.ds(r0, TB), _head_col(hh, 1):_head_col(hh, 1) + dh] = (
                    dk_ref[hh, pl.ds(r0, TB), :].astype(dp_ref.dtype))
                dp_ref[pl.ds(r0, TB), _head_col(hh, 2):_head_col(hh, 2) + dh] = (
                    dv_ref[hh, pl.ds(r0, TB), :].astype(dp_ref.dtype))
            return 0

        lax.fori_loop(0, nb, write_back, 0)

    pblk = pl.BlockSpec((T, 3 * owidth), lambda b, s: (b, s))
    staged = pltpu.VMEM((HP, T, dh), BF16)
    accum = pltpu.VMEM((HP, T, dh), F32)
    return pl.pallas_call(
        body, name=name, grid=(B, nsteps),
        in_specs=[pblk, pl.BlockSpec((T, owidth), lambda b, s: (b, cbase + s)),
                  pl.BlockSpec((HP, T, LANES), lambda b, s: (b * nsteps + s, 0, 0)), _ANY],
        out_specs=pblk, out_shape=jax.ShapeDtypeStruct(dp.shape, dp.dtype), input_output_aliases={3: 0},
        scratch_shapes=[staged, staged, staged, staged, accum, accum],
        compiler_params=_params(("parallel", "parallel")),
    )(p, dcat, carries, dp)


def _adamw(w, gparts, m, v, *, name):
    L, R, C = w.shape
    P = gparts[0].shape[0]
    assert len(gparts) == L
    tr = _tile(R, max(SUBLANES, (1 << 21) // (C * P)), SUBLANES)

    def body(*refs):
        w_ref, g_refs, (m_ref, v_ref) = refs[0], refs[1:1 + L], refs[1 + L:3 + L]
        go_ref, d_ref, mo_ref, vo_ref = refs[3 + L:]
        layer = pl.program_id(0)

        def update(g_ref):
            g = g_ref[0].astype(F32)
            for i in range(1, P):
                g = g + g_ref[i].astype(F32)
            m2 = ADAM_B1 * m_ref[...] + (1.0 - ADAM_B1) * g
            v2 = ADAM_B2 * v_ref[...] + (1.0 - ADAM_B2) * (g * g)
            m_hat = m2 / ADAM_C1
            v_hat = v2 / ADAM_C2
            go_ref[...] = g
            d_ref[...] = -ADAM_LR * (m_hat / (jnp.sqrt(v_hat) + ADAM_EPS) + ADAM_WD * w_ref[...])
            mo_ref[...] = m2
            vo_ref[...] = v2

        for l in range(L):
            pl.when(layer == l)(functools.partial(update, g_refs[l]))

    row = pl.BlockSpec((None, tr, C), lambda l, i: (l, i, 0))

    def part(mine):
        return pl.BlockSpec((P, tr, C), lambda l, i: (0, jnp.where(l == mine, i, 0), 0))

    shp = jax.ShapeDtypeStruct((L, R, C), F32)
    return pl.pallas_call(
        body, name=name, grid=(L, R // tr),
        in_specs=[row] + [part(l) for l in range(L)] + [row, row],
        out_specs=[row] * 4, out_shape=[shp] * 4, compiler_params=_params(("arbitrary", "arbitrary")),
    )(w, *gparts, m, v)


def _my_index():
    return 4 * lax.axis_index("x") + 2 * lax.axis_index("y") + lax.axis_index("c")


def _exchange(arrs, gather, *, name):
    n = len(arrs)

    def body(*refs):
        ins, outs = refs[:n], refs[n:2 * n]
        send_sems, recv_sems, local_sems = refs[2 * n:]
        x, y, c = lax.axis_index("x"), lax.axis_index("y"), lax.axis_index("c")
        me = 4 * x + 2 * y + c
        remote, local = [], []
        for a in range(n):
            own = ins[a] if gather[a] else ins[a].at[me]
            cp = pltpu.make_async_copy(own, outs[a].at[me], local_sems.at[a])
            cp.start()
            local.append(cp)
            for k in range(1, N_DEV):
                px = 1 - x if k & 4 else x
                py = 1 - y if k & 2 else y
                pc = 1 - c if k & 1 else c
                src = ins[a] if gather[a] else ins[a].at[4 * px + 2 * py + pc]
                cp = pltpu.make_async_remote_copy(
                    src_ref=src, dst_ref=outs[a].at[me],
                    send_sem=send_sems.at[a, k - 1], recv_sem=recv_sems.at[a, k - 1],
                    device_id=(px, py, pc), device_id_type=pl.DeviceIdType.MESH)
                cp.start()
                remote.append(cp)
        for cp in remote:
            cp.wait()
        for cp in local:
            cp.wait()

    hbm = pl.BlockSpec(memory_space=pltpu.HBM)
    out_shape = [jax.ShapeDtypeStruct(((N_DEV,) + a.shape) if g else a.shape, a.dtype)
                 for a, g in zip(arrs, gather)]
    return pl.pallas_call(
        body, name=name, in_specs=[hbm] * n, out_specs=[hbm] * n, out_shape=out_shape,
        scratch_shapes=[pltpu.SemaphoreType.DMA((n, N_DEV - 1)), pltpu.SemaphoreType.DMA((n, N_DEV - 1)),
                        pltpu.SemaphoreType.DMA((n,))],
    )(*arrs)


_HBM = pl.BlockSpec(memory_space=pltpu.HBM)


def _other_chips(x, y):
    return [(1 - x, y), (x, 1 - y), (1 - x, 1 - y)]


def _gather_two_level(arrs, *, name):
    n = len(arrs)

    def body(*refs):
        ins, outs = refs[:n], refs[n:2 * n]
        send_sems, recv_sems, local_sems = refs[2 * n:]
        x, y, c = lax.axis_index("x"), lax.axis_index("y"), lax.axis_index("c")
        me, sibling = (x, y, c), (x, y, 1 - c)
        chips = _other_chips(x, y)

        def slot(a, px, py, pc):
            return outs[a].at[4 * px + 2 * py + pc]

        def copy(a, k, block, to, src=None):
            return pltpu.make_async_remote_copy(
                src_ref=slot(a, *block) if src is None else src, dst_ref=slot(a, *block),
                send_sem=send_sems.at[a, k], recv_sem=recv_sems.at[a, k],
                device_id=to, device_id_type=pl.DeviceIdType.MESH)

        local, sends = [], []
        for a in range(n):
            cp = pltpu.make_async_copy(ins[a], slot(a, *me), local_sems.at[a])
            cp.start()
            local.append(cp)
            first = [copy(a, 0, me, sibling, src=ins[a])]
            first += [copy(a, 1 + j, me, (*chip, c), src=ins[a]) for j, chip in enumerate(chips)]
            for cp in first:
                cp.start()
            sends += first
        for j, chip in enumerate(chips):
            for a in range(n):
                copy(a, 1 + j, (*chip, c), me).wait_recv()
                cp = copy(a, 4 + j, (*chip, c), sibling)
                cp.start()
                sends.append(cp)
        for a in range(n):
            copy(a, 0, sibling, me).wait_recv()
            for j, chip in enumerate(chips):
                copy(a, 4 + j, (*chip, 1 - c), me).wait_recv()
        for cp in sends:
            cp.wait_send()
        for cp in local:
            cp.wait()

    return pl.pallas_call(
        body, name=name, in_specs=[_HBM] * n, out_specs=[_HBM] * n,
        out_shape=[jax.ShapeDtypeStruct((N_DEV,) + a.shape, a.dtype) for a in arrs],
        scratch_shapes=[pltpu.SemaphoreType.DMA((n, N_DEV - 1)), pltpu.SemaphoreType.DMA((n, N_DEV - 1)),
                        pltpu.SemaphoreType.DMA((n,))],
    )(*arrs)


_SEM = pl.BlockSpec(memory_space=pltpu.SEMAPHORE)
_SPLIT_COPY = pltpu.SideEffectType.DATAFLOW_SIDE_EFFECTING


def _peers(x, y, c):
    return [((1 - x if k & 4 else x), (1 - y if k & 2 else y), (1 - c if k & 1 else c)) for k in range(1, N_DEV)]


_SPLIT_SEMS = 2 * (N_DEV - 1) + 1


def _split_sems(sems, a):
    mine = sems[a * _SPLIT_SEMS:(a + 1) * _SPLIT_SEMS]
    return mine[:N_DEV - 1], mine[N_DEV - 1:2 * (N_DEV - 1)], mine[-1]


def _split_src(ref, scatter, index):
    return ref.at[index] if scatter else ref


def _gather_start(arrs, carry, *, name, scatter=False):
    n = len(arrs)
    ns = n * _SPLIT_SEMS

    def body(*refs):
        ins, lands = refs[:n], refs[n:2 * n]
        sems = refs[2 * n + 1:2 * n + 1 + ns]
        x, y, c = lax.axis_index("x"), lax.axis_index("y"), lax.axis_index("c")
        me = 4 * x + 2 * y + c
        for a in range(n):
            send, recv, local = _split_sems(sems, a)
            pltpu.make_async_copy(_split_src(ins[a], scatter, me), lands[a].at[me], local).start()
            for k, (px, py, pc) in enumerate(_peers(x, y, c)):
                pltpu.make_async_remote_copy(
                    src_ref=_split_src(ins[a], scatter, 4 * px + 2 * py + pc), dst_ref=lands[a].at[me],
                    send_sem=send[k], recv_sem=recv[k],
                    device_id=(px, py, pc), device_id_type=pl.DeviceIdType.MESH).start()

    lands = [lax.empty(a.shape if scatter else (N_DEV,) + a.shape, a.dtype) for a in arrs]
    operands = [pltpu.with_memory_space_constraint(a, pltpu.HBM) for a in list(arrs) + lands + [carry]]
    outs = pl.pallas_call(
        body, name=name, in_specs=[_HBM] * (2 * n + 1), out_specs=[_SEM] * ns + [_HBM] * (2 * n + 1),
        out_shape=[pltpu.SemaphoreType.DMA(())] * ns + [pltpu.HBM(a.shape, a.dtype) for a in operands],
        input_output_aliases={i: ns + i for i in range(2 * n + 1)},
        compiler_params=pltpu.CompilerParams(has_side_effects=_SPLIT_COPY),
    )(*operands)
    return tuple(outs[:-1]), outs[-1]


def _gather_wait(handles, after, *, name, scatter=False):
    n = len(handles) // (_SPLIT_SEMS + 2)
    ns = n * _SPLIT_SEMS
    sems, thru = handles[:ns], handles[ns:]

    def body(*refs):
        ins, lands = refs[:n], refs[n:2 * n]
        sems = refs[2 * n:2 * n + ns]
        x, y, c = lax.axis_index("x"), lax.axis_index("y"), lax.axis_index("c")
        me = 4 * x + 2 * y + c
        for a in range(n):
            send, recv, local = _split_sems(sems, a)
            src = _split_src(ins[a], scatter, me)
            pltpu.make_async_copy(src, lands[a].at[me], local).wait()
            for k, peer in enumerate(_peers(x, y, c)):
                cp = pltpu.make_async_remote_copy(
                    src_ref=src, dst_ref=lands[a].at[me], send_sem=send[k], recv_sem=recv[k],
                    device_id=peer, device_id_type=pl.DeviceIdType.MESH)
                cp.wait_send()
                cp.wait_recv()

    outs = pl.pallas_call(
        body, name=name, in_specs=[_HBM] * (2 * n) + [_SEM] * ns + [_ANY], out_specs=[_HBM] * (2 * n),
        out_shape=[pltpu.HBM(a.shape, a.dtype) for a in thru],
        input_output_aliases={i: i for i in range(2 * n)},
        compiler_params=pltpu.CompilerParams(has_side_effects=_SPLIT_COPY),
    )(*thru, *sems, after)
    return outs[n:]


def _group_in_cols(w, lead):
    X = (w.shape[-1] - lead) // 3
    g = w[..., lead:].reshape(w.shape[:-1] + (3, X // LANES, LANES))
    g = jnp.swapaxes(g, -3, -2).reshape(w.shape[:-1] + (3 * X,))
    return jnp.concatenate([g, w[..., :lead]], axis=-1)


def _ungroup_in_cols(w, lead):
    X = (w.shape[-1] - lead) // 3
    g = w[..., :3 * X].reshape(w.shape[:-1] + (X // LANES, 3, LANES))
    g = jnp.swapaxes(g, -3, -2).reshape(w.shape[:-1] + (3 * X,))
    return jnp.concatenate([w[..., 3 * X:], g], axis=-1)


def _residual_matmul(a, w, res, gain, *, name):
    if gain is None:
        return _matmul(a, w, name=name, extras=(res,), epilogue=_ep_add), None
    return _matmul(a, w, name=name, extras=(res,), rows=(gain,), epilogue=_ep_add_norm, out_dtypes=(F32, BF16),
                   tn=res.shape[1])


def _mlp_fwd(h, hn, W, tag, next_gain=None):
    w1 = _weight(W, f"mlp_w1_{tag}", hn)
    act = _matmul(hn, w1, name=f"mlp_up_{tag}", epilogue=_ep_relu2, out_dtypes=(BF16,), b_chunks=True, tm=2048)
    out, out_n = _residual_matmul(act, _weight(W, f"mlp_w2_{tag}", act), h, next_gain, name=f"mlp_down_{tag}")
    return out, out_n, (hn, act, _chunks_to_cols(w1))


def _mlp_bwd(dout, dout_b, h, g, w2, saved, tag, sent):
    hn, act, w1 = saved
    dw2 = _matmul(act, dout_b, ta=True, name=f"mlp_dw2_{tag}", out_dtypes=(GRAD_WIRE,))
    dout_b = sent(dout_b, {f"mlp_w2_{tag}": dw2})
    dz = _matmul(dout_b, w2, tb=True, name=f"mlp_dact_{tag}", extras=(act,), epilogue=_ep_relu2_bwd,
                 out_dtypes=(BF16,))
    dw1 = _matmul(hn, dz, ta=True, name=f"mlp_dw1_{tag}", out_dtypes=(GRAD_WIRE,), out_chunks=N_DEV, tk=4096)
    dz = sent(dz, {f"mlp_w1_{tag}": dw1})
    dh, dh_b, dg = _norm_bwd_matmul(dz, w1, h, g, dout, name=f"mlp_dhn_{tag}")
    return dh, dh_b, dg, dw1, dw2


class _Lazy:
    def __init__(self, handles, finish, name):
        self.handles, self.finish, self.name, self.done = handles, finish, name, None

    def take(self, after):
        if self.done is None:
            self.done = self.finish(_gather_wait(self.handles, after, name=self.name))
        return self.done


def _weight(W, n, after):
    if isinstance(W[n], _Lazy):
        W.update(W[n].take(after))
    return W[n]


def _local_step(x, target, W, emit=None):
    B, T, D = x.shape
    N = B * T
    G = {}
    row = lambda vec: vec.reshape(1, -1)

    def sent(nxt, grads):
        return nxt if emit is None else emit(grads, nxt)

    PW = W["pool_w"].shape[0] * W["pool_w"].shape[1]
    CW = W["conv_b"].shape[-1]
    SW = W["sgu_norm_g"].shape[-1]
    HW = W["att_width"]

    h0 = x.reshape(N, D)
    xn0 = _rmsnorm(h0, row(W["mix_norm_g"][0]), name="mix_norm_0")
    p0 = _matmul(xn0, W["ab_w_in"], name="ab_in")
    cat0 = _pool_fwd(p0, W["pool_w"], W["pool_scale"], B, T, 3 * CW, PW + CW, name="pool_fwd")
    cat0 = _conv_fwd(p0, cat0, W["conv_w"], row(W["conv_b"]), B, T, PW, name="conv_fwd")
    h1, hn0 = _residual_matmul(cat0, W["ab_w_out"], h0, row(W["mlp_norm_g"][0]), name="ab_out")
    h2, xn1, mlp0 = _mlp_fwd(h1, hn0, W, 0, row(W["mix_norm_g"][1]))
    p1 = _matmul(xn1, _weight(W, "cd_w_in", xn1), name="cd_in")
    ln_g, ln_b = row(W["sgu_norm_g"]), row(W["sgu_norm_b"])
    cat1 = _sgu_fwd(p1, ln_g, ln_b, W["sgu_w"], W["sgu_b"], 3 * HW, SW + HW, name="sgu_fwd")
    cat1, att_carries = _attn_fwd(p1, cat1, B, T, SW, name="attn_fwd")
    h3, hn1 = _residual_matmul(cat1, W["cd_w_out"], h2, row(W["mlp_norm_g"][1]), name="cd_out")
    h4, _, mlp1 = _mlp_fwd(h3, hn1, W, 1)

    loss, dh4, dh4_b, G["final_norm_g"] = _final_loss(h4, row(W["final_norm_g"]), target.reshape(N, D),
                                                      name="final_loss")

    dh3, dh3_b, dmlp_g1, dw1_1, dw2_1 = _mlp_bwd(dh4, dh4_b, h3, row(W["mlp_norm_g"][1]), W["mlp_w2_1"], mlp1, 1,
                                                 sent)
    G["cd_w_out"] = _matmul(cat1, dh3_b, ta=True, name="cd_out_dw", out_dtypes=(GRAD_WIRE,))[None]
    dh3_b = sent(dh3_b, {"cd_w_out": G["cd_w_out"][0]})
    dcat1 = _matmul(dh3_b, W["cd_w_out"], tb=True, name="cd_out_dx")
    dp1, G["sgu_w"], dsgu_b, G["sgu_norm_g"], G["sgu_norm_b"] = _sgu_bwd(
        p1, dcat1, ln_g, ln_b, W["sgu_w"], W["sgu_b"], 3 * HW, name="sgu_bwd")
    G["sgu_b"] = dsgu_b.reshape(W["sgu_b"].shape)
    dp1 = sent(dp1, {"sgu_w": G["sgu_w"]})
    dp1 = _attn_bwd(p1, dcat1, att_carries, dp1, B, T, SW, name="attn_bwd")
    G["cd_w_in"] = _matmul(xn1, dp1, ta=True, name="cd_in_dw", out_dtypes=(GRAD_WIRE,))[None]
    dp1 = sent(dp1, {"cd_w_in": G["cd_w_in"][0]})
    dh2, dh2_b, dmix_g1 = _norm_bwd_matmul(dp1, W["cd_w_in"], h2, row(W["mix_norm_g"][1]), dh3, name="cd_in_dx")

    dh1, dh1_b, dmlp_g0, dw1_0, dw2_0 = _mlp_bwd(dh2, dh2_b, h1, row(W["mlp_norm_g"][0]), W["mlp_w2_0"], mlp0, 0,
                                                 sent)
    G["ab_w_out"] = _matmul(cat0, dh1_b, ta=True, name="ab_out_dw", out_dtypes=(GRAD_WIRE,))[None]
    dh1_b = sent(dh1_b, {"ab_w_out": G["ab_w_out"][0]})
    dcat0 = _matmul(dh1_b, W["ab_w_out"], tb=True, name="ab_out_dx")
    dp0, G["pool_w"], dps = _pool_bwd(p0, dcat0, W["pool_w"], W["pool_scale"], B, T, 3 * CW, name="pool_bwd")
    dp0 = sent(dp0, {"pool_w": G["pool_w"]})
    G["pool_scale"] = dps.reshape(W["pool_scale"].shape)
    dp0, G["conv_w"], dcb = _conv_bwd(p0, dcat0, dp0, W["conv_w"], row(W["conv_b"]), B, T, PW, name="conv_bwd")
    G["conv_b"] = dcb.reshape(-1)
    G["ab_w_in"] = _matmul(xn0, dp0, ta=True, name="ab_in_dw", out_dtypes=(GRAD_WIRE,))[None]
    dp0 = sent(dp0, {"ab_w_in": G["ab_w_in"][0]})
    dx, _, dmix_g0 = _norm_bwd_matmul(dp0, W["ab_w_in"], h0, row(W["mix_norm_g"][0]), dh1, name="ab_in_dx")

    G["mix_norm_g"] = jnp.concatenate([dmix_g0, dmix_g1], axis=0)
    G["mlp_norm_g"] = jnp.concatenate([dmlp_g0, dmlp_g1], axis=0)
    G["mlp_w1"] = jnp.stack([_chunks_to_cols(dw1_0), _chunks_to_cols(dw1_1)])
    G["mlp_w2"] = jnp.stack([dw2_0, dw2_1])
    G["final_norm_g"] = G["final_norm_g"].reshape(-1)
    G["sgu_norm_g"] = G["sgu_norm_g"].reshape(-1)
    G["sgu_norm_b"] = G["sgu_norm_b"].reshape(-1)
    return loss[0, 0], dx.reshape(B, T, D), G


_NAMES = ["mix_norm_g", "mlp_norm_g", "ab_w_in", "pool_w", "pool_scale", "conv_w", "conv_b", "ab_w_out",
          "cd_w_in", "sgu_norm_g", "sgu_norm_b", "sgu_w", "sgu_b", "cd_w_out", "mlp_w1", "mlp_w2",
          "final_norm_g"]
_COL_SHARDED = ["ab_w_in", "cd_w_in", "mlp_w1"]
_ROW_SHARDED = ["ab_w_out", "cd_w_out", "mlp_w2"]
_SMALL_SHARDED = ["conv_w", "sgu_norm_g", "sgu_norm_b"]
_REPLICATED = ["sgu_w", "pool_w", "mix_norm_g", "mlp_norm_g", "pool_scale", "conv_b", "sgu_b", "final_norm_g"]
_EARLY_SMALL = _REPLICATED[:2]


def _pad_rows(a2d, mult=SUBLANES):
    pad = (-a2d.shape[0]) % mult
    return jnp.pad(a2d, ((0, pad), (0, 0))) if pad else a2d


def _pack(arrays):
    return _pad_rows(jnp.concatenate([a.reshape(-1, LANES) for a in arrays], axis=0))


def _unpack(packed, shapes):
    out, r = [], 0
    for s in shapes:
        n = math.prod(s) // LANES
        out.append(packed[r:r + n].reshape(s))
        r += n
    return out


def _small_shard_pack(arrays):
    rows = [jnp.pad(a.reshape(-1, a.shape[-1]), ((0, 0), (0, LANES - a.shape[-1]))) for a in arrays]
    return _pad_rows(jnp.concatenate(rows, axis=0))


def _cols_to_chunks(a):
    n = a.shape[-1] // N_DEV
    return jnp.moveaxis(a.reshape(a.shape[:-1] + (N_DEV, n)), -2, 0)


def _chunks_to_cols(a):
    t = jnp.moveaxis(a, 0, -2)
    return t.reshape(t.shape[:-2] + (t.shape[-2] * t.shape[-1],))


def _rows_to_chunks(a):
    r = a.shape[-2] // N_DEV
    return jnp.moveaxis(a.reshape(a.shape[:-2] + (N_DEV, r, a.shape[-1])), -3, 0)


def _chunks_to_rows(a):
    t = jnp.moveaxis(a, 0, -3)
    return t.reshape(t.shape[:-3] + (t.shape[-3] * t.shape[-2], t.shape[-1]))


def kernel(x, mix_norm_g, mlp_norm_g, ab_w_in, pool_w, pool_scale, conv_w, conv_b, ab_w_out, cd_w_in, sgu_norm_g, sgu_norm_b, sgu_w, sgu_b, cd_w_out, mlp_w1, mlp_w2, final_norm_g, loss_target, m_mix_norm_g, m_mlp_norm_g, m_ab_w_in, m_pool_w, m_pool_scale, m_conv_w, m_conv_b, m_ab_w_out, m_cd_w_in, m_sgu_norm_g, m_sgu_norm_b, m_sgu_w, m_sgu_b, m_cd_w_out, m_mlp_w1, m_mlp_w2, m_final_norm_g, v_mix_norm_g, v_mlp_norm_g, v_ab_w_in, v_pool_w, v_pool_scale, v_conv_w, v_conv_b, v_ab_w_out, v_cd_w_in, v_sgu_norm_g, v_sgu_norm_b, v_sgu_w, v_sgu_b, v_cd_w_out, v_mlp_w1, v_mlp_w2, v_final_norm_g):
    w = dict(zip(_NAMES, (mix_norm_g, mlp_norm_g, ab_w_in, pool_w, pool_scale, conv_w, conv_b, ab_w_out, cd_w_in,
                          sgu_norm_g, sgu_norm_b, sgu_w, sgu_b, cd_w_out, mlp_w1, mlp_w2, final_norm_g)))
    m = dict(zip(_NAMES, (m_mix_norm_g, m_mlp_norm_g, m_ab_w_in, m_pool_w, m_pool_scale, m_conv_w, m_conv_b,
                          m_ab_w_out, m_cd_w_in, m_sgu_norm_g, m_sgu_norm_b, m_sgu_w, m_sgu_b, m_cd_w_out,
                          m_mlp_w1, m_mlp_w2, m_final_norm_g)))
    v = dict(zip(_NAMES, (v_mix_norm_g, v_mlp_norm_g, v_ab_w_in, v_pool_w, v_pool_scale, v_conv_w, v_conv_b,
                          v_ab_w_out, v_cd_w_in, v_sgu_norm_g, v_sgu_norm_b, v_sgu_w, v_sgu_b, v_cd_w_out,
                          v_mlp_w1, v_mlp_w2, v_final_norm_g)))
    big = _COL_SHARDED + _ROW_SHARDED
    me = _my_index()

    small_sh = _small_shard_pack([w[n] for n in _SMALL_SHARDED])
    in_lead = {"ab_w_in": pool_w.shape[1] * pool_w.shape[2], "cd_w_in": 2 * sgu_norm_g.shape[-1] * N_DEV}
    shard = {"ab_w_in": ab_w_in[0], "ab_w_out": ab_w_out[0], "cd_w_in": cd_w_in[0], "cd_w_out": cd_w_out[0]}
    for layer in range(mlp_w1.shape[0]):
        shard[f"mlp_w1_{layer}"], shard[f"mlp_w2_{layer}"] = mlp_w1[layer], mlp_w2[layer]
    shard = {n: a.astype(BF16) for n, a in shard.items()}

    def whole(n, g):
        if n.startswith("mlp_w1"):
            return g
        if n in in_lead:
            return _group_in_cols(_chunks_to_cols(g), in_lead[n])
        return _chunks_to_rows(g)

    W = {"att_width": cd_w_out.shape[1] * N_DEV - sgu_norm_g.shape[-1] * N_DEV}
    later = [(["mlp_w2_0"], "mlp0_down"), (["cd_w_in", "cd_w_out"], "cd"), (["mlp_w1_1", "mlp_w2_1"], "mlp1")]

    def start(idx, carry, then=None):
        group, tag = later[idx]
        handles, carry = _gather_start([shard[n] for n in group], carry, name=f"gather_{tag}_start")

        def finish(got):
            done = {n: whole(n, g) for n, g in zip(group, got)}
            if then is not None:
                done[group[0]] = start(then, done[group[0]])
            return done

        W.update({n: _Lazy(handles, finish, f"gather_{tag}_wait") for n in group})
        return carry

    now = ["ab_w_in", "ab_w_out", "mlp_w1_0"]
    gathered = _gather_two_level([shard[n] for n in now] + [small_sh], name="gather_weights")
    W.update({n: whole(n, g) for n, g in zip(now, gathered)})
    W["ab_w_in"] = start(1, start(0, W["ab_w_in"], then=2))
    small_full = gathered[-1]
    r = 0
    for n in _SMALL_SHARDED:
        rows, width = math.prod(w[n].shape[:-1]), w[n].shape[-1]
        W[n] = _chunks_to_cols(small_full[:, r:r + rows, :width])
        r += rows
    for n in _REPLICATED:
        W[n] = w[n]
    for n in ("pool_w", "pool_scale", "sgu_w", "sgu_b"):
        W[n] = W[n][0]

    pending, early = [], {}

    def emit(grads, carry):
        names = list(grads)
        if names[0] in _EARLY_SMALL:
            part = grads[names[0]].reshape(-1, LANES)
            assert part.shape[0] % SUBLANES == 0
            early[names[0]], carry = _gather_start([part], carry, name=f"grads_{names[0]}_start")
            return carry
        parts = []
        for n in names:
            if n.startswith("mlp_w1"):
                parts.append(grads[n])
            elif n in in_lead:
                parts.append(_cols_to_chunks(_ungroup_in_cols(grads[n], in_lead[n])))
            else:
                parts.append(_rows_to_chunks(grads[n]))
        handles, carry = _gather_start(parts, carry, name=f"grads_{names[0]}_start", scatter=True)
        pending.append((names, handles))
        return carry

    loss_part, grad_x, G = _local_step(x, loss_target, W, emit)

    landed = {}
    for names, handles in pending:
        got = _gather_wait(handles, grad_x, name=f"grads_{names[0]}_wait", scatter=True)
        landed.update(zip(names, got))
    small_names = _REPLICATED + _SMALL_SHARDED
    early_parts = [_gather_wait(early[n], grad_x, name=f"grads_{n}_wait")[0] for n in _EARLY_SMALL]
    late_grads = [G[n].reshape(-1) for n in small_names[len(_EARLY_SMALL):]]
    loss_row = jnp.full((LANES,), loss_part, F32)
    late_parts = _exchange([_pack(late_grads + [loss_row])], [True], name="gather_small_grads")[0]
    small_parts = jnp.concatenate(early_parts + [late_parts], axis=1)

    grads, deltas, new_m, new_v = {}, {}, {}, {}
    for n in big:
        layers = [landed[f"{n}_{l}"] for l in range(w[n].shape[0])] if n.startswith("mlp") else [landed[n]]
        grads[n], deltas[n], new_m[n], new_v[n] = _adamw(w[n], layers, m[n], v[n], name=f"adamw_{n}")

    rep_shapes = [w[n].shape for n in _REPLICATED]
    rep_rows = sum(math.prod(s) for s in rep_shapes) // LANES
    small_sum_shapes = [(G[n].size,) for n in small_names] + [(LANES,)]
    zero_tail = [jnp.zeros((math.prod(s),), F32) for s in small_sum_shapes[len(_REPLICATED):]]
    w_pack = _pack([w[n] for n in _REPLICATED] + zero_tail)
    m_pack = _pack([m[n] for n in _REPLICATED] + zero_tail)
    v_pack = _pack([v[n] for n in _REPLICATED] + zero_tail)
    outs = [o[0] for o in _adamw(w_pack[None], [small_parts], m_pack[None], v_pack[None], name="adamw_small")]
    summed = _unpack(outs[0], small_sum_shapes)
    for i, n in enumerate(_REPLICATED):
        grads[n] = summed[i].reshape(w[n].shape)
    for dst, o in zip((deltas, new_m, new_v), outs[1:]):
        for n, val in zip(_REPLICATED, _unpack(o[:rep_rows], rep_shapes)):
            dst[n] = val
    loss = summed[-1][0]

    shard_g = []
    for i, n in enumerate(_SMALL_SHARDED):
        full = summed[len(_REPLICATED) + i].reshape(w[n].shape[:-1] + (-1,))
        width = w[n].shape[-1]
        shard_g.append(lax.dynamic_slice_in_dim(full, me * width, width, axis=full.ndim - 1))
    g_sh = _small_shard_pack(shard_g)
    m_sh = _small_shard_pack([m[n] for n in _SMALL_SHARDED])
    v_sh = _small_shard_pack([v[n] for n in _SMALL_SHARDED])
    outs = [o[0] for o in _adamw(small_sh[None], [g_sh[None]], m_sh[None], v_sh[None], name="adamw_small_sharded")]
    r = 0
    for n in _SMALL_SHARDED:
        rows, width = math.prod(w[n].shape[:-1]), w[n].shape[-1]
        for dst, o in zip((grads, deltas, new_m, new_v), outs):
            dst[n] = o[r:r + rows, :width].reshape(w[n].shape)
        r += rows

    return (loss, grad_x, *[grads[n] for n in _NAMES], *[deltas[n] for n in _NAMES],
            *[new_m[n] for n in _NAMES], *[new_v[n] for n in _NAMES])
```

```python
import functools
import math

import jax
import jax.numpy as jnp
from jax import lax
from jax.experimental import pallas as pl
from jax.experimental.pallas import tpu as pltpu

F32 = jnp.float32
BF16 = jnp.bfloat16
GRAD_WIRE = jnp.bfloat16

NORM_EPS = 1e-6
ADAM_LR = 0.001
ADAM_B1 = 0.9
ADAM_B2 = 0.999
ADAM_EPS = 1e-08
ADAM_WD = 0.01
ADAM_STEP = 10
ADAM_C1 = 1.0 - ADAM_B1 ** ADAM_STEP
ADAM_C2 = 1.0 - ADAM_B2 ** ADAM_STEP

N_DEV = 8
LANES = 128
SUBLANES = 8
SB_DH = 64
ATT_BLOCK = 256
POOL_LOG_WINDOWS = 4
VMEM_LIMIT = 56 * 1024 * 1024
MATMUL_B_BUFFERS = 3


def _params(semantics=None):
    return pltpu.CompilerParams(dimension_semantics=semantics, vmem_limit_bytes=VMEM_LIMIT)


def _tile(dim, pref, unit=LANES):
    if dim <= pref:
        return dim
    t = (pref // unit) * unit
    while t >= unit:
        if dim % t == 0:
            return t
        t -= unit
    return dim


def _matmul(a, b, *, name, ta=False, tb=False, extras=(), rows=(), epilogue=None, out_dtypes=(F32,),
            col_sums=0, tm=1024, tn=1024, tk=2048, b_chunks=False, out_chunks=0):
    M, K = (a.shape[1], a.shape[0]) if ta else a.shape
    if b_chunks:
        C, b_rows, n = b.shape
        N, tn, tk = (b_rows, tn, n) if tb else (C * n, n, tk)
        assert (C * n if tb else b_rows) == K, (a.shape, b.shape)
    else:
        N = b.shape[0] if tb else b.shape[1]
        assert (b.shape[1] if tb else b.shape[0]) == K, (a.shape, b.shape)
    if out_chunks:
        assert not extras and N % out_chunks == 0
        tn = N // out_chunks
    tm, tn, tk = _tile(M, tm), _tile(N, tn), _tile(K, tk)
    nk = K // tk
    dims = (((0 if ta else 1,), (1 if tb else 0,)), ((), ()))
    ne, no = len(extras) + len(rows), len(out_dtypes)
    nfull = no - col_sums
    assert col_sums == 0 or tn == N

    nj = N // tn
    steps = (M // tm) * nj * nk
    b_tile = (tn, tk) if tb else (tk, tn)

    def b_block(b_ref, step):
        j, k = (step // nk) % nj, step % nk
        if b_chunks:
            chunk, r0 = (k, j * tn) if tb else (j, k * tk)
            return b_ref.at[chunk, pl.ds(pl.multiple_of(r0, SUBLANES), b_tile[0]), :]
        r0, c0 = (j * tn, k * tk) if tb else (k * tk, j * tn)
        return b_ref.at[pl.ds(pl.multiple_of(r0, SUBLANES), b_tile[0]), pl.ds(pl.multiple_of(c0, LANES), b_tile[1])]

    def body(*refs):
        a_ref, b_ref = refs[0], refs[1]
        e_refs = refs[2:2 + ne]
        o_refs = refs[2 + ne:2 + ne + no]
        b_buf, b_sem = refs[2 + ne + no], refs[3 + ne + no]
        i = pl.program_id(0)
        k = pl.program_id(2)
        step = (i * nj + pl.program_id(1)) * nk + k

        def fetch(s):
            slot = s % MATMUL_B_BUFFERS
            return pltpu.make_async_copy(b_block(b_ref, s), b_buf.at[slot], b_sem.at[slot])

        @pl.when(step == 0)
        def _():
            for s in range(min(MATMUL_B_BUFFERS - 1, steps)):
                fetch(s).start()

        @pl.when(step + MATMUL_B_BUFFERS - 1 < steps)
        def _():
            fetch(step + MATMUL_B_BUFFERS - 1).start()

        fetch(step).wait()

        def part():
            return lax.dot_general(a_ref[...].astype(BF16), b_buf[step % MATMUL_B_BUFFERS].astype(BF16), dims,
                                   preferred_element_type=F32)

        def finish(acc):
            outs = epilogue(acc, *[e[...] for e in e_refs]) if epilogue is not None else (acc,)
            for o_ref, val in zip(o_refs[:nfull], outs):
                o_ref[...] = val.astype(o_ref.dtype)
            for o_ref, val in zip(o_refs[nfull:], outs[nfull:]):
                @pl.when(i == 0)
                def _():
                    o_ref[...] = jnp.zeros_like(o_ref)

                o_ref[...] += val

        if nk == 1:
            finish(part())
        else:
            acc_ref = refs[-1]

            @pl.when(k == 0)
            def _():
                acc_ref[...] = jnp.zeros_like(acc_ref)

            acc_ref[...] += part()

            @pl.when(k == nk - 1)
            def _():
                finish(acc_ref[...])

    a_spec = (pl.BlockSpec((tk, tm), lambda i, j, k: (k, i)) if ta
              else pl.BlockSpec((tm, tk), lambda i, j, k: (i, k)))
    b_spec = pl.BlockSpec(memory_space=pl.ANY)
    if out_chunks:
        o_spec = pl.BlockSpec((None, tm, tn), lambda i, j, k: (j, i, 0))
        o_shape = (out_chunks, M, tn)
    else:
        o_spec = pl.BlockSpec((tm, tn), lambda i, j, k: (i, j))
        o_shape = (M, N)
    r_spec = pl.BlockSpec((1, tn), lambda i, j, k: (0, j))
    outs = pl.pallas_call(
        body,
        name=name,
        grid=(M // tm, N // tn, nk),
        in_specs=[a_spec, b_spec] + [o_spec] * len(extras) + [r_spec] * len(rows),
        out_specs=[o_spec] * nfull + [r_spec] * col_sums,
        out_shape=[jax.ShapeDtypeStruct(o_shape, dt) for dt in out_dtypes[:nfull]]
        + [jax.ShapeDtypeStruct((1, N), dt) for dt in out_dtypes[nfull:]],
        scratch_shapes=[pltpu.VMEM((MATMUL_B_BUFFERS,) + b_tile, b.dtype), pltpu.SemaphoreType.DMA((MATMUL_B_BUFFERS,))]
        + ([pltpu.VMEM((tm, tn), F32)] if nk > 1 else []),
        compiler_params=_params(("arbitrary", "arbitrary", "arbitrary")),
    )(a, b, *extras, *rows)
    return outs[0] if no == 1 else outs


def _ep_add(acc, res):
    return (acc + res,)


def _ep_add_norm(acc, res, g):
    h = acc + res
    return h, (h * _rstd(h)) * g


def _ep_norm_bwd(acc, h, dres, g):
    r = _rstd(h)
    xh = h * r
    dxh = acc * g
    dh = dres + r * (dxh - xh * jnp.mean(dxh * xh, axis=-1, keepdims=True))
    return dh, dh, jnp.sum(acc * xh, axis=0, keepdims=True)


def _norm_bwd_matmul(dy_in, w, h, g, dres, *, name):
    return _matmul(dy_in, w, tb=True, name=name, extras=(h, dres), rows=(g,), epilogue=_ep_norm_bwd,
                   out_dtypes=(F32, BF16, F32), col_sums=1, tm=512, tn=h.shape[1])


def _ep_relu2(acc):
    r = jnp.maximum(acc, 0.0)
    return (r * r,)


def _ep_relu2_bwd(acc, act):
    return (acc * (2.0 * jnp.sqrt(act.astype(F32))),)


def _rstd(x):
    return lax.rsqrt(jnp.mean(x * x, axis=-1, keepdims=True) + NORM_EPS)


def _rmsnorm(h, g, *, name, tr=512):
    N, D = h.shape
    tr = _tile(N, tr, SUBLANES)

    def body(h_ref, g_ref, o_ref):
        x = h_ref[...]
        o_ref[...] = ((x * _rstd(x)) * g_ref[...]).astype(o_ref.dtype)

    row = pl.BlockSpec((tr, D), lambda i: (i, 0))
    vec = pl.BlockSpec((1, D), lambda i: (0, 0))
    return pl.pallas_call(
        body, name=name, grid=(N // tr,), in_specs=[row, vec], out_specs=row,
        out_shape=jax.ShapeDtypeStruct((N, D), BF16), compiler_params=_params(("parallel",)),
    )(h, g)


def _final_loss(h, g, target, *, name, tr=512):
    N, D = h.shape
    tr = _tile(N, tr, SUBLANES)

    def body(h_ref, g_ref, t_ref, loss_ref, dh_ref, dhb_ref, dg_ref):
        i = pl.program_id(0)
        x = h_ref[...]
        gg = g_ref[...]
        r = _rstd(x)
        xh = x * r
        err = xh * gg - t_ref[...]

        @pl.when(i == 0)
        def _():
            dg_ref[...] = jnp.zeros_like(dg_ref)
            loss_ref[...] = jnp.zeros_like(loss_ref)

        per_row = jnp.mean(err * err, axis=-1, keepdims=True)
        loss_ref[...] += 0.5 * jnp.sum(per_row, axis=0, keepdims=True)
        dy = err * (1.0 / D)
        dg_ref[...] += jnp.sum(dy * xh, axis=0, keepdims=True)
        dxh = dy * gg
        dh = r * (dxh - xh * jnp.mean(dxh * xh, axis=-1, keepdims=True))
        dh_ref[...] = dh
        dhb_ref[...] = dh.astype(dhb_ref.dtype)

    row = pl.BlockSpec((tr, D), lambda i: (i, 0))
    vec = pl.BlockSpec((1, D), lambda i: (0, 0))
    lvec = pl.BlockSpec((1, LANES), lambda i: (0, 0))
    return pl.pallas_call(
        body, name=name, grid=(N // tr,), in_specs=[row, vec, row], out_specs=[lvec, row, row, vec],
        out_shape=[jax.ShapeDtypeStruct((1, LANES), F32), jax.ShapeDtypeStruct((N, D), F32),
                   jax.ShapeDtypeStruct((N, D), BF16), jax.ShapeDtypeStruct((1, D), F32)],
        compiler_params=_params(("arbitrary",)),
    )(h, g, target)


def _shift_down(x, s):
    t = lax.broadcasted_iota(jnp.int32, x.shape, 0)
    return jnp.where(t >= s, pltpu.roll(x, s, 0), 0.0)


def _shift_up(x, s):
    n = x.shape[0]
    t = lax.broadcasted_iota(jnp.int32, x.shape, 0)
    return jnp.where(t < n - s, pltpu.roll(x, n - s, 0), 0.0)


def _window_sum(x, g, shift):
    s = x + shift(x, 1)
    for k in range(1, POOL_LOG_WINDOWS):
        s = jnp.where(k <= g, s + shift(s, 2 ** k), s)
    return s


def _pool_count(shape, g):
    t = lax.broadcasted_iota(jnp.int32, shape, 0)
    return jnp.minimum(t + 1, lax.shift_left(jnp.int32(2), g)).astype(F32)


def _pool_fwd(p, pool_w, pool_scale, B, T, off, width, *, name):
    G, dh = pool_w.shape[0], pool_w.shape[1]
    assert G == POOL_LOG_WINDOWS and off % dh == 0
    base = off // dh

    def body(a_ref, w_ref, s_ref, o_ref):
        g = pl.program_id(0)
        a = a_ref[...]
        pooled = _window_sum(a, g, _shift_down) / _pool_count(a.shape, g) - a
        m = jnp.dot(pooled.astype(BF16), w_ref[0].astype(BF16), preferred_element_type=F32)
        o_ref[...] = (m * s_ref[0]).astype(o_ref.dtype)

    return pl.pallas_call(
        body, name=name, grid=(G, B),
        in_specs=[pl.BlockSpec((T, dh), lambda g, b: (b, base + g)),
                  pl.BlockSpec((1, dh, dh), lambda g, b: (g, 0, 0)),
                  pl.BlockSpec((1, 1, dh), lambda g, b: (g, 0, 0))],
        out_specs=pl.BlockSpec((T, dh), lambda g, b: (b, g)),
        out_shape=jax.ShapeDtypeStruct((B * T, width), BF16),
        compiler_params=_params(("parallel", "parallel")),
    )(p, pool_w, pool_scale.reshape(G, 1, dh))


def _pool_bwd(p, dcat, pool_w, pool_scale, B, T, off, *, name):
    G, dh = pool_w.shape[0], pool_w.shape[1]
    base = off // dh

    def body(a_ref, d_ref, w_ref, s_ref, da_ref, dw_ref, ds_ref):
        g = pl.program_id(0)
        b = pl.program_id(1)
        a = a_ref[...]
        d = d_ref[...]
        cnt = _pool_count(a.shape, g)
        pooled = (_window_sum(a, g, _shift_down) / cnt - a).astype(BF16)
        w = w_ref[0].astype(BF16)
        m = jnp.dot(pooled, w, preferred_element_type=F32)

        @pl.when(b == 0)
        def _():
            dw_ref[...] = jnp.zeros_like(dw_ref)
            ds_ref[...] = jnp.zeros_like(ds_ref)

        ds_ref[0] += jnp.sum(d * m, axis=0, keepdims=True)
        dm = (d * s_ref[0]).astype(BF16)
        dw_ref[0] += lax.dot_general(pooled, dm, (((0,), (0,)), ((), ())), preferred_element_type=F32)
        dpooled = lax.dot_general(dm, w, (((1,), (1,)), ((), ())), preferred_element_type=F32)
        da = _window_sum(dpooled / cnt, g, _shift_up) - dpooled
        da_ref[...] = da.astype(da_ref.dtype)

    pblk = pl.BlockSpec((T, dh), lambda g, b: (b, base + g))
    dblk = pl.BlockSpec((T, dh), lambda g, b: (b, g))
    wspec = pl.BlockSpec((1, dh, dh), lambda g, b: (g, 0, 0))
    sspec = pl.BlockSpec((1, 1, dh), lambda g, b: (g, 0, 0))
    return pl.pallas_call(
        body, name=name, grid=(G, B), in_specs=[pblk, dblk, wspec, sspec], out_specs=[pblk, wspec, sspec],
        out_shape=[jax.ShapeDtypeStruct((B * T, p.shape[1]), BF16), jax.ShapeDtypeStruct((G, dh, dh), F32),
                   jax.ShapeDtypeStruct((G, 1, dh), F32)],
        compiler_params=_params(("parallel", "arbitrary")),
    )(p, dcat, pool_w, pool_scale.reshape(G, 1, dh))


_ANY = pl.BlockSpec(memory_space=pl.ANY)


def _conv_fwd(p, cat, conv_w, conv_b, B, T, coff, *, name):
    CW = conv_w.shape[1]
    tc = LANES
    assert coff % tc == 0 and CW % tc == 0
    cbase = coff // tc

    def body(p_ref, cat_ref, w_ref, b_ref, o_ref):
        xb, gb, gc = p_ref[:, 0:tc], p_ref[:, tc:2 * tc], p_ref[:, 2 * tc:3 * tc]
        c = gc * xb
        w = w_ref[...]
        y = _shift_down(c, 2) * w[0:1] + _shift_down(c, 1) * w[1:2] + c * w[2:3] + b_ref[...]
        o_ref[...] = (gb * y).astype(o_ref.dtype)

    return pl.pallas_call(
        body, name=name, grid=(CW // tc, B),
        in_specs=[pl.BlockSpec((T, 3 * tc), lambda j, b: (b, j)), _ANY,
                  pl.BlockSpec((3, tc), lambda j, b: (0, j)), pl.BlockSpec((1, tc), lambda j, b: (0, j))],
        out_specs=pl.BlockSpec((T, tc), lambda j, b: (b, cbase + j)),
        out_shape=jax.ShapeDtypeStruct(cat.shape, cat.dtype), input_output_aliases={1: 0},
        compiler_params=_params(("parallel", "parallel")),
    )(p, cat, conv_w, conv_b)


def _conv_bwd(p, dcat, dp, conv_w, conv_b, B, T, coff, *, name):
    CW = conv_w.shape[1]
    tc = LANES
    assert coff % tc == 0
    cbase = coff // tc

    def body(p_ref, d_ref, dp_in_ref, w_ref, b_ref, dp_ref, dw_ref, db_ref):
        b = pl.program_id(1)
        xb, gb, gc = p_ref[:, 0:tc], p_ref[:, tc:2 * tc], p_ref[:, 2 * tc:3 * tc]
        d = d_ref[...]
        w = w_ref[...]
        c = gc * xb
        c1 = _shift_down(c, 1)
        c2 = _shift_down(c, 2)
        y = c2 * w[0:1] + c1 * w[1:2] + c * w[2:3] + b_ref[...]
        dy = d * gb
        dp_ref[:, tc:2 * tc] = (d * y).astype(dp_ref.dtype)

        @pl.when(b == 0)
        def _():
            dw_ref[...] = jnp.zeros_like(dw_ref)
            db_ref[...] = jnp.zeros_like(db_ref)

        db_ref[...] += jnp.sum(dy, axis=0, keepdims=True)
        dw_ref[0:1, :] += jnp.sum(dy * c2, axis=0, keepdims=True)
        dw_ref[1:2, :] += jnp.sum(dy * c1, axis=0, keepdims=True)
        dw_ref[2:3, :] += jnp.sum(dy * c, axis=0, keepdims=True)
        dc = dy * w[2:3] + _shift_up(dy, 1) * w[1:2] + _shift_up(dy, 2) * w[0:1]
        dp_ref[:, 2 * tc:3 * tc] = (dc * xb).astype(dp_ref.dtype)
        dp_ref[:, 0:tc] = (dc * gc).astype(dp_ref.dtype)

    wspec = pl.BlockSpec((3, tc), lambda j, b: (0, j))
    bspec = pl.BlockSpec((1, tc), lambda j, b: (0, j))
    pblk = pl.BlockSpec((T, 3 * tc), lambda j, b: (b, j))
    return pl.pallas_call(
        body, name=name, grid=(CW // tc, B),
        in_specs=[pblk, pl.BlockSpec((T, tc), lambda j, b: (b, cbase + j)), _ANY, wspec, bspec],
        out_specs=[pblk, wspec, bspec],
        out_shape=[jax.ShapeDtypeStruct(dp.shape, dp.dtype), jax.ShapeDtypeStruct((3, CW), F32),
                   jax.ShapeDtypeStruct((1, CW), F32)],
        input_output_aliases={2: 0},
        compiler_params=_params(("parallel", "arbitrary")),
    )(p, dcat, dp, conv_w, conv_b)


_SQRT_HALF = 0.7071067811865476
_INV_SQRT_2PI = 0.3989422804014327


def _gelu(x):
    return x * (lax.erf(x * _SQRT_HALF) + 1.0) * 0.5


def _gelu_grad(x):
    return 0.5 * (lax.erf(x * _SQRT_HALF) + 1.0) + x * (_INV_SQRT_2PI * jnp.exp(-0.5 * x * x))


def _layernorm_parts(v):
    mu = jnp.mean(v, axis=-1, keepdims=True)
    vc = v - mu
    rstd = lax.rsqrt(jnp.mean(vc * vc, axis=-1, keepdims=True) + NORM_EPS)
    return vc * rstd, rstd


def _tril_mask(L):
    r = lax.broadcasted_iota(jnp.int32, (L, L), 0)
    c = lax.broadcasted_iota(jnp.int32, (L, L), 1)
    return r >= c


def _sgu_fwd(p, ln_g, ln_b, sgu_w, sgu_b, off, width, *, name, tr=512):
    N = p.shape[0]
    G, L = sgu_w.shape[0], sgu_w.shape[1]
    SW = ln_g.shape[1]
    dh = SW // G
    assert off % SW == 0
    ub = off // SW
    tr = _tile(N, tr, L)
    assert tr % L == 0

    def body(u_ref, v_ref, g_ref, beta_ref, w_ref, b_ref, o_ref):
        u = _gelu(u_ref[...])
        vhat, _ = _layernorm_parts(_gelu(v_ref[...]))
        vn = (vhat * g_ref[...] + beta_ref[...]).astype(BF16)
        mask = _tril_mask(L)
        for gi in range(G):
            w = jnp.where(mask, w_ref[gi], 0.0).astype(BF16)
            bias = b_ref[gi]
            cols = slice(gi * dh, (gi + 1) * dh)
            for n in range(tr // L):
                rows = slice(n * L, (n + 1) * L)
                s = jnp.dot(w, vn[rows, cols], preferred_element_type=F32) + bias
                o_ref[rows, cols] = (u[rows, cols] * s).astype(o_ref.dtype)

    def col(k):
        return pl.BlockSpec((tr, SW), lambda i: (i, k))

    vec = pl.BlockSpec((1, SW), lambda i: (0, 0))
    return pl.pallas_call(
        body, name=name, grid=(N // tr,),
        in_specs=[col(ub), col(ub + 1), vec, vec, pl.BlockSpec((G, L, L), lambda i: (0, 0, 0)),
                  pl.BlockSpec((G, L, 1), lambda i: (0, 0, 0))],
        out_specs=col(0), out_shape=jax.ShapeDtypeStruct((N, width), BF16),
        compiler_params=_params(("parallel",)),
    )(p, p, ln_g, ln_b, sgu_w, sgu_b.reshape(G, L, 1))


def _sgu_bwd(p, dcat, ln_g, ln_b, sgu_w, sgu_b, off, *, name, tr=512):
    N = p.shape[0]
    G, L = sgu_w.shape[0], sgu_w.shape[1]
    SW = ln_g.shape[1]
    dh = SW // G
    assert off % SW == 0
    ub = off // SW
    tr = _tile(N, tr, L)

    def compute(i, u_ref, v_ref, dc_ref, g_ref, beta_ref, w_ref, b_ref,
                du_ref, dv_ref, dw_ref, db_ref, dg_ref, dbeta_ref, du_s, dvn_s):
        pu = u_ref[...]
        pv = v_ref[...]
        u = _gelu(pu)
        vhat, rstd = _layernorm_parts(_gelu(pv))
        gg = g_ref[...]
        vn = (vhat * gg + beta_ref[...]).astype(BF16)
        dc = dc_ref[...]
        mask = _tril_mask(L)

        @pl.when(i == 0)
        def _():
            dw_ref[...] = jnp.zeros_like(dw_ref)
            db_ref[...] = jnp.zeros_like(db_ref)
            dg_ref[...] = jnp.zeros_like(dg_ref)
            dbeta_ref[...] = jnp.zeros_like(dbeta_ref)

        for gi in range(G):
            w = jnp.where(mask, w_ref[gi], 0.0).astype(BF16)
            bias = b_ref[gi]
            cols = slice(gi * dh, (gi + 1) * dh)
            dw_acc = jnp.zeros((L, L), F32)
            db_acc = jnp.zeros((L, 1), F32)
            for n in range(tr // L):
                rows = slice(n * L, (n + 1) * L)
                vb = vn[rows, cols]
                s = jnp.dot(w, vb, preferred_element_type=F32) + bias
                du_s[rows, cols] = dc[rows, cols] * s
                ds = dc[rows, cols] * u[rows, cols]
                db_acc += jnp.sum(ds, axis=1, keepdims=True)
                dsb = ds.astype(BF16)
                dw_acc += lax.dot_general(dsb, vb, (((1,), (1,)), ((), ())), preferred_element_type=F32)
                dvn_s[rows, cols] = lax.dot_general(w, dsb, (((0,), (0,)), ((), ())),
                                                    preferred_element_type=F32)
            dw_ref[gi] += jnp.where(mask, dw_acc, 0.0)
            db_ref[gi] += db_acc

        dvn = dvn_s[...]
        dg_ref[...] += jnp.sum(dvn * vhat, axis=0, keepdims=True)
        dbeta_ref[...] += jnp.sum(dvn, axis=0, keepdims=True)
        dvh = dvn * gg
        dv = rstd * (dvh - jnp.mean(dvh, axis=-1, keepdims=True)
                     - vhat * jnp.mean(dvh * vhat, axis=-1, keepdims=True))
        dv_ref[...] = (dv * _gelu_grad(pv)).astype(dv_ref.dtype)
        du_ref[...] = (du_s[...] * _gelu_grad(pu)).astype(du_ref.dtype)

    def body(u_ref, v_ref, dc_ref, g_ref, beta_ref, w_ref, b_ref,
             dp_ref, dw_ref, db_ref, dg_ref, dbeta_ref, du_s, dvn_s, dv_s):
        i = pl.program_id(0)
        half = pl.program_id(1)

        @pl.when(half == 0)
        def _():
            compute(i, u_ref, v_ref, dc_ref, g_ref, beta_ref, w_ref, b_ref,
                    dp_ref, dv_s, dw_ref, db_ref, dg_ref, dbeta_ref, du_s, dvn_s)

        @pl.when(half == 1)
        def _():
            dp_ref[...] = dv_s[...]

    def col(k):
        return pl.BlockSpec((tr, SW), lambda i, half: (i, k))

    vec = pl.BlockSpec((1, SW), lambda i, half: (0, 0))
    wspec = pl.BlockSpec((G, L, L), lambda i, half: (0, 0, 0))
    bspec = pl.BlockSpec((G, L, 1), lambda i, half: (0, 0, 0))
    return pl.pallas_call(
        body, name=name, grid=(N // tr, 2),
        in_specs=[col(ub), col(ub + 1), col(0), vec, vec, wspec, bspec],
        out_specs=[pl.BlockSpec((tr, SW), lambda i, half: (i, ub + half)), wspec, bspec, vec, vec],
        out_shape=[jax.ShapeDtypeStruct((N, p.shape[1]), BF16),
                   jax.ShapeDtypeStruct((G, L, L), F32), jax.ShapeDtypeStruct((G, L, 1), F32),
                   jax.ShapeDtypeStruct((1, SW), F32), jax.ShapeDtypeStruct((1, SW), F32)],
        scratch_shapes=[pltpu.VMEM((tr, SW), F32), pltpu.VMEM((tr, SW), F32), pltpu.VMEM((tr, SW), BF16)],
        compiler_params=_params(("arbitrary", "arbitrary")),
    )(p, p, dcat, ln_g, ln_b, sgu_w, sgu_b.reshape(G, L, 1))


def _log_sigmoid_pair(z):
    ls = jnp.minimum(z, 0.0) - jnp.log(1.0 + jnp.exp(-jnp.abs(z)))
    return ls, ls - z


def _stacked_cumsum(xs, m):
    his = [x.astype(BF16) for x in xs]
    los = [(x - h.astype(F32)).astype(BF16) for x, h in zip(xs, his)]
    n, rows = len(xs), xs[0].shape[0]
    s = jnp.dot(jnp.concatenate(his + los, axis=0), m, preferred_element_type=F32)
    return [s[i * rows:(i + 1) * rows] + s[(n + i) * rows:(n + i + 1) * rows] for i in range(n)]


def _att_logits(qts, kbs, strict):
    ls, lks = [], []
    for qt, kb in zip(qts, kbs):
        l, lk = _log_sigmoid_pair(lax.dot_general(qt, kb, (((1,), (1,)), ((), ())), preferred_element_type=F32))
        ls.append(l)
        lks.append(lk if strict is None else jnp.where(strict, lk, 0.0))
    return ls, lks


def _att_tiles(TB):
    r = lax.broadcasted_iota(jnp.int32, (TB, TB), 0)
    c = lax.broadcasted_iota(jnp.int32, (TB, TB), 1)
    return r, c


HEADS_PER_BLOCK = LANES // SB_DH
ATT_GROUPS = 2
HP = ATT_GROUPS * HEADS_PER_BLOCK
ATT_SCALE = 1.0 / math.sqrt(SB_DH)


def _head_col(hh, part):
    return (hh // HEADS_PER_BLOCK) * 3 * LANES + part * LANES + (hh % HEADS_PER_BLOCK) * SB_DH


def _stage_heads(src_ref, part, dst_ref, T, scale=None):
    rows = _tile(T, 256, SUBLANES)

    def chunk(n, _):
        r0 = pl.multiple_of(n * rows, rows)
        for hh in range(HP):
            col = hh * SB_DH if part is None else _head_col(hh, part)
            x = src_ref[pl.ds(r0, rows), col:col + SB_DH]
            if scale is not None:
                x = x * scale
            dst_ref[hh, pl.ds(r0, rows), :] = x.astype(dst_ref.dtype)
        return 0

    lax.fori_loop(0, T // rows, chunk, 0)


def _attn_fwd(p, cat, B, T, coff, *, name):
    dh = SB_DH
    owidth = ATT_GROUPS * LANES
    nsteps = (cat.shape[1] - coff) // owidth
    TB = _tile(T, ATT_BLOCK)
    nb = T // TB
    assert nb <= LANES and coff % owidth == 0 and (cat.shape[1] - coff) % owidth == 0
    cbase = coff // owidth

    def body(p_ref, cat_ref, o_ref, c_ref, q_ref, k_ref, v_ref):
        _stage_heads(p_ref, 0, q_ref, T, ATT_SCALE)
        _stage_heads(p_ref, 1, k_ref, T)
        _stage_heads(p_ref, 2, v_ref, T)
        r, c = _att_tiles(TB)
        strict = c < r
        later = (r > c).astype(BF16)
        lane = lax.broadcasted_iota(jnp.int32, (TB, LANES), 1)

        def tiles(qts, j, carries, mask):
            k0 = pl.multiple_of(j * TB, TB)
            ls, lks = _att_logits(qts, [k_ref[hh, pl.ds(k0, TB), :] for hh in range(HP)], mask)
            sums = _stacked_cumsum(lks, later)
            out = []
            for hh in range(HP):
                suffix = sums[hh] + carries[hh]
                a = jnp.exp(ls[hh] + suffix)
                if mask is not None:
                    a = jnp.where(mask, a, 0.0)
                pv = jnp.dot(a.astype(BF16), v_ref[hh, pl.ds(k0, TB), :], preferred_element_type=F32)
                out.append((pv, suffix[:, 0:1] + lks[hh][:, 0:1]))
            return out

        def qblock(i, _):
            q0 = pl.multiple_of(i * TB, TB)
            qts = [q_ref[hh, pl.ds(q0, TB), :] for hh in range(HP)]
            state = []
            for pv, carry in tiles(qts, i, [jnp.zeros((TB, 1), F32)] * HP, strict):
                state += [pv, carry, jnp.zeros((TB, LANES), F32)]

            def kblock(jj, st):
                j = i - jj
                out = []
                for hh, (pv, new_carry) in enumerate(tiles(qts, j, st[1::3], None)):
                    acc, carry, cm = st[3 * hh:3 * hh + 3]
                    out += [acc + pv, new_carry, jnp.where(lane == j, carry, cm)]
                return tuple(out)

            st = lax.fori_loop(1, i + 1, kblock, tuple(state))
            for hh in range(HP):
                o_ref[pl.ds(q0, TB), hh * dh:(hh + 1) * dh] = st[3 * hh].astype(o_ref.dtype)
                c_ref[hh, pl.ds(q0, TB), :] = st[3 * hh + 2]
            return 0

        lax.fori_loop(0, nb, qblock, 0)

    staged = pltpu.VMEM((HP, T, dh), BF16)
    return pl.pallas_call(
        body, name=name, grid=(B, nsteps),
        in_specs=[pl.BlockSpec((T, 3 * owidth), lambda b, s: (b, s)), _ANY],
        out_specs=[pl.BlockSpec((T, owidth), lambda b, s: (b, cbase + s)),
                   pl.BlockSpec((HP, T, LANES), lambda b, s: (b * nsteps + s, 0, 0))],
        out_shape=[jax.ShapeDtypeStruct(cat.shape, cat.dtype),
                   jax.ShapeDtypeStruct((B * nsteps * HP, T, LANES), F32)],
        input_output_aliases={1: 0}, scratch_shapes=[staged, staged, staged],
        compiler_params=_params(("parallel", "parallel")),
    )(p, cat)


def _attn_bwd(p, dcat, carries, dp, B, T, coff, *, name):
    dh = SB_DH
    owidth = ATT_GROUPS * LANES
    nsteps = (dcat.shape[1] - coff) // owidth
    TB = _tile(T, ATT_BLOCK)
    nb = T // TB
    cbase = coff // owidth

    def body(p_ref, d_ref, c_ref, dp_in_ref, dp_ref, q_ref, k_ref, v_ref, do_ref, dk_ref, dv_ref):
        _stage_heads(p_ref, 0, q_ref, T, ATT_SCALE)
        _stage_heads(p_ref, 1, k_ref, T)
        _stage_heads(p_ref, 2, v_ref, T)
        _stage_heads(d_ref, None, do_ref, T)
        r, c = _att_tiles(TB)
        strict = c < r
        later = (r > c).astype(BF16)
        earlier = (r < c).astype(BF16)
        lane = lax.broadcasted_iota(jnp.int32, (TB, LANES), 1)
        dk_ref[...] = jnp.zeros_like(dk_ref)
        dv_ref[...] = jnp.zeros_like(dv_ref)

        def tiles(qts, dots, cms, j, befores, mask):
            k0 = pl.multiple_of(j * TB, TB)
            kbs = [k_ref[hh, pl.ds(k0, TB), :] for hh in range(HP)]
            ls, lks = _att_logits(qts, kbs, mask)
            sums = _stacked_cumsum(lks, later)
            gls = []
            for hh in range(HP):
                carry = jnp.sum(jnp.where(lane == j, cms[hh], 0.0), axis=1, keepdims=True)
                a = jnp.exp(ls[hh] + sums[hh] + carry)
                if mask is not None:
                    a = jnp.where(mask, a, 0.0)
                dv_ref[hh, pl.ds(k0, TB), :] += lax.dot_general(a.astype(BF16), dots[hh], (((0,), (0,)), ((), ())),
                                                                preferred_element_type=F32)
                da = lax.dot_general(dots[hh], v_ref[hh, pl.ds(k0, TB), :], (((1,), (1,)), ((), ())),
                                     preferred_element_type=F32)
                gls.append(a * da)
            pres = _stacked_cumsum(gls, earlier)
            out = []
            for hh in range(HP):
                prefix = pres[hh] + befores[hh]
                dz = gls[hh] - jnp.exp(ls[hh]) * (gls[hh] + prefix)
                if mask is not None:
                    dz = jnp.where(mask, dz, 0.0)
                dzb = dz.astype(BF16)
                dk_ref[hh, pl.ds(k0, TB), :] += lax.dot_general(dzb, qts[hh], (((0,), (0,)), ((), ())),
                                                                preferred_element_type=F32)
                out.append((jnp.dot(dzb, kbs[hh], preferred_element_type=F32),
                            prefix[:, TB - 1:TB] + gls[hh][:, TB - 1:TB]))
            return out

        def qblock(i, _):
            q0 = pl.multiple_of(i * TB, TB)
            qts = [q_ref[hh, pl.ds(q0, TB), :] for hh in range(HP)]
            dots = [do_ref[hh, pl.ds(q0, TB), :] for hh in range(HP)]
            cms = [c_ref[hh, pl.ds(q0, TB), :] for hh in range(HP)]

            def kblock(j, st):
                out = []
                for hh, (part, new_before) in enumerate(tiles(qts, dots, cms, j, st[1::2], None)):
                    out += [st[2 * hh] + part, new_before]
                return tuple(out)

            st = lax.fori_loop(0, i, kblock, (jnp.zeros((TB, dh), F32), jnp.zeros((TB, 1), F32)) * HP)
            for hh, (part, _) in enumerate(tiles(qts, dots, cms, i, st[1::2], strict)):
                dq = (st[2 * hh] + part) * ATT_SCALE
                dp_ref[pl.ds(q0, TB), _head_col(hh, 0):_head_col(hh, 0) + dh] = dq.astype(dp_ref.dtype)
            return 0

        lax.fori_loop(0, nb, qblock, 0)

        def write_back(n, _):
            r0 = pl.multiple_of(n * TB, TB)
            for hh in range(HP):
                dp_ref[pl.ds(r0, TB), _head_col(hh, 1):_head_col(hh, 1) + dh] = (
                    dk_ref[hh, pl.ds(r0, TB), :].astype(dp_ref.dtype))
                dp_ref[pl.ds(r0, TB), _head_col(hh, 2):_head_col(hh, 2) + dh] = (
                    dv_ref[hh, pl.ds(r0, TB), :].astype(dp_ref.dtype))
            return 0

        lax.fori_loop(0, nb, write_back, 0)

    pblk = pl.BlockSpec((T, 3 * owidth), lambda b, s: (b, s))
    staged = pltpu.VMEM((HP, T, dh), BF16)
    accum = pltpu.VMEM((HP, T, dh), F32)
    return pl.pallas_call(
        body, name=name, grid=(B, nsteps),
        in_specs=[pblk, pl.BlockSpec((T, owidth), lambda b, s: (b, cbase + s)),
                  pl.BlockSpec((HP, T, LANES), lambda b, s: (b * nsteps + s, 0, 0)), _ANY],
        out_specs=pblk, out_shape=jax.ShapeDtypeStruct(dp.shape, dp.dtype), input_output_aliases={3: 0},
        scratch_shapes=[staged, staged, staged, staged, accum, accum],
        compiler_params=_params(("parallel", "parallel")),
    )(p, dcat, carries, dp)


def _adamw(w, gparts, m, v, *, name):
    L, R, C = w.shape
    P = gparts[0].shape[0]
    assert len(gparts) == L
    tr = _tile(R, max(SUBLANES, (1 << 21) // (C * P)), SUBLANES)

    def body(*refs):
        w_ref, g_refs, (m_ref, v_ref) = refs[0], refs[1:1 + L], refs[1 + L:3 + L]
        go_ref, d_ref, mo_ref, vo_ref = refs[3 + L:]
        layer = pl.program_id(0)

        def update(g_ref):
            g = g_ref[0].astype(F32)
            for i in range(1, P):
                g = g + g_ref[i].astype(F32)
            m2 = ADAM_B1 * m_ref[...] + (1.0 - ADAM_B1) * g
            v2 = ADAM_B2 * v_ref[...] + (1.0 - ADAM_B2) * (g * g)
            m_hat = m2 / ADAM_C1
            v_hat = v2 / ADAM_C2
            go_ref[...] = g
            d_ref[...] = -ADAM_LR * (m_hat / (jnp.sqrt(v_hat) + ADAM_EPS) + ADAM_WD * w_ref[...])
            mo_ref[...] = m2
            vo_ref[...] = v2

        for l in range(L):
            pl.when(layer == l)(functools.partial(update, g_refs[l]))

    row = pl.BlockSpec((None, tr, C), lambda l, i: (l, i, 0))

    def part(mine):
        return pl.BlockSpec((P, tr, C), lambda l, i: (0, jnp.where(l == mine, i, 0), 0))

    shp = jax.ShapeDtypeStruct((L, R, C), F32)
    return pl.pallas_call(
        body, name=name, grid=(L, R // tr),
        in_specs=[row] + [part(l) for l in range(L)] + [row, row],
        out_specs=[row] * 4, out_shape=[shp] * 4, compiler_params=_params(("arbitrary", "arbitrary")),
    )(w, *gparts, m, v)


def _my_index():
    return 4 * lax.axis_index("x") + 2 * lax.axis_index("y") + lax.axis_index("c")


def _exchange(arrs, gather, *, name):
    n = len(arrs)

    def body(*refs):
        ins, outs = refs[:n], refs[n:2 * n]
        send_sems, recv_sems, local_sems = refs[2 * n:]
        x, y, c = lax.axis_index("x"), lax.axis_index("y"), lax.axis_index("c")
        me = 4 * x + 2 * y + c
        remote, local = [], []
        for a in range(n):
            own = ins[a] if gather[a] else ins[a].at[me]
            cp = pltpu.make_async_copy(own, outs[a].at[me], local_sems.at[a])
            cp.start()
            local.append(cp)
            for k in range(1, N_DEV):
                px = 1 - x if k & 4 else x
                py = 1 - y if k & 2 else y
                pc = 1 - c if k & 1 else c
                src = ins[a] if gather[a] else ins[a].at[4 * px + 2 * py + pc]
                cp = pltpu.make_async_remote_copy(
                    src_ref=src, dst_ref=outs[a].at[me],
                    send_sem=send_sems.at[a, k - 1], recv_sem=recv_sems.at[a, k - 1],
                    device_id=(px, py, pc), device_id_type=pl.DeviceIdType.MESH)
                cp.start()
                remote.append(cp)
        for cp in remote:
            cp.wait()
        for cp in local:
            cp.wait()

    hbm = pl.BlockSpec(memory_space=pltpu.HBM)
    out_shape = [jax.ShapeDtypeStruct(((N_DEV,) + a.shape) if g else a.shape, a.dtype)
                 for a, g in zip(arrs, gather)]
    return pl.pallas_call(
        body, name=name, in_specs=[hbm] * n, out_specs=[hbm] * n, out_shape=out_shape,
        scratch_shapes=[pltpu.SemaphoreType.DMA((n, N_DEV - 1)), pltpu.SemaphoreType.DMA((n, N_DEV - 1)),
                        pltpu.SemaphoreType.DMA((n,))],
    )(*arrs)


_HBM = pl.BlockSpec(memory_space=pltpu.HBM)


def _other_chips(x, y):
    return [(1 - x, y), (x, 1 - y), (1 - x, 1 - y)]


def _gather_two_level(arrs, *, name):
    n = len(arrs)

    def body(*refs):
        ins, outs = refs[:n], refs[n:2 * n]
        send_sems, recv_sems, local_sems = refs[2 * n:]
        x, y, c = lax.axis_index("x"), lax.axis_index("y"), lax.axis_index("c")
        me, sibling = (x, y, c), (x, y, 1 - c)
        chips = _other_chips(x, y)

        def slot(a, px, py, pc):
            return outs[a].at[4 * px + 2 * py + pc]

        def copy(a, k, block, to, src=None):
            return pltpu.make_async_remote_copy(
                src_ref=slot(a, *block) if src is None else src, dst_ref=slot(a, *block),
                send_sem=send_sems.at[a, k], recv_sem=recv_sems.at[a, k],
                device_id=to, device_id_type=pl.DeviceIdType.MESH)

        local, sends = [], []
        for a in range(n):
            cp = pltpu.make_async_copy(ins[a], slot(a, *me), local_sems.at[a])
            cp.start()
            local.append(cp)
            first = [copy(a, 0, me, sibling, src=ins[a])]
            first += [copy(a, 1 + j, me, (*chip, c), src=ins[a]) for j, chip in enumerate(chips)]
            for cp in first:
                cp.start()
            sends += first
        for j, chip in enumerate(chips):
            for a in range(n):
                copy(a, 1 + j, (*chip, c), me).wait_recv()
                cp = copy(a, 4 + j, (*chip, c), sibling)
                cp.start()
                sends.append(cp)
        for a in range(n):
            copy(a, 0, sibling, me).wait_recv()
            for j, chip in enumerate(chips):
                copy(a, 4 + j, (*chip, 1 - c), me).wait_recv()
        for cp in sends:
            cp.wait_send()
        for cp in local:
            cp.wait()

    return pl.pallas_call(
        body, name=name, in_specs=[_HBM] * n, out_specs=[_HBM] * n,
        out_shape=[jax.ShapeDtypeStruct((N_DEV,) + a.shape, a.dtype) for a in arrs],
        scratch_shapes=[pltpu.SemaphoreType.DMA((n, N_DEV - 1)), pltpu.SemaphoreType.DMA((n, N_DEV - 1)),
                        pltpu.SemaphoreType.DMA((n,))],
    )(*arrs)


_SEM = pl.BlockSpec(memory_space=pltpu.SEMAPHORE)
_SPLIT_COPY = pltpu.SideEffectType.DATAFLOW_SIDE_EFFECTING


def _peers(x, y, c):
    return [((1 - x if k & 4 else x), (1 - y if k & 2 else y), (1 - c if k & 1 else c)) for k in range(1, N_DEV)]


_SPLIT_SEMS = 2 * (N_DEV - 1) + 1


def _split_sems(sems, a):
    mine = sems[a * _SPLIT_SEMS:(a + 1) * _SPLIT_SEMS]
    return mine[:N_DEV - 1], mine[N_DEV - 1:2 * (N_DEV - 1)], mine[-1]


def _split_src(ref, scatter, index):
    return ref.at[index] if scatter else ref


def _gather_start(arrs, carry, *, name, scatter=False):
    n = len(arrs)
    ns = n * _SPLIT_SEMS

    def body(*refs):
        ins, lands = refs[:n], refs[n:2 * n]
        sems = refs[2 * n + 1:2 * n + 1 + ns]
        x, y, c = lax.axis_index("x"), lax.axis_index("y"), lax.axis_index("c")
        me = 4 * x + 2 * y + c
        for a in range(n):
            send, recv, local = _split_sems(sems, a)
            pltpu.make_async_copy(_split_src(ins[a], scatter, me), lands[a].at[me], local).start()
            for k, (px, py, pc) in enumerate(_peers(x, y, c)):
                pltpu.make_async_remote_copy(
                    src_ref=_split_src(ins[a], scatter, 4 * px + 2 * py + pc), dst_ref=lands[a].at[me],
                    send_sem=send[k], recv_sem=recv[k],
                    device_id=(px, py, pc), device_id_type=pl.DeviceIdType.MESH).start()

    lands = [lax.empty(a.shape if scatter else (N_DEV,) + a.shape, a.dtype) for a in arrs]
    operands = [pltpu.with_memory_space_constraint(a, pltpu.HBM) for a in list(arrs) + lands + [carry]]
    outs = pl.pallas_call(
        body, name=name, in_specs=[_HBM] * (2 * n + 1), out_specs=[_SEM] * ns + [_HBM] * (2 * n + 1),
        out_shape=[pltpu.SemaphoreType.DMA(())] * ns + [pltpu.HBM(a.shape, a.dtype) for a in operands],
        input_output_aliases={i: ns + i for i in range(2 * n + 1)},
        compiler_params=pltpu.CompilerParams(has_side_effects=_SPLIT_COPY),
    )(*operands)
    return tuple(outs[:-1]), outs[-1]


def _gather_wait(handles, after, *, name, scatter=False):
    n = len(handles) // (_SPLIT_SEMS + 2)
    ns = n * _SPLIT_SEMS
    sems, thru = handles[:ns], handles[ns:]

    def body(*refs):
        ins, lands = refs[:n], refs[n:2 * n]
        sems = refs[2 * n:2 * n + ns]
        x, y, c = lax.axis_index("x"), lax.axis_index("y"), lax.axis_index("c")
        me = 4 * x + 2 * y + c
        for a in range(n):
            send, recv, local = _split_sems(sems, a)
            src = _split_src(ins[a], scatter, me)
            pltpu.make_async_copy(src, lands[a].at[me], local).wait()
            for k, peer in enumerate(_peers(x, y, c)):
                cp = pltpu.make_async_remote_copy(
                    src_ref=src, dst_ref=lands[a].at[me], send_sem=send[k], recv_sem=recv[k],
                    device_id=peer, device_id_type=pl.DeviceIdType.MESH)
                cp.wait_send()
                cp.wait_recv()

    outs = pl.pallas_call(
        body, name=name, in_specs=[_HBM] * (2 * n) + [_SEM] * ns + [_ANY], out_specs=[_HBM] * (2 * n),
        out_shape=[pltpu.HBM(a.shape, a.dtype) for a in thru],
        input_output_aliases={i: i for i in range(2 * n)},
        compiler_params=pltpu.CompilerParams(has_side_effects=_SPLIT_COPY),
    )(*thru, *sems, after)
    return outs[n:]


def _group_in_cols(w, lead):
    X = (w.shape[-1] - lead) // 3
    g = w[..., lead:].reshape(w.shape[:-1] + (3, X // LANES, LANES))
    g = jnp.swapaxes(g, -3, -2).reshape(w.shape[:-1] + (3 * X,))
    return jnp.concatenate([g, w[..., :lead]], axis=-1)


def _ungroup_in_cols(w, lead):
    X = (w.shape[-1] - lead) // 3
    g = w[..., :3 * X].reshape(w.shape[:-1] + (X // LANES, 3, LANES))
    g = jnp.swapaxes(g, -3, -2).reshape(w.shape[:-1] + (3 * X,))
    return jnp.concatenate([w[..., 3 * X:], g], axis=-1)


def _residual_matmul(a, w, res, gain, *, name):
    if gain is None:
        return _matmul(a, w, name=name, extras=(res,), epilogue=_ep_add), None
    return _matmul(a, w, name=name, extras=(res,), rows=(gain,), epilogue=_ep_add_norm, out_dtypes=(F32, BF16),
                   tn=res.shape[1])


def _mlp_fwd(h, hn, W, tag, next_gain=None):
    w1 = _weight(W, f"mlp_w1_{tag}", hn)
    act = _matmul(hn, w1, name=f"mlp_up_{tag}", epilogue=_ep_relu2, out_dtypes=(BF16,), b_chunks=True, tm=2048)
    out, out_n = _residual_matmul(act, _weight(W, f"mlp_w2_{tag}", act), h, next_gain, name=f"mlp_down_{tag}")
    return out, out_n, (hn, act, _chunks_to_cols(w1))


def _mlp_bwd(dout, dout_b, h, g, w2, saved, tag, sent):
    hn, act, w1 = saved
    dw2 = _matmul(act, dout_b, ta=True, name=f"mlp_dw2_{tag}", out_dtypes=(GRAD_WIRE,))
    dout_b = sent(dout_b, {f"mlp_w2_{tag}": dw2})
    dz = _matmul(dout_b, w2, tb=True, name=f"mlp_dact_{tag}", extras=(act,), epilogue=_ep_relu2_bwd,
                 out_dtypes=(BF16,))
    dw1 = _matmul(hn, dz, ta=True, name=f"mlp_dw1_{tag}", out_dtypes=(GRAD_WIRE,), out_chunks=N_DEV, tk=4096)
    dz = sent(dz, {f"mlp_w1_{tag}": dw1})
    dh, dh_b, dg = _norm_bwd_matmul(dz, w1, h, g, dout, name=f"mlp_dhn_{tag}")
    return dh, dh_b, dg, dw1, dw2


class _Lazy:
    def __init__(self, handles, finish, name):
        self.handles, self.finish, self.name, self.done = handles, finish, name, None

    def take(self, after):
        if self.done is None:
            self.done = self.finish(_gather_wait(self.handles, after, name=self.name))
        return self.done


def _weight(W, n, after):
    if isinstance(W[n], _Lazy):
        W.update(W[n].take(after))
    return W[n]


def _local_step(x, target, W, emit=None):
    B, T, D = x.shape
    N = B * T
    G = {}
    row = lambda vec: vec.reshape(1, -1)

    def sent(nxt, grads):
        return nxt if emit is None else emit(grads, nxt)

    PW = W["pool_w"].shape[0] * W["pool_w"].shape[1]
    CW = W["conv_b"].shape[-1]
    SW = W["sgu_norm_g"].shape[-1]
    HW = W["att_width"]

    h0 = x.reshape(N, D)
    xn0 = _rmsnorm(h0, row(W["mix_norm_g"][0]), name="mix_norm_0")
    p0 = _matmul(xn0, W["ab_w_in"], name="ab_in")
    cat0 = _pool_fwd(p0, W["pool_w"], W["pool_scale"], B, T, 3 * CW, PW + CW, name="pool_fwd")
    cat0 = _conv_fwd(p0, cat0, W["conv_w"], row(W["conv_b"]), B, T, PW, name="conv_fwd")
    h1, hn0 = _residual_matmul(cat0, W["ab_w_out"], h0, row(W["mlp_norm_g"][0]), name="ab_out")
    h2, xn1, mlp0 = _mlp_fwd(h1, hn0, W, 0, row(W["mix_norm_g"][1]))
    p1 = _matmul(xn1, _weight(W, "cd_w_in", xn1), name="cd_in")
    ln_g, ln_b = row(W["sgu_norm_g"]), row(W["sgu_norm_b"])
    cat1 = _sgu_fwd(p1, ln_g, ln_b, W["sgu_w"], W["sgu_b"], 3 * HW, SW + HW, name="sgu_fwd")
    cat1, att_carries = _attn_fwd(p1, cat1, B, T, SW, name="attn_fwd")
    h3, hn1 = _residual_matmul(cat1, W["cd_w_out"], h2, row(W["mlp_norm_g"][1]), name="cd_out")
    h4, _, mlp1 = _mlp_fwd(h3, hn1, W, 1)

    loss, dh4, dh4_b, G["final_norm_g"] = _final_loss(h4, row(W["final_norm_g"]), target.reshape(N, D),
                                                      name="final_loss")

    dh3, dh3_b, dmlp_g1, dw1_1, dw2_1 = _mlp_bwd(dh4, dh4_b, h3, row(W["mlp_norm_g"][1]), W["mlp_w2_1"], mlp1, 1,
                                                 sent)
    G["cd_w_out"] = _matmul(cat1, dh3_b, ta=True, name="cd_out_dw", out_dtypes=(GRAD_WIRE,))[None]
    dh3_b = sent(dh3_b, {"cd_w_out": G["cd_w_out"][0]})
    dcat1 = _matmul(dh3_b, W["cd_w_out"], tb=True, name="cd_out_dx")
    dp1, G["sgu_w"], dsgu_b, G["sgu_norm_g"], G["sgu_norm_b"] = _sgu_bwd(
        p1, dcat1, ln_g, ln_b, W["sgu_w"], W["sgu_b"], 3 * HW, name="sgu_bwd")
    G["sgu_b"] = dsgu_b.reshape(W["sgu_b"].shape)
    dp1 = sent(dp1, {"sgu_w": G["sgu_w"]})
    dp1 = _attn_bwd(p1, dcat1, att_carries, dp1, B, T, SW, name="attn_bwd")
    G["cd_w_in"] = _matmul(xn1, dp1, ta=True, name="cd_in_dw", out_dtypes=(GRAD_WIRE,))[None]
    dp1 = sent(dp1, {"cd_w_in": G["cd_w_in"][0]})
    dh2, dh2_b, dmix_g1 = _norm_bwd_matmul(dp1, W["cd_w_in"], h2, row(W["mix_norm_g"][1]), dh3, name="cd_in_dx")

    dh1, dh1_b, dmlp_g0, dw1_0, dw2_0 = _mlp_bwd(dh2, dh2_b, h1, row(W["mlp_norm_g"][0]), W["mlp_w2_0"], mlp0, 0,
                                                 sent)
    G["ab_w_out"] = _matmul(cat0, dh1_b, ta=True, name="ab_out_dw", out_dtypes=(GRAD_WIRE,))[None]
    dh1_b = sent(dh1_b, {"ab_w_out": G["ab_w_out"][0]})
    dcat0 = _matmul(dh1_b, W["ab_w_out"], tb=True, name="ab_out_dx")
    dp0, G["pool_w"], dps = _pool_bwd(p0, dcat0, W["pool_w"], W["pool_scale"], B, T, 3 * CW, name="pool_bwd")
    dp0 = sent(dp0, {"pool_w": G["pool_w"]})
    G["pool_scale"] = dps.reshape(W["pool_scale"].shape)
    dp0, G["conv_w"], dcb = _conv_bwd(p0, dcat0, dp0, W["conv_w"], row(W["conv_b"]), B, T, PW, name="conv_bwd")
    G["conv_b"] = dcb.reshape(-1)
    G["ab_w_in"] = _matmul(xn0, dp0, ta=True, name="ab_in_dw", out_dtypes=(GRAD_WIRE,))[None]
    dp0 = sent(dp0, {"ab_w_in": G["ab_w_in"][0]})
    dx, _, dmix_g0 = _norm_bwd_matmul(dp0, W["ab_w_in"], h0, row(W["mix_norm_g"][0]), dh1, name="ab_in_dx")

    G["mix_norm_g"] = jnp.concatenate([dmix_g0, dmix_g1], axis=0)
    G["mlp_norm_g"] = jnp.concatenate([dmlp_g0, dmlp_g1], axis=0)
    G["mlp_w1"] = jnp.stack([_chunks_to_cols(dw1_0), _chunks_to_cols(dw1_1)])
    G["mlp_w2"] = jnp.stack([dw2_0, dw2_1])
    G["final_norm_g"] = G["final_norm_g"].reshape(-1)
    G["sgu_norm_g"] = G["sgu_norm_g"].reshape(-1)
    G["sgu_norm_b"] = G["sgu_norm_b"].reshape(-1)
    return loss[0, 0], dx.reshape(B, T, D), G


_NAMES = ["mix_norm_g", "mlp_norm_g", "ab_w_in", "pool_w", "pool_scale", "conv_w", "conv_b", "ab_w_out",
          "cd_w_in", "sgu_norm_g", "sgu_norm_b", "sgu_w", "sgu_b", "cd_w_out", "mlp_w1", "mlp_w2",
          "final_norm_g"]
_COL_SHARDED = ["ab_w_in", "cd_w_in", "mlp_w1"]
_ROW_SHARDED = ["ab_w_out", "cd_w_out", "mlp_w2"]
_SMALL_SHARDED = ["conv_w", "sgu_norm_g", "sgu_norm_b"]
_REPLICATED = ["sgu_w", "pool_w", "mix_norm_g", "mlp_norm_g", "pool_scale", "conv_b", "sgu_b", "final_norm_g"]
_EARLY_SMALL = _REPLICATED[:2]


def _pad_rows(a2d, mult=SUBLANES):
    pad = (-a2d.shape[0]) % mult
    return jnp.pad(a2d, ((0, pad), (0, 0))) if pad else a2d


def _pack(arrays):
    return _pad_rows(jnp.concatenate([a.reshape(-1, LANES) for a in arrays], axis=0))


def _unpack(packed, shapes):
    out, r = [], 0
    for s in shapes:
        n = math.prod(s) // LANES
        out.append(packed[r:r + n].reshape(s))
        r += n
    return out


def _small_shard_pack(arrays):
    rows = [jnp.pad(a.reshape(-1, a.shape[-1]), ((0, 0), (0, LANES - a.shape[-1]))) for a in arrays]
    return _pad_rows(jnp.concatenate(rows, axis=0))


def _cols_to_chunks(a):
    n = a.shape[-1] // N_DEV
    return jnp.moveaxis(a.reshape(a.shape[:-1] + (N_DEV, n)), -2, 0)


def _chunks_to_cols(a):
    t = jnp.moveaxis(a, 0, -2)
    return t.reshape(t.shape[:-2] + (t.shape[-2] * t.shape[-1],))


def _rows_to_chunks(a):
    r = a.shape[-2] // N_DEV
    return jnp.moveaxis(a.reshape(a.shape[:-2] + (N_DEV, r, a.shape[-1])), -3, 0)


def _chunks_to_rows(a):
    t = jnp.moveaxis(a, 0, -3)
    return t.reshape(t.shape[:-3] + (t.shape[-3] * t.shape[-2], t.shape[-1]))


def kernel(x, mix_norm_g, mlp_norm_g, ab_w_in, pool_w, pool_scale, conv_w, conv_b, ab_w_out, cd_w_in, sgu_norm_g, sgu_norm_b, sgu_w, sgu_b, cd_w_out, mlp_w1, mlp_w2, final_norm_g, loss_target, m_mix_norm_g, m_mlp_norm_g, m_ab_w_in, m_pool_w, m_pool_scale, m_conv_w, m_conv_b, m_ab_w_out, m_cd_w_in, m_sgu_norm_g, m_sgu_norm_b, m_sgu_w, m_sgu_b, m_cd_w_out, m_mlp_w1, m_mlp_w2, m_final_norm_g, v_mix_norm_g, v_mlp_norm_g, v_ab_w_in, v_pool_w, v_pool_scale, v_conv_w, v_conv_b, v_ab_w_out, v_cd_w_in, v_sgu_norm_g, v_sgu_norm_b, v_sgu_w, v_sgu_b, v_cd_w_out, v_mlp_w1, v_mlp_w2, v_final_norm_g):
    w = dict(zip(_NAMES, (mix_norm_g, mlp_norm_g, ab_w_in, pool_w, pool_scale, conv_w, conv_b, ab_w_out, cd_w_in,
                          sgu_norm_g, sgu_norm_b, sgu_w, sgu_b, cd_w_out, mlp_w1, mlp_w2, final_norm_g)))
    m = dict(zip(_NAMES, (m_mix_norm_g, m_mlp_norm_g, m_ab_w_in, m_pool_w, m_pool_scale, m_conv_w, m_conv_b,
                          m_ab_w_out, m_cd_w_in, m_sgu_norm_g, m_sgu_norm_b, m_sgu_w, m_sgu_b, m_cd_w_out,
                          m_mlp_w1, m_mlp_w2, m_final_norm_g)))
    v = dict(zip(_NAMES, (v_mix_norm_g, v_mlp_norm_g, v_ab_w_in, v_pool_w, v_pool_scale, v_conv_w, v_conv_b,
                          v_ab_w_out, v_cd_w_in, v_sgu_norm_g, v_sgu_norm_b, v_sgu_w, v_sgu_b, v_cd_w_out,
                          v_mlp_w1, v_mlp_w2, v_final_norm_g)))
    big = _COL_SHARDED + _ROW_SHARDED
    me = _my_index()

    small_sh = _small_shard_pack([w[n] for n in _SMALL_SHARDED])
    in_lead = {"ab_w_in": pool_w.shape[1] * pool_w.shape[2], "cd_w_in": 2 * sgu_norm_g.shape[-1] * N_DEV}
    shard = {"ab_w_in": ab_w_in[0], "ab_w_out": ab_w_out[0], "cd_w_in": cd_w_in[0], "cd_w_out": cd_w_out[0]}
    for layer in range(mlp_w1.shape[0]):
        shard[f"mlp_w1_{layer}"], shard[f"mlp_w2_{layer}"] = mlp_w1[layer], mlp_w2[layer]
    shard = {n: a.astype(BF16) for n, a in shard.items()}

    def whole(n, g):
        if n.startswith("mlp_w1"):
            return g
        if n in in_lead:
            return _group_in_cols(_chunks_to_cols(g), in_lead[n])
        return _chunks_to_rows(g)

    W = {"att_width": cd_w_out.shape[1] * N_DEV - sgu_norm_g.shape[-1] * N_DEV}
    later = [(["mlp_w2_0"], "mlp0_down"), (["cd_w_in", "cd_w_out"], "cd"), (["mlp_w1_1", "mlp_w2_1"], "mlp1")]

    def start(idx, carry, then=None):
        group, tag = later[idx]
        handles, carry = _gather_start([shard[n] for n in group], carry, name=f"gather_{tag}_start")

        def finish(got):
            done = {n: whole(n, g) for n, g in zip(group, got)}
            if then is not None:
                done[group[0]] = start(then, done[group[0]])
            return done

        W.update({n: _Lazy(handles, finish, f"gather_{tag}_wait") for n in group})
        return carry

    now = ["ab_w_in", "ab_w_out", "mlp_w1_0"]
    gathered = _gather_two_level([shard[n] for n in now] + [small_sh], name="gather_weights")
    W.update({n: whole(n, g) for n, g in zip(now, gathered)})
    W["ab_w_in"] = start(1, start(0, W["ab_w_in"], then=2))
    small_full = gathered[-1]
    r = 0
    for n in _SMALL_SHARDED:
        rows, width = math.prod(w[n].shape[:-1]), w[n].shape[-1]
        W[n] = _chunks_to_cols(small_full[:, r:r + rows, :width])
        r += rows
    for n in _REPLICATED:
        W[n] = w[n]
    for n in ("pool_w", "pool_scale", "sgu_w", "sgu_b"):
        W[n] = W[n][0]

    pending, early = [], {}

    def emit(grads, carry):
        names = list(grads)
        if names[0] in _EARLY_SMALL:
            part = grads[names[0]].reshape(-1, LANES)
            assert part.shape[0] % SUBLANES == 0
            early[names[0]], carry = _gather_start([part], carry, name=f"grads_{names[0]}_start")
            return carry
        parts = []
        for n in names:
            if n.startswith("mlp_w1"):
                parts.append(grads[n])
            elif n in in_lead:
                parts.append(_cols_to_chunks(_ungroup_in_cols(grads[n], in_lead[n])))
            else:
                parts.append(_rows_to_chunks(grads[n]))
        handles, carry = _gather_start(parts, carry, name=f"grads_{names[0]}_start", scatter=True)
        pending.append((names, handles))
        return carry

    loss_part, grad_x, G = _local_step(x, loss_target, W, emit)

    landed = {}
    for names, handles in pending:
        got = _gather_wait(handles, grad_x, name=f"grads_{names[0]}_wait", scatter=True)
        landed.update(zip(names, got))
    small_names = _REPLICATED + _SMALL_SHARDED
    early_parts = [_gather_wait(early[n], grad_x, name=f"grads_{n}_wait")[0] for n in _EARLY_SMALL]
    late_grads = [G[n].reshape(-1) for n in small_names[len(_EARLY_SMALL):]]
    loss_row = jnp.full((LANES,), loss_part, F32)
    late_parts = _exchange([_pack(late_grads + [loss_row])], [True], name="gather_small_grads")[0]
    small_parts = jnp.concatenate(early_parts + [late_parts], axis=1)

    grads, deltas, new_m, new_v = {}, {}, {}, {}
    for n in big:
        layers = [landed[f"{n}_{l}"] for l in range(w[n].shape[0])] if n.startswith("mlp") else [landed[n]]
        grads[n], deltas[n], new_m[n], new_v[n] = _adamw(w[n], layers, m[n], v[n], name=f"adamw_{n}")

    rep_shapes = [w[n].shape for n in _REPLICATED]
    rep_rows = sum(math.prod(s) for s in rep_shapes) // LANES
    small_sum_shapes = [(G[n].size,) for n in small_names] + [(LANES,)]
    zero_tail = [jnp.zeros((math.prod(s),), F32) for s in small_sum_shapes[len(_REPLICATED):]]
    w_pack = _pack([w[n] for n in _REPLICATED] + zero_tail)
    m_pack = _pack([m[n] for n in _REPLICATED] + zero_tail)
    v_pack = _pack([v[n] for n in _REPLICATED] + zero_tail)
    outs = [o[0] for o in _adamw(w_pack[None], [small_parts], m_pack[None], v_pack[None], name="adamw_small")]
    summed = _unpack(outs[0], small_sum_shapes)
    for i, n in enumerate(_REPLICATED):
        grads[n] = summed[i].reshape(w[n].shape)
    for dst, o in zip((deltas, new_m, new_v), outs[1:]):
        for n, val in zip(_REPLICATED, _unpack(o[:rep_rows], rep_shapes)):
            dst[n] = val
    loss = summed[-1][0]

    shard_g = []
    for i, n in enumerate(_SMALL_SHARDED):
        full = summed[len(_REPLICATED) + i].reshape(w[n].shape[:-1] + (-1,))
        width = w[n].shape[-1]
        shard_g.append(lax.dynamic_slice_in_dim(full, me * width, width, axis=full.ndim - 1))
    g_sh = _small_shard_pack(shard_g)
    m_sh = _small_shard_pack([m[n] for n in _SMALL_SHARDED])
    v_sh = _small_shard_pack([v[n] for n in _SMALL_SHARDED])
    outs = [o[0] for o in _adamw(small_sh[None], [g_sh[None]], m_sh[None], v_sh[None], name="adamw_small_sharded")]
    r = 0
    for n in _SMALL_SHARDED:
        rows, width = math.prod(w[n].shape[:-1]), w[n].shape[-1]
        for dst, o in zip((grads, deltas, new_m, new_v), outs):
            dst[n] = o[r:r + rows, :width].reshape(w[n].shape)
        r += rows

    return (loss, grad_x, *[grads[n] for n in _NAMES], *[deltas[n] for n in _NAMES],
            *[new_m[n] for n in _NAMES], *[new_v[n] for n in _NAMES])
```
